```python
import jax, jax.numpy as jnp
from jax import lax
import numpy as np

D_MODEL = 2048
BATCH = 8
SEQ = 2048
DEPTH = 2

HEAD_DIM = 128
A_GROUPS = 4
A_WIDTH = A_GROUPS * HEAD_DIM
A_CHUNK = 128
B_HEADS = 6
B_WIDTH = B_HEADS * HEAD_DIM
B_CONV = 4
B_CHUNK = 64
C_HEADS = 6
C_WIDTH = C_HEADS * HEAD_DIM
C_BRANCHES = ((128, 1), (512, 4), (2048, 16))
C_BLOCK = 128
MIX_WIDTH = A_WIDTH + B_WIDTH + C_WIDTH
FFN_HIDDEN = -(-8 * D_MODEL // (3 * 256)) * 256
IN_SIZES = (A_WIDTH, A_WIDTH,
            B_WIDTH, B_WIDTH, B_WIDTH, B_WIDTH, B_HEADS, B_HEADS,
            C_WIDTH, C_WIDTH, C_WIDTH)
IN_TOTAL = sum(IN_SIZES)
EPS = 1e-6

kernel_name = "hybrid_sgu_deltanet_dilated_attn"


def rms_norm(x, g):
    xf = x.astype(jnp.float32)
    y = xf * lax.rsqrt(jnp.mean(xf * xf, axis=-1, keepdims=True) + EPS)
    return (y * g.astype(jnp.float32)).astype(x.dtype)


def l2_norm(x):
    return x * lax.rsqrt(jnp.sum(x * x, axis=-1, keepdims=True) + EPS)


def chunked_spatial_gating(u, v, sgu_g, w_s, b_s):
    bsz, s, _ = u.shape
    nc = s // A_CHUNK
    u = jax.nn.gelu(u)
    v = rms_norm(jax.nn.gelu(v).reshape(bsz, s, A_GROUPS, HEAD_DIM), sgu_g)
    v = v.reshape(bsz, nc, A_CHUNK, A_GROUPS, HEAD_DIM)
    causal = jnp.tril(jnp.ones((A_CHUNK, A_CHUNK), dtype=bool))
    w = jnp.where(causal[None], w_s, jnp.zeros_like(w_s)).astype(v.dtype)
    z = jnp.einsum('gij,bcjgd->bcigd', w, v) + b_s.T.astype(v.dtype)[None, None, :, :, None]
    return u * z.reshape(bsz, s, A_WIDTH)


def causal_depthwise_conv(x, w):
    k = w.shape[0]
    s = x.shape[1]
    xp = jnp.pad(x, ((0, 0), (k - 1, 0), (0, 0)))
    return sum(xp[:, j:j + s] * w[j].astype(x.dtype) for j in range(k))


def gated_delta_net(q, k, v, gate, beta_logit, a, conv_w, a_log, dt_bias, o_norm_g):
    bsz, s, _ = q.shape
    f32 = jnp.float32
    qkv = jax.nn.silu(causal_depthwise_conv(jnp.concatenate([q, k, v], axis=-1), conv_w))
    q, k, v = jnp.split(qkv, 3, axis=-1)
    to_heads = lambda t: t.reshape(bsz, s, B_HEADS, HEAD_DIM).transpose(0, 2, 1, 3).astype(f32)
    q = l2_norm(to_heads(q)) * (HEAD_DIM ** -0.5)
    k = l2_norm(to_heads(k))
    v = to_heads(v)
    beta = jax.nn.sigmoid(beta_logit.astype(f32)).transpose(0, 2, 1)
    g = (-jnp.exp(a_log.astype(f32))[None, None, :]
         * jax.nn.softplus(a.astype(f32) + dt_bias.astype(f32)[None, None, :])).transpose(0, 2, 1)
    nc = s // B_CHUNK
    chunk = lambda t: t.reshape(t.shape[:2] + (nc, B_CHUNK) + t.shape[3:])
    q, k, v, beta, g = map(chunk, (q, k, v, beta, g))
    g = jnp.cumsum(g, axis=-1)
    causal = jnp.tril(jnp.ones((B_CHUNK, B_CHUNK), dtype=bool))
    strict = jnp.tril(jnp.ones((B_CHUNK, B_CHUNK), dtype=bool), -1)
    gdiff = g[..., :, None] - g[..., None, :]
    decay_mat = jnp.exp(jnp.where(causal, gdiff, -jnp.inf))
    k_beta = k * beta[..., None]
    a_mat = jnp.einsum('bhnid,bhnjd->bhnij', k_beta, k) * jnp.where(strict, decay_mat, 0.0)
    m = a_mat + jnp.eye(B_CHUNK, dtype=f32)
    rhs = jnp.concatenate([v * beta[..., None], k_beta * jnp.exp(g)[..., None]], axis=-1)
    sol = lax.linalg.triangular_solve(m, rhs, left_side=True, lower=True, unit_diagonal=True)
    u_vals, w_keys = jnp.split(sol, 2, axis=-1)
    attn_intra = jnp.einsum('bhnid,bhnjd->bhnij', q, k) * decay_mat

    def step(state, xs):
        q_c, k_c, u_c, w_c, g_c, attn_c = xs
        v_new = u_c - jnp.einsum('bhik,bhkv->bhiv', w_c, state)
        o_c = (jnp.einsum('bhik,bhkv->bhiv', q_c * jnp.exp(g_c)[..., None], state)
               + jnp.einsum('bhij,bhjv->bhiv', attn_c, v_new))
        g_last = g_c[..., -1:]
        state = (state * jnp.exp(g_last)[..., None]
                 + jnp.einsum('bhik,bhiv->bhkv', k_c * jnp.exp(g_last - g_c)[..., None], v_new))
        return state, o_c

    xs = tuple(jnp.moveaxis(t, 2, 0) for t in (q, k, u_vals, w_keys, g, attn_intra))
    state0 = jnp.zeros((bsz, B_HEADS, HEAD_DIM, HEAD_DIM), f32)
    _, o = lax.scan(step, state0, xs)
    o = jnp.moveaxis(o, 0, 2).reshape(bsz, B_HEADS, s, HEAD_DIM).transpose(0, 2, 1, 3)
    gate = gate.reshape(bsz, s, B_HEADS, HEAD_DIM)
    y = rms_norm(o, o_norm_g) * jax.nn.silu(gate.astype(f32))
    return y.astype(gate.dtype).reshape(bsz, s, B_WIDTH)


def dilated_branch(q, k, v, slopes, window, dilation):
    bsz, s, h, hd = q.shape
    span = window // dilation
    seg = s // dilation
    nb = -(-seg // C_BLOCK)
    lp = nb * C_BLOCK

    def to_blocks(t):
        t = t.reshape(bsz, seg, dilation, h, hd)
        t = jnp.pad(t, ((0, 0), (0, lp - seg), (0, 0), (0, 0), (0, 0)))
        return t.reshape(bsz, nb, C_BLOCK, dilation, h, hd)

    qb, kb, vb = map(to_blocks, (q, k, v))
    prev = lambda t: jnp.pad(t, ((0, 0), (1, 0), (0, 0), (0, 0), (0, 0), (0, 0)))[:, :-1]
    kb = jnp.concatenate([prev(kb), kb], axis=2)
    vb = jnp.concatenate([prev(vb), vb], axis=2)
    scores = jnp.einsum('bnqrhd,bnkrhd->bnrhqk', qb, kb).astype(jnp.float32) * (hd ** -0.5)
    qi = jnp.arange(C_BLOCK)[:, None]
    kj = jnp.arange(2 * C_BLOCK)[None, :]
    delta = C_BLOCK + qi - kj
    in_band = (delta >= 0) & (delta <= span)
    key_exists = (jnp.arange(nb)[:, None, None] > 0) | (kj >= C_BLOCK)[None]
    mask = in_band[None] & key_exists
    bias = -slopes[:, None, None] * (delta * dilation).astype(jnp.float32)[None]
    sc = jnp.where(mask[None, :, None, None], scores + bias[None, None, None], -jnp.inf)
    mx = jnp.max(sc, axis=-1, keepdims=True)
    p = jnp.exp(sc - mx)
    den = jnp.sum(p, axis=-1, keepdims=True)
    out = jnp.einsum('bnrhqk,bnkrhd->bnqrhd', (p / den).astype(v.dtype), vb)
    lse = (mx + jnp.log(den))[..., 0]
    out = out.reshape(bsz, lp, dilation, h, hd)[:, :seg].reshape(bsz, s, h, hd)
    lse = lse.transpose(0, 1, 4, 2, 3).reshape(bsz, lp, dilation, h)[:, :seg].reshape(bsz, s, h)
    return out, lse


def dilated_attention(q, k, v, q_g, k_g, slopes):
    bsz, s, _ = q.shape
    to_heads = lambda t: t.reshape(bsz, s, C_HEADS, HEAD_DIM)
    q = rms_norm(to_heads(q), q_g)
    k = rms_norm(to_heads(k), k_g)
    v = to_heads(v)
    outs, lses = zip(*(dilated_branch(q, k, v, slopes, w, r) for (w, r) in C_BRANCHES))
    wts = jax.nn.softmax(jnp.stack(lses, axis=0), axis=0)
    out = jnp.sum(wts[..., None] * jnp.stack(outs, axis=0).astype(jnp.float32), axis=0)
    return out.astype(v.dtype).reshape(bsz, s, C_WIDTH)


def _fwd_setup_inputs(seed: int = 0) -> dict:
    key = jax.random.key(seed)
    ks = jax.random.split(key, 16)
    f32 = jnp.float32
    nrm = lambda k, shape, scale: jax.random.normal(k, shape, f32) * scale
    gain = lambda k, shape: 1.0 + 0.02 * jax.random.normal(k, shape, f32)
    dt = jnp.exp(jax.random.uniform(ks[7], (DEPTH, B_HEADS), f32, np.log(1e-3), np.log(1e-1)))
    return {
        "x": jax.random.normal(ks[0], (BATCH, SEQ, D_MODEL), f32),
        "norm1_g": gain(ks[1], (DEPTH, D_MODEL)),
        "w_in": nrm(ks[2], (DEPTH, D_MODEL, IN_TOTAL), D_MODEL ** -0.5),
        "sgu_norm_g": gain(ks[3], (DEPTH, A_GROUPS, HEAD_DIM)),
        "w_spatial": nrm(ks[4], (DEPTH, A_GROUPS, A_CHUNK, A_CHUNK), A_CHUNK ** -0.5),
        "b_spatial": gain(ks[5], (DEPTH, A_GROUPS, A_CHUNK)),
        "conv_w": nrm(ks[6], (DEPTH, B_CONV, 3 * B_WIDTH), B_CONV ** -0.5),
        "a_log": jnp.log(jax.random.uniform(ks[8], (DEPTH, B_HEADS), f32, 1.0, 16.0)),
        "dt_bias": dt + jnp.log(-jnp.expm1(-dt)),
        "o_norm_g": gain(ks[9], (DEPTH, HEAD_DIM)),
        "q_norm_g": gain(ks[10], (DEPTH, HEAD_DIM)),
        "k_norm_g": gain(ks[11], (DEPTH, HEAD_DIM)),
        "w_out": nrm(ks[12], (DEPTH, MIX_WIDTH, D_MODEL), MIX_WIDTH ** -0.5),
        "norm2_g": gain(ks[13], (DEPTH, D_MODEL)),
        "w_gate_up": nrm(ks[14], (DEPTH, D_MODEL, 2 * FFN_HIDDEN), D_MODEL ** -0.5),
        "w_down": nrm(ks[15], (DEPTH, FFN_HIDDEN, D_MODEL), FFN_HIDDEN ** -0.5),
    }


def _fwd_reference(x, norm1_g, w_in, sgu_norm_g, w_spatial, b_spatial, conv_w, a_log, dt_bias,
              o_norm_g, q_norm_g, k_norm_g, w_out, norm2_g, w_gate_up, w_down):
    slopes = jnp.exp2(-8.0 * (jnp.arange(C_HEADS, dtype=jnp.float32) + 1.0) / C_HEADS)
    offsets = np.cumsum(IN_SIZES)[:-1].tolist()
    for layer in range(DEPTH):
        h = rms_norm(x, norm1_g[layer])
        proj = jnp.einsum('bsd,de->bse', h, w_in[layer])
        (a_u, a_v, b_q, b_k, b_v, b_gate, b_beta, b_a, c_q, c_k, c_v) = jnp.split(proj, offsets, axis=-1)
        y_a = chunked_spatial_gating(a_u, a_v, sgu_norm_g[layer], w_spatial[layer], b_spatial[layer])
        y_b = gated_delta_net(b_q, b_k, b_v, b_gate, b_beta, b_a, conv_w[layer], a_log[layer],
                              dt_bias[layer], o_norm_g[layer])
        y_c = dilated_attention(c_q, c_k, c_v, q_norm_g[layer], k_norm_g[layer], slopes)
        mix = jnp.concatenate([y_a, y_b, y_c], axis=-1)
        x = x + jnp.einsum('bse,ed->bsd', mix, w_out[layer])
        h = rms_norm(x, norm2_g[layer])
        gt, up = jnp.split(jnp.einsum('bsd,df->bsf', h, w_gate_up[layer]), 2, axis=-1)
        x = x + jnp.einsum('bsf,fd->bsd', jax.nn.silu(gt) * up, w_down[layer])
    return x


import jax as _jax
import jax.numpy as _jnp

TWIN_FORMAT = 'train_step'
FWD_PARAMS = ['x', 'norm1_g', 'w_in', 'sgu_norm_g', 'w_spatial', 'b_spatial', 'conv_w', 'a_log', 'dt_bias', 'o_norm_g', 'q_norm_g', 'k_norm_g', 'w_out', 'norm2_g', 'w_gate_up', 'w_down']
TWIN_WEIGHTS = ['norm1_g', 'w_in', 'sgu_norm_g', 'w_spatial', 'b_spatial', 'conv_w', 'a_log', 'dt_bias', 'o_norm_g', 'q_norm_g', 'k_norm_g', 'w_out', 'norm2_g', 'w_gate_up', 'w_down']
TWIN_DIFF_INPUT = 'x'
TWIN_INPUTS = ['x', 'norm1_g', 'w_in', 'sgu_norm_g', 'w_spatial', 'b_spatial', 'conv_w', 'a_log', 'dt_bias', 'o_norm_g', 'q_norm_g', 'k_norm_g', 'w_out', 'norm2_g', 'w_gate_up', 'w_down', 'loss_target', 'm_norm1_g', 'm_w_in', 'm_sgu_norm_g', 'm_w_spatial', 'm_b_spatial', 'm_conv_w', 'm_a_log', 'm_dt_bias', 'm_o_norm_g', 'm_q_norm_g', 'm_k_norm_g', 'm_w_out', 'm_norm2_g', 'm_w_gate_up', 'm_w_down', 'v_norm1_g', 'v_w_in', 'v_sgu_norm_g', 'v_w_spatial', 'v_b_spatial', 'v_conv_w', 'v_a_log', 'v_dt_bias', 'v_o_norm_g', 'v_q_norm_g', 'v_k_norm_g', 'v_w_out', 'v_norm2_g', 'v_w_gate_up', 'v_w_down']
TWIN_OUTPUTS = ['loss', 'grad_x', 'grad_norm1_g', 'grad_w_in', 'grad_sgu_norm_g', 'grad_w_spatial', 'grad_b_spatial', 'grad_conv_w', 'grad_a_log', 'grad_dt_bias', 'grad_o_norm_g', 'grad_q_norm_g', 'grad_k_norm_g', 'grad_w_out', 'grad_norm2_g', 'grad_w_gate_up', 'grad_w_down', 'delta_norm1_g', 'delta_w_in', 'delta_sgu_norm_g', 'delta_w_spatial', 'delta_b_spatial', 'delta_conv_w', 'delta_a_log', 'delta_dt_bias', 'delta_o_norm_g', 'delta_q_norm_g', 'delta_k_norm_g', 'delta_w_out', 'delta_norm2_g', 'delta_w_gate_up', 'delta_w_down', 'new_m_norm1_g', 'new_m_w_in', 'new_m_sgu_norm_g', 'new_m_w_spatial', 'new_m_b_spatial', 'new_m_conv_w', 'new_m_a_log', 'new_m_dt_bias', 'new_m_o_norm_g', 'new_m_q_norm_g', 'new_m_k_norm_g', 'new_m_w_out', 'new_m_norm2_g', 'new_m_w_gate_up', 'new_m_w_down', 'new_v_norm1_g', 'new_v_w_in', 'new_v_sgu_norm_g', 'new_v_w_spatial', 'new_v_b_spatial', 'new_v_conv_w', 'new_v_a_log', 'new_v_dt_bias', 'new_v_o_norm_g', 'new_v_q_norm_g', 'new_v_k_norm_g', 'new_v_w_out', 'new_v_norm2_g', 'new_v_w_gate_up', 'new_v_w_down']
TWIN_LEAF_KINDS = {'loss': 'loss', 'grad_x': 'grad_x', 'grad_norm1_g': 'grad_w', 'grad_w_in': 'grad_w', 'grad_sgu_norm_g': 'grad_w', 'grad_w_spatial': 'grad_w', 'grad_b_spatial': 'grad_w', 'grad_conv_w': 'grad_w', 'grad_a_log': 'grad_w', 'grad_dt_bias': 'grad_w', 'grad_o_norm_g': 'grad_w', 'grad_q_norm_g': 'grad_w', 'grad_k_norm_g': 'grad_w', 'grad_w_out': 'grad_w', 'grad_norm2_g': 'grad_w', 'grad_w_gate_up': 'grad_w', 'grad_w_down': 'grad_w', 'delta_norm1_g': 'delta_w', 'delta_w_in': 'delta_w', 'delta_sgu_norm_g': 'delta_w', 'delta_w_spatial': 'delta_w', 'delta_b_spatial': 'delta_w', 'delta_conv_w': 'delta_w', 'delta_a_log': 'delta_w', 'delta_dt_bias': 'delta_w', 'delta_o_norm_g': 'delta_w', 'delta_q_norm_g': 'delta_w', 'delta_k_norm_g': 'delta_w', 'delta_w_out': 'delta_w', 'delta_norm2_g': 'delta_w', 'delta_w_gate_up': 'delta_w', 'delta_w_down': 'delta_w', 'new_m_norm1_g': 'new_m', 'new_m_w_in': 'new_m', 'new_m_sgu_norm_g': 'new_m', 'new_m_w_spatial': 'new_m', 'new_m_b_spatial': 'new_m', 'new_m_conv_w': 'new_m', 'new_m_a_log': 'new_m', 'new_m_dt_bias': 'new_m', 'new_m_o_norm_g': 'new_m', 'new_m_q_norm_g': 'new_m', 'new_m_k_norm_g': 'new_m', 'new_m_w_out': 'new_m', 'new_m_norm2_g': 'new_m', 'new_m_w_gate_up': 'new_m', 'new_m_w_down': 'new_m', 'new_v_norm1_g': 'new_v', 'new_v_w_in': 'new_v', 'new_v_sgu_norm_g': 'new_v', 'new_v_w_spatial': 'new_v', 'new_v_b_spatial': 'new_v', 'new_v_conv_w': 'new_v', 'new_v_a_log': 'new_v', 'new_v_dt_bias': 'new_v', 'new_v_o_norm_g': 'new_v', 'new_v_q_norm_g': 'new_v', 'new_v_k_norm_g': 'new_v', 'new_v_w_out': 'new_v', 'new_v_norm2_g': 'new_v', 'new_v_w_gate_up': 'new_v', 'new_v_w_down': 'new_v'}


def _forward(args):
    return _fwd_reference(*[args[k] for k in FWD_PARAMS])


def _output_shape():
    out = _jax.eval_shape(lambda: _forward(_fwd_setup_inputs(0)))
    return out.shape, out.dtype

N_MICROBATCH = 1
ADAM_LR = 0.001
ADAM_B1 = 0.9
ADAM_B2 = 0.999
ADAM_EPS = 1e-08
ADAM_WD = 0.01
ADAM_STEP = 10
PER_EXAMPLE_BATCH_AXIS = {'x': 0, 'loss_target': 0}
SHARED_INPUTS = []
_WEIGHT_DTYPES = {'norm1_g': _jnp.float32, 'w_in': _jnp.float32, 'sgu_norm_g': _jnp.float32, 'w_spatial': _jnp.float32, 'b_spatial': _jnp.float32, 'conv_w': _jnp.float32, 'a_log': _jnp.float32, 'dt_bias': _jnp.float32, 'o_norm_g': _jnp.float32, 'q_norm_g': _jnp.float32, 'k_norm_g': _jnp.float32, 'w_out': _jnp.float32, 'norm2_g': _jnp.float32, 'w_gate_up': _jnp.float32, 'w_down': _jnp.float32}
MOMENT_SCALE = {'norm1_g': 3.142128e+00, 'w_in': 1.856663e-01, 'sgu_norm_g': 1.792330e+00, 'w_spatial': 1.135059e+00, 'b_spatial': 3.668308e+00, 'conv_w': 2.468929e-01, 'a_log': 6.367887e+00, 'dt_bias': 5.981358e+00, 'o_norm_g': 1.794561e+01, 'q_norm_g': 1.321203e+00, 'k_norm_g': 1.318819e+00, 'w_out': 5.240637e-01, 'norm2_g': 6.206646e+00, 'w_gate_up': 1.078511e-01, 'w_down': 1.334491e-01}


def _to_microbatches(a, axis):
    t = _jnp.moveaxis(a, axis, 0)
    t = t.reshape((N_MICROBATCH, t.shape[0] // N_MICROBATCH) + t.shape[1:])
    return _jnp.moveaxis(t, 1, axis + 1)


def setup_inputs(seed: int = 0) -> dict:
    inp = _fwd_setup_inputs(seed)
    key = _jax.random.fold_in(_jax.random.key(seed), 7919)
    shape, _ = _output_shape()
    out = dict(inp)
    out["loss_target"] = _jax.random.normal(_jax.random.fold_in(key, 0), shape, _jnp.float32)
    for i, name in enumerate(TWIN_WEIGHTS):
        w = inp[name].astype(_jnp.float32)
        if MOMENT_SCALE is None:
            s = _jnp.sqrt(_jnp.mean(_jnp.square(w)) + 1e-30)
        else:
            s = MOMENT_SCALE[name]
        km, kv = _jax.random.split(_jax.random.fold_in(key, i + 1))
        out[name] = w
        out["m_" + name] = s * _jax.random.normal(km, w.shape, _jnp.float32)
        out["v_" + name] = (s * s) * _jax.random.uniform(kv, w.shape, _jnp.float32, 0.5, 1.5)
    if N_MICROBATCH > 1:
        for name, axis in PER_EXAMPLE_BATCH_AXIS.items():
            out[name] = _to_microbatches(out[name], axis)
    return {'x': out['x'], 'norm1_g': out['norm1_g'], 'w_in': out['w_in'], 'sgu_norm_g': out['sgu_norm_g'], 'w_spatial': out['w_spatial'], 'b_spatial': out['b_spatial'], 'conv_w': out['conv_w'], 'a_log': out['a_log'], 'dt_bias': out['dt_bias'], 'o_norm_g': out['o_norm_g'], 'q_norm_g': out['q_norm_g'], 'k_norm_g': out['k_norm_g'], 'w_out': out['w_out'], 'norm2_g': out['norm2_g'], 'w_gate_up': out['w_gate_up'], 'w_down': out['w_down'], 'loss_target': out['loss_target'], 'm_norm1_g': out['m_norm1_g'], 'm_w_in': out['m_w_in'], 'm_sgu_norm_g': out['m_sgu_norm_g'], 'm_w_spatial': out['m_w_spatial'], 'm_b_spatial': out['m_b_spatial'], 'm_conv_w': out['m_conv_w'], 'm_a_log': out['m_a_log'], 'm_dt_bias': out['m_dt_bias'], 'm_o_norm_g': out['m_o_norm_g'], 'm_q_norm_g': out['m_q_norm_g'], 'm_k_norm_g': out['m_k_norm_g'], 'm_w_out': out['m_w_out'], 'm_norm2_g': out['m_norm2_g'], 'm_w_gate_up': out['m_w_gate_up'], 'm_w_down': out['m_w_down'], 'v_norm1_g': out['v_norm1_g'], 'v_w_in': out['v_w_in'], 'v_sgu_norm_g': out['v_sgu_norm_g'], 'v_w_spatial': out['v_w_spatial'], 'v_b_spatial': out['v_b_spatial'], 'v_conv_w': out['v_conv_w'], 'v_a_log': out['v_a_log'], 'v_dt_bias': out['v_dt_bias'], 'v_o_norm_g': out['v_o_norm_g'], 'v_q_norm_g': out['v_q_norm_g'], 'v_k_norm_g': out['v_k_norm_g'], 'v_w_out': out['v_w_out'], 'v_norm2_g': out['v_norm2_g'], 'v_w_gate_up': out['v_w_gate_up'], 'v_w_down': out['v_w_down']}


def _loss(weights, diff, rest, loss_target):
    with _jax.named_scope("forward"):
        args = {**rest, TWIN_DIFF_INPUT: diff, **{k: w.astype(_WEIGHT_DTYPES[k]) for k, w in weights.items()}}
        y = _forward(args)
    with _jax.named_scope("loss_head"):
        err = _jnp.square(y.astype(_jnp.float32) - loss_target)
        return 0.5 * _jnp.sum(_jnp.mean(err, axis=-1)) if err.ndim else 0.5 * err


def _adamw(w, g, m, v):
    m = ADAM_B1 * m + (1.0 - ADAM_B1) * g
    v = ADAM_B2 * v + (1.0 - ADAM_B2) * _jnp.square(g)
    m_hat = m / (1.0 - ADAM_B1 ** ADAM_STEP)
    v_hat = v / (1.0 - ADAM_B2 ** ADAM_STEP)
    delta = -ADAM_LR * (m_hat / (_jnp.sqrt(v_hat) + ADAM_EPS) + ADAM_WD * w)
    return delta, m, v


def reference(x, norm1_g, w_in, sgu_norm_g, w_spatial, b_spatial, conv_w, a_log, dt_bias, o_norm_g, q_norm_g, k_norm_g, w_out, norm2_g, w_gate_up, w_down, loss_target, m_norm1_g, m_w_in, m_sgu_norm_g, m_w_spatial, m_b_spatial, m_conv_w, m_a_log, m_dt_bias, m_o_norm_g, m_q_norm_g, m_k_norm_g, m_w_out, m_norm2_g, m_w_gate_up, m_w_down, v_norm1_g, v_w_in, v_sgu_norm_g, v_w_spatial, v_b_spatial, v_conv_w, v_a_log, v_dt_bias, v_o_norm_g, v_q_norm_g, v_k_norm_g, v_w_out, v_norm2_g, v_w_gate_up, v_w_down):
    given = dict(x=x, norm1_g=norm1_g, w_in=w_in, sgu_norm_g=sgu_norm_g, w_spatial=w_spatial, b_spatial=b_spatial, conv_w=conv_w, a_log=a_log, dt_bias=dt_bias, o_norm_g=o_norm_g, q_norm_g=q_norm_g, k_norm_g=k_norm_g, w_out=w_out, norm2_g=norm2_g, w_gate_up=w_gate_up, w_down=w_down, loss_target=loss_target, m_norm1_g=m_norm1_g, m_w_in=m_w_in, m_sgu_norm_g=m_sgu_norm_g, m_w_spatial=m_w_spatial, m_b_spatial=m_b_spatial, m_conv_w=m_conv_w, m_a_log=m_a_log, m_dt_bias=m_dt_bias, m_o_norm_g=m_o_norm_g, m_q_norm_g=m_q_norm_g, m_k_norm_g=m_k_norm_g, m_w_out=m_w_out, m_norm2_g=m_norm2_g, m_w_gate_up=m_w_gate_up, m_w_down=m_w_down, v_norm1_g=v_norm1_g, v_w_in=v_w_in, v_sgu_norm_g=v_sgu_norm_g, v_w_spatial=v_w_spatial, v_b_spatial=v_b_spatial, v_conv_w=v_conv_w, v_a_log=v_a_log, v_dt_bias=v_dt_bias, v_o_norm_g=v_o_norm_g, v_q_norm_g=v_q_norm_g, v_k_norm_g=v_k_norm_g, v_w_out=v_w_out, v_norm2_g=v_norm2_g, v_w_gate_up=v_w_gate_up, v_w_down=v_w_down)
    weights = {n: given[n] for n in TWIN_WEIGHTS}
    shared = {n: given[n] for n in SHARED_INPUTS}
    per_example = {n: given[n] for n in ['x']}
    grad_fn = _jax.value_and_grad(_loss, argnums=(0, 1))

    def one_microbatch(ex, loss_target):
        ex = dict(ex)
        diff = ex.pop(TWIN_DIFF_INPUT)
        return grad_fn(weights, diff, {**shared, **ex}, loss_target)

    if N_MICROBATCH == 1:
        loss, (grad_w, grad_x) = one_microbatch(per_example, given["loss_target"])
    else:
        def body(carry, xs):
            loss_sum, grad_sum = carry
            l_k, (gw_k, gx_k) = one_microbatch(xs[0], xs[1])
            with _jax.named_scope("update"):
                return (loss_sum + l_k, _jax.tree.map(_jnp.add, grad_sum, gw_k)), gx_k

        init = (_jnp.zeros((), _jnp.float32), _jax.tree.map(_jnp.zeros_like, weights))
        (loss, grad_w), grad_x = _jax.lax.scan(body, init, (per_example, given["loss_target"]))
    with _jax.named_scope("update"):
        delta_w, new_m, new_v = {}, {}, {}
        for n in TWIN_WEIGHTS:
            delta_w[n], new_m[n], new_v[n] = _adamw(weights[n], grad_w[n], given["m_" + n], given["v_" + n])
    return (loss, grad_x, *[grad_w[n] for n in TWIN_WEIGHTS], *[delta_w[n] for n in TWIN_WEIGHTS],
            *[new_m[n] for n in TWIN_WEIGHTS], *[new_v[n] for n in TWIN_WEIGHTS])
```

```python
import functools
import math

import numpy as np
import jax
import jax.numpy as jnp
from jax import lax
from jax.experimental import pallas as pl
from jax.experimental.pallas import tpu as pltpu

F32 = jnp.float32
BF16 = jnp.bfloat16
HI = lax.Precision.HIGHEST

T = 2048
D = 2048
DEPTH = 2
HD = 128
A_GROUPS, A_WIDTH, A_CHUNK = 4, 512, 128
B_HEADS, B_WIDTH, B_CONV, B_CHUNK = 6, 768, 4, 64
C_HEADS, C_WIDTH, C_BLOCK = 6, 768, 128
C_BRANCHES = ((128, 1), (512, 4), (2048, 16))
FFN = 5632
IN_TOTAL = 6412
EPS = 1e-6
N_CHIPS = 4
N_DEV = 8
IN_SHARD = IN_TOTAL // N_CHIPS
IN_SHARD_PAD = 1664
GU_SHARD = 2 * FFN // N_CHIPS
OUT_SHARD = D // N_CHIPS
DOWN_SHARD = FFN // N_CHIPS
P_AU, P_AV, P_BQ, P_BK, P_BV, P_BG, P_BB, P_CQ, P_CK, P_CV, P_END = (
    0, 512, 1024, 1792, 2560, 3328, 4096, 4224, 4992, 5760, 6528)
GATE_COLS = 4108
VMEM_LIMIT = 56 * 1024 * 1024

ADAM_LR, ADAM_B1, ADAM_B2, ADAM_EPS, ADAM_WD, ADAM_STEP = 0.001, 0.9, 0.999, 1e-08, 0.01, 10


def _cparams(sem, vmem=VMEM_LIMIT):
    return pltpu.CompilerParams(dimension_semantics=sem, vmem_limit_bytes=vmem)


def _dims(nd, ta, tb):
    off = nd - 2
    ca = off + (0 if ta else 1)
    cb = off + (1 if tb else 0)
    batch = ((0,), (0,)) if nd == 3 else ((), ())
    return (((ca,), (cb,)), batch)


def _raw_mm(a, b, ta, tb, hi):
    if hi:
        return lax.dot_general(a, b, _dims(a.ndim, ta, tb), precision=HI, preferred_element_type=F32)
    return lax.dot_general(a.astype(BF16), b.astype(BF16), _dims(a.ndim, ta, tb), preferred_element_type=F32)


@functools.partial(jax.custom_vjp, nondiff_argnums=(2, 3, 4))
def _mm(a, b, ta=False, tb=False, hi=False):
    return _raw_mm(a, b, ta, tb, hi)


def _mm_fwd(a, b, ta, tb, hi):
    return _raw_mm(a, b, ta, tb, hi), (a, b)


def _mm_bwd(ta, tb, hi, res, g):
    a, b = res
    da = _raw_mm(g, b, False, not tb, hi) if not ta else _raw_mm(b, g, tb, True, hi)
    db = _raw_mm(a, g, not ta, False, hi) if not tb else _raw_mm(g, a, True, ta, hi)
    return da.astype(a.dtype), db.astype(b.dtype)


_mm.defvjp(_mm_fwd, _mm_bwd)


def _rms(x, g):
    return x * lax.rsqrt(jnp.mean(x * x, axis=-1, keepdims=True) + EPS) * g


def _gelu(x):
    return 0.5 * x * (1.0 + jnp.tanh(math.sqrt(2.0 / math.pi) * (x + 0.044715 * (x * x * x))))


def _sigmoid(x):
    return 1.0 / (1.0 + jnp.exp(-x))


def _silu(x):
    return x * _sigmoid(x)


def _softplus(x):
    return jnp.maximum(x, 0.0) + jnp.log(1.0 + jnp.exp(-jnp.abs(x)))


def _iota(shape, dim):
    return lax.broadcasted_iota(jnp.int32, shape, dim)


def _sgu_fn(u, v, sg, w, b):
    nc = T // A_CHUNK
    ug = _gelu(u)
    vn = _rms(_gelu(v), sg)
    causal = _iota((A_CHUNK, A_CHUNK), 0) >= _iota((A_CHUNK, A_CHUNK), 1)
    wm = jnp.where(causal, w, 0.0)
    wb = jnp.broadcast_to(wm[None], (nc, A_CHUNK, A_CHUNK))
    z = _mm(wb, vn.reshape(nc, A_CHUNK, HD)) + b[None]
    return ug * z.reshape(T, HD)


def _sgu_specs():
    col = lambda off: pl.BlockSpec((T, HD), lambda g, off=off: (0, off + g))
    par = [pl.BlockSpec((None, 1, HD), lambda g: (g, 0, 0)),
           pl.BlockSpec((None, A_CHUNK, A_CHUNK), lambda g: (g, 0, 0)),
           pl.BlockSpec((None, A_CHUNK, 1), lambda g: (g, 0, 0))]
    return col, par


def sgu_fwd(p2, sg, w, b):
    col, par = _sgu_specs()

    def body(u_ref, v_ref, sg_ref, w_ref, b_ref, y_ref):
        y_ref[...] = _sgu_fn(u_ref[...], v_ref[...], sg_ref[...], w_ref[...], b_ref[...])

    return pl.pallas_call(
        body, name="sgu_fwd", grid=(A_GROUPS,),
        in_specs=[col(P_AU // HD), col(P_AV // HD)] + par,
        out_specs=pl.BlockSpec((T, HD), lambda g: (0, g)),
        out_shape=jax.ShapeDtypeStruct((T, A_WIDTH), F32),
        compiler_params=_cparams(("arbitrary",)),
    )(p2, p2, sg, w, b)


def sgu_bwd(p2, sg, w, b, dmix):
    col, par = _sgu_specs()

    def body(u_ref, v_ref, sg_ref, w_ref, b_ref, dy_ref, du_ref, dv_ref, dsg_ref, dw_ref, db_ref):
        _, vjp = jax.vjp(_sgu_fn, u_ref[...], v_ref[...], sg_ref[...], w_ref[...], b_ref[...])
        du, dv, dsg, dw, db = vjp(dy_ref[...])
        du_ref[...] = du
        dv_ref[...] = dv
        dsg_ref[...] = dsg
        dw_ref[...] = dw
        db_ref[...] = db

    gcol = pl.BlockSpec((T, HD), lambda g: (0, g))
    return pl.pallas_call(
        body, name="sgu_bwd", grid=(A_GROUPS,),
        in_specs=[col(P_AU // HD), col(P_AV // HD)] + par + [gcol],
        out_specs=[gcol, gcol] + par,
        out_shape=[jax.ShapeDtypeStruct((T, A_WIDTH), F32), jax.ShapeDtypeStruct((T, A_WIDTH), F32),
                   jax.ShapeDtypeStruct((A_GROUPS, 1, HD), F32), jax.ShapeDtypeStruct((A_GROUPS, A_CHUNK, A_CHUNK), F32),
                   jax.ShapeDtypeStruct((A_GROUPS, A_CHUNK, 1), F32)],
        compiler_params=_cparams(("arbitrary",)),
    )(p2, p2, sg, w, b, dmix)


def _attn_fn(q, k, v, qg, kg, slope, *, dil, nb):
    n = T // C_BLOCK
    qb = _rms(q, qg).reshape(n, C_BLOCK, HD)
    kb = _rms(k, kg).reshape(n, C_BLOCK, HD)
    vb = v.reshape(n, C_BLOCK, HD)
    scale = HD ** -0.5
    qi = _iota((n, C_BLOCK, C_BLOCK), 1)
    kj = _iota((n, C_BLOCK, C_BLOCK), 2)
    sl = slope[None] * float(dil)
    d_cur = qi - kj
    sc = jnp.where(d_cur >= 0, _mm(qb, kb, tb=True) * scale - sl * d_cur.astype(F32), -jnp.inf)
    mx = jnp.max(sc, axis=-1, keepdims=True)
    if nb > 1:
        kp = jnp.concatenate([jnp.zeros((1, C_BLOCK, HD), F32), kb[:-1]], axis=0)
        vp = jnp.concatenate([jnp.zeros((1, C_BLOCK, HD), F32), vb[:-1]], axis=0)
        has_prev = (_iota((n, C_BLOCK, C_BLOCK), 0) % nb) > 0
        d_prev = C_BLOCK + qi - kj
        sp = jnp.where((kj >= qi) & has_prev, _mm(qb, kp, tb=True) * scale - sl * d_prev.astype(F32), -jnp.inf)
        mx = jnp.maximum(mx, jnp.max(sp, axis=-1, keepdims=True))
    p = jnp.exp(sc - mx)
    den = jnp.sum(p, axis=-1, keepdims=True)
    if nb > 1:
        pp = jnp.exp(sp - mx)
        den = den + jnp.sum(pp, axis=-1, keepdims=True)
    out = _mm(p / den, vb)
    if nb > 1:
        out = out + _mm(pp / den, vp)
    lse = mx + jnp.log(den)
    return out.reshape(T, HD), jnp.broadcast_to(lse, (n, C_BLOCK, HD)).reshape(T, HD)


def _attn_specs():
    hcol = pl.BlockSpec((T, HD), lambda h: (0, h))
    row = pl.BlockSpec((1, HD), lambda h: (0, 0))
    slope = pl.BlockSpec((None, 1, HD), lambda h: (h, 0, 0))
    return hcol, row, slope


def attn_fwd(q, k, v, qg, kg, slopes, dil, nb):
    hcol, row, slope = _attn_specs()

    def body(q_ref, k_ref, v_ref, qg_ref, kg_ref, s_ref, o_ref, l_ref):
        o, l = _attn_fn(q_ref[...], k_ref[...], v_ref[...], qg_ref[...], kg_ref[...], s_ref[...], dil=dil, nb=nb)
        o_ref[...] = o
        l_ref[...] = l

    return pl.pallas_call(
        body, name=f"attn_fwd_d{dil}", grid=(C_HEADS,),
        in_specs=[hcol, hcol, hcol, row, row, slope], out_specs=[hcol, hcol],
        out_shape=[jax.ShapeDtypeStruct((T, C_WIDTH), F32)] * 2,
        compiler_params=_cparams(("arbitrary",)),
    )(q, k, v, qg, kg, slopes)


def attn_bwd(q, k, v, qg, kg, slopes, do, dl, dil, nb):
    hcol, row, slope = _attn_specs()

    def body(q_ref, k_ref, v_ref, qg_ref, kg_ref, s_ref, do_ref, dl_ref, dq_ref, dk_ref, dv_ref, dqg_ref, dkg_ref):
        fn = functools.partial(_attn_fn, dil=dil, nb=nb)
        _, vjp = jax.vjp(lambda a, b, c, d, e: fn(a, b, c, d, e, s_ref[...]),
                         q_ref[...], k_ref[...], v_ref[...], qg_ref[...], kg_ref[...])
        dq, dk, dv, dqg, dkg = vjp((do_ref[...], dl_ref[...]))
        dq_ref[...] = dq
        dk_ref[...] = dk
        dv_ref[...] = dv

        @pl.when(pl.program_id(0) == 0)
        def _():
            dqg_ref[...] = jnp.zeros_like(dqg_ref)
            dkg_ref[...] = jnp.zeros_like(dkg_ref)

        dqg_ref[...] += dqg
        dkg_ref[...] += dkg

    return pl.pallas_call(
        body, name=f"attn_bwd_d{dil}", grid=(C_HEADS,),
        in_specs=[hcol, hcol, hcol, row, row, slope, hcol, hcol], out_specs=[hcol, hcol, hcol, row, row],
        out_shape=[jax.ShapeDtypeStruct((T, C_WIDTH), F32)] * 3 + [jax.ShapeDtypeStruct((1, HD), F32)] * 2,
        compiler_params=_cparams(("arbitrary",)),
    )(q, k, v, qg, kg, slopes, do, dl)


def _combine_fn(o1, o2, o3, l1, l2, l3):
    mx = jnp.maximum(jnp.maximum(l1, l2), l3)
    e1, e2, e3 = jnp.exp(l1 - mx), jnp.exp(l2 - mx), jnp.exp(l3 - mx)
    s = e1 + e2 + e3
    return (e1 / s) * o1 + (e2 / s) * o2 + (e3 / s) * o3


_CMB_ROWS = 512


def combine_fwd(outs, lses):
    blk = pl.BlockSpec((_CMB_ROWS, HD), lambda i, h: (i, h))

    def body(o1, o2, o3, l1, l2, l3, y_ref):
        y_ref[...] = _combine_fn(o1[...], o2[...], o3[...], l1[...], l2[...], l3[...])

    return pl.pallas_call(
        body, name="combine_fwd", grid=(T // _CMB_ROWS, C_HEADS), in_specs=[blk] * 6, out_specs=blk,
        out_shape=jax.ShapeDtypeStruct((T, C_WIDTH), F32), compiler_params=_cparams(("arbitrary", "arbitrary")),
    )(*outs, *lses)


def combine_bwd(outs, lses, dmix):
    blk = pl.BlockSpec((_CMB_ROWS, HD), lambda i, h: (i, h))
    dblk = pl.BlockSpec((_CMB_ROWS, HD), lambda i, h: (i, (A_WIDTH + B_WIDTH) // HD + h))

    def body(o1, o2, o3, l1, l2, l3, dy_ref, *outs_ref):
        _, vjp = jax.vjp(_combine_fn, o1[...], o2[...], o3[...], l1[...], l2[...], l3[...])
        for r, g in zip(outs_ref, vjp(dy_ref[...])):
            r[...] = g

    return pl.pallas_call(
        body, name="combine_bwd", grid=(T // _CMB_ROWS, C_HEADS), in_specs=[blk] * 6 + [dblk], out_specs=[blk] * 6,
        out_shape=[jax.ShapeDtypeStruct((T, C_WIDTH), F32)] * 6, compiler_params=_cparams(("arbitrary", "arbitrary")),
    )(*outs, *lses, dmix)


_NCH = T // B_CHUNK


def _conv_taps(x, w_ref):
    rows = _iota(x.shape, 0)
    taps = []
    for j in range(B_CONV):
        s = B_CONV - 1 - j
        taps.append(x if s == 0 else jnp.where(rows >= s, pltpu.roll(x, s, 0), 0.0))
    pre = sum(w_ref[j:j + 1, :] * taps[j] for j in range(B_CONV))
    return pre, taps


def _conv_post(pre, mode):
    y = _silu(pre)
    if mode == "v":
        return y
    y = y * lax.rsqrt(jnp.sum(y * y, axis=-1, keepdims=True) + EPS)
    return y * (HD ** -0.5) if mode == "q" else y


def conv_fwd(p2, conv_w, mode):
    idx = "qkv".index(mode)
    xcol = pl.BlockSpec((T, HD), lambda h: (0, P_BQ // HD + B_HEADS * idx + h))
    wcol = pl.BlockSpec((B_CONV, HD), lambda h: (0, B_HEADS * idx + h))
    hcol = pl.BlockSpec((T, HD), lambda h: (0, h))

    def body(x_ref, w_ref, y_ref):
        pre, _ = _conv_taps(x_ref[...], w_ref)
        y_ref[...] = _conv_post(pre, mode)

    return pl.pallas_call(
        body, name=f"conv_fwd_{mode}", grid=(B_HEADS,), in_specs=[xcol, wcol], out_specs=hcol,
        out_shape=jax.ShapeDtypeStruct((T, B_WIDTH), F32), compiler_params=_cparams(("arbitrary",)),
    )(p2, conv_w)


def conv_bwd(p2, conv_w, dys, mode):
    idx = "qkv".index(mode)
    xcol = pl.BlockSpec((T, HD), lambda h: (0, P_BQ // HD + B_HEADS * idx + h))
    wcol = pl.BlockSpec((B_CONV, HD), lambda h: (0, B_HEADS * idx + h))
    hcol = pl.BlockSpec((T, HD), lambda h: (0, h))
    wout = pl.BlockSpec((B_CONV, HD), lambda h: (0, h))

    def body(x_ref, w_ref, *rest):
        dy_refs, (dx_ref, dw_ref) = rest[:-2], rest[-2:]
        pre, taps = _conv_taps(x_ref[...], w_ref)
        _, vjp = jax.vjp(functools.partial(_conv_post, mode=mode), pre)
        (dpre,) = vjp(sum(r[...] for r in dy_refs))
        rows = _iota(dpre.shape, 0)
        dx = w_ref[B_CONV - 1:B_CONV, :] * dpre
        for j in range(B_CONV):
            s = B_CONV - 1 - j
            dw_ref[j:j + 1, :] = jnp.sum(dpre * taps[j], axis=0, keepdims=True)
            if s > 0:
                dx = dx + w_ref[j:j + 1, :] * jnp.where(rows < T - s, pltpu.roll(dpre, T - s, 0), 0.0)
        dx_ref[...] = dx

    return pl.pallas_call(
        body, name=f"conv_bwd_{mode}", grid=(B_HEADS,), in_specs=[xcol, wcol] + [hcol] * len(dys), out_specs=[hcol, wout],
        out_shape=[jax.ShapeDtypeStruct((T, B_WIDTH), F32), jax.ShapeDtypeStruct((B_CONV, B_WIDTH), F32)],
        compiler_params=_cparams(("arbitrary",)),
    )(p2, conv_w, *dys)


def _gates_fn(bg, al, dtb, h):
    r = _iota((HD, HD), 0)
    logit = _mm(bg, (r == h).astype(F32), hi=True)
    a = _mm(bg, (r == h + B_HEADS).astype(F32), hi=True)
    beta = _sigmoid(logit)
    graw = -jnp.exp(al) * _softplus(a + dtb)
    tri = (_iota((_NCH, B_CHUNK, B_CHUNK), 1) >= _iota((_NCH, B_CHUNK, B_CHUNK), 2)).astype(F32)
    g = _mm(tri, graw.reshape(_NCH, B_CHUNK, HD), hi=True).reshape(T, HD)
    return beta, g


def _gates_specs():
    bg = pl.BlockSpec((T, HD), lambda h: (0, P_BB // HD))
    par = pl.BlockSpec((None, 1, HD), lambda h: (h, 0, 0))
    out = pl.BlockSpec((None, T, HD), lambda h: (h, 0, 0))
    return bg, par, out


def gates_fwd(p2, al, dtb):
    bg, par, out = _gates_specs()

    def body(bg_ref, al_ref, dtb_ref, beta_ref, g_ref):
        beta, g = _gates_fn(bg_ref[...], al_ref[...], dtb_ref[...], pl.program_id(0))
        beta_ref[...] = beta
        g_ref[...] = g

    return pl.pallas_call(
        body, name="gates_fwd", grid=(B_HEADS,), in_specs=[bg, par, par], out_specs=[out, out],
        out_shape=[jax.ShapeDtypeStruct((B_HEADS, T, HD), F32)] * 2, compiler_params=_cparams(("arbitrary",)),
    )(p2, al, dtb)


def gates_bwd(p2, al, dtb, dbeta, dg1, dg2):
    bg, par, out = _gates_specs()
    acc = pl.BlockSpec((T, HD), lambda h: (0, 0))

    def body(bg_ref, al_ref, dtb_ref, dbeta_ref, dg1_ref, dg2_ref, dbg_ref, dal_ref, ddtb_ref):
        h = pl.program_id(0)
        _, vjp = jax.vjp(lambda a, b, c: _gates_fn(a, b, c, h), bg_ref[...], al_ref[...], dtb_ref[...])
        dbg, dal, ddtb = vjp((dbeta_ref[...], dg1_ref[...] + dg2_ref[...]))

        @pl.when(h == 0)
        def _():
            dbg_ref[...] = jnp.zeros_like(dbg_ref)

        dbg_ref[...] += dbg
        dal_ref[...] = jnp.broadcast_to(jnp.sum(dal, axis=-1, keepdims=True), (1, HD))
        ddtb_ref[...] = jnp.broadcast_to(jnp.sum(ddtb, axis=-1, keepdims=True), (1, HD))

    return pl.pallas_call(
        body, name="gates_bwd", grid=(B_HEADS,), in_specs=[bg, par, par, out, out, out], out_specs=[acc, par, par],
        out_shape=[jax.ShapeDtypeStruct((T, HD), F32)] + [jax.ShapeDtypeStruct((B_HEADS, 1, HD), F32)] * 2,
        compiler_params=_cparams(("arbitrary",)),
    )(p2, al, dtb, dbeta, dg1, dg2)


def _unit_lower_inverse(a):
    eye = (_iota(a.shape, 1) == _iota(a.shape, 2)).astype(F32)
    x = eye - a
    p = _mm(a, a, hi=True)
    for i in range(5):
        x = x + _mm(x, p, hi=True)
        if i < 4:
            p = _mm(p, p, hi=True)
    return x


_WY_CH = 8
_WY_ROWS = _WY_CH * B_CHUNK


def _wy_fn(q, k, v, beta, g):
    sh = (q.shape[0] // B_CHUNK, B_CHUNK, HD)
    q3, k3, v3, b3, g3 = (t.reshape(sh) for t in (q, k, v, beta, g))
    gd = g3[:, :, :B_CHUNK] - jnp.swapaxes(g3, 1, 2)[:, :B_CHUNK, :]
    ii, jj = _iota(gd.shape, 1), _iota(gd.shape, 2)
    decay = jnp.exp(jnp.where(ii >= jj, gd, -jnp.inf))
    kb = k3 * b3
    a = _mm(kb, k3, tb=True) * jnp.where(ii > jj, decay, 0.0)
    tinv = _unit_lower_inverse(a)
    u = _mm(tinv, v3 * b3, hi=True)
    w = _mm(tinv, kb * jnp.exp(g3), hi=True)
    attn = _mm(q3, k3, tb=True) * decay
    return u.reshape(q.shape), w.reshape(q.shape), attn


def _wy_specs():
    hcol = pl.BlockSpec((_WY_ROWS, HD), lambda h, i: (i, h))
    hb = pl.BlockSpec((None, _WY_ROWS, HD), lambda h, i: (h, i, 0))
    at = pl.BlockSpec((None, _WY_CH, B_CHUNK, B_CHUNK), lambda h, i: (h, i, 0, 0))
    return hcol, hb, at


_WY_GRID = (B_HEADS, _NCH // _WY_CH)


def wy_fwd(q, k, v, beta, g):
    hcol, hb, at = _wy_specs()

    def body(q_ref, k_ref, v_ref, b_ref, g_ref, u_ref, w_ref, a_ref):
        u, w, a = _wy_fn(q_ref[...], k_ref[...], v_ref[...], b_ref[...], g_ref[...])
        u_ref[...] = u
        w_ref[...] = w
        a_ref[...] = a

    return pl.pallas_call(
        body, name="wy_fwd", grid=_WY_GRID, in_specs=[hcol, hcol, hcol, hb, hb], out_specs=[hcol, hcol, at],
        out_shape=[jax.ShapeDtypeStruct((T, B_WIDTH), F32)] * 2 + [jax.ShapeDtypeStruct((B_HEADS, _NCH, B_CHUNK, B_CHUNK), F32)],
        compiler_params=_cparams(("arbitrary", "arbitrary")),
    )(q, k, v, beta, g)


def wy_bwd(q, k, v, beta, g, du, dw, dattn):
    hcol, hb, at = _wy_specs()

    def body(q_ref, k_ref, v_ref, b_ref, g_ref, du_ref, dw_ref, da_ref, dq_ref, dk_ref, dv_ref, db_ref, dg_ref):
        _, vjp = jax.vjp(_wy_fn, q_ref[...], k_ref[...], v_ref[...], b_ref[...], g_ref[...])
        for r, t in zip((dq_ref, dk_ref, dv_ref, db_ref, dg_ref), vjp((du_ref[...], dw_ref[...], da_ref[...]))):
            r[...] = t

    return pl.pallas_call(
        body, name="wy_bwd", grid=_WY_GRID, in_specs=[hcol, hcol, hcol, hb, hb, hcol, hcol, at],
        out_specs=[hcol, hcol, hcol, hb, hb],
        out_shape=[jax.ShapeDtypeStruct((T, B_WIDTH), F32)] * 3 + [jax.ShapeDtypeStruct((B_HEADS, T, HD), F32)] * 2,
        compiler_params=_cparams(("arbitrary", "arbitrary")),
    )(q, k, v, beta, g, du, dw, dattn)


def _scan_step_fn(q, k, u, w, g, attn, gate, og, s):
    v_new = u - _mm(w, s)
    o = _mm(q * jnp.exp(g), s) + _mm(attn, v_new)
    g_last = jnp.sum(jnp.where(_iota(g.shape, 0) == B_CHUNK - 1, g, 0.0), axis=0, keepdims=True)
    s_new = s * jnp.exp(g_last) + _mm(k * jnp.exp(g_last - g), v_new, ta=True)
    return _rms(o, og) * _silu(gate), s_new


def _scan_specs(rev):
    ch = (lambda n: _NCH - 1 - n) if rev else (lambda n: n)
    rows = pl.BlockSpec((B_CHUNK, B_WIDTH), lambda n: (ch(n), 0))
    gb = pl.BlockSpec((B_HEADS, B_CHUNK, HD), lambda n: (0, ch(n), 0))
    at = pl.BlockSpec((B_HEADS, None, B_CHUNK, B_CHUNK), lambda n: (0, ch(n), 0, 0))
    og = pl.BlockSpec((1, HD), lambda n: (0, 0))
    st = pl.BlockSpec((None, B_HEADS, HD, HD), lambda n: (ch(n), 0, 0, 0))
    return rows, gb, at, og, st


def scan_fwd(q, k, u, w, g, attn, gate, og):
    rows, gb, at, ogs, st = _scan_specs(False)

    def body(q_ref, k_ref, u_ref, w_ref, g_ref, a_ref, gate_ref, og_ref, y_ref, st_ref, s_ref):
        @pl.when(pl.program_id(0) == 0)
        def _():
            s_ref[...] = jnp.zeros_like(s_ref)

        for h in range(B_HEADS):
            c = slice(h * HD, (h + 1) * HD)
            s = s_ref[h]
            st_ref[h] = s
            y, s_new = _scan_step_fn(q_ref[:, c], k_ref[:, c], u_ref[:, c], w_ref[:, c], g_ref[h], a_ref[h],
                                     gate_ref[:, c], og_ref[...], s)
            y_ref[:, c] = y
            s_ref[h] = s_new

    return pl.pallas_call(
        body, name="scan_fwd", grid=(_NCH,), in_specs=[rows, rows, rows, rows, gb, at, rows, ogs], out_specs=[rows, st],
        out_shape=[jax.ShapeDtypeStruct((T, B_WIDTH), F32), jax.ShapeDtypeStruct((_NCH, B_HEADS, HD, HD), F32)],
        scratch_shapes=[pltpu.VMEM((B_HEADS, HD, HD), F32)], compiler_params=_cparams(("arbitrary",)),
    )(q, k, u, w, g, attn, gate, og)


def scan_bwd(q, k, u, w, g, attn, gate, og, states, dmix):
    rows, gb, at, ogs, st = _scan_specs(True)
    dyb = pl.BlockSpec((B_CHUNK, HD), lambda n: (_NCH - 1 - n, 0))

    def body(q_ref, k_ref, u_ref, w_ref, g_ref, a_ref, gate_ref, og_ref, st_ref, *rest):
        dy_refs, (dq_ref, dk_ref, du_ref, dw_ref, dgate_ref, dg_ref, da_ref, dog_ref, ds_ref) = rest[:B_HEADS], rest[B_HEADS:]

        @pl.when(pl.program_id(0) == 0)
        def _():
            ds_ref[...] = jnp.zeros_like(ds_ref)
            dog_ref[...] = jnp.zeros_like(dog_ref)

        for h in range(B_HEADS):
            c = slice(h * HD, (h + 1) * HD)
            _, vjp = jax.vjp(_scan_step_fn, q_ref[:, c], k_ref[:, c], u_ref[:, c], w_ref[:, c], g_ref[h], a_ref[h],
                             gate_ref[:, c], og_ref[...], st_ref[h])
            dq, dk, du, dw, dg, da, dgate, dog, ds = vjp((dy_refs[h][...], ds_ref[h]))
            dq_ref[:, c] = dq
            dk_ref[:, c] = dk
            du_ref[:, c] = du
            dw_ref[:, c] = dw
            dgate_ref[:, c] = dgate
            dg_ref[h] = dg
            da_ref[h] = da
            dog_ref[...] += dog
            ds_ref[h] = ds

    dy_specs = [pl.BlockSpec((B_CHUNK, HD), lambda n, h=h: (_NCH - 1 - n, A_WIDTH // HD + h)) for h in range(B_HEADS)]
    return pl.pallas_call(
        body, name="scan_bwd", grid=(_NCH,),
        in_specs=[rows, rows, rows, rows, gb, at, rows, ogs, st] + dy_specs,
        out_specs=[rows] * 5 + [gb, at, ogs],
        out_shape=[jax.ShapeDtypeStruct((T, B_WIDTH), F32)] * 5
        + [jax.ShapeDtypeStruct((B_HEADS, T, HD), F32), jax.ShapeDtypeStruct((B_HEADS, _NCH, B_CHUNK, B_CHUNK), F32),
           jax.ShapeDtypeStruct((1, HD), F32)],
        scratch_shapes=[pltpu.VMEM((B_HEADS, HD, HD), F32)], compiler_params=_cparams(("arbitrary",)),
    )(q, k, u, w, g, attn, gate, og, states, *([dmix] * B_HEADS))


def _lanes(vec):
    return jnp.broadcast_to(vec[:, None, None], (vec.shape[0], 1, HD))


def gdn_forward(p2, conv_w, a_log, dt_bias, og):
    qa, ka, va = (conv_fwd(p2, conv_w, m) for m in "qkv")
    beta, g = gates_fwd(p2, _lanes(a_log), _lanes(dt_bias))
    u, w, attn = wy_fwd(qa, ka, va, beta, g)
    gate = p2[:, P_BG:P_BB]
    y, states = scan_fwd(qa, ka, u, w, g, attn, gate, og)
    return y, (qa, ka, va, beta, g, u, w, attn, gate, states)


def gdn_backward(p2, conv_w, a_log, dt_bias, og, saved, dmix):
    qa, ka, va, beta, g, u, w, attn, gate, states = saved
    dq1, dk1, du, dw, dgate, dg1, dattn, dog = scan_bwd(qa, ka, u, w, g, attn, gate, og, states, dmix)
    dq2, dk2, dv, dbeta, dg2 = wy_bwd(qa, ka, va, beta, g, du, dw, dattn)
    dbg, dal, ddtb = gates_bwd(p2, _lanes(a_log), _lanes(dt_bias), dbeta, dg1, dg2)
    dxq, dwq = conv_bwd(p2, conv_w, [dq1, dq2], "q")
    dxk, dwk = conv_bwd(p2, conv_w, [dk1, dk2], "k")
    dxv, dwv = conv_bwd(p2, conv_w, [dv], "v")
    dseg = jnp.concatenate([dxq, dxk, dxv, dgate, dbg], axis=1)
    return dseg, jnp.concatenate([dwq, dwk, dwv], axis=1), dal[:, 0, 0], ddtb[:, 0, 0], dog


def _to_branch_order(x, dil):
    return x if dil == 1 else x.reshape(T // dil, dil, -1).transpose(1, 0, 2).reshape(T, -1)


def _from_branch_order(x, dil):
    return x if dil == 1 else x.reshape(dil, T // dil, -1).transpose(1, 0, 2).reshape(T, -1)


_SLOPES = np.exp2(-8.0 * (np.arange(C_HEADS, dtype=np.float64) + 1.0) / C_HEADS).astype(np.float32)


def _branch_blocks(dil):
    return -(-(T // dil) // C_BLOCK)


def dattn_forward(p2, qg, kg):
    slopes = _lanes(jnp.asarray(_SLOPES))
    q, k, v = p2[:, P_CQ:P_CK], p2[:, P_CK:P_CV], p2[:, P_CV:P_END]
    qkvs, outs, lses = [], [], []
    for _, dil in C_BRANCHES:
        qkv = tuple(_to_branch_order(t, dil) for t in (q, k, v))
        o, l = attn_fwd(*qkv, qg, kg, slopes, dil, _branch_blocks(dil))
        qkvs.append(qkv)
        outs.append(_from_branch_order(o, dil))
        lses.append(_from_branch_order(l, dil))
    return combine_fwd(outs, lses), (qkvs, outs, lses)


def dattn_backward(qg, kg, saved, dmix):
    slopes = _lanes(jnp.asarray(_SLOPES))
    qkvs, outs, lses = saved
    gs = combine_bwd(outs, lses, dmix)
    dqs, dks, dvs, dqgs, dkgs = [], [], [], [], []
    for i, (_, dil) in enumerate(C_BRANCHES):
        do, dl = _to_branch_order(gs[i], dil), _to_branch_order(gs[3 + i], dil)
        dq, dk, dv, dqg, dkg = attn_bwd(*qkvs[i], qg, kg, slopes, do, dl, dil, _branch_blocks(dil))
        dqs.append(_from_branch_order(dq, dil))
        dks.append(_from_branch_order(dk, dil))
        dvs.append(_from_branch_order(dv, dil))
        dqgs.append(dqg)
        dkgs.append(dkg)
    return dqs, dks, dvs, dqgs, dkgs


def sum_branches(big, rows):
    blk = pl.BlockSpec((_CMB_ROWS, C_WIDTH), lambda i: (i, 0))
    row = pl.BlockSpec((1, HD), lambda i: (0, 0))
    nb, nr = len(big) // 3, len(rows) // 3

    def body(*refs):
        ins, outs = refs[:len(big) + len(rows)], refs[len(big) + len(rows):]
        for j in range(nb + nr):
            outs[j][...] = ins[3 * j][...] + ins[3 * j + 1][...] + ins[3 * j + 2][...]

    return pl.pallas_call(
        body, name="sum_branches", grid=(T // _CMB_ROWS,), in_specs=[blk] * len(big) + [row] * len(rows),
        out_specs=[blk] * nb + [row] * nr,
        out_shape=[jax.ShapeDtypeStruct((T, C_WIDTH), F32)] * nb + [jax.ShapeDtypeStruct((1, HD), F32)] * nr,
        compiler_params=_cparams(("arbitrary",)),
    )(*big, *rows)


_ROWS = 256
_TM = 512


def rmsnorm_fwd(x, g):
    blk = pl.BlockSpec((_ROWS, D), lambda i: (i, 0))

    def body(x_ref, g_ref, h_ref):
        h_ref[...] = _rms(x_ref[...], g_ref[...]).astype(BF16)

    return pl.pallas_call(
        body, name="rmsnorm_fwd", grid=(T // _ROWS,), in_specs=[blk, pl.BlockSpec((1, D), lambda i: (0, 0))], out_specs=blk,
        out_shape=jax.ShapeDtypeStruct((T, D), BF16), compiler_params=_cparams(("arbitrary",)),
    )(x, g)


def rmsnorm_bwd(x, g, dh, dres):
    blk = pl.BlockSpec((_ROWS, D), lambda i: (i, 0))
    row = pl.BlockSpec((1, D), lambda i: (0, 0))

    def body(x_ref, g_ref, dh_ref, dres_ref, dx_ref, dg_ref):
        _, vjp = jax.vjp(_rms, x_ref[...], g_ref[...])
        dx, dg = vjp(dh_ref[...])
        dx_ref[...] = dres_ref[...] + dx

        @pl.when(pl.program_id(0) == 0)
        def _():
            dg_ref[...] = jnp.zeros_like(dg_ref)

        dg_ref[...] += dg

    return pl.pallas_call(
        body, name="rmsnorm_bwd", grid=(T // _ROWS,), in_specs=[blk, row, blk, blk], out_specs=[blk, row],
        out_shape=[jax.ShapeDtypeStruct((T, D), F32), jax.ShapeDtypeStruct((1, D), F32)],
        compiler_params=_cparams(("arbitrary",)),
    )(x, g, dh, dres)


def _matmul(name, a, b, *, grid, a_spec, b_spec, o_spec, out_shape, ta=False, tb=False, k_axis=None, res=None):
    dims = _dims(2, ta, tb)

    def body(a_ref, b_ref, *rest):
        o_ref = rest[-1]
        prod = lax.dot_general(a_ref[...].astype(BF16), b_ref[...].astype(BF16), dims, preferred_element_type=F32)
        if res is not None:
            prod = prod + rest[0][...]
        if k_axis is None:
            o_ref[...] = prod.astype(o_ref.dtype)
        else:
            @pl.when(pl.program_id(k_axis) == 0)
            def _():
                o_ref[...] = prod

            @pl.when(pl.program_id(k_axis) > 0)
            def _():
                o_ref[...] += prod

    sem = tuple("arbitrary" for _ in grid)
    ins = [a, b] + ([res] if res is not None else [])
    specs = [a_spec, b_spec] + ([o_spec] if res is not None else [])
    return pl.pallas_call(
        body, name=name, grid=grid, in_specs=specs, out_specs=o_spec, out_shape=out_shape, compiler_params=_cparams(sem),
    )(*ins)


def mm_proj(h1, wg_in, l):
    n = IN_SHARD_PAD
    return _matmul(
        "mm_proj", h1, wg_in, grid=(N_CHIPS, T // _TM),
        a_spec=pl.BlockSpec((_TM, D), lambda s, i: (i, 0)),
        b_spec=pl.BlockSpec((None, None, D, n), lambda s, i: (l, s, 0, 0)),
        o_spec=pl.BlockSpec((_TM, n), lambda s, i: (i, s)), out_shape=jax.ShapeDtypeStruct((T, N_CHIPS * n), F32))


def mm_dh1(dproj, wg_in, l):
    n = IN_SHARD_PAD
    return _matmul(
        "mm_dh1", dproj, wg_in, grid=(T // _TM, N_CHIPS), tb=True, k_axis=1,
        a_spec=pl.BlockSpec((_TM, n), lambda i, s: (i, s)),
        b_spec=pl.BlockSpec((None, None, D, n), lambda i, s: (l, s, 0, 0)),
        o_spec=pl.BlockSpec((_TM, D), lambda i, s: (i, 0)), out_shape=jax.ShapeDtypeStruct((T, D), F32))


def mm_dwin(h1, dproj):
    n = IN_SHARD_PAD
    return _matmul(
        "mm_dwin", h1, dproj, grid=(N_CHIPS, D // _TM), ta=True,
        a_spec=pl.BlockSpec((T, _TM), lambda s, i: (0, i)),
        b_spec=pl.BlockSpec((T, n), lambda s, i: (0, s)),
        o_spec=pl.BlockSpec((None, _TM, n), lambda s, i: (s, i, 0)), out_shape=jax.ShapeDtypeStruct((N_CHIPS, D, n), BF16))


def _mm_square(name, a, w, l, res, tb):
    tn = 1024
    b_spec = (pl.BlockSpec((None, tn, D), lambda j, i: (l, j, 0)) if tb else pl.BlockSpec((None, D, tn), lambda j, i: (l, 0, j)))
    return _matmul(
        name, a, w, grid=(D // tn, T // _TM), tb=tb, res=res,
        a_spec=pl.BlockSpec((_TM, D), lambda j, i: (i, 0)), b_spec=b_spec,
        o_spec=pl.BlockSpec((_TM, tn), lambda j, i: (i, j)), out_shape=jax.ShapeDtypeStruct((T, D), F32))


def mm_out(mix, wg_out, l, x):
    return _mm_square("mm_out", mix, wg_out, l, x, False)


def mm_dmix(dx1, wg_out, l):
    return _mm_square("mm_dmix", dx1, wg_out, l, None, True)


def mm_dwout(mix, dx1):
    tn = 1024
    return _matmul(
        "mm_dwout", mix, dx1, grid=(D // tn, D // _TM), ta=True,
        a_spec=pl.BlockSpec((T, _TM), lambda j, i: (0, i)), b_spec=pl.BlockSpec((T, tn), lambda j, i: (0, j)),
        o_spec=pl.BlockSpec((_TM, tn), lambda j, i: (i, j)), out_shape=jax.ShapeDtypeStruct((D, D), BF16))


_GU_TN = GU_SHARD // 2


def mm_gu(h2, wg_gu, l):
    return _matmul(
        "mm_gu", h2, wg_gu, grid=(N_CHIPS, 2, T // _TM),
        a_spec=pl.BlockSpec((_TM, D), lambda s, j, i: (i, 0)),
        b_spec=pl.BlockSpec((None, None, D, _GU_TN), lambda s, j, i: (l, s, 0, j)),
        o_spec=pl.BlockSpec((_TM, _GU_TN), lambda s, j, i: (i, 2 * s + j)),
        out_shape=jax.ShapeDtypeStruct((T, 2 * FFN), F32))


def mm_dh2(dgu, wg_gu, l):
    return _matmul(
        "mm_dh2", dgu, wg_gu, grid=(T // _TM, 2 * N_CHIPS), tb=True, k_axis=1,
        a_spec=pl.BlockSpec((_TM, _GU_TN), lambda i, k: (i, k)),
        b_spec=pl.BlockSpec((None, None, D, _GU_TN), lambda i, k: (l, k // 2, 0, k % 2)),
        o_spec=pl.BlockSpec((_TM, D), lambda i, k: (i, 0)), out_shape=jax.ShapeDtypeStruct((T, D), F32))


def mm_dwgu(h2, dgu):
    return _matmul(
        "mm_dwgu", h2, dgu, grid=(N_CHIPS, 2, D // _TM), ta=True,
        a_spec=pl.BlockSpec((T, _TM), lambda s, j, i: (0, i)),
        b_spec=pl.BlockSpec((T, _GU_TN), lambda s, j, i: (0, 2 * s + j)),
        o_spec=pl.BlockSpec((None, _TM, _GU_TN), lambda s, j, i: (s, i, j)),
        out_shape=jax.ShapeDtypeStruct((N_CHIPS, D, GU_SHARD), BF16))


def mm_down(act, wg_down, l, x1):
    tn = 512
    return _matmul(
        "mm_down", act, wg_down, grid=(D // tn, T // _TM), res=x1,
        a_spec=pl.BlockSpec((_TM, FFN), lambda j, i: (i, 0)),
        b_spec=pl.BlockSpec((None, FFN, tn), lambda j, i: (l, 0, j)),
        o_spec=pl.BlockSpec((_TM, tn), lambda j, i: (i, j)), out_shape=jax.ShapeDtypeStruct((T, D), F32))


def mm_dact(dx2, wg_down, l):
    tn = DOWN_SHARD
    return _matmul(
        "mm_dact", dx2, wg_down, grid=(FFN // tn, T // _TM), tb=True,
        a_spec=pl.BlockSpec((_TM, D), lambda j, i: (i, 0)),
        b_spec=pl.BlockSpec((None, tn, D), lambda j, i: (l, j, 0)),
        o_spec=pl.BlockSpec((_TM, tn), lambda j, i: (i, j)), out_shape=jax.ShapeDtypeStruct((T, FFN), F32))


def mm_dwdown(act, dx2):
    tm, tn = DOWN_SHARD, 512
    return _matmul(
        "mm_dwdown", act, dx2, grid=(D // tn, FFN // tm), ta=True,
        a_spec=pl.BlockSpec((T, tm), lambda j, i: (0, i)), b_spec=pl.BlockSpec((T, tn), lambda j, i: (0, j)),
        o_spec=pl.BlockSpec((tm, tn), lambda j, i: (i, j)), out_shape=jax.ShapeDtypeStruct((FFN, D), BF16))


_FF_TN = 1408


def _swiglu_fn(gt, up):
    return _silu(gt) * up


def swiglu_fwd(gu):
    nj = FFN // _FF_TN
    gt = pl.BlockSpec((_ROWS, _FF_TN), lambda i, j: (i, j))
    up = pl.BlockSpec((_ROWS, _FF_TN), lambda i, j: (i, nj + j))

    def body(gt_ref, up_ref, o_ref):
        o_ref[...] = _swiglu_fn(gt_ref[...], up_ref[...]).astype(BF16)

    return pl.pallas_call(
        body, name="swiglu_fwd", grid=(T // _ROWS, nj), in_specs=[gt, up], out_specs=gt,
        out_shape=jax.ShapeDtypeStruct((T, FFN), BF16), compiler_params=_cparams(("arbitrary", "arbitrary")),
    )(gu, gu)


def swiglu_bwd(gu, dact):
    nj = FFN // _FF_TN
    gt = pl.BlockSpec((_ROWS, _FF_TN), lambda i, j: (i, j))
    up = pl.BlockSpec((_ROWS, _FF_TN), lambda i, j: (i, nj + j))

    def body(gt_ref, up_ref, d_ref, dgt_ref, dup_ref):
        _, vjp = jax.vjp(_swiglu_fn, gt_ref[...], up_ref[...])
        dgt, dup = vjp(d_ref[...])
        dgt_ref[...] = dgt.astype(BF16)
        dup_ref[...] = dup.astype(BF16)

    dgt, dup = pl.pallas_call(
        body, name="swiglu_bwd", grid=(T // _ROWS, nj), in_specs=[gt, up, gt], out_specs=[gt, gt],
        out_shape=[jax.ShapeDtypeStruct((T, FFN), BF16)] * 2, compiler_params=_cparams(("arbitrary", "arbitrary")),
    )(gu, gu, dact)
    return jnp.concatenate([dgt, dup], axis=1)


def loss_and_grad(y, target):
    blk = pl.BlockSpec((_ROWS, D), lambda i: (i, 0))
    acc = pl.BlockSpec((8, HD), lambda i: (0, 0))

    def body(y_ref, t_ref, dy_ref, l_ref):
        err = y_ref[...] - t_ref[...]
        dy_ref[...] = err * (1.0 / D)

        @pl.when(pl.program_id(0) == 0)
        def _():
            l_ref[...] = jnp.zeros_like(l_ref)

        l_ref[...] += (0.5 / D) * jnp.sum(err * err)

    return pl.pallas_call(
        body, name="loss_and_grad", grid=(T // _ROWS,), in_specs=[blk, blk], out_specs=[blk, acc],
        out_shape=[jax.ShapeDtypeStruct((T, D), F32), jax.ShapeDtypeStruct((8, HD), F32)],
        compiler_params=_cparams(("arbitrary",)),
    )(y, target)


def adamw(w, g, m, v, name):
    rows, cols = w.shape
    tr = _ROWS if rows % _ROWS == 0 else rows
    blk = pl.BlockSpec((tr, cols), lambda i: (i, 0))

    def body(w_ref, g_ref, m_ref, v_ref, d_ref, nm_ref, nv_ref):
        gg = g_ref[...]
        nm = ADAM_B1 * m_ref[...] + (1.0 - ADAM_B1) * gg
        nv = ADAM_B2 * v_ref[...] + (1.0 - ADAM_B2) * (gg * gg)
        m_hat = nm / (1.0 - ADAM_B1 ** ADAM_STEP)
        v_hat = nv / (1.0 - ADAM_B2 ** ADAM_STEP)
        d_ref[...] = -ADAM_LR * (m_hat / (jnp.sqrt(v_hat) + ADAM_EPS) + ADAM_WD * w_ref[...])
        nm_ref[...] = nm
        nv_ref[...] = nv

    return pl.pallas_call(
        body, name=name, grid=(rows // tr,), in_specs=[blk] * 4, out_specs=[blk] * 3,
        out_shape=[jax.ShapeDtypeStruct(w.shape, F32)] * 3, compiler_params=_cparams(("arbitrary",)),
    )(w, g, m, v)


def _proj_to_p2(proj):
    valid = jnp.concatenate([proj[:, s * IN_SHARD_PAD:s * IN_SHARD_PAD + IN_SHARD] for s in range(N_CHIPS)], axis=1)
    pad = jnp.zeros((T, P_CQ - GATE_COLS), F32)
    return jnp.concatenate([valid[:, :GATE_COLS], pad, valid[:, GATE_COLS:]], axis=1)


def _dp2_to_dproj(dp2):
    dvalid = jnp.concatenate([dp2[:, :GATE_COLS], dp2[:, P_CQ:]], axis=1)
    z = jnp.zeros((T, IN_SHARD_PAD - IN_SHARD), F32)
    parts = []
    for s in range(N_CHIPS):
        parts += [dvalid[:, s * IN_SHARD:(s + 1) * IN_SHARD], z]
    return jnp.concatenate(parts, axis=1).astype(BF16)


def layer_forward(x, l, wg, sp):
    h1 = rmsnorm_fwd(x, sp["norm1_g"])
    p2 = _proj_to_p2(mm_proj(h1, wg["in"], l))
    y_a = sgu_fwd(p2, sp["sgu_norm_g"], sp["w_spatial"], sp["b_spatial"])
    y_b, saved_b = gdn_forward(p2, sp["conv_w"], sp["a_log"], sp["dt_bias"], sp["o_norm_g"])
    y_c, saved_c = dattn_forward(p2, sp["q_norm_g"], sp["k_norm_g"])
    mix = jnp.concatenate([y_a, y_b, y_c], axis=1)
    x1 = mm_out(mix, wg["out"], l, x)
    h2 = rmsnorm_fwd(x1, sp["norm2_g"])
    gu = mm_gu(h2, wg["gu"], l)
    act = swiglu_fwd(gu)
    x2 = mm_down(act, wg["down"], l, x1)
    return x2, (x, h1, p2, saved_b, saved_c, mix, x1, h2, gu, act)


def layer_backward(dx2, l, wg, sp, saved):
    x, h1, p2, saved_b, saved_c, mix, x1, h2, gu, act = saved
    dact = mm_dact(dx2, wg["down"], l)
    dw_down = mm_dwdown(act, dx2)
    dgu = swiglu_bwd(gu, dact)
    dw_gu = mm_dwgu(h2, dgu)
    dh2 = mm_dh2(dgu, wg["gu"], l)
    dx1, dnorm2 = rmsnorm_bwd(x1, sp["norm2_g"], dh2, dx2)
    dmix = mm_dmix(dx1, wg["out"], l)
    dw_out = mm_dwout(mix, dx1)
    du, dv, dsg, dws, dbs = sgu_bwd(p2, sp["sgu_norm_g"], sp["w_spatial"], sp["b_spatial"], dmix)
    dseg_b, dconv, dal, ddtb, dog = gdn_backward(p2, sp["conv_w"], sp["a_log"], sp["dt_bias"], sp["o_norm_g"], saved_b, dmix)
    dqs, dks, dvs, dqgs, dkgs = dattn_backward(sp["q_norm_g"], sp["k_norm_g"], saved_c, dmix)
    dcq, dck, dcv, dqg, dkg = sum_branches(dqs + dks + dvs, dqgs + dkgs)
    dp2 = jnp.concatenate([du, dv, dseg_b, dcq, dck, dcv], axis=1)
    dproj = _dp2_to_dproj(dp2)
    dw_in = mm_dwin(h1, dproj)
    dh1 = mm_dh1(dproj, wg["in"], l)
    dx, dnorm1 = rmsnorm_bwd(x, sp["norm1_g"], dh1, dx1)
    big = {"in": dw_in, "out": dw_out, "gu": dw_gu, "down": dw_down}
    small = {"norm1_g": dnorm1, "sgu_norm_g": dsg, "w_spatial": dws, "b_spatial": dbs, "conv_w": dconv, "a_log": dal,
             "dt_bias": ddtb, "o_norm_g": dog, "q_norm_g": dqg, "k_norm_g": dkg, "norm2_g": dnorm2}
    return dx, big, small


_HBM = pl.BlockSpec(memory_space=pltpu.HBM)
_MESH = pl.DeviceIdType.MESH


def _place():
    x, y, c = lax.axis_index("x"), lax.axis_index("y"), lax.axis_index("c")
    chips = [(1 - x, y), (x, 1 - y), (1 - x, 1 - y)]
    return x, y, c, chips


def _rcopy(src, dst, ssem, rsem, dev):
    return pltpu.make_async_remote_copy(src_ref=src, dst_ref=dst, send_sem=ssem, recv_sem=rsem, device_id=dev,
                                        device_id_type=_MESH)


def gather_weights(shards):
    n = len(shards)

    def body(*refs):
        ins, outs = refs[:n], refs[n:2 * n]
        s_ici, r_ici, s_d2d, r_d2d, l_sem = refs[2 * n:]
        x, y, c, chips = _place()
        s = 2 * x + y
        sibling = (x, y, 1 - c)
        local = [pltpu.make_async_copy(ins[a].at[l], outs[a].at[l, s], l_sem.at[a, l]) for a in range(n) for l in range(DEPTH)]
        for cp in local:
            cp.start()
        first = [[_rcopy(ins[a].at[c], outs[a].at[c, s], s_ici.at[a, j], r_ici.at[a, j], (*chip, c))
                  for j, chip in enumerate(chips)] for a in range(n)]
        for a in range(n):
            for cp in first[a]:
                cp.start()
        passed = []
        for a in range(n):
            for j, (cx, cy) in enumerate(chips):
                t = 2 * cx + cy
                _rcopy(ins[a].at[c], outs[a].at[c, t], s_ici.at[a, j], r_ici.at[a, j], (cx, cy, c)).wait_recv()
                cp = _rcopy(outs[a].at[c, t], outs[a].at[c, t], s_d2d.at[a, j], r_d2d.at[a, j], sibling)
                cp.start()
                passed.append(cp)
        for a in range(n):
            for j, (cx, cy) in enumerate(chips):
                t = 2 * cx + cy
                _rcopy(ins[a].at[1 - c], outs[a].at[1 - c, t], s_d2d.at[a, j], r_d2d.at[a, j], sibling).wait_recv()
        for a in range(n):
            for cp in first[a]:
                cp.wait_send()
        for cp in passed:
            cp.wait_send()
        for cp in local:
            cp.wait()

    dma = lambda: pltpu.SemaphoreType.DMA((n, 3))
    return pl.pallas_call(
        body, name="gather_weights", in_specs=[_HBM] * n, out_specs=[_HBM] * n,
        out_shape=[jax.ShapeDtypeStruct((DEPTH, N_CHIPS) + w.shape[1:], w.dtype) for w in shards],
        scratch_shapes=[dma(), dma(), dma(), dma(), pltpu.SemaphoreType.DMA((n, DEPTH))],
    )(*shards)


def exchange_halves(grads):
    n = len(grads)

    def body(*refs):
        ins, outs = refs[:n], refs[n:2 * n]
        ssem, rsem = refs[2 * n:]
        x, y, c, _ = _place()
        cps = []
        for a in range(n):
            h = grads[a].shape[1] // 2
            cps.append(_rcopy(ins[a].at[:, pl.ds((1 - c) * h, h)], outs[a], ssem.at[a], rsem.at[a], (x, y, 1 - c)))
        for cp in cps:
            cp.start()
        for cp in cps:
            cp.wait()

    return pl.pallas_call(
        body, name="exchange_halves", in_specs=[_HBM] * n, out_specs=[_HBM] * n,
        out_shape=[jax.ShapeDtypeStruct((g.shape[0], g.shape[1] // 2, g.shape[2]), g.dtype) for g in grads],
        scratch_shapes=[pltpu.SemaphoreType.DMA((n,)), pltpu.SemaphoreType.DMA((n,))],
    )(*grads)


def scatter_to_chips(parts):
    n = len(parts)

    def body(*refs):
        ins, outs = refs[:n], refs[n:2 * n]
        ssem, rsem = refs[2 * n:]
        x, y, c, chips = _place()
        cps = [_rcopy(ins[a].at[2 * cx + cy], outs[a].at[j], ssem.at[a, j], rsem.at[a, j], (cx, cy, c))
               for a in range(n) for j, (cx, cy) in enumerate(chips)]
        for cp in cps:
            cp.start()
        for cp in cps:
            cp.wait()

    return pl.pallas_call(
        body, name="scatter_to_chips", in_specs=[_HBM] * n, out_specs=[_HBM] * n,
        out_shape=[jax.ShapeDtypeStruct((3,) + p.shape[1:], p.dtype) for p in parts],
        scratch_shapes=[pltpu.SemaphoreType.DMA((n, 3)), pltpu.SemaphoreType.DMA((n, 3))],
    )(*parts)


def share_halves(halves, n_weights):
    n = len(halves)

    def body(*refs):
        ins, outs = refs[:n], refs[n:n + n_weights]
        ssem, rsem, lsem = refs[n + n_weights:]
        x, y, c, _ = _place()
        loc, rem = [], []
        for i in range(n):
            l, a = divmod(i, n_weights)
            h = halves[i].shape[0]
            dst = outs[a].at[l, pl.ds(c * h, h)]
            loc.append(pltpu.make_async_copy(ins[i], dst, lsem.at[i]))
            rem.append(_rcopy(ins[i], dst, ssem.at[i], rsem.at[i], (x, y, 1 - c)))
        for cp in loc + rem:
            cp.start()
        for i in range(n):
            l, a = divmod(i, n_weights)
            h = halves[i].shape[0]
            rem[i].wait_send()
            _rcopy(ins[i], outs[a].at[l, pl.ds((1 - c) * h, h)], ssem.at[i], rsem.at[i], (x, y, 1 - c)).wait_recv()
            loc[i].wait()

    return pl.pallas_call(
        body, name="share_halves", in_specs=[_HBM] * n, out_specs=[_HBM] * n_weights,
        out_shape=[jax.ShapeDtypeStruct((DEPTH, 2 * halves[a].shape[0], halves[a].shape[1]), halves[a].dtype) for a in range(n_weights)],
        scratch_shapes=[pltpu.SemaphoreType.DMA((n,)), pltpu.SemaphoreType.DMA((n,)), pltpu.SemaphoreType.DMA((n,))],
    )(*halves)


def _half_rows(h, cols):
    for tr in (512, 256, 352, 128, 64):
        if h % tr == 0 and tr * cols * 4 <= 6 * 1024 * 1024:
            return tr
    raise ValueError((h, cols))


def add_sibling(grad, recv, c):
    _, r, cols = grad.shape
    h = r // 2
    tr = _half_rows(h, cols)
    nb = h // tr

    def body(c_ref, g_ref, r_ref, o_ref):
        o_ref[...] = (g_ref[...].astype(F32) + r_ref[...].astype(F32)).astype(BF16)

    return pl.pallas_call(
        body, name="add_sibling",
        grid_spec=pltpu.PrefetchScalarGridSpec(
            num_scalar_prefetch=1, grid=(N_CHIPS, nb),
            in_specs=[pl.BlockSpec((None, tr, cols), lambda t, i, c_ref: (t, c_ref[0] * nb + i, 0)),
                      pl.BlockSpec((None, tr, cols), lambda t, i, c_ref: (t, i, 0))],
            out_specs=pl.BlockSpec((None, tr, cols), lambda t, i, c_ref: (t, i, 0))),
        out_shape=jax.ShapeDtypeStruct((N_CHIPS, h, cols), BF16), compiler_params=_cparams(("arbitrary", "arbitrary")),
    )(c, grad, recv)


def add_chips(part, recv, s):
    _, h, cols = part.shape
    tr = _half_rows(h, cols)

    def body(s_ref, p_ref, r_ref, o_ref):
        o_ref[...] = ((p_ref[...].astype(F32) + r_ref[0].astype(F32)) + r_ref[1].astype(F32)) + r_ref[2].astype(F32)

    return pl.pallas_call(
        body, name="add_chips",
        grid_spec=pltpu.PrefetchScalarGridSpec(
            num_scalar_prefetch=1, grid=(h // tr,),
            in_specs=[pl.BlockSpec((None, tr, cols), lambda i, s_ref: (s_ref[0], i, 0)),
                      pl.BlockSpec((3, tr, cols), lambda i, s_ref: (0, i, 0))],
            out_specs=pl.BlockSpec((tr, cols), lambda i, s_ref: (i, 0))),
        out_shape=jax.ShapeDtypeStruct((h, cols), F32), compiler_params=_cparams(("arbitrary",)),
    )(s, part, recv)


def allreduce_small(vec):
    rows = vec.shape[0]

    def body(v_ref, o_ref, buf, ssem, rsem, lsem):
        x, y, c, chips = _place()
        me, sibling = (x, y, c), (x, y, 1 - c)

        def blk(px, py, pc):
            return buf.at[4 * px + 2 * py + pc]

        def copy(k, block, to, src=None):
            return _rcopy(blk(*block) if src is None else src, blk(*block), ssem.at[k], rsem.at[k], to)

        mine = pltpu.make_async_copy(v_ref, blk(*me), lsem)
        mine.start()
        first = [copy(0, me, sibling, src=v_ref)] + [copy(1 + j, me, (*chip, c), src=v_ref) for j, chip in enumerate(chips)]
        for cp in first:
            cp.start()
        passed = [copy(4 + j, (*chip, c), sibling) for j, chip in enumerate(chips)]
        for j, chip in enumerate(chips):
            copy(1 + j, (*chip, c), me).wait_recv()
            passed[j].start()
        copy(0, sibling, me).wait_recv()
        for j, chip in enumerate(chips):
            copy(4 + j, (*chip, 1 - c), me).wait_recv()
        for cp in first + passed:
            cp.wait_send()
        mine.wait()
        acc = buf[0]
        for d in range(1, N_DEV):
            acc = acc + buf[d]
        o_ref[...] = acc

    vm = pl.BlockSpec(memory_space=pltpu.VMEM)
    return pl.pallas_call(
        body, name="allreduce_small", in_specs=[vm], out_specs=vm, out_shape=jax.ShapeDtypeStruct(vec.shape, F32),
        scratch_shapes=[pltpu.VMEM((N_DEV, rows, HD), F32), pltpu.SemaphoreType.DMA((7,)), pltpu.SemaphoreType.DMA((7,)),
                        pltpu.SemaphoreType.DMA],
        compiler_params=pltpu.CompilerParams(vmem_limit_bytes=VMEM_LIMIT),
    )(vec)


SMALL_NAMES = ("norm1_g", "sgu_norm_g", "w_spatial", "b_spatial", "conv_w", "a_log", "dt_bias", "o_norm_g", "q_norm_g",
               "k_norm_g", "norm2_g")


def small_params(l, p, conv_full):
    return {"norm1_g": p["norm1_g"][l][None], "sgu_norm_g": p["sgu_norm_g"][l][:, None, :], "w_spatial": p["w_spatial"][l],
            "b_spatial": p["b_spatial"][l][..., None], "conv_w": conv_full[l], "a_log": p["a_log"][l], "dt_bias": p["dt_bias"][l],
            "o_norm_g": p["o_norm_g"][l][None], "q_norm_g": p["q_norm_g"][l][None], "k_norm_g": p["k_norm_g"][l][None],
            "norm2_g": p["norm2_g"][l][None]}


def local_step(x, target, wg, sps):
    saved = []
    for l in range(DEPTH):
        x, s = layer_forward(x, l, wg, sps[l])
        saved.append(s)
    dx, loss = loss_and_grad(x, target)
    bigs, smalls = [None] * DEPTH, [None] * DEPTH
    for l in reversed(range(DEPTH)):
        dx, bigs[l], smalls[l] = layer_backward(dx, l, wg, sps[l], saved[l])
    return loss, dx, bigs, smalls


_PACK_TILE = 8 * HD


def _pack(arrays):
    flat = jnp.concatenate([a.reshape(-1) for a in arrays])
    pad = -flat.shape[0] % _PACK_TILE
    return jnp.pad(flat, (0, pad)).reshape(-1, HD)


def _unpack(packed, shapes):
    flat, out, off = packed.reshape(-1), [], 0
    for shp in shapes:
        n = int(np.prod(shp))
        out.append(flat[off:off + n].reshape(shp))
        off += n
    return out


BIG_NAMES = ("in", "out", "gu", "down")
WEIGHT_ORDER = ("norm1_g", "w_in", "sgu_norm_g", "w_spatial", "b_spatial", "conv_w", "a_log", "dt_bias", "o_norm_g", "q_norm_g",
                "k_norm_g", "w_out", "norm2_g", "w_gate_up", "w_down")


def kernel(x, norm1_g, w_in, sgu_norm_g, w_spatial, b_spatial, conv_w, a_log, dt_bias, o_norm_g, q_norm_g, k_norm_g, w_out, norm2_g, w_gate_up, w_down, loss_target, m_norm1_g, m_w_in, m_sgu_norm_g, m_w_spatial, m_b_spatial, m_conv_w, m_a_log, m_dt_bias, m_o_norm_g, m_q_norm_g, m_k_norm_g, m_w_out, m_norm2_g, m_w_gate_up, m_w_down, v_norm1_g, v_w_in, v_sgu_norm_g, v_w_spatial, v_b_spatial, v_conv_w, v_a_log, v_dt_bias, v_o_norm_g, v_q_norm_g, v_k_norm_g, v_w_out, v_norm2_g, v_w_gate_up, v_w_down):
    w = dict(norm1_g=norm1_g, w_in=w_in, sgu_norm_g=sgu_norm_g, w_spatial=w_spatial, b_spatial=b_spatial, conv_w=conv_w,
             a_log=a_log, dt_bias=dt_bias, o_norm_g=o_norm_g, q_norm_g=q_norm_g, k_norm_g=k_norm_g, w_out=w_out,
             norm2_g=norm2_g, w_gate_up=w_gate_up, w_down=w_down)
    m = dict(norm1_g=m_norm1_g, w_in=m_w_in, sgu_norm_g=m_sgu_norm_g, w_spatial=m_w_spatial, b_spatial=m_b_spatial,
             conv_w=m_conv_w, a_log=m_a_log, dt_bias=m_dt_bias, o_norm_g=m_o_norm_g, q_norm_g=m_q_norm_g, k_norm_g=m_k_norm_g,
             w_out=m_w_out, norm2_g=m_norm2_g, w_gate_up=m_w_gate_up, w_down=m_w_down)
    v = dict(norm1_g=v_norm1_g, w_in=v_w_in, sgu_norm_g=v_sgu_norm_g, w_spatial=v_w_spatial, b_spatial=v_b_spatial,
             conv_w=v_conv_w, a_log=v_a_log, dt_bias=v_dt_bias, o_norm_g=v_o_norm_g, q_norm_g=v_q_norm_g, k_norm_g=v_k_norm_g,
             w_out=v_w_out, norm2_g=v_norm2_g, w_gate_up=v_w_gate_up, w_down=v_w_down)
    chip = (2 * lax.axis_index("x") + lax.axis_index("y")).astype(jnp.int32)
    core = lax.axis_index("c").astype(jnp.int32)

    in_pad = IN_SHARD_PAD - IN_SHARD
    shards = [jnp.pad(w_in, ((0, 0), (0, 0), (0, in_pad))).astype(BF16), w_out.astype(BF16), w_gate_up.astype(BF16),
              w_down.astype(BF16), conv_w]
    g_in, g_out, g_gu, g_down, g_conv = gather_weights(shards)
    wg = {"in": g_in, "out": g_out.reshape(DEPTH, D, D), "gu": g_gu, "down": g_down.reshape(DEPTH, FFN, D)}
    conv_full = g_conv.transpose(0, 2, 1, 3).reshape(DEPTH, B_CONV, 3 * B_WIDTH)
    sps = [small_params(l, w, conv_full) for l in range(DEPTH)]

    loss_tile, dx, bigs, smalls = local_step(x[0], loss_target[0], wg, sps)

    shard_rows = {"in": D, "out": OUT_SHARD, "gu": D, "down": DOWN_SHARD}
    grads = [bigs[l][k].reshape(N_CHIPS, shard_rows[k], -1) for l in range(DEPTH) for k in BIG_NAMES]
    from_sibling = exchange_halves(grads)
    parts = [add_sibling(g, r, core.reshape(1)) for g, r in zip(grads, from_sibling)]
    from_chips = scatter_to_chips(parts)
    halves = [add_chips(p, r, chip.reshape(1)) for p, r in zip(parts, from_chips)]
    full = dict(zip(BIG_NAMES, share_halves(halves, len(BIG_NAMES))))
    grad = {"w_in": full["in"][:, :, :IN_SHARD], "w_out": full["out"], "w_gate_up": full["gu"], "w_down": full["down"]}

    stacked = [jnp.stack([smalls[l][n] for l in range(DEPTH)]) for n in SMALL_NAMES]
    total = allreduce_small(_pack(stacked + [loss_tile[0, :1]]))
    shapes = [(DEPTH, B_CONV, 3 * B_WIDTH) if n == "conv_w" else w[n].shape for n in SMALL_NAMES]
    small_grads = dict(zip(SMALL_NAMES, _unpack(total, shapes + [(1,)])[:-1]))
    loss = _unpack(total, shapes + [(1,)])[-1][0]
    conv_cols = conv_w.shape[-1]
    small_grads["conv_w"] = lax.dynamic_slice_in_dim(small_grads["conv_w"], chip * conv_cols, conv_cols, axis=2)
    grad.update(small_grads)

    delta, new_m, new_v = {}, {}, {}
    for n in ("w_in", "w_out", "w_gate_up", "w_down"):
        two_d = lambda a: a.reshape(-1, a.shape[-1])
        d_, m_, v_ = adamw(two_d(w[n]), two_d(grad[n]), two_d(m[n]), two_d(v[n]), "adamw_" + n)
        delta[n], new_m[n], new_v[n] = (t.reshape(w[n].shape) for t in (d_, m_, v_))
    sshapes = [w[n].shape for n in SMALL_NAMES]
    packed = [_pack([d[n] for n in SMALL_NAMES]) for d in (w, grad, m, v)]
    for dst, t in zip((delta, new_m, new_v), adamw(*packed, "adamw_small")):
        dst.update(zip(SMALL_NAMES, _unpack(t, sshapes)))

    out = [loss, dx[None]]
    for d in (grad, delta, new_m, new_v):
        out += [d[n] for n in WEIGHT_ORDER]
    return tuple(out)
```

```python
import functools
import math

import numpy as np
import jax
import jax.numpy as jnp
from jax import lax
from jax.experimental import pallas as pl
from jax.experimental.pallas import tpu as pltpu

F32 = jnp.float32
BF16 = jnp.bfloat16
HI = lax.Precision.HIGHEST

T = 2048
D = 2048
DEPTH = 2
HD = 128
A_GROUPS, A_WIDTH, A_CHUNK = 4, 512, 128
B_HEADS, B_WIDTH, B_CONV, B_CHUNK = 6, 768, 4, 64
C_HEADS, C_WIDTH, C_BLOCK = 6, 768, 128
C_BRANCHES = ((128, 1), (512, 4), (2048, 16))
FFN = 5632
IN_TOTAL = 6412
EPS = 1e-6
N_CHIPS = 4
N_DEV = 8
IN_SHARD = IN_TOTAL // N_CHIPS
IN_SHARD_PAD = 1664
GU_SHARD = 2 * FFN // N_CHIPS
OUT_SHARD = D // N_CHIPS
DOWN_SHARD = FFN // N_CHIPS
P_AU, P_AV, P_BQ, P_BK, P_BV, P_BG, P_BB, P_CQ, P_CK, P_CV, P_END = (
    0, 512, 1024, 1792, 2560, 3328, 4096, 4224, 4992, 5760, 6528)
GATE_COLS = 4108
VMEM_LIMIT = 56 * 1024 * 1024

ADAM_LR, ADAM_B1, ADAM_B2, ADAM_EPS, ADAM_WD, ADAM_STEP = 0.001, 0.9, 0.999, 1e-08, 0.01, 10


def _cparams(sem, vmem=VMEM_LIMIT):
    return pltpu.CompilerParams(dimension_semantics=sem, vmem_limit_bytes=vmem)


def _dims(nd, ta, tb):
    off = nd - 2
    ca = off + (0 if ta else 1)
    cb = off + (1 if tb else 0)
    batch = ((0,), (0,)) if nd == 3 else ((), ())
    return (((ca,), (cb,)), batch)


def _raw_mm(a, b, ta, tb, hi):
    if hi:
        return lax.dot_general(a, b, _dims(a.ndim, ta, tb), precision=HI, preferred_element_type=F32)
    return lax.dot_general(a.astype(BF16), b.astype(BF16), _dims(a.ndim, ta, tb), preferred_element_type=F32)


@functools.partial(jax.custom_vjp, nondiff_argnums=(2, 3, 4))
def _mm(a, b, ta=False, tb=False, hi=False):
    return _raw_mm(a, b, ta, tb, hi)


def _mm_fwd(a, b, ta, tb, hi):
    return _raw_mm(a, b, ta, tb, hi), (a, b)


def _mm_bwd(ta, tb, hi, res, g):
    a, b = res
    da = _raw_mm(g, b, False, not tb, hi) if not ta else _raw_mm(b, g, tb, True, hi)
    db = _raw_mm(a, g, not ta, False, hi) if not tb else _raw_mm(g, a, True, ta, hi)
    return da.astype(a.dtype), db.astype(b.dtype)


_mm.defvjp(_mm_fwd, _mm_bwd)


def _rms(x, g):
    return x * lax.rsqrt(jnp.mean(x * x, axis=-1, keepdims=True) + EPS) * g


def _gelu(x):
    return 0.5 * x * (1.0 + jnp.tanh(math.sqrt(2.0 / math.pi) * (x + 0.044715 * (x * x * x))))


def _sigmoid(x):
    return 1.0 / (1.0 + jnp.exp(-x))


def _silu(x):
    return x * _sigmoid(x)


def _softplus(x):
    return jnp.maximum(x, 0.0) + jnp.log(1.0 + jnp.exp(-jnp.abs(x)))


def _iota(shape, dim):
    return lax.broadcasted_iota(jnp.int32, shape, dim)


def _sgu_fn(u, v, sg, w, b):
    nc = T // A_CHUNK
    ug = _gelu(u)
    vn = _rms(_gelu(v), sg)
    causal = _iota((A_CHUNK, A_CHUNK), 0) >= _iota((A_CHUNK, A_CHUNK), 1)
    wm = jnp.where(causal, w, 0.0)
    wb = jnp.broadcast_to(wm[None], (nc, A_CHUNK, A_CHUNK))
    z = _mm(wb, vn.reshape(nc, A_CHUNK, HD)) + b[None]
    return ug * z.reshape(T, HD)


def _sgu_specs():
    col = lambda off: pl.BlockSpec((T, HD), lambda g, off=off: (0, off + g))
    par = [pl.BlockSpec((None, 1, HD), lambda g: (g, 0, 0)),
           pl.BlockSpec((None, A_CHUNK, A_CHUNK), lambda g: (g, 0, 0)),
           pl.BlockSpec((None, A_CHUNK, 1), lambda g: (g, 0, 0))]
    return col, par


def sgu_fwd(p2, sg, w, b):
    col, par = _sgu_specs()

    def body(u_ref, v_ref, sg_ref, w_ref, b_ref, y_ref):
        y_ref[...] = _sgu_fn(u_ref[...], v_ref[...], sg_ref[...], w_ref[...], b_ref[...])

    return pl.pallas_call(
        body, name="sgu_fwd", grid=(A_GROUPS,),
        in_specs=[col(P_AU // HD), col(P_AV // HD)] + par,
        out_specs=pl.BlockSpec((T, HD), lambda g: (0, g)),
        out_shape=jax.ShapeDtypeStruct((T, A_WIDTH), F32),
        compiler_params=_cparams(("arbitrary",)),
    )(p2, p2, sg, w, b)


def sgu_bwd(p2, sg, w, b, dmix):
    col, par = _sgu_specs()

    def body(u_ref, v_ref, sg_ref, w_ref, b_ref, dy_ref, du_ref, dv_ref, dsg_ref, dw_ref, db_ref):
        _, vjp = jax.vjp(_sgu_fn, u_ref[...], v_ref[...], sg_ref[...], w_ref[...], b_ref[...])
        du, dv, dsg, dw, db = vjp(dy_ref[...])
        du_ref[...] = du
        dv_ref[...] = dv
        dsg_ref[...] = dsg
        dw_ref[...] = dw
        db_ref[...] = db

    gcol = pl.BlockSpec((T, HD), lambda g: (0, g))
    return pl.pallas_call(
        body, name="sgu_bwd", grid=(A_GROUPS,),
        in_specs=[col(P_AU // HD), col(P_AV // HD)] + par + [gcol],
        out_specs=[gcol, gcol] + par,
        out_shape=[jax.ShapeDtypeStruct((T, A_WIDTH), F32), jax.ShapeDtypeStruct((T, A_WIDTH), F32),
                   jax.ShapeDtypeStruct((A_GROUPS, 1, HD), F32), jax.ShapeDtypeStruct((A_GROUPS, A_CHUNK, A_CHUNK), F32),
                   jax.ShapeDtypeStruct((A_GROUPS, A_CHUNK, 1), F32)],
        compiler_params=_cparams(("arbitrary",)),
    )(p2, p2, sg, w, b, dmix)


def _attn_fn(q, k, v, qg, kg, slope, *, dil, nb):
    n = T // C_BLOCK
    qb = _rms(q, qg).reshape(n, C_BLOCK, HD)
    kb = _rms(k, kg).reshape(n, C_BLOCK, HD)
    vb = v.reshape(n, C_BLOCK, HD)
    scale = HD ** -0.5
    qi = _iota((n, C_BLOCK, C_BLOCK), 1)
    kj = _iota((n, C_BLOCK, C_BLOCK), 2)
    sl = slope[None] * float(dil)
    d_cur = qi - kj
    sc = jnp.where(d_cur >= 0, _mm(qb, kb, tb=True) * scale - sl * d_cur.astype(F32), -jnp.inf)
    mx = jnp.max(sc, axis=-1, keepdims=True)
    if nb > 1:
        kp = jnp.concatenate([jnp.zeros((1, C_BLOCK, HD), F32), kb[:-1]], axis=0)
        vp = jnp.concatenate([jnp.zeros((1, C_BLOCK, HD), F32), vb[:-1]], axis=0)
        has_prev = (_iota((n, C_BLOCK, C_BLOCK), 0) % nb) > 0
        d_prev = C_BLOCK + qi - kj
        sp = jnp.where((kj >= qi) & has_prev, _mm(qb, kp, tb=True) * scale - sl * d_prev.astype(F32), -jnp.inf)
        mx = jnp.maximum(mx, jnp.max(sp, axis=-1, keepdims=True))
    p = jnp.exp(sc - mx)
    den = jnp.sum(p, axis=-1, keepdims=True)
    if nb > 1:
        pp = jnp.exp(sp - mx)
        den = den + jnp.sum(pp, axis=-1, keepdims=True)
    out = _mm(p / den, vb)
    if nb > 1:
        out = out + _mm(pp / den, vp)
    lse = mx + jnp.log(den)
    return out.reshape(T, HD), jnp.broadcast_to(lse, (n, C_BLOCK, HD)).reshape(T, HD)


def _attn_specs():
    hcol = pl.BlockSpec((T, HD), lambda h: (0, h))
    row = pl.BlockSpec((1, HD), lambda h: (0, 0))
    slope = pl.BlockSpec((None, 1, HD), lambda h: (h, 0, 0))
    return hcol, row, slope


def attn_fwd(q, k, v, qg, kg, slopes, dil, nb):
    hcol, row, slope = _attn_specs()

    def body(q_ref, k_ref, v_ref, qg_ref, kg_ref, s_ref, o_ref, l_ref):
        o, l = _attn_fn(q_ref[...], k_ref[...], v_ref[...], qg_ref[...], kg_ref[...], s_ref[...], dil=dil, nb=nb)
        o_ref[...] = o
        l_ref[...] = l

    return pl.pallas_call(
        body, name=f"attn_fwd_d{dil}", grid=(C_HEADS,),
        in_specs=[hcol, hcol, hcol, row, row, slope], out_specs=[hcol, hcol],
        out_shape=[jax.ShapeDtypeStruct((T, C_WIDTH), F32)] * 2,
        compiler_params=_cparams(("arbitrary",)),
    )(q, k, v, qg, kg, slopes)


def attn_bwd(q, k, v, qg, kg, slopes, do, dl, dil, nb):
    hcol, row, slope = _attn_specs()

    def body(q_ref, k_ref, v_ref, qg_ref, kg_ref, s_ref, do_ref, dl_ref, dq_ref, dk_ref, dv_ref, dqg_ref, dkg_ref):
        fn = functools.partial(_attn_fn, dil=dil, nb=nb)
        _, vjp = jax.vjp(lambda a, b, c, d, e: fn(a, b, c, d, e, s_ref[...]),
                         q_ref[...], k_ref[...], v_ref[...], qg_ref[...], kg_ref[...])
        dq, dk, dv, dqg, dkg = vjp((do_ref[...], dl_ref[...]))
        dq_ref[...] = dq
        dk_ref[...] = dk
        dv_ref[...] = dv

        @pl.when(pl.program_id(0) == 0)
        def _():
            dqg_ref[...] = jnp.zeros_like(dqg_ref)
            dkg_ref[...] = jnp.zeros_like(dkg_ref)

        dqg_ref[...] += dqg
        dkg_ref[...] += dkg

    return pl.pallas_call(
        body, name=f"attn_bwd_d{dil}", grid=(C_HEADS,),
        in_specs=[hcol, hcol, hcol, row, row, slope, hcol, hcol], out_specs=[hcol, hcol, hcol, row, row],
        out_shape=[jax.ShapeDtypeStruct((T, C_WIDTH), F32)] * 3 + [jax.ShapeDtypeStruct((1, HD), F32)] * 2,
        compiler_params=_cparams(("arbitrary",)),
    )(q, k, v, qg, kg, slopes, do, dl)


def _combine_fn(o1, o2, o3, l1, l2, l3):
    mx = jnp.maximum(jnp.maximum(l1, l2), l3)
    e1, e2, e3 = jnp.exp(l1 - mx), jnp.exp(l2 - mx), jnp.exp(l3 - mx)
    s = e1 + e2 + e3
    return (e1 / s) * o1 + (e2 / s) * o2 + (e3 / s) * o3


_CMB_ROWS = 512


def combine_fwd(outs, lses):
    blk = pl.BlockSpec((_CMB_ROWS, HD), lambda i, h: (i, h))

    def body(o1, o2, o3, l1, l2, l3, y_ref):
        y_ref[...] = _combine_fn(o1[...], o2[...], o3[...], l1[...], l2[...], l3[...])

    return pl.pallas_call(
        body, name="combine_fwd", grid=(T // _CMB_ROWS, C_HEADS), in_specs=[blk] * 6, out_specs=blk,
        out_shape=jax.ShapeDtypeStruct((T, C_WIDTH), F32), compiler_params=_cparams(("arbitrary", "arbitrary")),
    )(*outs, *lses)


def combine_bwd(outs, lses, dmix):
    blk = pl.BlockSpec((_CMB_ROWS, HD), lambda i, h: (i, h))
    dblk = pl.BlockSpec((_CMB_ROWS, HD), lambda i, h: (i, (A_WIDTH + B_WIDTH) // HD + h))

    def body(o1, o2, o3, l1, l2, l3, dy_ref, *outs_ref):
        _, vjp = jax.vjp(_combine_fn, o1[...], o2[...], o3[...], l1[...], l2[...], l3[...])
        for r, g in zip(outs_ref, vjp(dy_ref[...])):
            r[...] = g

    return pl.pallas_call(
        body, name="combine_bwd", grid=(T // _CMB_ROWS, C_HEADS), in_specs=[blk] * 6 + [dblk], out_specs=[blk] * 6,
        out_shape=[jax.ShapeDtypeStruct((T, C_WIDTH), F32)] * 6, compiler_params=_cparams(("arbitrary", "arbitrary")),
    )(*outs, *lses, dmix)


_NCH = T // B_CHUNK


def _conv_taps(x, w_ref):
    rows = _iota(x.shape, 0)
    taps = []
    for j in range(B_CONV):
        s = B_CONV - 1 - j
        taps.append(x if s == 0 else jnp.where(rows >= s, pltpu.roll(x, s, 0), 0.0))
    pre = sum(w_ref[j:j + 1, :] * taps[j] for j in range(B_CONV))
    return pre, taps


def _conv_post(pre, mode):
    y = _silu(pre)
    if mode == "v":
        return y
    y = y * lax.rsqrt(jnp.sum(y * y, axis=-1, keepdims=True) + EPS)
    return y * (HD ** -0.5) if mode == "q" else y


def conv_fwd(p2, conv_w, mode):
    idx = "qkv".index(mode)
    xcol = pl.BlockSpec((T, HD), lambda h: (0, P_BQ // HD + B_HEADS * idx + h))
    wcol = pl.BlockSpec((B_CONV, HD), lambda h: (0, B_HEADS * idx + h))
    hcol = pl.BlockSpec((T, HD), lambda h: (0, h))

    def body(x_ref, w_ref, y_ref):
        pre, _ = _conv_taps(x_ref[...], w_ref)
        y_ref[...] = _conv_post(pre, mode)

    return pl.pallas_call(
        body, name=f"conv_fwd_{mode}", grid=(B_HEADS,), in_specs=[xcol, wcol], out_specs=hcol,
        out_shape=jax.ShapeDtypeStruct((T, B_WIDTH), F32), compiler_params=_cparams(("arbitrary",)),
    )(p2, conv_w)


def conv_bwd(p2, conv_w, dys, mode):
    idx = "qkv".index(mode)
    xcol = pl.BlockSpec((T, HD), lambda h: (0, P_BQ // HD + B_HEADS * idx + h))
    wcol = pl.BlockSpec((B_CONV, HD), lambda h: (0, B_HEADS * idx + h))
    hcol = pl.BlockSpec((T, HD), lambda h: (0, h))
    wout = pl.BlockSpec((B_CONV, HD), lambda h: (0, h))

    def body(x_ref, w_ref, *rest):
        dy_refs, (dx_ref, dw_ref) = rest[:-2], rest[-2:]
        pre, taps = _conv_taps(x_ref[...], w_ref)
        _, vjp = jax.vjp(functools.partial(_conv_post, mode=mode), pre)
        (dpre,) = vjp(sum(r[...] for r in dy_refs))
        rows = _iota(dpre.shape, 0)
        dx = w_ref[B_CONV - 1:B_CONV, :] * dpre
        for j in range(B_CONV):
            s = B_CONV - 1 - j
            dw_ref[j:j + 1, :] = jnp.sum(dpre * taps[j], axis=0, keepdims=True)
            if s > 0:
                dx = dx + w_ref[j:j + 1, :] * jnp.where(rows < T - s, pltpu.roll(dpre, T - s, 0), 0.0)
        dx_ref[...] = dx

    return pl.pallas_call(
        body, name=f"conv_bwd_{mode}", grid=(B_HEADS,), in_specs=[xcol, wcol] + [hcol] * len(dys), out_specs=[hcol, wout],
        out_shape=[jax.ShapeDtypeStruct((T, B_WIDTH), F32), jax.ShapeDtypeStruct((B_CONV, B_WIDTH), F32)],
        compiler_params=_cparams(("arbitrary",)),
    )(p2, conv_w, *dys)


def _gates_fn(bg, al, dtb, h):
    r = _iota((HD, HD), 0)
    logit = _mm(bg, (r == h).astype(F32), hi=True)
    a = _mm(bg, (r == h + B_HEADS).astype(F32), hi=True)
    beta = _sigmoid(logit)
    graw = -jnp.exp(al) * _softplus(a + dtb)
    tri = (_iota((_NCH, B_CHUNK, B_CHUNK), 1) >= _iota((_NCH, B_CHUNK, B_CHUNK), 2)).astype(F32)
    g = _mm(tri, graw.reshape(_NCH, B_CHUNK, HD), hi=True).reshape(T, HD)
    return beta, g


def _gates_specs():
    bg = pl.BlockSpec((T, HD), lambda h: (0, P_BB // HD))
    par = pl.BlockSpec((None, 1, HD), lambda h: (h, 0, 0))
    out = pl.BlockSpec((None, T, HD), lambda h: (h, 0, 0))
    return bg, par, out


def gates_fwd(p2, al, dtb):
    bg, par, out = _gates_specs()

    def body(bg_ref, al_ref, dtb_ref, beta_ref, g_ref):
        beta, g = _gates_fn(bg_ref[...], al_ref[...], dtb_ref[...], pl.program_id(0))
        beta_ref[...] = beta
        g_ref[...] = g

    return pl.pallas_call(
        body, name="gates_fwd", grid=(B_HEADS,), in_specs=[bg, par, par], out_specs=[out, out],
        out_shape=[jax.ShapeDtypeStruct((B_HEADS, T, HD), F32)] * 2, compiler_params=_cparams(("arbitrary",)),
    )(p2, al, dtb)


def gates_bwd(p2, al, dtb, dbeta, dg1, dg2):
    bg, par, out = _gates_specs()
    acc = pl.BlockSpec((T, HD), lambda h: (0, 0))

    def body(bg_ref, al_ref, dtb_ref, dbeta_ref, dg1_ref, dg2_ref, dbg_ref, dal_ref, ddtb_ref):
        h = pl.program_id(0)
        _, vjp = jax.vjp(lambda a, b, c: _gates_fn(a, b, c, h), bg_ref[...], al_ref[...], dtb_ref[...])
        dbg, dal, ddtb = vjp((dbeta_ref[...], dg1_ref[...] + dg2_ref[...]))

        @pl.when(h == 0)
        def _():
            dbg_ref[...] = jnp.zeros_like(dbg_ref)

        dbg_ref[...] += dbg
        dal_ref[...] = jnp.broadcast_to(jnp.sum(dal, axis=-1, keepdims=True), (1, HD))
        ddtb_ref[...] = jnp.broadcast_to(jnp.sum(ddtb, axis=-1, keepdims=True), (1, HD))

    return pl.pallas_call(
        body, name="gates_bwd", grid=(B_HEADS,), in_specs=[bg, par, par, out, out, out], out_specs=[acc, par, par],
        out_shape=[jax.ShapeDtypeStruct((T, HD), F32)] + [jax.ShapeDtypeStruct((B_HEADS, 1, HD), F32)] * 2,
        compiler_params=_cparams(("arbitrary",)),
    )(p2, al, dtb, dbeta, dg1, dg2)


def _unit_lower_inverse(a):
    eye = (_iota(a.shape, 1) == _iota(a.shape, 2)).astype(F32)
    x = eye - a
    p = _mm(a, a, hi=True)
    for i in range(5):
        x = x + _mm(x, p, hi=True)
        if i < 4:
            p = _mm(p, p, hi=True)
    return x


_WY_CH = 8
_WY_ROWS = _WY_CH * B_CHUNK


def _wy_fn(q, k, v, beta, g):
    sh = (q.shape[0] // B_CHUNK, B_CHUNK, HD)
    q3, k3, v3, b3, g3 = (t.reshape(sh) for t in (q, k, v, beta, g))
    gd = g3[:, :, :B_CHUNK] - jnp.swapaxes(g3, 1, 2)[:, :B_CHUNK, :]
    ii, jj = _iota(gd.shape, 1), _iota(gd.shape, 2)
    decay = jnp.exp(jnp.where(ii >= jj, gd, -jnp.inf))
    kb = k3 * b3
    a = _mm(kb, k3, tb=True) * jnp.where(ii > jj, decay, 0.0)
    tinv = _unit_lower_inverse(a)
    u = _mm(tinv, v3 * b3, hi=True)
    w = _mm(tinv, kb * jnp.exp(g3), hi=True)
    attn = _mm(q3, k3, tb=True) * decay
    return u.reshape(q.shape), w.reshape(q.shape), attn


def _wy_specs():
    hcol = pl.BlockSpec((_WY_ROWS, HD), lambda h, i: (i, h))
    hb = pl.BlockSpec((None, _WY_ROWS, HD), lambda h, i: (h, i, 0))
    at = pl.BlockSpec((None, _WY_CH, B_CHUNK, B_CHUNK), lambda h, i: (h, i, 0, 0))
    return hcol, hb, at


_WY_GRID = (B_HEADS, _NCH // _WY_CH)


def wy_fwd(q, k, v, beta, g):
    hcol, hb, at = _wy_specs()

    def body(q_ref, k_ref, v_ref, b_ref, g_ref, u_ref, w_ref, a_ref):
        u, w, a = _wy_fn(q_ref[...], k_ref[...], v_ref[...], b_ref[...], g_ref[...])
        u_ref[...] = u
        w_ref[...] = w
        a_ref[...] = a

    return pl.pallas_call(
        body, name="wy_fwd", grid=_WY_GRID, in_specs=[hcol, hcol, hcol, hb, hb], out_specs=[hcol, hcol, at],
        out_shape=[jax.ShapeDtypeStruct((T, B_WIDTH), F32)] * 2 + [jax.ShapeDtypeStruct((B_HEADS, _NCH, B_CHUNK, B_CHUNK), F32)],
        compiler_params=_cparams(("arbitrary", "arbitrary")),
    )(q, k, v, beta, g)


def wy_bwd(q, k, v, beta, g, du, dw, dattn):
    hcol, hb, at = _wy_specs()

    def body(q_ref, k_ref, v_ref, b_ref, g_ref, du_ref, dw_ref, da_ref, dq_ref, dk_ref, dv_ref, db_ref, dg_ref):
        _, vjp = jax.vjp(_wy_fn, q_ref[...], k_ref[...], v_ref[...], b_ref[...], g_ref[...])
        for r, t in zip((dq_ref, dk_ref, dv_ref, db_ref, dg_ref), vjp((du_ref[...], dw_ref[...], da_ref[...]))):
            r[...] = t

    return pl.pallas_call(
        body, name="wy_bwd", grid=_WY_GRID, in_specs=[hcol, hcol, hcol, hb, hb, hcol, hcol, at],
        out_specs=[hcol, hcol, hcol, hb, hb],
        out_shape=[jax.ShapeDtypeStruct((T, B_WIDTH), F32)] * 3 + [jax.ShapeDtypeStruct((B_HEADS, T, HD), F32)] * 2,
        compiler_params=_cparams(("arbitrary", "arbitrary")),
    )(q, k, v, beta, g, du, dw, dattn)


def _scan_step_fn(q, k, u, w, g, attn, gate, og, s):
    v_new = u - _mm(w, s)
    o = _mm(q * jnp.exp(g), s) + _mm(attn, v_new)
    g_last = jnp.sum(jnp.where(_iota(g.shape, 0) == B_CHUNK - 1, g, 0.0), axis=0, keepdims=True)
    s_new = s * jnp.exp(g_last) + _mm(k * jnp.exp(g_last - g), v_new, ta=True)
    return _rms(o, og) * _silu(gate), s_new


def _scan_specs(rev):
    ch = (lambda n: _NCH - 1 - n) if rev else (lambda n: n)
    rows = pl.BlockSpec((B_CHUNK, B_WIDTH), lambda n: (ch(n), 0))
    gb = pl.BlockSpec((B_HEADS, B_CHUNK, HD), lambda n: (0, ch(n), 0))
    at = pl.BlockSpec((B_HEADS, None, B_CHUNK, B_CHUNK), lambda n: (0, ch(n), 0, 0))
    og = pl.BlockSpec((1, HD), lambda n: (0, 0))
    st = pl.BlockSpec((None, B_HEADS, HD, HD), lambda n: (ch(n), 0, 0, 0))
    return rows, gb, at, og, st


def scan_fwd(q, k, u, w, g, attn, gate, og):
    rows, gb, at, ogs, st = _scan_specs(False)

    def body(q_ref, k_ref, u_ref, w_ref, g_ref, a_ref, gate_ref, og_ref, y_ref, st_ref, s_ref):
        @pl.when(pl.program_id(0) == 0)
        def _():
            s_ref[...] = jnp.zeros_like(s_ref)

        for h in range(B_HEADS):
            c = slice(h * HD, (h + 1) * HD)
            s = s_ref[h]
            st_ref[h] = s
            y, s_new = _scan_step_fn(q_ref[:, c], k_ref[:, c], u_ref[:, c], w_ref[:, c], g_ref[h], a_ref[h],
                                     gate_ref[:, c], og_ref[...], s)
            y_ref[:, c] = y
            s_ref[h] = s_new

    return pl.pallas_call(
        body, name="scan_fwd", grid=(_NCH,), in_specs=[rows, rows, rows, rows, gb, at, rows, ogs], out_specs=[rows, st],
        out_shape=[jax.ShapeDtypeStruct((T, B_WIDTH), F32), jax.ShapeDtypeStruct((_NCH, B_HEADS, HD, HD), F32)],
        scratch_shapes=[pltpu.VMEM((B_HEADS, HD, HD), F32)], compiler_params=_cparams(("arbitrary",)),
    )(q, k, u, w, g, attn, gate, og)


def scan_bwd(q, k, u, w, g, attn, gate, og, states, dmix):
    rows, gb, at, ogs, st = _scan_specs(True)
    dyb = pl.BlockSpec((B_CHUNK, HD), lambda n: (_NCH - 1 - n, 0))

    def body(q_ref, k_ref, u_ref, w_ref, g_ref, a_ref, gate_ref, og_ref, st_ref, *rest):
        dy_refs, (dq_ref, dk_ref, du_ref, dw_ref, dgate_ref, dg_ref, da_ref, dog_ref, ds_ref) = rest[:B_HEADS], rest[B_HEADS:]

        @pl.when(pl.program_id(0) == 0)
        def _():
            ds_ref[...] = jnp.zeros_like(ds_ref)
            dog_ref[...] = jnp.zeros_like(dog_ref)

        for h in range(B_HEADS):
            c = slice(h * HD, (h + 1) * HD)
            _, vjp = jax.vjp(_scan_step_fn, q_ref[:, c], k_ref[:, c], u_ref[:, c], w_ref[:, c], g_ref[h], a_ref[h],
                             gate_ref[:, c], og_ref[...], st_ref[h])
            dq, dk, du, dw, dg, da, dgate, dog, ds = vjp((dy_refs[h][...], ds_ref[h]))
            dq_ref[:, c] = dq
            dk_ref[:, c] = dk
            du_ref[:, c] = du
            dw_ref[:, c] = dw
            dgate_ref[:, c] = dgate
            dg_ref[h] = dg
            da_ref[h] = da
            dog_ref[...] += dog
            ds_ref[h] = ds

    dy_specs = [pl.BlockSpec((B_CHUNK, HD), lambda n, h=h: (_NCH - 1 - n, A_WIDTH // HD + h)) for h in range(B_HEADS)]
    return pl.pallas_call(
        body, name="scan_bwd", grid=(_NCH,),
        in_specs=[rows, rows, rows, rows, gb, at, rows, ogs, st] + dy_specs,
        out_specs=[rows] * 5 + [gb, at, ogs],
        out_shape=[jax.ShapeDtypeStruct((T, B_WIDTH), F32)] * 5
        + [jax.ShapeDtypeStruct((B_HEADS, T, HD), F32), jax.ShapeDtypeStruct((B_HEADS, _NCH, B_CHUNK, B_CHUNK), F32),
           jax.ShapeDtypeStruct((1, HD), F32)],
        scratch_shapes=[pltpu.VMEM((B_HEADS, HD, HD), F32)], compiler_params=_cparams(("arbitrary",)),
    )(q, k, u, w, g, attn, gate, og, states, *([dmix] * B_HEADS))


def _lanes(vec):
    return jnp.broadcast_to(vec[:, None, None], (vec.shape[0], 1, HD))


def gdn_forward(p2, conv_w, a_log, dt_bias, og):
    qa, ka, va = (conv_fwd(p2, conv_w, m) for m in "qkv")
    beta, g = gates_fwd(p2, _lanes(a_log), _lanes(dt_bias))
    u, w, attn = wy_fwd(qa, ka, va, beta, g)
    gate = p2[:, P_BG:P_BB]
    y, states = scan_fwd(qa, ka, u, w, g, attn, gate, og)
    return y, (qa, ka, va, beta, g, u, w, attn, gate, states)


def gdn_backward(p2, conv_w, a_log, dt_bias, og, saved, dmix):
    qa, ka, va, beta, g, u, w, attn, gate, states = saved
    dq1, dk1, du, dw, dgate, dg1, dattn, dog = scan_bwd(qa, ka, u, w, g, attn, gate, og, states, dmix)
    dq2, dk2, dv, dbeta, dg2 = wy_bwd(qa, ka, va, beta, g, du, dw, dattn)
    dbg, dal, ddtb = gates_bwd(p2, _lanes(a_log), _lanes(dt_bias), dbeta, dg1, dg2)
    dxq, dwq = conv_bwd(p2, conv_w, [dq1, dq2], "q")
    dxk, dwk = conv_bwd(p2, conv_w, [dk1, dk2], "k")
    dxv, dwv = conv_bwd(p2, conv_w, [dv], "v")
    dseg = jnp.concatenate([dxq, dxk, dxv, dgate, dbg], axis=1)
    return dseg, jnp.concatenate([dwq, dwk, dwv], axis=1), dal[:, 0, 0], ddtb[:, 0, 0], dog


def _to_branch_order(x, dil):
    return x if dil == 1 else x.reshape(T // dil, dil, -1).transpose(1, 0, 2).reshape(T, -1)


def _from_branch_order(x, dil):
    return x if dil == 1 else x.reshape(dil, T // dil, -1).transpose(1, 0, 2).reshape(T, -1)


_SLOPES = np.exp2(-8.0 * (np.arange(C_HEADS, dtype=np.float64) + 1.0) / C_HEADS).astype(np.float32)


def _branch_blocks(dil):
    return -(-(T // dil) // C_BLOCK)


def dattn_forward(p2, qg, kg):
    slopes = _lanes(jnp.asarray(_SLOPES))
    q, k, v = p2[:, P_CQ:P_CK], p2[:, P_CK:P_CV], p2[:, P_CV:P_END]
    qkvs, outs, lses = [], [], []
    for _, dil in C_BRANCHES:
        qkv = tuple(_to_branch_order(t, dil) for t in (q, k, v))
        o, l = attn_fwd(*qkv, qg, kg, slopes, dil, _branch_blocks(dil))
        qkvs.append(qkv)
        outs.append(_from_branch_order(o, dil))
        lses.append(_from_branch_order(l, dil))
    return combine_fwd(outs, lses), (qkvs, outs, lses)


def dattn_backward(qg, kg, saved, dmix):
    slopes = _lanes(jnp.asarray(_SLOPES))
    qkvs, outs, lses = saved
    gs = combine_bwd(outs, lses, dmix)
    dqs, dks, dvs, dqgs, dkgs = [], [], [], [], []
    for i, (_, dil) in enumerate(C_BRANCHES):
        do, dl = _to_branch_order(gs[i], dil), _to_branch_order(gs[3 + i], dil)
        dq, dk, dv, dqg, dkg = attn_bwd(*qkvs[i], qg, kg, slopes, do, dl, dil, _branch_blocks(dil))
        dqs.append(_from_branch_order(dq, dil))
        dks.append(_from_branch_order(dk, dil))
        dvs.append(_from_branch_order(dv, dil))
        dqgs.append(dqg)
        dkgs.append(dkg)
    return dqs, dks, dvs, dqgs, dkgs


def sum_branches(big, rows):
    blk = pl.BlockSpec((_CMB_ROWS, C_WIDTH), lambda i: (i, 0))
    row = pl.BlockSpec((1, HD), lambda i: (0, 0))
    nb, nr = len(big) // 3, len(rows) // 3

    def body(*refs):
        ins, outs = refs[:len(big) + len(rows)], refs[len(big) + len(rows):]
        for j in range(nb + nr):
            outs[j][...] = ins[3 * j][...] + ins[3 * j + 1][...] + ins[3 * j + 2][...]

    return pl.pallas_call(
        body, name="sum_branches", grid=(T // _CMB_ROWS,), in_specs=[blk] * len(big) + [row] * len(rows),
        out_specs=[blk] * nb + [row] * nr,
        out_shape=[jax.ShapeDtypeStruct((T, C_WIDTH), F32)] * nb + [jax.ShapeDtypeStruct((1, HD), F32)] * nr,
        compiler_params=_cparams(("arbitrary",)),
    )(*big, *rows)


_ROWS = 256
_TM = 512


def rmsnorm_fwd(x, g):
    blk = pl.BlockSpec((_ROWS, D), lambda i: (i, 0))

    def body(x_ref, g_ref, h_ref):
        h_ref[...] = _rms(x_ref[...], g_ref[...]).astype(BF16)

    return pl.pallas_call(
        body, name="rmsnorm_fwd", grid=(T // _ROWS,), in_specs=[blk, pl.BlockSpec((1, D), lambda i: (0, 0))], out_specs=blk,
        out_shape=jax.ShapeDtypeStruct((T, D), BF16), compiler_params=_cparams(("arbitrary",)),
    )(x, g)


def rmsnorm_bwd(x, g, dh, dres):
    blk = pl.BlockSpec((_ROWS, D), lambda i: (i, 0))
    row = pl.BlockSpec((1, D), lambda i: (0, 0))

    def body(x_ref, g_ref, dh_ref, dres_ref, dx_ref, dg_ref):
        _, vjp = jax.vjp(_rms, x_ref[...], g_ref[...])
        dx, dg = vjp(dh_ref[...])
        dx_ref[...] = dres_ref[...] + dx

        @pl.when(pl.program_id(0) == 0)
        def _():
            dg_ref[...] = jnp.zeros_like(dg_ref)

        dg_ref[...] += dg

    return pl.pallas_call(
        body, name="rmsnorm_bwd", grid=(T // _ROWS,), in_specs=[blk, row, blk, blk], out_specs=[blk, row],
        out_shape=[jax.ShapeDtypeStruct((T, D), F32), jax.ShapeDtypeStruct((1, D), F32)],
        compiler_params=_cparams(("arbitrary",)),
    )(x, g, dh, dres)


def _matmul(name, a, b, *, grid, a_spec, b_spec, o_spec, out_shape, ta=False, tb=False, k_axis=None, res=None):
    dims = _dims(2, ta, tb)

    def body(a_ref, b_ref, *rest):
        o_ref = rest[-1]
        prod = lax.dot_general(a_ref[...].astype(BF16), b_ref[...].astype(BF16), dims, preferred_element_type=F32)
        if res is not None:
            prod = prod + rest[0][...]
        if k_axis is None:
            o_ref[...] = prod.astype(o_ref.dtype)
        else:
            @pl.when(pl.program_id(k_axis) == 0)
            def _():
                o_ref[...] = prod

            @pl.when(pl.program_id(k_axis) > 0)
            def _():
                o_ref[...] += prod

    sem = tuple("arbitrary" for _ in grid)
    ins = [a, b] + ([res] if res is not None else [])
    specs = [a_spec, b_spec] + ([o_spec] if res is not None else [])
    return pl.pallas_call(
        body, name=name, grid=grid, in_specs=specs, out_specs=o_spec, out_shape=out_shape, compiler_params=_cparams(sem),
    )(*ins)


_IN_TN = P_END // 3


def mm_proj(h1, wp_in, l):
    return _matmul(
        "mm_proj", h1, wp_in, grid=(P_END // _IN_TN, T // _TM),
        a_spec=pl.BlockSpec((_TM, D), lambda j, i: (i, 0)),
        b_spec=pl.BlockSpec((None, D, _IN_TN), lambda j, i: (l, 0, j)),
        o_spec=pl.BlockSpec((_TM, _IN_TN), lambda j, i: (i, j)), out_shape=jax.ShapeDtypeStruct((T, P_END), F32))


def mm_dh1(dp2, wp_in, l):
    return _matmul(
        "mm_dh1", dp2, wp_in, grid=(T // _TM, P_END // _IN_TN), tb=True, k_axis=1,
        a_spec=pl.BlockSpec((_TM, _IN_TN), lambda i, k: (i, k)),
        b_spec=pl.BlockSpec((None, D, _IN_TN), lambda i, k: (l, 0, k)),
        o_spec=pl.BlockSpec((_TM, D), lambda i, k: (i, 0)), out_shape=jax.ShapeDtypeStruct((T, D), F32))


def mm_dwin(h1, dp2):
    return _matmul(
        "mm_dwin", h1, dp2, grid=(P_END // _IN_TN, D // _TM), ta=True,
        a_spec=pl.BlockSpec((T, _TM), lambda j, i: (0, i)),
        b_spec=pl.BlockSpec((T, _IN_TN), lambda j, i: (0, j)),
        o_spec=pl.BlockSpec((_TM, _IN_TN), lambda j, i: (i, j)), out_shape=jax.ShapeDtypeStruct((D, P_END), BF16))


def _mm_square(name, a, w, l, res, tb):
    tn = 1024
    b_spec = (pl.BlockSpec((None, tn, D), lambda j, i: (l, j, 0)) if tb else pl.BlockSpec((None, D, tn), lambda j, i: (l, 0, j)))
    return _matmul(
        name, a, w, grid=(D // tn, T // _TM), tb=tb, res=res,
        a_spec=pl.BlockSpec((_TM, D), lambda j, i: (i, 0)), b_spec=b_spec,
        o_spec=pl.BlockSpec((_TM, tn), lambda j, i: (i, j)), out_shape=jax.ShapeDtypeStruct((T, D), F32))


def mm_out(mix, wg_out, l, x):
    return _mm_square("mm_out", mix, wg_out, l, x, False)


def mm_dmix(dx1, wg_out, l):
    return _mm_square("mm_dmix", dx1, wg_out, l, None, True)


def mm_dwout(mix, dx1):
    tn = 1024
    return _matmul(
        "mm_dwout", mix, dx1, grid=(D // tn, D // _TM), ta=True,
        a_spec=pl.BlockSpec((T, _TM), lambda j, i: (0, i)), b_spec=pl.BlockSpec((T, tn), lambda j, i: (0, j)),
        o_spec=pl.BlockSpec((_TM, tn), lambda j, i: (i, j)), out_shape=jax.ShapeDtypeStruct((D, D), BF16))


_GU_TN = GU_SHARD // 2


def mm_gu(h2, wg_gu, l):
    return _matmul(
        "mm_gu", h2, wg_gu, grid=(N_CHIPS, 2, T // _TM),
        a_spec=pl.BlockSpec((_TM, D), lambda s, j, i: (i, 0)),
        b_spec=pl.BlockSpec((None, None, D, _GU_TN), lambda s, j, i: (l, s, 0, j)),
        o_spec=pl.BlockSpec((_TM, _GU_TN), lambda s, j, i: (i, 2 * s + j)),
        out_shape=jax.ShapeDtypeStruct((T, 2 * FFN), F32))


_GU_NJ = FFN // _GU_TN


def mm_dh2(dgu, wg_gu, l):
    return _matmul(
        "mm_dh2", dgu, wg_gu, grid=(T // _TM, 2 * N_CHIPS), tb=True, k_axis=1,
        a_spec=pl.BlockSpec((None, _TM, _GU_TN), lambda i, k: (k // _GU_NJ, i, k % _GU_NJ)),
        b_spec=pl.BlockSpec((None, None, D, _GU_TN), lambda i, k: (l, k // 2, 0, k % 2)),
        o_spec=pl.BlockSpec((_TM, D), lambda i, k: (i, 0)), out_shape=jax.ShapeDtypeStruct((T, D), F32))


def mm_dwgu(h2, dgu):
    return _matmul(
        "mm_dwgu", h2, dgu, grid=(N_CHIPS, 2, D // _TM), ta=True,
        a_spec=pl.BlockSpec((T, _TM), lambda s, j, i: (0, i)),
        b_spec=pl.BlockSpec((None, T, _GU_TN), lambda s, j, i: ((2 * s + j) // _GU_NJ, 0, (2 * s + j) % _GU_NJ)),
        o_spec=pl.BlockSpec((None, _TM, _GU_TN), lambda s, j, i: (s, i, j)),
        out_shape=jax.ShapeDtypeStruct((N_CHIPS, D, GU_SHARD), BF16))


def mm_down(act, wg_down, l, x1):
    tn = 512
    return _matmul(
        "mm_down", act, wg_down, grid=(D // tn, T // _TM), res=x1,
        a_spec=pl.BlockSpec((_TM, FFN), lambda j, i: (i, 0)),
        b_spec=pl.BlockSpec((None, FFN, tn), lambda j, i: (l, 0, j)),
        o_spec=pl.BlockSpec((_TM, tn), lambda j, i: (i, j)), out_shape=jax.ShapeDtypeStruct((T, D), F32))


def mm_dact(dx2, wg_down, l):
    tn = DOWN_SHARD
    return _matmul(
        "mm_dact", dx2, wg_down, grid=(FFN // tn, T // _TM), tb=True,
        a_spec=pl.BlockSpec((_TM, D), lambda j, i: (i, 0)),
        b_spec=pl.BlockSpec((None, tn, D), lambda j, i: (l, j, 0)),
        o_spec=pl.BlockSpec((_TM, tn), lambda j, i: (i, j)), out_shape=jax.ShapeDtypeStruct((T, FFN), F32))


def mm_dwdown(act, dx2):
    tm, tn = DOWN_SHARD, 512
    return _matmul(
        "mm_dwdown", act, dx2, grid=(D // tn, FFN // tm), ta=True,
        a_spec=pl.BlockSpec((T, tm), lambda j, i: (0, i)), b_spec=pl.BlockSpec((T, tn), lambda j, i: (0, j)),
        o_spec=pl.BlockSpec((tm, tn), lambda j, i: (i, j)), out_shape=jax.ShapeDtypeStruct((FFN, D), BF16))


_FF_TN = 1408


def _swiglu_fn(gt, up):
    return _silu(gt) * up


def swiglu_fwd(gu):
    nj = FFN // _FF_TN
    gt = pl.BlockSpec((_ROWS, _FF_TN), lambda i, j: (i, j))
    up = pl.BlockSpec((_ROWS, _FF_TN), lambda i, j: (i, nj + j))

    def body(gt_ref, up_ref, o_ref):
        o_ref[...] = _swiglu_fn(gt_ref[...], up_ref[...]).astype(BF16)

    return pl.pallas_call(
        body, name="swiglu_fwd", grid=(T // _ROWS, nj), in_specs=[gt, up], out_specs=gt,
        out_shape=jax.ShapeDtypeStruct((T, FFN), BF16), compiler_params=_cparams(("arbitrary", "arbitrary")),
    )(gu, gu)


def swiglu_bwd(gu, dact):
    nj = FFN // _FF_TN
    gt = pl.BlockSpec((_ROWS, _FF_TN), lambda i, j: (i, j))
    up = pl.BlockSpec((_ROWS, _FF_TN), lambda i, j: (i, nj + j))

    def body(gt_ref, up_ref, d_ref, dgu_ref):
        _, vjp = jax.vjp(_swiglu_fn, gt_ref[...], up_ref[...])
        dgt, dup = vjp(d_ref[...])
        dgu_ref[0] = dgt.astype(BF16)
        dgu_ref[1] = dup.astype(BF16)

    return pl.pallas_call(
        body, name="swiglu_bwd", grid=(T // _ROWS, nj), in_specs=[gt, up, gt],
        out_specs=pl.BlockSpec((2, _ROWS, _FF_TN), lambda i, j: (0, i, j)),
        out_shape=jax.ShapeDtypeStruct((2, T, FFN), BF16), compiler_params=_cparams(("arbitrary", "arbitrary")),
    )(gu, gu, dact)


def loss_and_grad(y, target):
    blk = pl.BlockSpec((_ROWS, D), lambda i: (i, 0))
    acc = pl.BlockSpec((8, HD), lambda i: (0, 0))

    def body(y_ref, t_ref, dy_ref, l_ref):
        err = y_ref[...] - t_ref[...]
        dy_ref[...] = err * (1.0 / D)

        @pl.when(pl.program_id(0) == 0)
        def _():
            l_ref[...] = jnp.zeros_like(l_ref)

        l_ref[...] += (0.5 / D) * jnp.sum(err * err)

    return pl.pallas_call(
        body, name="loss_and_grad", grid=(T // _ROWS,), in_specs=[blk, blk], out_specs=[blk, acc],
        out_shape=[jax.ShapeDtypeStruct((T, D), F32), jax.ShapeDtypeStruct((8, HD), F32)],
        compiler_params=_cparams(("arbitrary",)),
    )(y, target)


def adamw(w, g, m, v, name):
    rows, cols = w.shape
    tr = _ROWS if rows % _ROWS == 0 else rows
    blk = pl.BlockSpec((tr, cols), lambda i: (i, 0))

    def body(w_ref, g_ref, m_ref, v_ref, d_ref, nm_ref, nv_ref):
        gg = g_ref[...]
        nm = ADAM_B1 * m_ref[...] + (1.0 - ADAM_B1) * gg
        nv = ADAM_B2 * v_ref[...] + (1.0 - ADAM_B2) * (gg * gg)
        m_hat = nm / (1.0 - ADAM_B1 ** ADAM_STEP)
        v_hat = nv / (1.0 - ADAM_B2 ** ADAM_STEP)
        d_ref[...] = -ADAM_LR * (m_hat / (jnp.sqrt(v_hat) + ADAM_EPS) + ADAM_WD * w_ref[...])
        nm_ref[...] = nm
        nv_ref[...] = nv

    return pl.pallas_call(
        body, name=name, grid=(rows // tr,), in_specs=[blk] * 4, out_specs=[blk] * 3,
        out_shape=[jax.ShapeDtypeStruct(w.shape, F32)] * 3, compiler_params=_cparams(("arbitrary",)),
    )(w, g, m, v)


def shards_to_segments(w):
    valid = jnp.concatenate([w[..., s, :, :IN_SHARD] for s in range(N_CHIPS)], axis=-1)
    pad = jnp.zeros(valid.shape[:-1] + (P_CQ - GATE_COLS,), w.dtype)
    return jnp.concatenate([valid[..., :GATE_COLS], pad, valid[..., GATE_COLS:]], axis=-1)


def segments_to_shards(w):
    valid = jnp.concatenate([w[:, :GATE_COLS], w[:, P_CQ:]], axis=1)
    pad = ((0, 0), (0, IN_SHARD_PAD - IN_SHARD))
    return jnp.stack([jnp.pad(valid[:, s * IN_SHARD:(s + 1) * IN_SHARD], pad) for s in range(N_CHIPS)])


def layer_forward(x, l, wg, sp):
    h1 = rmsnorm_fwd(x, sp["norm1_g"])
    p2 = mm_proj(h1, wg["in"], l)
    y_a = sgu_fwd(p2, sp["sgu_norm_g"], sp["w_spatial"], sp["b_spatial"])
    y_b, saved_b = gdn_forward(p2, sp["conv_w"], sp["a_log"], sp["dt_bias"], sp["o_norm_g"])
    y_c, saved_c = dattn_forward(p2, sp["q_norm_g"], sp["k_norm_g"])
    mix = jnp.concatenate([y_a, y_b, y_c], axis=1)
    x1 = mm_out(mix, wg["out"], l, x)
    h2 = rmsnorm_fwd(x1, sp["norm2_g"])
    gu = mm_gu(h2, wg["gu"], l)
    act = swiglu_fwd(gu)
    x2 = mm_down(act, wg["down"], l, x1)
    return x2, (x, h1, p2, saved_b, saved_c, mix, x1, h2, gu, act)


def layer_backward(dx2, l, wg, sp, saved):
    x, h1, p2, saved_b, saved_c, mix, x1, h2, gu, act = saved
    dact = mm_dact(dx2, wg["down"], l)
    dw_down = mm_dwdown(act, dx2)
    dgu = swiglu_bwd(gu, dact)
    dw_gu = mm_dwgu(h2, dgu)
    dh2 = mm_dh2(dgu, wg["gu"], l)
    dx1, dnorm2 = rmsnorm_bwd(x1, sp["norm2_g"], dh2, dx2)
    dmix = mm_dmix(dx1, wg["out"], l)
    dw_out = mm_dwout(mix, dx1)
    du, dv, dsg, dws, dbs = sgu_bwd(p2, sp["sgu_norm_g"], sp["w_spatial"], sp["b_spatial"], dmix)
    dseg_b, dconv, dal, ddtb, dog = gdn_backward(p2, sp["conv_w"], sp["a_log"], sp["dt_bias"], sp["o_norm_g"], saved_b, dmix)
    dqs, dks, dvs, dqgs, dkgs = dattn_backward(sp["q_norm_g"], sp["k_norm_g"], saved_c, dmix)
    dcq, dck, dcv, dqg, dkg = sum_branches(dqs + dks + dvs, dqgs + dkgs)
    dp2 = jnp.concatenate([du, dv, dseg_b, dcq, dck, dcv], axis=1).astype(BF16)
    dw_in = segments_to_shards(mm_dwin(h1, dp2))
    dh1 = mm_dh1(dp2, wg["in"], l)
    dx, dnorm1 = rmsnorm_bwd(x, sp["norm1_g"], dh1, dx1)
    big = {"in": dw_in, "out": dw_out, "gu": dw_gu, "down": dw_down}
    small = {"norm1_g": dnorm1, "sgu_norm_g": dsg, "w_spatial": dws, "b_spatial": dbs, "conv_w": dconv, "a_log": dal,
             "dt_bias": ddtb, "o_norm_g": dog, "q_norm_g": dqg, "k_norm_g": dkg, "norm2_g": dnorm2}
    return dx, big, small


_HBM = pl.BlockSpec(memory_space=pltpu.HBM)
_MESH = pl.DeviceIdType.MESH


def _place():
    x, y, c = lax.axis_index("x"), lax.axis_index("y"), lax.axis_index("c")
    chips = [(1 - x, y), (x, 1 - y), (1 - x, 1 - y)]
    return x, y, c, chips


def _rcopy(src, dst, ssem, rsem, dev):
    return pltpu.make_async_remote_copy(src_ref=src, dst_ref=dst, send_sem=ssem, recv_sem=rsem, device_id=dev,
                                        device_id_type=_MESH)


def _xor(a, b):
    return a + b - 2 * a * b


def gather_weights(shards):
    n = len(shards)

    def body(*refs):
        ins, outs = refs[:n], refs[n:2 * n]
        s_ici, r_ici, s_d2d, r_d2d, s_own, r_own = refs[2 * n:]
        x, y, c, _ = _place()
        s = 2 * x + y
        sibling = (x, y, 1 - c)
        nbr = [(1 - x, y), (x, 1 - y)]
        src_chip = (_xor(x, 1 - c), _xor(y, c))
        dst_chip = (_xor(x, c), _xor(y, 1 - c))
        t_src = 2 * src_chip[0] + src_chip[1]
        t_oth = 2 * dst_chip[0] + dst_chip[1]
        t_dia = 2 * (1 - x) + (1 - y)
        sends = []
        for a in range(n):
            for l in range(DEPTH):
                sends.append(_rcopy(ins[a].at[l], outs[a].at[l, s], s_own.at[a, l], r_own.at[a, l], sibling))
            for k in range(2):
                sends.append(_rcopy(ins[a].at[c], outs[a].at[c, s], s_ici.at[a, k], r_ici.at[a, k], (*nbr[k], c)))
        for cp in sends:
            cp.start()

        def landed(a, t, k):
            _rcopy(ins[a].at[c], outs[a].at[c, t], s_ici.at[a, k], r_ici.at[a, k], sibling).wait_recv()
            cp = _rcopy(outs[a].at[c, t], outs[a].at[c, t], s_d2d.at[a, k], r_d2d.at[a, k], sibling)
            cp.start()
            sends.append(cp)

        for a in range(n):
            landed(a, t_src, c)
            fwd = _rcopy(outs[a].at[c, t_src], outs[a].at[c, t_src], s_ici.at[a, 2], r_ici.at[a, 2], (*dst_chip, c))
            fwd.start()
            sends.append(fwd)
        for a in range(n):
            landed(a, t_oth, 1 - c)
        for a in range(n):
            landed(a, t_dia, 2)
        for a in range(n):
            for l in range(DEPTH):
                _rcopy(ins[a].at[l], outs[a].at[l, s], s_own.at[a, l], r_own.at[a, l], sibling).wait_recv()
            for k, t in enumerate([2 * nbr[0][0] + nbr[0][1], 2 * nbr[1][0] + nbr[1][1], t_dia]):
                _rcopy(ins[a].at[1 - c], outs[a].at[1 - c, t], s_d2d.at[a, k], r_d2d.at[a, k], sibling).wait_recv()
        for cp in sends:
            cp.wait_send()

    dma = lambda k: pltpu.SemaphoreType.DMA((n, k))
    return pl.pallas_call(
        body, name="gather_weights", in_specs=[_HBM] * n, out_specs=[_HBM] * n,
        out_shape=[jax.ShapeDtypeStruct((DEPTH, N_CHIPS) + w.shape[1:], w.dtype) for w in shards],
        scratch_shapes=[dma(3), dma(3), dma(3), dma(3), dma(DEPTH), dma(DEPTH)],
    )(*shards)


def exchange_halves(grads):
    n = len(grads)

    def body(*refs):
        ins, outs = refs[:n], refs[n:2 * n]
        ssem, rsem = refs[2 * n:]
        x, y, c, _ = _place()
        cps = []
        for a in range(n):
            h = grads[a].shape[1] // 2
            cps.append(_rcopy(ins[a].at[:, pl.ds((1 - c) * h, h)], outs[a], ssem.at[a], rsem.at[a], (x, y, 1 - c)))
        for cp in cps:
            cp.start()
        for cp in cps:
            cp.wait()

    return pl.pallas_call(
        body, name="exchange_halves", in_specs=[_HBM] * n, out_specs=[_HBM] * n,
        out_shape=[jax.ShapeDtypeStruct((g.shape[0], g.shape[1] // 2, g.shape[2]), g.dtype) for g in grads],
        scratch_shapes=[pltpu.SemaphoreType.DMA((n,)), pltpu.SemaphoreType.DMA((n,))],
    )(*grads)


def scatter_to_chips(parts):
    n = len(parts)

    def body(*refs):
        ins, outs = refs[:n], refs[n:2 * n]
        ssem, rsem = refs[2 * n:]
        x, y, c, chips = _place()
        cps = [_rcopy(ins[a].at[2 * cx + cy], outs[a].at[j], ssem.at[a, j], rsem.at[a, j], (cx, cy, c))
               for a in range(n) for j, (cx, cy) in enumerate(chips)]
        for cp in cps:
            cp.start()
        for cp in cps:
            cp.wait()

    return pl.pallas_call(
        body, name="scatter_to_chips", in_specs=[_HBM] * n, out_specs=[_HBM] * n,
        out_shape=[jax.ShapeDtypeStruct((3,) + p.shape[1:], p.dtype) for p in parts],
        scratch_shapes=[pltpu.SemaphoreType.DMA((n, 3)), pltpu.SemaphoreType.DMA((n, 3))],
    )(*parts)


def share_halves(halves):
    n = len(halves)

    def body(*refs):
        ins, outs = refs[:n], refs[n:2 * n]
        ssem, rsem = refs[2 * n:]
        x, y, c, _ = _place()
        cps = [_rcopy(ins[i], outs[i], ssem.at[i], rsem.at[i], (x, y, 1 - c)) for i in range(n)]
        for cp in cps:
            cp.start()
        for cp in cps:
            cp.wait()

    return pl.pallas_call(
        body, name="share_halves", in_specs=[_HBM] * n, out_specs=[_HBM] * n,
        out_shape=[jax.ShapeDtypeStruct(h.shape, h.dtype) for h in halves],
        scratch_shapes=[pltpu.SemaphoreType.DMA((n,)), pltpu.SemaphoreType.DMA((n,))],
    )(*halves)


def adamw_shard(w, m, v, mine, theirs, c, name):
    _, r, cw = w.shape
    h, cg = mine[0].shape
    tr = next(t for t in (256, 176, 128) if h % t == 0 and t * cg * 4 <= (3 << 19))
    nb = h // tr
    wblk = pl.BlockSpec((None, tr, cw), lambda l, i, c_ref: (l, i, 0))
    gblk = pl.BlockSpec((tr, cg), lambda l, i, c_ref: (i % nb, 0))

    def body(c_ref, w_ref, m_ref, v_ref, m0, m1, t0, t1, g_ref, d_ref, nm_ref, nv_ref):
        is_mine = (pl.program_id(1) // nb) == c_ref[0]
        first = pl.program_id(0) == 0
        gg = jnp.where(is_mine, jnp.where(first, m0[:, :cw], m1[:, :cw]), jnp.where(first, t0[:, :cw], t1[:, :cw]))
        nm = ADAM_B1 * m_ref[...] + (1.0 - ADAM_B1) * gg
        nv = ADAM_B2 * v_ref[...] + (1.0 - ADAM_B2) * (gg * gg)
        m_hat = nm / (1.0 - ADAM_B1 ** ADAM_STEP)
        v_hat = nv / (1.0 - ADAM_B2 ** ADAM_STEP)
        g_ref[...] = gg
        d_ref[...] = -ADAM_LR * (m_hat / (jnp.sqrt(v_hat) + ADAM_EPS) + ADAM_WD * w_ref[...])
        nm_ref[...] = nm
        nv_ref[...] = nv

    return pl.pallas_call(
        body, name=name,
        grid_spec=pltpu.PrefetchScalarGridSpec(
            num_scalar_prefetch=1, grid=(DEPTH, r // tr), in_specs=[wblk] * 3 + [gblk] * 4, out_specs=[wblk] * 4),
        out_shape=[jax.ShapeDtypeStruct(w.shape, F32)] * 4, compiler_params=_cparams(("arbitrary", "arbitrary")),
    )(c, w, m, v, mine[0], mine[1], theirs[0], theirs[1])


def _half_rows(h, cols):
    for tr in (512, 256, 352, 128, 64):
        if h % tr == 0 and tr * cols * 4 <= 6 * 1024 * 1024:
            return tr
    raise ValueError((h, cols))


def add_sibling(grad, recv, c):
    _, r, cols = grad.shape
    h = r // 2
    tr = _half_rows(h, cols)
    nb = h // tr

    def body(c_ref, g_ref, r_ref, o_ref):
        o_ref[...] = (g_ref[...].astype(F32) + r_ref[...].astype(F32)).astype(BF16)

    return pl.pallas_call(
        body, name="add_sibling",
        grid_spec=pltpu.PrefetchScalarGridSpec(
            num_scalar_prefetch=1, grid=(N_CHIPS, nb),
            in_specs=[pl.BlockSpec((None, tr, cols), lambda t, i, c_ref: (t, c_ref[0] * nb + i, 0)),
                      pl.BlockSpec((None, tr, cols), lambda t, i, c_ref: (t, i, 0))],
            out_specs=pl.BlockSpec((None, tr, cols), lambda t, i, c_ref: (t, i, 0))),
        out_shape=jax.ShapeDtypeStruct((N_CHIPS, h, cols), BF16), compiler_params=_cparams(("arbitrary", "arbitrary")),
    )(c, grad, recv)


def add_chips(part, recv, s):
    _, h, cols = part.shape
    tr = _half_rows(h, cols)

    def body(s_ref, p_ref, r_ref, o_ref):
        o_ref[...] = ((p_ref[...].astype(F32) + r_ref[0].astype(F32)) + r_ref[1].astype(F32)) + r_ref[2].astype(F32)

    return pl.pallas_call(
        body, name="add_chips",
        grid_spec=pltpu.PrefetchScalarGridSpec(
            num_scalar_prefetch=1, grid=(h // tr,),
            in_specs=[pl.BlockSpec((None, tr, cols), lambda i, s_ref: (s_ref[0], i, 0)),
                      pl.BlockSpec((3, tr, cols), lambda i, s_ref: (0, i, 0))],
            out_specs=pl.BlockSpec((tr, cols), lambda i, s_ref: (i, 0))),
        out_shape=jax.ShapeDtypeStruct((h, cols), F32), compiler_params=_cparams(("arbitrary",)),
    )(s, part, recv)


def allreduce_small(vec):
    rows = vec.shape[0]

    def body(v_ref, o_ref, buf, ssem, rsem, lsem):
        x, y, c, chips = _place()
        me, sibling = (x, y, c), (x, y, 1 - c)

        def blk(px, py, pc):
            return buf.at[4 * px + 2 * py + pc]

        def copy(k, block, to, src=None):
            return _rcopy(blk(*block) if src is None else src, blk(*block), ssem.at[k], rsem.at[k], to)

        mine = pltpu.make_async_copy(v_ref, blk(*me), lsem)
        mine.start()
        first = [copy(0, me, sibling, src=v_ref)] + [copy(1 + j, me, (*chip, c), src=v_ref) for j, chip in enumerate(chips)]
        for cp in first:
            cp.start()
        passed = [copy(4 + j, (*chip, c), sibling) for j, chip in enumerate(chips)]
        for j, chip in enumerate(chips):
            copy(1 + j, (*chip, c), me).wait_recv()
            passed[j].start()
        copy(0, sibling, me).wait_recv()
        for j, chip in enumerate(chips):
            copy(4 + j, (*chip, 1 - c), me).wait_recv()
        for cp in first + passed:
            cp.wait_send()
        mine.wait()
        acc = buf[0]
        for d in range(1, N_DEV):
            acc = acc + buf[d]
        o_ref[...] = acc

    vm = pl.BlockSpec(memory_space=pltpu.VMEM)
    return pl.pallas_call(
        body, name="allreduce_small", in_specs=[vm], out_specs=vm, out_shape=jax.ShapeDtypeStruct(vec.shape, F32),
        scratch_shapes=[pltpu.VMEM((N_DEV, rows, HD), F32), pltpu.SemaphoreType.DMA((7,)), pltpu.SemaphoreType.DMA((7,)),
                        pltpu.SemaphoreType.DMA],
        compiler_params=pltpu.CompilerParams(vmem_limit_bytes=VMEM_LIMIT),
    )(vec)


SMALL_NAMES = ("norm1_g", "sgu_norm_g", "w_spatial", "b_spatial", "conv_w", "a_log", "dt_bias", "o_norm_g", "q_norm_g",
               "k_norm_g", "norm2_g")


def small_params(l, p, conv_full):
    return {"norm1_g": p["norm1_g"][l][None], "sgu_norm_g": p["sgu_norm_g"][l][:, None, :], "w_spatial": p["w_spatial"][l],
            "b_spatial": p["b_spatial"][l][..., None], "conv_w": conv_full[l], "a_log": p["a_log"][l], "dt_bias": p["dt_bias"][l],
            "o_norm_g": p["o_norm_g"][l][None], "q_norm_g": p["q_norm_g"][l][None], "k_norm_g": p["k_norm_g"][l][None],
            "norm2_g": p["norm2_g"][l][None]}


def local_step(x, target, wg, sps):
    saved = []
    for l in range(DEPTH):
        x, s = layer_forward(x, l, wg, sps[l])
        saved.append(s)
    dx, loss = loss_and_grad(x, target)
    bigs, smalls = [None] * DEPTH, [None] * DEPTH
    for l in reversed(range(DEPTH)):
        dx, bigs[l], smalls[l] = layer_backward(dx, l, wg, sps[l], saved[l])
    return loss, dx, bigs, smalls


_PACK_TILE = 8 * HD


def _pack(arrays):
    flat = jnp.concatenate([a.reshape(-1) for a in arrays])
    pad = -flat.shape[0] % _PACK_TILE
    return jnp.pad(flat, (0, pad)).reshape(-1, HD)


def _unpack(packed, shapes):
    flat, out, off = packed.reshape(-1), [], 0
    for shp in shapes:
        n = int(np.prod(shp))
        out.append(flat[off:off + n].reshape(shp))
        off += n
    return out


BIG_NAMES = ("in", "out", "gu", "down")
WEIGHT_ORDER = ("norm1_g", "w_in", "sgu_norm_g", "w_spatial", "b_spatial", "conv_w", "a_log", "dt_bias", "o_norm_g", "q_norm_g",
                "k_norm_g", "w_out", "norm2_g", "w_gate_up", "w_down")


def kernel(x, norm1_g, w_in, sgu_norm_g, w_spatial, b_spatial, conv_w, a_log, dt_bias, o_norm_g, q_norm_g, k_norm_g, w_out, norm2_g, w_gate_up, w_down, loss_target, m_norm1_g, m_w_in, m_sgu_norm_g, m_w_spatial, m_b_spatial, m_conv_w, m_a_log, m_dt_bias, m_o_norm_g, m_q_norm_g, m_k_norm_g, m_w_out, m_norm2_g, m_w_gate_up, m_w_down, v_norm1_g, v_w_in, v_sgu_norm_g, v_w_spatial, v_b_spatial, v_conv_w, v_a_log, v_dt_bias, v_o_norm_g, v_q_norm_g, v_k_norm_g, v_w_out, v_norm2_g, v_w_gate_up, v_w_down):
    w = dict(norm1_g=norm1_g, w_in=w_in, sgu_norm_g=sgu_norm_g, w_spatial=w_spatial, b_spatial=b_spatial, conv_w=conv_w,
             a_log=a_log, dt_bias=dt_bias, o_norm_g=o_norm_g, q_norm_g=q_norm_g, k_norm_g=k_norm_g, w_out=w_out,
             norm2_g=norm2_g, w_gate_up=w_gate_up, w_down=w_down)
    m = dict(norm1_g=m_norm1_g, w_in=m_w_in, sgu_norm_g=m_sgu_norm_g, w_spatial=m_w_spatial, b_spatial=m_b_spatial,
             conv_w=m_conv_w, a_log=m_a_log, dt_bias=m_dt_bias, o_norm_g=m_o_norm_g, q_norm_g=m_q_norm_g, k_norm_g=m_k_norm_g,
             w_out=m_w_out, norm2_g=m_norm2_g, w_gate_up=m_w_gate_up, w_down=m_w_down)
    v = dict(norm1_g=v_norm1_g, w_in=v_w_in, sgu_norm_g=v_sgu_norm_g, w_spatial=v_w_spatial, b_spatial=v_b_spatial,
             conv_w=v_conv_w, a_log=v_a_log, dt_bias=v_dt_bias, o_norm_g=v_o_norm_g, q_norm_g=v_q_norm_g, k_norm_g=v_k_norm_g,
             w_out=v_w_out, norm2_g=v_norm2_g, w_gate_up=v_w_gate_up, w_down=v_w_down)
    chip = (2 * lax.axis_index("x") + lax.axis_index("y")).astype(jnp.int32)
    core = lax.axis_index("c").astype(jnp.int32)

    in_pad = IN_SHARD_PAD - IN_SHARD
    shards = [w_gate_up.astype(BF16), jnp.pad(w_in, ((0, 0), (0, 0), (0, in_pad))).astype(BF16), w_down.astype(BF16),
              w_out.astype(BF16), conv_w]
    g_gu, g_in, g_down, g_out, g_conv = gather_weights(shards)
    wg = {"in": shards_to_segments(g_in), "out": g_out.reshape(DEPTH, D, D), "gu": g_gu, "down": g_down.reshape(DEPTH, FFN, D)}
    conv_full = g_conv.transpose(0, 2, 1, 3).reshape(DEPTH, B_CONV, 3 * B_WIDTH)
    sps = [small_params(l, w, conv_full) for l in range(DEPTH)]

    loss_tile, dx, bigs, smalls = local_step(x[0], loss_target[0], wg, sps)

    shard_rows = {"in": D, "out": OUT_SHARD, "gu": D, "down": DOWN_SHARD}
    grads = [bigs[l][k].reshape(N_CHIPS, shard_rows[k], -1) for l in range(DEPTH) for k in BIG_NAMES]
    from_sibling = exchange_halves(grads)
    parts = [add_sibling(g, r, core.reshape(1)) for g, r in zip(grads, from_sibling)]
    from_chips = scatter_to_chips(parts)
    halves = [add_chips(p, r, chip.reshape(1)) for p, r in zip(parts, from_chips)]
    theirs = share_halves(halves)
    grad, delta, new_m, new_v = {}, {}, {}, {}
    for a, n in enumerate(("w_in", "w_out", "w_gate_up", "w_down")):
        pick = lambda lst: [lst[l * len(BIG_NAMES) + a] for l in range(DEPTH)]
        grad[n], delta[n], new_m[n], new_v[n] = adamw_shard(w[n], m[n], v[n], pick(halves), pick(theirs), core.reshape(1),
                                                            "adamw_" + n)

    stacked = [jnp.stack([smalls[l][n] for l in range(DEPTH)]) for n in SMALL_NAMES]
    total = allreduce_small(_pack(stacked + [loss_tile[0, :1]]))
    shapes = [(DEPTH, B_CONV, 3 * B_WIDTH) if n == "conv_w" else w[n].shape for n in SMALL_NAMES]
    small_grads = dict(zip(SMALL_NAMES, _unpack(total, shapes + [(1,)])[:-1]))
    loss = _unpack(total, shapes + [(1,)])[-1][0]
    conv_cols = conv_w.shape[-1]
    small_grads["conv_w"] = lax.dynamic_slice_in_dim(small_grads["conv_w"], chip * conv_cols, conv_cols, axis=2)
    grad.update(small_grads)

    sshapes = [w[n].shape for n in SMALL_NAMES]
    packed = [_pack([d[n] for n in SMALL_NAMES]) for d in (w, grad, m, v)]
    for dst, t in zip((delta, new_m, new_v), adamw(*packed, "adamw_small")):
        dst.update(zip(SMALL_NAMES, _unpack(t, sshapes)))

    out = [loss, dx[None]]
    for d in (grad, delta, new_m, new_v):
        out += [d[n] for n in WEIGHT_ORDER]
    return tuple(out)
```

```python
import functools
import math

import numpy as np
import jax
import jax.numpy as jnp
from jax import lax
from jax.experimental import pallas as pl
from jax.experimental.pallas import tpu as pltpu

F32 = jnp.float32
BF16 = jnp.bfloat16
HI = lax.Precision.HIGHEST

T = 2048
D = 2048
DEPTH = 2
HD = 128
A_GROUPS, A_WIDTH, A_CHUNK = 4, 512, 128
B_HEADS, B_WIDTH, B_CONV, B_CHUNK = 6, 768, 4, 64
C_HEADS, C_WIDTH, C_BLOCK = 6, 768, 128
C_BRANCHES = ((128, 1), (512, 4), (2048, 16))
FFN = 5632
IN_TOTAL = 6412
EPS = 1e-6
N_CHIPS = 4
N_DEV = 8
IN_SHARD = IN_TOTAL // N_CHIPS
IN_SHARD_PAD = 1664
GU_SHARD = 2 * FFN // N_CHIPS
OUT_SHARD = D // N_CHIPS
DOWN_SHARD = FFN // N_CHIPS
P_AU, P_AV, P_BQ, P_BK, P_BV, P_BG, P_BB, P_CQ, P_CK, P_CV, P_END = (
    0, 512, 1024, 1792, 2560, 3328, 4096, 4224, 4992, 5760, 6528)
GATE_COLS = 4108
VMEM_LIMIT = 56 * 1024 * 1024

ADAM_LR, ADAM_B1, ADAM_B2, ADAM_EPS, ADAM_WD, ADAM_STEP = 0.001, 0.9, 0.999, 1e-08, 0.01, 10


def _cparams(sem, vmem=VMEM_LIMIT):
    return pltpu.CompilerParams(dimension_semantics=sem, vmem_limit_bytes=vmem)


def _dims(nd, ta, tb):
    off = nd - 2
    ca = off + (0 if ta else 1)
    cb = off + (1 if tb else 0)
    batch = ((0,), (0,)) if nd == 3 else ((), ())
    return (((ca,), (cb,)), batch)


def _raw_mm(a, b, ta, tb, hi):
    if hi:
        return lax.dot_general(a, b, _dims(a.ndim, ta, tb), precision=HI, preferred_element_type=F32)
    return lax.dot_general(a.astype(BF16), b.astype(BF16), _dims(a.ndim, ta, tb), preferred_element_type=F32)


@functools.partial(jax.custom_vjp, nondiff_argnums=(2, 3, 4))
def _mm(a, b, ta=False, tb=False, hi=False):
    return _raw_mm(a, b, ta, tb, hi)


def _mm_fwd(a, b, ta, tb, hi):
    return _raw_mm(a, b, ta, tb, hi), (a, b)


def _mm_bwd(ta, tb, hi, res, g):
    a, b = res
    da = _raw_mm(g, b, False, not tb, hi) if not ta else _raw_mm(b, g, tb, True, hi)
    db = _raw_mm(a, g, not ta, False, hi) if not tb else _raw_mm(g, a, True, ta, hi)
    return da.astype(a.dtype), db.astype(b.dtype)


_mm.defvjp(_mm_fwd, _mm_bwd)


def _rms(x, g):
    return x * lax.rsqrt(jnp.mean(x * x, axis=-1, keepdims=True) + EPS) * g


def _gelu(x):
    return 0.5 * x * (1.0 + jnp.tanh(math.sqrt(2.0 / math.pi) * (x + 0.044715 * (x * x * x))))


def _sigmoid(x):
    return 1.0 / (1.0 + jnp.exp(-x))


def _silu(x):
    return x * _sigmoid(x)


def _softplus(x):
    return jnp.maximum(x, 0.0) + jnp.log(1.0 + jnp.exp(-jnp.abs(x)))


def _iota(shape, dim):
    return lax.broadcasted_iota(jnp.int32, shape, dim)


def _sgu_fn(u, v, sg, w, b):
    nc = T // A_CHUNK
    ug = _gelu(u)
    vn = _rms(_gelu(v), sg)
    causal = _iota((A_CHUNK, A_CHUNK), 0) >= _iota((A_CHUNK, A_CHUNK), 1)
    wm = jnp.where(causal, w, 0.0)
    wb = jnp.broadcast_to(wm[None], (nc, A_CHUNK, A_CHUNK))
    z = _mm(wb, vn.reshape(nc, A_CHUNK, HD)) + b[None]
    return ug * z.reshape(T, HD)


def _sgu_specs():
    col = lambda off: pl.BlockSpec((T, HD), lambda g, off=off: (0, off + g))
    par = [pl.BlockSpec((None, 1, HD), lambda g: (g, 0, 0)),
           pl.BlockSpec((None, A_CHUNK, A_CHUNK), lambda g: (g, 0, 0)),
           pl.BlockSpec((None, A_CHUNK, 1), lambda g: (g, 0, 0))]
    return col, par


def sgu_fwd(p2, sg, w, b):
    col, par = _sgu_specs()

    def body(u_ref, v_ref, sg_ref, w_ref, b_ref, y_ref):
        y_ref[...] = _sgu_fn(u_ref[...], v_ref[...], sg_ref[...], w_ref[...], b_ref[...])

    return pl.pallas_call(
        body, name="sgu_fwd", grid=(A_GROUPS,),
        in_specs=[col(P_AU // HD), col(P_AV // HD)] + par,
        out_specs=pl.BlockSpec((T, HD), lambda g: (0, g)),
        out_shape=jax.ShapeDtypeStruct((T, A_WIDTH), F32),
        compiler_params=_cparams(("arbitrary",)),
    )(p2, p2, sg, w, b)


def sgu_bwd(p2, sg, w, b, dmix):
    col, par = _sgu_specs()

    def body(u_ref, v_ref, sg_ref, w_ref, b_ref, dy_ref, du_ref, dv_ref, dsg_ref, dw_ref, db_ref):
        _, vjp = jax.vjp(_sgu_fn, u_ref[...], v_ref[...], sg_ref[...], w_ref[...], b_ref[...])
        du, dv, dsg, dw, db = vjp(dy_ref[...])
        du_ref[...] = du
        dv_ref[...] = dv
        dsg_ref[...] = dsg
        dw_ref[...] = dw
        db_ref[...] = db

    gcol = pl.BlockSpec((T, HD), lambda g: (0, g))
    return pl.pallas_call(
        body, name="sgu_bwd", grid=(A_GROUPS,),
        in_specs=[col(P_AU // HD), col(P_AV // HD)] + par + [gcol],
        out_specs=[gcol, gcol] + par,
        out_shape=[jax.ShapeDtypeStruct((T, A_WIDTH), F32), jax.ShapeDtypeStruct((T, A_WIDTH), F32),
                   jax.ShapeDtypeStruct((A_GROUPS, 1, HD), F32), jax.ShapeDtypeStruct((A_GROUPS, A_CHUNK, A_CHUNK), F32),
                   jax.ShapeDtypeStruct((A_GROUPS, A_CHUNK, 1), F32)],
        compiler_params=_cparams(("arbitrary",)),
    )(p2, p2, sg, w, b, dmix)


def _attn_fn(q, k, v, qg, kg, slope, *, dil, nb):
    n = T // C_BLOCK
    qb = _rms(q, qg).reshape(n, C_BLOCK, HD)
    kb = _rms(k, kg).reshape(n, C_BLOCK, HD)
    vb = v.reshape(n, C_BLOCK, HD)
    scale = HD ** -0.5
    qi = _iota((n, C_BLOCK, C_BLOCK), 1)
    kj = _iota((n, C_BLOCK, C_BLOCK), 2)
    sl = slope[None] * float(dil)
    d_cur = qi - kj
    sc = jnp.where(d_cur >= 0, _mm(qb, kb, tb=True) * scale - sl * d_cur.astype(F32), -jnp.inf)
    mx = jnp.max(sc, axis=-1, keepdims=True)
    if nb > 1:
        kp = jnp.concatenate([jnp.zeros((1, C_BLOCK, HD), F32), kb[:-1]], axis=0)
        vp = jnp.concatenate([jnp.zeros((1, C_BLOCK, HD), F32), vb[:-1]], axis=0)
        has_prev = (_iota((n, C_BLOCK, C_BLOCK), 0) % nb) > 0
        d_prev = C_BLOCK + qi - kj
        sp = jnp.where((kj >= qi) & has_prev, _mm(qb, kp, tb=True) * scale - sl * d_prev.astype(F32), -jnp.inf)
        mx = jnp.maximum(mx, jnp.max(sp, axis=-1, keepdims=True))
    p = jnp.exp(sc - mx)
    den = jnp.sum(p, axis=-1, keepdims=True)
    if nb > 1:
        pp = jnp.exp(sp - mx)
        den = den + jnp.sum(pp, axis=-1, keepdims=True)
    out = _mm(p / den, vb)
    if nb > 1:
        out = out + _mm(pp / den, vp)
    lse = mx + jnp.log(den)
    return out.reshape(T, HD), jnp.broadcast_to(lse, (n, C_BLOCK, HD)).reshape(T, HD)


def _attn_specs():
    hcol = pl.BlockSpec((T, HD), lambda h: (0, h))
    row = pl.BlockSpec((1, HD), lambda h: (0, 0))
    slope = pl.BlockSpec((None, 1, HD), lambda h: (h, 0, 0))
    return hcol, row, slope


def attn_fwd(q, k, v, qg, kg, slopes, dil, nb):
    hcol, row, slope = _attn_specs()

    def body(q_ref, k_ref, v_ref, qg_ref, kg_ref, s_ref, o_ref, l_ref):
        o, l = _attn_fn(q_ref[...], k_ref[...], v_ref[...], qg_ref[...], kg_ref[...], s_ref[...], dil=dil, nb=nb)
        o_ref[...] = o
        l_ref[...] = l

    return pl.pallas_call(
        body, name=f"attn_fwd_d{dil}", grid=(C_HEADS,),
        in_specs=[hcol, hcol, hcol, row, row, slope], out_specs=[hcol, hcol],
        out_shape=[jax.ShapeDtypeStruct((T, C_WIDTH), F32)] * 2,
        compiler_params=_cparams(("arbitrary",)),
    )(q, k, v, qg, kg, slopes)


def attn_bwd(q, k, v, qg, kg, slopes, do, dl, dil, nb):
    hcol, row, slope = _attn_specs()

    def body(q_ref, k_ref, v_ref, qg_ref, kg_ref, s_ref, do_ref, dl_ref, dq_ref, dk_ref, dv_ref, dqg_ref, dkg_ref):
        fn = functools.partial(_attn_fn, dil=dil, nb=nb)
        _, vjp = jax.vjp(lambda a, b, c, d, e: fn(a, b, c, d, e, s_ref[...]),
                         q_ref[...], k_ref[...], v_ref[...], qg_ref[...], kg_ref[...])
        dq, dk, dv, dqg, dkg = vjp((do_ref[...], dl_ref[...]))
        dq_ref[...] = dq
        dk_ref[...] = dk
        dv_ref[...] = dv

        @pl.when(pl.program_id(0) == 0)
        def _():
            dqg_ref[...] = jnp.zeros_like(dqg_ref)
            dkg_ref[...] = jnp.zeros_like(dkg_ref)

        dqg_ref[...] += dqg
        dkg_ref[...] += dkg

    return pl.pallas_call(
        body, name=f"attn_bwd_d{dil}", grid=(C_HEADS,),
        in_specs=[hcol, hcol, hcol, row, row, slope, hcol, hcol], out_specs=[hcol, hcol, hcol, row, row],
        out_shape=[jax.ShapeDtypeStruct((T, C_WIDTH), F32)] * 3 + [jax.ShapeDtypeStruct((1, HD), F32)] * 2,
        compiler_params=_cparams(("arbitrary",)),
    )(q, k, v, qg, kg, slopes, do, dl)


def _combine_fn(o1, o2, o3, l1, l2, l3):
    mx = jnp.maximum(jnp.maximum(l1, l2), l3)
    e1, e2, e3 = jnp.exp(l1 - mx), jnp.exp(l2 - mx), jnp.exp(l3 - mx)
    s = e1 + e2 + e3
    return (e1 / s) * o1 + (e2 / s) * o2 + (e3 / s) * o3


_CMB_ROWS = 512


def combine_fwd(outs, lses):
    blk = pl.BlockSpec((_CMB_ROWS, HD), lambda i, h: (i, h))

    def body(o1, o2, o3, l1, l2, l3, y_ref):
        y_ref[...] = _combine_fn(o1[...], o2[...], o3[...], l1[...], l2[...], l3[...])

    return pl.pallas_call(
        body, name="combine_fwd", grid=(T // _CMB_ROWS, C_HEADS), in_specs=[blk] * 6, out_specs=blk,
        out_shape=jax.ShapeDtypeStruct((T, C_WIDTH), F32), compiler_params=_cparams(("arbitrary", "arbitrary")),
    )(*outs, *lses)


def combine_bwd(outs, lses, dmix):
    blk = pl.BlockSpec((_CMB_ROWS, HD), lambda i, h: (i, h))
    dblk = pl.BlockSpec((_CMB_ROWS, HD), lambda i, h: (i, (A_WIDTH + B_WIDTH) // HD + h))

    def body(o1, o2, o3, l1, l2, l3, dy_ref, *outs_ref):
        _, vjp = jax.vjp(_combine_fn, o1[...], o2[...], o3[...], l1[...], l2[...], l3[...])
        for r, g in zip(outs_ref, vjp(dy_ref[...])):
            r[...] = g

    return pl.pallas_call(
        body, name="combine_bwd", grid=(T // _CMB_ROWS, C_HEADS), in_specs=[blk] * 6 + [dblk], out_specs=[blk] * 6,
        out_shape=[jax.ShapeDtypeStruct((T, C_WIDTH), F32)] * 6, compiler_params=_cparams(("arbitrary", "arbitrary")),
    )(*outs, *lses, dmix)


_NCH = T // B_CHUNK


def _conv_taps(x, w_ref):
    rows = _iota(x.shape, 0)
    taps = []
    for j in range(B_CONV):
        s = B_CONV - 1 - j
        taps.append(x if s == 0 else jnp.where(rows >= s, pltpu.roll(x, s, 0), 0.0))
    pre = sum(w_ref[j:j + 1, :] * taps[j] for j in range(B_CONV))
    return pre, taps


def _conv_post(pre, mode):
    y = _silu(pre)
    if mode == "v":
        return y
    y = y * lax.rsqrt(jnp.sum(y * y, axis=-1, keepdims=True) + EPS)
    return y * (HD ** -0.5) if mode == "q" else y


def conv_fwd(p2, conv_w, mode):
    idx = "qkv".index(mode)
    xcol = pl.BlockSpec((T, HD), lambda h: (0, P_BQ // HD + B_HEADS * idx + h))
    wcol = pl.BlockSpec((B_CONV, HD), lambda h: (0, B_HEADS * idx + h))
    hcol = pl.BlockSpec((T, HD), lambda h: (0, h))

    def body(x_ref, w_ref, y_ref):
        pre, _ = _conv_taps(x_ref[...], w_ref)
        y_ref[...] = _conv_post(pre, mode)

    return pl.pallas_call(
        body, name=f"conv_fwd_{mode}", grid=(B_HEADS,), in_specs=[xcol, wcol], out_specs=hcol,
        out_shape=jax.ShapeDtypeStruct((T, B_WIDTH), F32), compiler_params=_cparams(("arbitrary",)),
    )(p2, conv_w)


def conv_bwd(p2, conv_w, dys, mode):
    idx = "qkv".index(mode)
    xcol = pl.BlockSpec((T, HD), lambda h: (0, P_BQ // HD + B_HEADS * idx + h))
    wcol = pl.BlockSpec((B_CONV, HD), lambda h: (0, B_HEADS * idx + h))
    hcol = pl.BlockSpec((T, HD), lambda h: (0, h))
    wout = pl.BlockSpec((B_CONV, HD), lambda h: (0, h))

    def body(x_ref, w_ref, *rest):
        dy_refs, (dx_ref, dw_ref) = rest[:-2], rest[-2:]
        pre, taps = _conv_taps(x_ref[...], w_ref)
        _, vjp = jax.vjp(functools.partial(_conv_post, mode=mode), pre)
        (dpre,) = vjp(sum(r[...] for r in dy_refs))
        rows = _iota(dpre.shape, 0)
        dx = w_ref[B_CONV - 1:B_CONV, :] * dpre
        for j in range(B_CONV):
            s = B_CONV - 1 - j
            dw_ref[j:j + 1, :] = jnp.sum(dpre * taps[j], axis=0, keepdims=True)
            if s > 0:
                dx = dx + w_ref[j:j + 1, :] * jnp.where(rows < T - s, pltpu.roll(dpre, T - s, 0), 0.0)
        dx_ref[...] = dx

    return pl.pallas_call(
        body, name=f"conv_bwd_{mode}", grid=(B_HEADS,), in_specs=[xcol, wcol] + [hcol] * len(dys), out_specs=[hcol, wout],
        out_shape=[jax.ShapeDtypeStruct((T, B_WIDTH), F32), jax.ShapeDtypeStruct((B_CONV, B_WIDTH), F32)],
        compiler_params=_cparams(("arbitrary",)),
    )(p2, conv_w, *dys)


def _gates_fn(bg, al, dtb, h):
    r = _iota((HD, HD), 0)
    logit = _mm(bg, (r == h).astype(F32), hi=True)
    a = _mm(bg, (r == h + B_HEADS).astype(F32), hi=True)
    beta = _sigmoid(logit)
    graw = -jnp.exp(al) * _softplus(a + dtb)
    tri = (_iota((_NCH, B_CHUNK, B_CHUNK), 1) >= _iota((_NCH, B_CHUNK, B_CHUNK), 2)).astype(F32)
    g = _mm(tri, graw.reshape(_NCH, B_CHUNK, HD), hi=True).reshape(T, HD)
    return beta, g


def _gates_specs():
    bg = pl.BlockSpec((T, HD), lambda h: (0, P_BB // HD))
    par = pl.BlockSpec((None, 1, HD), lambda h: (h, 0, 0))
    out = pl.BlockSpec((None, T, HD), lambda h: (h, 0, 0))
    return bg, par, out


def gates_fwd(p2, al, dtb):
    bg, par, out = _gates_specs()

    def body(bg_ref, al_ref, dtb_ref, beta_ref, g_ref):
        beta, g = _gates_fn(bg_ref[...], al_ref[...], dtb_ref[...], pl.program_id(0))
        beta_ref[...] = beta
        g_ref[...] = g

    return pl.pallas_call(
        body, name="gates_fwd", grid=(B_HEADS,), in_specs=[bg, par, par], out_specs=[out, out],
        out_shape=[jax.ShapeDtypeStruct((B_HEADS, T, HD), F32)] * 2, compiler_params=_cparams(("arbitrary",)),
    )(p2, al, dtb)


def gates_bwd(p2, al, dtb, dbeta, dg1, dg2):
    bg, par, out = _gates_specs()
    acc = pl.BlockSpec((T, HD), lambda h: (0, 0))

    def body(bg_ref, al_ref, dtb_ref, dbeta_ref, dg1_ref, dg2_ref, dbg_ref, dal_ref, ddtb_ref):
        h = pl.program_id(0)
        _, vjp = jax.vjp(lambda a, b, c: _gates_fn(a, b, c, h), bg_ref[...], al_ref[...], dtb_ref[...])
        dbg, dal, ddtb = vjp((dbeta_ref[...], dg1_ref[...] + dg2_ref[...]))

        @pl.when(h == 0)
        def _():
            dbg_ref[...] = jnp.zeros_like(dbg_ref)

        dbg_ref[...] += dbg
        dal_ref[...] = jnp.broadcast_to(jnp.sum(dal, axis=-1, keepdims=True), (1, HD))
        ddtb_ref[...] = jnp.broadcast_to(jnp.sum(ddtb, axis=-1, keepdims=True), (1, HD))

    return pl.pallas_call(
        body, name="gates_bwd", grid=(B_HEADS,), in_specs=[bg, par, par, out, out, out], out_specs=[acc, par, par],
        out_shape=[jax.ShapeDtypeStruct((T, HD), F32)] + [jax.ShapeDtypeStruct((B_HEADS, 1, HD), F32)] * 2,
        compiler_params=_cparams(("arbitrary",)),
    )(p2, al, dtb, dbeta, dg1, dg2)


def _unit_lower_inverse(a):
    eye = (_iota(a.shape, 1) == _iota(a.shape, 2)).astype(F32)
    x = eye - a
    p = _mm(a, a, hi=True)
    for i in range(5):
        x = x + _mm(x, p, hi=True)
        if i < 4:
            p = _mm(p, p, hi=True)
    return x


_WY_CH = 8
_WY_ROWS = _WY_CH * B_CHUNK


def _wy_fn(q, k, v, beta, g):
    sh = (q.shape[0] // B_CHUNK, B_CHUNK, HD)
    q3, k3, v3, b3, g3 = (t.reshape(sh) for t in (q, k, v, beta, g))
    gd = g3[:, :, :B_CHUNK] - jnp.swapaxes(g3, 1, 2)[:, :B_CHUNK, :]
    ii, jj = _iota(gd.shape, 1), _iota(gd.shape, 2)
    decay = jnp.exp(jnp.where(ii >= jj, gd, -jnp.inf))
    kb = k3 * b3
    a = _mm(kb, k3, tb=True) * jnp.where(ii > jj, decay, 0.0)
    tinv = _unit_lower_inverse(a)
    u = _mm(tinv, v3 * b3, hi=True)
    w = _mm(tinv, kb * jnp.exp(g3), hi=True)
    attn = _mm(q3, k3, tb=True) * decay
    return u.reshape(q.shape), w.reshape(q.shape), attn


def _wy_specs():
    hcol = pl.BlockSpec((_WY_ROWS, HD), lambda h, i: (i, h))
    hb = pl.BlockSpec((None, _WY_ROWS, HD), lambda h, i: (h, i, 0))
    at = pl.BlockSpec((None, _WY_CH, B_CHUNK, B_CHUNK), lambda h, i: (h, i, 0, 0))
    return hcol, hb, at


_WY_GRID = (B_HEADS, _NCH // _WY_CH)


def wy_fwd(q, k, v, beta, g):
    hcol, hb, at = _wy_specs()

    def body(q_ref, k_ref, v_ref, b_ref, g_ref, u_ref, w_ref, a_ref):
        u, w, a = _wy_fn(q_ref[...], k_ref[...], v_ref[...], b_ref[...], g_ref[...])
        u_ref[...] = u
        w_ref[...] = w
        a_ref[...] = a

    return pl.pallas_call(
        body, name="wy_fwd", grid=_WY_GRID, in_specs=[hcol, hcol, hcol, hb, hb], out_specs=[hcol, hcol, at],
        out_shape=[jax.ShapeDtypeStruct((T, B_WIDTH), F32)] * 2 + [jax.ShapeDtypeStruct((B_HEADS, _NCH, B_CHUNK, B_CHUNK), F32)],
        compiler_params=_cparams(("arbitrary", "arbitrary")),
    )(q, k, v, beta, g)


def wy_bwd(q, k, v, beta, g, du, dw, dattn):
    hcol, hb, at = _wy_specs()

    def body(q_ref, k_ref, v_ref, b_ref, g_ref, du_ref, dw_ref, da_ref, dq_ref, dk_ref, dv_ref, db_ref, dg_ref):
        _, vjp = jax.vjp(_wy_fn, q_ref[...], k_ref[...], v_ref[...], b_ref[...], g_ref[...])
        for r, t in zip((dq_ref, dk_ref, dv_ref, db_ref, dg_ref), vjp((du_ref[...], dw_ref[...], da_ref[...]))):
            r[...] = t

    return pl.pallas_call(
        body, name="wy_bwd", grid=_WY_GRID, in_specs=[hcol, hcol, hcol, hb, hb, hcol, hcol, at],
        out_specs=[hcol, hcol, hcol, hb, hb],
        out_shape=[jax.ShapeDtypeStruct((T, B_WIDTH), F32)] * 3 + [jax.ShapeDtypeStruct((B_HEADS, T, HD), F32)] * 2,
        compiler_params=_cparams(("arbitrary", "arbitrary")),
    )(q, k, v, beta, g, du, dw, dattn)


def _scan_step_fn(q, k, u, w, g, attn, gate, og, s):
    v_new = u - _mm(w, s)
    o = _mm(q * jnp.exp(g), s) + _mm(attn, v_new)
    g_last = jnp.sum(jnp.where(_iota(g.shape, 0) == B_CHUNK - 1, g, 0.0), axis=0, keepdims=True)
    s_new = s * jnp.exp(g_last) + _mm(k * jnp.exp(g_last - g), v_new, ta=True)
    return _rms(o, og) * _silu(gate), s_new


def _scan_specs(rev):
    ch = (lambda n: _NCH - 1 - n) if rev else (lambda n: n)
    rows = pl.BlockSpec((B_CHUNK, B_WIDTH), lambda n: (ch(n), 0))
    gb = pl.BlockSpec((B_HEADS, B_CHUNK, HD), lambda n: (0, ch(n), 0))
    at = pl.BlockSpec((B_HEADS, None, B_CHUNK, B_CHUNK), lambda n: (0, ch(n), 0, 0))
    og = pl.BlockSpec((1, HD), lambda n: (0, 0))
    st = pl.BlockSpec((None, B_HEADS, HD, HD), lambda n: (ch(n), 0, 0, 0))
    return rows, gb, at, og, st


def scan_fwd(q, k, u, w, g, attn, gate, og):
    rows, gb, at, ogs, st = _scan_specs(False)

    def body(q_ref, k_ref, u_ref, w_ref, g_ref, a_ref, gate_ref, og_ref, y_ref, st_ref, s_ref):
        @pl.when(pl.program_id(0) == 0)
        def _():
            s_ref[...] = jnp.zeros_like(s_ref)

        for h in range(B_HEADS):
            c = slice(h * HD, (h + 1) * HD)
            s = s_ref[h]
            st_ref[h] = s
            y, s_new = _scan_step_fn(q_ref[:, c], k_ref[:, c], u_ref[:, c], w_ref[:, c], g_ref[h], a_ref[h],
                                     gate_ref[:, c], og_ref[...], s)
            y_ref[:, c] = y
            s_ref[h] = s_new

    return pl.pallas_call(
        body, name="scan_fwd", grid=(_NCH,), in_specs=[rows, rows, rows, rows, gb, at, rows, ogs], out_specs=[rows, st],
        out_shape=[jax.ShapeDtypeStruct((T, B_WIDTH), F32), jax.ShapeDtypeStruct((_NCH, B_HEADS, HD, HD), F32)],
        scratch_shapes=[pltpu.VMEM((B_HEADS, HD, HD), F32)], compiler_params=_cparams(("arbitrary",)),
    )(q, k, u, w, g, attn, gate, og)


def scan_bwd(q, k, u, w, g, attn, gate, og, states, dmix):
    rows, gb, at, ogs, st = _scan_specs(True)
    dyb = pl.BlockSpec((B_CHUNK, HD), lambda n: (_NCH - 1 - n, 0))

    def body(q_ref, k_ref, u_ref, w_ref, g_ref, a_ref, gate_ref, og_ref, st_ref, *rest):
        dy_refs, (dq_ref, dk_ref, du_ref, dw_ref, dgate_ref, dg_ref, da_ref, dog_ref, ds_ref) = rest[:B_HEADS], rest[B_HEADS:]

        @pl.when(pl.program_id(0) == 0)
        def _():
            ds_ref[...] = jnp.zeros_like(ds_ref)
            dog_ref[...] = jnp.zeros_like(dog_ref)

        for h in range(B_HEADS):
            c = slice(h * HD, (h + 1) * HD)
            _, vjp = jax.vjp(_scan_step_fn, q_ref[:, c], k_ref[:, c], u_ref[:, c], w_ref[:, c], g_ref[h], a_ref[h],
                             gate_ref[:, c], og_ref[...], st_ref[h])
            dq, dk, du, dw, dg, da, dgate, dog, ds = vjp((dy_refs[h][...], ds_ref[h]))
            dq_ref[:, c] = dq
            dk_ref[:, c] = dk
            du_ref[:, c] = du
            dw_ref[:, c] = dw
            dgate_ref[:, c] = dgate
            dg_ref[h] = dg
            da_ref[h] = da
            dog_ref[...] += dog
            ds_ref[h] = ds

    dy_specs = [pl.BlockSpec((B_CHUNK, HD), lambda n, h=h: (_NCH - 1 - n, A_WIDTH // HD + h)) for h in range(B_HEADS)]
    return pl.pallas_call(
        body, name="scan_bwd", grid=(_NCH,),
        in_specs=[rows, rows, rows, rows, gb, at, rows, ogs, st] + dy_specs,
        out_specs=[rows] * 5 + [gb, at, ogs],
        out_shape=[jax.ShapeDtypeStruct((T, B_WIDTH), F32)] * 5
        + [jax.ShapeDtypeStruct((B_HEADS, T, HD), F32), jax.ShapeDtypeStruct((B_HEADS, _NCH, B_CHUNK, B_CHUNK), F32),
           jax.ShapeDtypeStruct((1, HD), F32)],
        scratch_shapes=[pltpu.VMEM((B_HEADS, HD, HD), F32)], compiler_params=_cparams(("arbitrary",)),
    )(q, k, u, w, g, attn, gate, og, states, *([dmix] * B_HEADS))


def _lanes(vec):
    return jnp.broadcast_to(vec[:, None, None], (vec.shape[0], 1, HD))


def gdn_forward(p2, conv_w, a_log, dt_bias, og):
    qa, ka, va = (conv_fwd(p2, conv_w, m) for m in "qkv")
    beta, g = gates_fwd(p2, _lanes(a_log), _lanes(dt_bias))
    u, w, attn = wy_fwd(qa, ka, va, beta, g)
    gate = p2[:, P_BG:P_BB]
    y, states = scan_fwd(qa, ka, u, w, g, attn, gate, og)
    return y, (qa, ka, va, beta, g, u, w, attn, gate, states)


def gdn_backward(p2, conv_w, a_log, dt_bias, og, saved, dmix):
    qa, ka, va, beta, g, u, w, attn, gate, states = saved
    dq1, dk1, du, dw, dgate, dg1, dattn, dog = scan_bwd(qa, ka, u, w, g, attn, gate, og, states, dmix)
    dq2, dk2, dv, dbeta, dg2 = wy_bwd(qa, ka, va, beta, g, du, dw, dattn)
    dbg, dal, ddtb = gates_bwd(p2, _lanes(a_log), _lanes(dt_bias), dbeta, dg1, dg2)
    dxq, dwq = conv_bwd(p2, conv_w, [dq1, dq2], "q")
    dxk, dwk = conv_bwd(p2, conv_w, [dk1, dk2], "k")
    dxv, dwv = conv_bwd(p2, conv_w, [dv], "v")
    dseg = jnp.concatenate([dxq, dxk, dxv, dgate, dbg], axis=1)
    return dseg, jnp.concatenate([dwq, dwk, dwv], axis=1), dal[:, 0, 0], ddtb[:, 0, 0], dog


def _to_branch_order(x, dil):
    return x if dil == 1 else x.reshape(T // dil, dil, -1).transpose(1, 0, 2).reshape(T, -1)


def _from_branch_order(x, dil):
    return x if dil == 1 else x.reshape(dil, T // dil, -1).transpose(1, 0, 2).reshape(T, -1)


_SLOPES = np.exp2(-8.0 * (np.arange(C_HEADS, dtype=np.float64) + 1.0) / C_HEADS).astype(np.float32)


def _branch_blocks(dil):
    return -(-(T // dil) // C_BLOCK)


def dattn_forward(p2, qg, kg):
    slopes = _lanes(jnp.asarray(_SLOPES))
    q, k, v = p2[:, P_CQ:P_CK], p2[:, P_CK:P_CV], p2[:, P_CV:P_END]
    qkvs, outs, lses = [], [], []
    for _, dil in C_BRANCHES:
        qkv = tuple(_to_branch_order(t, dil) for t in (q, k, v))
        o, l = attn_fwd(*qkv, qg, kg, slopes, dil, _branch_blocks(dil))
        qkvs.append(qkv)
        outs.append(_from_branch_order(o, dil))
        lses.append(_from_branch_order(l, dil))
    return combine_fwd(outs, lses), (qkvs, outs, lses)


def dattn_backward(qg, kg, saved, dmix):
    slopes = _lanes(jnp.asarray(_SLOPES))
    qkvs, outs, lses = saved
    gs = combine_bwd(outs, lses, dmix)
    dqs, dks, dvs, dqgs, dkgs = [], [], [], [], []
    for i, (_, dil) in enumerate(C_BRANCHES):
        do, dl = _to_branch_order(gs[i], dil), _to_branch_order(gs[3 + i], dil)
        dq, dk, dv, dqg, dkg = attn_bwd(*qkvs[i], qg, kg, slopes, do, dl, dil, _branch_blocks(dil))
        dqs.append(_from_branch_order(dq, dil))
        dks.append(_from_branch_order(dk, dil))
        dvs.append(_from_branch_order(dv, dil))
        dqgs.append(dqg)
        dkgs.append(dkg)
    return dqs, dks, dvs, dqgs, dkgs


def sum_branches(big, rows):
    blk = pl.BlockSpec((_CMB_ROWS, C_WIDTH), lambda i: (i, 0))
    row = pl.BlockSpec((1, HD), lambda i: (0, 0))
    nb, nr = len(big) // 3, len(rows) // 3

    def body(*refs):
        ins, outs = refs[:len(big) + len(rows)], refs[len(big) + len(rows):]
        for j in range(nb + nr):
            outs[j][...] = ins[3 * j][...] + ins[3 * j + 1][...] + ins[3 * j + 2][...]

    return pl.pallas_call(
        body, name="sum_branches", grid=(T // _CMB_ROWS,), in_specs=[blk] * len(big) + [row] * len(rows),
        out_specs=[blk] * nb + [row] * nr,
        out_shape=[jax.ShapeDtypeStruct((T, C_WIDTH), F32)] * nb + [jax.ShapeDtypeStruct((1, HD), F32)] * nr,
        compiler_params=_cparams(("arbitrary",)),
    )(*big, *rows)


_ROWS = 256
_TM = 512


def _dep_specs(dep, ngrid):
    if dep is None:
        return [], []
    return [dep], [pl.BlockSpec((8, HD), lambda *_: (0, 0))]


def rmsnorm_fwd(x, g, dep=None):
    blk = pl.BlockSpec((_ROWS, D), lambda i: (i, 0))
    deps, dspecs = _dep_specs(dep, 1)

    def body(x_ref, g_ref, *rest):
        rest[-1][...] = _rms(x_ref[...], g_ref[...]).astype(BF16)

    return pl.pallas_call(
        body, name="rmsnorm_fwd", grid=(T // _ROWS,), in_specs=[blk, pl.BlockSpec((1, D), lambda i: (0, 0))] + dspecs,
        out_specs=blk, out_shape=jax.ShapeDtypeStruct((T, D), BF16), compiler_params=_cparams(("arbitrary",)),
    )(x, g, *deps)


def rmsnorm_bwd(x, g, dh, dres):
    blk = pl.BlockSpec((_ROWS, D), lambda i: (i, 0))
    row = pl.BlockSpec((1, D), lambda i: (0, 0))

    def body(x_ref, g_ref, dh_ref, dres_ref, dx_ref, dg_ref):
        _, vjp = jax.vjp(_rms, x_ref[...], g_ref[...])
        dx, dg = vjp(dh_ref[...])
        dx_ref[...] = dres_ref[...] + dx

        @pl.when(pl.program_id(0) == 0)
        def _():
            dg_ref[...] = jnp.zeros_like(dg_ref)

        dg_ref[...] += dg

    return pl.pallas_call(
        body, name="rmsnorm_bwd", grid=(T // _ROWS,), in_specs=[blk, row, blk, blk], out_specs=[blk, row],
        out_shape=[jax.ShapeDtypeStruct((T, D), F32), jax.ShapeDtypeStruct((1, D), F32)],
        compiler_params=_cparams(("arbitrary",)),
    )(x, g, dh, dres)


def _matmul(name, a, b, *, grid, a_spec, b_spec, o_spec, out_shape, ta=False, tb=False, k_axis=None, res=None, dep=None):
    dims = _dims(2, ta, tb)
    deps, dspecs = _dep_specs(dep, len(grid))

    def body(a_ref, b_ref, *rest):
        o_ref = rest[-1]
        prod = lax.dot_general(a_ref[...].astype(BF16), b_ref[...].astype(BF16), dims, preferred_element_type=F32)
        if res is not None:
            prod = prod + rest[0][...]
        if k_axis is None:
            o_ref[...] = prod.astype(o_ref.dtype)
        else:
            @pl.when(pl.program_id(k_axis) == 0)
            def _():
                o_ref[...] = prod

            @pl.when(pl.program_id(k_axis) > 0)
            def _():
                o_ref[...] += prod

    sem = tuple("arbitrary" for _ in grid)
    ins = [a, b] + ([res] if res is not None else []) + deps
    specs = [a_spec, b_spec] + ([o_spec] if res is not None else []) + dspecs
    return pl.pallas_call(
        body, name=name, grid=grid, in_specs=specs, out_specs=o_spec, out_shape=out_shape, compiler_params=_cparams(sem),
    )(*ins)


_IN_TN = P_END // 3


def mm_proj(h1, wp_in, l):
    return _matmul(
        "mm_proj", h1, wp_in, grid=(P_END // _IN_TN, T // _TM),
        a_spec=pl.BlockSpec((_TM, D), lambda j, i: (i, 0)),
        b_spec=pl.BlockSpec((None, D, _IN_TN), lambda j, i: (l, 0, j)),
        o_spec=pl.BlockSpec((_TM, _IN_TN), lambda j, i: (i, j)), out_shape=jax.ShapeDtypeStruct((T, P_END), F32))


def mm_dh1(dp2, wp_in, l):
    return _matmul(
        "mm_dh1", dp2, wp_in, grid=(T // _TM, P_END // _IN_TN), tb=True, k_axis=1,
        a_spec=pl.BlockSpec((_TM, _IN_TN), lambda i, k: (i, k)),
        b_spec=pl.BlockSpec((None, D, _IN_TN), lambda i, k: (l, 0, k)),
        o_spec=pl.BlockSpec((_TM, D), lambda i, k: (i, 0)), out_shape=jax.ShapeDtypeStruct((T, D), F32))


def mm_dwin(h1, dp2):
    return _matmul(
        "mm_dwin", h1, dp2, grid=(P_END // _IN_TN, D // _TM), ta=True,
        a_spec=pl.BlockSpec((T, _TM), lambda j, i: (0, i)),
        b_spec=pl.BlockSpec((T, _IN_TN), lambda j, i: (0, j)),
        o_spec=pl.BlockSpec((_TM, _IN_TN), lambda j, i: (i, j)), out_shape=jax.ShapeDtypeStruct((D, P_END), BF16))


def _mm_square(name, a, w, l, res, tb):
    tn = 1024
    b_spec = (pl.BlockSpec((None, tn, D), lambda j, i: (l, j, 0)) if tb else pl.BlockSpec((None, D, tn), lambda j, i: (l, 0, j)))
    return _matmul(
        name, a, w, grid=(D // tn, T // _TM), tb=tb, res=res,
        a_spec=pl.BlockSpec((_TM, D), lambda j, i: (i, 0)), b_spec=b_spec,
        o_spec=pl.BlockSpec((_TM, tn), lambda j, i: (i, j)), out_shape=jax.ShapeDtypeStruct((T, D), F32))


def mm_out(mix, wg_out, l, x):
    return _mm_square("mm_out", mix, wg_out, l, x, False)


def mm_dmix(dx1, wg_out, l):
    return _mm_square("mm_dmix", dx1, wg_out, l, None, True)


def mm_dwout(mix, dx1):
    tn = 1024
    return _matmul(
        "mm_dwout", mix, dx1, grid=(D // tn, D // _TM), ta=True,
        a_spec=pl.BlockSpec((T, _TM), lambda j, i: (0, i)), b_spec=pl.BlockSpec((T, tn), lambda j, i: (0, j)),
        o_spec=pl.BlockSpec((_TM, tn), lambda j, i: (i, j)), out_shape=jax.ShapeDtypeStruct((D, D), BF16))


_GU_TN = GU_SHARD // 2


def mm_gu(h2, wg_gu, l):
    return _matmul(
        "mm_gu", h2, wg_gu, grid=(N_CHIPS, 2, T // _TM),
        a_spec=pl.BlockSpec((_TM, D), lambda s, j, i: (i, 0)),
        b_spec=pl.BlockSpec((None, None, D, _GU_TN), lambda s, j, i: (l, s, 0, j)),
        o_spec=pl.BlockSpec((_TM, _GU_TN), lambda s, j, i: (i, 2 * s + j)),
        out_shape=jax.ShapeDtypeStruct((T, 2 * FFN), F32))


_GU_NJ = FFN // _GU_TN


def mm_dh2(dgu, wg_gu, l):
    return _matmul(
        "mm_dh2", dgu, wg_gu, grid=(T // _TM, 2 * N_CHIPS), tb=True, k_axis=1,
        a_spec=pl.BlockSpec((None, _TM, _GU_TN), lambda i, k: (k // _GU_NJ, i, k % _GU_NJ)),
        b_spec=pl.BlockSpec((None, None, D, _GU_TN), lambda i, k: (l, k // 2, 0, k % 2)),
        o_spec=pl.BlockSpec((_TM, D), lambda i, k: (i, 0)), out_shape=jax.ShapeDtypeStruct((T, D), F32))


def mm_dwgu(h2, dgu):
    return _matmul(
        "mm_dwgu", h2, dgu, grid=(N_CHIPS, 2, D // _TM), ta=True,
        a_spec=pl.BlockSpec((T, _TM), lambda s, j, i: (0, i)),
        b_spec=pl.BlockSpec((None, T, _GU_TN), lambda s, j, i: ((2 * s + j) // _GU_NJ, 0, (2 * s + j) % _GU_NJ)),
        o_spec=pl.BlockSpec((None, _TM, _GU_TN), lambda s, j, i: (s, i, j)),
        out_shape=jax.ShapeDtypeStruct((N_CHIPS, D, GU_SHARD), BF16))


def mm_down(act, wg_down, l, x1):
    tn = 512
    return _matmul(
        "mm_down", act, wg_down, grid=(D // tn, T // _TM), res=x1,
        a_spec=pl.BlockSpec((_TM, FFN), lambda j, i: (i, 0)),
        b_spec=pl.BlockSpec((None, FFN, tn), lambda j, i: (l, 0, j)),
        o_spec=pl.BlockSpec((_TM, tn), lambda j, i: (i, j)), out_shape=jax.ShapeDtypeStruct((T, D), F32))


def mm_dact(dx2, wg_down, l, dep=None):
    tn = DOWN_SHARD
    return _matmul(
        "mm_dact", dx2, wg_down, grid=(FFN // tn, T // _TM), tb=True, dep=dep,
        a_spec=pl.BlockSpec((_TM, D), lambda j, i: (i, 0)),
        b_spec=pl.BlockSpec((None, tn, D), lambda j, i: (l, j, 0)),
        o_spec=pl.BlockSpec((_TM, tn), lambda j, i: (i, j)), out_shape=jax.ShapeDtypeStruct((T, FFN), F32))


def mm_dwdown(act, dx2):
    tm, tn = DOWN_SHARD, 512
    return _matmul(
        "mm_dwdown", act, dx2, grid=(D // tn, FFN // tm), ta=True,
        a_spec=pl.BlockSpec((T, tm), lambda j, i: (0, i)), b_spec=pl.BlockSpec((T, tn), lambda j, i: (0, j)),
        o_spec=pl.BlockSpec((tm, tn), lambda j, i: (i, j)), out_shape=jax.ShapeDtypeStruct((FFN, D), BF16))


_FF_TN = 1408


def _swiglu_fn(gt, up):
    return _silu(gt) * up


def swiglu_fwd(gu):
    nj = FFN // _FF_TN
    gt = pl.BlockSpec((_ROWS, _FF_TN), lambda i, j: (i, j))
    up = pl.BlockSpec((_ROWS, _FF_TN), lambda i, j: (i, nj + j))

    def body(gt_ref, up_ref, o_ref):
        o_ref[...] = _swiglu_fn(gt_ref[...], up_ref[...]).astype(BF16)

    return pl.pallas_call(
        body, name="swiglu_fwd", grid=(T // _ROWS, nj), in_specs=[gt, up], out_specs=gt,
        out_shape=jax.ShapeDtypeStruct((T, FFN), BF16), compiler_params=_cparams(("arbitrary", "arbitrary")),
    )(gu, gu)


def swiglu_bwd(gu, dact):
    nj = FFN // _FF_TN
    gt = pl.BlockSpec((_ROWS, _FF_TN), lambda i, j: (i, j))
    up = pl.BlockSpec((_ROWS, _FF_TN), lambda i, j: (i, nj + j))

    def body(gt_ref, up_ref, d_ref, dgu_ref):
        _, vjp = jax.vjp(_swiglu_fn, gt_ref[...], up_ref[...])
        dgt, dup = vjp(d_ref[...])
        dgu_ref[0] = dgt.astype(BF16)
        dgu_ref[1] = dup.astype(BF16)

    return pl.pallas_call(
        body, name="swiglu_bwd", grid=(T // _ROWS, nj), in_specs=[gt, up, gt],
        out_specs=pl.BlockSpec((2, _ROWS, _FF_TN), lambda i, j: (0, i, j)),
        out_shape=jax.ShapeDtypeStruct((2, T, FFN), BF16), compiler_params=_cparams(("arbitrary", "arbitrary")),
    )(gu, gu, dact)


def loss_and_grad(y, target):
    blk = pl.BlockSpec((_ROWS, D), lambda i: (i, 0))
    acc = pl.BlockSpec((8, HD), lambda i: (0, 0))

    def body(y_ref, t_ref, dy_ref, l_ref):
        err = y_ref[...] - t_ref[...]
        dy_ref[...] = err * (1.0 / D)

        @pl.when(pl.program_id(0) == 0)
        def _():
            l_ref[...] = jnp.zeros_like(l_ref)

        l_ref[...] += (0.5 / D) * jnp.sum(err * err)

    return pl.pallas_call(
        body, name="loss_and_grad", grid=(T // _ROWS,), in_specs=[blk, blk], out_specs=[blk, acc],
        out_shape=[jax.ShapeDtypeStruct((T, D), F32), jax.ShapeDtypeStruct((8, HD), F32)],
        compiler_params=_cparams(("arbitrary",)),
    )(y, target)


def adamw(w, g, m, v, name):
    rows, cols = w.shape
    tr = _ROWS if rows % _ROWS == 0 else rows
    blk = pl.BlockSpec((tr, cols), lambda i: (i, 0))

    def body(w_ref, g_ref, m_ref, v_ref, d_ref, nm_ref, nv_ref):
        gg = g_ref[...]
        nm = ADAM_B1 * m_ref[...] + (1.0 - ADAM_B1) * gg
        nv = ADAM_B2 * v_ref[...] + (1.0 - ADAM_B2) * (gg * gg)
        m_hat = nm / (1.0 - ADAM_B1 ** ADAM_STEP)
        v_hat = nv / (1.0 - ADAM_B2 ** ADAM_STEP)
        d_ref[...] = -ADAM_LR * (m_hat / (jnp.sqrt(v_hat) + ADAM_EPS) + ADAM_WD * w_ref[...])
        nm_ref[...] = nm
        nv_ref[...] = nv

    return pl.pallas_call(
        body, name=name, grid=(rows // tr,), in_specs=[blk] * 4, out_specs=[blk] * 3,
        out_shape=[jax.ShapeDtypeStruct(w.shape, F32)] * 3, compiler_params=_cparams(("arbitrary",)),
    )(w, g, m, v)


def shards_to_segments(w):
    valid = jnp.concatenate([w[..., s, :, :IN_SHARD] for s in range(N_CHIPS)], axis=-1)
    pad = jnp.zeros(valid.shape[:-1] + (P_CQ - GATE_COLS,), w.dtype)
    return jnp.concatenate([valid[..., :GATE_COLS], pad, valid[..., GATE_COLS:]], axis=-1)


def segments_to_shards(w):
    valid = jnp.concatenate([w[:, :GATE_COLS], w[:, P_CQ:]], axis=1)
    pad = ((0, 0), (0, IN_SHARD_PAD - IN_SHARD))
    return jnp.stack([jnp.pad(valid[:, s * IN_SHARD:(s + 1) * IN_SHARD], pad) for s in range(N_CHIPS)])


def layer_forward(x, wg, sp, dep=None):
    l = 0
    h1 = rmsnorm_fwd(x, sp["norm1_g"], dep)
    p2 = mm_proj(h1, wg["in"], l)
    y_a = sgu_fwd(p2, sp["sgu_norm_g"], sp["w_spatial"], sp["b_spatial"])
    y_b, saved_b = gdn_forward(p2, sp["conv_w"], sp["a_log"], sp["dt_bias"], sp["o_norm_g"])
    y_c, saved_c = dattn_forward(p2, sp["q_norm_g"], sp["k_norm_g"])
    mix = jnp.concatenate([y_a, y_b, y_c], axis=1)
    x1 = mm_out(mix, wg["out"], l, x)
    h2 = rmsnorm_fwd(x1, sp["norm2_g"])
    gu = mm_gu(h2, wg["gu"], l)
    act = swiglu_fwd(gu)
    x2 = mm_down(act, wg["down"], l, x1)
    return x2, (x, h1, p2, saved_b, saved_c, mix, x1, h2, gu, act)


def layer_backward(dx2, wg, sp, saved, dep=None):
    l = 0
    x, h1, p2, saved_b, saved_c, mix, x1, h2, gu, act = saved
    dact = mm_dact(dx2, wg["down"], l, dep)
    dw_down = mm_dwdown(act, dx2)
    dgu = swiglu_bwd(gu, dact)
    dw_gu = mm_dwgu(h2, dgu)
    dh2 = mm_dh2(dgu, wg["gu"], l)
    dx1, dnorm2 = rmsnorm_bwd(x1, sp["norm2_g"], dh2, dx2)
    dmix = mm_dmix(dx1, wg["out"], l)
    dw_out = mm_dwout(mix, dx1)
    du, dv, dsg, dws, dbs = sgu_bwd(p2, sp["sgu_norm_g"], sp["w_spatial"], sp["b_spatial"], dmix)
    dseg_b, dconv, dal, ddtb, dog = gdn_backward(p2, sp["conv_w"], sp["a_log"], sp["dt_bias"], sp["o_norm_g"], saved_b, dmix)
    dqs, dks, dvs, dqgs, dkgs = dattn_backward(sp["q_norm_g"], sp["k_norm_g"], saved_c, dmix)
    dcq, dck, dcv, dqg, dkg = sum_branches(dqs + dks + dvs, dqgs + dkgs)
    dp2 = jnp.concatenate([du, dv, dseg_b, dcq, dck, dcv], axis=1).astype(BF16)
    dw_in = segments_to_shards(mm_dwin(h1, dp2))
    dh1 = mm_dh1(dp2, wg["in"], l)
    dx, dnorm1 = rmsnorm_bwd(x, sp["norm1_g"], dh1, dx1)
    big = {"in": dw_in, "out": dw_out, "gu": dw_gu, "down": dw_down}
    small = {"norm1_g": dnorm1, "sgu_norm_g": dsg, "w_spatial": dws, "b_spatial": dbs, "conv_w": dconv, "a_log": dal,
             "dt_bias": ddtb, "o_norm_g": dog, "q_norm_g": dqg, "k_norm_g": dkg, "norm2_g": dnorm2}
    return dx, big, small


_HBM = pl.BlockSpec(memory_space=pltpu.HBM)
_MESH = pl.DeviceIdType.MESH


def _place():
    x, y, c = lax.axis_index("x"), lax.axis_index("y"), lax.axis_index("c")
    chips = [(1 - x, y), (x, 1 - y), (1 - x, 1 - y)]
    return x, y, c, chips


def _rcopy(src, dst, ssem, rsem, dev):
    return pltpu.make_async_remote_copy(src_ref=src, dst_ref=dst, send_sem=ssem, recv_sem=rsem, device_id=dev,
                                        device_id_type=_MESH)


def _xor(a, b):
    return a + b - 2 * a * b


def gather_weights(shards):
    n = len(shards)

    def body(*refs):
        ins, outs = refs[:n], refs[n:2 * n]
        s_ici, r_ici, s_d2d, r_d2d, s_own, r_own = refs[2 * n:]
        x, y, c, _ = _place()
        s = 2 * x + y
        sibling = (x, y, 1 - c)
        nbr = [(1 - x, y), (x, 1 - y)]
        src_chip = (_xor(x, 1 - c), _xor(y, c))
        dst_chip = (_xor(x, c), _xor(y, 1 - c))
        t_src = 2 * src_chip[0] + src_chip[1]
        t_oth = 2 * dst_chip[0] + dst_chip[1]
        t_dia = 2 * (1 - x) + (1 - y)
        sends = []
        for a in range(n):
            for u in range(2):
                sends.append(_rcopy(ins[a].at[u], outs[a].at[s, u], s_own.at[a, u], r_own.at[a, u], sibling))
            for k in range(2):
                sends.append(_rcopy(ins[a].at[c], outs[a].at[s, c], s_ici.at[a, k], r_ici.at[a, k], (*nbr[k], c)))
        for cp in sends:
            cp.start()

        def landed(a, t, k):
            _rcopy(ins[a].at[c], outs[a].at[t, c], s_ici.at[a, k], r_ici.at[a, k], sibling).wait_recv()
            cp = _rcopy(outs[a].at[t, c], outs[a].at[t, c], s_d2d.at[a, k], r_d2d.at[a, k], sibling)
            cp.start()
            sends.append(cp)

        for a in range(n):
            landed(a, t_src, c)
            fwd = _rcopy(outs[a].at[t_src, c], outs[a].at[t_src, c], s_ici.at[a, 2], r_ici.at[a, 2], (*dst_chip, c))
            fwd.start()
            sends.append(fwd)
        for a in range(n):
            landed(a, t_oth, 1 - c)
        for a in range(n):
            landed(a, t_dia, 2)
        for a in range(n):
            for u in range(2):
                _rcopy(ins[a].at[u], outs[a].at[s, u], s_own.at[a, u], r_own.at[a, u], sibling).wait_recv()
            for k, t in enumerate([2 * nbr[0][0] + nbr[0][1], 2 * nbr[1][0] + nbr[1][1], t_dia]):
                _rcopy(ins[a].at[1 - c], outs[a].at[t, 1 - c], s_d2d.at[a, k], r_d2d.at[a, k], sibling).wait_recv()
        for cp in sends:
            cp.wait_send()

    dma = lambda k: pltpu.SemaphoreType.DMA((n, k))
    return pl.pallas_call(
        body, name="gather_weights", in_specs=[_HBM] * n, out_specs=[_HBM] * n,
        out_shape=[jax.ShapeDtypeStruct((N_CHIPS,) + w.shape, w.dtype) for w in shards],
        scratch_shapes=[dma(3), dma(3), dma(3), dma(3), dma(2), dma(2)],
    )(*shards)


_SEM = pl.BlockSpec(memory_space=pltpu.SEMAPHORE)
_SIDE_EFFECT = pltpu.SideEffectType.DATAFLOW_SIDE_EFFECTING


def _in_hbm(a):
    return pltpu.with_memory_space_constraint(a, pltpu.HBM)


def _split_copy(name, srcs, land_shapes, n_sems, copies):
    n, m = len(srcs), len(land_shapes)
    thru = [pltpu.HBM(a.shape, a.dtype) for a in srcs] + [pltpu.HBM(s.shape, s.dtype) for s in land_shapes]
    sems = (pltpu.SemaphoreType.DMA((n_sems,)), pltpu.SemaphoreType.DMA((n_sems,)))

    def start():
        def body(*refs):
            ins, lands, ssem, rsem, token = refs[:n], refs[n:n + m], refs[n + m], refs[n + m + 1], refs[-1]
            for cp in copies(ins, lands, ssem, rsem)[0]:
                cp.start()
            token[...] = jnp.zeros_like(token)

        out = pl.pallas_call(
            body, name=name + "_start", out_shape=(*sems, *thru, jax.ShapeDtypeStruct((8, HD), F32)),
            in_specs=[_HBM] * (n + m), out_specs=(_SEM, _SEM, *[_HBM] * (n + m), pl.BlockSpec(memory_space=pltpu.VMEM)),
            input_output_aliases={i: 2 + i for i in range(n + m)},
            compiler_params=pltpu.CompilerParams(has_side_effects=_SIDE_EFFECT),
        )(*[_in_hbm(a) for a in srcs], *[_in_hbm(lax.empty(s.shape, s.dtype)) for s in land_shapes])
        return out[:-1], out[-1]

    def wait(state, after):
        def body(*refs):
            ins, lands, ssem, rsem = refs[:n], refs[n:n + m], refs[n + m], refs[n + m + 1]
            sent, arrivals = copies(ins, lands, ssem, rsem)
            for cp in sent:
                cp.wait_send()
            for cp in arrivals:
                cp.wait_recv()

        out = pl.pallas_call(
            body, name=name + "_wait", out_shape=tuple(thru),
            in_specs=[_HBM] * (n + m) + [_SEM, _SEM, pl.BlockSpec(memory_space=pl.ANY)], out_specs=[_HBM] * (n + m),
            input_output_aliases={i: i for i in range(n + m)},
            compiler_params=pltpu.CompilerParams(has_side_effects=_SIDE_EFFECT),
        )(*state[2:], state[0], state[1], after)
        return out[n:]

    return start, wait


def gather_direct(shards):
    n = len(shards)

    def copies(ins, lands, ssem, rsem):
        x, y, c, chips = _place()
        s = 2 * x + y
        sibling = (x, y, 1 - c)
        sent, arrivals = [], []
        for a in range(n):
            for u in range(2):
                cp = _rcopy(ins[a].at[u], lands[a].at[s, u], ssem.at[5 * a + u], rsem.at[5 * a + u], sibling)
                sent.append(cp)
                arrivals.append(cp)
            for j, (cx, cy) in enumerate(chips):
                k = 5 * a + 2 + j
                sent.append(_rcopy(ins[a].at[c], lands[a].at[s, c], ssem.at[k], rsem.at[k], (cx, cy, c)))
                arrivals.append(_rcopy(ins[a].at[c], lands[a].at[2 * cx + cy, c], ssem.at[k], rsem.at[k], (cx, cy, c)))
        return sent, arrivals

    lands = [jax.ShapeDtypeStruct((N_CHIPS,) + w.shape, w.dtype) for w in shards]
    return _split_copy("gather_direct", shards, lands, 5 * n, copies)


def pass_to_sibling(lands):
    n = len(lands)

    def body(*refs):
        ins = refs[:n]
        ssem, rsem = refs[2 * n:]
        x, y, c, chips = _place()
        sibling = (x, y, 1 - c)
        cps, arrivals = [], []
        for a in range(n):
            for j, (cx, cy) in enumerate(chips):
                t = 2 * cx + cy
                cps.append(_rcopy(ins[a].at[t, c], ins[a].at[t, c], ssem.at[a, j], rsem.at[a, j], sibling))
                arrivals.append(_rcopy(ins[a].at[t, c], ins[a].at[t, 1 - c], ssem.at[a, j], rsem.at[a, j], sibling))
        for cp in cps:
            cp.start()
        for cp, ar in zip(cps, arrivals):
            cp.wait_send()
            ar.wait_recv()

    return pl.pallas_call(
        body, name="pass_to_sibling", in_specs=[_HBM] * n, out_specs=[_HBM] * n,
        out_shape=[jax.ShapeDtypeStruct(a.shape, a.dtype) for a in lands], input_output_aliases={a: a for a in range(n)},
        scratch_shapes=[pltpu.SemaphoreType.DMA((n, 3)), pltpu.SemaphoreType.DMA((n, 3))],
    )(*lands)


def exchange_halves(grads):
    n = len(grads)

    def body(*refs):
        ins, outs = refs[:n], refs[n:2 * n]
        ssem, rsem = refs[2 * n:]
        x, y, c, _ = _place()
        cps = []
        for a in range(n):
            h = grads[a].shape[1] // 2
            cps.append(_rcopy(ins[a].at[:, pl.ds((1 - c) * h, h)], outs[a], ssem.at[a], rsem.at[a], (x, y, 1 - c)))
        for cp in cps:
            cp.start()
        for cp in cps:
            cp.wait()

    return pl.pallas_call(
        body, name="exchange_halves", in_specs=[_HBM] * n, out_specs=[_HBM] * n,
        out_shape=[jax.ShapeDtypeStruct((g.shape[0], g.shape[1] // 2, g.shape[2]), g.dtype) for g in grads],
        scratch_shapes=[pltpu.SemaphoreType.DMA((n,)), pltpu.SemaphoreType.DMA((n,))],
    )(*grads)


def scatter_to_chips(parts):
    n = len(parts)

    def body(*refs):
        ins, outs = refs[:n], refs[n:2 * n]
        ssem, rsem = refs[2 * n:]
        x, y, c, chips = _place()
        cps = [_rcopy(ins[a].at[2 * cx + cy], outs[a].at[j], ssem.at[a, j], rsem.at[a, j], (cx, cy, c))
               for a in range(n) for j, (cx, cy) in enumerate(chips)]
        for cp in cps:
            cp.start()
        for cp in cps:
            cp.wait()

    return pl.pallas_call(
        body, name="scatter_to_chips", in_specs=[_HBM] * n, out_specs=[_HBM] * n,
        out_shape=[jax.ShapeDtypeStruct((3,) + p.shape[1:], p.dtype) for p in parts],
        scratch_shapes=[pltpu.SemaphoreType.DMA((n, 3)), pltpu.SemaphoreType.DMA((n, 3))],
    )(*parts)


def scatter_direct(parts):
    n = len(parts)

    def copies(ins, lands, ssem, rsem):
        x, y, c, chips = _place()
        cps = [_rcopy(ins[a].at[2 * cx + cy], lands[a].at[j], ssem.at[3 * a + j], rsem.at[3 * a + j], (cx, cy, c))
               for a in range(n) for j, (cx, cy) in enumerate(chips)]
        return cps, cps

    lands = [jax.ShapeDtypeStruct((3,) + p.shape[1:], p.dtype) for p in parts]
    return _split_copy("scatter_direct", parts, lands, 3 * n, copies)


def share_halves(halves):
    n = len(halves)

    def body(*refs):
        ins, outs = refs[:n], refs[n:2 * n]
        ssem, rsem = refs[2 * n:]
        x, y, c, _ = _place()
        cps = [_rcopy(ins[i], outs[i], ssem.at[i], rsem.at[i], (x, y, 1 - c)) for i in range(n)]
        for cp in cps:
            cp.start()
        for cp in cps:
            cp.wait()

    return pl.pallas_call(
        body, name="share_halves", in_specs=[_HBM] * n, out_specs=[_HBM] * n,
        out_shape=[jax.ShapeDtypeStruct(h.shape, h.dtype) for h in halves],
        scratch_shapes=[pltpu.SemaphoreType.DMA((n,)), pltpu.SemaphoreType.DMA((n,))],
    )(*halves)


def adamw_shard(w, m, v, mine, theirs, c, name):
    _, r, cw = w.shape
    h, cg = mine[0].shape
    tr = next(t for t in (256, 176, 128) if h % t == 0 and t * cg * 4 <= (3 << 19))
    nb = h // tr
    wblk = pl.BlockSpec((None, tr, cw), lambda l, i, c_ref: (l, i, 0))
    gblk = pl.BlockSpec((tr, cg), lambda l, i, c_ref: (i % nb, 0))

    def body(c_ref, w_ref, m_ref, v_ref, m0, m1, t0, t1, g_ref, d_ref, nm_ref, nv_ref):
        is_mine = (pl.program_id(1) // nb) == c_ref[0]
        first = pl.program_id(0) == 0
        gg = jnp.where(is_mine, jnp.where(first, m0[:, :cw], m1[:, :cw]), jnp.where(first, t0[:, :cw], t1[:, :cw]))
        nm = ADAM_B1 * m_ref[...] + (1.0 - ADAM_B1) * gg
        nv = ADAM_B2 * v_ref[...] + (1.0 - ADAM_B2) * (gg * gg)
        m_hat = nm / (1.0 - ADAM_B1 ** ADAM_STEP)
        v_hat = nv / (1.0 - ADAM_B2 ** ADAM_STEP)
        g_ref[...] = gg
        d_ref[...] = -ADAM_LR * (m_hat / (jnp.sqrt(v_hat) + ADAM_EPS) + ADAM_WD * w_ref[...])
        nm_ref[...] = nm
        nv_ref[...] = nv

    return pl.pallas_call(
        body, name=name,
        grid_spec=pltpu.PrefetchScalarGridSpec(
            num_scalar_prefetch=1, grid=(DEPTH, r // tr), in_specs=[wblk] * 3 + [gblk] * 4, out_specs=[wblk] * 4),
        out_shape=[jax.ShapeDtypeStruct(w.shape, F32)] * 4, compiler_params=_cparams(("arbitrary", "arbitrary")),
    )(c, w, m, v, mine[0], mine[1], theirs[0], theirs[1])


def _half_rows(h, cols):
    for tr in (512, 256, 352, 128, 64):
        if h % tr == 0 and tr * cols * 4 <= 6 * 1024 * 1024:
            return tr
    raise ValueError((h, cols))


def add_sibling(grad, recv, c):
    _, r, cols = grad.shape
    h = r // 2
    tr = _half_rows(h, cols)
    nb = h // tr

    def body(c_ref, g_ref, r_ref, o_ref):
        o_ref[...] = (g_ref[...].astype(F32) + r_ref[...].astype(F32)).astype(BF16)

    return pl.pallas_call(
        body, name="add_sibling",
        grid_spec=pltpu.PrefetchScalarGridSpec(
            num_scalar_prefetch=1, grid=(N_CHIPS, nb),
            in_specs=[pl.BlockSpec((None, tr, cols), lambda t, i, c_ref: (t, c_ref[0] * nb + i, 0)),
                      pl.BlockSpec((None, tr, cols), lambda t, i, c_ref: (t, i, 0))],
            out_specs=pl.BlockSpec((None, tr, cols), lambda t, i, c_ref: (t, i, 0))),
        out_shape=jax.ShapeDtypeStruct((N_CHIPS, h, cols), BF16), compiler_params=_cparams(("arbitrary", "arbitrary")),
    )(c, grad, recv)


def add_chips(part, recv, s):
    _, h, cols = part.shape
    tr = _half_rows(h, cols)

    def body(s_ref, p_ref, r_ref, o_ref):
        o_ref[...] = ((p_ref[...].astype(F32) + r_ref[0].astype(F32)) + r_ref[1].astype(F32)) + r_ref[2].astype(F32)

    return pl.pallas_call(
        body, name="add_chips",
        grid_spec=pltpu.PrefetchScalarGridSpec(
            num_scalar_prefetch=1, grid=(h // tr,),
            in_specs=[pl.BlockSpec((None, tr, cols), lambda i, s_ref: (s_ref[0], i, 0)),
                      pl.BlockSpec((3, tr, cols), lambda i, s_ref: (0, i, 0))],
            out_specs=pl.BlockSpec((tr, cols), lambda i, s_ref: (i, 0))),
        out_shape=jax.ShapeDtypeStruct((h, cols), F32), compiler_params=_cparams(("arbitrary",)),
    )(s, part, recv)


def allreduce_small(vec):
    rows = vec.shape[0]

    def body(v_ref, o_ref, buf, ssem, rsem, lsem):
        x, y, c, chips = _place()
        me, sibling = (x, y, c), (x, y, 1 - c)

        def blk(px, py, pc):
            return buf.at[4 * px + 2 * py + pc]

        def copy(k, block, to, src=None):
            return _rcopy(blk(*block) if src is None else src, blk(*block), ssem.at[k], rsem.at[k], to)

        mine = pltpu.make_async_copy(v_ref, blk(*me), lsem)
        mine.start()
        first = [copy(0, me, sibling, src=v_ref)] + [copy(1 + j, me, (*chip, c), src=v_ref) for j, chip in enumerate(chips)]
        for cp in first:
            cp.start()
        passed = [copy(4 + j, (*chip, c), sibling) for j, chip in enumerate(chips)]
        for j, chip in enumerate(chips):
            copy(1 + j, (*chip, c), me).wait_recv()
            passed[j].start()
        copy(0, sibling, me).wait_recv()
        for j, chip in enumerate(chips):
            copy(4 + j, (*chip, 1 - c), me).wait_recv()
        for cp in first + passed:
            cp.wait_send()
        mine.wait()
        acc = buf[0]
        for d in range(1, N_DEV):
            acc = acc + buf[d]
        o_ref[...] = acc

    vm = pl.BlockSpec(memory_space=pltpu.VMEM)
    return pl.pallas_call(
        body, name="allreduce_small", in_specs=[vm], out_specs=vm, out_shape=jax.ShapeDtypeStruct(vec.shape, F32),
        scratch_shapes=[pltpu.VMEM((N_DEV, rows, HD), F32), pltpu.SemaphoreType.DMA((7,)), pltpu.SemaphoreType.DMA((7,)),
                        pltpu.SemaphoreType.DMA],
        compiler_params=pltpu.CompilerParams(vmem_limit_bytes=VMEM_LIMIT),
    )(vec)


SMALL_NAMES = ("norm1_g", "sgu_norm_g", "w_spatial", "b_spatial", "conv_w", "a_log", "dt_bias", "o_norm_g", "q_norm_g",
               "k_norm_g", "norm2_g")


def small_params(l, p, conv_full):
    return {"norm1_g": p["norm1_g"][l][None], "sgu_norm_g": p["sgu_norm_g"][l][:, None, :], "w_spatial": p["w_spatial"][l],
            "b_spatial": p["b_spatial"][l][..., None], "conv_w": conv_full[l], "a_log": p["a_log"][l], "dt_bias": p["dt_bias"][l],
            "o_norm_g": p["o_norm_g"][l][None], "q_norm_g": p["q_norm_g"][l][None], "k_norm_g": p["k_norm_g"][l][None],
            "norm2_g": p["norm2_g"][l][None]}


def local_step(x, target, wg, sps):
    saved = []
    for l in range(DEPTH):
        x, s = layer_forward(x, wg[l], sps[l])
        saved.append(s)
    dx, loss = loss_and_grad(x, target)
    bigs, smalls = [None] * DEPTH, [None] * DEPTH
    for l in reversed(range(DEPTH)):
        dx, bigs[l], smalls[l] = layer_backward(dx, wg[l], sps[l], saved[l])
    return loss, dx, bigs, smalls


_PACK_TILE = 8 * HD


def _pack(arrays):
    flat = jnp.concatenate([a.reshape(-1) for a in arrays])
    pad = -flat.shape[0] % _PACK_TILE
    return jnp.pad(flat, (0, pad)).reshape(-1, HD)


def _unpack(packed, shapes):
    flat, out, off = packed.reshape(-1), [], 0
    for shp in shapes:
        n = int(np.prod(shp))
        out.append(flat[off:off + n].reshape(shp))
        off += n
    return out


BIG_NAMES = ("in", "out", "gu", "down")
WEIGHT_ORDER = ("norm1_g", "w_in", "sgu_norm_g", "w_spatial", "b_spatial", "conv_w", "a_log", "dt_bias", "o_norm_g", "q_norm_g",
                "k_norm_g", "w_out", "norm2_g", "w_gate_up", "w_down")


def kernel(x, norm1_g, w_in, sgu_norm_g, w_spatial, b_spatial, conv_w, a_log, dt_bias, o_norm_g, q_norm_g, k_norm_g, w_out, norm2_g, w_gate_up, w_down, loss_target, m_norm1_g, m_w_in, m_sgu_norm_g, m_w_spatial, m_b_spatial, m_conv_w, m_a_log, m_dt_bias, m_o_norm_g, m_q_norm_g, m_k_norm_g, m_w_out, m_norm2_g, m_w_gate_up, m_w_down, v_norm1_g, v_w_in, v_sgu_norm_g, v_w_spatial, v_b_spatial, v_conv_w, v_a_log, v_dt_bias, v_o_norm_g, v_q_norm_g, v_k_norm_g, v_w_out, v_norm2_g, v_w_gate_up, v_w_down):
    w = dict(norm1_g=norm1_g, w_in=w_in, sgu_norm_g=sgu_norm_g, w_spatial=w_spatial, b_spatial=b_spatial, conv_w=conv_w,
             a_log=a_log, dt_bias=dt_bias, o_norm_g=o_norm_g, q_norm_g=q_norm_g, k_norm_g=k_norm_g, w_out=w_out,
             norm2_g=norm2_g, w_gate_up=w_gate_up, w_down=w_down)
    m = dict(norm1_g=m_norm1_g, w_in=m_w_in, sgu_norm_g=m_sgu_norm_g, w_spatial=m_w_spatial, b_spatial=m_b_spatial,
             conv_w=m_conv_w, a_log=m_a_log, dt_bias=m_dt_bias, o_norm_g=m_o_norm_g, q_norm_g=m_q_norm_g, k_norm_g=m_k_norm_g,
             w_out=m_w_out, norm2_g=m_norm2_g, w_gate_up=m_w_gate_up, w_down=m_w_down)
    v = dict(norm1_g=v_norm1_g, w_in=v_w_in, sgu_norm_g=v_sgu_norm_g, w_spatial=v_w_spatial, b_spatial=v_b_spatial,
             conv_w=v_conv_w, a_log=v_a_log, dt_bias=v_dt_bias, o_norm_g=v_o_norm_g, q_norm_g=v_q_norm_g, k_norm_g=v_k_norm_g,
             w_out=v_w_out, norm2_g=v_norm2_g, w_gate_up=v_w_gate_up, w_down=v_w_down)
    chip = (2 * lax.axis_index("x") + lax.axis_index("y")).astype(jnp.int32)
    core = lax.axis_index("c").astype(jnp.int32)

    in_pad = IN_SHARD_PAD - IN_SHARD
    w_in_pad = jnp.pad(w_in, ((0, 0), (0, 0), (0, in_pad)))

    def layer_shards(l):
        halves_of = lambda a: a.reshape(2, a.shape[0] // 2, a.shape[1])
        return [halves_of(w_gate_up[l].astype(BF16)), halves_of(w_in_pad[l].astype(BF16)), halves_of(w_down[l].astype(BF16)),
                halves_of(w_out[l].astype(BF16)), halves_of(conv_w[l])]

    def layer_weights(g):
        g_gu, g_in, g_down, g_out, g_conv = g
        wts = {"in": shards_to_segments(g_in.reshape(N_CHIPS, D, IN_SHARD_PAD))[None], "out": g_out.reshape(1, D, D),
               "gu": g_gu.reshape(1, N_CHIPS, D, GU_SHARD), "down": g_down.reshape(1, FFN, D)}
        return wts, g_conv.reshape(N_CHIPS, B_CONV, -1).transpose(1, 0, 2).reshape(B_CONV, 3 * B_WIDTH)

    def layer_params(l, conv_full):
        return small_params(0, {n: w[n][l:l + 1] for n in SMALL_NAMES if n != "conv_w"}, conv_full[None])

    wg0, conv0 = layer_weights(gather_weights(layer_shards(0)))
    start1, wait1 = gather_direct(layer_shards(1))
    state1, token1 = start1()
    sps = [layer_params(0, conv0), None]
    x1, saved0 = layer_forward(x[0], wg0, sps[0], dep=token1)
    wg1, conv1 = layer_weights(pass_to_sibling(wait1(state1, x1)))
    sps[1] = layer_params(1, conv1)
    x2, saved1 = layer_forward(x1, wg1, sps[1])
    dx, loss_tile = loss_and_grad(x2, loss_target[0])

    shard_rows = {"in": D, "out": OUT_SHARD, "gu": D, "down": DOWN_SHARD}
    by_chip = lambda big: [big[k].reshape(N_CHIPS, shard_rows[k], -1) for k in BIG_NAMES]
    to_chip_parts = lambda grads: [add_sibling(g, r, core.reshape(1)) for g, r in zip(grads, exchange_halves(grads))]
    smalls = [None] * DEPTH
    dx, big1, smalls[1] = layer_backward(dx, wg1, sps[1], saved1)
    parts1 = to_chip_parts(by_chip(big1))
    sc_start, sc_wait = scatter_direct(parts1)
    sc_state, sc_token = sc_start()
    dx, big0, smalls[0] = layer_backward(dx, wg0, sps[0], saved0, dep=sc_token)
    parts0 = to_chip_parts(by_chip(big0))
    from_chips = list(scatter_to_chips(parts0)) + list(sc_wait(sc_state, dx))
    halves = [add_chips(p, r, chip.reshape(1)) for p, r in zip(parts0 + parts1, from_chips)]
    theirs = share_halves(halves)
    grad, delta, new_m, new_v = {}, {}, {}, {}
    for a, n in enumerate(("w_in", "w_out", "w_gate_up", "w_down")):
        pick = lambda lst: [lst[l * len(BIG_NAMES) + a] for l in range(DEPTH)]
        grad[n], delta[n], new_m[n], new_v[n] = adamw_shard(w[n], m[n], v[n], pick(halves), pick(theirs), core.reshape(1),
                                                            "adamw_" + n)

    stacked = [jnp.stack([smalls[l][n] for l in range(DEPTH)]) for n in SMALL_NAMES]
    total = allreduce_small(_pack(stacked + [loss_tile[0, :1]]))
    shapes = [(DEPTH, B_CONV, 3 * B_WIDTH) if n == "conv_w" else w[n].shape for n in SMALL_NAMES]
    small_grads = dict(zip(SMALL_NAMES, _unpack(total, shapes + [(1,)])[:-1]))
    loss = _unpack(total, shapes + [(1,)])[-1][0]
    conv_cols = conv_w.shape[-1]
    small_grads["conv_w"] = lax.dynamic_slice_in_dim(small_grads["conv_w"], chip * conv_cols, conv_cols, axis=2)
    grad.update(small_grads)

    sshapes = [w[n].shape for n in SMALL_NAMES]
    packed = [_pack([d[n] for n in SMALL_NAMES]) for d in (w, grad, m, v)]
    for dst, t in zip((delta, new_m, new_v), adamw(*packed, "adamw_small")):
        dst.update(zip(SMALL_NAMES, _unpack(t, sshapes)))

    out = [loss, dx[None]]
    for d in (grad, delta, new_m, new_v):
        out += [d[n] for n in WEIGHT_ORDER]
    return tuple(out)
```

```python
import functools
import math

import numpy as np
import jax
import jax.numpy as jnp
from jax import lax
from jax.experimental import pallas as pl
from jax.experimental.pallas import tpu as pltpu

F32 = jnp.float32
BF16 = jnp.bfloat16
HI = lax.Precision.HIGH

T = 2048
D = 2048
DEPTH = 2
HD = 128
A_GROUPS, A_WIDTH, A_CHUNK = 4, 512, 128
B_HEADS, B_WIDTH, B_CONV, B_CHUNK = 6, 768, 4, 64
C_HEADS, C_WIDTH, C_BLOCK = 6, 768, 128
C_BRANCHES = ((128, 1), (512, 4), (2048, 16))
FFN = 5632
IN_TOTAL = 6412
EPS = 1e-6
N_CHIPS = 4
N_DEV = 8
IN_SHARD = IN_TOTAL // N_CHIPS
IN_SHARD_PAD = 1664
GU_SHARD = 2 * FFN // N_CHIPS
OUT_SHARD = D // N_CHIPS
DOWN_SHARD = FFN // N_CHIPS
P_AU, P_AV, P_BQ, P_BK, P_BV, P_BG, P_BB, P_CQ, P_CK, P_CV, P_END = (
    0, 512, 1024, 1792, 2560, 3328, 4096, 4224, 4992, 5760, 6528)
GATE_COLS = 4108
VMEM_LIMIT = 56 * 1024 * 1024

ADAM_LR, ADAM_B1, ADAM_B2, ADAM_EPS, ADAM_WD, ADAM_STEP = 0.001, 0.9, 0.999, 1e-08, 0.01, 10


def _cparams(sem, vmem=VMEM_LIMIT):
    return pltpu.CompilerParams(dimension_semantics=sem, vmem_limit_bytes=vmem)


def _dims(nd, ta, tb):
    off = nd - 2
    ca = off + (0 if ta else 1)
    cb = off + (1 if tb else 0)
    batch = ((0,), (0,)) if nd == 3 else ((), ())
    return (((ca,), (cb,)), batch)


def _raw_mm(a, b, ta, tb, hi):
    if hi:
        return lax.dot_general(a, b, _dims(a.ndim, ta, tb), precision=HI, preferred_element_type=F32)
    return lax.dot_general(a.astype(BF16), b.astype(BF16), _dims(a.ndim, ta, tb), preferred_element_type=F32)


@functools.partial(jax.custom_vjp, nondiff_argnums=(2, 3, 4))
def _mm(a, b, ta=False, tb=False, hi=False):
    return _raw_mm(a, b, ta, tb, hi)


def _mm_fwd(a, b, ta, tb, hi):
    return _raw_mm(a, b, ta, tb, hi), (a, b)


def _mm_bwd(ta, tb, hi, res, g):
    a, b = res
    da = _raw_mm(g, b, False, not tb, hi) if not ta else _raw_mm(b, g, tb, True, hi)
    db = _raw_mm(a, g, not ta, False, hi) if not tb else _raw_mm(g, a, True, ta, hi)
    return da.astype(a.dtype), db.astype(b.dtype)


_mm.defvjp(_mm_fwd, _mm_bwd)


def _rms(x, g):
    return x * lax.rsqrt(jnp.mean(x * x, axis=-1, keepdims=True) + EPS) * g


def _gelu(x):
    return 0.5 * x * (1.0 + jnp.tanh(math.sqrt(2.0 / math.pi) * (x + 0.044715 * (x * x * x))))


def _sigmoid(x):
    return 1.0 / (1.0 + jnp.exp(-x))


def _silu(x):
    return x * _sigmoid(x)


def _softplus(x):
    return jnp.maximum(x, 0.0) + jnp.log(1.0 + jnp.exp(-jnp.abs(x)))


def _iota(shape, dim):
    return lax.broadcasted_iota(jnp.int32, shape, dim)


def _sgu_fn(u, v, sg, w, b):
    nc = T // A_CHUNK
    ug = _gelu(u)
    vn = _rms(_gelu(v), sg)
    causal = _iota((A_CHUNK, A_CHUNK), 0) >= _iota((A_CHUNK, A_CHUNK), 1)
    wm = jnp.where(causal, w, 0.0)
    wb = jnp.broadcast_to(wm[None], (nc, A_CHUNK, A_CHUNK))
    z = _mm(wb, vn.reshape(nc, A_CHUNK, HD)) + b[None]
    return ug * z.reshape(T, HD)


def _sgu_specs():
    col = lambda off: pl.BlockSpec((T, HD), lambda g, off=off: (0, off + g))
    par = [pl.BlockSpec((None, 1, HD), lambda g: (g, 0, 0)),
           pl.BlockSpec((None, A_CHUNK, A_CHUNK), lambda g: (g, 0, 0)),
           pl.BlockSpec((None, A_CHUNK, 1), lambda g: (g, 0, 0))]
    return col, par


def sgu_fwd(p2, sg, w, b):
    col, par = _sgu_specs()

    def body(u_ref, v_ref, sg_ref, w_ref, b_ref, y_ref):
        y_ref[...] = _sgu_fn(u_ref[...], v_ref[...], sg_ref[...], w_ref[...], b_ref[...]).astype(BF16)

    return pl.pallas_call(
        body, name="sgu_fwd", grid=(A_GROUPS,),
        in_specs=[col(P_AU // HD), col(P_AV // HD)] + par,
        out_specs=pl.BlockSpec((T, HD), lambda g: (0, g)),
        out_shape=jax.ShapeDtypeStruct((T, A_WIDTH), BF16),
        compiler_params=_cparams(("arbitrary",)),
    )(p2, p2, sg, w, b)


def sgu_bwd(p2, sg, w, b, dmix):
    col, par = _sgu_specs()

    def body(u_ref, v_ref, sg_ref, w_ref, b_ref, dy_ref, du_ref, dv_ref, dsg_ref, dw_ref, db_ref):
        _, vjp = jax.vjp(_sgu_fn, u_ref[...], v_ref[...], sg_ref[...], w_ref[...], b_ref[...])
        du, dv, dsg, dw, db = vjp(dy_ref[...])
        du_ref[...] = du.astype(BF16)
        dv_ref[...] = dv.astype(BF16)
        dsg_ref[...] = dsg
        dw_ref[...] = dw
        db_ref[...] = db

    gcol = pl.BlockSpec((T, HD), lambda g: (0, g))
    return pl.pallas_call(
        body, name="sgu_bwd", grid=(A_GROUPS,),
        in_specs=[col(P_AU // HD), col(P_AV // HD)] + par + [gcol],
        out_specs=[gcol, gcol] + par,
        out_shape=[jax.ShapeDtypeStruct((T, A_WIDTH), BF16), jax.ShapeDtypeStruct((T, A_WIDTH), BF16),
                   jax.ShapeDtypeStruct((A_GROUPS, 1, HD), F32), jax.ShapeDtypeStruct((A_GROUPS, A_CHUNK, A_CHUNK), F32),
                   jax.ShapeDtypeStruct((A_GROUPS, A_CHUNK, 1), F32)],
        compiler_params=_cparams(("arbitrary",)),
    )(p2, p2, sg, w, b, dmix)


def _attn_fn(q, k, v, qg, kg, slope, *, dil, nb):
    n = T // C_BLOCK
    qb = _rms(q, qg).reshape(n, C_BLOCK, HD)
    kb = _rms(k, kg).reshape(n, C_BLOCK, HD)
    vb = v.reshape(n, C_BLOCK, HD)
    scale = HD ** -0.5
    qi = _iota((n, C_BLOCK, C_BLOCK), 1)
    kj = _iota((n, C_BLOCK, C_BLOCK), 2)
    sl = slope[None] * float(dil)
    d_cur = qi - kj
    sc = jnp.where(d_cur >= 0, _mm(qb, kb, tb=True) * scale - sl * d_cur.astype(F32), -jnp.inf)
    mx = jnp.max(sc, axis=-1, keepdims=True)
    if nb > 1:
        kp = jnp.concatenate([jnp.zeros((1, C_BLOCK, HD), F32), kb[:-1]], axis=0)
        vp = jnp.concatenate([jnp.zeros((1, C_BLOCK, HD), F32), vb[:-1]], axis=0)
        has_prev = (_iota((n, C_BLOCK, C_BLOCK), 0) % nb) > 0
        d_prev = C_BLOCK + qi - kj
        sp = jnp.where((kj >= qi) & has_prev, _mm(qb, kp, tb=True) * scale - sl * d_prev.astype(F32), -jnp.inf)
        mx = jnp.maximum(mx, jnp.max(sp, axis=-1, keepdims=True))
    p = jnp.exp(sc - mx)
    den = jnp.sum(p, axis=-1, keepdims=True)
    if nb > 1:
        pp = jnp.exp(sp - mx)
        den = den + jnp.sum(pp, axis=-1, keepdims=True)
    out = _mm(p / den, vb)
    if nb > 1:
        out = out + _mm(pp / den, vp)
    lse = mx + jnp.log(den)
    return out.reshape(T, HD), jnp.broadcast_to(lse, (n, C_BLOCK, HD)).reshape(T, HD)


def _combine_fn(o1, o2, o3, l1, l2, l3):
    mx = jnp.maximum(jnp.maximum(l1, l2), l3)
    e1, e2, e3 = jnp.exp(l1 - mx), jnp.exp(l2 - mx), jnp.exp(l3 - mx)
    s = e1 + e2 + e3
    return (e1 / s) * o1 + (e2 / s) * o2 + (e3 / s) * o3


def _branch_blocks(dil):
    return -(-(T // dil) // C_BLOCK)


def _load_branch_order(ref, dil):
    if dil == 1:
        return ref[...]
    seg = T // dil
    return jnp.concatenate([ref[pl.ds(r, seg, stride=dil), :] for r in range(dil)], axis=0)


def _store_position_order(ref, val, dil, add=False):
    seg = T // dil
    for r in range(dil):
        rows = slice(None) if dil == 1 else pl.ds(r, seg, stride=dil)
        piece = val if dil == 1 else val[r * seg:(r + 1) * seg]
        if add:
            ref[rows, :] += piece
        else:
            ref[rows, :] = piece


def _dattn_specs():
    col = lambda off: pl.BlockSpec((T, HD), lambda h, off=off: (0, off // HD + h))
    row = pl.BlockSpec((1, HD), lambda h: (0, 0))
    slope = pl.BlockSpec((None, 1, HD), lambda h: (h, 0, 0))
    return [col(P_CQ), col(P_CK), col(P_CV), row, row, slope]


def _dattn_branches(q_ref, k_ref, v_ref, qg, kg, slope, o_scr, l_scr):
    for b, (_, dil) in enumerate(C_BRANCHES):
        q, k, v = (_load_branch_order(r, dil) for r in (q_ref, k_ref, v_ref))
        o, l = _attn_fn(q, k, v, qg, kg, slope, dil=dil, nb=_branch_blocks(dil))
        _store_position_order(o_scr.at[b], o, dil)
        _store_position_order(l_scr.at[b], l, dil)


def dattn_fwd(p2, qg, kg, slopes):
    def body(q_ref, k_ref, v_ref, qg_ref, kg_ref, s_ref, y_ref, o_scr, l_scr):
        _dattn_branches(q_ref, k_ref, v_ref, qg_ref[...], kg_ref[...], s_ref[...], o_scr, l_scr)
        y_ref[...] = _combine_fn(o_scr[0], o_scr[1], o_scr[2], l_scr[0], l_scr[1], l_scr[2]).astype(BF16)

    return pl.pallas_call(
        body, name="dattn_fwd", grid=(C_HEADS,), in_specs=_dattn_specs(), out_specs=pl.BlockSpec((T, HD), lambda h: (0, h)),
        out_shape=jax.ShapeDtypeStruct((T, C_WIDTH), BF16),
        scratch_shapes=[pltpu.VMEM((3, T, HD), F32), pltpu.VMEM((3, T, HD), F32)], compiler_params=_cparams(("arbitrary",)),
    )(p2, p2, p2, qg, kg, slopes)


def dattn_bwd(p2, qg, kg, slopes, dmix):
    hcol = pl.BlockSpec((T, HD), lambda h: (0, h))
    row = pl.BlockSpec((1, HD), lambda h: (0, 0))
    dy = pl.BlockSpec((T, HD), lambda h: (0, (A_WIDTH + B_WIDTH) // HD + h))

    def body(q_ref, k_ref, v_ref, qg_ref, kg_ref, s_ref, dy_ref, dq_ref, dk_ref, dv_ref, dqg_ref, dkg_ref, o_scr, l_scr, g_scr,
             acc):
        qg, kg, slope = qg_ref[...], kg_ref[...], s_ref[...]
        _dattn_branches(q_ref, k_ref, v_ref, qg, kg, slope, o_scr, l_scr)
        _, vjp = jax.vjp(_combine_fn, o_scr[0], o_scr[1], o_scr[2], l_scr[0], l_scr[1], l_scr[2])
        for i, g in enumerate(vjp(dy_ref[...])):
            g_scr[i] = g

        @pl.when(pl.program_id(0) == 0)
        def _():
            dqg_ref[...] = jnp.zeros_like(dqg_ref)
            dkg_ref[...] = jnp.zeros_like(dkg_ref)

        for b, (_, dil) in enumerate(C_BRANCHES):
            q, k, v = (_load_branch_order(r, dil) for r in (q_ref, k_ref, v_ref))
            do, dl = _load_branch_order(g_scr.at[b], dil), _load_branch_order(g_scr.at[3 + b], dil)
            fn = functools.partial(_attn_fn, dil=dil, nb=_branch_blocks(dil))
            _, vjp_b = jax.vjp(lambda a, b_, c, d, e: fn(a, b_, c, d, e, slope), q, k, v, qg, kg)
            dq, dk, dv, dqg, dkg = vjp_b((do, dl))
            for i, val in enumerate((dq, dk, dv)):
                _store_position_order(acc.at[i], val, dil, add=b > 0)
            dqg_ref[...] += dqg
            dkg_ref[...] += dkg
        for i, ref in enumerate((dq_ref, dk_ref, dv_ref)):
            ref[...] = acc[i].astype(BF16)

    scr = lambda n: pltpu.VMEM((n, T, HD), F32)
    return pl.pallas_call(
        body, name="dattn_bwd", grid=(C_HEADS,), in_specs=_dattn_specs() + [dy], out_specs=[hcol, hcol, hcol, row, row],
        out_shape=[jax.ShapeDtypeStruct((T, C_WIDTH), BF16)] * 3 + [jax.ShapeDtypeStruct((1, HD), F32)] * 2,
        scratch_shapes=[scr(3), scr(3), scr(6), scr(3)], compiler_params=_cparams(("arbitrary",)),
    )(p2, p2, p2, qg, kg, slopes, dmix)


_NCH = T // B_CHUNK


def _conv_taps(x, w_ref):
    rows = _iota(x.shape, 0)
    taps = []
    for j in range(B_CONV):
        s = B_CONV - 1 - j
        taps.append(x if s == 0 else jnp.where(rows >= s, pltpu.roll(x, s, 0), 0.0))
    pre = sum(w_ref[j:j + 1, :] * taps[j] for j in range(B_CONV))
    return pre, taps


def _conv_post(pre, mode):
    y = _silu(pre)
    if mode == "v":
        return y
    y = y * lax.rsqrt(jnp.sum(y * y, axis=-1, keepdims=True) + EPS)
    return y * (HD ** -0.5) if mode == "q" else y


def conv_fwd(p2, conv_w, mode):
    idx = "qkv".index(mode)
    xcol = pl.BlockSpec((T, HD), lambda h: (0, P_BQ // HD + B_HEADS * idx + h))
    wcol = pl.BlockSpec((B_CONV, HD), lambda h: (0, B_HEADS * idx + h))
    hcol = pl.BlockSpec((T, HD), lambda h: (0, h))

    def body(x_ref, w_ref, y_ref):
        pre, _ = _conv_taps(x_ref[...], w_ref)
        y_ref[...] = _conv_post(pre, mode)

    return pl.pallas_call(
        body, name=f"conv_fwd_{mode}", grid=(B_HEADS,), in_specs=[xcol, wcol], out_specs=hcol,
        out_shape=jax.ShapeDtypeStruct((T, B_WIDTH), F32), compiler_params=_cparams(("arbitrary",)),
    )(p2, conv_w)


def conv_bwd(p2, conv_w, dys, mode):
    idx = "qkv".index(mode)
    xcol = pl.BlockSpec((T, HD), lambda h: (0, P_BQ // HD + B_HEADS * idx + h))
    wcol = pl.BlockSpec((B_CONV, HD), lambda h: (0, B_HEADS * idx + h))
    hcol = pl.BlockSpec((T, HD), lambda h: (0, h))
    wout = pl.BlockSpec((B_CONV, HD), lambda h: (0, h))

    def body(x_ref, w_ref, *rest):
        dy_refs, (dx_ref, dw_ref) = rest[:-2], rest[-2:]
        pre, taps = _conv_taps(x_ref[...], w_ref)
        _, vjp = jax.vjp(functools.partial(_conv_post, mode=mode), pre)
        (dpre,) = vjp(sum(r[...] for r in dy_refs))
        rows = _iota(dpre.shape, 0)
        dx = w_ref[B_CONV - 1:B_CONV, :] * dpre
        for j in range(B_CONV):
            s = B_CONV - 1 - j
            dw_ref[j:j + 1, :] = jnp.sum(dpre * taps[j], axis=0, keepdims=True)
            if s > 0:
                dx = dx + w_ref[j:j + 1, :] * jnp.where(rows < T - s, pltpu.roll(dpre, T - s, 0), 0.0)
        dx_ref[...] = dx.astype(BF16)

    return pl.pallas_call(
        body, name=f"conv_bwd_{mode}", grid=(B_HEADS,), in_specs=[xcol, wcol] + [hcol] * len(dys), out_specs=[hcol, wout],
        out_shape=[jax.ShapeDtypeStruct((T, B_WIDTH), BF16), jax.ShapeDtypeStruct((B_CONV, B_WIDTH), F32)],
        compiler_params=_cparams(("arbitrary",)),
    )(p2, conv_w, *dys)


def _gates_fn(bg, al, dtb, h):
    r = _iota((HD, HD), 0)
    logit = _mm(bg, (r == h).astype(F32), hi=True)
    a = _mm(bg, (r == h + B_HEADS).astype(F32), hi=True)
    beta = _sigmoid(logit)
    graw = -jnp.exp(al) * _softplus(a + dtb)
    tri = (_iota((_NCH, B_CHUNK, B_CHUNK), 1) >= _iota((_NCH, B_CHUNK, B_CHUNK), 2)).astype(F32)
    g = _mm(tri, graw.reshape(_NCH, B_CHUNK, HD), hi=True).reshape(T, HD)
    return beta, g


def _gates_specs():
    bg = pl.BlockSpec((T, HD), lambda h: (0, P_BB // HD))
    par = pl.BlockSpec((None, 1, HD), lambda h: (h, 0, 0))
    out = pl.BlockSpec((None, T, HD), lambda h: (h, 0, 0))
    return bg, par, out


def gates_fwd(p2, al, dtb):
    bg, par, out = _gates_specs()

    def body(bg_ref, al_ref, dtb_ref, beta_ref, g_ref):
        beta, g = _gates_fn(bg_ref[...], al_ref[...], dtb_ref[...], pl.program_id(0))
        beta_ref[...] = beta
        g_ref[...] = g

    return pl.pallas_call(
        body, name="gates_fwd", grid=(B_HEADS,), in_specs=[bg, par, par], out_specs=[out, out],
        out_shape=[jax.ShapeDtypeStruct((B_HEADS, T, HD), F32)] * 2, compiler_params=_cparams(("arbitrary",)),
    )(p2, al, dtb)


def gates_bwd(p2, al, dtb, dbeta, dg1, dg2):
    bg, par, out = _gates_specs()
    acc = pl.BlockSpec((T, HD), lambda h: (0, 0))

    def body(bg_ref, al_ref, dtb_ref, dbeta_ref, dg1_ref, dg2_ref, dbg_ref, dal_ref, ddtb_ref, acc_ref):
        h = pl.program_id(0)
        _, vjp = jax.vjp(lambda a, b, c: _gates_fn(a, b, c, h), bg_ref[...], al_ref[...], dtb_ref[...])
        dbg, dal, ddtb = vjp((dbeta_ref[...], dg1_ref[...] + dg2_ref[...]))

        @pl.when(h == 0)
        def _():
            acc_ref[...] = jnp.zeros_like(acc_ref)

        acc_ref[...] += dbg
        dbg_ref[...] = acc_ref[...].astype(BF16)
        dal_ref[...] = jnp.broadcast_to(jnp.sum(dal, axis=-1, keepdims=True), (1, HD))
        ddtb_ref[...] = jnp.broadcast_to(jnp.sum(ddtb, axis=-1, keepdims=True), (1, HD))

    return pl.pallas_call(
        body, name="gates_bwd", grid=(B_HEADS,), in_specs=[bg, par, par, out, out, out], out_specs=[acc, par, par],
        out_shape=[jax.ShapeDtypeStruct((T, HD), BF16)] + [jax.ShapeDtypeStruct((B_HEADS, 1, HD), F32)] * 2,
        scratch_shapes=[pltpu.VMEM((T, HD), F32)], compiler_params=_cparams(("arbitrary",)),
    )(p2, al, dtb, dbeta, dg1, dg2)


def _unit_lower_inverse(a):
    eye = (_iota(a.shape, 1) == _iota(a.shape, 2)).astype(F32)
    x = eye - a
    p = _mm(a, a, hi=True)
    for i in range(5):
        x = x + _mm(x, p, hi=True)
        if i < 4:
            p = _mm(p, p, hi=True)
    return x


_WY_CH = 8
_WY_ROWS = _WY_CH * B_CHUNK


def _wy_fn(q, k, v, beta, g):
    sh = (q.shape[0] // B_CHUNK, B_CHUNK, HD)
    q3, k3, v3, b3, g3 = (t.reshape(sh) for t in (q, k, v, beta, g))
    gd = g3[:, :, :B_CHUNK] - jnp.swapaxes(g3, 1, 2)[:, :B_CHUNK, :]
    ii, jj = _iota(gd.shape, 1), _iota(gd.shape, 2)
    decay = jnp.exp(jnp.where(ii >= jj, gd, -jnp.inf))
    kb = k3 * b3
    a = _mm(kb, k3, tb=True) * jnp.where(ii > jj, decay, 0.0)
    tinv = _unit_lower_inverse(a)
    u = _mm(tinv, v3 * b3, hi=True)
    w = _mm(tinv, kb * jnp.exp(g3), hi=True)
    attn = _mm(q3, k3, tb=True) * decay
    return u.reshape(q.shape), w.reshape(q.shape), attn


def _wy_specs():
    hcol = pl.BlockSpec((_WY_ROWS, HD), lambda h, i: (i, h))
    hb = pl.BlockSpec((None, _WY_ROWS, HD), lambda h, i: (h, i, 0))
    at = pl.BlockSpec((None, _WY_CH, B_CHUNK, B_CHUNK), lambda h, i: (h, i, 0, 0))
    return hcol, hb, at


_WY_GRID = (B_HEADS, _NCH // _WY_CH)


def wy_fwd(q, k, v, beta, g):
    hcol, hb, at = _wy_specs()

    def body(q_ref, k_ref, v_ref, b_ref, g_ref, u_ref, w_ref, a_ref):
        u, w, a = _wy_fn(q_ref[...], k_ref[...], v_ref[...], b_ref[...], g_ref[...])
        u_ref[...] = u
        w_ref[...] = w
        a_ref[...] = a

    return pl.pallas_call(
        body, name="wy_fwd", grid=_WY_GRID, in_specs=[hcol, hcol, hcol, hb, hb], out_specs=[hcol, hcol, at],
        out_shape=[jax.ShapeDtypeStruct((T, B_WIDTH), F32)] * 2 + [jax.ShapeDtypeStruct((B_HEADS, _NCH, B_CHUNK, B_CHUNK), F32)],
        compiler_params=_cparams(("arbitrary", "arbitrary")),
    )(q, k, v, beta, g)


def wy_bwd(q, k, v, beta, g, du, dw, dattn):
    hcol, hb, at = _wy_specs()

    def body(q_ref, k_ref, v_ref, b_ref, g_ref, du_ref, dw_ref, da_ref, dq_ref, dk_ref, dv_ref, db_ref, dg_ref):
        _, vjp = jax.vjp(_wy_fn, q_ref[...], k_ref[...], v_ref[...], b_ref[...], g_ref[...])
        for r, t in zip((dq_ref, dk_ref, dv_ref, db_ref, dg_ref), vjp((du_ref[...], dw_ref[...], da_ref[...]))):
            r[...] = t

    return pl.pallas_call(
        body, name="wy_bwd", grid=_WY_GRID, in_specs=[hcol, hcol, hcol, hb, hb, hcol, hcol, at],
        out_specs=[hcol, hcol, hcol, hb, hb],
        out_shape=[jax.ShapeDtypeStruct((T, B_WIDTH), F32)] * 3 + [jax.ShapeDtypeStruct((B_HEADS, T, HD), F32)] * 2,
        compiler_params=_cparams(("arbitrary", "arbitrary")),
    )(q, k, v, beta, g, du, dw, dattn)


def _scan_step_fn(q, k, u, w, g, attn, gate, og, s):
    v_new = u - _mm(w, s)
    o = _mm(q * jnp.exp(g), s) + _mm(attn, v_new)
    g_last = jnp.sum(jnp.where(_iota(g.shape, 0) == B_CHUNK - 1, g, 0.0), axis=0, keepdims=True)
    s_new = s * jnp.exp(g_last) + _mm(k * jnp.exp(g_last - g), v_new, ta=True)
    return _rms(o, og) * _silu(gate), s_new


def _scan_specs(rev):
    ch = (lambda n: _NCH - 1 - n) if rev else (lambda n: n)
    rows = pl.BlockSpec((B_CHUNK, B_WIDTH), lambda n: (ch(n), 0))
    gb = pl.BlockSpec((B_HEADS, B_CHUNK, HD), lambda n: (0, ch(n), 0))
    at = pl.BlockSpec((B_HEADS, None, B_CHUNK, B_CHUNK), lambda n: (0, ch(n), 0, 0))
    og = pl.BlockSpec((1, HD), lambda n: (0, 0))
    st = pl.BlockSpec((None, B_HEADS, HD, HD), lambda n: (ch(n), 0, 0, 0))
    return rows, gb, at, og, st


def scan_fwd(q, k, u, w, g, attn, gate, og):
    rows, gb, at, ogs, st = _scan_specs(False)

    def body(q_ref, k_ref, u_ref, w_ref, g_ref, a_ref, gate_ref, og_ref, y_ref, st_ref, s_ref):
        @pl.when(pl.program_id(0) == 0)
        def _():
            s_ref[...] = jnp.zeros_like(s_ref)

        for h in range(B_HEADS):
            c = slice(h * HD, (h + 1) * HD)
            s = s_ref[h]
            st_ref[h] = s
            y, s_new = _scan_step_fn(q_ref[:, c], k_ref[:, c], u_ref[:, c], w_ref[:, c], g_ref[h], a_ref[h],
                                     gate_ref[:, c], og_ref[...], s)
            y_ref[:, c] = y.astype(BF16)
            s_ref[h] = s_new

    return pl.pallas_call(
        body, name="scan_fwd", grid=(_NCH,), in_specs=[rows, rows, rows, rows, gb, at, rows, ogs], out_specs=[rows, st],
        out_shape=[jax.ShapeDtypeStruct((T, B_WIDTH), BF16), jax.ShapeDtypeStruct((_NCH, B_HEADS, HD, HD), F32)],
        scratch_shapes=[pltpu.VMEM((B_HEADS, HD, HD), F32)], compiler_params=_cparams(("arbitrary",)),
    )(q, k, u, w, g, attn, gate, og)


def scan_bwd(q, k, u, w, g, attn, gate, og, states, dmix):
    rows, gb, at, ogs, st = _scan_specs(True)
    dyb = pl.BlockSpec((B_CHUNK, HD), lambda n: (_NCH - 1 - n, 0))

    def body(q_ref, k_ref, u_ref, w_ref, g_ref, a_ref, gate_ref, og_ref, st_ref, *rest):
        dy_refs, (dq_ref, dk_ref, du_ref, dw_ref, dgate_ref, dg_ref, da_ref, dog_ref, ds_ref) = rest[:B_HEADS], rest[B_HEADS:]

        @pl.when(pl.program_id(0) == 0)
        def _():
            ds_ref[...] = jnp.zeros_like(ds_ref)
            dog_ref[...] = jnp.zeros_like(dog_ref)

        for h in range(B_HEADS):
            c = slice(h * HD, (h + 1) * HD)
            _, vjp = jax.vjp(_scan_step_fn, q_ref[:, c], k_ref[:, c], u_ref[:, c], w_ref[:, c], g_ref[h], a_ref[h],
                             gate_ref[:, c], og_ref[...], st_ref[h])
            dq, dk, du, dw, dg, da, dgate, dog, ds = vjp((dy_refs[h][...], ds_ref[h]))
            dq_ref[:, c] = dq
            dk_ref[:, c] = dk
            du_ref[:, c] = du
            dw_ref[:, c] = dw
            dgate_ref[:, c] = dgate.astype(BF16)
            dg_ref[h] = dg
            da_ref[h] = da
            dog_ref[...] += dog
            ds_ref[h] = ds

    dy_specs = [pl.BlockSpec((B_CHUNK, HD), lambda n, h=h: (_NCH - 1 - n, A_WIDTH // HD + h)) for h in range(B_HEADS)]
    return pl.pallas_call(
        body, name="scan_bwd", grid=(_NCH,),
        in_specs=[rows, rows, rows, rows, gb, at, rows, ogs, st] + dy_specs,
        out_specs=[rows] * 5 + [gb, at, ogs],
        out_shape=[jax.ShapeDtypeStruct((T, B_WIDTH), F32)] * 4 + [jax.ShapeDtypeStruct((T, B_WIDTH), BF16)]
        + [jax.ShapeDtypeStruct((B_HEADS, T, HD), F32), jax.ShapeDtypeStruct((B_HEADS, _NCH, B_CHUNK, B_CHUNK), F32),
           jax.ShapeDtypeStruct((1, HD), F32)],
        scratch_shapes=[pltpu.VMEM((B_HEADS, HD, HD), F32)], compiler_params=_cparams(("arbitrary",)),
    )(q, k, u, w, g, attn, gate, og, states, *([dmix] * B_HEADS))


def _lanes(vec):
    return jnp.broadcast_to(vec[:, None, None], (vec.shape[0], 1, HD))


def gdn_forward(p2, conv_w, a_log, dt_bias, og):
    qa, ka, va = (conv_fwd(p2, conv_w, m) for m in "qkv")
    beta, g = gates_fwd(p2, _lanes(a_log), _lanes(dt_bias))
    u, w, attn = wy_fwd(qa, ka, va, beta, g)
    gate = p2[:, P_BG:P_BB]
    y, states = scan_fwd(qa, ka, u, w, g, attn, gate, og)
    return y, (qa, ka, va, beta, g, u, w, attn, gate, states)


def gdn_backward(p2, conv_w, a_log, dt_bias, og, saved, dmix):
    qa, ka, va, beta, g, u, w, attn, gate, states = saved
    dq1, dk1, du, dw, dgate, dg1, dattn, dog = scan_bwd(qa, ka, u, w, g, attn, gate, og, states, dmix)
    dq2, dk2, dv, dbeta, dg2 = wy_bwd(qa, ka, va, beta, g, du, dw, dattn)
    dbg, dal, ddtb = gates_bwd(p2, _lanes(a_log), _lanes(dt_bias), dbeta, dg1, dg2)
    dxq, dwq = conv_bwd(p2, conv_w, [dq1, dq2], "q")
    dxk, dwk = conv_bwd(p2, conv_w, [dk1, dk2], "k")
    dxv, dwv = conv_bwd(p2, conv_w, [dv], "v")
    return [dxq, dxk, dxv, dgate, dbg], jnp.concatenate([dwq, dwk, dwv], axis=1), dal[:, 0, 0], ddtb[:, 0, 0], dog


_SLOPES = np.exp2(-8.0 * (np.arange(C_HEADS, dtype=np.float64) + 1.0) / C_HEADS).astype(np.float32)


def _alibi_slopes():
    return _lanes(jnp.asarray(_SLOPES))


_ROWS = 256
_TM = 512


def _dep_specs(dep, ngrid):
    if dep is None:
        return [], []
    return [dep], [pl.BlockSpec((8, HD), lambda *_: (0, 0))]


def rmsnorm_fwd(x, g, dep=None):
    blk = pl.BlockSpec((_ROWS, D), lambda i: (i, 0))
    deps, dspecs = _dep_specs(dep, 1)

    def body(x_ref, g_ref, *rest):
        rest[-1][...] = _rms(x_ref[...], g_ref[...]).astype(BF16)

    return pl.pallas_call(
        body, name="rmsnorm_fwd", grid=(T // _ROWS,), in_specs=[blk, pl.BlockSpec((1, D), lambda i: (0, 0))] + dspecs,
        out_specs=blk, out_shape=jax.ShapeDtypeStruct((T, D), BF16), compiler_params=_cparams(("arbitrary",)),
    )(x, g, *deps)


def rmsnorm_bwd(x, g, dh, dres):
    blk = pl.BlockSpec((_ROWS, D), lambda i: (i, 0))
    row = pl.BlockSpec((1, D), lambda i: (0, 0))

    def body(x_ref, g_ref, dh_ref, dres_ref, dx_ref, dg_ref):
        _, vjp = jax.vjp(_rms, x_ref[...], g_ref[...])
        dx, dg = vjp(dh_ref[...])
        dx_ref[...] = dres_ref[...] + dx

        @pl.when(pl.program_id(0) == 0)
        def _():
            dg_ref[...] = jnp.zeros_like(dg_ref)

        dg_ref[...] += dg

    return pl.pallas_call(
        body, name="rmsnorm_bwd", grid=(T // _ROWS,), in_specs=[blk, row, blk, blk], out_specs=[blk, row],
        out_shape=[jax.ShapeDtypeStruct((T, D), F32), jax.ShapeDtypeStruct((1, D), F32)],
        compiler_params=_cparams(("arbitrary",)),
    )(x, g, dh, dres)


def _matmul(name, a, b, *, grid, a_spec, b_spec, o_spec, out_shape, ta=False, tb=False, k_axis=None, res=None, dep=None):
    dims = _dims(2, ta, tb)
    deps, dspecs = _dep_specs(dep, len(grid))

    def body(a_ref, b_ref, *rest):
        o_ref = rest[-1]
        prod = lax.dot_general(a_ref[...].astype(BF16), b_ref[...].astype(BF16), dims, preferred_element_type=F32)
        if res is not None:
            prod = prod + rest[0][...]
        if k_axis is None:
            o_ref[...] = prod.astype(o_ref.dtype)
        else:
            @pl.when(pl.program_id(k_axis) == 0)
            def _():
                o_ref[...] = prod

            @pl.when(pl.program_id(k_axis) > 0)
            def _():
                o_ref[...] += prod

    sem = tuple("arbitrary" for _ in grid)
    ins = [a, b] + ([res] if res is not None else []) + deps
    specs = [a_spec, b_spec] + ([o_spec] if res is not None else []) + dspecs
    return pl.pallas_call(
        body, name=name, grid=grid, in_specs=specs, out_specs=o_spec, out_shape=out_shape, compiler_params=_cparams(sem),
    )(*ins)


_IN_TN = P_END // 3


def mm_proj(h1, wp_in, l):
    return _matmul(
        "mm_proj", h1, wp_in, grid=(P_END // _IN_TN, T // _TM),
        a_spec=pl.BlockSpec((_TM, D), lambda j, i: (i, 0)),
        b_spec=pl.BlockSpec((None, D, _IN_TN), lambda j, i: (l, 0, j)),
        o_spec=pl.BlockSpec((_TM, _IN_TN), lambda j, i: (i, j)), out_shape=jax.ShapeDtypeStruct((T, P_END), F32))


def mm_dh1(dp2, wp_in, l):
    return _matmul(
        "mm_dh1", dp2, wp_in, grid=(T // _TM, P_END // _IN_TN), tb=True, k_axis=1,
        a_spec=pl.BlockSpec((_TM, _IN_TN), lambda i, k: (i, k)),
        b_spec=pl.BlockSpec((None, D, _IN_TN), lambda i, k: (l, 0, k)),
        o_spec=pl.BlockSpec((_TM, D), lambda i, k: (i, 0)), out_shape=jax.ShapeDtypeStruct((T, D), F32))


def mm_dwin(h1, dp2):
    return _matmul(
        "mm_dwin", h1, dp2, grid=(P_END // _IN_TN, D // _TM), ta=True,
        a_spec=pl.BlockSpec((T, _TM), lambda j, i: (0, i)),
        b_spec=pl.BlockSpec((T, _IN_TN), lambda j, i: (0, j)),
        o_spec=pl.BlockSpec((_TM, _IN_TN), lambda j, i: (i, j)), out_shape=jax.ShapeDtypeStruct((D, P_END), BF16))


def _mm_square(name, a, w, l, res, tb, dep=None):
    tn = 1024
    b_spec = (pl.BlockSpec((None, tn, D), lambda j, i: (l, j, 0)) if tb else pl.BlockSpec((None, D, tn), lambda j, i: (l, 0, j)))
    return _matmul(
        name, a, w, grid=(D // tn, T // _TM), tb=tb, res=res, dep=dep,
        a_spec=pl.BlockSpec((_TM, D), lambda j, i: (i, 0)), b_spec=b_spec,
        o_spec=pl.BlockSpec((_TM, tn), lambda j, i: (i, j)), out_shape=jax.ShapeDtypeStruct((T, D), F32))


def mm_out(mix, wg_out, l, x):
    return _mm_square("mm_out", mix, wg_out, l, x, False)


def mm_dmix(dx1, wg_out, l, dep=None):
    return _mm_square("mm_dmix", dx1, wg_out, l, None, True, dep)


def mm_dwout(mix, dx1):
    tn = 1024
    return _matmul(
        "mm_dwout", mix, dx1, grid=(D // tn, D // _TM), ta=True,
        a_spec=pl.BlockSpec((T, _TM), lambda j, i: (0, i)), b_spec=pl.BlockSpec((T, tn), lambda j, i: (0, j)),
        o_spec=pl.BlockSpec((_TM, tn), lambda j, i: (i, j)), out_shape=jax.ShapeDtypeStruct((D, D), BF16))


_GU_TN = GU_SHARD // 2


def mm_gu(h2, wg_gu, l):
    return _matmul(
        "mm_gu", h2, wg_gu, grid=(N_CHIPS, 2, T // _TM),
        a_spec=pl.BlockSpec((_TM, D), lambda s, j, i: (i, 0)),
        b_spec=pl.BlockSpec((None, None, D, _GU_TN), lambda s, j, i: (l, s, 0, j)),
        o_spec=pl.BlockSpec((_TM, _GU_TN), lambda s, j, i: (i, 2 * s + j)),
        out_shape=jax.ShapeDtypeStruct((T, 2 * FFN), F32))


_GU_NJ = FFN // _GU_TN


def mm_dh2(dgu, wg_gu, l):
    return _matmul(
        "mm_dh2", dgu, wg_gu, grid=(T // _TM, 2 * N_CHIPS), tb=True, k_axis=1,
        a_spec=pl.BlockSpec((None, _TM, _GU_TN), lambda i, k: (k // _GU_NJ, i, k % _GU_NJ)),
        b_spec=pl.BlockSpec((None, None, D, _GU_TN), lambda i, k: (l, k // 2, 0, k % 2)),
        o_spec=pl.BlockSpec((_TM, D), lambda i, k: (i, 0)), out_shape=jax.ShapeDtypeStruct((T, D), F32))


def mm_dwgu(h2, dgu):
    return _matmul(
        "mm_dwgu", h2, dgu, grid=(N_CHIPS, 2, D // _TM), ta=True,
        a_spec=pl.BlockSpec((T, _TM), lambda s, j, i: (0, i)),
        b_spec=pl.BlockSpec((None, T, _GU_TN), lambda s, j, i: ((2 * s + j) // _GU_NJ, 0, (2 * s + j) % _GU_NJ)),
        o_spec=pl.BlockSpec((None, _TM, _GU_TN), lambda s, j, i: (s, i, j)),
        out_shape=jax.ShapeDtypeStruct((N_CHIPS, D, GU_SHARD), BF16))


def mm_down(act, wg_down, l, x1):
    tn = 512
    return _matmul(
        "mm_down", act, wg_down, grid=(D // tn, T // _TM), res=x1,
        a_spec=pl.BlockSpec((_TM, FFN), lambda j, i: (i, 0)),
        b_spec=pl.BlockSpec((None, FFN, tn), lambda j, i: (l, 0, j)),
        o_spec=pl.BlockSpec((_TM, tn), lambda j, i: (i, j)), out_shape=jax.ShapeDtypeStruct((T, D), F32))


def mm_dact(dx2, wg_down, l, dep=None):
    tn = DOWN_SHARD
    return _matmul(
        "mm_dact", dx2, wg_down, grid=(FFN // tn, T // _TM), tb=True, dep=dep,
        a_spec=pl.BlockSpec((_TM, D), lambda j, i: (i, 0)),
        b_spec=pl.BlockSpec((None, tn, D), lambda j, i: (l, j, 0)),
        o_spec=pl.BlockSpec((_TM, tn), lambda j, i: (i, j)), out_shape=jax.ShapeDtypeStruct((T, FFN), F32))


def mm_dwdown(act, dx2):
    tm, tn = DOWN_SHARD, 512
    return _matmul(
        "mm_dwdown", act, dx2, grid=(D // tn, FFN // tm), ta=True,
        a_spec=pl.BlockSpec((T, tm), lambda j, i: (0, i)), b_spec=pl.BlockSpec((T, tn), lambda j, i: (0, j)),
        o_spec=pl.BlockSpec((tm, tn), lambda j, i: (i, j)), out_shape=jax.ShapeDtypeStruct((FFN, D), BF16))


_FF_TN = 1408


def _swiglu_fn(gt, up):
    return _silu(gt) * up


def swiglu_fwd(gu):
    nj = FFN // _FF_TN
    gt = pl.BlockSpec((_ROWS, _FF_TN), lambda i, j: (i, j))
    up = pl.BlockSpec((_ROWS, _FF_TN), lambda i, j: (i, nj + j))

    def body(gt_ref, up_ref, o_ref):
        o_ref[...] = _swiglu_fn(gt_ref[...], up_ref[...]).astype(BF16)

    return pl.pallas_call(
        body, name="swiglu_fwd", grid=(T // _ROWS, nj), in_specs=[gt, up], out_specs=gt,
        out_shape=jax.ShapeDtypeStruct((T, FFN), BF16), compiler_params=_cparams(("arbitrary", "arbitrary")),
    )(gu, gu)


def swiglu_bwd(gu, dact):
    nj = FFN // _FF_TN
    gt = pl.BlockSpec((_ROWS, _FF_TN), lambda i, j: (i, j))
    up = pl.BlockSpec((_ROWS, _FF_TN), lambda i, j: (i, nj + j))

    def body(gt_ref, up_ref, d_ref, dgu_ref):
        _, vjp = jax.vjp(_swiglu_fn, gt_ref[...], up_ref[...])
        dgt, dup = vjp(d_ref[...])
        dgu_ref[0] = dgt.astype(BF16)
        dgu_ref[1] = dup.astype(BF16)

    return pl.pallas_call(
        body, name="swiglu_bwd", grid=(T // _ROWS, nj), in_specs=[gt, up, gt],
        out_specs=pl.BlockSpec((2, _ROWS, _FF_TN), lambda i, j: (0, i, j)),
        out_shape=jax.ShapeDtypeStruct((2, T, FFN), BF16), compiler_params=_cparams(("arbitrary", "arbitrary")),
    )(gu, gu, dact)


def loss_and_grad(y, target):
    blk = pl.BlockSpec((_ROWS, D), lambda i: (i, 0))
    acc = pl.BlockSpec((8, HD), lambda i: (0, 0))

    def body(y_ref, t_ref, dy_ref, l_ref):
        err = y_ref[...] - t_ref[...]
        dy_ref[...] = err * (1.0 / D)

        @pl.when(pl.program_id(0) == 0)
        def _():
            l_ref[...] = jnp.zeros_like(l_ref)

        l_ref[...] += (0.5 / D) * jnp.sum(err * err)

    return pl.pallas_call(
        body, name="loss_and_grad", grid=(T // _ROWS,), in_specs=[blk, blk], out_specs=[blk, acc],
        out_shape=[jax.ShapeDtypeStruct((T, D), F32), jax.ShapeDtypeStruct((8, HD), F32)],
        compiler_params=_cparams(("arbitrary",)),
    )(y, target)


def adamw(w, g, m, v, name):
    rows, cols = w.shape
    tr = _ROWS if rows % _ROWS == 0 else rows
    blk = pl.BlockSpec((tr, cols), lambda i: (i, 0))

    def body(w_ref, g_ref, m_ref, v_ref, d_ref, nm_ref, nv_ref):
        gg = g_ref[...]
        nm = ADAM_B1 * m_ref[...] + (1.0 - ADAM_B1) * gg
        nv = ADAM_B2 * v_ref[...] + (1.0 - ADAM_B2) * (gg * gg)
        m_hat = nm / (1.0 - ADAM_B1 ** ADAM_STEP)
        v_hat = nv / (1.0 - ADAM_B2 ** ADAM_STEP)
        d_ref[...] = -ADAM_LR * (m_hat / (jnp.sqrt(v_hat) + ADAM_EPS) + ADAM_WD * w_ref[...])
        nm_ref[...] = nm
        nv_ref[...] = nv

    return pl.pallas_call(
        body, name=name, grid=(rows // tr,), in_specs=[blk] * 4, out_specs=[blk] * 3,
        out_shape=[jax.ShapeDtypeStruct(w.shape, F32)] * 3, compiler_params=_cparams(("arbitrary",)),
    )(w, g, m, v)


def shards_to_segments(w):
    valid = jnp.concatenate([w[..., s, :, :IN_SHARD] for s in range(N_CHIPS)], axis=-1)
    pad = jnp.zeros(valid.shape[:-1] + (P_CQ - GATE_COLS,), w.dtype)
    return jnp.concatenate([valid[..., :GATE_COLS], pad, valid[..., GATE_COLS:]], axis=-1)


def segments_to_shards(w):
    valid = jnp.concatenate([w[:, :GATE_COLS], w[:, P_CQ:]], axis=1)
    pad = ((0, 0), (0, IN_SHARD_PAD - IN_SHARD))
    return jnp.stack([jnp.pad(valid[:, s * IN_SHARD:(s + 1) * IN_SHARD], pad) for s in range(N_CHIPS)])


def mixers_forward(x, w_in, sp, dep=None):
    h1 = rmsnorm_fwd(x, sp["norm1_g"], dep)
    p2 = mm_proj(h1, w_in, 0)
    y_a = sgu_fwd(p2, sp["sgu_norm_g"], sp["w_spatial"], sp["b_spatial"])
    y_b, saved_b = gdn_forward(p2, sp["conv_w"], sp["a_log"], sp["dt_bias"], sp["o_norm_g"])
    y_c = dattn_fwd(p2, sp["q_norm_g"], sp["k_norm_g"], _alibi_slopes())
    mix = jnp.concatenate([y_a, y_b, y_c], axis=1)
    return mix, (x, h1, p2, saved_b, mix)


def ffn_forward(x, mix, wg, sp):
    x1 = mm_out(mix, wg["out"], 0, x)
    h2 = rmsnorm_fwd(x1, sp["norm2_g"])
    gu = mm_gu(h2, wg["gu"], 0)
    act = swiglu_fwd(gu)
    x2 = mm_down(act, wg["down"], 0, x1)
    return x2, (x1, h2, gu, act)


def ffn_backward(dx2, wg, sp, saved, dep=None):
    x1, h2, gu, act = saved
    dact = mm_dact(dx2, wg["down"], 0, dep)
    dw_down = mm_dwdown(act, dx2)
    dgu = swiglu_bwd(gu, dact)
    dw_gu = mm_dwgu(h2, dgu)
    dh2 = mm_dh2(dgu, wg["gu"], 0)
    dx1, dnorm2 = rmsnorm_bwd(x1, sp["norm2_g"], dh2, dx2)
    return dx1, dnorm2, dw_gu, dw_down


def mixers_backward(dx1, wg, sp, saved, dep=None):
    x, h1, p2, saved_b, mix = saved
    dmix = mm_dmix(dx1, wg["out"], 0, dep)
    dw_out = mm_dwout(mix, dx1)
    du, dv, dsg, dws, dbs = sgu_bwd(p2, sp["sgu_norm_g"], sp["w_spatial"], sp["b_spatial"], dmix)
    dseg_b, dconv, dal, ddtb, dog = gdn_backward(p2, sp["conv_w"], sp["a_log"], sp["dt_bias"], sp["o_norm_g"], saved_b, dmix)
    dcq, dck, dcv, dqg, dkg = dattn_bwd(p2, sp["q_norm_g"], sp["k_norm_g"], _alibi_slopes(), dmix)
    dp2 = jnp.concatenate([du, dv] + dseg_b + [dcq, dck, dcv], axis=1)
    dw_in = segments_to_shards(mm_dwin(h1, dp2))
    dh1 = mm_dh1(dp2, wg["in"], 0)
    dx, dnorm1 = rmsnorm_bwd(x, sp["norm1_g"], dh1, dx1)
    small = {"norm1_g": dnorm1, "sgu_norm_g": dsg, "w_spatial": dws, "b_spatial": dbs, "conv_w": dconv, "a_log": dal,
             "dt_bias": ddtb, "o_norm_g": dog, "q_norm_g": dqg, "k_norm_g": dkg}
    return dx, dw_in, dw_out, small


def layer_forward(x, wg, sp, dep=None):
    mix, saved_m = mixers_forward(x, wg["in"], sp, dep)
    x2, saved_f = ffn_forward(x, mix, wg, sp)
    return x2, (saved_m, saved_f)


def layer_backward(dx2, wg, sp, saved, dep=None):
    dx1, dnorm2, dw_gu, dw_down = ffn_backward(dx2, wg, sp, saved[1], dep)
    dx, dw_in, dw_out, small = mixers_backward(dx1, wg, sp, saved[0])
    return dx, {"in": dw_in, "out": dw_out, "gu": dw_gu, "down": dw_down}, {**small, "norm2_g": dnorm2}


_HBM = pl.BlockSpec(memory_space=pltpu.HBM)
_MESH = pl.DeviceIdType.MESH


def _place():
    x, y, c = lax.axis_index("x"), lax.axis_index("y"), lax.axis_index("c")
    chips = [(1 - x, y), (x, 1 - y), (1 - x, 1 - y)]
    return x, y, c, chips


def _rcopy(src, dst, ssem, rsem, dev):
    return pltpu.make_async_remote_copy(src_ref=src, dst_ref=dst, send_sem=ssem, recv_sem=rsem, device_id=dev,
                                        device_id_type=_MESH)


def _xor(a, b):
    return a + b - 2 * a * b


def gather_weights(shards):
    n = len(shards)

    def body(*refs):
        ins, outs = refs[:n], refs[n:2 * n]
        s_ici, r_ici, s_d2d, r_d2d, s_own, r_own = refs[2 * n:]
        x, y, c, _ = _place()
        s = 2 * x + y
        sibling = (x, y, 1 - c)
        nbr = [(1 - x, y), (x, 1 - y)]
        src_chip = (_xor(x, 1 - c), _xor(y, c))
        dst_chip = (_xor(x, c), _xor(y, 1 - c))
        t_src = 2 * src_chip[0] + src_chip[1]
        t_oth = 2 * dst_chip[0] + dst_chip[1]
        t_dia = 2 * (1 - x) + (1 - y)
        sends = []
        for a in range(n):
            for u in range(2):
                sends.append(_rcopy(ins[a].at[u], outs[a].at[s, u], s_own.at[a, u], r_own.at[a, u], sibling))
            for k in range(2):
                sends.append(_rcopy(ins[a].at[c], outs[a].at[s, c], s_ici.at[a, k], r_ici.at[a, k], (*nbr[k], c)))
        for cp in sends:
            cp.start()

        def landed(a, t, k):
            _rcopy(ins[a].at[c], outs[a].at[t, c], s_ici.at[a, k], r_ici.at[a, k], sibling).wait_recv()
            cp = _rcopy(outs[a].at[t, c], outs[a].at[t, c], s_d2d.at[a, k], r_d2d.at[a, k], sibling)
            cp.start()
            sends.append(cp)

        for a in range(n):
            landed(a, t_src, c)
            fwd = _rcopy(outs[a].at[t_src, c], outs[a].at[t_src, c], s_ici.at[a, 2], r_ici.at[a, 2], (*dst_chip, c))
            fwd.start()
            sends.append(fwd)
        for a in range(n):
            landed(a, t_oth, 1 - c)
        for a in range(n):
            landed(a, t_dia, 2)
        for a in range(n):
            for u in range(2):
                _rcopy(ins[a].at[u], outs[a].at[s, u], s_own.at[a, u], r_own.at[a, u], sibling).wait_recv()
            for k, t in enumerate([2 * nbr[0][0] + nbr[0][1], 2 * nbr[1][0] + nbr[1][1], t_dia]):
                _rcopy(ins[a].at[1 - c], outs[a].at[t, 1 - c], s_d2d.at[a, k], r_d2d.at[a, k], sibling).wait_recv()
        for cp in sends:
            cp.wait_send()

    dma = lambda k: pltpu.SemaphoreType.DMA((n, k))
    return pl.pallas_call(
        body, name="gather_weights", in_specs=[_HBM] * n, out_specs=[_HBM] * n,
        out_shape=[jax.ShapeDtypeStruct((N_CHIPS,) + w.shape, w.dtype) for w in shards],
        scratch_shapes=[dma(3), dma(3), dma(3), dma(3), dma(2), dma(2)],
    )(*shards)


_SEM = pl.BlockSpec(memory_space=pltpu.SEMAPHORE)
_SIDE_EFFECT = pltpu.SideEffectType.DATAFLOW_SIDE_EFFECTING


def _in_hbm(a):
    return pltpu.with_memory_space_constraint(a, pltpu.HBM)


def _split_copy(name, srcs, land_shapes, n_sems, copies):
    n, m = len(srcs), len(land_shapes)
    thru = [pltpu.HBM(a.shape, a.dtype) for a in srcs] + [pltpu.HBM(s.shape, s.dtype) for s in land_shapes]
    sems = (pltpu.SemaphoreType.DMA((n_sems,)), pltpu.SemaphoreType.DMA((n_sems,)))

    def start(dep=None):
        deps = [] if dep is None else [dep]

        def body(*refs):
            ins, lands = refs[:n], refs[n:n + m]
            ssem, rsem, token = refs[n + m + len(deps)], refs[n + m + len(deps) + 1], refs[-1]
            for cp in copies(ins, lands, ssem, rsem)[0]:
                cp.start()
            token[...] = jnp.zeros_like(token)

        out = pl.pallas_call(
            body, name=name + "_start", out_shape=(*sems, *thru, jax.ShapeDtypeStruct((8, HD), F32)),
            in_specs=[_HBM] * (n + m) + [pl.BlockSpec(memory_space=pl.ANY)] * len(deps),
            out_specs=(_SEM, _SEM, *[_HBM] * (n + m), pl.BlockSpec(memory_space=pltpu.VMEM)),
            input_output_aliases={i: 2 + i for i in range(n + m)},
            compiler_params=pltpu.CompilerParams(has_side_effects=_SIDE_EFFECT),
        )(*[_in_hbm(a) for a in srcs], *[_in_hbm(lax.empty(s.shape, s.dtype)) for s in land_shapes], *deps)
        return out[:-1], out[-1]

    def wait(state, after):
        def body(*refs):
            ins, lands, ssem, rsem = refs[:n], refs[n:n + m], refs[n + m], refs[n + m + 1]
            sent, arrivals = copies(ins, lands, ssem, rsem)
            for cp in sent:
                cp.wait_send()
            for cp in arrivals:
                cp.wait_recv()

        out = pl.pallas_call(
            body, name=name + "_wait", out_shape=tuple(thru),
            in_specs=[_HBM] * (n + m) + [_SEM, _SEM, pl.BlockSpec(memory_space=pl.ANY)], out_specs=[_HBM] * (n + m),
            input_output_aliases={i: i for i in range(n + m)},
            compiler_params=pltpu.CompilerParams(has_side_effects=_SIDE_EFFECT),
        )(*state[2:], state[0], state[1], after)
        return out[n:]

    return start, wait


def gather_direct(shards, tag):
    n = len(shards)

    def copies(ins, lands, ssem, rsem):
        x, y, c, chips = _place()
        s = 2 * x + y
        sibling = (x, y, 1 - c)
        sent, arrivals = [], []
        for a in range(n):
            for u in range(2):
                cp = _rcopy(ins[a].at[u], lands[a].at[s, u], ssem.at[5 * a + u], rsem.at[5 * a + u], sibling)
                sent.append(cp)
                arrivals.append(cp)
            for j, (cx, cy) in enumerate(chips):
                k = 5 * a + 2 + j
                sent.append(_rcopy(ins[a].at[c], lands[a].at[s, c], ssem.at[k], rsem.at[k], (cx, cy, c)))
                arrivals.append(_rcopy(ins[a].at[c], lands[a].at[2 * cx + cy, c], ssem.at[k], rsem.at[k], (cx, cy, c)))
        return sent, arrivals

    lands = [jax.ShapeDtypeStruct((N_CHIPS,) + w.shape, w.dtype) for w in shards]
    return _split_copy("gather_direct_" + tag, shards, lands, 5 * n, copies)


def pass_to_sibling(lands):
    n = len(lands)

    def body(*refs):
        ins = refs[:n]
        ssem, rsem = refs[2 * n:]
        x, y, c, chips = _place()
        sibling = (x, y, 1 - c)
        cps, arrivals = [], []
        for a in range(n):
            for j, (cx, cy) in enumerate(chips):
                t = 2 * cx + cy
                cps.append(_rcopy(ins[a].at[t, c], ins[a].at[t, c], ssem.at[a, j], rsem.at[a, j], sibling))
                arrivals.append(_rcopy(ins[a].at[t, c], ins[a].at[t, 1 - c], ssem.at[a, j], rsem.at[a, j], sibling))
        for cp in cps:
            cp.start()
        for cp, ar in zip(cps, arrivals):
            cp.wait_send()
            ar.wait_recv()

    return pl.pallas_call(
        body, name="pass_to_sibling", in_specs=[_HBM] * n, out_specs=[_HBM] * n,
        out_shape=[jax.ShapeDtypeStruct(a.shape, a.dtype) for a in lands], input_output_aliases={a: a for a in range(n)},
        scratch_shapes=[pltpu.SemaphoreType.DMA((n, 3)), pltpu.SemaphoreType.DMA((n, 3))],
    )(*lands)


def exchange_halves(grads):
    n = len(grads)

    def body(*refs):
        ins, outs = refs[:n], refs[n:2 * n]
        ssem, rsem = refs[2 * n:]
        x, y, c, _ = _place()
        cps = []
        for a in range(n):
            h = grads[a].shape[1] // 2
            cps.append(_rcopy(ins[a].at[:, pl.ds((1 - c) * h, h)], outs[a], ssem.at[a], rsem.at[a], (x, y, 1 - c)))
        for cp in cps:
            cp.start()
        for cp in cps:
            cp.wait()

    return pl.pallas_call(
        body, name="exchange_halves", in_specs=[_HBM] * n, out_specs=[_HBM] * n,
        out_shape=[jax.ShapeDtypeStruct((g.shape[0], g.shape[1] // 2, g.shape[2]), g.dtype) for g in grads],
        scratch_shapes=[pltpu.SemaphoreType.DMA((n,)), pltpu.SemaphoreType.DMA((n,))],
    )(*grads)


def scatter_to_chips(parts):
    n = len(parts)

    def body(*refs):
        ins, outs = refs[:n], refs[n:2 * n]
        ssem, rsem = refs[2 * n:]
        x, y, c, chips = _place()
        cps = [_rcopy(ins[a].at[2 * cx + cy], outs[a].at[j], ssem.at[a, j], rsem.at[a, j], (cx, cy, c))
               for a in range(n) for j, (cx, cy) in enumerate(chips)]
        for cp in cps:
            cp.start()
        for cp in cps:
            cp.wait()

    return pl.pallas_call(
        body, name="scatter_to_chips", in_specs=[_HBM] * n, out_specs=[_HBM] * n,
        out_shape=[jax.ShapeDtypeStruct((3,) + p.shape[1:], p.dtype) for p in parts],
        scratch_shapes=[pltpu.SemaphoreType.DMA((n, 3)), pltpu.SemaphoreType.DMA((n, 3))],
    )(*parts)


def scatter_direct(parts, tag):
    n = len(parts)

    def copies(ins, lands, ssem, rsem):
        x, y, c, chips = _place()
        cps = [_rcopy(ins[a].at[2 * cx + cy], lands[a].at[j], ssem.at[3 * a + j], rsem.at[3 * a + j], (cx, cy, c))
               for a in range(n) for j, (cx, cy) in enumerate(chips)]
        return cps, cps

    lands = [jax.ShapeDtypeStruct((3,) + p.shape[1:], p.dtype) for p in parts]
    return _split_copy("scatter_direct_" + tag, parts, lands, 3 * n, copies)


def share_halves(halves):
    n = len(halves)

    def body(*refs):
        ins, outs = refs[:n], refs[n:2 * n]
        ssem, rsem = refs[2 * n:]
        x, y, c, _ = _place()
        cps = [_rcopy(ins[i], outs[i], ssem.at[i], rsem.at[i], (x, y, 1 - c)) for i in range(n)]
        for cp in cps:
            cp.start()
        for cp in cps:
            cp.wait()

    return pl.pallas_call(
        body, name="share_halves", in_specs=[_HBM] * n, out_specs=[_HBM] * n,
        out_shape=[jax.ShapeDtypeStruct(h.shape, h.dtype) for h in halves],
        scratch_shapes=[pltpu.SemaphoreType.DMA((n,)), pltpu.SemaphoreType.DMA((n,))],
    )(*halves)


def adamw_shard(w, m, v, mine, theirs, c, name):
    _, r, cw = w.shape
    h, cg = mine[0].shape
    tr = next(t for t in (256, 176, 128) if h % t == 0 and t * cg * 4 <= (3 << 19))
    nb = h // tr
    wblk = pl.BlockSpec((None, tr, cw), lambda l, i, c_ref: (l, i, 0))
    gblk = lambda layer, own: pl.BlockSpec((tr, cg), lambda l, i, c_ref: (_held_block(l, i, c_ref, layer, own, nb), 0))
    return _adamw_halves(w, m, v, mine, theirs, c, name, (DEPTH, r // tr), wblk, gblk, nb, cw)


def _held_block(l, i, c_ref, layer, own, nb):
    in_use = (l == layer) & (((i // nb) == c_ref[0]) == own)
    return jnp.where(in_use, i % nb, 0)


def _adamw_halves(w, m, v, mine, theirs, c, name, grid, wblk, gblk, nb, cw):
    def body(c_ref, w_ref, m_ref, v_ref, m0, m1, t0, t1, g_ref, d_ref, nm_ref, nv_ref):
        is_mine = (pl.program_id(1) // nb) == c_ref[0]
        first = pl.program_id(0) == 0
        gg = jnp.where(is_mine, jnp.where(first, m0[:, :cw], m1[:, :cw]), jnp.where(first, t0[:, :cw], t1[:, :cw]))
        nm = ADAM_B1 * m_ref[...] + (1.0 - ADAM_B1) * gg
        nv = ADAM_B2 * v_ref[...] + (1.0 - ADAM_B2) * (gg * gg)
        m_hat = nm / (1.0 - ADAM_B1 ** ADAM_STEP)
        v_hat = nv / (1.0 - ADAM_B2 ** ADAM_STEP)
        g_ref[...] = gg
        d_ref[...] = -ADAM_LR * (m_hat / (jnp.sqrt(v_hat) + ADAM_EPS) + ADAM_WD * w_ref[...])
        nm_ref[...] = nm
        nv_ref[...] = nv

    return pl.pallas_call(
        body, name=name,
        grid_spec=pltpu.PrefetchScalarGridSpec(
            num_scalar_prefetch=1, grid=grid,
            in_specs=[wblk] * 3 + [gblk(0, True), gblk(1, True), gblk(0, False), gblk(1, False)], out_specs=[wblk] * 4),
        out_shape=[jax.ShapeDtypeStruct(w.shape, F32)] * 4, compiler_params=_cparams(("arbitrary", "arbitrary")),
    )(c, w, m, v, mine[0], mine[1], theirs[0], theirs[1])


def adamw_shard_t(wt, mt, vt, mine_t, theirs_t, c, name):
    _, cw, r = wt.shape
    h = mine_t[0].shape[1]
    tc = 256
    nb = h // tc
    wblk = pl.BlockSpec((None, cw, tc), lambda l, j, c_ref: (l, 0, j))
    gblk = lambda layer, own: pl.BlockSpec((cw, tc), lambda l, j, c_ref: (0, _held_block(l, j, c_ref, layer, own, nb)))
    return _adamw_halves(wt, mt, vt, mine_t, theirs_t, c, name, (DEPTH, r // tc), wblk, gblk, nb, cw)


def _half_rows(h, cols):
    for tr in (512, 256, 352, 128, 64):
        if h % tr == 0 and tr * cols * 4 <= 6 * 1024 * 1024:
            return tr
    raise ValueError((h, cols))


def add_sibling(grad, recv, c):
    _, r, cols = grad.shape
    h = r // 2
    tr = _half_rows(h, cols)
    nb = h // tr

    def body(c_ref, g_ref, r_ref, o_ref):
        o_ref[...] = (g_ref[...].astype(F32) + r_ref[...].astype(F32)).astype(BF16)

    return pl.pallas_call(
        body, name="add_sibling",
        grid_spec=pltpu.PrefetchScalarGridSpec(
            num_scalar_prefetch=1, grid=(N_CHIPS, nb),
            in_specs=[pl.BlockSpec((None, tr, cols), lambda t, i, c_ref: (t, c_ref[0] * nb + i, 0)),
                      pl.BlockSpec((None, tr, cols), lambda t, i, c_ref: (t, i, 0))],
            out_specs=pl.BlockSpec((None, tr, cols), lambda t, i, c_ref: (t, i, 0))),
        out_shape=jax.ShapeDtypeStruct((N_CHIPS, h, cols), BF16), compiler_params=_cparams(("arbitrary", "arbitrary")),
    )(c, grad, recv)


def add_chips(part, recv, s):
    _, h, cols = part.shape
    tr = _half_rows(h, cols)

    def body(s_ref, p_ref, r_ref, o_ref):
        o_ref[...] = ((p_ref[...].astype(F32) + r_ref[0].astype(F32)) + r_ref[1].astype(F32)) + r_ref[2].astype(F32)

    return pl.pallas_call(
        body, name="add_chips",
        grid_spec=pltpu.PrefetchScalarGridSpec(
            num_scalar_prefetch=1, grid=(h // tr,),
            in_specs=[pl.BlockSpec((None, tr, cols), lambda i, s_ref: (s_ref[0], i, 0)),
                      pl.BlockSpec((3, tr, cols), lambda i, s_ref: (0, i, 0))],
            out_specs=pl.BlockSpec((tr, cols), lambda i, s_ref: (i, 0))),
        out_shape=jax.ShapeDtypeStruct((h, cols), F32), compiler_params=_cparams(("arbitrary",)),
    )(s, part, recv)


def allreduce_small(vec):
    rows = vec.shape[0]

    def body(v_ref, o_ref, buf, ssem, rsem, lsem):
        x, y, c, chips = _place()
        me, sibling = (x, y, c), (x, y, 1 - c)

        def blk(px, py, pc):
            return buf.at[4 * px + 2 * py + pc]

        def copy(k, block, to, src=None):
            return _rcopy(blk(*block) if src is None else src, blk(*block), ssem.at[k], rsem.at[k], to)

        mine = pltpu.make_async_copy(v_ref, blk(*me), lsem)
        mine.start()
        first = [copy(0, me, sibling, src=v_ref)] + [copy(1 + j, me, (*chip, c), src=v_ref) for j, chip in enumerate(chips)]
        for cp in first:
            cp.start()
        passed = [copy(4 + j, (*chip, c), sibling) for j, chip in enumerate(chips)]
        for j, chip in enumerate(chips):
            copy(1 + j, (*chip, c), me).wait_recv()
            passed[j].start()
        copy(0, sibling, me).wait_recv()
        for j, chip in enumerate(chips):
            copy(4 + j, (*chip, 1 - c), me).wait_recv()
        for cp in first + passed:
            cp.wait_send()
        mine.wait()
        acc = buf[0]
        for d in range(1, N_DEV):
            acc = acc + buf[d]
        o_ref[...] = acc

    vm = pl.BlockSpec(memory_space=pltpu.VMEM)
    return pl.pallas_call(
        body, name="allreduce_small", in_specs=[vm], out_specs=vm, out_shape=jax.ShapeDtypeStruct(vec.shape, F32),
        scratch_shapes=[pltpu.VMEM((N_DEV, rows, HD), F32), pltpu.SemaphoreType.DMA((7,)), pltpu.SemaphoreType.DMA((7,)),
                        pltpu.SemaphoreType.DMA],
        compiler_params=pltpu.CompilerParams(vmem_limit_bytes=VMEM_LIMIT),
    )(vec)


SMALL_NAMES = ("norm1_g", "sgu_norm_g", "w_spatial", "b_spatial", "conv_w", "a_log", "dt_bias", "o_norm_g", "q_norm_g",
               "k_norm_g", "norm2_g")


def small_params(l, p, conv_full):
    return {"norm1_g": p["norm1_g"][l][None], "sgu_norm_g": p["sgu_norm_g"][l][:, None, :], "w_spatial": p["w_spatial"][l],
            "b_spatial": p["b_spatial"][l][..., None], "conv_w": conv_full[l], "a_log": p["a_log"][l], "dt_bias": p["dt_bias"][l],
            "o_norm_g": p["o_norm_g"][l][None], "q_norm_g": p["q_norm_g"][l][None], "k_norm_g": p["k_norm_g"][l][None],
            "norm2_g": p["norm2_g"][l][None]}


def local_step(x, target, wg, sps):
    saved = []
    for l in range(DEPTH):
        x, s = layer_forward(x, wg[l], sps[l])
        saved.append(s)
    dx, loss = loss_and_grad(x, target)
    bigs, smalls = [None] * DEPTH, [None] * DEPTH
    for l in reversed(range(DEPTH)):
        dx, bigs[l], smalls[l] = layer_backward(dx, wg[l], sps[l], saved[l])
    return loss, dx, bigs, smalls


_PACK_TILE = 8 * HD


def _pack(arrays):
    flat = jnp.concatenate([a.reshape(-1) for a in arrays])
    pad = -flat.shape[0] % _PACK_TILE
    return jnp.pad(flat, (0, pad)).reshape(-1, HD)


def _unpack(packed, shapes):
    flat, out, off = packed.reshape(-1), [], 0
    for shp in shapes:
        n = int(np.prod(shp))
        out.append(flat[off:off + n].reshape(shp))
        off += n
    return out


BIG_NAMES = ("in", "out", "gu", "down")
WEIGHT_ORDER = ("norm1_g", "w_in", "sgu_norm_g", "w_spatial", "b_spatial", "conv_w", "a_log", "dt_bias", "o_norm_g", "q_norm_g",
                "k_norm_g", "w_out", "norm2_g", "w_gate_up", "w_down")


def kernel(x, norm1_g, w_in, sgu_norm_g, w_spatial, b_spatial, conv_w, a_log, dt_bias, o_norm_g, q_norm_g, k_norm_g, w_out, norm2_g, w_gate_up, w_down, loss_target, m_norm1_g, m_w_in, m_sgu_norm_g, m_w_spatial, m_b_spatial, m_conv_w, m_a_log, m_dt_bias, m_o_norm_g, m_q_norm_g, m_k_norm_g, m_w_out, m_norm2_g, m_w_gate_up, m_w_down, v_norm1_g, v_w_in, v_sgu_norm_g, v_w_spatial, v_b_spatial, v_conv_w, v_a_log, v_dt_bias, v_o_norm_g, v_q_norm_g, v_k_norm_g, v_w_out, v_norm2_g, v_w_gate_up, v_w_down):
    w = dict(norm1_g=norm1_g, w_in=w_in, sgu_norm_g=sgu_norm_g, w_spatial=w_spatial, b_spatial=b_spatial, conv_w=conv_w,
             a_log=a_log, dt_bias=dt_bias, o_norm_g=o_norm_g, q_norm_g=q_norm_g, k_norm_g=k_norm_g, w_out=w_out,
             norm2_g=norm2_g, w_gate_up=w_gate_up, w_down=w_down)
    m = dict(norm1_g=m_norm1_g, w_in=m_w_in, sgu_norm_g=m_sgu_norm_g, w_spatial=m_w_spatial, b_spatial=m_b_spatial,
             conv_w=m_conv_w, a_log=m_a_log, dt_bias=m_dt_bias, o_norm_g=m_o_norm_g, q_norm_g=m_q_norm_g, k_norm_g=m_k_norm_g,
             w_out=m_w_out, norm2_g=m_norm2_g, w_gate_up=m_w_gate_up, w_down=m_w_down)
    v = dict(norm1_g=v_norm1_g, w_in=v_w_in, sgu_norm_g=v_sgu_norm_g, w_spatial=v_w_spatial, b_spatial=v_b_spatial,
             conv_w=v_conv_w, a_log=v_a_log, dt_bias=v_dt_bias, o_norm_g=v_o_norm_g, q_norm_g=v_q_norm_g, k_norm_g=v_k_norm_g,
             w_out=v_w_out, norm2_g=v_norm2_g, w_gate_up=v_w_gate_up, w_down=v_w_down)
    chip = (2 * lax.axis_index("x") + lax.axis_index("y")).astype(jnp.int32)
    core = lax.axis_index("c").astype(jnp.int32)

    in_pad = IN_SHARD_PAD - IN_SHARD
    w_in_pad = jnp.pad(w_in, ((0, 0), (0, 0), (0, in_pad)))

    halves_of = lambda a: a.reshape(2, a.shape[0] // 2, a.shape[1])
    bf_halves = lambda a: halves_of(a.astype(BF16))

    def ffn_shards(l):
        return [bf_halves(w_gate_up[l]), bf_halves(w_down[l]), bf_halves(w_out[l])]

    def mixer_shards(l):
        return [bf_halves(w_in_pad[l]), halves_of(conv_w[l])]

    def mixer_weights(g):
        g_in, g_conv = g
        return (shards_to_segments(g_in.reshape(N_CHIPS, D, IN_SHARD_PAD))[None],
                g_conv.reshape(N_CHIPS, B_CONV, -1).transpose(1, 0, 2).reshape(B_CONV, 3 * B_WIDTH))

    def ffn_weights(g, w_in_seg):
        g_gu, g_down, g_out = g
        return {"in": w_in_seg, "out": g_out.reshape(1, D, D), "gu": g_gu.reshape(1, N_CHIPS, D, GU_SHARD),
                "down": g_down.reshape(1, FFN, D)}

    def layer_params(l, conv_full):
        return small_params(0, {n: w[n][l:l + 1] for n in SMALL_NAMES if n != "conv_w"}, conv_full[None])

    w_in0, conv0 = mixer_weights(gather_weights(mixer_shards(0)))
    start_a, wait_a = gather_direct(ffn_shards(0), "ffn0")
    start_b, wait_b = gather_direct(mixer_shards(1) + ffn_shards(1), "layer1")
    state_a, token_a = start_a()
    state_b, token_b = start_b(token_a)
    sps = [layer_params(0, conv0), None]
    mix0, saved_m0 = mixers_forward(x[0], w_in0, sps[0], dep=token_b)
    wg0 = ffn_weights(pass_to_sibling(wait_a(state_a, mix0)), w_in0)
    x1, saved_f0 = ffn_forward(x[0], mix0, wg0, sps[0])
    g1 = pass_to_sibling(wait_b(state_b, x1))
    w_in1, conv1 = mixer_weights(g1[:2])
    wg1 = ffn_weights(g1[2:], w_in1)
    sps[1] = layer_params(1, conv1)
    x2, saved1 = layer_forward(x1, wg1, sps[1])
    dx, loss_tile = loss_and_grad(x2, loss_target[0])

    def to_chip_parts(grads):
        return [add_sibling(g, r, core.reshape(1)) for g, r in zip(grads, exchange_halves(grads))]

    def start_scatter(grads, tag):
        parts = to_chip_parts(grads)
        start, wait = scatter_direct(parts, tag)
        state, token = start()
        return parts, wait, state, token

    smalls = [None] * DEPTH
    by_chip_out = lambda t: t.reshape(N_CHIPS, OUT_SHARD, D)
    by_chip_down = lambda t: t.reshape(N_CHIPS, DOWN_SHARD, D)
    dx1, dnorm2_1, dw_gu1, dw_down1 = ffn_backward(dx, wg1, sps[1], saved1[1])
    pf1, wait_f1, st_f1, tok_f1 = start_scatter([dw_gu1, by_chip_down(dw_down1)], "ffn1")
    dx, dw_in1, dw_out1, small1 = mixers_backward(dx1, wg1, sps[1], saved1[0], dep=tok_f1)
    smalls[1] = {**small1, "norm2_g": dnorm2_1}
    pm1, wait_m1, st_m1, tok_m1 = start_scatter([dw_in1, by_chip_out(dw_out1)], "mix1")
    dx1, dnorm2_0, dw_gu0, dw_down0 = ffn_backward(dx, wg0, sps[0], saved_f0, dep=tok_m1)
    pf0, wait_f0, st_f0, tok_f0 = start_scatter([dw_gu0, by_chip_down(dw_down0)], "ffn0")
    dx, dw_in0, dw_out0, small0 = mixers_backward(dx1, wg0, sps[0], saved_m0, dep=tok_f0)
    smalls[0] = {**small0, "norm2_g": dnorm2_0}
    pm0 = to_chip_parts([dw_in0, by_chip_out(dw_out0)])
    rm0 = list(scatter_to_chips(pm0))
    rf0, rm1, rf1 = (list(wt(st, dx)) for wt, st in ((wait_f0, st_f0), (wait_m1, st_m1), (wait_f1, st_f1)))
    parts = [pm0[0], pm0[1], pf0[0], pf0[1], pm1[0], pm1[1], pf1[0], pf1[1]]
    from_chips = [rm0[0], rm0[1], rf0[0], rf0[1], rm1[0], rm1[1], rf1[0], rf1[1]]
    halves = [add_chips(p, r, chip.reshape(1)) for p, r in zip(parts, from_chips)]
    theirs = share_halves(halves)
    grad, delta, new_m, new_v = {}, {}, {}, {}
    for a, n in enumerate(("w_in", "w_out", "w_gate_up", "w_down")):
        pick = lambda lst: [lst[l * len(BIG_NAMES) + a] for l in range(DEPTH)]
        if n == "w_in":
            tr_ = lambda t: jnp.swapaxes(t, -1, -2)
            cut = lambda lst: [tr_(t[:, :IN_SHARD]) for t in pick(lst)]
            res = adamw_shard_t(tr_(w[n]), tr_(m[n]), tr_(v[n]), cut(halves), cut(theirs), core.reshape(1), "adamw_" + n)
            grad[n], delta[n], new_m[n], new_v[n] = (tr_(t) for t in res)
        else:
            grad[n], delta[n], new_m[n], new_v[n] = adamw_shard(w[n], m[n], v[n], pick(halves), pick(theirs),
                                                                core.reshape(1), "adamw_" + n)

    stacked = [jnp.stack([smalls[l][n] for l in range(DEPTH)]) for n in SMALL_NAMES]
    total = allreduce_small(_pack(stacked + [loss_tile[0, :1]]))
    shapes = [(DEPTH, B_CONV, 3 * B_WIDTH) if n == "conv_w" else w[n].shape for n in SMALL_NAMES]
    small_grads = dict(zip(SMALL_NAMES, _unpack(total, shapes + [(1,)])[:-1]))
    loss = _unpack(total, shapes + [(1,)])[-1][0]
    conv_cols = conv_w.shape[-1]
    small_grads["conv_w"] = lax.dynamic_slice_in_dim(small_grads["conv_w"], chip * conv_cols, conv_cols, axis=2)
    grad.update(small_grads)

    sshapes = [w[n].shape for n in SMALL_NAMES]
    packed = [_pack([d[n] for n in SMALL_NAMES]) for d in (w, grad, m, v)]
    for dst, t in zip((delta, new_m, new_v), adamw(*packed, "adamw_small")):
        dst.update(zip(SMALL_NAMES, _unpack(t, sshapes)))

    out = [loss, dx[None]]
    for d in (grad, delta, new_m, new_v):
        out += [d[n] for n in WEIGHT_ORDER]
    return tuple(out)
```

```python
import functools
import math

import numpy as np
import jax
import jax.numpy as jnp
from jax import lax
from jax.experimental import pallas as pl
from jax.experimental.pallas import tpu as pltpu

F32 = jnp.float32
BF16 = jnp.bfloat16
HI = lax.Precision.HIGH

T = 2048
D = 2048
DEPTH = 2
HD = 128
A_GROUPS, A_WIDTH, A_CHUNK = 4, 512, 128
B_HEADS, B_WIDTH, B_CONV, B_CHUNK = 6, 768, 4, 64
C_HEADS, C_WIDTH, C_BLOCK = 6, 768, 128
C_BRANCHES = ((128, 1), (512, 4), (2048, 16))
FFN = 5632
IN_TOTAL = 6412
EPS = 1e-6
N_CHIPS = 4
N_DEV = 8
IN_SHARD = IN_TOTAL // N_CHIPS
IN_SHARD_PAD = 1664
GU_SHARD = 2 * FFN // N_CHIPS
OUT_SHARD = D // N_CHIPS
DOWN_SHARD = FFN // N_CHIPS
P_AU, P_AV, P_BQ, P_BK, P_BV, P_BG, P_BB, P_CQ, P_CK, P_CV, P_END = (
    0, 512, 1024, 1792, 2560, 3328, 4096, 4224, 4992, 5760, 6528)
GATE_COLS = 4108
VMEM_LIMIT = 56 * 1024 * 1024

ADAM_LR, ADAM_B1, ADAM_B2, ADAM_EPS, ADAM_WD, ADAM_STEP = 0.001, 0.9, 0.999, 1e-08, 0.01, 10


def _cparams(sem, vmem=VMEM_LIMIT):
    return pltpu.CompilerParams(dimension_semantics=sem, vmem_limit_bytes=vmem)


def _dims(nd, ta, tb):
    off = nd - 2
    ca = off + (0 if ta else 1)
    cb = off + (1 if tb else 0)
    batch = ((0,), (0,)) if nd == 3 else ((), ())
    return (((ca,), (cb,)), batch)


def _raw_mm(a, b, ta, tb, hi):
    if hi:
        return lax.dot_general(a, b, _dims(a.ndim, ta, tb), precision=HI, preferred_element_type=F32)
    return lax.dot_general(a.astype(BF16), b.astype(BF16), _dims(a.ndim, ta, tb), preferred_element_type=F32)


@functools.partial(jax.custom_vjp, nondiff_argnums=(2, 3, 4))
def _mm(a, b, ta=False, tb=False, hi=False):
    return _raw_mm(a, b, ta, tb, hi)


def _mm_fwd(a, b, ta, tb, hi):
    return _raw_mm(a, b, ta, tb, hi), (a, b)


def _mm_bwd(ta, tb, hi, res, g):
    a, b = res
    da = _raw_mm(g, b, False, not tb, hi) if not ta else _raw_mm(b, g, tb, True, hi)
    db = _raw_mm(a, g, not ta, False, hi) if not tb else _raw_mm(g, a, True, ta, hi)
    return da.astype(a.dtype), db.astype(b.dtype)


_mm.defvjp(_mm_fwd, _mm_bwd)


def _rms(x, g):
    return x * lax.rsqrt(jnp.mean(x * x, axis=-1, keepdims=True) + EPS) * g


def _gelu(x):
    return 0.5 * x * (1.0 + jnp.tanh(math.sqrt(2.0 / math.pi) * (x + 0.044715 * (x * x * x))))


def _sigmoid(x):
    return 1.0 / (1.0 + jnp.exp(-x))


def _silu(x):
    return x * _sigmoid(x)


def _softplus(x):
    return jnp.maximum(x, 0.0) + jnp.log(1.0 + jnp.exp(-jnp.abs(x)))


def _iota(shape, dim):
    return lax.broadcasted_iota(jnp.int32, shape, dim)


def _sgu_fn(u, v, sg, w, b):
    nc = T // A_CHUNK
    ug = _gelu(u)
    vn = _rms(_gelu(v), sg)
    causal = _iota((A_CHUNK, A_CHUNK), 0) >= _iota((A_CHUNK, A_CHUNK), 1)
    wm = jnp.where(causal, w, 0.0)
    wb = jnp.broadcast_to(wm[None], (nc, A_CHUNK, A_CHUNK))
    z = _mm(wb, vn.reshape(nc, A_CHUNK, HD)) + b[None]
    return ug * z.reshape(T, HD)


def _sgu_specs():
    col = lambda off: pl.BlockSpec((T, HD), lambda g, off=off: (0, off + g))
    par = [pl.BlockSpec((None, 1, HD), lambda g: (g, 0, 0)),
           pl.BlockSpec((None, A_CHUNK, A_CHUNK), lambda g: (g, 0, 0)),
           pl.BlockSpec((None, A_CHUNK, 1), lambda g: (g, 0, 0))]
    return col, par


def sgu_fwd(p2, sg, w, b):
    col, par = _sgu_specs()

    def body(u_ref, v_ref, sg_ref, w_ref, b_ref, y_ref):
        y_ref[...] = _sgu_fn(u_ref[...], v_ref[...], sg_ref[...], w_ref[...], b_ref[...]).astype(BF16)

    return pl.pallas_call(
        body, name="sgu_fwd", grid=(A_GROUPS,),
        in_specs=[col(P_AU // HD), col(P_AV // HD)] + par,
        out_specs=pl.BlockSpec((T, HD), lambda g: (0, g)),
        out_shape=jax.ShapeDtypeStruct((T, A_WIDTH), BF16),
        compiler_params=_cparams(("arbitrary",)),
    )(p2, p2, sg, w, b)


def sgu_bwd(p2, sg, w, b, dmix):
    col, par = _sgu_specs()

    def body(u_ref, v_ref, sg_ref, w_ref, b_ref, dy_ref, du_ref, dv_ref, dsg_ref, dw_ref, db_ref):
        _, vjp = jax.vjp(_sgu_fn, u_ref[...], v_ref[...], sg_ref[...], w_ref[...], b_ref[...])
        du, dv, dsg, dw, db = vjp(dy_ref[...])
        du_ref[...] = du.astype(BF16)
        dv_ref[...] = dv.astype(BF16)
        dsg_ref[...] = dsg
        dw_ref[...] = dw
        db_ref[...] = db

    gcol = pl.BlockSpec((T, HD), lambda g: (0, g))
    return pl.pallas_call(
        body, name="sgu_bwd", grid=(A_GROUPS,),
        in_specs=[col(P_AU // HD), col(P_AV // HD)] + par + [gcol],
        out_specs=[gcol, gcol] + par,
        out_shape=[jax.ShapeDtypeStruct((T, A_WIDTH), BF16), jax.ShapeDtypeStruct((T, A_WIDTH), BF16),
                   jax.ShapeDtypeStruct((A_GROUPS, 1, HD), F32), jax.ShapeDtypeStruct((A_GROUPS, A_CHUNK, A_CHUNK), F32),
                   jax.ShapeDtypeStruct((A_GROUPS, A_CHUNK, 1), F32)],
        compiler_params=_cparams(("arbitrary",)),
    )(p2, p2, sg, w, b, dmix)


def _attn_fn(q, k, v, qg, kg, slope, *, dil, nb):
    n = T // C_BLOCK
    qb = _rms(q, qg).reshape(n, C_BLOCK, HD)
    kb = _rms(k, kg).reshape(n, C_BLOCK, HD)
    vb = v.reshape(n, C_BLOCK, HD)
    scale = HD ** -0.5
    qi = _iota((n, C_BLOCK, C_BLOCK), 1)
    kj = _iota((n, C_BLOCK, C_BLOCK), 2)
    sl = slope[None] * float(dil)
    d_cur = qi - kj
    sc = jnp.where(d_cur >= 0, _mm(qb, kb, tb=True) * scale - sl * d_cur.astype(F32), -jnp.inf)
    mx = jnp.max(sc, axis=-1, keepdims=True)
    if nb > 1:
        kp = jnp.concatenate([jnp.zeros((1, C_BLOCK, HD), F32), kb[:-1]], axis=0)
        vp = jnp.concatenate([jnp.zeros((1, C_BLOCK, HD), F32), vb[:-1]], axis=0)
        has_prev = (_iota((n, C_BLOCK, C_BLOCK), 0) % nb) > 0
        d_prev = C_BLOCK + qi - kj
        sp = jnp.where((kj >= qi) & has_prev, _mm(qb, kp, tb=True) * scale - sl * d_prev.astype(F32), -jnp.inf)
        mx = jnp.maximum(mx, jnp.max(sp, axis=-1, keepdims=True))
    p = jnp.exp(sc - mx)
    den = jnp.sum(p, axis=-1, keepdims=True)
    if nb > 1:
        pp = jnp.exp(sp - mx)
        den = den + jnp.sum(pp, axis=-1, keepdims=True)
    out = _mm(p / den, vb)
    if nb > 1:
        out = out + _mm(pp / den, vp)
    lse = mx + jnp.log(den)
    return out.reshape(T, HD), jnp.broadcast_to(lse, (n, C_BLOCK, HD)).reshape(T, HD)


def _combine_fn(o1, o2, o3, l1, l2, l3):
    mx = jnp.maximum(jnp.maximum(l1, l2), l3)
    e1, e2, e3 = jnp.exp(l1 - mx), jnp.exp(l2 - mx), jnp.exp(l3 - mx)
    s = e1 + e2 + e3
    return (e1 / s) * o1 + (e2 / s) * o2 + (e3 / s) * o3


def _branch_blocks(dil):
    return -(-(T // dil) // C_BLOCK)


def _load_branch_order(ref, dil):
    if dil == 1:
        return ref[...]
    seg = T // dil
    return jnp.concatenate([ref[pl.ds(r, seg, stride=dil), :] for r in range(dil)], axis=0)


def _store_position_order(ref, val, dil, add=False):
    seg = T // dil
    for r in range(dil):
        rows = slice(None) if dil == 1 else pl.ds(r, seg, stride=dil)
        piece = val if dil == 1 else val[r * seg:(r + 1) * seg]
        if add:
            ref[rows, :] += piece
        else:
            ref[rows, :] = piece


def _dattn_specs():
    col = lambda off: pl.BlockSpec((T, HD), lambda h, off=off: (0, off // HD + h))
    row = pl.BlockSpec((1, HD), lambda h: (0, 0))
    slope = pl.BlockSpec((None, 1, HD), lambda h: (h, 0, 0))
    return [col(P_CQ), col(P_CK), col(P_CV), row, row, slope]


def _dattn_branches(q_ref, k_ref, v_ref, qg, kg, slope, o_scr, l_scr):
    for b, (_, dil) in enumerate(C_BRANCHES):
        q, k, v = (_load_branch_order(r, dil) for r in (q_ref, k_ref, v_ref))
        o, l = _attn_fn(q, k, v, qg, kg, slope, dil=dil, nb=_branch_blocks(dil))
        _store_position_order(o_scr.at[b], o, dil)
        _store_position_order(l_scr.at[b], l, dil)


def dattn_fwd(p2, qg, kg, slopes):
    def body(q_ref, k_ref, v_ref, qg_ref, kg_ref, s_ref, y_ref, o_scr, l_scr):
        _dattn_branches(q_ref, k_ref, v_ref, qg_ref[...], kg_ref[...], s_ref[...], o_scr, l_scr)
        y_ref[...] = _combine_fn(o_scr[0], o_scr[1], o_scr[2], l_scr[0], l_scr[1], l_scr[2]).astype(BF16)

    return pl.pallas_call(
        body, name="dattn_fwd", grid=(C_HEADS,), in_specs=_dattn_specs(), out_specs=pl.BlockSpec((T, HD), lambda h: (0, h)),
        out_shape=jax.ShapeDtypeStruct((T, C_WIDTH), BF16),
        scratch_shapes=[pltpu.VMEM((3, T, HD), F32), pltpu.VMEM((3, T, HD), F32)], compiler_params=_cparams(("arbitrary",)),
    )(p2, p2, p2, qg, kg, slopes)


def dattn_bwd(p2, qg, kg, slopes, dmix):
    hcol = pl.BlockSpec((T, HD), lambda h: (0, h))
    row = pl.BlockSpec((1, HD), lambda h: (0, 0))
    dy = pl.BlockSpec((T, HD), lambda h: (0, (A_WIDTH + B_WIDTH) // HD + h))

    def body(q_ref, k_ref, v_ref, qg_ref, kg_ref, s_ref, dy_ref, dq_ref, dk_ref, dv_ref, dqg_ref, dkg_ref, o_scr, l_scr, g_scr,
             acc):
        qg, kg, slope = qg_ref[...], kg_ref[...], s_ref[...]
        _dattn_branches(q_ref, k_ref, v_ref, qg, kg, slope, o_scr, l_scr)
        _, vjp = jax.vjp(_combine_fn, o_scr[0], o_scr[1], o_scr[2], l_scr[0], l_scr[1], l_scr[2])
        for i, g in enumerate(vjp(dy_ref[...])):
            g_scr[i] = g

        @pl.when(pl.program_id(0) == 0)
        def _():
            dqg_ref[...] = jnp.zeros_like(dqg_ref)
            dkg_ref[...] = jnp.zeros_like(dkg_ref)

        for b, (_, dil) in enumerate(C_BRANCHES):
            q, k, v = (_load_branch_order(r, dil) for r in (q_ref, k_ref, v_ref))
            do, dl = _load_branch_order(g_scr.at[b], dil), _load_branch_order(g_scr.at[3 + b], dil)
            fn = functools.partial(_attn_fn, dil=dil, nb=_branch_blocks(dil))
            _, vjp_b = jax.vjp(lambda a, b_, c, d, e, fn=fn: fn(a, b_, c, d, e, slope), q, k, v, qg, kg)
            dq, dk, dv, dqg, dkg = vjp_b((do, dl))
            for i, val in enumerate((dq, dk, dv)):
                _store_position_order(acc.at[i], val, dil, add=b > 0)
            dqg_ref[...] += dqg
            dkg_ref[...] += dkg
        for i, ref in enumerate((dq_ref, dk_ref, dv_ref)):
            ref[...] = acc[i].astype(BF16)

    scr = lambda n: pltpu.VMEM((n, T, HD), F32)
    return pl.pallas_call(
        body, name="dattn_bwd", grid=(C_HEADS,), in_specs=_dattn_specs() + [dy], out_specs=[hcol, hcol, hcol, row, row],
        out_shape=[jax.ShapeDtypeStruct((T, C_WIDTH), BF16)] * 3 + [jax.ShapeDtypeStruct((1, HD), F32)] * 2,
        scratch_shapes=[scr(3), scr(3), scr(6), scr(3)], compiler_params=_cparams(("arbitrary",)),
    )(p2, p2, p2, qg, kg, slopes, dmix)


_NCH = T // B_CHUNK


def _conv_taps(x, w_ref):
    rows = _iota(x.shape, 0)
    taps = []
    for j in range(B_CONV):
        s = B_CONV - 1 - j
        taps.append(x if s == 0 else jnp.where(rows >= s, pltpu.roll(x, s, 0), 0.0))
    pre = sum(w_ref[j:j + 1, :] * taps[j] for j in range(B_CONV))
    return pre, taps


def _conv_post(pre, mode):
    y = _silu(pre)
    if mode == "v":
        return y
    y = y * lax.rsqrt(jnp.sum(y * y, axis=-1, keepdims=True) + EPS)
    return y * (HD ** -0.5) if mode == "q" else y


def conv_fwd(p2, conv_w, mode):
    idx = "qkv".index(mode)
    xcol = pl.BlockSpec((T, HD), lambda h: (0, P_BQ // HD + B_HEADS * idx + h))
    wcol = pl.BlockSpec((B_CONV, HD), lambda h: (0, B_HEADS * idx + h))
    hcol = pl.BlockSpec((T, HD), lambda h: (0, h))

    def body(x_ref, w_ref, y_ref):
        pre, _ = _conv_taps(x_ref[...], w_ref)
        y_ref[...] = _conv_post(pre, mode)

    return pl.pallas_call(
        body, name=f"conv_fwd_{mode}", grid=(B_HEADS,), in_specs=[xcol, wcol], out_specs=hcol,
        out_shape=jax.ShapeDtypeStruct((T, B_WIDTH), F32), compiler_params=_cparams(("arbitrary",)),
    )(p2, conv_w)


def conv_bwd(p2, conv_w, dys, mode):
    idx = "qkv".index(mode)
    xcol = pl.BlockSpec((T, HD), lambda h: (0, P_BQ // HD + B_HEADS * idx + h))
    wcol = pl.BlockSpec((B_CONV, HD), lambda h: (0, B_HEADS * idx + h))
    hcol = pl.BlockSpec((T, HD), lambda h: (0, h))
    wout = pl.BlockSpec((B_CONV, HD), lambda h: (0, h))

    def body(x_ref, w_ref, *rest):
        dy_refs, (dx_ref, dw_ref) = rest[:-2], rest[-2:]
        pre, taps = _conv_taps(x_ref[...], w_ref)
        _, vjp = jax.vjp(functools.partial(_conv_post, mode=mode), pre)
        (dpre,) = vjp(sum(r[...] for r in dy_refs))
        rows = _iota(dpre.shape, 0)
        dx = w_ref[B_CONV - 1:B_CONV, :] * dpre
        for j in range(B_CONV):
            s = B_CONV - 1 - j
            dw_ref[j:j + 1, :] = jnp.sum(dpre * taps[j], axis=0, keepdims=True)
            if s > 0:
                dx = dx + w_ref[j:j + 1, :] * jnp.where(rows < T - s, pltpu.roll(dpre, T - s, 0), 0.0)
        dx_ref[...] = dx.astype(BF16)

    return pl.pallas_call(
        body, name=f"conv_bwd_{mode}", grid=(B_HEADS,), in_specs=[xcol, wcol] + [hcol] * len(dys), out_specs=[hcol, wout],
        out_shape=[jax.ShapeDtypeStruct((T, B_WIDTH), BF16), jax.ShapeDtypeStruct((B_CONV, B_WIDTH), F32)],
        compiler_params=_cparams(("arbitrary",)),
    )(p2, conv_w, *dys)


def _gates_fn(bg, al, dtb, h):
    r = _iota((HD, HD), 0)
    logit = _mm(bg, (r == h).astype(F32), hi=True)
    a = _mm(bg, (r == h + B_HEADS).astype(F32), hi=True)
    beta = _sigmoid(logit)
    graw = -jnp.exp(al) * _softplus(a + dtb)
    tri = (_iota((_NCH, B_CHUNK, B_CHUNK), 1) >= _iota((_NCH, B_CHUNK, B_CHUNK), 2)).astype(F32)
    g = _mm(tri, graw.reshape(_NCH, B_CHUNK, HD), hi=True).reshape(T, HD)
    return beta, g


def _gates_specs():
    bg = pl.BlockSpec((T, HD), lambda h: (0, P_BB // HD))
    par = pl.BlockSpec((None, 1, HD), lambda h: (h, 0, 0))
    out = pl.BlockSpec((None, T, HD), lambda h: (h, 0, 0))
    return bg, par, out


def gates_fwd(p2, al, dtb):
    bg, par, out = _gates_specs()

    def body(bg_ref, al_ref, dtb_ref, beta_ref, g_ref):
        beta, g = _gates_fn(bg_ref[...], al_ref[...], dtb_ref[...], pl.program_id(0))
        beta_ref[...] = beta
        g_ref[...] = g

    return pl.pallas_call(
        body, name="gates_fwd", grid=(B_HEADS,), in_specs=[bg, par, par], out_specs=[out, out],
        out_shape=[jax.ShapeDtypeStruct((B_HEADS, T, HD), F32)] * 2, compiler_params=_cparams(("arbitrary",)),
    )(p2, al, dtb)


def gates_bwd(p2, al, dtb, dbeta, dg1, dg2):
    bg, par, out = _gates_specs()
    acc = pl.BlockSpec((T, HD), lambda h: (0, 0))

    def body(bg_ref, al_ref, dtb_ref, dbeta_ref, dg1_ref, dg2_ref, dbg_ref, dal_ref, ddtb_ref, acc_ref):
        h = pl.program_id(0)
        _, vjp = jax.vjp(lambda a, b, c: _gates_fn(a, b, c, h), bg_ref[...], al_ref[...], dtb_ref[...])
        dbg, dal, ddtb = vjp((dbeta_ref[...], dg1_ref[...] + dg2_ref[...]))

        @pl.when(h == 0)
        def _():
            acc_ref[...] = jnp.zeros_like(acc_ref)

        acc_ref[...] += dbg
        dbg_ref[...] = acc_ref[...].astype(BF16)
        dal_ref[...] = jnp.broadcast_to(jnp.sum(dal, axis=-1, keepdims=True), (1, HD))
        ddtb_ref[...] = jnp.broadcast_to(jnp.sum(ddtb, axis=-1, keepdims=True), (1, HD))

    return pl.pallas_call(
        body, name="gates_bwd", grid=(B_HEADS,), in_specs=[bg, par, par, out, out, out], out_specs=[acc, par, par],
        out_shape=[jax.ShapeDtypeStruct((T, HD), BF16)] + [jax.ShapeDtypeStruct((B_HEADS, 1, HD), F32)] * 2,
        scratch_shapes=[pltpu.VMEM((T, HD), F32)], compiler_params=_cparams(("arbitrary",)),
    )(p2, al, dtb, dbeta, dg1, dg2)


def _unit_lower_inverse(a):
    eye = (_iota(a.shape, 1) == _iota(a.shape, 2)).astype(F32)
    x = eye - a
    p = _mm(a, a, hi=True)
    for i in range(5):
        x = x + _mm(x, p, hi=True)
        if i < 4:
            p = _mm(p, p, hi=True)
    return x


_WY_CH = 8
_WY_ROWS = _WY_CH * B_CHUNK


def _wy_fn(q, k, v, beta, g):
    sh = (q.shape[0] // B_CHUNK, B_CHUNK, HD)
    q3, k3, v3, b3, g3 = (t.reshape(sh) for t in (q, k, v, beta, g))
    gd = g3[:, :, :B_CHUNK] - jnp.swapaxes(g3, 1, 2)[:, :B_CHUNK, :]
    ii, jj = _iota(gd.shape, 1), _iota(gd.shape, 2)
    decay = jnp.exp(jnp.where(ii >= jj, gd, -jnp.inf))
    kb = k3 * b3
    a = _mm(kb, k3, tb=True) * jnp.where(ii > jj, decay, 0.0)
    tinv = _unit_lower_inverse(a)
    u = _mm(tinv, v3 * b3, hi=True)
    w = _mm(tinv, kb * jnp.exp(g3), hi=True)
    attn = _mm(q3, k3, tb=True) * decay
    return u.reshape(q.shape), w.reshape(q.shape), attn


def _wy_specs():
    hcol = pl.BlockSpec((_WY_ROWS, HD), lambda h, i: (i, h))
    hb = pl.BlockSpec((None, _WY_ROWS, HD), lambda h, i: (h, i, 0))
    at = pl.BlockSpec((None, _WY_CH, B_CHUNK, B_CHUNK), lambda h, i: (h, i, 0, 0))
    return hcol, hb, at


_WY_GRID = (B_HEADS, _NCH // _WY_CH)


def wy_fwd(q, k, v, beta, g):
    hcol, hb, at = _wy_specs()

    def body(q_ref, k_ref, v_ref, b_ref, g_ref, u_ref, w_ref, a_ref):
        u, w, a = _wy_fn(q_ref[...], k_ref[...], v_ref[...], b_ref[...], g_ref[...])
        u_ref[...] = u
        w_ref[...] = w
        a_ref[...] = a

    return pl.pallas_call(
        body, name="wy_fwd", grid=_WY_GRID, in_specs=[hcol, hcol, hcol, hb, hb], out_specs=[hcol, hcol, at],
        out_shape=[jax.ShapeDtypeStruct((T, B_WIDTH), F32)] * 2 + [jax.ShapeDtypeStruct((B_HEADS, _NCH, B_CHUNK, B_CHUNK), F32)],
        compiler_params=_cparams(("arbitrary", "arbitrary")),
    )(q, k, v, beta, g)


def wy_bwd(q, k, v, beta, g, du, dw, dattn):
    hcol, hb, at = _wy_specs()

    def body(q_ref, k_ref, v_ref, b_ref, g_ref, du_ref, dw_ref, da_ref, dq_ref, dk_ref, dv_ref, db_ref, dg_ref):
        _, vjp = jax.vjp(_wy_fn, q_ref[...], k_ref[...], v_ref[...], b_ref[...], g_ref[...])
        for r, t in zip((dq_ref, dk_ref, dv_ref, db_ref, dg_ref), vjp((du_ref[...], dw_ref[...], da_ref[...]))):
            r[...] = t

    return pl.pallas_call(
        body, name="wy_bwd", grid=_WY_GRID, in_specs=[hcol, hcol, hcol, hb, hb, hcol, hcol, at],
        out_specs=[hcol, hcol, hcol, hb, hb],
        out_shape=[jax.ShapeDtypeStruct((T, B_WIDTH), F32)] * 3 + [jax.ShapeDtypeStruct((B_HEADS, T, HD), F32)] * 2,
        compiler_params=_cparams(("arbitrary", "arbitrary")),
    )(q, k, v, beta, g, du, dw, dattn)


def _scan_step_fn(q, k, u, w, g, attn, gate, og, s):
    v_new = u - _mm(w, s)
    o = _mm(q * jnp.exp(g), s) + _mm(attn, v_new)
    g_last = jnp.sum(jnp.where(_iota(g.shape, 0) == B_CHUNK - 1, g, 0.0), axis=0, keepdims=True)
    s_new = s * jnp.exp(g_last) + _mm(k * jnp.exp(g_last - g), v_new, ta=True)
    return _rms(o, og) * _silu(gate), s_new


def _scan_specs(rev):
    ch = (lambda n: _NCH - 1 - n) if rev else (lambda n: n)
    rows = pl.BlockSpec((B_CHUNK, B_WIDTH), lambda n: (ch(n), 0))
    gb = pl.BlockSpec((B_HEADS, B_CHUNK, HD), lambda n: (0, ch(n), 0))
    at = pl.BlockSpec((B_HEADS, None, B_CHUNK, B_CHUNK), lambda n: (0, ch(n), 0, 0))
    og = pl.BlockSpec((1, HD), lambda n: (0, 0))
    st = pl.BlockSpec((None, B_HEADS, HD, HD), lambda n: (ch(n), 0, 0, 0))
    return rows, gb, at, og, st


def scan_fwd(q, k, u, w, g, attn, gate, og):
    rows, gb, at, ogs, st = _scan_specs(False)

    def body(q_ref, k_ref, u_ref, w_ref, g_ref, a_ref, gate_ref, og_ref, y_ref, st_ref, s_ref):
        @pl.when(pl.program_id(0) == 0)
        def _():
            s_ref[...] = jnp.zeros_like(s_ref)

        for h in range(B_HEADS):
            c = slice(h * HD, (h + 1) * HD)
            s = s_ref[h]
            st_ref[h] = s
            y, s_new = _scan_step_fn(q_ref[:, c], k_ref[:, c], u_ref[:, c], w_ref[:, c], g_ref[h], a_ref[h],
                                     gate_ref[:, c], og_ref[...], s)
            y_ref[:, c] = y.astype(BF16)
            s_ref[h] = s_new

    return pl.pallas_call(
        body, name="scan_fwd", grid=(_NCH,), in_specs=[rows, rows, rows, rows, gb, at, rows, ogs], out_specs=[rows, st],
        out_shape=[jax.ShapeDtypeStruct((T, B_WIDTH), BF16), jax.ShapeDtypeStruct((_NCH, B_HEADS, HD, HD), F32)],
        scratch_shapes=[pltpu.VMEM((B_HEADS, HD, HD), F32)], compiler_params=_cparams(("arbitrary",)),
    )(q, k, u, w, g, attn, gate, og)


def scan_bwd(q, k, u, w, g, attn, gate, og, states, dmix):
    rows, gb, at, ogs, st = _scan_specs(True)
    dyb = pl.BlockSpec((B_CHUNK, HD), lambda n: (_NCH - 1 - n, 0))

    def body(q_ref, k_ref, u_ref, w_ref, g_ref, a_ref, gate_ref, og_ref, st_ref, *rest):
        dy_refs, (dq_ref, dk_ref, du_ref, dw_ref, dgate_ref, dg_ref, da_ref, dog_ref, ds_ref) = rest[:B_HEADS], rest[B_HEADS:]

        @pl.when(pl.program_id(0) == 0)
        def _():
            ds_ref[...] = jnp.zeros_like(ds_ref)
            dog_ref[...] = jnp.zeros_like(dog_ref)

        for h in range(B_HEADS):
            c = slice(h * HD, (h + 1) * HD)
            _, vjp = jax.vjp(_scan_step_fn, q_ref[:, c], k_ref[:, c], u_ref[:, c], w_ref[:, c], g_ref[h], a_ref[h],
                             gate_ref[:, c], og_ref[...], st_ref[h])
            dq, dk, du, dw, dg, da, dgate, dog, ds = vjp((dy_refs[h][...], ds_ref[h]))
            dq_ref[:, c] = dq
            dk_ref[:, c] = dk
            du_ref[:, c] = du
            dw_ref[:, c] = dw
            dgate_ref[:, c] = dgate.astype(BF16)
            dg_ref[h] = dg
            da_ref[h] = da
            dog_ref[...] += dog
            ds_ref[h] = ds

    dy_specs = [pl.BlockSpec((B_CHUNK, HD), lambda n, h=h: (_NCH - 1 - n, A_WIDTH // HD + h)) for h in range(B_HEADS)]
    return pl.pallas_call(
        body, name="scan_bwd", grid=(_NCH,),
        in_specs=[rows, rows, rows, rows, gb, at, rows, ogs, st] + dy_specs,
        out_specs=[rows] * 5 + [gb, at, ogs],
        out_shape=[jax.ShapeDtypeStruct((T, B_WIDTH), F32)] * 4 + [jax.ShapeDtypeStruct((T, B_WIDTH), BF16)]
        + [jax.ShapeDtypeStruct((B_HEADS, T, HD), F32), jax.ShapeDtypeStruct((B_HEADS, _NCH, B_CHUNK, B_CHUNK), F32),
           jax.ShapeDtypeStruct((1, HD), F32)],
        scratch_shapes=[pltpu.VMEM((B_HEADS, HD, HD), F32)], compiler_params=_cparams(("arbitrary",)),
    )(q, k, u, w, g, attn, gate, og, states, *([dmix] * B_HEADS))


def _lanes(vec):
    return jnp.broadcast_to(vec[:, None, None], (vec.shape[0], 1, HD))


def gdn_forward(p2, conv_w, a_log, dt_bias, og):
    qa, ka, va = (conv_fwd(p2, conv_w, m) for m in "qkv")
    beta, g = gates_fwd(p2, _lanes(a_log), _lanes(dt_bias))
    u, w, attn = wy_fwd(qa, ka, va, beta, g)
    gate = p2[:, P_BG:P_BB]
    y, states = scan_fwd(qa, ka, u, w, g, attn, gate, og)
    return y, (qa, ka, va, beta, g, u, w, attn, gate, states)


def gdn_backward(p2, conv_w, a_log, dt_bias, og, saved, dmix):
    qa, ka, va, beta, g, u, w, attn, gate, states = saved
    dq1, dk1, du, dw, dgate, dg1, dattn, dog = scan_bwd(qa, ka, u, w, g, attn, gate, og, states, dmix)
    dq2, dk2, dv, dbeta, dg2 = wy_bwd(qa, ka, va, beta, g, du, dw, dattn)
    dbg, dal, ddtb = gates_bwd(p2, _lanes(a_log), _lanes(dt_bias), dbeta, dg1, dg2)
    dxq, dwq = conv_bwd(p2, conv_w, [dq1, dq2], "q")
    dxk, dwk = conv_bwd(p2, conv_w, [dk1, dk2], "k")
    dxv, dwv = conv_bwd(p2, conv_w, [dv], "v")
    return [dxq, dxk, dxv, dgate, dbg], jnp.concatenate([dwq, dwk, dwv], axis=1), dal[:, 0, 0], ddtb[:, 0, 0], dog


_SLOPES = np.exp2(-8.0 * (np.arange(C_HEADS, dtype=np.float64) + 1.0) / C_HEADS).astype(np.float32)


def _alibi_slopes():
    return _lanes(jnp.asarray(_SLOPES))


_ROWS = 256
_TM = 512


def _dep_specs(dep, ngrid):
    if dep is None:
        return [], []
    return [dep], [pl.BlockSpec((8, HD), lambda *_: (0, 0))]


def rmsnorm_fwd(x, g, dep=None):
    blk = pl.BlockSpec((_ROWS, D), lambda i: (i, 0))
    deps, dspecs = _dep_specs(dep, 1)

    def body(x_ref, g_ref, *rest):
        rest[-1][...] = _rms(x_ref[...], g_ref[...]).astype(BF16)

    return pl.pallas_call(
        body, name="rmsnorm_fwd", grid=(T // _ROWS,), in_specs=[blk, pl.BlockSpec((1, D), lambda i: (0, 0))] + dspecs,
        out_specs=blk, out_shape=jax.ShapeDtypeStruct((T, D), BF16), compiler_params=_cparams(("arbitrary",)),
    )(x, g, *deps)


def rmsnorm_bwd(x, g, dh, dres):
    blk = pl.BlockSpec((_ROWS, D), lambda i: (i, 0))
    row = pl.BlockSpec((1, D), lambda i: (0, 0))

    def body(x_ref, g_ref, dh_ref, dres_ref, dx_ref, dg_ref):
        _, vjp = jax.vjp(_rms, x_ref[...], g_ref[...])
        dx, dg = vjp(dh_ref[...])
        dx_ref[...] = dres_ref[...] + dx

        @pl.when(pl.program_id(0) == 0)
        def _():
            dg_ref[...] = jnp.zeros_like(dg_ref)

        dg_ref[...] += dg

    return pl.pallas_call(
        body, name="rmsnorm_bwd", grid=(T // _ROWS,), in_specs=[blk, row, blk, blk], out_specs=[blk, row],
        out_shape=[jax.ShapeDtypeStruct((T, D), F32), jax.ShapeDtypeStruct((1, D), F32)],
        compiler_params=_cparams(("arbitrary",)),
    )(x, g, dh, dres)


def _matmul(name, a, b, *, grid, a_spec, b_spec, o_spec, out_shape, ta=False, tb=False, k_axis=None, res=None, dep=None):
    dims = _dims(2, ta, tb)
    deps, dspecs = _dep_specs(dep, len(grid))

    def body(a_ref, b_ref, *rest):
        o_ref = rest[-1]
        prod = lax.dot_general(a_ref[...].astype(BF16), b_ref[...].astype(BF16), dims, preferred_element_type=F32)
        if res is not None:
            prod = prod + rest[0][...]
        if k_axis is None:
            o_ref[...] = prod.astype(o_ref.dtype)
        else:
            @pl.when(pl.program_id(k_axis) == 0)
            def _():
                o_ref[...] = prod

            @pl.when(pl.program_id(k_axis) > 0)
            def _():
                o_ref[...] += prod

    sem = tuple("arbitrary" for _ in grid)
    ins = [a, b] + ([res] if res is not None else []) + deps
    specs = [a_spec, b_spec] + ([o_spec] if res is not None else []) + dspecs
    return pl.pallas_call(
        body, name=name, grid=grid, in_specs=specs, out_specs=o_spec, out_shape=out_shape, compiler_params=_cparams(sem),
    )(*ins)


_IN_TN = P_END // 3


def mm_proj(h1, wp_in, l):
    return _matmul(
        "mm_proj", h1, wp_in, grid=(P_END // _IN_TN, T // _TM),
        a_spec=pl.BlockSpec((_TM, D), lambda j, i: (i, 0)),
        b_spec=pl.BlockSpec((None, D, _IN_TN), lambda j, i: (l, 0, j)),
        o_spec=pl.BlockSpec((_TM, _IN_TN), lambda j, i: (i, j)), out_shape=jax.ShapeDtypeStruct((T, P_END), F32))


def mm_dh1(dp2, wp_in, l):
    return _matmul(
        "mm_dh1", dp2, wp_in, grid=(T // _TM, P_END // _IN_TN), tb=True, k_axis=1,
        a_spec=pl.BlockSpec((_TM, _IN_TN), lambda i, k: (i, k)),
        b_spec=pl.BlockSpec((None, D, _IN_TN), lambda i, k: (l, 0, k)),
        o_spec=pl.BlockSpec((_TM, D), lambda i, k: (i, 0)), out_shape=jax.ShapeDtypeStruct((T, D), F32))


def mm_dwin(h1, dp2):
    return _matmul(
        "mm_dwin", h1, dp2, grid=(P_END // _IN_TN, D // _TM), ta=True,
        a_spec=pl.BlockSpec((T, _TM), lambda j, i: (0, i)),
        b_spec=pl.BlockSpec((T, _IN_TN), lambda j, i: (0, j)),
        o_spec=pl.BlockSpec((_TM, _IN_TN), lambda j, i: (i, j)), out_shape=jax.ShapeDtypeStruct((D, P_END), BF16))


def _mm_square(name, a, w, l, res, tb, dep=None):
    tn = 1024
    b_spec = (pl.BlockSpec((None, tn, D), lambda j, i: (l, j, 0)) if tb else pl.BlockSpec((None, D, tn), lambda j, i: (l, 0, j)))
    return _matmul(
        name, a, w, grid=(D // tn, T // _TM), tb=tb, res=res, dep=dep,
        a_spec=pl.BlockSpec((_TM, D), lambda j, i: (i, 0)), b_spec=b_spec,
        o_spec=pl.BlockSpec((_TM, tn), lambda j, i: (i, j)), out_shape=jax.ShapeDtypeStruct((T, D), F32))


def mm_out(mix, wg_out, l, x):
    return _mm_square("mm_out", mix, wg_out, l, x, False)


def mm_dmix(dx1, wg_out, l, dep=None):
    return _mm_square("mm_dmix", dx1, wg_out, l, None, True, dep)


def mm_dwout(mix, dx1):
    tn = 1024
    return _matmul(
        "mm_dwout", mix, dx1, grid=(D // tn, D // _TM), ta=True,
        a_spec=pl.BlockSpec((T, _TM), lambda j, i: (0, i)), b_spec=pl.BlockSpec((T, tn), lambda j, i: (0, j)),
        o_spec=pl.BlockSpec((_TM, tn), lambda j, i: (i, j)), out_shape=jax.ShapeDtypeStruct((D, D), BF16))


_GU_TN = GU_SHARD // 2


def mm_gu(h2, wg_gu, l):
    return _matmul(
        "mm_gu", h2, wg_gu, grid=(N_CHIPS, 2, T // _TM),
        a_spec=pl.BlockSpec((_TM, D), lambda s, j, i: (i, 0)),
        b_spec=pl.BlockSpec((None, None, D, _GU_TN), lambda s, j, i: (l, s, 0, j)),
        o_spec=pl.BlockSpec((_TM, _GU_TN), lambda s, j, i: (i, 2 * s + j)),
        out_shape=jax.ShapeDtypeStruct((T, 2 * FFN), F32))


_GU_NJ = FFN // _GU_TN


def mm_dh2(dgu, wg_gu, l):
    return _matmul(
        "mm_dh2", dgu, wg_gu, grid=(T // _TM, 2 * N_CHIPS), tb=True, k_axis=1,
        a_spec=pl.BlockSpec((None, _TM, _GU_TN), lambda i, k: (k // _GU_NJ, i, k % _GU_NJ)),
        b_spec=pl.BlockSpec((None, None, D, _GU_TN), lambda i, k: (l, k // 2, 0, k % 2)),
        o_spec=pl.BlockSpec((_TM, D), lambda i, k: (i, 0)), out_shape=jax.ShapeDtypeStruct((T, D), F32))


def mm_dwgu(h2, dgu):
    return _matmul(
        "mm_dwgu", h2, dgu, grid=(N_CHIPS, 2, D // _TM), ta=True,
        a_spec=pl.BlockSpec((T, _TM), lambda s, j, i: (0, i)),
        b_spec=pl.BlockSpec((None, T, _GU_TN), lambda s, j, i: ((2 * s + j) // _GU_NJ, 0, (2 * s + j) % _GU_NJ)),
        o_spec=pl.BlockSpec((None, _TM, _GU_TN), lambda s, j, i: (s, i, j)),
        out_shape=jax.ShapeDtypeStruct((N_CHIPS, D, GU_SHARD), BF16))


def mm_down(act, wg_down, l, x1):
    tn = 512
    return _matmul(
        "mm_down", act, wg_down, grid=(D // tn, T // _TM), res=x1,
        a_spec=pl.BlockSpec((_TM, FFN), lambda j, i: (i, 0)),
        b_spec=pl.BlockSpec((None, FFN, tn), lambda j, i: (l, 0, j)),
        o_spec=pl.BlockSpec((_TM, tn), lambda j, i: (i, j)), out_shape=jax.ShapeDtypeStruct((T, D), F32))


def mm_dact(dx2, wg_down, l, dep=None):
    tn = DOWN_SHARD
    return _matmul(
        "mm_dact", dx2, wg_down, grid=(FFN // tn, T // _TM), tb=True, dep=dep,
        a_spec=pl.BlockSpec((_TM, D), lambda j, i: (i, 0)),
        b_spec=pl.BlockSpec((None, tn, D), lambda j, i: (l, j, 0)),
        o_spec=pl.BlockSpec((_TM, tn), lambda j, i: (i, j)), out_shape=jax.ShapeDtypeStruct((T, FFN), F32))


def mm_dwdown(act, dx2):
    tm, tn = DOWN_SHARD, 512
    return _matmul(
        "mm_dwdown", act, dx2, grid=(D // tn, FFN // tm), ta=True,
        a_spec=pl.BlockSpec((T, tm), lambda j, i: (0, i)), b_spec=pl.BlockSpec((T, tn), lambda j, i: (0, j)),
        o_spec=pl.BlockSpec((tm, tn), lambda j, i: (i, j)), out_shape=jax.ShapeDtypeStruct((FFN, D), BF16))


_FF_TN = 1408


def _swiglu_fn(gt, up):
    return _silu(gt) * up


def swiglu_fwd(gu):
    nj = FFN // _FF_TN
    gt = pl.BlockSpec((_ROWS, _FF_TN), lambda i, j: (i, j))
    up = pl.BlockSpec((_ROWS, _FF_TN), lambda i, j: (i, nj + j))

    def body(gt_ref, up_ref, o_ref):
        o_ref[...] = _swiglu_fn(gt_ref[...], up_ref[...]).astype(BF16)

    return pl.pallas_call(
        body, name="swiglu_fwd", grid=(T // _ROWS, nj), in_specs=[gt, up], out_specs=gt,
        out_shape=jax.ShapeDtypeStruct((T, FFN), BF16), compiler_params=_cparams(("arbitrary", "arbitrary")),
    )(gu, gu)


def swiglu_bwd(gu, dact):
    nj = FFN // _FF_TN
    gt = pl.BlockSpec((_ROWS, _FF_TN), lambda i, j: (i, j))
    up = pl.BlockSpec((_ROWS, _FF_TN), lambda i, j: (i, nj + j))

    def body(gt_ref, up_ref, d_ref, dgu_ref):
        _, vjp = jax.vjp(_swiglu_fn, gt_ref[...], up_ref[...])
        dgt, dup = vjp(d_ref[...])
        dgu_ref[0] = dgt.astype(BF16)
        dgu_ref[1] = dup.astype(BF16)

    return pl.pallas_call(
        body, name="swiglu_bwd", grid=(T // _ROWS, nj), in_specs=[gt, up, gt],
        out_specs=pl.BlockSpec((2, _ROWS, _FF_TN), lambda i, j: (0, i, j)),
        out_shape=jax.ShapeDtypeStruct((2, T, FFN), BF16), compiler_params=_cparams(("arbitrary", "arbitrary")),
    )(gu, gu, dact)


def loss_and_grad(y, target):
    blk = pl.BlockSpec((_ROWS, D), lambda i: (i, 0))
    acc = pl.BlockSpec((8, HD), lambda i: (0, 0))

    def body(y_ref, t_ref, dy_ref, l_ref):
        err = y_ref[...] - t_ref[...]
        dy_ref[...] = err * (1.0 / D)

        @pl.when(pl.program_id(0) == 0)
        def _():
            l_ref[...] = jnp.zeros_like(l_ref)

        l_ref[...] += (0.5 / D) * jnp.sum(err * err)

    return pl.pallas_call(
        body, name="loss_and_grad", grid=(T // _ROWS,), in_specs=[blk, blk], out_specs=[blk, acc],
        out_shape=[jax.ShapeDtypeStruct((T, D), F32), jax.ShapeDtypeStruct((8, HD), F32)],
        compiler_params=_cparams(("arbitrary",)),
    )(y, target)


def adamw(w, g, m, v, name):
    rows, cols = w.shape
    tr = _ROWS if rows % _ROWS == 0 else rows
    blk = pl.BlockSpec((tr, cols), lambda i: (i, 0))

    def body(w_ref, g_ref, m_ref, v_ref, d_ref, nm_ref, nv_ref):
        gg = g_ref[...]
        nm = ADAM_B1 * m_ref[...] + (1.0 - ADAM_B1) * gg
        nv = ADAM_B2 * v_ref[...] + (1.0 - ADAM_B2) * (gg * gg)
        m_hat = nm / (1.0 - ADAM_B1 ** ADAM_STEP)
        v_hat = nv / (1.0 - ADAM_B2 ** ADAM_STEP)
        d_ref[...] = -ADAM_LR * (m_hat / (jnp.sqrt(v_hat) + ADAM_EPS) + ADAM_WD * w_ref[...])
        nm_ref[...] = nm
        nv_ref[...] = nv

    return pl.pallas_call(
        body, name=name, grid=(rows // tr,), in_specs=[blk] * 4, out_specs=[blk] * 3,
        out_shape=[jax.ShapeDtypeStruct(w.shape, F32)] * 3, compiler_params=_cparams(("arbitrary",)),
    )(w, g, m, v)


def shards_to_segments(w):
    valid = jnp.concatenate([w[..., s, :, :IN_SHARD] for s in range(N_CHIPS)], axis=-1)
    pad = jnp.zeros(valid.shape[:-1] + (P_CQ - GATE_COLS,), w.dtype)
    return jnp.concatenate([valid[..., :GATE_COLS], pad, valid[..., GATE_COLS:]], axis=-1)


def segments_to_shards(w):
    valid = jnp.concatenate([w[:, :GATE_COLS], w[:, P_CQ:]], axis=1)
    pad = ((0, 0), (0, IN_SHARD_PAD - IN_SHARD))
    return jnp.stack([jnp.pad(valid[:, s * IN_SHARD:(s + 1) * IN_SHARD], pad) for s in range(N_CHIPS)])


def mixers_forward(x, w_in, sp, dep=None):
    h1 = rmsnorm_fwd(x, sp["norm1_g"], dep)
    p2 = mm_proj(h1, w_in, 0)
    y_a = sgu_fwd(p2, sp["sgu_norm_g"], sp["w_spatial"], sp["b_spatial"])
    y_b, saved_b = gdn_forward(p2, sp["conv_w"], sp["a_log"], sp["dt_bias"], sp["o_norm_g"])
    y_c = dattn_fwd(p2, sp["q_norm_g"], sp["k_norm_g"], _alibi_slopes())
    mix = jnp.concatenate([y_a, y_b, y_c], axis=1)
    return mix, (x, h1, p2, saved_b, mix)


def ffn_forward(x, mix, wg, sp):
    x1 = mm_out(mix, wg["out"], 0, x)
    h2 = rmsnorm_fwd(x1, sp["norm2_g"])
    gu = mm_gu(h2, wg["gu"], 0)
    act = swiglu_fwd(gu)
    x2 = mm_down(act, wg["down"], 0, x1)
    return x2, (x1, h2, gu, act)


def ffn_backward(dx2, wg, sp, saved, dep=None):
    x1, h2, gu, act = saved
    dact = mm_dact(dx2, wg["down"], 0, dep)
    dw_down = mm_dwdown(act, dx2)
    dgu = swiglu_bwd(gu, dact)
    dw_gu = mm_dwgu(h2, dgu)
    dh2 = mm_dh2(dgu, wg["gu"], 0)
    dx1, dnorm2 = rmsnorm_bwd(x1, sp["norm2_g"], dh2, dx2)
    return dx1, dnorm2, dw_gu, dw_down


def mixers_backward(dx1, wg, sp, saved, dep=None):
    x, h1, p2, saved_b, mix = saved
    dmix = mm_dmix(dx1, wg["out"], 0, dep)
    dw_out = mm_dwout(mix, dx1)
    du, dv, dsg, dws, dbs = sgu_bwd(p2, sp["sgu_norm_g"], sp["w_spatial"], sp["b_spatial"], dmix)
    dseg_b, dconv, dal, ddtb, dog = gdn_backward(p2, sp["conv_w"], sp["a_log"], sp["dt_bias"], sp["o_norm_g"], saved_b, dmix)
    dcq, dck, dcv, dqg, dkg = dattn_bwd(p2, sp["q_norm_g"], sp["k_norm_g"], _alibi_slopes(), dmix)
    dp2 = jnp.concatenate([du, dv] + dseg_b + [dcq, dck, dcv], axis=1)
    dw_in = segments_to_shards(mm_dwin(h1, dp2))
    dh1 = mm_dh1(dp2, wg["in"], 0)
    dx, dnorm1 = rmsnorm_bwd(x, sp["norm1_g"], dh1, dx1)
    small = {"norm1_g": dnorm1, "sgu_norm_g": dsg, "w_spatial": dws, "b_spatial": dbs, "conv_w": dconv, "a_log": dal,
             "dt_bias": ddtb, "o_norm_g": dog, "q_norm_g": dqg, "k_norm_g": dkg}
    return dx, dw_in, dw_out, small


def layer_forward(x, wg, sp, dep=None):
    mix, saved_m = mixers_forward(x, wg["in"], sp, dep)
    x2, saved_f = ffn_forward(x, mix, wg, sp)
    return x2, (saved_m, saved_f)


def layer_backward(dx2, wg, sp, saved, dep=None):
    dx1, dnorm2, dw_gu, dw_down = ffn_backward(dx2, wg, sp, saved[1], dep)
    dx, dw_in, dw_out, small = mixers_backward(dx1, wg, sp, saved[0])
    return dx, {"in": dw_in, "out": dw_out, "gu": dw_gu, "down": dw_down}, {**small, "norm2_g": dnorm2}


_HBM = pl.BlockSpec(memory_space=pltpu.HBM)
_MESH = pl.DeviceIdType.MESH


def _place():
    x, y, c = lax.axis_index("x"), lax.axis_index("y"), lax.axis_index("c")
    chips = [(1 - x, y), (x, 1 - y), (1 - x, 1 - y)]
    return x, y, c, chips


def _rcopy(src, dst, ssem, rsem, dev):
    return pltpu.make_async_remote_copy(src_ref=src, dst_ref=dst, send_sem=ssem, recv_sem=rsem, device_id=dev,
                                        device_id_type=_MESH)


def _xor(a, b):
    return a + b - 2 * a * b


def gather_weights(shards):
    n = len(shards)

    def body(*refs):
        ins, outs = refs[:n], refs[n:2 * n]
        s_ici, r_ici, s_d2d, r_d2d, s_own, r_own = refs[2 * n:]
        x, y, c, _ = _place()
        s = 2 * x + y
        sibling = (x, y, 1 - c)
        nbr = [(1 - x, y), (x, 1 - y)]
        src_chip = (_xor(x, 1 - c), _xor(y, c))
        dst_chip = (_xor(x, c), _xor(y, 1 - c))
        t_src = 2 * src_chip[0] + src_chip[1]
        t_oth = 2 * dst_chip[0] + dst_chip[1]
        t_dia = 2 * (1 - x) + (1 - y)
        sends = []
        for a in range(n):
            for u in range(2):
                sends.append(_rcopy(ins[a].at[u], outs[a].at[s, u], s_own.at[a, u], r_own.at[a, u], sibling))
            for k in range(2):
                sends.append(_rcopy(ins[a].at[c], outs[a].at[s, c], s_ici.at[a, k], r_ici.at[a, k], (*nbr[k], c)))
        for cp in sends:
            cp.start()

        def landed(a, t, k):
            _rcopy(ins[a].at[c], outs[a].at[t, c], s_ici.at[a, k], r_ici.at[a, k], sibling).wait_recv()
            cp = _rcopy(outs[a].at[t, c], outs[a].at[t, c], s_d2d.at[a, k], r_d2d.at[a, k], sibling)
            cp.start()
            sends.append(cp)

        for a in range(n):
            landed(a, t_src, c)
            fwd = _rcopy(outs[a].at[t_src, c], outs[a].at[t_src, c], s_ici.at[a, 2], r_ici.at[a, 2], (*dst_chip, c))
            fwd.start()
            sends.append(fwd)
        for a in range(n):
            landed(a, t_oth, 1 - c)
        for a in range(n):
            landed(a, t_dia, 2)
        for a in range(n):
            for u in range(2):
                _rcopy(ins[a].at[u], outs[a].at[s, u], s_own.at[a, u], r_own.at[a, u], sibling).wait_recv()
            for k, t in enumerate([2 * nbr[0][0] + nbr[0][1], 2 * nbr[1][0] + nbr[1][1], t_dia]):
                _rcopy(ins[a].at[1 - c], outs[a].at[t, 1 - c], s_d2d.at[a, k], r_d2d.at[a, k], sibling).wait_recv()
        for cp in sends:
            cp.wait_send()

    dma = lambda k: pltpu.SemaphoreType.DMA((n, k))
    return pl.pallas_call(
        body, name="gather_weights", in_specs=[_HBM] * n, out_specs=[_HBM] * n,
        out_shape=[jax.ShapeDtypeStruct((N_CHIPS,) + w.shape, w.dtype) for w in shards],
        scratch_shapes=[dma(3), dma(3), dma(3), dma(3), dma(2), dma(2)],
    )(*shards)


_SEM = pl.BlockSpec(memory_space=pltpu.SEMAPHORE)
_SIDE_EFFECT = pltpu.SideEffectType.DATAFLOW_SIDE_EFFECTING


def _in_hbm(a):
    return pltpu.with_memory_space_constraint(a, pltpu.HBM)


def _split_copy(name, srcs, land_shapes, n_sems, copies):
    n, m = len(srcs), len(land_shapes)
    thru = [pltpu.HBM(a.shape, a.dtype) for a in srcs] + [pltpu.HBM(s.shape, s.dtype) for s in land_shapes]
    sems = (pltpu.SemaphoreType.DMA((n_sems,)), pltpu.SemaphoreType.DMA((n_sems,)))

    def start(dep=None):
        deps = [] if dep is None else [dep]

        def body(*refs):
            ins, lands = refs[:n], refs[n:n + m]
            ssem, rsem, token = refs[n + m + len(deps)], refs[n + m + len(deps) + 1], refs[-1]
            for cp in copies(ins, lands, ssem, rsem)[0]:
                cp.start()
            token[...] = jnp.zeros_like(token)

        out = pl.pallas_call(
            body, name=name + "_start", out_shape=(*sems, *thru, jax.ShapeDtypeStruct((8, HD), F32)),
            in_specs=[_HBM] * (n + m) + [pl.BlockSpec(memory_space=pl.ANY)] * len(deps),
            out_specs=(_SEM, _SEM, *[_HBM] * (n + m), pl.BlockSpec(memory_space=pltpu.VMEM)),
            input_output_aliases={i: 2 + i for i in range(n + m)},
            compiler_params=pltpu.CompilerParams(has_side_effects=_SIDE_EFFECT),
        )(*[_in_hbm(a) for a in srcs], *[_in_hbm(lax.empty(s.shape, s.dtype)) for s in land_shapes], *deps)
        return out[:-1], out[-1]

    def wait(state, after):
        def body(*refs):
            ins, lands, ssem, rsem = refs[:n], refs[n:n + m], refs[n + m], refs[n + m + 1]
            sent, arrivals = copies(ins, lands, ssem, rsem)
            for cp in sent:
                cp.wait_send()
            for cp in arrivals:
                cp.wait_recv()

        out = pl.pallas_call(
            body, name=name + "_wait", out_shape=tuple(thru),
            in_specs=[_HBM] * (n + m) + [_SEM, _SEM, pl.BlockSpec(memory_space=pl.ANY)], out_specs=[_HBM] * (n + m),
            input_output_aliases={i: i for i in range(n + m)},
            compiler_params=pltpu.CompilerParams(has_side_effects=_SIDE_EFFECT),
        )(*state[2:], state[0], state[1], after)
        return out[n:]

    return start, wait


def gather_direct(shards, tag):
    n = len(shards)

    def copies(ins, lands, ssem, rsem):
        x, y, c, chips = _place()
        s = 2 * x + y
        sibling = (x, y, 1 - c)
        sent, arrivals = [], []
        for a in range(n):
            for u in range(2):
                cp = _rcopy(ins[a].at[u], lands[a].at[s, u], ssem.at[5 * a + u], rsem.at[5 * a + u], sibling)
                sent.append(cp)
                arrivals.append(cp)
            for j, (cx, cy) in enumerate(chips):
                k = 5 * a + 2 + j
                sent.append(_rcopy(ins[a].at[c], lands[a].at[s, c], ssem.at[k], rsem.at[k], (cx, cy, c)))
                arrivals.append(_rcopy(ins[a].at[c], lands[a].at[2 * cx + cy, c], ssem.at[k], rsem.at[k], (cx, cy, c)))
        return sent, arrivals

    lands = [jax.ShapeDtypeStruct((N_CHIPS,) + w.shape, w.dtype) for w in shards]
    return _split_copy("gather_direct_" + tag, shards, lands, 5 * n, copies)


def pass_to_sibling(lands):
    n = len(lands)

    def body(*refs):
        ins = refs[:n]
        ssem, rsem = refs[2 * n:]
        x, y, c, chips = _place()
        sibling = (x, y, 1 - c)
        cps, arrivals = [], []
        for a in range(n):
            for j, (cx, cy) in enumerate(chips):
                t = 2 * cx + cy
                cps.append(_rcopy(ins[a].at[t, c], ins[a].at[t, c], ssem.at[a, j], rsem.at[a, j], sibling))
                arrivals.append(_rcopy(ins[a].at[t, c], ins[a].at[t, 1 - c], ssem.at[a, j], rsem.at[a, j], sibling))
        for cp in cps:
            cp.start()
        for cp, ar in zip(cps, arrivals):
            cp.wait_send()
            ar.wait_recv()

    return pl.pallas_call(
        body, name="pass_to_sibling", in_specs=[_HBM] * n, out_specs=[_HBM] * n,
        out_shape=[jax.ShapeDtypeStruct(a.shape, a.dtype) for a in lands], input_output_aliases={a: a for a in range(n)},
        scratch_shapes=[pltpu.SemaphoreType.DMA((n, 3)), pltpu.SemaphoreType.DMA((n, 3))],
    )(*lands)


def exchange_halves(grads):
    n = len(grads)

    def body(*refs):
        ins, outs = refs[:n], refs[n:2 * n]
        ssem, rsem = refs[2 * n:]
        x, y, c, _ = _place()
        cps = []
        for a in range(n):
            h = grads[a].shape[1] // 2
            cps.append(_rcopy(ins[a].at[:, pl.ds((1 - c) * h, h)], outs[a], ssem.at[a], rsem.at[a], (x, y, 1 - c)))
        for cp in cps:
            cp.start()
        for cp in cps:
            cp.wait()

    return pl.pallas_call(
        body, name="exchange_halves", in_specs=[_HBM] * n, out_specs=[_HBM] * n,
        out_shape=[jax.ShapeDtypeStruct((g.shape[0], g.shape[1] // 2, g.shape[2]), g.dtype) for g in grads],
        scratch_shapes=[pltpu.SemaphoreType.DMA((n,)), pltpu.SemaphoreType.DMA((n,))],
    )(*grads)


def scatter_to_chips(parts):
    n = len(parts)

    def body(*refs):
        ins, outs = refs[:n], refs[n:2 * n]
        ssem, rsem = refs[2 * n:]
        x, y, c, chips = _place()
        cps = [_rcopy(ins[a].at[2 * cx + cy], outs[a].at[j], ssem.at[a, j], rsem.at[a, j], (cx, cy, c))
               for a in range(n) for j, (cx, cy) in enumerate(chips)]
        for cp in cps:
            cp.start()
        for cp in cps:
            cp.wait()

    return pl.pallas_call(
        body, name="scatter_to_chips", in_specs=[_HBM] * n, out_specs=[_HBM] * n,
        out_shape=[jax.ShapeDtypeStruct((3,) + p.shape[1:], p.dtype) for p in parts],
        scratch_shapes=[pltpu.SemaphoreType.DMA((n, 3)), pltpu.SemaphoreType.DMA((n, 3))],
    )(*parts)


def scatter_direct(parts, tag):
    n = len(parts)

    def copies(ins, lands, ssem, rsem):
        x, y, c, chips = _place()
        cps = [_rcopy(ins[a].at[2 * cx + cy], lands[a].at[j], ssem.at[3 * a + j], rsem.at[3 * a + j], (cx, cy, c))
               for a in range(n) for j, (cx, cy) in enumerate(chips)]
        return cps, cps

    lands = [jax.ShapeDtypeStruct((3,) + p.shape[1:], p.dtype) for p in parts]
    return _split_copy("scatter_direct_" + tag, parts, lands, 3 * n, copies)


def share_halves(halves):
    n = len(halves)

    def body(*refs):
        ins, outs = refs[:n], refs[n:2 * n]
        ssem, rsem = refs[2 * n:]
        x, y, c, _ = _place()
        cps = [_rcopy(ins[i], outs[i], ssem.at[i], rsem.at[i], (x, y, 1 - c)) for i in range(n)]
        for cp in cps:
            cp.start()
        for cp in cps:
            cp.wait()

    return pl.pallas_call(
        body, name="share_halves", in_specs=[_HBM] * n, out_specs=[_HBM] * n,
        out_shape=[jax.ShapeDtypeStruct(h.shape, h.dtype) for h in halves],
        scratch_shapes=[pltpu.SemaphoreType.DMA((n,)), pltpu.SemaphoreType.DMA((n,))],
    )(*halves)


def adamw_shard(w, m, v, mine, theirs, c, name):
    _, r, cw = w.shape
    h, cg = mine[0].shape
    tr = next(t for t in (256, 176, 128) if h % t == 0 and t * cg * 4 <= (3 << 19))
    nb = h // tr
    wblk = pl.BlockSpec((None, tr, cw), lambda l, i, c_ref: (l, i, 0))
    gblk = lambda layer, own: pl.BlockSpec((tr, cg), lambda l, i, c_ref: (_held_block(l, i, c_ref, layer, own, nb), 0))
    return _adamw_halves(w, m, v, mine, theirs, c, name, (DEPTH, r // tr), wblk, gblk, nb, cw)


def _held_block(l, i, c_ref, layer, own, nb):
    in_use = (l == layer) & (((i // nb) == c_ref[0]) == own)
    return jnp.where(in_use, i % nb, 0)


def _adamw_halves(w, m, v, mine, theirs, c, name, grid, wblk, gblk, nb, cw):
    def body(c_ref, w_ref, m_ref, v_ref, m0, m1, t0, t1, g_ref, d_ref, nm_ref, nv_ref):
        is_mine = (pl.program_id(1) // nb) == c_ref[0]
        first = pl.program_id(0) == 0
        gg = jnp.where(is_mine, jnp.where(first, m0[:, :cw], m1[:, :cw]), jnp.where(first, t0[:, :cw], t1[:, :cw]))
        nm = ADAM_B1 * m_ref[...] + (1.0 - ADAM_B1) * gg
        nv = ADAM_B2 * v_ref[...] + (1.0 - ADAM_B2) * (gg * gg)
        m_hat = nm / (1.0 - ADAM_B1 ** ADAM_STEP)
        v_hat = nv / (1.0 - ADAM_B2 ** ADAM_STEP)
        g_ref[...] = gg
        d_ref[...] = -ADAM_LR * (m_hat / (jnp.sqrt(v_hat) + ADAM_EPS) + ADAM_WD * w_ref[...])
        nm_ref[...] = nm
        nv_ref[...] = nv

    return pl.pallas_call(
        body, name=name,
        grid_spec=pltpu.PrefetchScalarGridSpec(
            num_scalar_prefetch=1, grid=grid,
            in_specs=[wblk] * 3 + [gblk(0, True), gblk(1, True), gblk(0, False), gblk(1, False)], out_specs=[wblk] * 4),
        out_shape=[jax.ShapeDtypeStruct(w.shape, F32)] * 4, compiler_params=_cparams(("arbitrary", "arbitrary")),
    )(c, w, m, v, mine[0], mine[1], theirs[0], theirs[1])


def adamw_shard_t(wt, mt, vt, mine_t, theirs_t, c, name):
    _, cw, r = wt.shape
    h = mine_t[0].shape[1]
    tc = 256
    nb = h // tc
    wblk = pl.BlockSpec((None, cw, tc), lambda l, j, c_ref: (l, 0, j))
    gblk = lambda layer, own: pl.BlockSpec((cw, tc), lambda l, j, c_ref: (0, _held_block(l, j, c_ref, layer, own, nb)))
    return _adamw_halves(wt, mt, vt, mine_t, theirs_t, c, name, (DEPTH, r // tc), wblk, gblk, nb, cw)


def _half_rows(h, cols):
    for tr in (512, 256, 352, 128, 64):
        if h % tr == 0 and tr * cols * 4 <= 6 * 1024 * 1024:
            return tr
    raise ValueError((h, cols))


def add_sibling(grad, recv, c):
    _, r, cols = grad.shape
    h = r // 2
    tr = _half_rows(h, cols)
    nb = h // tr

    def body(c_ref, g_ref, r_ref, o_ref):
        o_ref[...] = (g_ref[...].astype(F32) + r_ref[...].astype(F32)).astype(BF16)

    return pl.pallas_call(
        body, name="add_sibling",
        grid_spec=pltpu.PrefetchScalarGridSpec(
            num_scalar_prefetch=1, grid=(N_CHIPS, nb),
            in_specs=[pl.BlockSpec((None, tr, cols), lambda t, i, c_ref: (t, c_ref[0] * nb + i, 0)),
                      pl.BlockSpec((None, tr, cols), lambda t, i, c_ref: (t, i, 0))],
            out_specs=pl.BlockSpec((None, tr, cols), lambda t, i, c_ref: (t, i, 0))),
        out_shape=jax.ShapeDtypeStruct((N_CHIPS, h, cols), BF16), compiler_params=_cparams(("arbitrary", "arbitrary")),
    )(c, grad, recv)


def add_chips(part, recv, s):
    _, h, cols = part.shape
    tr = _half_rows(h, cols)

    def body(s_ref, p_ref, r_ref, o_ref):
        o_ref[...] = ((p_ref[...].astype(F32) + r_ref[0].astype(F32)) + r_ref[1].astype(F32)) + r_ref[2].astype(F32)

    return pl.pallas_call(
        body, name="add_chips",
        grid_spec=pltpu.PrefetchScalarGridSpec(
            num_scalar_prefetch=1, grid=(h // tr,),
            in_specs=[pl.BlockSpec((None, tr, cols), lambda i, s_ref: (s_ref[0], i, 0)),
                      pl.BlockSpec((3, tr, cols), lambda i, s_ref: (0, i, 0))],
            out_specs=pl.BlockSpec((tr, cols), lambda i, s_ref: (i, 0))),
        out_shape=jax.ShapeDtypeStruct((h, cols), F32), compiler_params=_cparams(("arbitrary",)),
    )(s, part, recv)


def allreduce_small(vec):
    rows = vec.shape[0]

    def body(v_ref, o_ref, buf, ssem, rsem, lsem):
        x, y, c, chips = _place()
        me, sibling = (x, y, c), (x, y, 1 - c)

        def blk(px, py, pc):
            return buf.at[4 * px + 2 * py + pc]

        def copy(k, block, to, src=None):
            return _rcopy(blk(*block) if src is None else src, blk(*block), ssem.at[k], rsem.at[k], to)

        mine = pltpu.make_async_copy(v_ref, blk(*me), lsem)
        mine.start()
        first = [copy(0, me, sibling, src=v_ref)] + [copy(1 + j, me, (*chip, c), src=v_ref) for j, chip in enumerate(chips)]
        for cp in first:
            cp.start()
        passed = [copy(4 + j, (*chip, c), sibling) for j, chip in enumerate(chips)]
        for j, chip in enumerate(chips):
            copy(1 + j, (*chip, c), me).wait_recv()
            passed[j].start()
        copy(0, sibling, me).wait_recv()
        for j, chip in enumerate(chips):
            copy(4 + j, (*chip, 1 - c), me).wait_recv()
        for cp in first + passed:
            cp.wait_send()
        mine.wait()
        acc = buf[0]
        for d in range(1, N_DEV):
            acc = acc + buf[d]
        o_ref[...] = acc

    vm = pl.BlockSpec(memory_space=pltpu.VMEM)
    return pl.pallas_call(
        body, name="allreduce_small", in_specs=[vm], out_specs=vm, out_shape=jax.ShapeDtypeStruct(vec.shape, F32),
        scratch_shapes=[pltpu.VMEM((N_DEV, rows, HD), F32), pltpu.SemaphoreType.DMA((7,)), pltpu.SemaphoreType.DMA((7,)),
                        pltpu.SemaphoreType.DMA],
        compiler_params=pltpu.CompilerParams(vmem_limit_bytes=VMEM_LIMIT),
    )(vec)


SMALL_NAMES = ("norm1_g", "sgu_norm_g", "w_spatial", "b_spatial", "conv_w", "a_log", "dt_bias", "o_norm_g", "q_norm_g",
               "k_norm_g", "norm2_g")


def small_params(l, p, conv_full):
    return {"norm1_g": p["norm1_g"][l][None], "sgu_norm_g": p["sgu_norm_g"][l][:, None, :], "w_spatial": p["w_spatial"][l],
            "b_spatial": p["b_spatial"][l][..., None], "conv_w": conv_full[l], "a_log": p["a_log"][l], "dt_bias": p["dt_bias"][l],
            "o_norm_g": p["o_norm_g"][l][None], "q_norm_g": p["q_norm_g"][l][None], "k_norm_g": p["k_norm_g"][l][None],
            "norm2_g": p["norm2_g"][l][None]}


def local_step(x, target, wg, sps):
    saved = []
    for l in range(DEPTH):
        x, s = layer_forward(x, wg[l], sps[l])
        saved.append(s)
    dx, loss = loss_and_grad(x, target)
    bigs, smalls = [None] * DEPTH, [None] * DEPTH
    for l in reversed(range(DEPTH)):
        dx, bigs[l], smalls[l] = layer_backward(dx, wg[l], sps[l], saved[l])
    return loss, dx, bigs, smalls


_PACK_TILE = 8 * HD


def _pack(arrays):
    flat = jnp.concatenate([a.reshape(-1) for a in arrays])
    pad = -flat.shape[0] % _PACK_TILE
    return jnp.pad(flat, (0, pad)).reshape(-1, HD)


def _unpack(packed, shapes):
    flat, out, off = packed.reshape(-1), [], 0
    for shp in shapes:
        n = int(np.prod(shp))
        out.append(flat[off:off + n].reshape(shp))
        off += n
    return out


BIG_NAMES = ("in", "out", "gu", "down")
WEIGHT_ORDER = ("norm1_g", "w_in", "sgu_norm_g", "w_spatial", "b_spatial", "conv_w", "a_log", "dt_bias", "o_norm_g", "q_norm_g",
                "k_norm_g", "w_out", "norm2_g", "w_gate_up", "w_down")


def kernel(x, norm1_g, w_in, sgu_norm_g, w_spatial, b_spatial, conv_w, a_log, dt_bias, o_norm_g, q_norm_g, k_norm_g, w_out, norm2_g, w_gate_up, w_down, loss_target, m_norm1_g, m_w_in, m_sgu_norm_g, m_w_spatial, m_b_spatial, m_conv_w, m_a_log, m_dt_bias, m_o_norm_g, m_q_norm_g, m_k_norm_g, m_w_out, m_norm2_g, m_w_gate_up, m_w_down, v_norm1_g, v_w_in, v_sgu_norm_g, v_w_spatial, v_b_spatial, v_conv_w, v_a_log, v_dt_bias, v_o_norm_g, v_q_norm_g, v_k_norm_g, v_w_out, v_norm2_g, v_w_gate_up, v_w_down):
    w = dict(norm1_g=norm1_g, w_in=w_in, sgu_norm_g=sgu_norm_g, w_spatial=w_spatial, b_spatial=b_spatial, conv_w=conv_w,
             a_log=a_log, dt_bias=dt_bias, o_norm_g=o_norm_g, q_norm_g=q_norm_g, k_norm_g=k_norm_g, w_out=w_out,
             norm2_g=norm2_g, w_gate_up=w_gate_up, w_down=w_down)
    m = dict(norm1_g=m_norm1_g, w_in=m_w_in, sgu_norm_g=m_sgu_norm_g, w_spatial=m_w_spatial, b_spatial=m_b_spatial,
             conv_w=m_conv_w, a_log=m_a_log, dt_bias=m_dt_bias, o_norm_g=m_o_norm_g, q_norm_g=m_q_norm_g, k_norm_g=m_k_norm_g,
             w_out=m_w_out, norm2_g=m_norm2_g, w_gate_up=m_w_gate_up, w_down=m_w_down)
    v = dict(norm1_g=v_norm1_g, w_in=v_w_in, sgu_norm_g=v_sgu_norm_g, w_spatial=v_w_spatial, b_spatial=v_b_spatial,
             conv_w=v_conv_w, a_log=v_a_log, dt_bias=v_dt_bias, o_norm_g=v_o_norm_g, q_norm_g=v_q_norm_g, k_norm_g=v_k_norm_g,
             w_out=v_w_out, norm2_g=v_norm2_g, w_gate_up=v_w_gate_up, w_down=v_w_down)
    chip = (2 * lax.axis_index("x") + lax.axis_index("y")).astype(jnp.int32)
    core = lax.axis_index("c").astype(jnp.int32)

    in_pad = IN_SHARD_PAD - IN_SHARD
    w_in_pad = jnp.pad(w_in, ((0, 0), (0, 0), (0, in_pad)))

    halves_of = lambda a: a.reshape(2, a.shape[0] // 2, a.shape[1])
    bf_halves = lambda a: halves_of(a.astype(BF16))

    def ffn_shards(l):
        return [bf_halves(w_gate_up[l]), bf_halves(w_down[l]), bf_halves(w_out[l])]

    def mixer_shards(l):
        return [bf_halves(w_in_pad[l]), halves_of(conv_w[l])]

    def mixer_weights(g):
        g_in, g_conv = g
        return (shards_to_segments(g_in.reshape(N_CHIPS, D, IN_SHARD_PAD))[None],
                g_conv.reshape(N_CHIPS, B_CONV, -1).transpose(1, 0, 2).reshape(B_CONV, 3 * B_WIDTH))

    def ffn_weights(g, w_in_seg):
        g_gu, g_down, g_out = g
        return {"in": w_in_seg, "out": g_out.reshape(1, D, D), "gu": g_gu.reshape(1, N_CHIPS, D, GU_SHARD),
                "down": g_down.reshape(1, FFN, D)}

    def layer_params(l, conv_full):
        return small_params(0, {n: w[n][l:l + 1] for n in SMALL_NAMES if n != "conv_w"}, conv_full[None])

    w_in0, conv0 = mixer_weights(gather_weights(mixer_shards(0)))
    start_a, wait_a = gather_direct(ffn_shards(0), "ffn0")
    start_b, wait_b = gather_direct(mixer_shards(1), "mix1")
    start_c, wait_c = gather_direct(ffn_shards(1), "ffn1")
    state_a, token_a = start_a()
    state_b, token_b = start_b(token_a)
    state_c, token_c = start_c(token_b)
    sps = [layer_params(0, conv0), None]
    mix0, saved_m0 = mixers_forward(x[0], w_in0, sps[0], dep=token_c)
    wg0 = ffn_weights(pass_to_sibling(wait_a(state_a, mix0)), w_in0)
    x1, saved_f0 = ffn_forward(x[0], mix0, wg0, sps[0])
    w_in1, conv1 = mixer_weights(pass_to_sibling(wait_b(state_b, x1)))
    sps[1] = layer_params(1, conv1)
    mix1, saved_m1 = mixers_forward(x1, w_in1, sps[1])
    wg1 = ffn_weights(pass_to_sibling(wait_c(state_c, mix1)), w_in1)
    x2, saved_f1 = ffn_forward(x1, mix1, wg1, sps[1])
    saved1 = (saved_m1, saved_f1)
    dx, loss_tile = loss_and_grad(x2, loss_target[0])

    def to_chip_parts(grads):
        return [add_sibling(g, r, core.reshape(1)) for g, r in zip(grads, exchange_halves(grads))]

    def start_scatter(grads, tag):
        parts = to_chip_parts(grads)
        start, wait = scatter_direct(parts, tag)
        state, token = start()
        return parts, wait, state, token

    smalls = [None] * DEPTH
    by_chip_out = lambda t: t.reshape(N_CHIPS, OUT_SHARD, D)
    by_chip_down = lambda t: t.reshape(N_CHIPS, DOWN_SHARD, D)
    dx1, dnorm2_1, dw_gu1, dw_down1 = ffn_backward(dx, wg1, sps[1], saved1[1])
    pf1, wait_f1, st_f1, tok_f1 = start_scatter([dw_gu1, by_chip_down(dw_down1)], "ffn1")
    dx, dw_in1, dw_out1, small1 = mixers_backward(dx1, wg1, sps[1], saved1[0], dep=tok_f1)
    smalls[1] = {**small1, "norm2_g": dnorm2_1}
    pm1, wait_m1, st_m1, tok_m1 = start_scatter([dw_in1, by_chip_out(dw_out1)], "mix1")
    dx1, dnorm2_0, dw_gu0, dw_down0 = ffn_backward(dx, wg0, sps[0], saved_f0, dep=tok_m1)
    pf0, wait_f0, st_f0, tok_f0 = start_scatter([dw_gu0, by_chip_down(dw_down0)], "ffn0")
    dx, dw_in0, dw_out0, small0 = mixers_backward(dx1, wg0, sps[0], saved_m0, dep=tok_f0)
    smalls[0] = {**small0, "norm2_g": dnorm2_0}
    pm0, wait_m0, st_m0, tok_m0 = start_scatter([dw_in0, by_chip_out(dw_out0)], "mix0")
    rf0, rm1, rf1 = (list(wt(st, tok_m0)) for wt, st in ((wait_f0, st_f0), (wait_m1, st_m1), (wait_f1, st_f1)))

    def reduce_group(parts, from_chips):
        mine = [add_chips(p, r, chip.reshape(1)) for p, r in zip(parts, from_chips)]
        return mine, list(share_halves(mine))

    mine_f, theirs_f = reduce_group(pf0 + pf1 + pm1, rf0 + rf1 + rm1)
    grad, delta, new_m, new_v = {}, {}, {}, {}
    for a, n in enumerate(("w_gate_up", "w_down")):
        grad[n], delta[n], new_m[n], new_v[n] = adamw_shard(w[n], m[n], v[n], [mine_f[a], mine_f[2 + a]],
                                                            [theirs_f[a], theirs_f[2 + a]], core.reshape(1), "adamw_" + n)

    stacked = [jnp.stack([smalls[l][n] for l in range(DEPTH)]) for n in SMALL_NAMES]
    total = allreduce_small(_pack(stacked + [loss_tile[0, :1]]))
    shapes = [(DEPTH, B_CONV, 3 * B_WIDTH) if n == "conv_w" else w[n].shape for n in SMALL_NAMES]
    small_grads = dict(zip(SMALL_NAMES, _unpack(total, shapes + [(1,)])[:-1]))
    loss = _unpack(total, shapes + [(1,)])[-1][0]
    conv_cols = conv_w.shape[-1]
    small_grads["conv_w"] = lax.dynamic_slice_in_dim(small_grads["conv_w"], chip * conv_cols, conv_cols, axis=2)
    grad.update(small_grads)

    sshapes = [w[n].shape for n in SMALL_NAMES]
    packed = [_pack([d[n] for n in SMALL_NAMES]) for d in (w, grad, m, v)]
    small_out = adamw(*packed, "adamw_small")
    for dst, t in zip((delta, new_m, new_v), small_out):
        dst.update(zip(SMALL_NAMES, _unpack(t, sshapes)))

    mine_m0, theirs_m0 = reduce_group(pm0, list(wait_m0(st_m0, small_out[0])))
    tr_ = lambda t: jnp.swapaxes(t, -1, -2)
    cut = lambda t: tr_(t[:, :IN_SHARD])
    res = adamw_shard_t(tr_(w_in), tr_(m_w_in), tr_(v_w_in), [cut(mine_m0[0]), cut(mine_f[4])],
                        [cut(theirs_m0[0]), cut(theirs_f[4])], core.reshape(1), "adamw_w_in")
    grad["w_in"], delta["w_in"], new_m["w_in"], new_v["w_in"] = (tr_(t) for t in res)
    grad["w_out"], delta["w_out"], new_m["w_out"], new_v["w_out"] = adamw_shard(
        w_out, m_w_out, v_w_out, [mine_m0[1], mine_f[5]], [theirs_m0[1], theirs_f[5]], core.reshape(1), "adamw_w_out")

    out = [loss, dx[None]]
    for d in (grad, delta, new_m, new_v):
        out += [d[n] for n in WEIGHT_ORDER]
    return tuple(out)
```

```python
import functools
import math

import numpy as np
import jax
import jax.numpy as jnp
from jax import lax
from jax.experimental import pallas as pl
from jax.experimental.pallas import tpu as pltpu

F32 = jnp.float32
BF16 = jnp.bfloat16
HI = lax.Precision.HIGH

T = 2048
D = 2048
DEPTH = 2
HD = 128
A_GROUPS, A_WIDTH, A_CHUNK = 4, 512, 128
B_HEADS, B_WIDTH, B_CONV, B_CHUNK = 6, 768, 4, 64
C_HEADS, C_WIDTH, C_BLOCK = 6, 768, 128
C_BRANCHES = ((128, 1), (512, 4), (2048, 16))
FFN = 5632
IN_TOTAL = 6412
EPS = 1e-6
N_CHIPS = 4
N_DEV = 8
IN_SHARD = IN_TOTAL // N_CHIPS
IN_SHARD_PAD = 1664
GU_SHARD = 2 * FFN // N_CHIPS
OUT_SHARD = D // N_CHIPS
DOWN_SHARD = FFN // N_CHIPS
P_AU, P_AV, P_BQ, P_BK, P_BV, P_BG, P_BB, P_CQ, P_CK, P_CV, P_END = (
    0, 512, 1024, 1792, 2560, 3328, 4096, 4224, 4992, 5760, 6528)
GATE_COLS = 4108
VMEM_LIMIT = 56 * 1024 * 1024

ADAM_LR, ADAM_B1, ADAM_B2, ADAM_EPS, ADAM_WD, ADAM_STEP = 0.001, 0.9, 0.999, 1e-08, 0.01, 10


def _cparams(sem, vmem=VMEM_LIMIT):
    return pltpu.CompilerParams(dimension_semantics=sem, vmem_limit_bytes=vmem)


def _dims(nd, ta, tb):
    off = nd - 2
    ca = off + (0 if ta else 1)
    cb = off + (1 if tb else 0)
    batch = ((0,), (0,)) if nd == 3 else ((), ())
    return (((ca,), (cb,)), batch)


def _raw_mm(a, b, ta, tb, hi):
    if hi:
        return lax.dot_general(a, b, _dims(a.ndim, ta, tb), precision=HI, preferred_element_type=F32)
    return lax.dot_general(a.astype(BF16), b.astype(BF16), _dims(a.ndim, ta, tb), preferred_element_type=F32)


@functools.partial(jax.custom_vjp, nondiff_argnums=(2, 3, 4))
def _mm(a, b, ta=False, tb=False, hi=False):
    return _raw_mm(a, b, ta, tb, hi)


def _mm_fwd(a, b, ta, tb, hi):
    return _raw_mm(a, b, ta, tb, hi), (a, b)


def _mm_bwd(ta, tb, hi, res, g):
    a, b = res
    da = _raw_mm(g, b, False, not tb, hi) if not ta else _raw_mm(b, g, tb, True, hi)
    db = _raw_mm(a, g, not ta, False, hi) if not tb else _raw_mm(g, a, True, ta, hi)
    return da.astype(a.dtype), db.astype(b.dtype)


_mm.defvjp(_mm_fwd, _mm_bwd)


def _rms(x, g):
    return x * lax.rsqrt(jnp.mean(x * x, axis=-1, keepdims=True) + EPS) * g


def _gelu(x):
    return 0.5 * x * (1.0 + jnp.tanh(math.sqrt(2.0 / math.pi) * (x + 0.044715 * (x * x * x))))


def _sigmoid(x):
    return 1.0 / (1.0 + jnp.exp(-x))


def _silu(x):
    return x * _sigmoid(x)


def _softplus(x):
    return jnp.maximum(x, 0.0) + jnp.log(1.0 + jnp.exp(-jnp.abs(x)))


def _iota(shape, dim):
    return lax.broadcasted_iota(jnp.int32, shape, dim)


def _sgu_fn(u, v, sg, w, b):
    nc = T // A_CHUNK
    ug = _gelu(u)
    vn = _rms(_gelu(v), sg)
    causal = _iota((A_CHUNK, A_CHUNK), 0) >= _iota((A_CHUNK, A_CHUNK), 1)
    wm = jnp.where(causal, w, 0.0)
    wb = jnp.broadcast_to(wm[None], (nc, A_CHUNK, A_CHUNK))
    z = _mm(wb, vn.reshape(nc, A_CHUNK, HD)) + b[None]
    return ug * z.reshape(T, HD)


def _sgu_specs():
    col = lambda off: pl.BlockSpec((T, HD), lambda g, off=off: (0, off + g))
    par = [pl.BlockSpec((None, 1, HD), lambda g: (g, 0, 0)),
           pl.BlockSpec((None, A_CHUNK, A_CHUNK), lambda g: (g, 0, 0)),
           pl.BlockSpec((None, A_CHUNK, 1), lambda g: (g, 0, 0))]
    return col, par


def sgu_fwd(p2, sg, w, b):
    col, par = _sgu_specs()

    def body(u_ref, v_ref, sg_ref, w_ref, b_ref, y_ref):
        y_ref[...] = _sgu_fn(u_ref[...], v_ref[...], sg_ref[...], w_ref[...], b_ref[...]).astype(BF16)

    return pl.pallas_call(
        body, name="sgu_fwd", grid=(A_GROUPS,),
        in_specs=[col(P_AU // HD), col(P_AV // HD)] + par,
        out_specs=pl.BlockSpec((T, HD), lambda g: (0, g)),
        out_shape=jax.ShapeDtypeStruct((T, A_WIDTH), BF16),
        compiler_params=_cparams(("arbitrary",)),
    )(p2, p2, sg, w, b)


def sgu_bwd(p2, sg, w, b, dmix):
    col, par = _sgu_specs()

    def body(u_ref, v_ref, sg_ref, w_ref, b_ref, dy_ref, du_ref, dv_ref, dsg_ref, dw_ref, db_ref):
        _, vjp = jax.vjp(_sgu_fn, u_ref[...], v_ref[...], sg_ref[...], w_ref[...], b_ref[...])
        du, dv, dsg, dw, db = vjp(dy_ref[...])
        du_ref[...] = du.astype(BF16)
        dv_ref[...] = dv.astype(BF16)
        dsg_ref[...] = dsg
        dw_ref[...] = dw
        db_ref[...] = db

    gcol = pl.BlockSpec((T, HD), lambda g: (0, g))
    return pl.pallas_call(
        body, name="sgu_bwd", grid=(A_GROUPS,),
        in_specs=[col(P_AU // HD), col(P_AV // HD)] + par + [gcol],
        out_specs=[gcol, gcol] + par,
        out_shape=[jax.ShapeDtypeStruct((T, A_WIDTH), BF16), jax.ShapeDtypeStruct((T, A_WIDTH), BF16),
                   jax.ShapeDtypeStruct((A_GROUPS, 1, HD), F32), jax.ShapeDtypeStruct((A_GROUPS, A_CHUNK, A_CHUNK), F32),
                   jax.ShapeDtypeStruct((A_GROUPS, A_CHUNK, 1), F32)],
        compiler_params=_cparams(("arbitrary",)),
    )(p2, p2, sg, w, b, dmix)


def _attn_fn(q, k, v, qg, kg, slope, *, dil, nb):
    n = T // C_BLOCK
    qb = _rms(q, qg).reshape(n, C_BLOCK, HD)
    kb = _rms(k, kg).reshape(n, C_BLOCK, HD)
    vb = v.reshape(n, C_BLOCK, HD)
    scale = HD ** -0.5
    qi = _iota((n, C_BLOCK, C_BLOCK), 1)
    kj = _iota((n, C_BLOCK, C_BLOCK), 2)
    sl = slope[None] * float(dil)
    d_cur = qi - kj
    sc = jnp.where(d_cur >= 0, _mm(qb, kb, tb=True) * scale - sl * d_cur.astype(F32), -jnp.inf)
    mx = jnp.max(sc, axis=-1, keepdims=True)
    if nb > 1:
        kp = jnp.concatenate([jnp.zeros((1, C_BLOCK, HD), F32), kb[:-1]], axis=0)
        vp = jnp.concatenate([jnp.zeros((1, C_BLOCK, HD), F32), vb[:-1]], axis=0)
        has_prev = (_iota((n, C_BLOCK, C_BLOCK), 0) % nb) > 0
        d_prev = C_BLOCK + qi - kj
        sp = jnp.where((kj >= qi) & has_prev, _mm(qb, kp, tb=True) * scale - sl * d_prev.astype(F32), -jnp.inf)
        mx = jnp.maximum(mx, jnp.max(sp, axis=-1, keepdims=True))
    p = jnp.exp(sc - mx)
    den = jnp.sum(p, axis=-1, keepdims=True)
    if nb > 1:
        pp = jnp.exp(sp - mx)
        den = den + jnp.sum(pp, axis=-1, keepdims=True)
    out = _mm(p / den, vb)
    if nb > 1:
        out = out + _mm(pp / den, vp)
    lse = mx + jnp.log(den)
    return out.reshape(T, HD), jnp.broadcast_to(lse, (n, C_BLOCK, HD)).reshape(T, HD)


def _combine_fn(o1, o2, o3, l1, l2, l3):
    mx = jnp.maximum(jnp.maximum(l1, l2), l3)
    e1, e2, e3 = jnp.exp(l1 - mx), jnp.exp(l2 - mx), jnp.exp(l3 - mx)
    s = e1 + e2 + e3
    return (e1 / s) * o1 + (e2 / s) * o2 + (e3 / s) * o3


def _branch_blocks(dil):
    return -(-(T // dil) // C_BLOCK)


def _load_branch_order(ref, dil):
    if dil == 1:
        return ref[...]
    seg = T // dil
    return jnp.concatenate([ref[pl.ds(r, seg, stride=dil), :] for r in range(dil)], axis=0)


def _store_position_order(ref, val, dil, add=False):
    seg = T // dil
    for r in range(dil):
        rows = slice(None) if dil == 1 else pl.ds(r, seg, stride=dil)
        piece = val if dil == 1 else val[r * seg:(r + 1) * seg]
        if add:
            ref[rows, :] += piece
        else:
            ref[rows, :] = piece


def _dattn_specs():
    col = lambda off: pl.BlockSpec((T, HD), lambda h, off=off: (0, off // HD + h))
    row = pl.BlockSpec((1, HD), lambda h: (0, 0))
    slope = pl.BlockSpec((None, 1, HD), lambda h: (h, 0, 0))
    return [col(P_CQ), col(P_CK), col(P_CV), row, row, slope]


def _dattn_branches(q_ref, k_ref, v_ref, qg, kg, slope, o_scr, l_scr):
    for b, (_, dil) in enumerate(C_BRANCHES):
        q, k, v = (_load_branch_order(r, dil) for r in (q_ref, k_ref, v_ref))
        o, l = _attn_fn(q, k, v, qg, kg, slope, dil=dil, nb=_branch_blocks(dil))
        _store_position_order(o_scr.at[b], o, dil)
        _store_position_order(l_scr.at[b], l, dil)


def dattn_fwd(p2, qg, kg, slopes):
    def body(q_ref, k_ref, v_ref, qg_ref, kg_ref, s_ref, y_ref, o_scr, l_scr):
        _dattn_branches(q_ref, k_ref, v_ref, qg_ref[...], kg_ref[...], s_ref[...], o_scr, l_scr)
        y_ref[...] = _combine_fn(o_scr[0], o_scr[1], o_scr[2], l_scr[0], l_scr[1], l_scr[2]).astype(BF16)

    return pl.pallas_call(
        body, name="dattn_fwd", grid=(C_HEADS,), in_specs=_dattn_specs(), out_specs=pl.BlockSpec((T, HD), lambda h: (0, h)),
        out_shape=jax.ShapeDtypeStruct((T, C_WIDTH), BF16),
        scratch_shapes=[pltpu.VMEM((3, T, HD), F32), pltpu.VMEM((3, T, HD), F32)], compiler_params=_cparams(("arbitrary",)),
    )(p2, p2, p2, qg, kg, slopes)


def dattn_bwd(p2, qg, kg, slopes, dmix):
    hcol = pl.BlockSpec((T, HD), lambda h: (0, h))
    row = pl.BlockSpec((1, HD), lambda h: (0, 0))
    dy = pl.BlockSpec((T, HD), lambda h: (0, (A_WIDTH + B_WIDTH) // HD + h))

    def body(q_ref, k_ref, v_ref, qg_ref, kg_ref, s_ref, dy_ref, dq_ref, dk_ref, dv_ref, dqg_ref, dkg_ref, o_scr, l_scr, g_scr,
             acc):
        qg, kg, slope = qg_ref[...], kg_ref[...], s_ref[...]
        _dattn_branches(q_ref, k_ref, v_ref, qg, kg, slope, o_scr, l_scr)
        _, vjp = jax.vjp(_combine_fn, o_scr[0], o_scr[1], o_scr[2], l_scr[0], l_scr[1], l_scr[2])
        for i, g in enumerate(vjp(dy_ref[...])):
            g_scr[i] = g

        @pl.when(pl.program_id(0) == 0)
        def _():
            dqg_ref[...] = jnp.zeros_like(dqg_ref)
            dkg_ref[...] = jnp.zeros_like(dkg_ref)

        for b, (_, dil) in enumerate(C_BRANCHES):
            q, k, v = (_load_branch_order(r, dil) for r in (q_ref, k_ref, v_ref))
            do, dl = _load_branch_order(g_scr.at[b], dil), _load_branch_order(g_scr.at[3 + b], dil)
            fn = functools.partial(_attn_fn, dil=dil, nb=_branch_blocks(dil))
            _, vjp_b = jax.vjp(lambda a, b_, c, d, e, fn=fn: fn(a, b_, c, d, e, slope), q, k, v, qg, kg)
            dq, dk, dv, dqg, dkg = vjp_b((do, dl))
            for i, val in enumerate((dq, dk, dv)):
                _store_position_order(acc.at[i], val, dil, add=b > 0)
            dqg_ref[...] += dqg
            dkg_ref[...] += dkg
        for i, ref in enumerate((dq_ref, dk_ref, dv_ref)):
            ref[...] = acc[i].astype(BF16)

    scr = lambda n: pltpu.VMEM((n, T, HD), F32)
    return pl.pallas_call(
        body, name="dattn_bwd", grid=(C_HEADS,), in_specs=_dattn_specs() + [dy], out_specs=[hcol, hcol, hcol, row, row],
        out_shape=[jax.ShapeDtypeStruct((T, C_WIDTH), BF16)] * 3 + [jax.ShapeDtypeStruct((1, HD), F32)] * 2,
        scratch_shapes=[scr(3), scr(3), scr(6), scr(3)], compiler_params=_cparams(("arbitrary",)),
    )(p2, p2, p2, qg, kg, slopes, dmix)


_NCH = T // B_CHUNK


def _conv_taps(x, w_ref):
    rows = _iota(x.shape, 0)
    taps = []
    for j in range(B_CONV):
        s = B_CONV - 1 - j
        taps.append(x if s == 0 else jnp.where(rows >= s, pltpu.roll(x, s, 0), 0.0))
    pre = sum(w_ref[j:j + 1, :] * taps[j] for j in range(B_CONV))
    return pre, taps


def _conv_post(pre, mode):
    y = _silu(pre)
    if mode == "v":
        return y
    y = y * lax.rsqrt(jnp.sum(y * y, axis=-1, keepdims=True) + EPS)
    return y * (HD ** -0.5) if mode == "q" else y


def conv_fwd(p2, conv_w, mode):
    idx = "qkv".index(mode)
    xcol = pl.BlockSpec((T, HD), lambda h: (0, P_BQ // HD + B_HEADS * idx + h))
    wcol = pl.BlockSpec((B_CONV, HD), lambda h: (0, B_HEADS * idx + h))
    hcol = pl.BlockSpec((T, HD), lambda h: (0, h))

    def body(x_ref, w_ref, y_ref):
        pre, _ = _conv_taps(x_ref[...], w_ref)
        y_ref[...] = _conv_post(pre, mode)

    return pl.pallas_call(
        body, name=f"conv_fwd_{mode}", grid=(B_HEADS,), in_specs=[xcol, wcol], out_specs=hcol,
        out_shape=jax.ShapeDtypeStruct((T, B_WIDTH), F32), compiler_params=_cparams(("arbitrary",)),
    )(p2, conv_w)


def conv_bwd(p2, conv_w, dys, mode):
    idx = "qkv".index(mode)
    xcol = pl.BlockSpec((T, HD), lambda h: (0, P_BQ // HD + B_HEADS * idx + h))
    wcol = pl.BlockSpec((B_CONV, HD), lambda h: (0, B_HEADS * idx + h))
    hcol = pl.BlockSpec((T, HD), lambda h: (0, h))
    wout = pl.BlockSpec((B_CONV, HD), lambda h: (0, h))

    def body(x_ref, w_ref, *rest):
        dy_refs, (dx_ref, dw_ref) = rest[:-2], rest[-2:]
        pre, taps = _conv_taps(x_ref[...], w_ref)
        _, vjp = jax.vjp(functools.partial(_conv_post, mode=mode), pre)
        (dpre,) = vjp(sum(r[...] for r in dy_refs))
        rows = _iota(dpre.shape, 0)
        dx = w_ref[B_CONV - 1:B_CONV, :] * dpre
        for j in range(B_CONV):
            s = B_CONV - 1 - j
            dw_ref[j:j + 1, :] = jnp.sum(dpre * taps[j], axis=0, keepdims=True)
            if s > 0:
                dx = dx + w_ref[j:j + 1, :] * jnp.where(rows < T - s, pltpu.roll(dpre, T - s, 0), 0.0)
        dx_ref[...] = dx.astype(BF16)

    return pl.pallas_call(
        body, name=f"conv_bwd_{mode}", grid=(B_HEADS,), in_specs=[xcol, wcol] + [hcol] * len(dys), out_specs=[hcol, wout],
        out_shape=[jax.ShapeDtypeStruct((T, B_WIDTH), BF16), jax.ShapeDtypeStruct((B_CONV, B_WIDTH), F32)],
        compiler_params=_cparams(("arbitrary",)),
    )(p2, conv_w, *dys)


def _gates_fn(bg, al, dtb, h):
    r = _iota((HD, HD), 0)
    logit = _mm(bg, (r == h).astype(F32), hi=True)
    a = _mm(bg, (r == h + B_HEADS).astype(F32), hi=True)
    beta = _sigmoid(logit)
    graw = -jnp.exp(al) * _softplus(a + dtb)
    tri = (_iota((_NCH, B_CHUNK, B_CHUNK), 1) >= _iota((_NCH, B_CHUNK, B_CHUNK), 2)).astype(F32)
    g = _mm(tri, graw.reshape(_NCH, B_CHUNK, HD), hi=True).reshape(T, HD)
    return beta, g


def _gates_specs():
    bg = pl.BlockSpec((T, HD), lambda h: (0, P_BB // HD))
    par = pl.BlockSpec((None, 1, HD), lambda h: (h, 0, 0))
    out = pl.BlockSpec((None, T, HD), lambda h: (h, 0, 0))
    return bg, par, out


def gates_fwd(p2, al, dtb):
    bg, par, out = _gates_specs()

    def body(bg_ref, al_ref, dtb_ref, beta_ref, g_ref):
        beta, g = _gates_fn(bg_ref[...], al_ref[...], dtb_ref[...], pl.program_id(0))
        beta_ref[...] = beta
        g_ref[...] = g

    return pl.pallas_call(
        body, name="gates_fwd", grid=(B_HEADS,), in_specs=[bg, par, par], out_specs=[out, out],
        out_shape=[jax.ShapeDtypeStruct((B_HEADS, T, HD), F32)] * 2, compiler_params=_cparams(("arbitrary",)),
    )(p2, al, dtb)


def gates_bwd(p2, al, dtb, dbeta, dg1, dg2):
    bg, par, out = _gates_specs()
    acc = pl.BlockSpec((T, HD), lambda h: (0, 0))

    def body(bg_ref, al_ref, dtb_ref, dbeta_ref, dg1_ref, dg2_ref, dbg_ref, dal_ref, ddtb_ref, acc_ref):
        h = pl.program_id(0)
        _, vjp = jax.vjp(lambda a, b, c: _gates_fn(a, b, c, h), bg_ref[...], al_ref[...], dtb_ref[...])
        dbg, dal, ddtb = vjp((dbeta_ref[...], dg1_ref[...] + dg2_ref[...]))

        @pl.when(h == 0)
        def _():
            acc_ref[...] = jnp.zeros_like(acc_ref)

        acc_ref[...] += dbg
        dbg_ref[...] = acc_ref[...].astype(BF16)
        dal_ref[...] = jnp.broadcast_to(jnp.sum(dal, axis=-1, keepdims=True), (1, HD))
        ddtb_ref[...] = jnp.broadcast_to(jnp.sum(ddtb, axis=-1, keepdims=True), (1, HD))

    return pl.pallas_call(
        body, name="gates_bwd", grid=(B_HEADS,), in_specs=[bg, par, par, out, out, out], out_specs=[acc, par, par],
        out_shape=[jax.ShapeDtypeStruct((T, HD), BF16)] + [jax.ShapeDtypeStruct((B_HEADS, 1, HD), F32)] * 2,
        scratch_shapes=[pltpu.VMEM((T, HD), F32)], compiler_params=_cparams(("arbitrary",)),
    )(p2, al, dtb, dbeta, dg1, dg2)


def _unit_lower_inverse(a):
    eye = (_iota(a.shape, 1) == _iota(a.shape, 2)).astype(F32)
    x = eye - a
    p = _mm(a, a, hi=True)
    for i in range(5):
        x = x + _mm(x, p, hi=True)
        if i < 4:
            p = _mm(p, p, hi=True)
    return x


_WY_CH = 8
_WY_ROWS = _WY_CH * B_CHUNK


def _wy_fn(q, k, v, beta, g):
    sh = (q.shape[0] // B_CHUNK, B_CHUNK, HD)
    q3, k3, v3, b3, g3 = (t.reshape(sh) for t in (q, k, v, beta, g))
    gd = g3[:, :, :B_CHUNK] - jnp.swapaxes(g3, 1, 2)[:, :B_CHUNK, :]
    ii, jj = _iota(gd.shape, 1), _iota(gd.shape, 2)
    decay = jnp.exp(jnp.where(ii >= jj, gd, -jnp.inf))
    kb = k3 * b3
    a = _mm(kb, k3, tb=True) * jnp.where(ii > jj, decay, 0.0)
    tinv = _unit_lower_inverse(a)
    u = _mm(tinv, v3 * b3, hi=True)
    w = _mm(tinv, kb * jnp.exp(g3), hi=True)
    attn = _mm(q3, k3, tb=True) * decay
    return u.reshape(q.shape), w.reshape(q.shape), attn


def _wy_specs():
    hcol = pl.BlockSpec((_WY_ROWS, HD), lambda h, i: (i, h))
    hb = pl.BlockSpec((None, _WY_ROWS, HD), lambda h, i: (h, i, 0))
    at = pl.BlockSpec((None, _WY_CH, B_CHUNK, B_CHUNK), lambda h, i: (h, i, 0, 0))
    return hcol, hb, at


_WY_GRID = (B_HEADS, _NCH // _WY_CH)


def wy_fwd(q, k, v, beta, g):
    hcol, hb, at = _wy_specs()

    def body(q_ref, k_ref, v_ref, b_ref, g_ref, u_ref, w_ref, a_ref):
        u, w, a = _wy_fn(q_ref[...], k_ref[...], v_ref[...], b_ref[...], g_ref[...])
        u_ref[...] = u
        w_ref[...] = w
        a_ref[...] = a

    return pl.pallas_call(
        body, name="wy_fwd", grid=_WY_GRID, in_specs=[hcol, hcol, hcol, hb, hb], out_specs=[hcol, hcol, at],
        out_shape=[jax.ShapeDtypeStruct((T, B_WIDTH), F32)] * 2 + [jax.ShapeDtypeStruct((B_HEADS, _NCH, B_CHUNK, B_CHUNK), F32)],
        compiler_params=_cparams(("arbitrary", "arbitrary")),
    )(q, k, v, beta, g)


def wy_bwd(q, k, v, beta, g, du, dw, dattn):
    hcol, hb, at = _wy_specs()

    def body(q_ref, k_ref, v_ref, b_ref, g_ref, du_ref, dw_ref, da_ref, dq_ref, dk_ref, dv_ref, db_ref, dg_ref):
        _, vjp = jax.vjp(_wy_fn, q_ref[...], k_ref[...], v_ref[...], b_ref[...], g_ref[...])
        for r, t in zip((dq_ref, dk_ref, dv_ref, db_ref, dg_ref), vjp((du_ref[...], dw_ref[...], da_ref[...]))):
            r[...] = t

    return pl.pallas_call(
        body, name="wy_bwd", grid=_WY_GRID, in_specs=[hcol, hcol, hcol, hb, hb, hcol, hcol, at],
        out_specs=[hcol, hcol, hcol, hb, hb],
        out_shape=[jax.ShapeDtypeStruct((T, B_WIDTH), F32)] * 3 + [jax.ShapeDtypeStruct((B_HEADS, T, HD), F32)] * 2,
        compiler_params=_cparams(("arbitrary", "arbitrary")),
    )(q, k, v, beta, g, du, dw, dattn)


def _scan_step_fn(q, k, u, w, g, attn, gate, og, s):
    v_new = u - _mm(w, s)
    o = _mm(q * jnp.exp(g), s) + _mm(attn, v_new)
    g_last = jnp.sum(jnp.where(_iota(g.shape, 0) == B_CHUNK - 1, g, 0.0), axis=0, keepdims=True)
    s_new = s * jnp.exp(g_last) + _mm(k * jnp.exp(g_last - g), v_new, ta=True)
    return _rms(o, og) * _silu(gate), s_new


def _scan_specs(rev):
    ch = (lambda n: _NCH - 1 - n) if rev else (lambda n: n)
    rows = pl.BlockSpec((B_CHUNK, B_WIDTH), lambda n: (ch(n), 0))
    gb = pl.BlockSpec((B_HEADS, B_CHUNK, HD), lambda n: (0, ch(n), 0))
    at = pl.BlockSpec((B_HEADS, None, B_CHUNK, B_CHUNK), lambda n: (0, ch(n), 0, 0))
    og = pl.BlockSpec((1, HD), lambda n: (0, 0))
    st = pl.BlockSpec((None, B_HEADS, HD, HD), lambda n: (ch(n), 0, 0, 0))
    return rows, gb, at, og, st


def scan_fwd(q, k, u, w, g, attn, gate, og):
    rows, gb, at, ogs, st = _scan_specs(False)

    def body(q_ref, k_ref, u_ref, w_ref, g_ref, a_ref, gate_ref, og_ref, y_ref, st_ref, s_ref):
        @pl.when(pl.program_id(0) == 0)
        def _():
            s_ref[...] = jnp.zeros_like(s_ref)

        for h in range(B_HEADS):
            c = slice(h * HD, (h + 1) * HD)
            s = s_ref[h]
            st_ref[h] = s
            y, s_new = _scan_step_fn(q_ref[:, c], k_ref[:, c], u_ref[:, c], w_ref[:, c], g_ref[h], a_ref[h],
                                     gate_ref[:, c], og_ref[...], s)
            y_ref[:, c] = y.astype(BF16)
            s_ref[h] = s_new

    return pl.pallas_call(
        body, name="scan_fwd", grid=(_NCH,), in_specs=[rows, rows, rows, rows, gb, at, rows, ogs], out_specs=[rows, st],
        out_shape=[jax.ShapeDtypeStruct((T, B_WIDTH), BF16), jax.ShapeDtypeStruct((_NCH, B_HEADS, HD, HD), F32)],
        scratch_shapes=[pltpu.VMEM((B_HEADS, HD, HD), F32)], compiler_params=_cparams(("arbitrary",)),
    )(q, k, u, w, g, attn, gate, og)


def scan_bwd(q, k, u, w, g, attn, gate, og, states, dmix):
    rows, gb, at, ogs, st = _scan_specs(True)
    dyb = pl.BlockSpec((B_CHUNK, HD), lambda n: (_NCH - 1 - n, 0))

    def body(q_ref, k_ref, u_ref, w_ref, g_ref, a_ref, gate_ref, og_ref, st_ref, *rest):
        dy_refs, (dq_ref, dk_ref, du_ref, dw_ref, dgate_ref, dg_ref, da_ref, dog_ref, ds_ref) = rest[:B_HEADS], rest[B_HEADS:]

        @pl.when(pl.program_id(0) == 0)
        def _():
            ds_ref[...] = jnp.zeros_like(ds_ref)
            dog_ref[...] = jnp.zeros_like(dog_ref)

        for h in range(B_HEADS):
            c = slice(h * HD, (h + 1) * HD)
            _, vjp = jax.vjp(_scan_step_fn, q_ref[:, c], k_ref[:, c], u_ref[:, c], w_ref[:, c], g_ref[h], a_ref[h],
                             gate_ref[:, c], og_ref[...], st_ref[h])
            dq, dk, du, dw, dg, da, dgate, dog, ds = vjp((dy_refs[h][...], ds_ref[h]))
            dq_ref[:, c] = dq
            dk_ref[:, c] = dk
            du_ref[:, c] = du
            dw_ref[:, c] = dw
            dgate_ref[:, c] = dgate.astype(BF16)
            dg_ref[h] = dg
            da_ref[h] = da
            dog_ref[...] += dog
            ds_ref[h] = ds

    dy_specs = [pl.BlockSpec((B_CHUNK, HD), lambda n, h=h: (_NCH - 1 - n, A_WIDTH // HD + h)) for h in range(B_HEADS)]
    return pl.pallas_call(
        body, name="scan_bwd", grid=(_NCH,),
        in_specs=[rows, rows, rows, rows, gb, at, rows, ogs, st] + dy_specs,
        out_specs=[rows] * 5 + [gb, at, ogs],
        out_shape=[jax.ShapeDtypeStruct((T, B_WIDTH), F32)] * 4 + [jax.ShapeDtypeStruct((T, B_WIDTH), BF16)]
        + [jax.ShapeDtypeStruct((B_HEADS, T, HD), F32), jax.ShapeDtypeStruct((B_HEADS, _NCH, B_CHUNK, B_CHUNK), F32),
           jax.ShapeDtypeStruct((1, HD), F32)],
        scratch_shapes=[pltpu.VMEM((B_HEADS, HD, HD), F32)], compiler_params=_cparams(("arbitrary",)),
    )(q, k, u, w, g, attn, gate, og, states, *([dmix] * B_HEADS))


def _lanes(vec):
    return jnp.broadcast_to(vec[:, None, None], (vec.shape[0], 1, HD))


def gdn_forward(p2, conv_w, a_log, dt_bias, og):
    qa, ka, va = (conv_fwd(p2, conv_w, m) for m in "qkv")
    beta, g = gates_fwd(p2, _lanes(a_log), _lanes(dt_bias))
    u, w, attn = wy_fwd(qa, ka, va, beta, g)
    gate = p2[:, P_BG:P_BB]
    y, states = scan_fwd(qa, ka, u, w, g, attn, gate, og)
    return y, (qa, ka, va, beta, g, u, w, attn, gate, states)


def gdn_backward(p2, conv_w, a_log, dt_bias, og, saved, dmix):
    qa, ka, va, beta, g, u, w, attn, gate, states = saved
    dq1, dk1, du, dw, dgate, dg1, dattn, dog = scan_bwd(qa, ka, u, w, g, attn, gate, og, states, dmix)
    dq2, dk2, dv, dbeta, dg2 = wy_bwd(qa, ka, va, beta, g, du, dw, dattn)
    dbg, dal, ddtb = gates_bwd(p2, _lanes(a_log), _lanes(dt_bias), dbeta, dg1, dg2)
    dxq, dwq = conv_bwd(p2, conv_w, [dq1, dq2], "q")
    dxk, dwk = conv_bwd(p2, conv_w, [dk1, dk2], "k")
    dxv, dwv = conv_bwd(p2, conv_w, [dv], "v")
    return [dxq, dxk, dxv, dgate, dbg], jnp.concatenate([dwq, dwk, dwv], axis=1), dal[:, 0, 0], ddtb[:, 0, 0], dog


_SLOPES = np.exp2(-8.0 * (np.arange(C_HEADS, dtype=np.float64) + 1.0) / C_HEADS).astype(np.float32)


def _alibi_slopes():
    return _lanes(jnp.asarray(_SLOPES))


_ROWS = 256
_TM = 512


def _dep_specs(dep, ngrid):
    if dep is None:
        return [], []
    return [dep], [pl.BlockSpec((8, HD), lambda *_: (0, 0))]


def rmsnorm_fwd(x, g, dep=None):
    blk = pl.BlockSpec((_ROWS, D), lambda i: (i, 0))
    deps, dspecs = _dep_specs(dep, 1)

    def body(x_ref, g_ref, *rest):
        rest[-1][...] = _rms(x_ref[...], g_ref[...]).astype(BF16)

    return pl.pallas_call(
        body, name="rmsnorm_fwd", grid=(T // _ROWS,), in_specs=[blk, pl.BlockSpec((1, D), lambda i: (0, 0))] + dspecs,
        out_specs=blk, out_shape=jax.ShapeDtypeStruct((T, D), BF16), compiler_params=_cparams(("arbitrary",)),
    )(x, g, *deps)


def rmsnorm_bwd(x, g, dh, dres):
    blk = pl.BlockSpec((_ROWS, D), lambda i: (i, 0))
    row = pl.BlockSpec((1, D), lambda i: (0, 0))

    def body(x_ref, g_ref, dh_ref, dres_ref, dx_ref, dg_ref):
        _, vjp = jax.vjp(_rms, x_ref[...], g_ref[...])
        dx, dg = vjp(dh_ref[...])
        dx_ref[...] = dres_ref[...] + dx

        @pl.when(pl.program_id(0) == 0)
        def _():
            dg_ref[...] = jnp.zeros_like(dg_ref)

        dg_ref[...] += dg

    return pl.pallas_call(
        body, name="rmsnorm_bwd", grid=(T // _ROWS,), in_specs=[blk, row, blk, blk], out_specs=[blk, row],
        out_shape=[jax.ShapeDtypeStruct((T, D), F32), jax.ShapeDtypeStruct((1, D), F32)],
        compiler_params=_cparams(("arbitrary",)),
    )(x, g, dh, dres)


def _matmul(name, a, b, *, grid, a_spec, b_spec, o_spec, out_shape, ta=False, tb=False, k_axis=None, res=None, dep=None):
    dims = _dims(2, ta, tb)
    deps, dspecs = _dep_specs(dep, len(grid))

    def body(a_ref, b_ref, *rest):
        o_ref = rest[-1]
        prod = lax.dot_general(a_ref[...].astype(BF16), b_ref[...].astype(BF16), dims, preferred_element_type=F32)
        if res is not None:
            prod = prod + rest[0][...]
        if k_axis is None:
            o_ref[...] = prod.astype(o_ref.dtype)
        else:
            @pl.when(pl.program_id(k_axis) == 0)
            def _():
                o_ref[...] = prod

            @pl.when(pl.program_id(k_axis) > 0)
            def _():
                o_ref[...] += prod

    sem = tuple("arbitrary" for _ in grid)
    ins = [a, b] + ([res] if res is not None else []) + deps
    specs = [a_spec, b_spec] + ([o_spec] if res is not None else []) + dspecs
    return pl.pallas_call(
        body, name=name, grid=grid, in_specs=specs, out_specs=o_spec, out_shape=out_shape, compiler_params=_cparams(sem),
    )(*ins)


_IN_TN = P_END // 3


def mm_proj(h1, wp_in, l):
    return _matmul(
        "mm_proj", h1, wp_in, grid=(P_END // _IN_TN, T // _TM),
        a_spec=pl.BlockSpec((_TM, D), lambda j, i: (i, 0)),
        b_spec=pl.BlockSpec((None, D, _IN_TN), lambda j, i: (l, 0, j)),
        o_spec=pl.BlockSpec((_TM, _IN_TN), lambda j, i: (i, j)), out_shape=jax.ShapeDtypeStruct((T, P_END), F32))


def mm_dh1(dp2, wp_in, l):
    return _matmul(
        "mm_dh1", dp2, wp_in, grid=(T // _TM, P_END // _IN_TN), tb=True, k_axis=1,
        a_spec=pl.BlockSpec((_TM, _IN_TN), lambda i, k: (i, k)),
        b_spec=pl.BlockSpec((None, D, _IN_TN), lambda i, k: (l, 0, k)),
        o_spec=pl.BlockSpec((_TM, D), lambda i, k: (i, 0)), out_shape=jax.ShapeDtypeStruct((T, D), F32))


def mm_dwin(h1, dp2):
    return _matmul(
        "mm_dwin", h1, dp2, grid=(P_END // _IN_TN, D // _TM), ta=True,
        a_spec=pl.BlockSpec((T, _TM), lambda j, i: (0, i)),
        b_spec=pl.BlockSpec((T, _IN_TN), lambda j, i: (0, j)),
        o_spec=pl.BlockSpec((_TM, _IN_TN), lambda j, i: (i, j)), out_shape=jax.ShapeDtypeStruct((D, P_END), BF16))


def _mm_square(name, a, w, l, res, tb, dep=None):
    tn = 1024
    b_spec = (pl.BlockSpec((None, tn, D), lambda j, i: (l, j, 0)) if tb else pl.BlockSpec((None, D, tn), lambda j, i: (l, 0, j)))
    return _matmul(
        name, a, w, grid=(D // tn, T // _TM), tb=tb, res=res, dep=dep,
        a_spec=pl.BlockSpec((_TM, D), lambda j, i: (i, 0)), b_spec=b_spec,
        o_spec=pl.BlockSpec((_TM, tn), lambda j, i: (i, j)), out_shape=jax.ShapeDtypeStruct((T, D), F32))


def mm_out(mix, wg_out, l, x):
    return _mm_square("mm_out", mix, wg_out, l, x, False)


def mm_dmix(dx1, wg_out, l, dep=None):
    return _mm_square("mm_dmix", dx1, wg_out, l, None, True, dep)


def mm_dwout(mix, dx1):
    tn = 1024
    return _matmul(
        "mm_dwout", mix, dx1, grid=(D // tn, D // _TM), ta=True,
        a_spec=pl.BlockSpec((T, _TM), lambda j, i: (0, i)), b_spec=pl.BlockSpec((T, tn), lambda j, i: (0, j)),
        o_spec=pl.BlockSpec((_TM, tn), lambda j, i: (i, j)), out_shape=jax.ShapeDtypeStruct((D, D), BF16))


_GU_TN = GU_SHARD // 2


def mm_gu(h2, wg_gu, l):
    return _matmul(
        "mm_gu", h2, wg_gu, grid=(N_CHIPS, 2, T // _TM),
        a_spec=pl.BlockSpec((_TM, D), lambda s, j, i: (i, 0)),
        b_spec=pl.BlockSpec((None, None, D, _GU_TN), lambda s, j, i: (l, s, 0, j)),
        o_spec=pl.BlockSpec((_TM, _GU_TN), lambda s, j, i: (i, 2 * s + j)),
        out_shape=jax.ShapeDtypeStruct((T, 2 * FFN), F32))


_GU_NJ = FFN // _GU_TN


def mm_dh2(dgu, wg_gu, l):
    return _matmul(
        "mm_dh2", dgu, wg_gu, grid=(T // _TM, 2 * N_CHIPS), tb=True, k_axis=1,
        a_spec=pl.BlockSpec((None, _TM, _GU_TN), lambda i, k: (k // _GU_NJ, i, k % _GU_NJ)),
        b_spec=pl.BlockSpec((None, None, D, _GU_TN), lambda i, k: (l, k // 2, 0, k % 2)),
        o_spec=pl.BlockSpec((_TM, D), lambda i, k: (i, 0)), out_shape=jax.ShapeDtypeStruct((T, D), F32))


def mm_dwgu(h2, dgu):
    return _matmul(
        "mm_dwgu", h2, dgu, grid=(N_CHIPS, 2, D // _TM), ta=True,
        a_spec=pl.BlockSpec((T, _TM), lambda s, j, i: (0, i)),
        b_spec=pl.BlockSpec((None, T, _GU_TN), lambda s, j, i: ((2 * s + j) // _GU_NJ, 0, (2 * s + j) % _GU_NJ)),
        o_spec=pl.BlockSpec((None, _TM, _GU_TN), lambda s, j, i: (s, i, j)),
        out_shape=jax.ShapeDtypeStruct((N_CHIPS, D, GU_SHARD), BF16))


def mm_down(act, wg_down, l, x1):
    tn = 512
    return _matmul(
        "mm_down", act, wg_down, grid=(D // tn, T // _TM), res=x1,
        a_spec=pl.BlockSpec((_TM, FFN), lambda j, i: (i, 0)),
        b_spec=pl.BlockSpec((None, FFN, tn), lambda j, i: (l, 0, j)),
        o_spec=pl.BlockSpec((_TM, tn), lambda j, i: (i, j)), out_shape=jax.ShapeDtypeStruct((T, D), F32))


def mm_dact(dx2, wg_down, l, dep=None):
    tn = DOWN_SHARD
    return _matmul(
        "mm_dact", dx2, wg_down, grid=(FFN // tn, T // _TM), tb=True, dep=dep,
        a_spec=pl.BlockSpec((_TM, D), lambda j, i: (i, 0)),
        b_spec=pl.BlockSpec((None, tn, D), lambda j, i: (l, j, 0)),
        o_spec=pl.BlockSpec((_TM, tn), lambda j, i: (i, j)), out_shape=jax.ShapeDtypeStruct((T, FFN), F32))


def mm_dwdown(act, dx2):
    tm, tn = DOWN_SHARD, 512
    return _matmul(
        "mm_dwdown", act, dx2, grid=(D // tn, FFN // tm), ta=True,
        a_spec=pl.BlockSpec((T, tm), lambda j, i: (0, i)), b_spec=pl.BlockSpec((T, tn), lambda j, i: (0, j)),
        o_spec=pl.BlockSpec((tm, tn), lambda j, i: (i, j)), out_shape=jax.ShapeDtypeStruct((FFN, D), BF16))


_FF_TN = 1408


def _swiglu_fn(gt, up):
    return _silu(gt) * up


def _gate_up_specs():
    gate = pl.BlockSpec((None, None, D, _FF_TN), lambda j, i: (0, j // 2, 0, j % 2))
    up = pl.BlockSpec((None, None, D, _FF_TN), lambda j, i: (0, N_CHIPS // 2 + j // 2, 0, j % 2))
    both = pl.BlockSpec((2, _TM, _FF_TN), lambda j, i: (0, i, j))
    return gate, up, both


def mm_gu_swiglu(h2, wg_gu):
    gate, up, both = _gate_up_specs()

    def body(h_ref, wg_ref, wu_ref, gu_ref, act_ref):
        h = h_ref[...]
        gt = jnp.dot(h, wg_ref[...], preferred_element_type=F32)
        u = jnp.dot(h, wu_ref[...], preferred_element_type=F32)
        gu_ref[0] = gt
        gu_ref[1] = u
        act_ref[...] = _swiglu_fn(gt, u).astype(BF16)

    return pl.pallas_call(
        body, name="mm_gu_swiglu", grid=(FFN // _FF_TN, T // _TM),
        in_specs=[pl.BlockSpec((_TM, D), lambda j, i: (i, 0)), gate, up],
        out_specs=[both, pl.BlockSpec((_TM, _FF_TN), lambda j, i: (i, j))],
        out_shape=[jax.ShapeDtypeStruct((2, T, FFN), F32), jax.ShapeDtypeStruct((T, FFN), BF16)],
        compiler_params=_cparams(("arbitrary", "arbitrary")),
    )(h2, wg_gu, wg_gu)


def mm_dact_swiglu(dx2, wg_down, gu, dep=None):
    _, _, both = _gate_up_specs()
    deps, dspecs = _dep_specs(dep, 2)

    def body(dx_ref, w_ref, gu_ref, *rest):
        dact = lax.dot_general(dx_ref[...].astype(BF16), w_ref[...], _dims(2, False, True), preferred_element_type=F32)
        _, vjp = jax.vjp(_swiglu_fn, gu_ref[0], gu_ref[1])
        dgt, dup = vjp(dact)
        rest[-1][0] = dgt.astype(BF16)
        rest[-1][1] = dup.astype(BF16)

    return pl.pallas_call(
        body, name="mm_dact_swiglu", grid=(FFN // _FF_TN, T // _TM),
        in_specs=[pl.BlockSpec((_TM, D), lambda j, i: (i, 0)), pl.BlockSpec((None, _FF_TN, D), lambda j, i: (0, j, 0)), both]
        + dspecs,
        out_specs=both, out_shape=jax.ShapeDtypeStruct((2, T, FFN), BF16),
        compiler_params=_cparams(("arbitrary", "arbitrary")),
    )(dx2, wg_down, gu, *deps)


def loss_and_grad(y, target):
    blk = pl.BlockSpec((_ROWS, D), lambda i: (i, 0))
    acc = pl.BlockSpec((8, HD), lambda i: (0, 0))

    def body(y_ref, t_ref, dy_ref, l_ref):
        err = y_ref[...] - t_ref[...]
        dy_ref[...] = err * (1.0 / D)

        @pl.when(pl.program_id(0) == 0)
        def _():
            l_ref[...] = jnp.zeros_like(l_ref)

        l_ref[...] += (0.5 / D) * jnp.sum(err * err)

    return pl.pallas_call(
        body, name="loss_and_grad", grid=(T // _ROWS,), in_specs=[blk, blk], out_specs=[blk, acc],
        out_shape=[jax.ShapeDtypeStruct((T, D), F32), jax.ShapeDtypeStruct((8, HD), F32)],
        compiler_params=_cparams(("arbitrary",)),
    )(y, target)


def adamw(w, g, m, v, name):
    rows, cols = w.shape
    tr = _ROWS if rows % _ROWS == 0 else rows
    blk = pl.BlockSpec((tr, cols), lambda i: (i, 0))

    def body(w_ref, g_ref, m_ref, v_ref, d_ref, nm_ref, nv_ref):
        gg = g_ref[...]
        nm = ADAM_B1 * m_ref[...] + (1.0 - ADAM_B1) * gg
        nv = ADAM_B2 * v_ref[...] + (1.0 - ADAM_B2) * (gg * gg)
        m_hat = nm / (1.0 - ADAM_B1 ** ADAM_STEP)
        v_hat = nv / (1.0 - ADAM_B2 ** ADAM_STEP)
        d_ref[...] = -ADAM_LR * (m_hat / (jnp.sqrt(v_hat) + ADAM_EPS) + ADAM_WD * w_ref[...])
        nm_ref[...] = nm
        nv_ref[...] = nv

    return pl.pallas_call(
        body, name=name, grid=(rows // tr,), in_specs=[blk] * 4, out_specs=[blk] * 3,
        out_shape=[jax.ShapeDtypeStruct(w.shape, F32)] * 3, compiler_params=_cparams(("arbitrary",)),
    )(w, g, m, v)


_LANE = 128


def _segment_of_shard_column():
    flat = np.full(P_END, -1, np.int64)
    for o in range(P_END):
        if GATE_COLS <= o < P_CQ:
            continue
        c = o if o < GATE_COLS else o - (P_CQ - GATE_COLS)
        flat[o] = (c // IN_SHARD) * IN_SHARD_PAD + c % IN_SHARD
    return flat


def _block_pairs(src_of_dst):
    return [sorted({int(c) // _LANE for c in src_of_dst[db * _LANE:(db + 1) * _LANE] if c >= 0})
            for db in range(len(src_of_dst) // _LANE)]


_RELAYOUT_ROWS = 512
_SHARD_BLOCKS = IN_SHARD_PAD // _LANE


def _relayout(name, x, to_segments):
    seg_of = _segment_of_shard_column()
    if to_segments:
        src_of_dst = seg_of
    else:
        src_of_dst = np.full(N_CHIPS * IN_SHARD_PAD, -1, np.int64)
        src_of_dst[seg_of[seg_of >= 0]] = np.nonzero(seg_of >= 0)[0]
    sources = _block_pairs(src_of_dst)
    n_dst = len(sources)
    col_map = jnp.asarray(src_of_dst.reshape(n_dst, 1, _LANE), jnp.int32)
    shard_blk = pl.BlockSpec((N_CHIPS, _RELAYOUT_ROWS, IN_SHARD_PAD), lambda i: (0, i, 0))
    seg_blk = pl.BlockSpec((_RELAYOUT_ROWS, P_END), lambda i: (i, 0))

    def shard_cols(ref, b):
        return ref.at[b // _SHARD_BLOCKS, :, pl.ds((b % _SHARD_BLOCKS) * _LANE, _LANE)]

    def seg_cols(ref, b):
        return ref.at[:, pl.ds(b * _LANE, _LANE)]

    src_cols, dst_cols = (shard_cols, seg_cols) if to_segments else (seg_cols, shard_cols)

    def body(x_ref, map_ref, o_ref):
        src_row = _iota((_LANE, _LANE), 0)
        for d in range(n_dst):
            acc = jnp.zeros((_RELAYOUT_ROWS, _LANE), F32)
            for sb in sources[d]:
                sel = (src_row + sb * _LANE == map_ref[d]).astype(x_ref.dtype)
                acc = acc + jnp.dot(src_cols(x_ref, sb)[...], sel, preferred_element_type=F32)
            dst_cols(o_ref, d)[...] = acc.astype(o_ref.dtype)

    rows = x.shape[-2]
    out_shape = (rows, P_END) if to_segments else (N_CHIPS, rows, IN_SHARD_PAD)
    return pl.pallas_call(
        body, name=name, grid=(rows // _RELAYOUT_ROWS,),
        in_specs=[shard_blk if to_segments else seg_blk, pl.BlockSpec(col_map.shape, lambda i: (0, 0, 0))],
        out_specs=seg_blk if to_segments else shard_blk, out_shape=jax.ShapeDtypeStruct(out_shape, x.dtype),
        compiler_params=_cparams(("arbitrary",)),
    )(x, col_map)


def shards_to_segments(w):
    return _relayout("shards_to_segments", w, True)


def segments_to_shards(w):
    return _relayout("segments_to_shards", w, False)


def mixers_forward(x, w_in, sp, dep=None):
    h1 = rmsnorm_fwd(x, sp["norm1_g"], dep)
    p2 = mm_proj(h1, w_in, 0)
    y_a = sgu_fwd(p2, sp["sgu_norm_g"], sp["w_spatial"], sp["b_spatial"])
    y_b, saved_b = gdn_forward(p2, sp["conv_w"], sp["a_log"], sp["dt_bias"], sp["o_norm_g"])
    y_c = dattn_fwd(p2, sp["q_norm_g"], sp["k_norm_g"], _alibi_slopes())
    mix = jnp.concatenate([y_a, y_b, y_c], axis=1)
    return mix, (x, h1, p2, saved_b, mix)


def ffn_forward(x, mix, wg, sp):
    x1 = mm_out(mix, wg["out"], 0, x)
    h2 = rmsnorm_fwd(x1, sp["norm2_g"])
    gu, act = mm_gu_swiglu(h2, wg["gu"])
    x2 = mm_down(act, wg["down"], 0, x1)
    return x2, (x1, h2, gu, act)


def ffn_backward(dx2, wg, sp, saved, dep=None):
    x1, h2, gu, act = saved
    dgu = mm_dact_swiglu(dx2, wg["down"], gu, dep)
    dw_down = mm_dwdown(act, dx2)
    dw_gu = mm_dwgu(h2, dgu)
    dh2 = mm_dh2(dgu, wg["gu"], 0)
    dx1, dnorm2 = rmsnorm_bwd(x1, sp["norm2_g"], dh2, dx2)
    return dx1, dnorm2, dw_gu, dw_down


def mixers_backward(dx1, wg, sp, saved, dep=None):
    x, h1, p2, saved_b, mix = saved
    dmix = mm_dmix(dx1, wg["out"], 0, dep)
    dw_out = mm_dwout(mix, dx1)
    du, dv, dsg, dws, dbs = sgu_bwd(p2, sp["sgu_norm_g"], sp["w_spatial"], sp["b_spatial"], dmix)
    dseg_b, dconv, dal, ddtb, dog = gdn_backward(p2, sp["conv_w"], sp["a_log"], sp["dt_bias"], sp["o_norm_g"], saved_b, dmix)
    dcq, dck, dcv, dqg, dkg = dattn_bwd(p2, sp["q_norm_g"], sp["k_norm_g"], _alibi_slopes(), dmix)
    dp2 = jnp.concatenate([du, dv] + dseg_b + [dcq, dck, dcv], axis=1)
    dw_in = segments_to_shards(mm_dwin(h1, dp2))
    dh1 = mm_dh1(dp2, wg["in"], 0)
    dx, dnorm1 = rmsnorm_bwd(x, sp["norm1_g"], dh1, dx1)
    small = {"norm1_g": dnorm1, "sgu_norm_g": dsg, "w_spatial": dws, "b_spatial": dbs, "conv_w": dconv, "a_log": dal,
             "dt_bias": ddtb, "o_norm_g": dog, "q_norm_g": dqg, "k_norm_g": dkg}
    return dx, dw_in, dw_out, small


def layer_forward(x, wg, sp, dep=None):
    mix, saved_m = mixers_forward(x, wg["in"], sp, dep)
    x2, saved_f = ffn_forward(x, mix, wg, sp)
    return x2, (saved_m, saved_f)


def layer_backward(dx2, wg, sp, saved, dep=None):
    dx1, dnorm2, dw_gu, dw_down = ffn_backward(dx2, wg, sp, saved[1], dep)
    dx, dw_in, dw_out, small = mixers_backward(dx1, wg, sp, saved[0])
    return dx, {"in": dw_in, "out": dw_out, "gu": dw_gu, "down": dw_down}, {**small, "norm2_g": dnorm2}


_HBM = pl.BlockSpec(memory_space=pltpu.HBM)
_MESH = pl.DeviceIdType.MESH


def _place():
    x, y, c = lax.axis_index("x"), lax.axis_index("y"), lax.axis_index("c")
    chips = [(1 - x, y), (x, 1 - y), (1 - x, 1 - y)]
    return x, y, c, chips


def _rcopy(src, dst, ssem, rsem, dev):
    return pltpu.make_async_remote_copy(src_ref=src, dst_ref=dst, send_sem=ssem, recv_sem=rsem, device_id=dev,
                                        device_id_type=_MESH)


def _xor(a, b):
    return a + b - 2 * a * b


def gather_weights(shards):
    n = len(shards)

    def body(*refs):
        ins, outs = refs[:n], refs[n:2 * n]
        s_ici, r_ici, s_d2d, r_d2d, s_own, r_own = refs[2 * n:]
        x, y, c, _ = _place()
        s = 2 * x + y
        sibling = (x, y, 1 - c)
        nbr = [(1 - x, y), (x, 1 - y)]
        src_chip = (_xor(x, 1 - c), _xor(y, c))
        dst_chip = (_xor(x, c), _xor(y, 1 - c))
        t_src = 2 * src_chip[0] + src_chip[1]
        t_oth = 2 * dst_chip[0] + dst_chip[1]
        t_dia = 2 * (1 - x) + (1 - y)
        sends = []
        for a in range(n):
            for u in range(2):
                sends.append(_rcopy(ins[a].at[u], outs[a].at[s, u], s_own.at[a, u], r_own.at[a, u], sibling))
            for k in range(2):
                sends.append(_rcopy(ins[a].at[c], outs[a].at[s, c], s_ici.at[a, k], r_ici.at[a, k], (*nbr[k], c)))
        for cp in sends:
            cp.start()

        def landed(a, t, k):
            _rcopy(ins[a].at[c], outs[a].at[t, c], s_ici.at[a, k], r_ici.at[a, k], sibling).wait_recv()
            cp = _rcopy(outs[a].at[t, c], outs[a].at[t, c], s_d2d.at[a, k], r_d2d.at[a, k], sibling)
            cp.start()
            sends.append(cp)

        for a in range(n):
            landed(a, t_src, c)
            fwd = _rcopy(outs[a].at[t_src, c], outs[a].at[t_src, c], s_ici.at[a, 2], r_ici.at[a, 2], (*dst_chip, c))
            fwd.start()
            sends.append(fwd)
        for a in range(n):
            landed(a, t_oth, 1 - c)
        for a in range(n):
            landed(a, t_dia, 2)
        for a in range(n):
            for u in range(2):
                _rcopy(ins[a].at[u], outs[a].at[s, u], s_own.at[a, u], r_own.at[a, u], sibling).wait_recv()
            for k, t in enumerate([2 * nbr[0][0] + nbr[0][1], 2 * nbr[1][0] + nbr[1][1], t_dia]):
                _rcopy(ins[a].at[1 - c], outs[a].at[t, 1 - c], s_d2d.at[a, k], r_d2d.at[a, k], sibling).wait_recv()
        for cp in sends:
            cp.wait_send()

    dma = lambda k: pltpu.SemaphoreType.DMA((n, k))
    return pl.pallas_call(
        body, name="gather_weights", in_specs=[_HBM] * n, out_specs=[_HBM] * n,
        out_shape=[jax.ShapeDtypeStruct((N_CHIPS,) + w.shape, w.dtype) for w in shards],
        scratch_shapes=[dma(3), dma(3), dma(3), dma(3), dma(2), dma(2)],
    )(*shards)


_SEM = pl.BlockSpec(memory_space=pltpu.SEMAPHORE)
_SIDE_EFFECT = pltpu.SideEffectType.DATAFLOW_SIDE_EFFECTING


def _in_hbm(a):
    return pltpu.with_memory_space_constraint(a, pltpu.HBM)


def _split_copy(name, srcs, land_shapes, n_sems, copies):
    n, m = len(srcs), len(land_shapes)
    thru = [pltpu.HBM(a.shape, a.dtype) for a in srcs] + [pltpu.HBM(s.shape, s.dtype) for s in land_shapes]
    sems = (pltpu.SemaphoreType.DMA((n_sems,)), pltpu.SemaphoreType.DMA((n_sems,)))

    def start(dep=None):
        deps = [] if dep is None else [dep]

        def body(*refs):
            ins, lands = refs[:n], refs[n:n + m]
            ssem, rsem, token = refs[n + m + len(deps)], refs[n + m + len(deps) + 1], refs[-1]
            for cp in copies(ins, lands, ssem, rsem)[0]:
                cp.start()
            token[...] = jnp.zeros_like(token)

        out = pl.pallas_call(
            body, name=name + "_start", out_shape=(*sems, *thru, jax.ShapeDtypeStruct((8, HD), F32)),
            in_specs=[_HBM] * (n + m) + [pl.BlockSpec(memory_space=pl.ANY)] * len(deps),
            out_specs=(_SEM, _SEM, *[_HBM] * (n + m), pl.BlockSpec(memory_space=pltpu.VMEM)),
            input_output_aliases={i: 2 + i for i in range(n + m)},
            compiler_params=pltpu.CompilerParams(has_side_effects=_SIDE_EFFECT),
        )(*[_in_hbm(a) for a in srcs], *[_in_hbm(lax.empty(s.shape, s.dtype)) for s in land_shapes], *deps)
        return out[:-1], out[-1]

    def wait(state, after):
        def body(*refs):
            ins, lands, ssem, rsem = refs[:n], refs[n:n + m], refs[n + m], refs[n + m + 1]
            sent, arrivals = copies(ins, lands, ssem, rsem)
            for cp in sent:
                cp.wait_send()
            for cp in arrivals:
                cp.wait_recv()

        out = pl.pallas_call(
            body, name=name + "_wait", out_shape=tuple(thru),
            in_specs=[_HBM] * (n + m) + [_SEM, _SEM, pl.BlockSpec(memory_space=pl.ANY)], out_specs=[_HBM] * (n + m),
            input_output_aliases={i: i for i in range(n + m)},
            compiler_params=pltpu.CompilerParams(has_side_effects=_SIDE_EFFECT),
        )(*state[2:], state[0], state[1], after)
        return out[n:]

    return start, wait


def gather_direct(shards, tag):
    n = len(shards)

    def copies(ins, lands, ssem, rsem):
        x, y, c, chips = _place()
        s = 2 * x + y
        sibling = (x, y, 1 - c)
        sent, arrivals = [], []
        for a in range(n):
            for u in range(2):
                cp = _rcopy(ins[a].at[u], lands[a].at[s, u], ssem.at[5 * a + u], rsem.at[5 * a + u], sibling)
                sent.append(cp)
                arrivals.append(cp)
            for j, (cx, cy) in enumerate(chips):
                k = 5 * a + 2 + j
                sent.append(_rcopy(ins[a].at[c], lands[a].at[s, c], ssem.at[k], rsem.at[k], (cx, cy, c)))
                arrivals.append(_rcopy(ins[a].at[c], lands[a].at[2 * cx + cy, c], ssem.at[k], rsem.at[k], (cx, cy, c)))
        return sent, arrivals

    lands = [jax.ShapeDtypeStruct((N_CHIPS,) + w.shape, w.dtype) for w in shards]
    return _split_copy("gather_direct_" + tag, shards, lands, 5 * n, copies)


def pass_to_sibling(lands):
    n = len(lands)

    def body(*refs):
        ins = refs[:n]
        ssem, rsem = refs[2 * n:]
        x, y, c, chips = _place()
        sibling = (x, y, 1 - c)
        cps, arrivals = [], []
        for a in range(n):
            for j, (cx, cy) in enumerate(chips):
                t = 2 * cx + cy
                cps.append(_rcopy(ins[a].at[t, c], ins[a].at[t, c], ssem.at[a, j], rsem.at[a, j], sibling))
                arrivals.append(_rcopy(ins[a].at[t, c], ins[a].at[t, 1 - c], ssem.at[a, j], rsem.at[a, j], sibling))
        for cp in cps:
            cp.start()
        for cp, ar in zip(cps, arrivals):
            cp.wait_send()
            ar.wait_recv()

    return pl.pallas_call(
        body, name="pass_to_sibling", in_specs=[_HBM] * n, out_specs=[_HBM] * n,
        out_shape=[jax.ShapeDtypeStruct(a.shape, a.dtype) for a in lands], input_output_aliases={a: a for a in range(n)},
        scratch_shapes=[pltpu.SemaphoreType.DMA((n, 3)), pltpu.SemaphoreType.DMA((n, 3))],
    )(*lands)


def exchange_halves(grads):
    n = len(grads)

    def body(*refs):
        ins, outs = refs[:n], refs[n:2 * n]
        ssem, rsem = refs[2 * n:]
        x, y, c, _ = _place()
        cps = []
        for a in range(n):
            h = grads[a].shape[1] // 2
            cps.append(_rcopy(ins[a].at[:, pl.ds((1 - c) * h, h)], outs[a], ssem.at[a], rsem.at[a], (x, y, 1 - c)))
        for cp in cps:
            cp.start()
        for cp in cps:
            cp.wait()

    return pl.pallas_call(
        body, name="exchange_halves", in_specs=[_HBM] * n, out_specs=[_HBM] * n,
        out_shape=[jax.ShapeDtypeStruct((g.shape[0], g.shape[1] // 2, g.shape[2]), g.dtype) for g in grads],
        scratch_shapes=[pltpu.SemaphoreType.DMA((n,)), pltpu.SemaphoreType.DMA((n,))],
    )(*grads)


def scatter_to_chips(parts):
    n = len(parts)

    def body(*refs):
        ins, outs = refs[:n], refs[n:2 * n]
        ssem, rsem = refs[2 * n:]
        x, y, c, chips = _place()
        cps = [_rcopy(ins[a].at[2 * cx + cy], outs[a].at[j], ssem.at[a, j], rsem.at[a, j], (cx, cy, c))
               for a in range(n) for j, (cx, cy) in enumerate(chips)]
        for cp in cps:
            cp.start()
        for cp in cps:
            cp.wait()

    return pl.pallas_call(
        body, name="scatter_to_chips", in_specs=[_HBM] * n, out_specs=[_HBM] * n,
        out_shape=[jax.ShapeDtypeStruct((3,) + p.shape[1:], p.dtype) for p in parts],
        scratch_shapes=[pltpu.SemaphoreType.DMA((n, 3)), pltpu.SemaphoreType.DMA((n, 3))],
    )(*parts)


def scatter_direct(parts, tag):
    n = len(parts)

    def copies(ins, lands, ssem, rsem):
        x, y, c, chips = _place()
        cps = [_rcopy(ins[a].at[2 * cx + cy], lands[a].at[j], ssem.at[3 * a + j], rsem.at[3 * a + j], (cx, cy, c))
               for a in range(n) for j, (cx, cy) in enumerate(chips)]
        return cps, cps

    lands = [jax.ShapeDtypeStruct((3,) + p.shape[1:], p.dtype) for p in parts]
    return _split_copy("scatter_direct_" + tag, parts, lands, 3 * n, copies)


def share_halves(halves):
    n = len(halves)

    def body(*refs):
        ins, outs = refs[:n], refs[n:2 * n]
        ssem, rsem = refs[2 * n:]
        x, y, c, _ = _place()
        cps = [_rcopy(ins[i], outs[i], ssem.at[i], rsem.at[i], (x, y, 1 - c)) for i in range(n)]
        for cp in cps:
            cp.start()
        for cp in cps:
            cp.wait()

    return pl.pallas_call(
        body, name="share_halves", in_specs=[_HBM] * n, out_specs=[_HBM] * n,
        out_shape=[jax.ShapeDtypeStruct(h.shape, h.dtype) for h in halves],
        scratch_shapes=[pltpu.SemaphoreType.DMA((n,)), pltpu.SemaphoreType.DMA((n,))],
    )(*halves)


def adamw_shard(w, m, v, mine, theirs, c, name):
    _, r, cw = w.shape
    h, cg = mine[0].shape
    tr = next(t for t in (256, 176, 128) if h % t == 0 and t * cg * 4 <= (3 << 19))
    nb = h // tr
    wblk = pl.BlockSpec((None, tr, cw), lambda l, i, c_ref: (l, i, 0))
    gblk = lambda layer, own: pl.BlockSpec((tr, cg), lambda l, i, c_ref: (_held_block(l, i, c_ref, layer, own, nb), 0))
    return _adamw_halves(w, m, v, mine, theirs, c, name, (DEPTH, r // tr), wblk, gblk, nb, cw)


def _held_block(l, i, c_ref, layer, own, nb):
    in_use = (l == layer) & (((i // nb) == c_ref[0]) == own)
    return jnp.where(in_use, i % nb, 0)


def _adamw_halves(w, m, v, mine, theirs, c, name, grid, wblk, gblk, nb, cw):
    def body(c_ref, w_ref, m_ref, v_ref, m0, m1, t0, t1, g_ref, d_ref, nm_ref, nv_ref):
        is_mine = (pl.program_id(1) // nb) == c_ref[0]
        first = pl.program_id(0) == 0
        gg = jnp.where(is_mine, jnp.where(first, m0[:, :cw], m1[:, :cw]), jnp.where(first, t0[:, :cw], t1[:, :cw]))
        nm = ADAM_B1 * m_ref[...] + (1.0 - ADAM_B1) * gg
        nv = ADAM_B2 * v_ref[...] + (1.0 - ADAM_B2) * (gg * gg)
        m_hat = nm / (1.0 - ADAM_B1 ** ADAM_STEP)
        v_hat = nv / (1.0 - ADAM_B2 ** ADAM_STEP)
        g_ref[...] = gg
        d_ref[...] = -ADAM_LR * (m_hat / (jnp.sqrt(v_hat) + ADAM_EPS) + ADAM_WD * w_ref[...])
        nm_ref[...] = nm
        nv_ref[...] = nv

    return pl.pallas_call(
        body, name=name,
        grid_spec=pltpu.PrefetchScalarGridSpec(
            num_scalar_prefetch=1, grid=grid,
            in_specs=[wblk] * 3 + [gblk(0, True), gblk(1, True), gblk(0, False), gblk(1, False)], out_specs=[wblk] * 4),
        out_shape=[jax.ShapeDtypeStruct(w.shape, F32)] * 4, compiler_params=_cparams(("arbitrary", "arbitrary")),
    )(c, w, m, v, mine[0], mine[1], theirs[0], theirs[1])


def adamw_shard_t(wt, mt, vt, mine_t, theirs_t, c, name):
    _, cw, r = wt.shape
    h = mine_t[0].shape[1]
    tc = 256
    nb = h // tc
    wblk = pl.BlockSpec((None, cw, tc), lambda l, j, c_ref: (l, 0, j))
    gblk = lambda layer, own: pl.BlockSpec((cw, tc), lambda l, j, c_ref: (0, _held_block(l, j, c_ref, layer, own, nb)))
    return _adamw_halves(wt, mt, vt, mine_t, theirs_t, c, name, (DEPTH, r // tc), wblk, gblk, nb, cw)


def _half_rows(h, cols):
    for tr in (512, 256, 352, 128, 64):
        if h % tr == 0 and tr * cols * 4 <= 6 * 1024 * 1024:
            return tr
    raise ValueError((h, cols))


def add_sibling(grad, recv, c):
    _, r, cols = grad.shape
    h = r // 2
    tr = _half_rows(h, cols)
    nb = h // tr

    def body(c_ref, g_ref, r_ref, o_ref):
        o_ref[...] = (g_ref[...].astype(F32) + r_ref[...].astype(F32)).astype(BF16)

    return pl.pallas_call(
        body, name="add_sibling",
        grid_spec=pltpu.PrefetchScalarGridSpec(
            num_scalar_prefetch=1, grid=(N_CHIPS, nb),
            in_specs=[pl.BlockSpec((None, tr, cols), lambda t, i, c_ref: (t, c_ref[0] * nb + i, 0)),
                      pl.BlockSpec((None, tr, cols), lambda t, i, c_ref: (t, i, 0))],
            out_specs=pl.BlockSpec((None, tr, cols), lambda t, i, c_ref: (t, i, 0))),
        out_shape=jax.ShapeDtypeStruct((N_CHIPS, h, cols), BF16), compiler_params=_cparams(("arbitrary", "arbitrary")),
    )(c, grad, recv)


def add_chips(part, recv, s):
    _, h, cols = part.shape
    tr = _half_rows(h, cols)

    def body(s_ref, p_ref, r_ref, o_ref):
        o_ref[...] = ((p_ref[...].astype(F32) + r_ref[0].astype(F32)) + r_ref[1].astype(F32)) + r_ref[2].astype(F32)

    return pl.pallas_call(
        body, name="add_chips",
        grid_spec=pltpu.PrefetchScalarGridSpec(
            num_scalar_prefetch=1, grid=(h // tr,),
            in_specs=[pl.BlockSpec((None, tr, cols), lambda i, s_ref: (s_ref[0], i, 0)),
                      pl.BlockSpec((3, tr, cols), lambda i, s_ref: (0, i, 0))],
            out_specs=pl.BlockSpec((tr, cols), lambda i, s_ref: (i, 0))),
        out_shape=jax.ShapeDtypeStruct((h, cols), F32), compiler_params=_cparams(("arbitrary",)),
    )(s, part, recv)


def allreduce_small(vec):
    rows = vec.shape[0]

    def body(v_ref, o_ref, buf, ssem, rsem, lsem):
        x, y, c, chips = _place()
        me, sibling = (x, y, c), (x, y, 1 - c)

        def blk(px, py, pc):
            return buf.at[4 * px + 2 * py + pc]

        def copy(k, block, to, src=None):
            return _rcopy(blk(*block) if src is None else src, blk(*block), ssem.at[k], rsem.at[k], to)

        mine = pltpu.make_async_copy(v_ref, blk(*me), lsem)
        mine.start()
        first = [copy(0, me, sibling, src=v_ref)] + [copy(1 + j, me, (*chip, c), src=v_ref) for j, chip in enumerate(chips)]
        for cp in first:
            cp.start()
        passed = [copy(4 + j, (*chip, c), sibling) for j, chip in enumerate(chips)]
        for j, chip in enumerate(chips):
            copy(1 + j, (*chip, c), me).wait_recv()
            passed[j].start()
        copy(0, sibling, me).wait_recv()
        for j, chip in enumerate(chips):
            copy(4 + j, (*chip, 1 - c), me).wait_recv()
        for cp in first + passed:
            cp.wait_send()
        mine.wait()
        acc = buf[0]
        for d in range(1, N_DEV):
            acc = acc + buf[d]
        o_ref[...] = acc

    vm = pl.BlockSpec(memory_space=pltpu.VMEM)
    return pl.pallas_call(
        body, name="allreduce_small", in_specs=[vm], out_specs=vm, out_shape=jax.ShapeDtypeStruct(vec.shape, F32),
        scratch_shapes=[pltpu.VMEM((N_DEV, rows, HD), F32), pltpu.SemaphoreType.DMA((7,)), pltpu.SemaphoreType.DMA((7,)),
                        pltpu.SemaphoreType.DMA],
        compiler_params=pltpu.CompilerParams(vmem_limit_bytes=VMEM_LIMIT),
    )(vec)


SMALL_NAMES = ("norm1_g", "sgu_norm_g", "w_spatial", "b_spatial", "conv_w", "a_log", "dt_bias", "o_norm_g", "q_norm_g",
               "k_norm_g", "norm2_g")


def small_params(l, p, conv_full):
    return {"norm1_g": p["norm1_g"][l][None], "sgu_norm_g": p["sgu_norm_g"][l][:, None, :], "w_spatial": p["w_spatial"][l],
            "b_spatial": p["b_spatial"][l][..., None], "conv_w": conv_full[l], "a_log": p["a_log"][l], "dt_bias": p["dt_bias"][l],
            "o_norm_g": p["o_norm_g"][l][None], "q_norm_g": p["q_norm_g"][l][None], "k_norm_g": p["k_norm_g"][l][None],
            "norm2_g": p["norm2_g"][l][None]}


def local_step(x, target, wg, sps):
    saved = []
    for l in range(DEPTH):
        x, s = layer_forward(x, wg[l], sps[l])
        saved.append(s)
    dx, loss = loss_and_grad(x, target)
    bigs, smalls = [None] * DEPTH, [None] * DEPTH
    for l in reversed(range(DEPTH)):
        dx, bigs[l], smalls[l] = layer_backward(dx, wg[l], sps[l], saved[l])
    return loss, dx, bigs, smalls


_PACK_TILE = 8 * HD


def _pack(arrays):
    flat = jnp.concatenate([a.reshape(-1) for a in arrays])
    pad = -flat.shape[0] % _PACK_TILE
    return jnp.pad(flat, (0, pad)).reshape(-1, HD)


def _unpack(packed, shapes):
    flat, out, off = packed.reshape(-1), [], 0
    for shp in shapes:
        n = int(np.prod(shp))
        out.append(flat[off:off + n].reshape(shp))
        off += n
    return out


BIG_NAMES = ("in", "out", "gu", "down")
WEIGHT_ORDER = ("norm1_g", "w_in", "sgu_norm_g", "w_spatial", "b_spatial", "conv_w", "a_log", "dt_bias", "o_norm_g", "q_norm_g",
                "k_norm_g", "w_out", "norm2_g", "w_gate_up", "w_down")


def kernel(x, norm1_g, w_in, sgu_norm_g, w_spatial, b_spatial, conv_w, a_log, dt_bias, o_norm_g, q_norm_g, k_norm_g, w_out, norm2_g, w_gate_up, w_down, loss_target, m_norm1_g, m_w_in, m_sgu_norm_g, m_w_spatial, m_b_spatial, m_conv_w, m_a_log, m_dt_bias, m_o_norm_g, m_q_norm_g, m_k_norm_g, m_w_out, m_norm2_g, m_w_gate_up, m_w_down, v_norm1_g, v_w_in, v_sgu_norm_g, v_w_spatial, v_b_spatial, v_conv_w, v_a_log, v_dt_bias, v_o_norm_g, v_q_norm_g, v_k_norm_g, v_w_out, v_norm2_g, v_w_gate_up, v_w_down):
    w = dict(norm1_g=norm1_g, w_in=w_in, sgu_norm_g=sgu_norm_g, w_spatial=w_spatial, b_spatial=b_spatial, conv_w=conv_w,
             a_log=a_log, dt_bias=dt_bias, o_norm_g=o_norm_g, q_norm_g=q_norm_g, k_norm_g=k_norm_g, w_out=w_out,
             norm2_g=norm2_g, w_gate_up=w_gate_up, w_down=w_down)
    m = dict(norm1_g=m_norm1_g, w_in=m_w_in, sgu_norm_g=m_sgu_norm_g, w_spatial=m_w_spatial, b_spatial=m_b_spatial,
             conv_w=m_conv_w, a_log=m_a_log, dt_bias=m_dt_bias, o_norm_g=m_o_norm_g, q_norm_g=m_q_norm_g, k_norm_g=m_k_norm_g,
             w_out=m_w_out, norm2_g=m_norm2_g, w_gate_up=m_w_gate_up, w_down=m_w_down)
    v = dict(norm1_g=v_norm1_g, w_in=v_w_in, sgu_norm_g=v_sgu_norm_g, w_spatial=v_w_spatial, b_spatial=v_b_spatial,
             conv_w=v_conv_w, a_log=v_a_log, dt_bias=v_dt_bias, o_norm_g=v_o_norm_g, q_norm_g=v_q_norm_g, k_norm_g=v_k_norm_g,
             w_out=v_w_out, norm2_g=v_norm2_g, w_gate_up=v_w_gate_up, w_down=v_w_down)
    chip = (2 * lax.axis_index("x") + lax.axis_index("y")).astype(jnp.int32)
    core = lax.axis_index("c").astype(jnp.int32)

    in_pad = IN_SHARD_PAD - IN_SHARD
    w_in_pad = jnp.pad(w_in, ((0, 0), (0, 0), (0, in_pad)))

    halves_of = lambda a: a.reshape(2, a.shape[0] // 2, a.shape[1])
    bf_halves = lambda a: halves_of(a.astype(BF16))

    def ffn_shards(l):
        return [bf_halves(w_gate_up[l]), bf_halves(w_down[l]), bf_halves(w_out[l])]

    def mixer_shards(l):
        return [bf_halves(w_in_pad[l]), halves_of(conv_w[l])]

    def mixer_weights(g):
        g_in, g_conv = g
        return (shards_to_segments(g_in.reshape(N_CHIPS, D, IN_SHARD_PAD))[None],
                g_conv.reshape(N_CHIPS, B_CONV, -1).transpose(1, 0, 2).reshape(B_CONV, 3 * B_WIDTH))

    def ffn_weights(g, w_in_seg):
        g_gu, g_down, g_out = g
        return {"in": w_in_seg, "out": g_out.reshape(1, D, D), "gu": g_gu.reshape(1, N_CHIPS, D, GU_SHARD),
                "down": g_down.reshape(1, FFN, D)}

    def layer_params(l, conv_full):
        return small_params(0, {n: w[n][l:l + 1] for n in SMALL_NAMES if n != "conv_w"}, conv_full[None])

    w_in0, conv0 = mixer_weights(gather_weights(mixer_shards(0)))
    start_a, wait_a = gather_direct(ffn_shards(0), "ffn0")
    start_b, wait_b = gather_direct(mixer_shards(1), "mix1")
    start_c, wait_c = gather_direct(ffn_shards(1), "ffn1")
    state_a, token_a = start_a()
    state_b, token_b = start_b(token_a)
    state_c, token_c = start_c(token_b)
    sps = [layer_params(0, conv0), None]
    mix0, saved_m0 = mixers_forward(x[0], w_in0, sps[0], dep=token_c)
    wg0 = ffn_weights(pass_to_sibling(wait_a(state_a, mix0)), w_in0)
    x1, saved_f0 = ffn_forward(x[0], mix0, wg0, sps[0])
    w_in1, conv1 = mixer_weights(pass_to_sibling(wait_b(state_b, x1)))
    sps[1] = layer_params(1, conv1)
    mix1, saved_m1 = mixers_forward(x1, w_in1, sps[1])
    wg1 = ffn_weights(pass_to_sibling(wait_c(state_c, mix1)), w_in1)
    x2, saved_f1 = ffn_forward(x1, mix1, wg1, sps[1])
    saved1 = (saved_m1, saved_f1)
    dx, loss_tile = loss_and_grad(x2, loss_target[0])

    def to_chip_parts(grads):
        return [add_sibling(g, r, core.reshape(1)) for g, r in zip(grads, exchange_halves(grads))]

    def start_scatter(grads, tag, dep=None):
        parts = to_chip_parts(grads)
        start, wait = scatter_direct(parts, tag)
        state, token = start(dep)
        return parts, wait, state, token

    smalls = [None] * DEPTH
    by_chip_out = lambda t: t.reshape(N_CHIPS, OUT_SHARD, D)
    by_chip_down = lambda t: t.reshape(N_CHIPS, DOWN_SHARD, D)
    dx1, dnorm2_1, dw_gu1, dw_down1 = ffn_backward(dx, wg1, sps[1], saved1[1])
    pf1, wait_f1, st_f1, tok_f1 = start_scatter([dw_gu1, by_chip_down(dw_down1)], "ffn1")
    dx, dw_in1, dw_out1, small1 = mixers_backward(dx1, wg1, sps[1], saved1[0], dep=tok_f1)
    smalls[1] = {**small1, "norm2_g": dnorm2_1}
    pm1, wait_m1, st_m1, tok_m1 = start_scatter([dw_in1, by_chip_out(dw_out1)], "mix1")
    dx1, dnorm2_0, dw_gu0, dw_down0 = ffn_backward(dx, wg0, sps[0], saved_f0, dep=tok_m1)
    pf0, wait_f0, st_f0, tok_f0 = start_scatter([dw_gu0, by_chip_down(dw_down0)], "ffn0")
    dx, dw_in0, dw_out0, small0 = mixers_backward(dx1, wg0, sps[0], saved_m0, dep=tok_f0)
    smalls[0] = {**small0, "norm2_g": dnorm2_0}

    grad, delta, new_m, new_v = {}, {}, {}, {}
    stacked = [jnp.stack([smalls[l][n] for l in range(DEPTH)]) for n in SMALL_NAMES]
    total = allreduce_small(_pack(stacked + [loss_tile[0, :1]]))
    shapes = [(DEPTH, B_CONV, 3 * B_WIDTH) if n == "conv_w" else w[n].shape for n in SMALL_NAMES]
    small_grads = dict(zip(SMALL_NAMES, _unpack(total, shapes + [(1,)])[:-1]))
    loss = _unpack(total, shapes + [(1,)])[-1][0]
    conv_cols = conv_w.shape[-1]
    small_grads["conv_w"] = lax.dynamic_slice_in_dim(small_grads["conv_w"], chip * conv_cols, conv_cols, axis=2)
    grad.update(small_grads)
    sshapes = [w[n].shape for n in SMALL_NAMES]
    packed = [_pack([d[n] for n in SMALL_NAMES]) for d in (w, grad, m, v)]
    for dst, t in zip((delta, new_m, new_v), adamw(*packed, "adamw_small")):
        dst.update(zip(SMALL_NAMES, _unpack(t, sshapes)))

    pm0, wait_m0, st_m0, tok_m0 = start_scatter([dw_in0, by_chip_out(dw_out0)], "mix0", dep=total)
    rf0, rm1, rf1 = (list(wt(st, tok_m0)) for wt, st in ((wait_f0, st_f0), (wait_m1, st_m1), (wait_f1, st_f1)))

    def reduce_group(parts, from_chips):
        mine = [add_chips(p, r, chip.reshape(1)) for p, r in zip(parts, from_chips)]
        return mine, list(share_halves(mine))

    mine_f, theirs_f = reduce_group(pf0 + pf1 + pm1, rf0 + rf1 + rm1)
    for a, n in enumerate(("w_gate_up", "w_down")):
        grad[n], delta[n], new_m[n], new_v[n] = adamw_shard(w[n], m[n], v[n], [mine_f[a], mine_f[2 + a]],
                                                            [theirs_f[a], theirs_f[2 + a]], core.reshape(1), "adamw_" + n)

    mine_m0, theirs_m0 = reduce_group(pm0, list(wait_m0(st_m0, new_v["w_down"])))
    tr_ = lambda t: jnp.swapaxes(t, -1, -2)
    cut = lambda t: tr_(t[:, :IN_SHARD])
    res = adamw_shard_t(tr_(w_in), tr_(m_w_in), tr_(v_w_in), [cut(mine_m0[0]), cut(mine_f[4])],
                        [cut(theirs_m0[0]), cut(theirs_f[4])], core.reshape(1), "adamw_w_in")
    grad["w_in"], delta["w_in"], new_m["w_in"], new_v["w_in"] = (tr_(t) for t in res)
    grad["w_out"], delta["w_out"], new_m["w_out"], new_v["w_out"] = adamw_shard(
        w_out, m_w_out, v_w_out, [mine_m0[1], mine_f[5]], [theirs_m0[1], theirs_f[5]], core.reshape(1), "adamw_w_out")

    out = [loss, dx[None]]
    for d in (grad, delta, new_m, new_v):
        out += [d[n] for n in WEIGHT_ORDER]
    return tuple(out)
```

```python
import functools
import math

import numpy as np
import jax
import jax.numpy as jnp
from jax import lax
from jax.experimental import pallas as pl
from jax.experimental.pallas import tpu as pltpu

F32 = jnp.float32
BF16 = jnp.bfloat16
HI = lax.Precision.HIGH

T = 2048
D = 2048
DEPTH = 2
HD = 128
A_GROUPS, A_WIDTH, A_CHUNK = 4, 512, 128
B_HEADS, B_WIDTH, B_CONV, B_CHUNK = 6, 768, 4, 64
C_HEADS, C_WIDTH, C_BLOCK = 6, 768, 128
C_BRANCHES = ((128, 1), (512, 4), (2048, 16))
FFN = 5632
IN_TOTAL = 6412
EPS = 1e-6
N_CHIPS = 4
N_DEV = 8
IN_SHARD = IN_TOTAL // N_CHIPS
IN_SHARD_PAD = 1664
GU_SHARD = 2 * FFN // N_CHIPS
OUT_SHARD = D // N_CHIPS
DOWN_SHARD = FFN // N_CHIPS
P_AU, P_AV, P_BQ, P_BK, P_BV, P_BG, P_BB, P_CQ, P_CK, P_CV, P_END = (
    0, 512, 1024, 1792, 2560, 3328, 4096, 4224, 4992, 5760, 6528)
GATE_COLS = 4108
VMEM_LIMIT = 56 * 1024 * 1024

ADAM_LR, ADAM_B1, ADAM_B2, ADAM_EPS, ADAM_WD, ADAM_STEP = 0.001, 0.9, 0.999, 1e-08, 0.01, 10


def _cparams(sem, vmem=VMEM_LIMIT):
    return pltpu.CompilerParams(dimension_semantics=sem, vmem_limit_bytes=vmem)


def _dims(nd, ta, tb):
    off = nd - 2
    ca = off + (0 if ta else 1)
    cb = off + (1 if tb else 0)
    batch = ((0,), (0,)) if nd == 3 else ((), ())
    return (((ca,), (cb,)), batch)


def _raw_mm(a, b, ta, tb, hi):
    if hi:
        return lax.dot_general(a, b, _dims(a.ndim, ta, tb), precision=HI, preferred_element_type=F32)
    return lax.dot_general(a.astype(BF16), b.astype(BF16), _dims(a.ndim, ta, tb), preferred_element_type=F32)


@functools.partial(jax.custom_vjp, nondiff_argnums=(2, 3, 4))
def _mm(a, b, ta=False, tb=False, hi=False):
    return _raw_mm(a, b, ta, tb, hi)


def _mm_fwd(a, b, ta, tb, hi):
    return _raw_mm(a, b, ta, tb, hi), (a, b)


def _mm_bwd(ta, tb, hi, res, g):
    a, b = res
    da = _raw_mm(g, b, False, not tb, hi) if not ta else _raw_mm(b, g, tb, True, hi)
    db = _raw_mm(a, g, not ta, False, hi) if not tb else _raw_mm(g, a, True, ta, hi)
    return da.astype(a.dtype), db.astype(b.dtype)


_mm.defvjp(_mm_fwd, _mm_bwd)


def _rms(x, g):
    return x * lax.rsqrt(jnp.mean(x * x, axis=-1, keepdims=True) + EPS) * g


def _gelu(x):
    return 0.5 * x * (1.0 + jnp.tanh(math.sqrt(2.0 / math.pi) * (x + 0.044715 * (x * x * x))))


def _sigmoid(x):
    return 1.0 / (1.0 + jnp.exp(-x))


def _silu(x):
    return x * _sigmoid(x)


def _softplus(x):
    return jnp.maximum(x, 0.0) + jnp.log(1.0 + jnp.exp(-jnp.abs(x)))


def _iota(shape, dim):
    return lax.broadcasted_iota(jnp.int32, shape, dim)


def _sgu_fn(u, v, sg, w, b):
    nc = T // A_CHUNK
    ug = _gelu(u)
    vn = _rms(_gelu(v), sg)
    causal = _iota((A_CHUNK, A_CHUNK), 0) >= _iota((A_CHUNK, A_CHUNK), 1)
    wm = jnp.where(causal, w, 0.0)
    wb = jnp.broadcast_to(wm[None], (nc, A_CHUNK, A_CHUNK))
    z = _mm(wb, vn.reshape(nc, A_CHUNK, HD)) + b[None]
    return ug * z.reshape(T, HD)


def _sgu_specs():
    col = lambda off: pl.BlockSpec((T, HD), lambda g, off=off: (0, off + g))
    par = [pl.BlockSpec((None, 1, HD), lambda g: (g, 0, 0)),
           pl.BlockSpec((None, A_CHUNK, A_CHUNK), lambda g: (g, 0, 0)),
           pl.BlockSpec((None, A_CHUNK, 1), lambda g: (g, 0, 0))]
    return col, par


def sgu_fwd(p2, sg, w, b):
    col, par = _sgu_specs()

    def body(u_ref, v_ref, sg_ref, w_ref, b_ref, y_ref):
        y_ref[...] = _sgu_fn(u_ref[...], v_ref[...], sg_ref[...], w_ref[...], b_ref[...]).astype(BF16)

    return pl.pallas_call(
        body, name="sgu_fwd", grid=(A_GROUPS,),
        in_specs=[col(P_AU // HD), col(P_AV // HD)] + par,
        out_specs=pl.BlockSpec((T, HD), lambda g: (0, g)),
        out_shape=jax.ShapeDtypeStruct((T, A_WIDTH), BF16),
        compiler_params=_cparams(("arbitrary",)),
    )(p2, p2, sg, w, b)


def sgu_bwd(p2, sg, w, b, dmix):
    col, par = _sgu_specs()

    def body(u_ref, v_ref, sg_ref, w_ref, b_ref, dy_ref, du_ref, dv_ref, dsg_ref, dw_ref, db_ref):
        _, vjp = jax.vjp(_sgu_fn, u_ref[...], v_ref[...], sg_ref[...], w_ref[...], b_ref[...])
        du, dv, dsg, dw, db = vjp(dy_ref[...])
        du_ref[...] = du.astype(BF16)
        dv_ref[...] = dv.astype(BF16)
        dsg_ref[...] = dsg
        dw_ref[...] = dw
        db_ref[...] = db

    gcol = pl.BlockSpec((T, HD), lambda g: (0, g))
    return pl.pallas_call(
        body, name="sgu_bwd", grid=(A_GROUPS,),
        in_specs=[col(P_AU // HD), col(P_AV // HD)] + par + [gcol],
        out_specs=[gcol, gcol] + par,
        out_shape=[jax.ShapeDtypeStruct((T, A_WIDTH), BF16), jax.ShapeDtypeStruct((T, A_WIDTH), BF16),
                   jax.ShapeDtypeStruct((A_GROUPS, 1, HD), F32), jax.ShapeDtypeStruct((A_GROUPS, A_CHUNK, A_CHUNK), F32),
                   jax.ShapeDtypeStruct((A_GROUPS, A_CHUNK, 1), F32)],
        compiler_params=_cparams(("arbitrary",)),
    )(p2, p2, sg, w, b, dmix)


def _attn_fn(q, k, v, qg, kg, slope, *, dil, nb):
    n = T // C_BLOCK
    qb = _rms(q, qg).reshape(n, C_BLOCK, HD)
    kb = _rms(k, kg).reshape(n, C_BLOCK, HD)
    vb = v.reshape(n, C_BLOCK, HD)
    scale = HD ** -0.5
    qi = _iota((n, C_BLOCK, C_BLOCK), 1)
    kj = _iota((n, C_BLOCK, C_BLOCK), 2)
    sl = slope[None] * float(dil)
    d_cur = qi - kj
    sc = jnp.where(d_cur >= 0, _mm(qb, kb, tb=True) * scale - sl * d_cur.astype(F32), -jnp.inf)
    mx = jnp.max(sc, axis=-1, keepdims=True)
    if nb > 1:
        kp = jnp.concatenate([jnp.zeros((1, C_BLOCK, HD), F32), kb[:-1]], axis=0)
        vp = jnp.concatenate([jnp.zeros((1, C_BLOCK, HD), F32), vb[:-1]], axis=0)
        has_prev = (_iota((n, C_BLOCK, C_BLOCK), 0) % nb) > 0
        d_prev = C_BLOCK + qi - kj
        sp = jnp.where((kj >= qi) & has_prev, _mm(qb, kp, tb=True) * scale - sl * d_prev.astype(F32), -jnp.inf)
        mx = jnp.maximum(mx, jnp.max(sp, axis=-1, keepdims=True))
    p = jnp.exp(sc - mx)
    den = jnp.sum(p, axis=-1, keepdims=True)
    if nb > 1:
        pp = jnp.exp(sp - mx)
        den = den + jnp.sum(pp, axis=-1, keepdims=True)
    out = _mm(p / den, vb)
    if nb > 1:
        out = out + _mm(pp / den, vp)
    lse = mx + jnp.log(den)
    return out.reshape(T, HD), jnp.broadcast_to(lse, (n, C_BLOCK, HD)).reshape(T, HD)


def _combine_fn(o1, o2, o3, l1, l2, l3):
    mx = jnp.maximum(jnp.maximum(l1, l2), l3)
    e1, e2, e3 = jnp.exp(l1 - mx), jnp.exp(l2 - mx), jnp.exp(l3 - mx)
    s = e1 + e2 + e3
    return (e1 / s) * o1 + (e2 / s) * o2 + (e3 / s) * o3


def _branch_blocks(dil):
    return -(-(T // dil) // C_BLOCK)


def _load_branch_order(ref, dil):
    if dil == 1:
        return ref[...]
    seg = T // dil
    return jnp.concatenate([ref[pl.ds(r, seg, stride=dil), :] for r in range(dil)], axis=0)


def _store_position_order(ref, val, dil, add=False):
    seg = T // dil
    for r in range(dil):
        rows = slice(None) if dil == 1 else pl.ds(r, seg, stride=dil)
        piece = val if dil == 1 else val[r * seg:(r + 1) * seg]
        if add:
            ref[rows, :] += piece
        else:
            ref[rows, :] = piece


def _dattn_specs():
    col = lambda off: pl.BlockSpec((T, HD), lambda h, off=off: (0, off // HD + h))
    row = pl.BlockSpec((1, HD), lambda h: (0, 0))
    slope = pl.BlockSpec((None, 1, HD), lambda h: (h, 0, 0))
    return [col(P_CQ), col(P_CK), col(P_CV), row, row, slope]


def _dattn_branches(q_ref, k_ref, v_ref, qg, kg, slope, o_scr, l_scr):
    for b, (_, dil) in enumerate(C_BRANCHES):
        q, k, v = (_load_branch_order(r, dil) for r in (q_ref, k_ref, v_ref))
        o, l = _attn_fn(q, k, v, qg, kg, slope, dil=dil, nb=_branch_blocks(dil))
        _store_position_order(o_scr.at[b], o, dil)
        _store_position_order(l_scr.at[b], l, dil)


def dattn_fwd(p2, qg, kg, slopes):
    def body(q_ref, k_ref, v_ref, qg_ref, kg_ref, s_ref, y_ref, o_scr, l_scr):
        _dattn_branches(q_ref, k_ref, v_ref, qg_ref[...], kg_ref[...], s_ref[...], o_scr, l_scr)
        y_ref[...] = _combine_fn(o_scr[0], o_scr[1], o_scr[2], l_scr[0], l_scr[1], l_scr[2]).astype(BF16)

    return pl.pallas_call(
        body, name="dattn_fwd", grid=(C_HEADS,), in_specs=_dattn_specs(), out_specs=pl.BlockSpec((T, HD), lambda h: (0, h)),
        out_shape=jax.ShapeDtypeStruct((T, C_WIDTH), BF16),
        scratch_shapes=[pltpu.VMEM((3, T, HD), F32), pltpu.VMEM((3, T, HD), F32)], compiler_params=_cparams(("arbitrary",)),
    )(p2, p2, p2, qg, kg, slopes)


def dattn_bwd(p2, qg, kg, slopes, dmix):
    hcol = pl.BlockSpec((T, HD), lambda h: (0, h))
    row = pl.BlockSpec((1, HD), lambda h: (0, 0))
    dy = pl.BlockSpec((T, HD), lambda h: (0, (A_WIDTH + B_WIDTH) // HD + h))

    def body(q_ref, k_ref, v_ref, qg_ref, kg_ref, s_ref, dy_ref, dq_ref, dk_ref, dv_ref, dqg_ref, dkg_ref, o_scr, l_scr, g_scr,
             acc):
        qg, kg, slope = qg_ref[...], kg_ref[...], s_ref[...]
        _dattn_branches(q_ref, k_ref, v_ref, qg, kg, slope, o_scr, l_scr)
        _, vjp = jax.vjp(_combine_fn, o_scr[0], o_scr[1], o_scr[2], l_scr[0], l_scr[1], l_scr[2])
        for i, g in enumerate(vjp(dy_ref[...])):
            g_scr[i] = g

        @pl.when(pl.program_id(0) == 0)
        def _():
            dqg_ref[...] = jnp.zeros_like(dqg_ref)
            dkg_ref[...] = jnp.zeros_like(dkg_ref)

        for b, (_, dil) in enumerate(C_BRANCHES):
            q, k, v = (_load_branch_order(r, dil) for r in (q_ref, k_ref, v_ref))
            do, dl = _load_branch_order(g_scr.at[b], dil), _load_branch_order(g_scr.at[3 + b], dil)
            fn = functools.partial(_attn_fn, dil=dil, nb=_branch_blocks(dil))
            _, vjp_b = jax.vjp(lambda a, b_, c, d, e, fn=fn: fn(a, b_, c, d, e, slope), q, k, v, qg, kg)
            dq, dk, dv, dqg, dkg = vjp_b((do, dl))
            for i, val in enumerate((dq, dk, dv)):
                _store_position_order(acc.at[i], val, dil, add=b > 0)
            dqg_ref[...] += dqg
            dkg_ref[...] += dkg
        for i, ref in enumerate((dq_ref, dk_ref, dv_ref)):
            ref[...] = acc[i].astype(BF16)

    scr = lambda n: pltpu.VMEM((n, T, HD), F32)
    return pl.pallas_call(
        body, name="dattn_bwd", grid=(C_HEADS,), in_specs=_dattn_specs() + [dy], out_specs=[hcol, hcol, hcol, row, row],
        out_shape=[jax.ShapeDtypeStruct((T, C_WIDTH), BF16)] * 3 + [jax.ShapeDtypeStruct((1, HD), F32)] * 2,
        scratch_shapes=[scr(3), scr(3), scr(6), scr(3)], compiler_params=_cparams(("arbitrary",)),
    )(p2, p2, p2, qg, kg, slopes, dmix)


_NCH = T // B_CHUNK


def _conv_taps(x, w_ref):
    rows = _iota(x.shape, 0)
    taps = []
    for j in range(B_CONV):
        s = B_CONV - 1 - j
        taps.append(x if s == 0 else jnp.where(rows >= s, pltpu.roll(x, s, 0), 0.0))
    pre = sum(w_ref[j:j + 1, :] * taps[j] for j in range(B_CONV))
    return pre, taps


def _conv_post(pre, mode):
    y = _silu(pre)
    if mode == "v":
        return y
    y = y * lax.rsqrt(jnp.sum(y * y, axis=-1, keepdims=True) + EPS)
    return y * (HD ** -0.5) if mode == "q" else y


def conv_fwd(p2, conv_w, mode):
    idx = "qkv".index(mode)
    xcol = pl.BlockSpec((T, HD), lambda h: (0, P_BQ // HD + B_HEADS * idx + h))
    wcol = pl.BlockSpec((B_CONV, HD), lambda h: (0, B_HEADS * idx + h))
    hcol = pl.BlockSpec((T, HD), lambda h: (0, h))

    def body(x_ref, w_ref, y_ref):
        pre, _ = _conv_taps(x_ref[...], w_ref)
        y_ref[...] = _conv_post(pre, mode)

    return pl.pallas_call(
        body, name=f"conv_fwd_{mode}", grid=(B_HEADS,), in_specs=[xcol, wcol], out_specs=hcol,
        out_shape=jax.ShapeDtypeStruct((T, B_WIDTH), F32), compiler_params=_cparams(("arbitrary",)),
    )(p2, conv_w)


def conv_bwd(p2, conv_w, dys, mode):
    idx = "qkv".index(mode)
    xcol = pl.BlockSpec((T, HD), lambda h: (0, P_BQ // HD + B_HEADS * idx + h))
    wcol = pl.BlockSpec((B_CONV, HD), lambda h: (0, B_HEADS * idx + h))
    hcol = pl.BlockSpec((T, HD), lambda h: (0, h))
    wout = pl.BlockSpec((B_CONV, HD), lambda h: (0, h))

    def body(x_ref, w_ref, *rest):
        dy_refs, (dx_ref, dw_ref) = rest[:-2], rest[-2:]
        pre, taps = _conv_taps(x_ref[...], w_ref)
        _, vjp = jax.vjp(functools.partial(_conv_post, mode=mode), pre)
        (dpre,) = vjp(sum(r[...] for r in dy_refs))
        rows = _iota(dpre.shape, 0)
        dx = w_ref[B_CONV - 1:B_CONV, :] * dpre
        for j in range(B_CONV):
            s = B_CONV - 1 - j
            dw_ref[j:j + 1, :] = jnp.sum(dpre * taps[j], axis=0, keepdims=True)
            if s > 0:
                dx = dx + w_ref[j:j + 1, :] * jnp.where(rows < T - s, pltpu.roll(dpre, T - s, 0), 0.0)
        dx_ref[...] = dx.astype(BF16)

    return pl.pallas_call(
        body, name=f"conv_bwd_{mode}", grid=(B_HEADS,), in_specs=[xcol, wcol] + [hcol] * len(dys), out_specs=[hcol, wout],
        out_shape=[jax.ShapeDtypeStruct((T, B_WIDTH), BF16), jax.ShapeDtypeStruct((B_CONV, B_WIDTH), F32)],
        compiler_params=_cparams(("arbitrary",)),
    )(p2, conv_w, *dys)


def _gates_fn(bg, al, dtb, h):
    r = _iota((HD, HD), 0)
    logit = _mm(bg, (r == h).astype(F32), hi=True)
    a = _mm(bg, (r == h + B_HEADS).astype(F32), hi=True)
    beta = _sigmoid(logit)
    graw = -jnp.exp(al) * _softplus(a + dtb)
    tri = (_iota((_NCH, B_CHUNK, B_CHUNK), 1) >= _iota((_NCH, B_CHUNK, B_CHUNK), 2)).astype(F32)
    g = _mm(tri, graw.reshape(_NCH, B_CHUNK, HD), hi=True).reshape(T, HD)
    return beta, g


def _gates_specs():
    bg = pl.BlockSpec((T, HD), lambda h: (0, P_BB // HD))
    par = pl.BlockSpec((None, 1, HD), lambda h: (h, 0, 0))
    out = pl.BlockSpec((None, T, HD), lambda h: (h, 0, 0))
    return bg, par, out


def gates_fwd(p2, al, dtb):
    bg, par, out = _gates_specs()

    def body(bg_ref, al_ref, dtb_ref, beta_ref, g_ref):
        beta, g = _gates_fn(bg_ref[...], al_ref[...], dtb_ref[...], pl.program_id(0))
        beta_ref[...] = beta
        g_ref[...] = g

    return pl.pallas_call(
        body, name="gates_fwd", grid=(B_HEADS,), in_specs=[bg, par, par], out_specs=[out, out],
        out_shape=[jax.ShapeDtypeStruct((B_HEADS, T, HD), F32)] * 2, compiler_params=_cparams(("arbitrary",)),
    )(p2, al, dtb)


def gates_bwd(p2, al, dtb, dbeta, dg1, dg2):
    bg, par, out = _gates_specs()
    acc = pl.BlockSpec((T, HD), lambda h: (0, 0))

    def body(bg_ref, al_ref, dtb_ref, dbeta_ref, dg1_ref, dg2_ref, dbg_ref, dal_ref, ddtb_ref, acc_ref):
        h = pl.program_id(0)
        _, vjp = jax.vjp(lambda a, b, c: _gates_fn(a, b, c, h), bg_ref[...], al_ref[...], dtb_ref[...])
        dbg, dal, ddtb = vjp((dbeta_ref[...], dg1_ref[...] + dg2_ref[...]))

        @pl.when(h == 0)
        def _():
            acc_ref[...] = jnp.zeros_like(acc_ref)

        acc_ref[...] += dbg
        dbg_ref[...] = acc_ref[...].astype(BF16)
        dal_ref[...] = jnp.broadcast_to(jnp.sum(dal, axis=-1, keepdims=True), (1, HD))
        ddtb_ref[...] = jnp.broadcast_to(jnp.sum(ddtb, axis=-1, keepdims=True), (1, HD))

    return pl.pallas_call(
        body, name="gates_bwd", grid=(B_HEADS,), in_specs=[bg, par, par, out, out, out], out_specs=[acc, par, par],
        out_shape=[jax.ShapeDtypeStruct((T, HD), BF16)] + [jax.ShapeDtypeStruct((B_HEADS, 1, HD), F32)] * 2,
        scratch_shapes=[pltpu.VMEM((T, HD), F32)], compiler_params=_cparams(("arbitrary",)),
    )(p2, al, dtb, dbeta, dg1, dg2)


def _unit_lower_inverse(a):
    eye = (_iota(a.shape, 1) == _iota(a.shape, 2)).astype(F32)
    x = eye - a
    p = _mm(a, a, hi=True)
    for i in range(5):
        x = x + _mm(x, p, hi=True)
        if i < 4:
            p = _mm(p, p, hi=True)
    return x


_WY_CH = 8
_WY_ROWS = _WY_CH * B_CHUNK


def _wy_fn(q, k, v, beta, g):
    sh = (q.shape[0] // B_CHUNK, B_CHUNK, HD)
    q3, k3, v3, b3, g3 = (t.reshape(sh) for t in (q, k, v, beta, g))
    gd = g3[:, :, :B_CHUNK] - jnp.swapaxes(g3, 1, 2)[:, :B_CHUNK, :]
    ii, jj = _iota(gd.shape, 1), _iota(gd.shape, 2)
    decay = jnp.exp(jnp.where(ii >= jj, gd, -jnp.inf))
    kb = k3 * b3
    a = _mm(kb, k3, tb=True) * jnp.where(ii > jj, decay, 0.0)
    tinv = _unit_lower_inverse(a)
    u = _mm(tinv, v3 * b3, hi=True)
    w = _mm(tinv, kb * jnp.exp(g3), hi=True)
    attn = _mm(q3, k3, tb=True) * decay
    return u.reshape(q.shape), w.reshape(q.shape), attn


def _wy_specs():
    hcol = pl.BlockSpec((_WY_ROWS, HD), lambda h, i: (i, h))
    hb = pl.BlockSpec((None, _WY_ROWS, HD), lambda h, i: (h, i, 0))
    at = pl.BlockSpec((None, _WY_CH, B_CHUNK, B_CHUNK), lambda h, i: (h, i, 0, 0))
    return hcol, hb, at


_WY_GRID = (B_HEADS, _NCH // _WY_CH)


def wy_fwd(q, k, v, beta, g):
    hcol, hb, at = _wy_specs()

    def body(q_ref, k_ref, v_ref, b_ref, g_ref, u_ref, w_ref, a_ref):
        u, w, a = _wy_fn(q_ref[...], k_ref[...], v_ref[...], b_ref[...], g_ref[...])
        u_ref[...] = u
        w_ref[...] = w
        a_ref[...] = a

    return pl.pallas_call(
        body, name="wy_fwd", grid=_WY_GRID, in_specs=[hcol, hcol, hcol, hb, hb], out_specs=[hcol, hcol, at],
        out_shape=[jax.ShapeDtypeStruct((T, B_WIDTH), F32)] * 2 + [jax.ShapeDtypeStruct((B_HEADS, _NCH, B_CHUNK, B_CHUNK), F32)],
        compiler_params=_cparams(("arbitrary", "arbitrary")),
    )(q, k, v, beta, g)


def wy_bwd(q, k, v, beta, g, du, dw, dattn):
    hcol, hb, at = _wy_specs()

    def body(q_ref, k_ref, v_ref, b_ref, g_ref, du_ref, dw_ref, da_ref, dq_ref, dk_ref, dv_ref, db_ref, dg_ref):
        _, vjp = jax.vjp(_wy_fn, q_ref[...], k_ref[...], v_ref[...], b_ref[...], g_ref[...])
        for r, t in zip((dq_ref, dk_ref, dv_ref, db_ref, dg_ref), vjp((du_ref[...], dw_ref[...], da_ref[...]))):
            r[...] = t

    return pl.pallas_call(
        body, name="wy_bwd", grid=_WY_GRID, in_specs=[hcol, hcol, hcol, hb, hb, hcol, hcol, at],
        out_specs=[hcol, hcol, hcol, hb, hb],
        out_shape=[jax.ShapeDtypeStruct((T, B_WIDTH), F32)] * 3 + [jax.ShapeDtypeStruct((B_HEADS, T, HD), F32)] * 2,
        compiler_params=_cparams(("arbitrary", "arbitrary")),
    )(q, k, v, beta, g, du, dw, dattn)


def _scan_step_fn(q, k, u, w, g, attn, gate, og, s):
    v_new = u - _mm(w, s)
    o = _mm(q * jnp.exp(g), s) + _mm(attn, v_new)
    g_last = jnp.sum(jnp.where(_iota(g.shape, 0) == B_CHUNK - 1, g, 0.0), axis=0, keepdims=True)
    s_new = s * jnp.exp(g_last) + _mm(k * jnp.exp(g_last - g), v_new, ta=True)
    return _rms(o, og) * _silu(gate), s_new


def _scan_specs(rev):
    ch = (lambda n: _NCH - 1 - n) if rev else (lambda n: n)
    rows = pl.BlockSpec((B_CHUNK, B_WIDTH), lambda n: (ch(n), 0))
    gb = pl.BlockSpec((B_HEADS, B_CHUNK, HD), lambda n: (0, ch(n), 0))
    at = pl.BlockSpec((B_HEADS, None, B_CHUNK, B_CHUNK), lambda n: (0, ch(n), 0, 0))
    og = pl.BlockSpec((1, HD), lambda n: (0, 0))
    st = pl.BlockSpec((None, B_HEADS, HD, HD), lambda n: (ch(n), 0, 0, 0))
    return rows, gb, at, og, st


def scan_fwd(q, k, u, w, g, attn, gate, og):
    rows, gb, at, ogs, st = _scan_specs(False)

    def body(q_ref, k_ref, u_ref, w_ref, g_ref, a_ref, gate_ref, og_ref, y_ref, st_ref, s_ref):
        @pl.when(pl.program_id(0) == 0)
        def _():
            s_ref[...] = jnp.zeros_like(s_ref)

        for h in range(B_HEADS):
            c = slice(h * HD, (h + 1) * HD)
            s = s_ref[h]
            st_ref[h] = s
            y, s_new = _scan_step_fn(q_ref[:, c], k_ref[:, c], u_ref[:, c], w_ref[:, c], g_ref[h], a_ref[h],
                                     gate_ref[:, c], og_ref[...], s)
            y_ref[:, c] = y.astype(BF16)
            s_ref[h] = s_new

    return pl.pallas_call(
        body, name="scan_fwd", grid=(_NCH,), in_specs=[rows, rows, rows, rows, gb, at, rows, ogs], out_specs=[rows, st],
        out_shape=[jax.ShapeDtypeStruct((T, B_WIDTH), BF16), jax.ShapeDtypeStruct((_NCH, B_HEADS, HD, HD), F32)],
        scratch_shapes=[pltpu.VMEM((B_HEADS, HD, HD), F32)], compiler_params=_cparams(("arbitrary",)),
    )(q, k, u, w, g, attn, gate, og)


def scan_bwd(q, k, u, w, g, attn, gate, og, states, dmix):
    rows, gb, at, ogs, st = _scan_specs(True)
    dyb = pl.BlockSpec((B_CHUNK, HD), lambda n: (_NCH - 1 - n, 0))

    def body(q_ref, k_ref, u_ref, w_ref, g_ref, a_ref, gate_ref, og_ref, st_ref, *rest):
        dy_refs, (dq_ref, dk_ref, du_ref, dw_ref, dgate_ref, dg_ref, da_ref, dog_ref, ds_ref) = rest[:B_HEADS], rest[B_HEADS:]

        @pl.when(pl.program_id(0) == 0)
        def _():
            ds_ref[...] = jnp.zeros_like(ds_ref)
            dog_ref[...] = jnp.zeros_like(dog_ref)

        for h in range(B_HEADS):
            c = slice(h * HD, (h + 1) * HD)
            _, vjp = jax.vjp(_scan_step_fn, q_ref[:, c], k_ref[:, c], u_ref[:, c], w_ref[:, c], g_ref[h], a_ref[h],
                             gate_ref[:, c], og_ref[...], st_ref[h])
            dq, dk, du, dw, dg, da, dgate, dog, ds = vjp((dy_refs[h][...], ds_ref[h]))
            dq_ref[:, c] = dq
            dk_ref[:, c] = dk
            du_ref[:, c] = du
            dw_ref[:, c] = dw
            dgate_ref[:, c] = dgate.astype(BF16)
            dg_ref[h] = dg
            da_ref[h] = da
            dog_ref[...] += dog
            ds_ref[h] = ds

    dy_specs = [pl.BlockSpec((B_CHUNK, HD), lambda n, h=h: (_NCH - 1 - n, A_WIDTH // HD + h)) for h in range(B_HEADS)]
    return pl.pallas_call(
        body, name="scan_bwd", grid=(_NCH,),
        in_specs=[rows, rows, rows, rows, gb, at, rows, ogs, st] + dy_specs,
        out_specs=[rows] * 5 + [gb, at, ogs],
        out_shape=[jax.ShapeDtypeStruct((T, B_WIDTH), F32)] * 4 + [jax.ShapeDtypeStruct((T, B_WIDTH), BF16)]
        + [jax.ShapeDtypeStruct((B_HEADS, T, HD), F32), jax.ShapeDtypeStruct((B_HEADS, _NCH, B_CHUNK, B_CHUNK), F32),
           jax.ShapeDtypeStruct((1, HD), F32)],
        scratch_shapes=[pltpu.VMEM((B_HEADS, HD, HD), F32)], compiler_params=_cparams(("arbitrary",)),
    )(q, k, u, w, g, attn, gate, og, states, *([dmix] * B_HEADS))


def _lanes(vec):
    return jnp.broadcast_to(vec[:, None, None], (vec.shape[0], 1, HD))


def gdn_forward(p2, conv_w, a_log, dt_bias, og):
    qa, ka, va = (conv_fwd(p2, conv_w, m) for m in "qkv")
    beta, g = gates_fwd(p2, _lanes(a_log), _lanes(dt_bias))
    u, w, attn = wy_fwd(qa, ka, va, beta, g)
    gate = p2[:, P_BG:P_BB]
    y, states = scan_fwd(qa, ka, u, w, g, attn, gate, og)
    return y, (qa, ka, va, beta, g, u, w, attn, gate, states)


def gdn_backward(p2, conv_w, a_log, dt_bias, og, saved, dmix):
    qa, ka, va, beta, g, u, w, attn, gate, states = saved
    dq1, dk1, du, dw, dgate, dg1, dattn, dog = scan_bwd(qa, ka, u, w, g, attn, gate, og, states, dmix)
    dq2, dk2, dv, dbeta, dg2 = wy_bwd(qa, ka, va, beta, g, du, dw, dattn)
    dbg, dal, ddtb = gates_bwd(p2, _lanes(a_log), _lanes(dt_bias), dbeta, dg1, dg2)
    dxq, dwq = conv_bwd(p2, conv_w, [dq1, dq2], "q")
    dxk, dwk = conv_bwd(p2, conv_w, [dk1, dk2], "k")
    dxv, dwv = conv_bwd(p2, conv_w, [dv], "v")
    return [dxq, dxk, dxv, dgate, dbg], jnp.concatenate([dwq, dwk, dwv], axis=1), dal[:, 0, 0], ddtb[:, 0, 0], dog


_SLOPES = np.exp2(-8.0 * (np.arange(C_HEADS, dtype=np.float64) + 1.0) / C_HEADS).astype(np.float32)


def _alibi_slopes():
    return _lanes(jnp.asarray(_SLOPES))


_ROWS = 256
_TM = 1024
_TM_FFN = 512


def _dep_specs(dep, ngrid):
    if dep is None:
        return [], []
    return [dep], [pl.BlockSpec((8, HD), lambda *_: (0, 0))]


def rmsnorm_fwd(x, g, dep=None):
    blk = pl.BlockSpec((_ROWS, D), lambda i: (i, 0))
    deps, dspecs = _dep_specs(dep, 1)

    def body(x_ref, g_ref, *rest):
        rest[-1][...] = _rms(x_ref[...], g_ref[...]).astype(BF16)

    return pl.pallas_call(
        body, name="rmsnorm_fwd", grid=(T // _ROWS,), in_specs=[blk, pl.BlockSpec((1, D), lambda i: (0, 0))] + dspecs,
        out_specs=blk, out_shape=jax.ShapeDtypeStruct((T, D), BF16), compiler_params=_cparams(("arbitrary",)),
    )(x, g, *deps)


def rmsnorm_bwd(x, g, dh, dres):
    blk = pl.BlockSpec((_ROWS, D), lambda i: (i, 0))
    row = pl.BlockSpec((1, D), lambda i: (0, 0))

    def body(x_ref, g_ref, dh_ref, dres_ref, dx_ref, dg_ref):
        _, vjp = jax.vjp(_rms, x_ref[...], g_ref[...])
        dx, dg = vjp(dh_ref[...])
        dx_ref[...] = dres_ref[...] + dx

        @pl.when(pl.program_id(0) == 0)
        def _():
            dg_ref[...] = jnp.zeros_like(dg_ref)

        dg_ref[...] += dg

    return pl.pallas_call(
        body, name="rmsnorm_bwd", grid=(T // _ROWS,), in_specs=[blk, row, blk, blk], out_specs=[blk, row],
        out_shape=[jax.ShapeDtypeStruct((T, D), F32), jax.ShapeDtypeStruct((1, D), F32)],
        compiler_params=_cparams(("arbitrary",)),
    )(x, g, dh, dres)


def _matmul(name, a, b, *, grid, a_spec, b_spec, o_spec, out_shape, ta=False, tb=False, k_axis=None, res=None, dep=None):
    dims = _dims(2, ta, tb)
    deps, dspecs = _dep_specs(dep, len(grid))

    def body(a_ref, b_ref, *rest):
        o_ref = rest[-1]
        prod = lax.dot_general(a_ref[...].astype(BF16), b_ref[...].astype(BF16), dims, preferred_element_type=F32)
        if res is not None:
            prod = prod + rest[0][...]
        if k_axis is None:
            o_ref[...] = prod.astype(o_ref.dtype)
        else:
            @pl.when(pl.program_id(k_axis) == 0)
            def _():
                o_ref[...] = prod

            @pl.when(pl.program_id(k_axis) > 0)
            def _():
                o_ref[...] += prod

    sem = tuple("arbitrary" for _ in grid)
    ins = [a, b] + ([res] if res is not None else []) + deps
    specs = [a_spec, b_spec] + ([o_spec] if res is not None else []) + dspecs
    return pl.pallas_call(
        body, name=name, grid=grid, in_specs=specs, out_specs=o_spec, out_shape=out_shape, compiler_params=_cparams(sem),
    )(*ins)


_IN_TN = P_END // 3


def mm_proj(h1, wp_in, l):
    return _matmul(
        "mm_proj", h1, wp_in, grid=(P_END // _IN_TN, T // _TM),
        a_spec=pl.BlockSpec((_TM, D), lambda j, i: (i, 0)),
        b_spec=pl.BlockSpec((None, D, _IN_TN), lambda j, i: (l, 0, j)),
        o_spec=pl.BlockSpec((_TM, _IN_TN), lambda j, i: (i, j)), out_shape=jax.ShapeDtypeStruct((T, P_END), F32))


def mm_dh1(dp2, wp_in, l):
    return _matmul(
        "mm_dh1", dp2, wp_in, grid=(T // _TM, P_END // _IN_TN), tb=True, k_axis=1,
        a_spec=pl.BlockSpec((_TM, _IN_TN), lambda i, k: (i, k)),
        b_spec=pl.BlockSpec((None, D, _IN_TN), lambda i, k: (l, 0, k)),
        o_spec=pl.BlockSpec((_TM, D), lambda i, k: (i, 0)), out_shape=jax.ShapeDtypeStruct((T, D), F32))


def mm_dwin(h1, dp2):
    return _matmul(
        "mm_dwin", h1, dp2, grid=(P_END // _IN_TN, D // _TM), ta=True,
        a_spec=pl.BlockSpec((T, _TM), lambda j, i: (0, i)),
        b_spec=pl.BlockSpec((T, _IN_TN), lambda j, i: (0, j)),
        o_spec=pl.BlockSpec((_TM, _IN_TN), lambda j, i: (i, j)), out_shape=jax.ShapeDtypeStruct((D, P_END), BF16))


def _mm_square(name, a, w, l, res, tb, dep=None):
    tn = 1024
    b_spec = (pl.BlockSpec((None, tn, D), lambda j, i: (l, j, 0)) if tb else pl.BlockSpec((None, D, tn), lambda j, i: (l, 0, j)))
    return _matmul(
        name, a, w, grid=(D // tn, T // _TM), tb=tb, res=res, dep=dep,
        a_spec=pl.BlockSpec((_TM, D), lambda j, i: (i, 0)), b_spec=b_spec,
        o_spec=pl.BlockSpec((_TM, tn), lambda j, i: (i, j)), out_shape=jax.ShapeDtypeStruct((T, D), F32))


def mm_out(mix, wg_out, l, x):
    return _mm_square("mm_out", mix, wg_out, l, x, False)


def mm_dmix(dx1, wg_out, l, dep=None):
    return _mm_square("mm_dmix", dx1, wg_out, l, None, True, dep)


def mm_dwout(mix, dx1):
    tn = 1024
    return _matmul(
        "mm_dwout", mix, dx1, grid=(D // tn, D // _TM), ta=True,
        a_spec=pl.BlockSpec((T, _TM), lambda j, i: (0, i)), b_spec=pl.BlockSpec((T, tn), lambda j, i: (0, j)),
        o_spec=pl.BlockSpec((_TM, tn), lambda j, i: (i, j)), out_shape=jax.ShapeDtypeStruct((D, D), BF16))


_GU_TN = GU_SHARD // 2


_GU_NJ = FFN // _GU_TN


def mm_dh2(dgu, wg_gu, l):
    return _matmul(
        "mm_dh2", dgu, wg_gu, grid=(T // _TM, 2 * N_CHIPS), tb=True, k_axis=1,
        a_spec=pl.BlockSpec((None, _TM, _GU_TN), lambda i, k: (k // _GU_NJ, i, k % _GU_NJ)),
        b_spec=pl.BlockSpec((None, None, D, _GU_TN), lambda i, k: (l, k // 2, 0, k % 2)),
        o_spec=pl.BlockSpec((_TM, D), lambda i, k: (i, 0)), out_shape=jax.ShapeDtypeStruct((T, D), F32))


def mm_dwgu(h2, dgu):
    return _matmul(
        "mm_dwgu", h2, dgu, grid=(N_CHIPS, 2, D // _TM), ta=True,
        a_spec=pl.BlockSpec((T, _TM), lambda s, j, i: (0, i)),
        b_spec=pl.BlockSpec((None, T, _GU_TN), lambda s, j, i: ((2 * s + j) // _GU_NJ, 0, (2 * s + j) % _GU_NJ)),
        o_spec=pl.BlockSpec((None, _TM, _GU_TN), lambda s, j, i: (s, i, j)),
        out_shape=jax.ShapeDtypeStruct((N_CHIPS, D, GU_SHARD), BF16))


def mm_down(act, wg_down, l, x1):
    tn = 512
    return _matmul(
        "mm_down", act, wg_down, grid=(D // tn, T // _TM), res=x1,
        a_spec=pl.BlockSpec((_TM, FFN), lambda j, i: (i, 0)),
        b_spec=pl.BlockSpec((None, FFN, tn), lambda j, i: (l, 0, j)),
        o_spec=pl.BlockSpec((_TM, tn), lambda j, i: (i, j)), out_shape=jax.ShapeDtypeStruct((T, D), F32))


def mm_dwdown(act, dx2):
    tm, tn = DOWN_SHARD, 512
    return _matmul(
        "mm_dwdown", act, dx2, grid=(D // tn, FFN // tm), ta=True,
        a_spec=pl.BlockSpec((T, tm), lambda j, i: (0, i)), b_spec=pl.BlockSpec((T, tn), lambda j, i: (0, j)),
        o_spec=pl.BlockSpec((tm, tn), lambda j, i: (i, j)), out_shape=jax.ShapeDtypeStruct((FFN, D), BF16))


_FF_TN = 1408


def _swiglu_fn(gt, up):
    return _silu(gt) * up


def _gate_up_specs():
    gate = pl.BlockSpec((None, None, D, _FF_TN), lambda j, i: (0, j // 2, 0, j % 2))
    up = pl.BlockSpec((None, None, D, _FF_TN), lambda j, i: (0, N_CHIPS // 2 + j // 2, 0, j % 2))
    both = pl.BlockSpec((2, _TM_FFN, _FF_TN), lambda j, i: (0, i, j))
    return gate, up, both


def mm_gu_swiglu(h2, wg_gu):
    gate, up, both = _gate_up_specs()

    def body(h_ref, wg_ref, wu_ref, gu_ref, act_ref):
        h = h_ref[...]
        gt = jnp.dot(h, wg_ref[...], preferred_element_type=F32)
        u = jnp.dot(h, wu_ref[...], preferred_element_type=F32)
        gu_ref[0] = gt
        gu_ref[1] = u
        act_ref[...] = _swiglu_fn(gt, u).astype(BF16)

    return pl.pallas_call(
        body, name="mm_gu_swiglu", grid=(FFN // _FF_TN, T // _TM_FFN),
        in_specs=[pl.BlockSpec((_TM_FFN, D), lambda j, i: (i, 0)), gate, up],
        out_specs=[both, pl.BlockSpec((_TM_FFN, _FF_TN), lambda j, i: (i, j))],
        out_shape=[jax.ShapeDtypeStruct((2, T, FFN), F32), jax.ShapeDtypeStruct((T, FFN), BF16)],
        compiler_params=_cparams(("arbitrary", "arbitrary")),
    )(h2, wg_gu, wg_gu)


def mm_dact_swiglu(dx2, wg_down, gu, dep=None):
    _, _, both = _gate_up_specs()
    deps, dspecs = _dep_specs(dep, 2)

    def body(dx_ref, w_ref, gu_ref, *rest):
        dact = lax.dot_general(dx_ref[...].astype(BF16), w_ref[...], _dims(2, False, True), preferred_element_type=F32)
        _, vjp = jax.vjp(_swiglu_fn, gu_ref[0], gu_ref[1])
        dgt, dup = vjp(dact)
        rest[-1][0] = dgt.astype(BF16)
        rest[-1][1] = dup.astype(BF16)

    return pl.pallas_call(
        body, name="mm_dact_swiglu", grid=(FFN // _FF_TN, T // _TM_FFN),
        in_specs=[pl.BlockSpec((_TM_FFN, D), lambda j, i: (i, 0)), pl.BlockSpec((None, _FF_TN, D), lambda j, i: (0, j, 0)),
                  both]
        + dspecs,
        out_specs=both, out_shape=jax.ShapeDtypeStruct((2, T, FFN), BF16),
        compiler_params=_cparams(("arbitrary", "arbitrary")),
    )(dx2, wg_down, gu, *deps)


def loss_and_grad(y, target):
    blk = pl.BlockSpec((_ROWS, D), lambda i: (i, 0))
    acc = pl.BlockSpec((8, HD), lambda i: (0, 0))

    def body(y_ref, t_ref, dy_ref, l_ref):
        err = y_ref[...] - t_ref[...]
        dy_ref[...] = err * (1.0 / D)

        @pl.when(pl.program_id(0) == 0)
        def _():
            l_ref[...] = jnp.zeros_like(l_ref)

        l_ref[...] += (0.5 / D) * jnp.sum(err * err)

    return pl.pallas_call(
        body, name="loss_and_grad", grid=(T // _ROWS,), in_specs=[blk, blk], out_specs=[blk, acc],
        out_shape=[jax.ShapeDtypeStruct((T, D), F32), jax.ShapeDtypeStruct((8, HD), F32)],
        compiler_params=_cparams(("arbitrary",)),
    )(y, target)


def adamw(w, g, m, v, name):
    rows, cols = w.shape
    tr = _ROWS if rows % _ROWS == 0 else rows
    blk = pl.BlockSpec((tr, cols), lambda i: (i, 0))

    def body(w_ref, g_ref, m_ref, v_ref, d_ref, nm_ref, nv_ref):
        gg = g_ref[...]
        nm = ADAM_B1 * m_ref[...] + (1.0 - ADAM_B1) * gg
        nv = ADAM_B2 * v_ref[...] + (1.0 - ADAM_B2) * (gg * gg)
        m_hat = nm / (1.0 - ADAM_B1 ** ADAM_STEP)
        v_hat = nv / (1.0 - ADAM_B2 ** ADAM_STEP)
        d_ref[...] = -ADAM_LR * (m_hat / (jnp.sqrt(v_hat) + ADAM_EPS) + ADAM_WD * w_ref[...])
        nm_ref[...] = nm
        nv_ref[...] = nv

    return pl.pallas_call(
        body, name=name, grid=(rows // tr,), in_specs=[blk] * 4, out_specs=[blk] * 3,
        out_shape=[jax.ShapeDtypeStruct(w.shape, F32)] * 3, compiler_params=_cparams(("arbitrary",)),
    )(w, g, m, v)


_LANE = 128


def _segment_of_shard_column():
    flat = np.full(P_END, -1, np.int64)
    for o in range(P_END):
        if GATE_COLS <= o < P_CQ:
            continue
        c = o if o < GATE_COLS else o - (P_CQ - GATE_COLS)
        flat[o] = (c // IN_SHARD) * IN_SHARD_PAD + c % IN_SHARD
    return flat


def _block_pairs(src_of_dst):
    return [sorted({int(c) // _LANE for c in src_of_dst[db * _LANE:(db + 1) * _LANE] if c >= 0})
            for db in range(len(src_of_dst) // _LANE)]


_RELAYOUT_ROWS = 512
_SHARD_BLOCKS = IN_SHARD_PAD // _LANE


def _relayout(name, x, to_segments):
    seg_of = _segment_of_shard_column()
    if to_segments:
        src_of_dst = seg_of
    else:
        src_of_dst = np.full(N_CHIPS * IN_SHARD_PAD, -1, np.int64)
        src_of_dst[seg_of[seg_of >= 0]] = np.nonzero(seg_of >= 0)[0]
    sources = _block_pairs(src_of_dst)
    n_dst = len(sources)
    col_map = jnp.asarray(src_of_dst.reshape(n_dst, 1, _LANE), jnp.int32)
    shard_blk = pl.BlockSpec((N_CHIPS, _RELAYOUT_ROWS, IN_SHARD_PAD), lambda i: (0, i, 0))
    seg_blk = pl.BlockSpec((_RELAYOUT_ROWS, P_END), lambda i: (i, 0))

    def shard_cols(ref, b):
        return ref.at[b // _SHARD_BLOCKS, :, pl.ds((b % _SHARD_BLOCKS) * _LANE, _LANE)]

    def seg_cols(ref, b):
        return ref.at[:, pl.ds(b * _LANE, _LANE)]

    src_cols, dst_cols = (shard_cols, seg_cols) if to_segments else (seg_cols, shard_cols)

    def body(x_ref, map_ref, o_ref):
        src_row = _iota((_LANE, _LANE), 0)
        for d in range(n_dst):
            acc = jnp.zeros((_RELAYOUT_ROWS, _LANE), F32)
            for sb in sources[d]:
                sel = (src_row + sb * _LANE == map_ref[d]).astype(x_ref.dtype)
                acc = acc + jnp.dot(src_cols(x_ref, sb)[...], sel, preferred_element_type=F32)
            dst_cols(o_ref, d)[...] = acc.astype(o_ref.dtype)

    rows = x.shape[-2]
    out_shape = (rows, P_END) if to_segments else (N_CHIPS, rows, IN_SHARD_PAD)
    return pl.pallas_call(
        body, name=name, grid=(rows // _RELAYOUT_ROWS,),
        in_specs=[shard_blk if to_segments else seg_blk, pl.BlockSpec(col_map.shape, lambda i: (0, 0, 0))],
        out_specs=seg_blk if to_segments else shard_blk, out_shape=jax.ShapeDtypeStruct(out_shape, x.dtype),
        compiler_params=_cparams(("arbitrary",)),
    )(x, col_map)


def shards_to_segments(w):
    return _relayout("shards_to_segments", w, True)


def segments_to_shards(w):
    return _relayout("segments_to_shards", w, False)


def mixers_forward(x, w_in, sp, dep=None):
    h1 = rmsnorm_fwd(x, sp["norm1_g"], dep)
    p2 = mm_proj(h1, w_in, 0)
    y_a = sgu_fwd(p2, sp["sgu_norm_g"], sp["w_spatial"], sp["b_spatial"])
    y_b, saved_b = gdn_forward(p2, sp["conv_w"], sp["a_log"], sp["dt_bias"], sp["o_norm_g"])
    y_c = dattn_fwd(p2, sp["q_norm_g"], sp["k_norm_g"], _alibi_slopes())
    mix = jnp.concatenate([y_a, y_b, y_c], axis=1)
    return mix, (x, h1, p2, saved_b, mix)


def ffn_forward(x, mix, wg, sp):
    x1 = mm_out(mix, wg["out"], 0, x)
    h2 = rmsnorm_fwd(x1, sp["norm2_g"])
    gu, act = mm_gu_swiglu(h2, wg["gu"])
    x2 = mm_down(act, wg["down"], 0, x1)
    return x2, (x1, h2, gu, act)


def ffn_backward(dx2, wg, sp, saved, dep=None):
    x1, h2, gu, act = saved
    dgu = mm_dact_swiglu(dx2, wg["down"], gu, dep)
    dw_down = mm_dwdown(act, dx2)
    dw_gu = mm_dwgu(h2, dgu)
    dh2 = mm_dh2(dgu, wg["gu"], 0)
    dx1, dnorm2 = rmsnorm_bwd(x1, sp["norm2_g"], dh2, dx2)
    return dx1, dnorm2, dw_gu, dw_down


def mixers_backward(dx1, wg, sp, saved, dep=None):
    x, h1, p2, saved_b, mix = saved
    dmix = mm_dmix(dx1, wg["out"], 0, dep)
    dw_out = mm_dwout(mix, dx1)
    du, dv, dsg, dws, dbs = sgu_bwd(p2, sp["sgu_norm_g"], sp["w_spatial"], sp["b_spatial"], dmix)
    dseg_b, dconv, dal, ddtb, dog = gdn_backward(p2, sp["conv_w"], sp["a_log"], sp["dt_bias"], sp["o_norm_g"], saved_b, dmix)
    dcq, dck, dcv, dqg, dkg = dattn_bwd(p2, sp["q_norm_g"], sp["k_norm_g"], _alibi_slopes(), dmix)
    dp2 = jnp.concatenate([du, dv] + dseg_b + [dcq, dck, dcv], axis=1)
    dw_in = segments_to_shards(mm_dwin(h1, dp2))
    dh1 = mm_dh1(dp2, wg["in"], 0)
    dx, dnorm1 = rmsnorm_bwd(x, sp["norm1_g"], dh1, dx1)
    small = {"norm1_g": dnorm1, "sgu_norm_g": dsg, "w_spatial": dws, "b_spatial": dbs, "conv_w": dconv, "a_log": dal,
             "dt_bias": ddtb, "o_norm_g": dog, "q_norm_g": dqg, "k_norm_g": dkg}
    return dx, dw_in, dw_out, small


def layer_forward(x, wg, sp, dep=None):
    mix, saved_m = mixers_forward(x, wg["in"], sp, dep)
    x2, saved_f = ffn_forward(x, mix, wg, sp)
    return x2, (saved_m, saved_f)


def layer_backward(dx2, wg, sp, saved, dep=None):
    dx1, dnorm2, dw_gu, dw_down = ffn_backward(dx2, wg, sp, saved[1], dep)
    dx, dw_in, dw_out, small = mixers_backward(dx1, wg, sp, saved[0])
    return dx, {"in": dw_in, "out": dw_out, "gu": dw_gu, "down": dw_down}, {**small, "norm2_g": dnorm2}


_HBM = pl.BlockSpec(memory_space=pltpu.HBM)
_MESH = pl.DeviceIdType.MESH


def _place():
    x, y, c = lax.axis_index("x"), lax.axis_index("y"), lax.axis_index("c")
    chips = [(1 - x, y), (x, 1 - y), (1 - x, 1 - y)]
    return x, y, c, chips


def _rcopy(src, dst, ssem, rsem, dev):
    return pltpu.make_async_remote_copy(src_ref=src, dst_ref=dst, send_sem=ssem, recv_sem=rsem, device_id=dev,
                                        device_id_type=_MESH)


def _xor(a, b):
    return a + b - 2 * a * b


def gather_weights(shards):
    n = len(shards)

    def body(*refs):
        ins, outs = refs[:n], refs[n:2 * n]
        s_ici, r_ici, s_d2d, r_d2d, s_own, r_own = refs[2 * n:]
        x, y, c, _ = _place()
        s = 2 * x + y
        sibling = (x, y, 1 - c)
        nbr = [(1 - x, y), (x, 1 - y)]
        src_chip = (_xor(x, 1 - c), _xor(y, c))
        dst_chip = (_xor(x, c), _xor(y, 1 - c))
        t_src = 2 * src_chip[0] + src_chip[1]
        t_oth = 2 * dst_chip[0] + dst_chip[1]
        t_dia = 2 * (1 - x) + (1 - y)
        sends = []
        for a in range(n):
            for u in range(2):
                sends.append(_rcopy(ins[a].at[u], outs[a].at[s, u], s_own.at[a, u], r_own.at[a, u], sibling))
            for k in range(2):
                sends.append(_rcopy(ins[a].at[c], outs[a].at[s, c], s_ici.at[a, k], r_ici.at[a, k], (*nbr[k], c)))
        for cp in sends:
            cp.start()

        def landed(a, t, k):
            _rcopy(ins[a].at[c], outs[a].at[t, c], s_ici.at[a, k], r_ici.at[a, k], sibling).wait_recv()
            cp = _rcopy(outs[a].at[t, c], outs[a].at[t, c], s_d2d.at[a, k], r_d2d.at[a, k], sibling)
            cp.start()
            sends.append(cp)

        for a in range(n):
            landed(a, t_src, c)
            fwd = _rcopy(outs[a].at[t_src, c], outs[a].at[t_src, c], s_ici.at[a, 2], r_ici.at[a, 2], (*dst_chip, c))
            fwd.start()
            sends.append(fwd)
        for a in range(n):
            landed(a, t_oth, 1 - c)
        for a in range(n):
            landed(a, t_dia, 2)
        for a in range(n):
            for u in range(2):
                _rcopy(ins[a].at[u], outs[a].at[s, u], s_own.at[a, u], r_own.at[a, u], sibling).wait_recv()
            for k, t in enumerate([2 * nbr[0][0] + nbr[0][1], 2 * nbr[1][0] + nbr[1][1], t_dia]):
                _rcopy(ins[a].at[1 - c], outs[a].at[t, 1 - c], s_d2d.at[a, k], r_d2d.at[a, k], sibling).wait_recv()
        for cp in sends:
            cp.wait_send()

    dma = lambda k: pltpu.SemaphoreType.DMA((n, k))
    return pl.pallas_call(
        body, name="gather_weights", in_specs=[_HBM] * n, out_specs=[_HBM] * n,
        out_shape=[jax.ShapeDtypeStruct((N_CHIPS,) + w.shape, w.dtype) for w in shards],
        scratch_shapes=[dma(3), dma(3), dma(3), dma(3), dma(2), dma(2)],
    )(*shards)


_SEM = pl.BlockSpec(memory_space=pltpu.SEMAPHORE)
_SIDE_EFFECT = pltpu.SideEffectType.DATAFLOW_SIDE_EFFECTING


def _in_hbm(a):
    return pltpu.with_memory_space_constraint(a, pltpu.HBM)


def _split_copy(name, srcs, land_shapes, n_sems, copies):
    n, m = len(srcs), len(land_shapes)
    thru = [pltpu.HBM(a.shape, a.dtype) for a in srcs] + [pltpu.HBM(s.shape, s.dtype) for s in land_shapes]
    sems = (pltpu.SemaphoreType.DMA((n_sems,)), pltpu.SemaphoreType.DMA((n_sems,)))

    def start(dep=None):
        deps = [] if dep is None else [dep]

        def body(*refs):
            ins, lands = refs[:n], refs[n:n + m]
            ssem, rsem, token = refs[n + m + len(deps)], refs[n + m + len(deps) + 1], refs[-1]
            for cp in copies(ins, lands, ssem, rsem)[0]:
                cp.start()
            token[...] = jnp.zeros_like(token)

        out = pl.pallas_call(
            body, name=name + "_start", out_shape=(*sems, *thru, jax.ShapeDtypeStruct((8, HD), F32)),
            in_specs=[_HBM] * (n + m) + [pl.BlockSpec(memory_space=pl.ANY)] * len(deps),
            out_specs=(_SEM, _SEM, *[_HBM] * (n + m), pl.BlockSpec(memory_space=pltpu.VMEM)),
            input_output_aliases={i: 2 + i for i in range(n + m)},
            compiler_params=pltpu.CompilerParams(has_side_effects=_SIDE_EFFECT),
        )(*[_in_hbm(a) for a in srcs], *[_in_hbm(lax.empty(s.shape, s.dtype)) for s in land_shapes], *deps)
        return out[:-1], out[-1]

    def wait(state, after):
        def body(*refs):
            ins, lands, ssem, rsem = refs[:n], refs[n:n + m], refs[n + m], refs[n + m + 1]
            sent, arrivals = copies(ins, lands, ssem, rsem)
            for cp in sent:
                cp.wait_send()
            for cp in arrivals:
                cp.wait_recv()

        out = pl.pallas_call(
            body, name=name + "_wait", out_shape=tuple(thru),
            in_specs=[_HBM] * (n + m) + [_SEM, _SEM, pl.BlockSpec(memory_space=pl.ANY)], out_specs=[_HBM] * (n + m),
            input_output_aliases={i: i for i in range(n + m)},
            compiler_params=pltpu.CompilerParams(has_side_effects=_SIDE_EFFECT),
        )(*state[2:], state[0], state[1], after)
        return list(out[:n]), list(out[n:])

    return start, wait


def gather_direct(shards, tag):
    n = len(shards)

    def copies(ins, lands, ssem, rsem):
        x, y, c, chips = _place()
        s = 2 * x + y
        sibling = (x, y, 1 - c)
        sent, arrivals = [], []
        for a in range(n):
            for u in range(2):
                cp = _rcopy(ins[a].at[u], lands[a].at[s, u], ssem.at[5 * a + u], rsem.at[5 * a + u], sibling)
                sent.append(cp)
                arrivals.append(cp)
            for j, (cx, cy) in enumerate(chips):
                k = 5 * a + 2 + j
                sent.append(_rcopy(ins[a].at[c], lands[a].at[s, c], ssem.at[k], rsem.at[k], (cx, cy, c)))
                arrivals.append(_rcopy(ins[a].at[c], lands[a].at[2 * cx + cy, c], ssem.at[k], rsem.at[k], (cx, cy, c)))
        return sent, arrivals

    lands = [jax.ShapeDtypeStruct((N_CHIPS,) + w.shape, w.dtype) for w in shards]
    return _split_copy("gather_direct_" + tag, shards, lands, 5 * n, copies)


def pass_to_sibling(lands):
    n = len(lands)

    def body(*refs):
        ins = refs[:n]
        ssem, rsem = refs[2 * n:]
        x, y, c, chips = _place()
        sibling = (x, y, 1 - c)
        cps, arrivals = [], []
        for a in range(n):
            for j, (cx, cy) in enumerate(chips):
                t = 2 * cx + cy
                cps.append(_rcopy(ins[a].at[t, c], ins[a].at[t, c], ssem.at[a, j], rsem.at[a, j], sibling))
                arrivals.append(_rcopy(ins[a].at[t, c], ins[a].at[t, 1 - c], ssem.at[a, j], rsem.at[a, j], sibling))
        for cp in cps:
            cp.start()
        for cp, ar in zip(cps, arrivals):
            cp.wait_send()
            ar.wait_recv()

    return pl.pallas_call(
        body, name="pass_to_sibling", in_specs=[_HBM] * n, out_specs=[_HBM] * n,
        out_shape=[jax.ShapeDtypeStruct(a.shape, a.dtype) for a in lands], input_output_aliases={a: a for a in range(n)},
        scratch_shapes=[pltpu.SemaphoreType.DMA((n, 3)), pltpu.SemaphoreType.DMA((n, 3))],
    )(*lands)


def exchange_halves(grads):
    n = len(grads)

    def body(*refs):
        ins, outs = refs[:n], refs[n:2 * n]
        ssem, rsem = refs[2 * n:]
        x, y, c, _ = _place()
        cps = []
        for a in range(n):
            h = grads[a].shape[1] // 2
            cps.append(_rcopy(ins[a].at[:, pl.ds((1 - c) * h, h)], outs[a], ssem.at[a], rsem.at[a], (x, y, 1 - c)))
        for cp in cps:
            cp.start()
        for cp in cps:
            cp.wait()

    return pl.pallas_call(
        body, name="exchange_halves", in_specs=[_HBM] * n, out_specs=[_HBM] * n,
        out_shape=[jax.ShapeDtypeStruct((g.shape[0], g.shape[1] // 2, g.shape[2]), g.dtype) for g in grads],
        scratch_shapes=[pltpu.SemaphoreType.DMA((n,)), pltpu.SemaphoreType.DMA((n,))],
    )(*grads)


def scatter_to_chips(parts):
    n = len(parts)

    def body(*refs):
        ins, outs = refs[:n], refs[n:2 * n]
        ssem, rsem = refs[2 * n:]
        x, y, c, chips = _place()
        cps = [_rcopy(ins[a].at[2 * cx + cy], outs[a].at[j], ssem.at[a, j], rsem.at[a, j], (cx, cy, c))
               for a in range(n) for j, (cx, cy) in enumerate(chips)]
        for cp in cps:
            cp.start()
        for cp in cps:
            cp.wait()

    return pl.pallas_call(
        body, name="scatter_to_chips", in_specs=[_HBM] * n, out_specs=[_HBM] * n,
        out_shape=[jax.ShapeDtypeStruct((3,) + p.shape[1:], p.dtype) for p in parts],
        scratch_shapes=[pltpu.SemaphoreType.DMA((n, 3)), pltpu.SemaphoreType.DMA((n, 3))],
    )(*parts)


def scatter_direct(parts, tag):
    n = len(parts)

    def copies(ins, lands, ssem, rsem):
        x, y, c, chips = _place()
        cps = [_rcopy(ins[a].at[2 * cx + cy], lands[a].at[j], ssem.at[3 * a + j], rsem.at[3 * a + j], (cx, cy, c))
               for a in range(n) for j, (cx, cy) in enumerate(chips)]
        return cps, cps

    lands = [jax.ShapeDtypeStruct((3,) + p.shape[1:], p.dtype) for p in parts]
    return _split_copy("scatter_direct_" + tag, parts, lands, 3 * n, copies)


def share_halves(halves):
    n = len(halves)

    def body(*refs):
        ins, outs = refs[:n], refs[n:2 * n]
        ssem, rsem = refs[2 * n:]
        x, y, c, _ = _place()
        cps = [_rcopy(ins[i], outs[i], ssem.at[i], rsem.at[i], (x, y, 1 - c)) for i in range(n)]
        for cp in cps:
            cp.start()
        for cp in cps:
            cp.wait()

    return pl.pallas_call(
        body, name="share_halves", in_specs=[_HBM] * n, out_specs=[_HBM] * n,
        out_shape=[jax.ShapeDtypeStruct(h.shape, h.dtype) for h in halves],
        scratch_shapes=[pltpu.SemaphoreType.DMA((n,)), pltpu.SemaphoreType.DMA((n,))],
    )(*halves)


def adamw_shard(w, m, v, mine, theirs, c, name):
    _, r, cw = w.shape
    h, cg = mine[0].shape
    tr = next(t for t in (256, 176, 128) if h % t == 0 and t * cg * 4 <= (3 << 19))
    nb = h // tr
    wblk = pl.BlockSpec((None, tr, cw), lambda l, i, c_ref: (l, i, 0))
    gblk = lambda layer, own: pl.BlockSpec((tr, cg), lambda l, i, c_ref: (_held_block(l, i, c_ref, layer, own, nb), 0))
    return _adamw_halves(w, m, v, mine, theirs, c, name, (DEPTH, r // tr), wblk, gblk, nb, cw)


def _held_block(l, i, c_ref, layer, own, nb):
    in_use = (l == layer) & (((i // nb) == c_ref[0]) == own)
    return jnp.where(in_use, i % nb, 0)


def _adamw_halves(w, m, v, mine, theirs, c, name, grid, wblk, gblk, nb, cw):
    def body(c_ref, w_ref, m_ref, v_ref, m0, m1, t0, t1, g_ref, d_ref, nm_ref, nv_ref):
        is_mine = (pl.program_id(1) // nb) == c_ref[0]
        first = pl.program_id(0) == 0
        gg = jnp.where(is_mine, jnp.where(first, m0[:, :cw], m1[:, :cw]), jnp.where(first, t0[:, :cw], t1[:, :cw]))
        nm = ADAM_B1 * m_ref[...] + (1.0 - ADAM_B1) * gg
        nv = ADAM_B2 * v_ref[...] + (1.0 - ADAM_B2) * (gg * gg)
        m_hat = nm / (1.0 - ADAM_B1 ** ADAM_STEP)
        v_hat = nv / (1.0 - ADAM_B2 ** ADAM_STEP)
        g_ref[...] = gg
        d_ref[...] = -ADAM_LR * (m_hat / (jnp.sqrt(v_hat) + ADAM_EPS) + ADAM_WD * w_ref[...])
        nm_ref[...] = nm
        nv_ref[...] = nv

    return pl.pallas_call(
        body, name=name,
        grid_spec=pltpu.PrefetchScalarGridSpec(
            num_scalar_prefetch=1, grid=grid,
            in_specs=[wblk] * 3 + [gblk(0, True), gblk(1, True), gblk(0, False), gblk(1, False)], out_specs=[wblk] * 4),
        out_shape=[jax.ShapeDtypeStruct(w.shape, F32)] * 4, compiler_params=_cparams(("arbitrary", "arbitrary")),
    )(c, w, m, v, mine[0], mine[1], theirs[0], theirs[1])


def adamw_shard_t(wt, mt, vt, mine_t, theirs_t, c, name):
    _, cw, r = wt.shape
    h = mine_t[0].shape[1]
    tc = 256
    nb = h // tc
    wblk = pl.BlockSpec((None, cw, tc), lambda l, j, c_ref: (l, 0, j))
    gblk = lambda layer, own: pl.BlockSpec((cw, tc), lambda l, j, c_ref: (0, _held_block(l, j, c_ref, layer, own, nb)))
    return _adamw_halves(wt, mt, vt, mine_t, theirs_t, c, name, (DEPTH, r // tc), wblk, gblk, nb, cw)


def _half_rows(h, cols):
    for tr in (512, 256, 352, 128, 64):
        if h % tr == 0 and tr * cols * 4 <= 6 * 1024 * 1024:
            return tr
    raise ValueError((h, cols))


def add_sibling(grad, recv, c):
    _, r, cols = grad.shape
    h = r // 2
    tr = _half_rows(h, cols)
    nb = h // tr

    def body(c_ref, g_ref, r_ref, o_ref):
        o_ref[...] = (g_ref[...].astype(F32) + r_ref[...].astype(F32)).astype(BF16)

    return pl.pallas_call(
        body, name="add_sibling",
        grid_spec=pltpu.PrefetchScalarGridSpec(
            num_scalar_prefetch=1, grid=(N_CHIPS, nb),
            in_specs=[pl.BlockSpec((None, tr, cols), lambda t, i, c_ref: (t, c_ref[0] * nb + i, 0)),
                      pl.BlockSpec((None, tr, cols), lambda t, i, c_ref: (t, i, 0))],
            out_specs=pl.BlockSpec((None, tr, cols), lambda t, i, c_ref: (t, i, 0))),
        out_shape=jax.ShapeDtypeStruct((N_CHIPS, h, cols), BF16), compiler_params=_cparams(("arbitrary", "arbitrary")),
    )(c, grad, recv)


def add_chips(part, recv, s):
    _, h, cols = part.shape
    tr = _half_rows(h, cols)

    def body(s_ref, p_ref, r_ref, o_ref):
        o_ref[...] = ((p_ref[...].astype(F32) + r_ref[0].astype(F32)) + r_ref[1].astype(F32)) + r_ref[2].astype(F32)

    return pl.pallas_call(
        body, name="add_chips",
        grid_spec=pltpu.PrefetchScalarGridSpec(
            num_scalar_prefetch=1, grid=(h // tr,),
            in_specs=[pl.BlockSpec((None, tr, cols), lambda i, s_ref: (s_ref[0], i, 0)),
                      pl.BlockSpec((3, tr, cols), lambda i, s_ref: (0, i, 0))],
            out_specs=pl.BlockSpec((tr, cols), lambda i, s_ref: (i, 0))),
        out_shape=jax.ShapeDtypeStruct((h, cols), F32), compiler_params=_cparams(("arbitrary",)),
    )(s, part, recv)


def allreduce_small(vec):
    rows = vec.shape[0]

    def body(v_ref, o_ref, buf, ssem, rsem, lsem):
        x, y, c, chips = _place()
        me, sibling = (x, y, c), (x, y, 1 - c)

        def blk(px, py, pc):
            return buf.at[4 * px + 2 * py + pc]

        def copy(k, block, to, src=None):
            return _rcopy(blk(*block) if src is None else src, blk(*block), ssem.at[k], rsem.at[k], to)

        mine = pltpu.make_async_copy(v_ref, blk(*me), lsem)
        mine.start()
        first = [copy(0, me, sibling, src=v_ref)] + [copy(1 + j, me, (*chip, c), src=v_ref) for j, chip in enumerate(chips)]
        for cp in first:
            cp.start()
        passed = [copy(4 + j, (*chip, c), sibling) for j, chip in enumerate(chips)]
        for j, chip in enumerate(chips):
            copy(1 + j, (*chip, c), me).wait_recv()
            passed[j].start()
        copy(0, sibling, me).wait_recv()
        for j, chip in enumerate(chips):
            copy(4 + j, (*chip, 1 - c), me).wait_recv()
        for cp in first + passed:
            cp.wait_send()
        mine.wait()
        acc = buf[0]
        for d in range(1, N_DEV):
            acc = acc + buf[d]
        o_ref[...] = acc

    vm = pl.BlockSpec(memory_space=pltpu.VMEM)
    return pl.pallas_call(
        body, name="allreduce_small", in_specs=[vm], out_specs=vm, out_shape=jax.ShapeDtypeStruct(vec.shape, F32),
        scratch_shapes=[pltpu.VMEM((N_DEV, rows, HD), F32), pltpu.SemaphoreType.DMA((7,)), pltpu.SemaphoreType.DMA((7,)),
                        pltpu.SemaphoreType.DMA],
        compiler_params=pltpu.CompilerParams(vmem_limit_bytes=VMEM_LIMIT),
    )(vec)


SMALL_NAMES = ("norm1_g", "sgu_norm_g", "w_spatial", "b_spatial", "conv_w", "a_log", "dt_bias", "o_norm_g", "q_norm_g",
               "k_norm_g", "norm2_g")


def small_params(l, p, conv_full):
    return {"norm1_g": p["norm1_g"][l][None], "sgu_norm_g": p["sgu_norm_g"][l][:, None, :], "w_spatial": p["w_spatial"][l],
            "b_spatial": p["b_spatial"][l][..., None], "conv_w": conv_full[l], "a_log": p["a_log"][l], "dt_bias": p["dt_bias"][l],
            "o_norm_g": p["o_norm_g"][l][None], "q_norm_g": p["q_norm_g"][l][None], "k_norm_g": p["k_norm_g"][l][None],
            "norm2_g": p["norm2_g"][l][None]}


def local_step(x, target, wg, sps):
    saved = []
    for l in range(DEPTH):
        x, s = layer_forward(x, wg[l], sps[l])
        saved.append(s)
    dx, loss = loss_and_grad(x, target)
    bigs, smalls = [None] * DEPTH, [None] * DEPTH
    for l in reversed(range(DEPTH)):
        dx, bigs[l], smalls[l] = layer_backward(dx, wg[l], sps[l], saved[l])
    return loss, dx, bigs, smalls


_PACK_TILE = 8 * HD


def _pack(arrays):
    flat = jnp.concatenate([a.reshape(-1) for a in arrays])
    pad = -flat.shape[0] % _PACK_TILE
    return jnp.pad(flat, (0, pad)).reshape(-1, HD)


def _unpack(packed, shapes):
    flat, out, off = packed.reshape(-1), [], 0
    for shp in shapes:
        n = int(np.prod(shp))
        out.append(flat[off:off + n].reshape(shp))
        off += n
    return out


BIG_NAMES = ("in", "out", "gu", "down")
WEIGHT_ORDER = ("norm1_g", "w_in", "sgu_norm_g", "w_spatial", "b_spatial", "conv_w", "a_log", "dt_bias", "o_norm_g", "q_norm_g",
                "k_norm_g", "w_out", "norm2_g", "w_gate_up", "w_down")


def kernel(x, norm1_g, w_in, sgu_norm_g, w_spatial, b_spatial, conv_w, a_log, dt_bias, o_norm_g, q_norm_g, k_norm_g, w_out, norm2_g, w_gate_up, w_down, loss_target, m_norm1_g, m_w_in, m_sgu_norm_g, m_w_spatial, m_b_spatial, m_conv_w, m_a_log, m_dt_bias, m_o_norm_g, m_q_norm_g, m_k_norm_g, m_w_out, m_norm2_g, m_w_gate_up, m_w_down, v_norm1_g, v_w_in, v_sgu_norm_g, v_w_spatial, v_b_spatial, v_conv_w, v_a_log, v_dt_bias, v_o_norm_g, v_q_norm_g, v_k_norm_g, v_w_out, v_norm2_g, v_w_gate_up, v_w_down):
    w = dict(norm1_g=norm1_g, w_in=w_in, sgu_norm_g=sgu_norm_g, w_spatial=w_spatial, b_spatial=b_spatial, conv_w=conv_w,
             a_log=a_log, dt_bias=dt_bias, o_norm_g=o_norm_g, q_norm_g=q_norm_g, k_norm_g=k_norm_g, w_out=w_out,
             norm2_g=norm2_g, w_gate_up=w_gate_up, w_down=w_down)
    m = dict(norm1_g=m_norm1_g, w_in=m_w_in, sgu_norm_g=m_sgu_norm_g, w_spatial=m_w_spatial, b_spatial=m_b_spatial,
             conv_w=m_conv_w, a_log=m_a_log, dt_bias=m_dt_bias, o_norm_g=m_o_norm_g, q_norm_g=m_q_norm_g, k_norm_g=m_k_norm_g,
             w_out=m_w_out, norm2_g=m_norm2_g, w_gate_up=m_w_gate_up, w_down=m_w_down)
    v = dict(norm1_g=v_norm1_g, w_in=v_w_in, sgu_norm_g=v_sgu_norm_g, w_spatial=v_w_spatial, b_spatial=v_b_spatial,
             conv_w=v_conv_w, a_log=v_a_log, dt_bias=v_dt_bias, o_norm_g=v_o_norm_g, q_norm_g=v_q_norm_g, k_norm_g=v_k_norm_g,
             w_out=v_w_out, norm2_g=v_norm2_g, w_gate_up=v_w_gate_up, w_down=v_w_down)
    chip = (2 * lax.axis_index("x") + lax.axis_index("y")).astype(jnp.int32)
    core = lax.axis_index("c").astype(jnp.int32)

    in_pad = IN_SHARD_PAD - IN_SHARD
    w_in_pad = jnp.pad(w_in, ((0, 0), (0, 0), (0, in_pad)))

    halves_of = lambda a: a.reshape(2, a.shape[0] // 2, a.shape[1])
    bf_halves = lambda a: halves_of(a.astype(BF16))

    def ffn_shards(l):
        return [bf_halves(w_gate_up[l]), bf_halves(w_down[l]), bf_halves(w_out[l])]

    def mixer_shards(l):
        return [bf_halves(w_in_pad[l]), halves_of(conv_w[l])]

    def mixer_weights(g):
        g_in, g_conv = g
        return (shards_to_segments(g_in.reshape(N_CHIPS, D, IN_SHARD_PAD))[None],
                g_conv.reshape(N_CHIPS, B_CONV, -1).transpose(1, 0, 2).reshape(B_CONV, 3 * B_WIDTH))

    def ffn_weights(g, w_in_seg):
        g_gu, g_down, g_out = g
        return {"in": w_in_seg, "out": g_out.reshape(1, D, D), "gu": g_gu.reshape(1, N_CHIPS, D, GU_SHARD),
                "down": g_down.reshape(1, FFN, D)}

    def layer_params(l, conv_full):
        return small_params(0, {n: w[n][l:l + 1] for n in SMALL_NAMES if n != "conv_w"}, conv_full[None])

    w_in0, conv0 = mixer_weights(gather_weights(mixer_shards(0)))
    start_a, wait_a = gather_direct(ffn_shards(0), "ffn0")
    start_b, wait_b = gather_direct(mixer_shards(1), "mix1")
    start_c, wait_c = gather_direct(ffn_shards(1), "ffn1")
    state_a, token_a = start_a()
    state_b, token_b = start_b(token_a)
    state_c, token_c = start_c(token_b)
    sps = [layer_params(0, conv0), None]
    mix0, saved_m0 = mixers_forward(x[0], w_in0, sps[0], dep=token_c)
    wg0 = ffn_weights(pass_to_sibling(wait_a(state_a, mix0)[1]), w_in0)
    x1, saved_f0 = ffn_forward(x[0], mix0, wg0, sps[0])
    w_in1, conv1 = mixer_weights(pass_to_sibling(wait_b(state_b, x1)[1]))
    sps[1] = layer_params(1, conv1)
    mix1, saved_m1 = mixers_forward(x1, w_in1, sps[1])
    wg1 = ffn_weights(pass_to_sibling(wait_c(state_c, mix1)[1]), w_in1)
    x2, saved_f1 = ffn_forward(x1, mix1, wg1, sps[1])
    saved1 = (saved_m1, saved_f1)
    dx, loss_tile = loss_and_grad(x2, loss_target[0])

    def to_chip_parts(grads):
        return [add_sibling(g, r, core.reshape(1)) for g, r in zip(grads, exchange_halves(grads))]

    def start_scatter(grads, tag, dep=None):
        start, wait = scatter_direct(to_chip_parts(grads), tag)
        state, token = start(dep)
        return functools.partial(wait, state), token

    smalls = [None] * DEPTH
    by_chip_out = lambda t: t.reshape(N_CHIPS, OUT_SHARD, D)
    by_chip_down = lambda t: t.reshape(N_CHIPS, DOWN_SHARD, D)
    dx1, dnorm2_1, dw_gu1, dw_down1 = ffn_backward(dx, wg1, sps[1], saved1[1])
    wait_f1, tok_f1 = start_scatter([dw_gu1, by_chip_down(dw_down1)], "ffn1")
    dx, dw_in1, dw_out1, small1 = mixers_backward(dx1, wg1, sps[1], saved1[0], dep=tok_f1)
    smalls[1] = {**small1, "norm2_g": dnorm2_1}
    wait_m1, tok_m1 = start_scatter([dw_in1, by_chip_out(dw_out1)], "mix1")
    dx1, dnorm2_0, dw_gu0, dw_down0 = ffn_backward(dx, wg0, sps[0], saved_f0, dep=tok_m1)
    wait_f0, tok_f0 = start_scatter([dw_gu0, by_chip_down(dw_down0)], "ffn0")
    dx, dw_in0, dw_out0, small0 = mixers_backward(dx1, wg0, sps[0], saved_m0, dep=tok_f0)
    smalls[0] = {**small0, "norm2_g": dnorm2_0}

    grad, delta, new_m, new_v = {}, {}, {}, {}
    stacked = [jnp.stack([smalls[l][n] for l in range(DEPTH)]) for n in SMALL_NAMES]
    total = allreduce_small(_pack(stacked + [loss_tile[0, :1]]))
    shapes = [(DEPTH, B_CONV, 3 * B_WIDTH) if n == "conv_w" else w[n].shape for n in SMALL_NAMES]
    small_grads = dict(zip(SMALL_NAMES, _unpack(total, shapes + [(1,)])[:-1]))
    loss = _unpack(total, shapes + [(1,)])[-1][0]
    conv_cols = conv_w.shape[-1]
    small_grads["conv_w"] = lax.dynamic_slice_in_dim(small_grads["conv_w"], chip * conv_cols, conv_cols, axis=2)
    grad.update(small_grads)
    sshapes = [w[n].shape for n in SMALL_NAMES]
    packed = [_pack([d[n] for n in SMALL_NAMES]) for d in (w, grad, m, v)]
    for dst, t in zip((delta, new_m, new_v), adamw(*packed, "adamw_small")):
        dst.update(zip(SMALL_NAMES, _unpack(t, sshapes)))

    wait_m0, tok_m0 = start_scatter([dw_in0, by_chip_out(dw_out0)], "mix0", dep=total)
    (pf0, rf0), (pm1, rm1), (pf1, rf1) = (wt(tok_m0) for wt in (wait_f0, wait_m1, wait_f1))

    def reduce_group(parts, from_chips):
        mine = [add_chips(p, r, chip.reshape(1)) for p, r in zip(parts, from_chips)]
        return mine, list(share_halves(mine))

    mine_f, theirs_f = reduce_group(pf0 + pf1 + pm1, rf0 + rf1 + rm1)
    for a, n in enumerate(("w_gate_up", "w_down")):
        grad[n], delta[n], new_m[n], new_v[n] = adamw_shard(w[n], m[n], v[n], [mine_f[a], mine_f[2 + a]],
                                                            [theirs_f[a], theirs_f[2 + a]], core.reshape(1), "adamw_" + n)

    mine_m0, theirs_m0 = reduce_group(*wait_m0(new_v["w_down"]))
    tr_ = lambda t: jnp.swapaxes(t, -1, -2)
    cut = lambda t: tr_(t[:, :IN_SHARD])
    res = adamw_shard_t(tr_(w_in), tr_(m_w_in), tr_(v_w_in), [cut(mine_m0[0]), cut(mine_f[4])],
                        [cut(theirs_m0[0]), cut(theirs_f[4])], core.reshape(1), "adamw_w_in")
    grad["w_in"], delta["w_in"], new_m["w_in"], new_v["w_in"] = (tr_(t) for t in res)
    grad["w_out"], delta["w_out"], new_m["w_out"], new_v["w_out"] = adamw_shard(
        w_out, m_w_out, v_w_out, [mine_m0[1], mine_f[5]], [theirs_m0[1], theirs_f[5]], core.reshape(1), "adamw_w_out")

    out = [loss, dx[None]]
    for d in (grad, delta, new_m, new_v):
        out += [d[n] for n in WEIGHT_ORDER]
    return tuple(out)
```

```python
import functools
import math

import numpy as np
import jax
import jax.numpy as jnp
from jax import lax
from jax.experimental import pallas as pl
from jax.experimental.pallas import tpu as pltpu

F32 = jnp.float32
BF16 = jnp.bfloat16
HI = lax.Precision.HIGH

T = 2048
D = 2048
DEPTH = 2
HD = 128
A_GROUPS, A_WIDTH, A_CHUNK = 4, 512, 128
B_HEADS, B_WIDTH, B_CONV, B_CHUNK = 6, 768, 4, 64
C_HEADS, C_WIDTH, C_BLOCK = 6, 768, 128
C_BRANCHES = ((128, 1), (512, 4), (2048, 16))
FFN = 5632
IN_TOTAL = 6412
EPS = 1e-6
N_CHIPS = 4
N_DEV = 8
IN_SHARD = IN_TOTAL // N_CHIPS
IN_SHARD_PAD = 1664
GU_SHARD = 2 * FFN // N_CHIPS
OUT_SHARD = D // N_CHIPS
DOWN_SHARD = FFN // N_CHIPS
P_AU, P_AV, P_BQ, P_BK, P_BV, P_BG, P_BB, P_CQ, P_CK, P_CV, P_END = (
    0, 512, 1024, 1792, 2560, 3328, 4096, 4224, 4992, 5760, 6528)
GATE_COLS = 4108
VMEM_LIMIT = 56 * 1024 * 1024

ADAM_LR, ADAM_B1, ADAM_B2, ADAM_EPS, ADAM_WD, ADAM_STEP = 0.001, 0.9, 0.999, 1e-08, 0.01, 10


def _cparams(sem, vmem=VMEM_LIMIT):
    return pltpu.CompilerParams(dimension_semantics=sem, vmem_limit_bytes=vmem)


def _dims(nd, ta, tb):
    off = nd - 2
    ca = off + (0 if ta else 1)
    cb = off + (1 if tb else 0)
    batch = ((0,), (0,)) if nd == 3 else ((), ())
    return (((ca,), (cb,)), batch)


def _raw_mm(a, b, ta, tb, hi):
    if hi:
        return lax.dot_general(a, b, _dims(a.ndim, ta, tb), precision=HI, preferred_element_type=F32)
    return lax.dot_general(a.astype(BF16), b.astype(BF16), _dims(a.ndim, ta, tb), preferred_element_type=F32)


@functools.partial(jax.custom_vjp, nondiff_argnums=(2, 3, 4))
def _mm(a, b, ta=False, tb=False, hi=False):
    return _raw_mm(a, b, ta, tb, hi)


def _mm_fwd(a, b, ta, tb, hi):
    return _raw_mm(a, b, ta, tb, hi), (a, b)


def _mm_bwd(ta, tb, hi, res, g):
    a, b = res
    da = _raw_mm(g, b, False, not tb, False) if not ta else _raw_mm(b, g, tb, True, False)
    db = _raw_mm(a, g, not ta, False, False) if not tb else _raw_mm(g, a, True, ta, False)
    return da.astype(a.dtype), db.astype(b.dtype)


_mm.defvjp(_mm_fwd, _mm_bwd)


def _rms(x, g):
    return x * lax.rsqrt(jnp.mean(x * x, axis=-1, keepdims=True) + EPS) * g


def _gelu(x):
    return 0.5 * x * (1.0 + jnp.tanh(math.sqrt(2.0 / math.pi) * (x + 0.044715 * (x * x * x))))


def _sigmoid(x):
    return 1.0 / (1.0 + jnp.exp(-x))


def _silu(x):
    return x * _sigmoid(x)


def _softplus(x):
    return jnp.maximum(x, 0.0) + jnp.log(1.0 + jnp.exp(-jnp.abs(x)))


def _iota(shape, dim):
    return lax.broadcasted_iota(jnp.int32, shape, dim)


def _sgu_fn(u, v, sg, w, b):
    nc = T // A_CHUNK
    ug = _gelu(u)
    vn = _rms(_gelu(v), sg)
    causal = _iota((A_CHUNK, A_CHUNK), 0) >= _iota((A_CHUNK, A_CHUNK), 1)
    wm = jnp.where(causal, w, 0.0)
    wb = jnp.broadcast_to(wm[None], (nc, A_CHUNK, A_CHUNK))
    z = _mm(wb, vn.reshape(nc, A_CHUNK, HD)) + b[None]
    return ug * z.reshape(T, HD)


def _sgu_specs():
    col = lambda off: pl.BlockSpec((T, HD), lambda g, off=off: (0, off + g))
    par = [pl.BlockSpec((None, 1, HD), lambda g: (g, 0, 0)),
           pl.BlockSpec((None, A_CHUNK, A_CHUNK), lambda g: (g, 0, 0)),
           pl.BlockSpec((None, A_CHUNK, 1), lambda g: (g, 0, 0))]
    return col, par


def sgu_fwd(p2, sg, w, b):
    col, par = _sgu_specs()

    def body(u_ref, v_ref, sg_ref, w_ref, b_ref, y_ref):
        y_ref[...] = _sgu_fn(u_ref[...], v_ref[...], sg_ref[...], w_ref[...], b_ref[...]).astype(BF16)

    return pl.pallas_call(
        body, name="sgu_fwd", grid=(A_GROUPS,),
        in_specs=[col(P_AU // HD), col(P_AV // HD)] + par,
        out_specs=pl.BlockSpec((T, HD), lambda g: (0, g)),
        out_shape=jax.ShapeDtypeStruct((T, A_WIDTH), BF16),
        compiler_params=_cparams(("arbitrary",)),
    )(p2, p2, sg, w, b)


def sgu_bwd(p2, sg, w, b, dmix):
    col, par = _sgu_specs()

    def body(u_ref, v_ref, sg_ref, w_ref, b_ref, dy_ref, du_ref, dv_ref, dsg_ref, dw_ref, db_ref):
        _, vjp = jax.vjp(_sgu_fn, u_ref[...], v_ref[...], sg_ref[...], w_ref[...], b_ref[...])
        du, dv, dsg, dw, db = vjp(dy_ref[...])
        du_ref[...] = du.astype(BF16)
        dv_ref[...] = dv.astype(BF16)
        dsg_ref[...] = dsg
        dw_ref[...] = dw
        db_ref[...] = db

    gcol = pl.BlockSpec((T, HD), lambda g: (0, g))
    return pl.pallas_call(
        body, name="sgu_bwd", grid=(A_GROUPS,),
        in_specs=[col(P_AU // HD), col(P_AV // HD)] + par + [gcol],
        out_specs=[gcol, gcol] + par,
        out_shape=[jax.ShapeDtypeStruct((T, A_WIDTH), BF16), jax.ShapeDtypeStruct((T, A_WIDTH), BF16),
                   jax.ShapeDtypeStruct((A_GROUPS, 1, HD), F32), jax.ShapeDtypeStruct((A_GROUPS, A_CHUNK, A_CHUNK), F32),
                   jax.ShapeDtypeStruct((A_GROUPS, A_CHUNK, 1), F32)],
        compiler_params=_cparams(("arbitrary",)),
    )(p2, p2, sg, w, b, dmix)


def _attn_fn(q, k, v, qg, kg, slope, *, dil, nb):
    n = T // C_BLOCK
    qb = _rms(q, qg).reshape(n, C_BLOCK, HD)
    kb = _rms(k, kg).reshape(n, C_BLOCK, HD)
    vb = v.reshape(n, C_BLOCK, HD)
    scale = HD ** -0.5
    qi = _iota((n, C_BLOCK, C_BLOCK), 1)
    kj = _iota((n, C_BLOCK, C_BLOCK), 2)
    sl = slope[None] * float(dil)
    d_cur = qi - kj
    sc = jnp.where(d_cur >= 0, _mm(qb, kb, tb=True) * scale - sl * d_cur.astype(F32), -jnp.inf)
    mx = jnp.max(sc, axis=-1, keepdims=True)
    if nb > 1:
        kp = jnp.concatenate([jnp.zeros((1, C_BLOCK, HD), F32), kb[:-1]], axis=0)
        vp = jnp.concatenate([jnp.zeros((1, C_BLOCK, HD), F32), vb[:-1]], axis=0)
        has_prev = (_iota((n, C_BLOCK, C_BLOCK), 0) % nb) > 0
        d_prev = C_BLOCK + qi - kj
        sp = jnp.where((kj >= qi) & has_prev, _mm(qb, kp, tb=True) * scale - sl * d_prev.astype(F32), -jnp.inf)
        mx = jnp.maximum(mx, jnp.max(sp, axis=-1, keepdims=True))
    p = jnp.exp(sc - mx)
    den = jnp.sum(p, axis=-1, keepdims=True)
    if nb > 1:
        pp = jnp.exp(sp - mx)
        den = den + jnp.sum(pp, axis=-1, keepdims=True)
    out = _mm(p / den, vb)
    if nb > 1:
        out = out + _mm(pp / den, vp)
    lse = mx + jnp.log(den)
    return out.reshape(T, HD), jnp.broadcast_to(lse, (n, C_BLOCK, HD)).reshape(T, HD)


def _combine_fn(o1, o2, o3, l1, l2, l3):
    mx = jnp.maximum(jnp.maximum(l1, l2), l3)
    e1, e2, e3 = jnp.exp(l1 - mx), jnp.exp(l2 - mx), jnp.exp(l3 - mx)
    s = e1 + e2 + e3
    return (e1 / s) * o1 + (e2 / s) * o2 + (e3 / s) * o3


def _branch_blocks(dil):
    return -(-(T // dil) // C_BLOCK)


def _load_branch_order(ref, dil):
    if dil == 1:
        return ref[...]
    seg = T // dil
    return jnp.concatenate([ref[pl.ds(r, seg, stride=dil), :] for r in range(dil)], axis=0)


def _store_position_order(ref, val, dil, add=False):
    seg = T // dil
    for r in range(dil):
        rows = slice(None) if dil == 1 else pl.ds(r, seg, stride=dil)
        piece = val if dil == 1 else val[r * seg:(r + 1) * seg]
        if add:
            ref[rows, :] += piece
        else:
            ref[rows, :] = piece


def _dattn_specs():
    col = lambda off: pl.BlockSpec((T, HD), lambda h, off=off: (0, off // HD + h))
    row = pl.BlockSpec((1, HD), lambda h: (0, 0))
    slope = pl.BlockSpec((None, 1, HD), lambda h: (h, 0, 0))
    return [col(P_CQ), col(P_CK), col(P_CV), row, row, slope]


def _dattn_branches(q_ref, k_ref, v_ref, qg, kg, slope, o_scr, l_scr):
    for b, (_, dil) in enumerate(C_BRANCHES):
        q, k, v = (_load_branch_order(r, dil) for r in (q_ref, k_ref, v_ref))
        o, l = _attn_fn(q, k, v, qg, kg, slope, dil=dil, nb=_branch_blocks(dil))
        _store_position_order(o_scr.at[b], o, dil)
        _store_position_order(l_scr.at[b], l, dil)


def dattn_fwd(p2, qg, kg, slopes):
    def body(q_ref, k_ref, v_ref, qg_ref, kg_ref, s_ref, y_ref, o_scr, l_scr):
        _dattn_branches(q_ref, k_ref, v_ref, qg_ref[...], kg_ref[...], s_ref[...], o_scr, l_scr)
        y_ref[...] = _combine_fn(o_scr[0], o_scr[1], o_scr[2], l_scr[0], l_scr[1], l_scr[2]).astype(BF16)

    return pl.pallas_call(
        body, name="dattn_fwd", grid=(C_HEADS,), in_specs=_dattn_specs(), out_specs=pl.BlockSpec((T, HD), lambda h: (0, h)),
        out_shape=jax.ShapeDtypeStruct((T, C_WIDTH), BF16),
        scratch_shapes=[pltpu.VMEM((3, T, HD), F32), pltpu.VMEM((3, T, HD), F32)], compiler_params=_cparams(("arbitrary",)),
    )(p2, p2, p2, qg, kg, slopes)


def dattn_bwd(p2, qg, kg, slopes, dmix):
    hcol = pl.BlockSpec((T, HD), lambda h: (0, h))
    row = pl.BlockSpec((1, HD), lambda h: (0, 0))
    dy = pl.BlockSpec((T, HD), lambda h: (0, (A_WIDTH + B_WIDTH) // HD + h))

    def body(q_ref, k_ref, v_ref, qg_ref, kg_ref, s_ref, dy_ref, dq_ref, dk_ref, dv_ref, dqg_ref, dkg_ref, o_scr, l_scr, g_scr,
             acc):
        qg, kg, slope = qg_ref[...], kg_ref[...], s_ref[...]
        _dattn_branches(q_ref, k_ref, v_ref, qg, kg, slope, o_scr, l_scr)
        _, vjp = jax.vjp(_combine_fn, o_scr[0], o_scr[1], o_scr[2], l_scr[0], l_scr[1], l_scr[2])
        for i, g in enumerate(vjp(dy_ref[...])):
            g_scr[i] = g

        @pl.when(pl.program_id(0) == 0)
        def _():
            dqg_ref[...] = jnp.zeros_like(dqg_ref)
            dkg_ref[...] = jnp.zeros_like(dkg_ref)

        for b, (_, dil) in enumerate(C_BRANCHES):
            q, k, v = (_load_branch_order(r, dil) for r in (q_ref, k_ref, v_ref))
            do, dl = _load_branch_order(g_scr.at[b], dil), _load_branch_order(g_scr.at[3 + b], dil)
            fn = functools.partial(_attn_fn, dil=dil, nb=_branch_blocks(dil))
            _, vjp_b = jax.vjp(lambda a, b_, c, d, e, fn=fn: fn(a, b_, c, d, e, slope), q, k, v, qg, kg)
            dq, dk, dv, dqg, dkg = vjp_b((do, dl))
            for i, val in enumerate((dq, dk, dv)):
                _store_position_order(acc.at[i], val, dil, add=b > 0)
            dqg_ref[...] += dqg
            dkg_ref[...] += dkg
        for i, ref in enumerate((dq_ref, dk_ref, dv_ref)):
            ref[...] = acc[i].astype(BF16)

    scr = lambda n: pltpu.VMEM((n, T, HD), F32)
    return pl.pallas_call(
        body, name="dattn_bwd", grid=(C_HEADS,), in_specs=_dattn_specs() + [dy], out_specs=[hcol, hcol, hcol, row, row],
        out_shape=[jax.ShapeDtypeStruct((T, C_WIDTH), BF16)] * 3 + [jax.ShapeDtypeStruct((1, HD), F32)] * 2,
        scratch_shapes=[scr(3), scr(3), scr(6), scr(3)], compiler_params=_cparams(("arbitrary",)),
    )(p2, p2, p2, qg, kg, slopes, dmix)


_NCH = T // B_CHUNK


def _conv_taps(x, w_ref):
    rows = _iota(x.shape, 0)
    taps = []
    for j in range(B_CONV):
        s = B_CONV - 1 - j
        taps.append(x if s == 0 else jnp.where(rows >= s, pltpu.roll(x, s, 0), 0.0))
    pre = sum(w_ref[j:j + 1, :] * taps[j] for j in range(B_CONV))
    return pre, taps


def _conv_post(pre, mode):
    y = _silu(pre)
    if mode == "v":
        return y
    y = y * lax.rsqrt(jnp.sum(y * y, axis=-1, keepdims=True) + EPS)
    return y * (HD ** -0.5) if mode == "q" else y


def conv_fwd(p2, conv_w, mode):
    idx = "qkv".index(mode)
    xcol = pl.BlockSpec((T, HD), lambda h: (0, P_BQ // HD + B_HEADS * idx + h))
    wcol = pl.BlockSpec((B_CONV, HD), lambda h: (0, B_HEADS * idx + h))
    hcol = pl.BlockSpec((T, HD), lambda h: (0, h))

    def body(x_ref, w_ref, y_ref):
        pre, _ = _conv_taps(x_ref[...], w_ref)
        y_ref[...] = _conv_post(pre, mode)

    return pl.pallas_call(
        body, name=f"conv_fwd_{mode}", grid=(B_HEADS,), in_specs=[xcol, wcol], out_specs=hcol,
        out_shape=jax.ShapeDtypeStruct((T, B_WIDTH), F32), compiler_params=_cparams(("arbitrary",)),
    )(p2, conv_w)


def conv_bwd(p2, conv_w, dys, mode):
    idx = "qkv".index(mode)
    xcol = pl.BlockSpec((T, HD), lambda h: (0, P_BQ // HD + B_HEADS * idx + h))
    wcol = pl.BlockSpec((B_CONV, HD), lambda h: (0, B_HEADS * idx + h))
    hcol = pl.BlockSpec((T, HD), lambda h: (0, h))
    wout = pl.BlockSpec((B_CONV, HD), lambda h: (0, h))

    def body(x_ref, w_ref, *rest):
        dy_refs, (dx_ref, dw_ref) = rest[:-2], rest[-2:]
        pre, taps = _conv_taps(x_ref[...], w_ref)
        _, vjp = jax.vjp(functools.partial(_conv_post, mode=mode), pre)
        (dpre,) = vjp(sum(r[...] for r in dy_refs))
        rows = _iota(dpre.shape, 0)
        dx = w_ref[B_CONV - 1:B_CONV, :] * dpre
        for j in range(B_CONV):
            s = B_CONV - 1 - j
            dw_ref[j:j + 1, :] = jnp.sum(dpre * taps[j], axis=0, keepdims=True)
            if s > 0:
                dx = dx + w_ref[j:j + 1, :] * jnp.where(rows < T - s, pltpu.roll(dpre, T - s, 0), 0.0)
        dx_ref[...] = dx.astype(BF16)

    return pl.pallas_call(
        body, name=f"conv_bwd_{mode}", grid=(B_HEADS,), in_specs=[xcol, wcol] + [hcol] * len(dys), out_specs=[hcol, wout],
        out_shape=[jax.ShapeDtypeStruct((T, B_WIDTH), BF16), jax.ShapeDtypeStruct((B_CONV, B_WIDTH), F32)],
        compiler_params=_cparams(("arbitrary",)),
    )(p2, conv_w, *dys)


def _gates_fn(bg, al, dtb, h):
    r = _iota((HD, HD), 0)
    logit = _mm(bg, (r == h).astype(F32), hi=True)
    a = _mm(bg, (r == h + B_HEADS).astype(F32), hi=True)
    beta = _sigmoid(logit)
    graw = -jnp.exp(al) * _softplus(a + dtb)
    tri = (_iota((_NCH, B_CHUNK, B_CHUNK), 1) >= _iota((_NCH, B_CHUNK, B_CHUNK), 2)).astype(F32)
    g = _mm(tri, graw.reshape(_NCH, B_CHUNK, HD), hi=True).reshape(T, HD)
    return beta, g


def _gates_specs():
    bg = pl.BlockSpec((T, HD), lambda h: (0, P_BB // HD))
    par = pl.BlockSpec((None, 1, HD), lambda h: (h, 0, 0))
    out = pl.BlockSpec((None, T, HD), lambda h: (h, 0, 0))
    return bg, par, out


def gates_fwd(p2, al, dtb):
    bg, par, out = _gates_specs()

    def body(bg_ref, al_ref, dtb_ref, beta_ref, g_ref):
        beta, g = _gates_fn(bg_ref[...], al_ref[...], dtb_ref[...], pl.program_id(0))
        beta_ref[...] = beta
        g_ref[...] = g

    return pl.pallas_call(
        body, name="gates_fwd", grid=(B_HEADS,), in_specs=[bg, par, par], out_specs=[out, out],
        out_shape=[jax.ShapeDtypeStruct((B_HEADS, T, HD), F32)] * 2, compiler_params=_cparams(("arbitrary",)),
    )(p2, al, dtb)


def gates_bwd(p2, al, dtb, dbeta, dg1, dg2):
    bg, par, out = _gates_specs()
    acc = pl.BlockSpec((T, HD), lambda h: (0, 0))

    def body(bg_ref, al_ref, dtb_ref, dbeta_ref, dg1_ref, dg2_ref, dbg_ref, dal_ref, ddtb_ref, acc_ref):
        h = pl.program_id(0)
        _, vjp = jax.vjp(lambda a, b, c: _gates_fn(a, b, c, h), bg_ref[...], al_ref[...], dtb_ref[...])
        dbg, dal, ddtb = vjp((dbeta_ref[...], dg1_ref[...] + dg2_ref[...]))

        @pl.when(h == 0)
        def _():
            acc_ref[...] = jnp.zeros_like(acc_ref)

        acc_ref[...] += dbg
        dbg_ref[...] = acc_ref[...].astype(BF16)
        dal_ref[...] = jnp.broadcast_to(jnp.sum(dal, axis=-1, keepdims=True), (1, HD))
        ddtb_ref[...] = jnp.broadcast_to(jnp.sum(ddtb, axis=-1, keepdims=True), (1, HD))

    return pl.pallas_call(
        body, name="gates_bwd", grid=(B_HEADS,), in_specs=[bg, par, par, out, out, out], out_specs=[acc, par, par],
        out_shape=[jax.ShapeDtypeStruct((T, HD), BF16)] + [jax.ShapeDtypeStruct((B_HEADS, 1, HD), F32)] * 2,
        scratch_shapes=[pltpu.VMEM((T, HD), F32)], compiler_params=_cparams(("arbitrary",)),
    )(p2, al, dtb, dbeta, dg1, dg2)


def _unit_lower_inverse(a):
    eye = (_iota(a.shape, 1) == _iota(a.shape, 2)).astype(F32)
    x = eye - a
    p = _mm(a, a, hi=True)
    for i in range(5):
        x = x + _mm(x, p, hi=True)
        if i < 4:
            p = _mm(p, p, hi=True)
    return x


_WY_CH = 8
_WY_ROWS = _WY_CH * B_CHUNK


def _wy_fn(q, k, v, beta, g):
    sh = (q.shape[0] // B_CHUNK, B_CHUNK, HD)
    q3, k3, v3, b3, g3 = (t.reshape(sh) for t in (q, k, v, beta, g))
    gd = g3[:, :, :B_CHUNK] - jnp.swapaxes(g3, 1, 2)[:, :B_CHUNK, :]
    ii, jj = _iota(gd.shape, 1), _iota(gd.shape, 2)
    decay = jnp.exp(jnp.where(ii >= jj, gd, -jnp.inf))
    kb = k3 * b3
    a = _mm(kb, k3, tb=True) * jnp.where(ii > jj, decay, 0.0)
    tinv = _unit_lower_inverse(a)
    u = _mm(tinv, v3 * b3, hi=True)
    w = _mm(tinv, kb * jnp.exp(g3), hi=True)
    attn = _mm(q3, k3, tb=True) * decay
    return u.reshape(q.shape), w.reshape(q.shape), attn


def _wy_specs():
    hcol = pl.BlockSpec((_WY_ROWS, HD), lambda h, i: (i, h))
    hb = pl.BlockSpec((None, _WY_ROWS, HD), lambda h, i: (h, i, 0))
    at = pl.BlockSpec((None, _WY_CH, B_CHUNK, B_CHUNK), lambda h, i: (h, i, 0, 0))
    return hcol, hb, at


_WY_GRID = (B_HEADS, _NCH // _WY_CH)


def wy_fwd(q, k, v, beta, g):
    hcol, hb, at = _wy_specs()

    def body(q_ref, k_ref, v_ref, b_ref, g_ref, u_ref, w_ref, a_ref):
        u, w, a = _wy_fn(q_ref[...], k_ref[...], v_ref[...], b_ref[...], g_ref[...])
        u_ref[...] = u
        w_ref[...] = w
        a_ref[...] = a

    return pl.pallas_call(
        body, name="wy_fwd", grid=_WY_GRID, in_specs=[hcol, hcol, hcol, hb, hb], out_specs=[hcol, hcol, at],
        out_shape=[jax.ShapeDtypeStruct((T, B_WIDTH), F32)] * 2 + [jax.ShapeDtypeStruct((B_HEADS, _NCH, B_CHUNK, B_CHUNK), F32)],
        compiler_params=_cparams(("arbitrary", "arbitrary")),
    )(q, k, v, beta, g)


def wy_bwd(q, k, v, beta, g, du, dw, dattn):
    hcol, hb, at = _wy_specs()

    def body(q_ref, k_ref, v_ref, b_ref, g_ref, du_ref, dw_ref, da_ref, dq_ref, dk_ref, dv_ref, db_ref, dg_ref):
        _, vjp = jax.vjp(_wy_fn, q_ref[...], k_ref[...], v_ref[...], b_ref[...], g_ref[...])
        for r, t in zip((dq_ref, dk_ref, dv_ref, db_ref, dg_ref), vjp((du_ref[...], dw_ref[...], da_ref[...]))):
            r[...] = t

    return pl.pallas_call(
        body, name="wy_bwd", grid=_WY_GRID, in_specs=[hcol, hcol, hcol, hb, hb, hcol, hcol, at],
        out_specs=[hcol, hcol, hcol, hb, hb],
        out_shape=[jax.ShapeDtypeStruct((T, B_WIDTH), F32)] * 3 + [jax.ShapeDtypeStruct((B_HEADS, T, HD), F32)] * 2,
        compiler_params=_cparams(("arbitrary", "arbitrary")),
    )(q, k, v, beta, g, du, dw, dattn)


def _scan_step_fn(q, k, u, w, g, attn, gate, og, s):
    v_new = u - _mm(w, s)
    o = _mm(q * jnp.exp(g), s) + _mm(attn, v_new)
    g_last = jnp.sum(jnp.where(_iota(g.shape, 0) == B_CHUNK - 1, g, 0.0), axis=0, keepdims=True)
    s_new = s * jnp.exp(g_last) + _mm(k * jnp.exp(g_last - g), v_new, ta=True)
    return _rms(o, og) * _silu(gate), s_new


def _scan_specs(rev):
    ch = (lambda n: _NCH - 1 - n) if rev else (lambda n: n)
    rows = pl.BlockSpec((B_CHUNK, B_WIDTH), lambda n: (ch(n), 0))
    gb = pl.BlockSpec((B_HEADS, B_CHUNK, HD), lambda n: (0, ch(n), 0))
    at = pl.BlockSpec((B_HEADS, None, B_CHUNK, B_CHUNK), lambda n: (0, ch(n), 0, 0))
    og = pl.BlockSpec((1, HD), lambda n: (0, 0))
    st = pl.BlockSpec((None, B_HEADS, HD, HD), lambda n: (ch(n), 0, 0, 0))
    return rows, gb, at, og, st


def scan_fwd(q, k, u, w, g, attn, gate, og):
    rows, gb, at, ogs, st = _scan_specs(False)

    def body(q_ref, k_ref, u_ref, w_ref, g_ref, a_ref, gate_ref, og_ref, y_ref, st_ref, s_ref):
        @pl.when(pl.program_id(0) == 0)
        def _():
            s_ref[...] = jnp.zeros_like(s_ref)

        for h in range(B_HEADS):
            c = slice(h * HD, (h + 1) * HD)
            s = s_ref[h]
            st_ref[h] = s
            y, s_new = _scan_step_fn(q_ref[:, c], k_ref[:, c], u_ref[:, c], w_ref[:, c], g_ref[h], a_ref[h],
                                     gate_ref[:, c], og_ref[...], s)
            y_ref[:, c] = y.astype(BF16)
            s_ref[h] = s_new

    return pl.pallas_call(
        body, name="scan_fwd", grid=(_NCH,), in_specs=[rows, rows, rows, rows, gb, at, rows, ogs], out_specs=[rows, st],
        out_shape=[jax.ShapeDtypeStruct((T, B_WIDTH), BF16), jax.ShapeDtypeStruct((_NCH, B_HEADS, HD, HD), F32)],
        scratch_shapes=[pltpu.VMEM((B_HEADS, HD, HD), F32)], compiler_params=_cparams(("arbitrary",)),
    )(q, k, u, w, g, attn, gate, og)


def scan_bwd(q, k, u, w, g, attn, gate, og, states, dmix):
    rows, gb, at, ogs, st = _scan_specs(True)
    dyb = pl.BlockSpec((B_CHUNK, HD), lambda n: (_NCH - 1 - n, 0))

    def body(q_ref, k_ref, u_ref, w_ref, g_ref, a_ref, gate_ref, og_ref, st_ref, *rest):
        dy_refs, (dq_ref, dk_ref, du_ref, dw_ref, dgate_ref, dg_ref, da_ref, dog_ref, ds_ref) = rest[:B_HEADS], rest[B_HEADS:]

        @pl.when(pl.program_id(0) == 0)
        def _():
            ds_ref[...] = jnp.zeros_like(ds_ref)
            dog_ref[...] = jnp.zeros_like(dog_ref)

        for h in range(B_HEADS):
            c = slice(h * HD, (h + 1) * HD)
            _, vjp = jax.vjp(_scan_step_fn, q_ref[:, c], k_ref[:, c], u_ref[:, c], w_ref[:, c], g_ref[h], a_ref[h],
                             gate_ref[:, c], og_ref[...], st_ref[h])
            dq, dk, du, dw, dg, da, dgate, dog, ds = vjp((dy_refs[h][...], ds_ref[h]))
            dq_ref[:, c] = dq
            dk_ref[:, c] = dk
            du_ref[:, c] = du
            dw_ref[:, c] = dw
            dgate_ref[:, c] = dgate.astype(BF16)
            dg_ref[h] = dg
            da_ref[h] = da
            dog_ref[...] += dog
            ds_ref[h] = ds

    dy_specs = [pl.BlockSpec((B_CHUNK, HD), lambda n, h=h: (_NCH - 1 - n, A_WIDTH // HD + h)) for h in range(B_HEADS)]
    return pl.pallas_call(
        body, name="scan_bwd", grid=(_NCH,),
        in_specs=[rows, rows, rows, rows, gb, at, rows, ogs, st] + dy_specs,
        out_specs=[rows] * 5 + [gb, at, ogs],
        out_shape=[jax.ShapeDtypeStruct((T, B_WIDTH), F32)] * 4 + [jax.ShapeDtypeStruct((T, B_WIDTH), BF16)]
        + [jax.ShapeDtypeStruct((B_HEADS, T, HD), F32), jax.ShapeDtypeStruct((B_HEADS, _NCH, B_CHUNK, B_CHUNK), F32),
           jax.ShapeDtypeStruct((1, HD), F32)],
        scratch_shapes=[pltpu.VMEM((B_HEADS, HD, HD), F32)], compiler_params=_cparams(("arbitrary",)),
    )(q, k, u, w, g, attn, gate, og, states, *([dmix] * B_HEADS))


def _lanes(vec):
    return jnp.broadcast_to(vec[:, None, None], (vec.shape[0], 1, HD))


def gdn_forward(p2, conv_w, a_log, dt_bias, og):
    qa, ka, va = (conv_fwd(p2, conv_w, m) for m in "qkv")
    beta, g = gates_fwd(p2, _lanes(a_log), _lanes(dt_bias))
    u, w, attn = wy_fwd(qa, ka, va, beta, g)
    gate = p2[:, P_BG:P_BB]
    y, states = scan_fwd(qa, ka, u, w, g, attn, gate, og)
    return y, (qa, ka, va, beta, g, u, w, attn, gate, states)


def gdn_backward(p2, conv_w, a_log, dt_bias, og, saved, dmix):
    qa, ka, va, beta, g, u, w, attn, gate, states = saved
    dq1, dk1, du, dw, dgate, dg1, dattn, dog = scan_bwd(qa, ka, u, w, g, attn, gate, og, states, dmix)
    dq2, dk2, dv, dbeta, dg2 = wy_bwd(qa, ka, va, beta, g, du, dw, dattn)
    dbg, dal, ddtb = gates_bwd(p2, _lanes(a_log), _lanes(dt_bias), dbeta, dg1, dg2)
    dxq, dwq = conv_bwd(p2, conv_w, [dq1, dq2], "q")
    dxk, dwk = conv_bwd(p2, conv_w, [dk1, dk2], "k")
    dxv, dwv = conv_bwd(p2, conv_w, [dv], "v")
    return [dxq, dxk, dxv, dgate, dbg], jnp.concatenate([dwq, dwk, dwv], axis=1), dal[:, 0, 0], ddtb[:, 0, 0], dog


_SLOPES = np.exp2(-8.0 * (np.arange(C_HEADS, dtype=np.float64) + 1.0) / C_HEADS).astype(np.float32)


def _alibi_slopes():
    return _lanes(jnp.asarray(_SLOPES))


_ROWS = 256
_TM = 1024
_TM_FFN = 512


def _dep_specs(dep, ngrid):
    if dep is None:
        return [], []
    return [dep], [pl.BlockSpec((8, HD), lambda *_: (0, 0))]


def rmsnorm_fwd(x, g, dep=None):
    blk = pl.BlockSpec((_ROWS, D), lambda i: (i, 0))
    deps, dspecs = _dep_specs(dep, 1)

    def body(x_ref, g_ref, *rest):
        rest[-1][...] = _rms(x_ref[...], g_ref[...]).astype(BF16)

    return pl.pallas_call(
        body, name="rmsnorm_fwd", grid=(T // _ROWS,), in_specs=[blk, pl.BlockSpec((1, D), lambda i: (0, 0))] + dspecs,
        out_specs=blk, out_shape=jax.ShapeDtypeStruct((T, D), BF16), compiler_params=_cparams(("arbitrary",)),
    )(x, g, *deps)


def rmsnorm_bwd(x, g, dh, dres):
    blk = pl.BlockSpec((_ROWS, D), lambda i: (i, 0))
    row = pl.BlockSpec((1, D), lambda i: (0, 0))

    def body(x_ref, g_ref, dh_ref, dres_ref, dx_ref, dg_ref):
        _, vjp = jax.vjp(_rms, x_ref[...], g_ref[...])
        dx, dg = vjp(dh_ref[...])
        dx_ref[...] = dres_ref[...] + dx

        @pl.when(pl.program_id(0) == 0)
        def _():
            dg_ref[...] = jnp.zeros_like(dg_ref)

        dg_ref[...] += dg

    return pl.pallas_call(
        body, name="rmsnorm_bwd", grid=(T // _ROWS,), in_specs=[blk, row, blk, blk], out_specs=[blk, row],
        out_shape=[jax.ShapeDtypeStruct((T, D), F32), jax.ShapeDtypeStruct((1, D), F32)],
        compiler_params=_cparams(("arbitrary",)),
    )(x, g, dh, dres)


def _matmul(name, a, b, *, grid, a_spec, b_spec, o_spec, out_shape, ta=False, tb=False, k_axis=None, res=None, dep=None):
    dims = _dims(2, ta, tb)
    deps, dspecs = _dep_specs(dep, len(grid))

    def body(a_ref, b_ref, *rest):
        o_ref = rest[-1]
        prod = lax.dot_general(a_ref[...].astype(BF16), b_ref[...].astype(BF16), dims, preferred_element_type=F32)
        if res is not None:
            prod = prod + rest[0][...]
        if k_axis is None:
            o_ref[...] = prod.astype(o_ref.dtype)
        else:
            @pl.when(pl.program_id(k_axis) == 0)
            def _():
                o_ref[...] = prod

            @pl.when(pl.program_id(k_axis) > 0)
            def _():
                o_ref[...] += prod

    sem = tuple("arbitrary" for _ in grid)
    ins = [a, b] + ([res] if res is not None else []) + deps
    specs = [a_spec, b_spec] + ([o_spec] if res is not None else []) + dspecs
    return pl.pallas_call(
        body, name=name, grid=grid, in_specs=specs, out_specs=o_spec, out_shape=out_shape, compiler_params=_cparams(sem),
    )(*ins)


_IN_TN = P_END // 3


def mm_proj(h1, wp_in, l):
    return _matmul(
        "mm_proj", h1, wp_in, grid=(P_END // _IN_TN, T // _TM),
        a_spec=pl.BlockSpec((_TM, D), lambda j, i: (i, 0)),
        b_spec=pl.BlockSpec((None, D, _IN_TN), lambda j, i: (l, 0, j)),
        o_spec=pl.BlockSpec((_TM, _IN_TN), lambda j, i: (i, j)), out_shape=jax.ShapeDtypeStruct((T, P_END), F32))


def mm_dh1(dp2, wp_in, l):
    return _matmul(
        "mm_dh1", dp2, wp_in, grid=(T // _TM, P_END // _IN_TN), tb=True, k_axis=1,
        a_spec=pl.BlockSpec((_TM, _IN_TN), lambda i, k: (i, k)),
        b_spec=pl.BlockSpec((None, D, _IN_TN), lambda i, k: (l, 0, k)),
        o_spec=pl.BlockSpec((_TM, D), lambda i, k: (i, 0)), out_shape=jax.ShapeDtypeStruct((T, D), F32))


def mm_dwin(h1, dp2):
    return _matmul(
        "mm_dwin", h1, dp2, grid=(P_END // _IN_TN, D // _TM), ta=True,
        a_spec=pl.BlockSpec((T, _TM), lambda j, i: (0, i)),
        b_spec=pl.BlockSpec((T, _IN_TN), lambda j, i: (0, j)),
        o_spec=pl.BlockSpec((_TM, _IN_TN), lambda j, i: (i, j)), out_shape=jax.ShapeDtypeStruct((D, P_END), BF16))


def _mm_square(name, a, w, l, res, tb, dep=None):
    tn = 1024
    b_spec = (pl.BlockSpec((None, tn, D), lambda j, i: (l, j, 0)) if tb else pl.BlockSpec((None, D, tn), lambda j, i: (l, 0, j)))
    return _matmul(
        name, a, w, grid=(D // tn, T // _TM), tb=tb, res=res, dep=dep,
        a_spec=pl.BlockSpec((_TM, D), lambda j, i: (i, 0)), b_spec=b_spec,
        o_spec=pl.BlockSpec((_TM, tn), lambda j, i: (i, j)), out_shape=jax.ShapeDtypeStruct((T, D), F32))


def mm_out(mix, wg_out, l, x):
    return _mm_square("mm_out", mix, wg_out, l, x, False)


def mm_dmix(dx1, wg_out, l, dep=None):
    return _mm_square("mm_dmix", dx1, wg_out, l, None, True, dep)


def mm_dwout(mix, dx1):
    tn = 1024
    return _matmul(
        "mm_dwout", mix, dx1, grid=(D // tn, D // _TM), ta=True,
        a_spec=pl.BlockSpec((T, _TM), lambda j, i: (0, i)), b_spec=pl.BlockSpec((T, tn), lambda j, i: (0, j)),
        o_spec=pl.BlockSpec((_TM, tn), lambda j, i: (i, j)), out_shape=jax.ShapeDtypeStruct((D, D), BF16))


_GU_TN = GU_SHARD // 2


_GU_NJ = FFN // _GU_TN


def mm_dh2(dgu, wg_gu, l):
    return _matmul(
        "mm_dh2", dgu, wg_gu, grid=(T // _TM, 2 * N_CHIPS), tb=True, k_axis=1,
        a_spec=pl.BlockSpec((None, _TM, _GU_TN), lambda i, k: (k // _GU_NJ, i, k % _GU_NJ)),
        b_spec=pl.BlockSpec((None, None, D, _GU_TN), lambda i, k: (l, k // 2, 0, k % 2)),
        o_spec=pl.BlockSpec((_TM, D), lambda i, k: (i, 0)), out_shape=jax.ShapeDtypeStruct((T, D), F32))


def mm_dwgu(h2, dgu):
    return _matmul(
        "mm_dwgu", h2, dgu, grid=(N_CHIPS, 2, D // _TM), ta=True,
        a_spec=pl.BlockSpec((T, _TM), lambda s, j, i: (0, i)),
        b_spec=pl.BlockSpec((None, T, _GU_TN), lambda s, j, i: ((2 * s + j) // _GU_NJ, 0, (2 * s + j) % _GU_NJ)),
        o_spec=pl.BlockSpec((None, _TM, _GU_TN), lambda s, j, i: (s, i, j)),
        out_shape=jax.ShapeDtypeStruct((N_CHIPS, D, GU_SHARD), BF16))


def mm_down(act, wg_down, l, x1):
    tn = 512
    return _matmul(
        "mm_down", act, wg_down, grid=(D // tn, T // _TM), res=x1,
        a_spec=pl.BlockSpec((_TM, FFN), lambda j, i: (i, 0)),
        b_spec=pl.BlockSpec((None, FFN, tn), lambda j, i: (l, 0, j)),
        o_spec=pl.BlockSpec((_TM, tn), lambda j, i: (i, j)), out_shape=jax.ShapeDtypeStruct((T, D), F32))


def mm_dwdown(act, dx2):
    tm, tn = DOWN_SHARD, 512
    return _matmul(
        "mm_dwdown", act, dx2, grid=(D // tn, FFN // tm), ta=True,
        a_spec=pl.BlockSpec((T, tm), lambda j, i: (0, i)), b_spec=pl.BlockSpec((T, tn), lambda j, i: (0, j)),
        o_spec=pl.BlockSpec((tm, tn), lambda j, i: (i, j)), out_shape=jax.ShapeDtypeStruct((FFN, D), BF16))


_FF_TN = 1408


def _swiglu_fn(gt, up):
    return _silu(gt) * up


def _gate_up_specs():
    gate = pl.BlockSpec((None, None, D, _FF_TN), lambda j, i: (0, j // 2, 0, j % 2))
    up = pl.BlockSpec((None, None, D, _FF_TN), lambda j, i: (0, N_CHIPS // 2 + j // 2, 0, j % 2))
    both = pl.BlockSpec((2, _TM_FFN, _FF_TN), lambda j, i: (0, i, j))
    return gate, up, both


def mm_gu_swiglu(h2, wg_gu):
    gate, up, both = _gate_up_specs()

    def body(h_ref, wg_ref, wu_ref, gu_ref, act_ref):
        h = h_ref[...]
        gt = jnp.dot(h, wg_ref[...], preferred_element_type=F32)
        u = jnp.dot(h, wu_ref[...], preferred_element_type=F32)
        gu_ref[0] = gt
        gu_ref[1] = u
        act_ref[...] = _swiglu_fn(gt, u).astype(BF16)

    return pl.pallas_call(
        body, name="mm_gu_swiglu", grid=(FFN // _FF_TN, T // _TM_FFN),
        in_specs=[pl.BlockSpec((_TM_FFN, D), lambda j, i: (i, 0)), gate, up],
        out_specs=[both, pl.BlockSpec((_TM_FFN, _FF_TN), lambda j, i: (i, j))],
        out_shape=[jax.ShapeDtypeStruct((2, T, FFN), F32), jax.ShapeDtypeStruct((T, FFN), BF16)],
        compiler_params=_cparams(("arbitrary", "arbitrary")),
    )(h2, wg_gu, wg_gu)


def mm_dact_swiglu(dx2, wg_down, gu, dep=None):
    _, _, both = _gate_up_specs()
    deps, dspecs = _dep_specs(dep, 2)

    def body(dx_ref, w_ref, gu_ref, *rest):
        dact = lax.dot_general(dx_ref[...].astype(BF16), w_ref[...], _dims(2, False, True), preferred_element_type=F32)
        _, vjp = jax.vjp(_swiglu_fn, gu_ref[0], gu_ref[1])
        dgt, dup = vjp(dact)
        rest[-1][0] = dgt.astype(BF16)
        rest[-1][1] = dup.astype(BF16)

    return pl.pallas_call(
        body, name="mm_dact_swiglu", grid=(FFN // _FF_TN, T // _TM_FFN),
        in_specs=[pl.BlockSpec((_TM_FFN, D), lambda j, i: (i, 0)), pl.BlockSpec((None, _FF_TN, D), lambda j, i: (0, j, 0)),
                  both]
        + dspecs,
        out_specs=both, out_shape=jax.ShapeDtypeStruct((2, T, FFN), BF16),
        compiler_params=_cparams(("arbitrary", "arbitrary")),
    )(dx2, wg_down, gu, *deps)


def loss_and_grad(y, target):
    blk = pl.BlockSpec((_ROWS, D), lambda i: (i, 0))
    acc = pl.BlockSpec((8, HD), lambda i: (0, 0))

    def body(y_ref, t_ref, dy_ref, l_ref):
        err = y_ref[...] - t_ref[...]
        dy_ref[...] = err * (1.0 / D)

        @pl.when(pl.program_id(0) == 0)
        def _():
            l_ref[...] = jnp.zeros_like(l_ref)

        l_ref[...] += (0.5 / D) * jnp.sum(err * err)

    return pl.pallas_call(
        body, name="loss_and_grad", grid=(T // _ROWS,), in_specs=[blk, blk], out_specs=[blk, acc],
        out_shape=[jax.ShapeDtypeStruct((T, D), F32), jax.ShapeDtypeStruct((8, HD), F32)],
        compiler_params=_cparams(("arbitrary",)),
    )(y, target)


def adamw(w, g, m, v, name):
    rows, cols = w.shape
    tr = _ROWS if rows % _ROWS == 0 else rows
    blk = pl.BlockSpec((tr, cols), lambda i: (i, 0))

    def body(w_ref, g_ref, m_ref, v_ref, d_ref, nm_ref, nv_ref):
        gg = g_ref[...]
        nm = ADAM_B1 * m_ref[...] + (1.0 - ADAM_B1) * gg
        nv = ADAM_B2 * v_ref[...] + (1.0 - ADAM_B2) * (gg * gg)
        m_hat = nm / (1.0 - ADAM_B1 ** ADAM_STEP)
        v_hat = nv / (1.0 - ADAM_B2 ** ADAM_STEP)
        d_ref[...] = -ADAM_LR * (m_hat / (jnp.sqrt(v_hat) + ADAM_EPS) + ADAM_WD * w_ref[...])
        nm_ref[...] = nm
        nv_ref[...] = nv

    return pl.pallas_call(
        body, name=name, grid=(rows // tr,), in_specs=[blk] * 4, out_specs=[blk] * 3,
        out_shape=[jax.ShapeDtypeStruct(w.shape, F32)] * 3, compiler_params=_cparams(("arbitrary",)),
    )(w, g, m, v)


_LANE = 128


def _segment_of_shard_column():
    flat = np.full(P_END, -1, np.int64)
    for o in range(P_END):
        if GATE_COLS <= o < P_CQ:
            continue
        c = o if o < GATE_COLS else o - (P_CQ - GATE_COLS)
        flat[o] = (c // IN_SHARD) * IN_SHARD_PAD + c % IN_SHARD
    return flat


def _block_pairs(src_of_dst):
    return [sorted({int(c) // _LANE for c in src_of_dst[db * _LANE:(db + 1) * _LANE] if c >= 0})
            for db in range(len(src_of_dst) // _LANE)]


_RELAYOUT_ROWS = 512
_SHARD_BLOCKS = IN_SHARD_PAD // _LANE


def _relayout(name, x, to_segments):
    seg_of = _segment_of_shard_column()
    if to_segments:
        src_of_dst = seg_of
    else:
        src_of_dst = np.full(N_CHIPS * IN_SHARD_PAD, -1, np.int64)
        src_of_dst[seg_of[seg_of >= 0]] = np.nonzero(seg_of >= 0)[0]
    sources = _block_pairs(src_of_dst)
    n_dst = len(sources)
    col_map = jnp.asarray(src_of_dst.reshape(n_dst, 1, _LANE), jnp.int32)
    shard_blk = pl.BlockSpec((N_CHIPS, _RELAYOUT_ROWS, IN_SHARD_PAD), lambda i: (0, i, 0))
    seg_blk = pl.BlockSpec((_RELAYOUT_ROWS, P_END), lambda i: (i, 0))

    def shard_cols(ref, b):
        return ref.at[b // _SHARD_BLOCKS, :, pl.ds((b % _SHARD_BLOCKS) * _LANE, _LANE)]

    def seg_cols(ref, b):
        return ref.at[:, pl.ds(b * _LANE, _LANE)]

    src_cols, dst_cols = (shard_cols, seg_cols) if to_segments else (seg_cols, shard_cols)

    def body(x_ref, map_ref, o_ref):
        src_row = _iota((_LANE, _LANE), 0)
        for d in range(n_dst):
            acc = jnp.zeros((_RELAYOUT_ROWS, _LANE), F32)
            for sb in sources[d]:
                sel = (src_row + sb * _LANE == map_ref[d]).astype(x_ref.dtype)
                acc = acc + jnp.dot(src_cols(x_ref, sb)[...], sel, preferred_element_type=F32)
            dst_cols(o_ref, d)[...] = acc.astype(o_ref.dtype)

    rows = x.shape[-2]
    out_shape = (rows, P_END) if to_segments else (N_CHIPS, rows, IN_SHARD_PAD)
    return pl.pallas_call(
        body, name=name, grid=(rows // _RELAYOUT_ROWS,),
        in_specs=[shard_blk if to_segments else seg_blk, pl.BlockSpec(col_map.shape, lambda i: (0, 0, 0))],
        out_specs=seg_blk if to_segments else shard_blk, out_shape=jax.ShapeDtypeStruct(out_shape, x.dtype),
        compiler_params=_cparams(("arbitrary",)),
    )(x, col_map)


def shards_to_segments(w):
    return _relayout("shards_to_segments", w, True)


def segments_to_shards(w):
    return _relayout("segments_to_shards", w, False)


def mixers_forward(x, w_in, sp, dep=None):
    h1 = rmsnorm_fwd(x, sp["norm1_g"], dep)
    p2 = mm_proj(h1, w_in, 0)
    y_a = sgu_fwd(p2, sp["sgu_norm_g"], sp["w_spatial"], sp["b_spatial"])
    y_b, saved_b = gdn_forward(p2, sp["conv_w"], sp["a_log"], sp["dt_bias"], sp["o_norm_g"])
    y_c = dattn_fwd(p2, sp["q_norm_g"], sp["k_norm_g"], _alibi_slopes())
    mix = jnp.concatenate([y_a, y_b, y_c], axis=1)
    return mix, (x, h1, p2, saved_b, mix)


def ffn_up(x, mix, w_out, w_gu, sp):
    x1 = mm_out(mix, w_out, 0, x)
    h2 = rmsnorm_fwd(x1, sp["norm2_g"])
    gu, act = mm_gu_swiglu(h2, w_gu)
    return x1, h2, gu, act


def ffn_forward(x, mix, wg, sp):
    x1, h2, gu, act = ffn_up(x, mix, wg["out"], wg["gu"], sp)
    x2 = mm_down(act, wg["down"], 0, x1)
    return x2, (x1, h2, gu, act)


def ffn_backward(dx2, wg, sp, saved, dep=None):
    x1, h2, gu, act = saved
    dgu = mm_dact_swiglu(dx2, wg["down"], gu, dep)
    dw_down = mm_dwdown(act, dx2)
    dw_gu = mm_dwgu(h2, dgu)
    dh2 = mm_dh2(dgu, wg["gu"], 0)
    dx1, dnorm2 = rmsnorm_bwd(x1, sp["norm2_g"], dh2, dx2)
    return dx1, dnorm2, dw_gu, dw_down


def mixers_backward(dx1, wg, sp, saved, dep=None):
    x, h1, p2, saved_b, mix = saved
    dmix = mm_dmix(dx1, wg["out"], 0, dep)
    dw_out = mm_dwout(mix, dx1)
    du, dv, dsg, dws, dbs = sgu_bwd(p2, sp["sgu_norm_g"], sp["w_spatial"], sp["b_spatial"], dmix)
    dseg_b, dconv, dal, ddtb, dog = gdn_backward(p2, sp["conv_w"], sp["a_log"], sp["dt_bias"], sp["o_norm_g"], saved_b, dmix)
    dcq, dck, dcv, dqg, dkg = dattn_bwd(p2, sp["q_norm_g"], sp["k_norm_g"], _alibi_slopes(), dmix)
    dp2 = jnp.concatenate([du, dv] + dseg_b + [dcq, dck, dcv], axis=1)
    dw_in = segments_to_shards(mm_dwin(h1, dp2))
    dh1 = mm_dh1(dp2, wg["in"], 0)
    dx, dnorm1 = rmsnorm_bwd(x, sp["norm1_g"], dh1, dx1)
    small = {"norm1_g": dnorm1, "sgu_norm_g": dsg, "w_spatial": dws, "b_spatial": dbs, "conv_w": dconv, "a_log": dal,
             "dt_bias": ddtb, "o_norm_g": dog, "q_norm_g": dqg, "k_norm_g": dkg}
    return dx, dw_in, dw_out, small


def layer_forward(x, wg, sp, dep=None):
    mix, saved_m = mixers_forward(x, wg["in"], sp, dep)
    x2, saved_f = ffn_forward(x, mix, wg, sp)
    return x2, (saved_m, saved_f)


def layer_backward(dx2, wg, sp, saved, dep=None):
    dx1, dnorm2, dw_gu, dw_down = ffn_backward(dx2, wg, sp, saved[1], dep)
    dx, dw_in, dw_out, small = mixers_backward(dx1, wg, sp, saved[0])
    return dx, {"in": dw_in, "out": dw_out, "gu": dw_gu, "down": dw_down}, {**small, "norm2_g": dnorm2}


_HBM = pl.BlockSpec(memory_space=pltpu.HBM)
_MESH = pl.DeviceIdType.MESH


def _place():
    x, y, c = lax.axis_index("x"), lax.axis_index("y"), lax.axis_index("c")
    chips = [(1 - x, y), (x, 1 - y), (1 - x, 1 - y)]
    return x, y, c, chips


def _rcopy(src, dst, ssem, rsem, dev):
    return pltpu.make_async_remote_copy(src_ref=src, dst_ref=dst, send_sem=ssem, recv_sem=rsem, device_id=dev,
                                        device_id_type=_MESH)


def _xor(a, b):
    return a + b - 2 * a * b


def gather_weights(shards):
    n = len(shards)

    def body(*refs):
        ins, outs = refs[:n], refs[n:2 * n]
        s_ici, r_ici, s_d2d, r_d2d, s_own, r_own = refs[2 * n:]
        x, y, c, _ = _place()
        s = 2 * x + y
        sibling = (x, y, 1 - c)
        nbr = [(1 - x, y), (x, 1 - y)]
        src_chip = (_xor(x, 1 - c), _xor(y, c))
        dst_chip = (_xor(x, c), _xor(y, 1 - c))
        t_src = 2 * src_chip[0] + src_chip[1]
        t_oth = 2 * dst_chip[0] + dst_chip[1]
        t_dia = 2 * (1 - x) + (1 - y)
        sends = []
        for a in range(n):
            for u in range(2):
                sends.append(_rcopy(ins[a].at[u], outs[a].at[s, u], s_own.at[a, u], r_own.at[a, u], sibling))
            for k in range(2):
                sends.append(_rcopy(ins[a].at[c], outs[a].at[s, c], s_ici.at[a, k], r_ici.at[a, k], (*nbr[k], c)))
        for cp in sends:
            cp.start()

        def landed(a, t, k):
            _rcopy(ins[a].at[c], outs[a].at[t, c], s_ici.at[a, k], r_ici.at[a, k], sibling).wait_recv()
            cp = _rcopy(outs[a].at[t, c], outs[a].at[t, c], s_d2d.at[a, k], r_d2d.at[a, k], sibling)
            cp.start()
            sends.append(cp)

        for a in range(n):
            landed(a, t_src, c)
            fwd = _rcopy(outs[a].at[t_src, c], outs[a].at[t_src, c], s_ici.at[a, 2], r_ici.at[a, 2], (*dst_chip, c))
            fwd.start()
            sends.append(fwd)
        for a in range(n):
            landed(a, t_oth, 1 - c)
        for a in range(n):
            landed(a, t_dia, 2)
        for a in range(n):
            for u in range(2):
                _rcopy(ins[a].at[u], outs[a].at[s, u], s_own.at[a, u], r_own.at[a, u], sibling).wait_recv()
            for k, t in enumerate([2 * nbr[0][0] + nbr[0][1], 2 * nbr[1][0] + nbr[1][1], t_dia]):
                _rcopy(ins[a].at[1 - c], outs[a].at[t, 1 - c], s_d2d.at[a, k], r_d2d.at[a, k], sibling).wait_recv()
        for cp in sends:
            cp.wait_send()

    dma = lambda k: pltpu.SemaphoreType.DMA((n, k))
    return pl.pallas_call(
        body, name="gather_weights", in_specs=[_HBM] * n, out_specs=[_HBM] * n,
        out_shape=[jax.ShapeDtypeStruct((N_CHIPS,) + w.shape, w.dtype) for w in shards],
        scratch_shapes=[dma(3), dma(3), dma(3), dma(3), dma(2), dma(2)],
    )(*shards)


_SEM = pl.BlockSpec(memory_space=pltpu.SEMAPHORE)
_SIDE_EFFECT = pltpu.SideEffectType.DATAFLOW_SIDE_EFFECTING


def _in_hbm(a):
    return pltpu.with_memory_space_constraint(a, pltpu.HBM)


def _split_copy(name, srcs, land_shapes, n_sems, copies):
    n, m = len(srcs), len(land_shapes)
    thru = [pltpu.HBM(a.shape, a.dtype) for a in srcs] + [pltpu.HBM(s.shape, s.dtype) for s in land_shapes]
    sems = (pltpu.SemaphoreType.DMA((n_sems,)), pltpu.SemaphoreType.DMA((n_sems,)))

    def start(dep=None):
        deps = [] if dep is None else [dep]

        def body(*refs):
            ins, lands = refs[:n], refs[n:n + m]
            ssem, rsem, token = refs[n + m + len(deps)], refs[n + m + len(deps) + 1], refs[-1]
            for cp in copies(ins, lands, ssem, rsem)[0]:
                cp.start()
            token[...] = jnp.zeros_like(token)

        out = pl.pallas_call(
            body, name=name + "_start", out_shape=(*sems, *thru, jax.ShapeDtypeStruct((8, HD), F32)),
            in_specs=[_HBM] * (n + m) + [pl.BlockSpec(memory_space=pl.ANY)] * len(deps),
            out_specs=(_SEM, _SEM, *[_HBM] * (n + m), pl.BlockSpec(memory_space=pltpu.VMEM)),
            input_output_aliases={i: 2 + i for i in range(n + m)},
            compiler_params=pltpu.CompilerParams(has_side_effects=_SIDE_EFFECT),
        )(*[_in_hbm(a) for a in srcs], *[_in_hbm(lax.empty(s.shape, s.dtype)) for s in land_shapes], *deps)
        return out[:-1], out[-1]

    def wait(state, after):
        def body(*refs):
            ins, lands, ssem, rsem = refs[:n], refs[n:n + m], refs[n + m], refs[n + m + 1]
            sent, arrivals = copies(ins, lands, ssem, rsem)
            for cp in sent:
                cp.wait_send()
            for cp in arrivals:
                cp.wait_recv()

        out = pl.pallas_call(
            body, name=name + "_wait", out_shape=tuple(thru),
            in_specs=[_HBM] * (n + m) + [_SEM, _SEM, pl.BlockSpec(memory_space=pl.ANY)], out_specs=[_HBM] * (n + m),
            input_output_aliases={i: i for i in range(n + m)},
            compiler_params=pltpu.CompilerParams(has_side_effects=_SIDE_EFFECT),
        )(*state[2:], state[0], state[1], after)
        return list(out[:n]), list(out[n:])

    return start, wait


def gather_direct(shards, tag):
    n = len(shards)

    def copies(ins, lands, ssem, rsem):
        x, y, c, chips = _place()
        s = 2 * x + y
        sibling = (x, y, 1 - c)
        sent, arrivals = [], []
        for a in range(n):
            for u in range(2):
                cp = _rcopy(ins[a].at[u], lands[a].at[s, u], ssem.at[5 * a + u], rsem.at[5 * a + u], sibling)
                sent.append(cp)
                arrivals.append(cp)
            for j, (cx, cy) in enumerate(chips):
                k = 5 * a + 2 + j
                sent.append(_rcopy(ins[a].at[c], lands[a].at[s, c], ssem.at[k], rsem.at[k], (cx, cy, c)))
                arrivals.append(_rcopy(ins[a].at[c], lands[a].at[2 * cx + cy, c], ssem.at[k], rsem.at[k], (cx, cy, c)))
        return sent, arrivals

    lands = [jax.ShapeDtypeStruct((N_CHIPS,) + w.shape, w.dtype) for w in shards]
    return _split_copy("gather_direct_" + tag, shards, lands, 5 * n, copies)


def pass_to_sibling(lands):
    n = len(lands)

    def body(*refs):
        ins = refs[:n]
        ssem, rsem = refs[2 * n:]
        x, y, c, chips = _place()
        sibling = (x, y, 1 - c)
        cps, arrivals = [], []
        for a in range(n):
            for j, (cx, cy) in enumerate(chips):
                t = 2 * cx + cy
                cps.append(_rcopy(ins[a].at[t, c], ins[a].at[t, c], ssem.at[a, j], rsem.at[a, j], sibling))
                arrivals.append(_rcopy(ins[a].at[t, c], ins[a].at[t, 1 - c], ssem.at[a, j], rsem.at[a, j], sibling))
        for cp in cps:
            cp.start()
        for cp, ar in zip(cps, arrivals):
            cp.wait_send()
            ar.wait_recv()

    return pl.pallas_call(
        body, name="pass_to_sibling", in_specs=[_HBM] * n, out_specs=[_HBM] * n,
        out_shape=[jax.ShapeDtypeStruct(a.shape, a.dtype) for a in lands], input_output_aliases={a: a for a in range(n)},
        scratch_shapes=[pltpu.SemaphoreType.DMA((n, 3)), pltpu.SemaphoreType.DMA((n, 3))],
    )(*lands)


def exchange_halves(grads):
    n = len(grads)

    def body(*refs):
        ins, outs = refs[:n], refs[n:2 * n]
        ssem, rsem = refs[2 * n:]
        x, y, c, _ = _place()
        cps = []
        for a in range(n):
            h = grads[a].shape[1] // 2
            cps.append(_rcopy(ins[a].at[:, pl.ds((1 - c) * h, h)], outs[a], ssem.at[a], rsem.at[a], (x, y, 1 - c)))
        for cp in cps:
            cp.start()
        for cp in cps:
            cp.wait()

    return pl.pallas_call(
        body, name="exchange_halves", in_specs=[_HBM] * n, out_specs=[_HBM] * n,
        out_shape=[jax.ShapeDtypeStruct((g.shape[0], g.shape[1] // 2, g.shape[2]), g.dtype) for g in grads],
        scratch_shapes=[pltpu.SemaphoreType.DMA((n,)), pltpu.SemaphoreType.DMA((n,))],
    )(*grads)


def scatter_to_chips(parts):
    n = len(parts)

    def body(*refs):
        ins, outs = refs[:n], refs[n:2 * n]
        ssem, rsem = refs[2 * n:]
        x, y, c, chips = _place()
        cps = [_rcopy(ins[a].at[2 * cx + cy], outs[a].at[j], ssem.at[a, j], rsem.at[a, j], (cx, cy, c))
               for a in range(n) for j, (cx, cy) in enumerate(chips)]
        for cp in cps:
            cp.start()
        for cp in cps:
            cp.wait()

    return pl.pallas_call(
        body, name="scatter_to_chips", in_specs=[_HBM] * n, out_specs=[_HBM] * n,
        out_shape=[jax.ShapeDtypeStruct((3,) + p.shape[1:], p.dtype) for p in parts],
        scratch_shapes=[pltpu.SemaphoreType.DMA((n, 3)), pltpu.SemaphoreType.DMA((n, 3))],
    )(*parts)


def scatter_direct(parts, tag):
    n = len(parts)

    def copies(ins, lands, ssem, rsem):
        x, y, c, chips = _place()
        cps = [_rcopy(ins[a].at[2 * cx + cy], lands[a].at[j], ssem.at[3 * a + j], rsem.at[3 * a + j], (cx, cy, c))
               for a in range(n) for j, (cx, cy) in enumerate(chips)]
        return cps, cps

    lands = [jax.ShapeDtypeStruct((3,) + p.shape[1:], p.dtype) for p in parts]
    return _split_copy("scatter_direct_" + tag, parts, lands, 3 * n, copies)


def share_halves(halves):
    n = len(halves)

    def body(*refs):
        ins, outs = refs[:n], refs[n:2 * n]
        ssem, rsem = refs[2 * n:]
        x, y, c, _ = _place()
        cps = [_rcopy(ins[i], outs[i], ssem.at[i], rsem.at[i], (x, y, 1 - c)) for i in range(n)]
        for cp in cps:
            cp.start()
        for cp in cps:
            cp.wait()

    return pl.pallas_call(
        body, name="share_halves", in_specs=[_HBM] * n, out_specs=[_HBM] * n,
        out_shape=[jax.ShapeDtypeStruct(h.shape, h.dtype) for h in halves],
        scratch_shapes=[pltpu.SemaphoreType.DMA((n,)), pltpu.SemaphoreType.DMA((n,))],
    )(*halves)


def adamw_shard(w, m, v, mine, theirs, c, name):
    _, r, cw = w.shape
    h, cg = mine[0].shape
    tr = next(t for t in (256, 176, 128) if h % t == 0 and t * cg * 4 <= (3 << 19))
    nb = h // tr
    wblk = pl.BlockSpec((None, tr, cw), lambda l, i, c_ref: (l, i, 0))
    gblk = lambda layer, own: pl.BlockSpec((tr, cg), lambda l, i, c_ref: (_held_block(l, i, c_ref, layer, own, nb), 0))
    return _adamw_halves(w, m, v, mine, theirs, c, name, (DEPTH, r // tr), wblk, gblk, nb, cw)


def _held_block(l, i, c_ref, layer, own, nb):
    in_use = (l == layer) & (((i // nb) == c_ref[0]) == own)
    return jnp.where(in_use, i % nb, 0)


def _adamw_halves(w, m, v, mine, theirs, c, name, grid, wblk, gblk, nb, cw):
    def body(c_ref, w_ref, m_ref, v_ref, m0, m1, t0, t1, g_ref, d_ref, nm_ref, nv_ref):
        is_mine = (pl.program_id(1) // nb) == c_ref[0]
        first = pl.program_id(0) == 0
        gg = jnp.where(is_mine, jnp.where(first, m0[:, :cw], m1[:, :cw]), jnp.where(first, t0[:, :cw], t1[:, :cw]))
        nm = ADAM_B1 * m_ref[...] + (1.0 - ADAM_B1) * gg
        nv = ADAM_B2 * v_ref[...] + (1.0 - ADAM_B2) * (gg * gg)
        m_hat = nm / (1.0 - ADAM_B1 ** ADAM_STEP)
        v_hat = nv / (1.0 - ADAM_B2 ** ADAM_STEP)
        g_ref[...] = gg
        d_ref[...] = -ADAM_LR * (m_hat / (jnp.sqrt(v_hat) + ADAM_EPS) + ADAM_WD * w_ref[...])
        nm_ref[...] = nm
        nv_ref[...] = nv

    return pl.pallas_call(
        body, name=name,
        grid_spec=pltpu.PrefetchScalarGridSpec(
            num_scalar_prefetch=1, grid=grid,
            in_specs=[wblk] * 3 + [gblk(0, True), gblk(1, True), gblk(0, False), gblk(1, False)], out_specs=[wblk] * 4),
        out_shape=[jax.ShapeDtypeStruct(w.shape, F32)] * 4, compiler_params=_cparams(("arbitrary", "arbitrary")),
    )(c, w, m, v, mine[0], mine[1], theirs[0], theirs[1])


def adamw_shard_t(wt, mt, vt, mine_t, theirs_t, c, name):
    _, cw, r = wt.shape
    h = mine_t[0].shape[1]
    tc = 256
    nb = h // tc
    wblk = pl.BlockSpec((None, cw, tc), lambda l, j, c_ref: (l, 0, j))
    gblk = lambda layer, own: pl.BlockSpec((cw, tc), lambda l, j, c_ref: (0, _held_block(l, j, c_ref, layer, own, nb)))
    return _adamw_halves(wt, mt, vt, mine_t, theirs_t, c, name, (DEPTH, r // tc), wblk, gblk, nb, cw)


def _half_rows(h, cols):
    for tr in (512, 256, 352, 128, 64):
        if h % tr == 0 and tr * cols * 4 <= 6 * 1024 * 1024:
            return tr
    raise ValueError((h, cols))


def add_sibling(grad, recv, c):
    _, r, cols = grad.shape
    h = r // 2
    tr = _half_rows(h, cols)
    nb = h // tr

    def body(c_ref, g_ref, r_ref, o_ref):
        o_ref[...] = (g_ref[...].astype(F32) + r_ref[...].astype(F32)).astype(BF16)

    return pl.pallas_call(
        body, name="add_sibling",
        grid_spec=pltpu.PrefetchScalarGridSpec(
            num_scalar_prefetch=1, grid=(N_CHIPS, nb),
            in_specs=[pl.BlockSpec((None, tr, cols), lambda t, i, c_ref: (t, c_ref[0] * nb + i, 0)),
                      pl.BlockSpec((None, tr, cols), lambda t, i, c_ref: (t, i, 0))],
            out_specs=pl.BlockSpec((None, tr, cols), lambda t, i, c_ref: (t, i, 0))),
        out_shape=jax.ShapeDtypeStruct((N_CHIPS, h, cols), BF16), compiler_params=_cparams(("arbitrary", "arbitrary")),
    )(c, grad, recv)


def add_chips(part, recv, s):
    _, h, cols = part.shape
    tr = _half_rows(h, cols)

    def body(s_ref, p_ref, r_ref, o_ref):
        o_ref[...] = ((p_ref[...].astype(F32) + r_ref[0].astype(F32)) + r_ref[1].astype(F32)) + r_ref[2].astype(F32)

    return pl.pallas_call(
        body, name="add_chips",
        grid_spec=pltpu.PrefetchScalarGridSpec(
            num_scalar_prefetch=1, grid=(h // tr,),
            in_specs=[pl.BlockSpec((None, tr, cols), lambda i, s_ref: (s_ref[0], i, 0)),
                      pl.BlockSpec((3, tr, cols), lambda i, s_ref: (0, i, 0))],
            out_specs=pl.BlockSpec((tr, cols), lambda i, s_ref: (i, 0))),
        out_shape=jax.ShapeDtypeStruct((h, cols), F32), compiler_params=_cparams(("arbitrary",)),
    )(s, part, recv)


def allreduce_small(vec):
    rows = vec.shape[0]

    def body(v_ref, o_ref, buf, ssem, rsem, lsem):
        x, y, c, chips = _place()
        me, sibling = (x, y, c), (x, y, 1 - c)

        def blk(px, py, pc):
            return buf.at[4 * px + 2 * py + pc]

        def copy(k, block, to, src=None):
            return _rcopy(blk(*block) if src is None else src, blk(*block), ssem.at[k], rsem.at[k], to)

        mine = pltpu.make_async_copy(v_ref, blk(*me), lsem)
        mine.start()
        first = [copy(0, me, sibling, src=v_ref)] + [copy(1 + j, me, (*chip, c), src=v_ref) for j, chip in enumerate(chips)]
        for cp in first:
            cp.start()
        passed = [copy(4 + j, (*chip, c), sibling) for j, chip in enumerate(chips)]
        for j, chip in enumerate(chips):
            copy(1 + j, (*chip, c), me).wait_recv()
            passed[j].start()
        copy(0, sibling, me).wait_recv()
        for j, chip in enumerate(chips):
            copy(4 + j, (*chip, 1 - c), me).wait_recv()
        for cp in first + passed:
            cp.wait_send()
        mine.wait()
        acc = buf[0]
        for d in range(1, N_DEV):
            acc = acc + buf[d]
        o_ref[...] = acc

    vm = pl.BlockSpec(memory_space=pltpu.VMEM)
    return pl.pallas_call(
        body, name="allreduce_small", in_specs=[vm], out_specs=vm, out_shape=jax.ShapeDtypeStruct(vec.shape, F32),
        scratch_shapes=[pltpu.VMEM((N_DEV, rows, HD), F32), pltpu.SemaphoreType.DMA((7,)), pltpu.SemaphoreType.DMA((7,)),
                        pltpu.SemaphoreType.DMA],
        compiler_params=pltpu.CompilerParams(vmem_limit_bytes=VMEM_LIMIT),
    )(vec)


SMALL_NAMES = ("norm1_g", "sgu_norm_g", "w_spatial", "b_spatial", "conv_w", "a_log", "dt_bias", "o_norm_g", "q_norm_g",
               "k_norm_g", "norm2_g")


def small_params(l, p, conv_full):
    return {"norm1_g": p["norm1_g"][l][None], "sgu_norm_g": p["sgu_norm_g"][l][:, None, :], "w_spatial": p["w_spatial"][l],
            "b_spatial": p["b_spatial"][l][..., None], "conv_w": conv_full[l], "a_log": p["a_log"][l], "dt_bias": p["dt_bias"][l],
            "o_norm_g": p["o_norm_g"][l][None], "q_norm_g": p["q_norm_g"][l][None], "k_norm_g": p["k_norm_g"][l][None],
            "norm2_g": p["norm2_g"][l][None]}


def local_step(x, target, wg, sps):
    saved = []
    for l in range(DEPTH):
        x, s = layer_forward(x, wg[l], sps[l])
        saved.append(s)
    dx, loss = loss_and_grad(x, target)
    bigs, smalls = [None] * DEPTH, [None] * DEPTH
    for l in reversed(range(DEPTH)):
        dx, bigs[l], smalls[l] = layer_backward(dx, wg[l], sps[l], saved[l])
    return loss, dx, bigs, smalls


_PACK_TILE = 8 * HD


def _pack(arrays):
    flat = jnp.concatenate([a.reshape(-1) for a in arrays])
    pad = -flat.shape[0] % _PACK_TILE
    return jnp.pad(flat, (0, pad)).reshape(-1, HD)


def _unpack(packed, shapes):
    flat, out, off = packed.reshape(-1), [], 0
    for shp in shapes:
        n = int(np.prod(shp))
        out.append(flat[off:off + n].reshape(shp))
        off += n
    return out


BIG_NAMES = ("in", "out", "gu", "down")
WEIGHT_ORDER = ("norm1_g", "w_in", "sgu_norm_g", "w_spatial", "b_spatial", "conv_w", "a_log", "dt_bias", "o_norm_g", "q_norm_g",
                "k_norm_g", "w_out", "norm2_g", "w_gate_up", "w_down")


def kernel(x, norm1_g, w_in, sgu_norm_g, w_spatial, b_spatial, conv_w, a_log, dt_bias, o_norm_g, q_norm_g, k_norm_g, w_out, norm2_g, w_gate_up, w_down, loss_target, m_norm1_g, m_w_in, m_sgu_norm_g, m_w_spatial, m_b_spatial, m_conv_w, m_a_log, m_dt_bias, m_o_norm_g, m_q_norm_g, m_k_norm_g, m_w_out, m_norm2_g, m_w_gate_up, m_w_down, v_norm1_g, v_w_in, v_sgu_norm_g, v_w_spatial, v_b_spatial, v_conv_w, v_a_log, v_dt_bias, v_o_norm_g, v_q_norm_g, v_k_norm_g, v_w_out, v_norm2_g, v_w_gate_up, v_w_down):
    w = dict(norm1_g=norm1_g, w_in=w_in, sgu_norm_g=sgu_norm_g, w_spatial=w_spatial, b_spatial=b_spatial, conv_w=conv_w,
             a_log=a_log, dt_bias=dt_bias, o_norm_g=o_norm_g, q_norm_g=q_norm_g, k_norm_g=k_norm_g, w_out=w_out,
             norm2_g=norm2_g, w_gate_up=w_gate_up, w_down=w_down)
    m = dict(norm1_g=m_norm1_g, w_in=m_w_in, sgu_norm_g=m_sgu_norm_g, w_spatial=m_w_spatial, b_spatial=m_b_spatial,
             conv_w=m_conv_w, a_log=m_a_log, dt_bias=m_dt_bias, o_norm_g=m_o_norm_g, q_norm_g=m_q_norm_g, k_norm_g=m_k_norm_g,
             w_out=m_w_out, norm2_g=m_norm2_g, w_gate_up=m_w_gate_up, w_down=m_w_down)
    v = dict(norm1_g=v_norm1_g, w_in=v_w_in, sgu_norm_g=v_sgu_norm_g, w_spatial=v_w_spatial, b_spatial=v_b_spatial,
             conv_w=v_conv_w, a_log=v_a_log, dt_bias=v_dt_bias, o_norm_g=v_o_norm_g, q_norm_g=v_q_norm_g, k_norm_g=v_k_norm_g,
             w_out=v_w_out, norm2_g=v_norm2_g, w_gate_up=v_w_gate_up, w_down=v_w_down)
    chip = (2 * lax.axis_index("x") + lax.axis_index("y")).astype(jnp.int32)
    core = lax.axis_index("c").astype(jnp.int32)

    in_pad = IN_SHARD_PAD - IN_SHARD
    w_in_pad = jnp.pad(w_in, ((0, 0), (0, 0), (0, in_pad)))

    halves_of = lambda a: a.reshape(2, a.shape[0] // 2, a.shape[1])
    bf_halves = lambda a: halves_of(a.astype(BF16))

    def ffn_shards(l):
        return [bf_halves(w_gate_up[l]), bf_halves(w_down[l]), bf_halves(w_out[l])]

    def mixer_shards(l):
        return [bf_halves(w_in_pad[l]), halves_of(conv_w[l])]

    def mixer_weights(g):
        g_in, g_conv = g
        return (shards_to_segments(g_in.reshape(N_CHIPS, D, IN_SHARD_PAD))[None],
                g_conv.reshape(N_CHIPS, B_CONV, -1).transpose(1, 0, 2).reshape(B_CONV, 3 * B_WIDTH))

    def ffn_weights(g, w_in_seg):
        g_gu, g_down, g_out = g
        return {"in": w_in_seg, "out": g_out.reshape(1, D, D), "gu": g_gu.reshape(1, N_CHIPS, D, GU_SHARD),
                "down": None if g_down is None else g_down.reshape(1, FFN, D)}

    def layer_params(l, conv_full):
        return small_params(0, {n: w[n][l:l + 1] for n in SMALL_NAMES if n != "conv_w"}, conv_full[None])

    w_in0, conv0 = mixer_weights(gather_weights(mixer_shards(0)))
    gu0, down0, out0 = ffn_shards(0)
    start_a, wait_a = gather_direct([gu0, out0], "ffn0")
    start_b, wait_b = gather_direct([down0] + mixer_shards(1), "mid")
    start_c, wait_c = gather_direct(ffn_shards(1), "ffn1")
    state_a, token_a = start_a()
    state_b, token_b = start_b(token_a)
    state_c, token_c = start_c(token_b)
    sps = [layer_params(0, conv0), None]
    mix0, saved_m0 = mixers_forward(x[0], w_in0, sps[0], dep=token_c)
    g_gu0, g_out0 = pass_to_sibling(wait_a(state_a, mix0)[1])
    wg0 = ffn_weights((g_gu0, None, g_out0), w_in0)
    x1_0, h2_0, gu_0, act_0 = ffn_up(x[0], mix0, wg0["out"], wg0["gu"], sps[0])
    g_down0, g_in1, g_conv1 = pass_to_sibling(wait_b(state_b, act_0)[1])
    wg0["down"] = g_down0.reshape(1, FFN, D)
    x1 = mm_down(act_0, wg0["down"], 0, x1_0)
    saved_f0 = (x1_0, h2_0, gu_0, act_0)
    w_in1, conv1 = mixer_weights((g_in1, g_conv1))
    sps[1] = layer_params(1, conv1)
    mix1, saved_m1 = mixers_forward(x1, w_in1, sps[1])
    wg1 = ffn_weights(pass_to_sibling(wait_c(state_c, mix1)[1]), w_in1)
    x2, saved_f1 = ffn_forward(x1, mix1, wg1, sps[1])
    saved1 = (saved_m1, saved_f1)
    dx, loss_tile = loss_and_grad(x2, loss_target[0])

    def to_chip_parts(grads):
        return [add_sibling(g, r, core.reshape(1)) for g, r in zip(grads, exchange_halves(grads))]

    def start_scatter(grads, tag, dep=None):
        start, wait = scatter_direct(to_chip_parts(grads), tag)
        state, token = start(dep)
        return functools.partial(wait, state), token

    smalls = [None] * DEPTH
    by_chip_out = lambda t: t.reshape(N_CHIPS, OUT_SHARD, D)
    by_chip_down = lambda t: t.reshape(N_CHIPS, DOWN_SHARD, D)
    dx1, dnorm2_1, dw_gu1, dw_down1 = ffn_backward(dx, wg1, sps[1], saved1[1])
    wait_f1, tok_f1 = start_scatter([dw_gu1, by_chip_down(dw_down1)], "ffn1")
    dx, dw_in1, dw_out1, small1 = mixers_backward(dx1, wg1, sps[1], saved1[0], dep=tok_f1)
    smalls[1] = {**small1, "norm2_g": dnorm2_1}
    wait_m1, tok_m1 = start_scatter([dw_in1, by_chip_out(dw_out1)], "mix1")
    dx1, dnorm2_0, dw_gu0, dw_down0 = ffn_backward(dx, wg0, sps[0], saved_f0, dep=tok_m1)
    wait_f0, tok_f0 = start_scatter([dw_gu0, by_chip_down(dw_down0)], "ffn0")
    dx, dw_in0, dw_out0, small0 = mixers_backward(dx1, wg0, sps[0], saved_m0, dep=tok_f0)
    smalls[0] = {**small0, "norm2_g": dnorm2_0}

    grad, delta, new_m, new_v = {}, {}, {}, {}
    stacked = [jnp.stack([smalls[l][n] for l in range(DEPTH)]) for n in SMALL_NAMES]
    total = allreduce_small(_pack(stacked + [loss_tile[0, :1]]))
    shapes = [(DEPTH, B_CONV, 3 * B_WIDTH) if n == "conv_w" else w[n].shape for n in SMALL_NAMES]
    small_grads = dict(zip(SMALL_NAMES, _unpack(total, shapes + [(1,)])[:-1]))
    loss = _unpack(total, shapes + [(1,)])[-1][0]
    conv_cols = conv_w.shape[-1]
    small_grads["conv_w"] = lax.dynamic_slice_in_dim(small_grads["conv_w"], chip * conv_cols, conv_cols, axis=2)
    grad.update(small_grads)
    sshapes = [w[n].shape for n in SMALL_NAMES]
    packed = [_pack([d[n] for n in SMALL_NAMES]) for d in (w, grad, m, v)]
    for dst, t in zip((delta, new_m, new_v), adamw(*packed, "adamw_small")):
        dst.update(zip(SMALL_NAMES, _unpack(t, sshapes)))

    wait_m0, tok_m0 = start_scatter([dw_in0, by_chip_out(dw_out0)], "mix0", dep=total)
    (pf0, rf0), (pm1, rm1), (pf1, rf1) = (wt(tok_m0) for wt in (wait_f0, wait_m1, wait_f1))

    def reduce_group(parts, from_chips):
        mine = [add_chips(p, r, chip.reshape(1)) for p, r in zip(parts, from_chips)]
        return mine, list(share_halves(mine))

    mine_f, theirs_f = reduce_group(pf0 + pf1 + pm1, rf0 + rf1 + rm1)
    for a, n in enumerate(("w_gate_up", "w_down")):
        grad[n], delta[n], new_m[n], new_v[n] = adamw_shard(w[n], m[n], v[n], [mine_f[a], mine_f[2 + a]],
                                                            [theirs_f[a], theirs_f[2 + a]], core.reshape(1), "adamw_" + n)

    mine_m0, theirs_m0 = reduce_group(*wait_m0(new_v["w_down"]))
    tr_ = lambda t: jnp.swapaxes(t, -1, -2)
    cut = lambda t: tr_(t[:, :IN_SHARD])
    res = adamw_shard_t(tr_(w_in), tr_(m_w_in), tr_(v_w_in), [cut(mine_m0[0]), cut(mine_f[4])],
                        [cut(theirs_m0[0]), cut(theirs_f[4])], core.reshape(1), "adamw_w_in")
    grad["w_in"], delta["w_in"], new_m["w_in"], new_v["w_in"] = (tr_(t) for t in res)
    grad["w_out"], delta["w_out"], new_m["w_out"], new_v["w_out"] = adamw_shard(
        w_out, m_w_out, v_w_out, [mine_m0[1], mine_f[5]], [theirs_m0[1], theirs_f[5]], core.reshape(1), "adamw_w_out")

    out = [loss, dx[None]]
    for d in (grad, delta, new_m, new_v):
        out += [d[n] for n in WEIGHT_ORDER]
    return tuple(out)
```

```python
import functools
import math

import numpy as np
import jax
import jax.numpy as jnp
from jax import lax
from jax.experimental import pallas as pl
from jax.experimental.pallas import tpu as pltpu

F32 = jnp.float32
BF16 = jnp.bfloat16
HI = lax.Precision.HIGH

T = 2048
D = 2048
DEPTH = 2
HD = 128
A_GROUPS, A_WIDTH, A_CHUNK = 4, 512, 128
B_HEADS, B_WIDTH, B_CONV, B_CHUNK = 6, 768, 4, 64
C_HEADS, C_WIDTH, C_BLOCK = 6, 768, 128
C_BRANCHES = ((128, 1), (512, 4), (2048, 16))
FFN = 5632
IN_TOTAL = 6412
EPS = 1e-6
N_CHIPS = 4
N_DEV = 8
IN_SHARD = IN_TOTAL // N_CHIPS
IN_SHARD_PAD = 1664
GU_SHARD = 2 * FFN // N_CHIPS
OUT_SHARD = D // N_CHIPS
DOWN_SHARD = FFN // N_CHIPS
P_AU, P_AV, P_BQ, P_BK, P_BV, P_BG, P_BB, P_CQ, P_CK, P_CV, P_END = (
    0, 512, 1024, 1792, 2560, 3328, 4096, 4224, 4992, 5760, 6528)
GATE_COLS = 4108
VMEM_LIMIT = 56 * 1024 * 1024

ADAM_LR, ADAM_B1, ADAM_B2, ADAM_EPS, ADAM_WD, ADAM_STEP = 0.001, 0.9, 0.999, 1e-08, 0.01, 10


def _cparams(sem, vmem=VMEM_LIMIT):
    return pltpu.CompilerParams(dimension_semantics=sem, vmem_limit_bytes=vmem)


def _dims(nd, ta, tb):
    off = nd - 2
    ca = off + (0 if ta else 1)
    cb = off + (1 if tb else 0)
    batch = ((0,), (0,)) if nd == 3 else ((), ())
    return (((ca,), (cb,)), batch)


def _raw_mm(a, b, ta, tb, hi):
    if hi:
        return lax.dot_general(a, b, _dims(a.ndim, ta, tb), precision=HI, preferred_element_type=F32)
    return lax.dot_general(a.astype(BF16), b.astype(BF16), _dims(a.ndim, ta, tb), preferred_element_type=F32)


@functools.partial(jax.custom_vjp, nondiff_argnums=(2, 3, 4))
def _mm(a, b, ta=False, tb=False, hi=False):
    return _raw_mm(a, b, ta, tb, hi)


def _mm_fwd(a, b, ta, tb, hi):
    return _raw_mm(a, b, ta, tb, hi), (a, b)


def _mm_bwd(ta, tb, hi, res, g):
    a, b = res
    da = _raw_mm(g, b, False, not tb, False) if not ta else _raw_mm(b, g, tb, True, False)
    db = _raw_mm(a, g, not ta, False, False) if not tb else _raw_mm(g, a, True, ta, False)
    return da.astype(a.dtype), db.astype(b.dtype)


_mm.defvjp(_mm_fwd, _mm_bwd)


def _rms(x, g):
    return x * lax.rsqrt(jnp.mean(x * x, axis=-1, keepdims=True) + EPS) * g


def _gelu(x):
    return 0.5 * x * (1.0 + jnp.tanh(math.sqrt(2.0 / math.pi) * (x + 0.044715 * (x * x * x))))


def _sigmoid(x):
    return 1.0 / (1.0 + jnp.exp(-x))


def _silu(x):
    return x * _sigmoid(x)


def _softplus(x):
    return jnp.maximum(x, 0.0) + jnp.log(1.0 + jnp.exp(-jnp.abs(x)))


def _iota(shape, dim):
    return lax.broadcasted_iota(jnp.int32, shape, dim)


def _sgu_fn(u, v, sg, w, b):
    nc = T // A_CHUNK
    ug = _gelu(u)
    vn = _rms(_gelu(v), sg)
    causal = _iota((A_CHUNK, A_CHUNK), 0) >= _iota((A_CHUNK, A_CHUNK), 1)
    wm = jnp.where(causal, w, 0.0)
    wb = jnp.broadcast_to(wm[None], (nc, A_CHUNK, A_CHUNK))
    z = _mm(wb, vn.reshape(nc, A_CHUNK, HD)) + b[None]
    return ug * z.reshape(T, HD)


def _sgu_specs():
    col = lambda off: pl.BlockSpec((T, HD), lambda g, off=off: (0, off + g))
    par = [pl.BlockSpec((None, 1, HD), lambda g: (g, 0, 0)),
           pl.BlockSpec((None, A_CHUNK, A_CHUNK), lambda g: (g, 0, 0)),
           pl.BlockSpec((None, A_CHUNK, 1), lambda g: (g, 0, 0))]
    return col, par


def sgu_fwd(p2, sg, w, b):
    col, par = _sgu_specs()

    def body(u_ref, v_ref, sg_ref, w_ref, b_ref, y_ref):
        y_ref[...] = _sgu_fn(u_ref[...], v_ref[...], sg_ref[...], w_ref[...], b_ref[...]).astype(BF16)

    return pl.pallas_call(
        body, name="sgu_fwd", grid=(A_GROUPS,),
        in_specs=[col(P_AU // HD), col(P_AV // HD)] + par,
        out_specs=pl.BlockSpec((T, HD), lambda g: (0, g)),
        out_shape=jax.ShapeDtypeStruct((T, A_WIDTH), BF16),
        compiler_params=_cparams(("arbitrary",)),
    )(p2, p2, sg, w, b)


def sgu_bwd(p2, sg, w, b, dmix):
    col, par = _sgu_specs()

    def body(u_ref, v_ref, sg_ref, w_ref, b_ref, dy_ref, du_ref, dv_ref, dsg_ref, dw_ref, db_ref):
        _, vjp = jax.vjp(_sgu_fn, u_ref[...], v_ref[...], sg_ref[...], w_ref[...], b_ref[...])
        du, dv, dsg, dw, db = vjp(dy_ref[...])
        du_ref[...] = du.astype(BF16)
        dv_ref[...] = dv.astype(BF16)
        dsg_ref[...] = dsg
        dw_ref[...] = dw
        db_ref[...] = db

    gcol = pl.BlockSpec((T, HD), lambda g: (0, g))
    return pl.pallas_call(
        body, name="sgu_bwd", grid=(A_GROUPS,),
        in_specs=[col(P_AU // HD), col(P_AV // HD)] + par + [gcol],
        out_specs=[gcol, gcol] + par,
        out_shape=[jax.ShapeDtypeStruct((T, A_WIDTH), BF16), jax.ShapeDtypeStruct((T, A_WIDTH), BF16),
                   jax.ShapeDtypeStruct((A_GROUPS, 1, HD), F32), jax.ShapeDtypeStruct((A_GROUPS, A_CHUNK, A_CHUNK), F32),
                   jax.ShapeDtypeStruct((A_GROUPS, A_CHUNK, 1), F32)],
        compiler_params=_cparams(("arbitrary",)),
    )(p2, p2, sg, w, b, dmix)


def _attn_fn(q, k, v, qg, kg, slope, *, dil, nb):
    n = T // C_BLOCK
    qb = _rms(q, qg).reshape(n, C_BLOCK, HD)
    kb = _rms(k, kg).reshape(n, C_BLOCK, HD)
    vb = v.reshape(n, C_BLOCK, HD)
    scale = HD ** -0.5
    qi = _iota((n, C_BLOCK, C_BLOCK), 1)
    kj = _iota((n, C_BLOCK, C_BLOCK), 2)
    sl = slope[None] * float(dil)
    d_cur = qi - kj
    sc = jnp.where(d_cur >= 0, _mm(qb, kb, tb=True) * scale - sl * d_cur.astype(F32), -jnp.inf)
    mx = jnp.max(sc, axis=-1, keepdims=True)
    if nb > 1:
        kp = jnp.concatenate([jnp.zeros((1, C_BLOCK, HD), F32), kb[:-1]], axis=0)
        vp = jnp.concatenate([jnp.zeros((1, C_BLOCK, HD), F32), vb[:-1]], axis=0)
        has_prev = (_iota((n, C_BLOCK, C_BLOCK), 0) % nb) > 0
        d_prev = C_BLOCK + qi - kj
        sp = jnp.where((kj >= qi) & has_prev, _mm(qb, kp, tb=True) * scale - sl * d_prev.astype(F32), -jnp.inf)
        mx = jnp.maximum(mx, jnp.max(sp, axis=-1, keepdims=True))
    p = jnp.exp(sc - mx)
    den = jnp.sum(p, axis=-1, keepdims=True)
    if nb > 1:
        pp = jnp.exp(sp - mx)
        den = den + jnp.sum(pp, axis=-1, keepdims=True)
    out = _mm(p / den, vb)
    if nb > 1:
        out = out + _mm(pp / den, vp)
    lse = mx + jnp.log(den)
    return out.reshape(T, HD), jnp.broadcast_to(lse, (n, C_BLOCK, HD)).reshape(T, HD)


def _combine_fn(o1, o2, o3, l1, l2, l3):
    mx = jnp.maximum(jnp.maximum(l1, l2), l3)
    e1, e2, e3 = jnp.exp(l1 - mx), jnp.exp(l2 - mx), jnp.exp(l3 - mx)
    s = e1 + e2 + e3
    return (e1 / s) * o1 + (e2 / s) * o2 + (e3 / s) * o3


def _branch_blocks(dil):
    return -(-(T // dil) // C_BLOCK)


def _load_branch_order(ref, dil):
    if dil == 1:
        return ref[...]
    seg = T // dil
    return jnp.concatenate([ref[pl.ds(r, seg, stride=dil), :] for r in range(dil)], axis=0)


def _store_position_order(ref, val, dil, add=False):
    seg = T // dil
    for r in range(dil):
        rows = slice(None) if dil == 1 else pl.ds(r, seg, stride=dil)
        piece = val if dil == 1 else val[r * seg:(r + 1) * seg]
        if add:
            ref[rows, :] += piece
        else:
            ref[rows, :] = piece


def _dattn_specs():
    col = lambda off: pl.BlockSpec((T, HD), lambda h, off=off: (0, off // HD + h))
    row = pl.BlockSpec((1, HD), lambda h: (0, 0))
    slope = pl.BlockSpec((None, 1, HD), lambda h: (h, 0, 0))
    return [col(P_CQ), col(P_CK), col(P_CV), row, row, slope]


def _dattn_branches(q_ref, k_ref, v_ref, qg, kg, slope, o_scr, l_scr):
    for b, (_, dil) in enumerate(C_BRANCHES):
        q, k, v = (_load_branch_order(r, dil) for r in (q_ref, k_ref, v_ref))
        o, l = _attn_fn(q, k, v, qg, kg, slope, dil=dil, nb=_branch_blocks(dil))
        _store_position_order(o_scr.at[b], o, dil)
        _store_position_order(l_scr.at[b], l, dil)


def dattn_fwd(p2, qg, kg, slopes):
    per_branch = pl.BlockSpec((3, T, HD), lambda h: (0, 0, h))

    def body(q_ref, k_ref, v_ref, qg_ref, kg_ref, s_ref, y_ref, o_ref, l_ref):
        _dattn_branches(q_ref, k_ref, v_ref, qg_ref[...], kg_ref[...], s_ref[...], o_ref, l_ref)
        y_ref[...] = _combine_fn(o_ref[0], o_ref[1], o_ref[2], l_ref[0], l_ref[1], l_ref[2]).astype(BF16)

    return pl.pallas_call(
        body, name="dattn_fwd", grid=(C_HEADS,), in_specs=_dattn_specs(),
        out_specs=[pl.BlockSpec((T, HD), lambda h: (0, h)), per_branch, per_branch],
        out_shape=[jax.ShapeDtypeStruct((T, C_WIDTH), BF16)] + [jax.ShapeDtypeStruct((3, T, C_WIDTH), F32)] * 2,
        compiler_params=_cparams(("arbitrary",)),
    )(p2, p2, p2, qg, kg, slopes)


def dattn_bwd(p2, qg, kg, slopes, outs, lses, dmix):
    hcol = pl.BlockSpec((T, HD), lambda h: (0, h))
    row = pl.BlockSpec((1, HD), lambda h: (0, 0))
    dy = pl.BlockSpec((T, HD), lambda h: (0, (A_WIDTH + B_WIDTH) // HD + h))
    per_branch = pl.BlockSpec((3, T, HD), lambda h: (0, 0, h))

    def body(q_ref, k_ref, v_ref, qg_ref, kg_ref, s_ref, o_scr, l_scr, dy_ref, dq_ref, dk_ref, dv_ref, dqg_ref, dkg_ref, g_scr,
             acc):
        qg, kg, slope = qg_ref[...], kg_ref[...], s_ref[...]
        _, vjp = jax.vjp(_combine_fn, o_scr[0], o_scr[1], o_scr[2], l_scr[0], l_scr[1], l_scr[2])
        for i, g in enumerate(vjp(dy_ref[...])):
            g_scr[i] = g

        @pl.when(pl.program_id(0) == 0)
        def _():
            dqg_ref[...] = jnp.zeros_like(dqg_ref)
            dkg_ref[...] = jnp.zeros_like(dkg_ref)

        for b, (_, dil) in enumerate(C_BRANCHES):
            q, k, v = (_load_branch_order(r, dil) for r in (q_ref, k_ref, v_ref))
            do, dl = _load_branch_order(g_scr.at[b], dil), _load_branch_order(g_scr.at[3 + b], dil)
            fn = functools.partial(_attn_fn, dil=dil, nb=_branch_blocks(dil))
            _, vjp_b = jax.vjp(lambda a, b_, c, d, e, fn=fn: fn(a, b_, c, d, e, slope), q, k, v, qg, kg)
            dq, dk, dv, dqg, dkg = vjp_b((do, dl))
            for i, val in enumerate((dq, dk, dv)):
                _store_position_order(acc.at[i], val, dil, add=b > 0)
            dqg_ref[...] += dqg
            dkg_ref[...] += dkg
        for i, ref in enumerate((dq_ref, dk_ref, dv_ref)):
            ref[...] = acc[i].astype(BF16)

    scr = lambda n: pltpu.VMEM((n, T, HD), F32)
    return pl.pallas_call(
        body, name="dattn_bwd", grid=(C_HEADS,), in_specs=_dattn_specs() + [per_branch, per_branch, dy],
        out_specs=[hcol, hcol, hcol, row, row],
        out_shape=[jax.ShapeDtypeStruct((T, C_WIDTH), BF16)] * 3 + [jax.ShapeDtypeStruct((1, HD), F32)] * 2,
        scratch_shapes=[scr(6), scr(3)], compiler_params=_cparams(("arbitrary",)),
    )(p2, p2, p2, qg, kg, slopes, outs, lses, dmix)


_NCH = T // B_CHUNK


def _conv_taps(x, w_ref):
    rows = _iota(x.shape, 0)
    taps = []
    for j in range(B_CONV):
        s = B_CONV - 1 - j
        taps.append(x if s == 0 else jnp.where(rows >= s, pltpu.roll(x, s, 0), 0.0))
    pre = sum(w_ref[j:j + 1, :] * taps[j] for j in range(B_CONV))
    return pre, taps


def _conv_post(pre, mode):
    y = _silu(pre)
    if mode == "v":
        return y
    y = y * lax.rsqrt(jnp.sum(y * y, axis=-1, keepdims=True) + EPS)
    return y * (HD ** -0.5) if mode == "q" else y


def conv_fwd(p2, conv_w, mode):
    idx = "qkv".index(mode)
    xcol = pl.BlockSpec((T, HD), lambda h: (0, P_BQ // HD + B_HEADS * idx + h))
    wcol = pl.BlockSpec((B_CONV, HD), lambda h: (0, B_HEADS * idx + h))
    hcol = pl.BlockSpec((T, HD), lambda h: (0, h))

    def body(x_ref, w_ref, y_ref):
        pre, _ = _conv_taps(x_ref[...], w_ref)
        y_ref[...] = _conv_post(pre, mode)

    return pl.pallas_call(
        body, name=f"conv_fwd_{mode}", grid=(B_HEADS,), in_specs=[xcol, wcol], out_specs=hcol,
        out_shape=jax.ShapeDtypeStruct((T, B_WIDTH), F32), compiler_params=_cparams(("arbitrary",)),
    )(p2, conv_w)


def conv_bwd(p2, conv_w, dys, mode):
    idx = "qkv".index(mode)
    xcol = pl.BlockSpec((T, HD), lambda h: (0, P_BQ // HD + B_HEADS * idx + h))
    wcol = pl.BlockSpec((B_CONV, HD), lambda h: (0, B_HEADS * idx + h))
    hcol = pl.BlockSpec((T, HD), lambda h: (0, h))
    wout = pl.BlockSpec((B_CONV, HD), lambda h: (0, h))

    def body(x_ref, w_ref, *rest):
        dy_refs, (dx_ref, dw_ref) = rest[:-2], rest[-2:]
        pre, taps = _conv_taps(x_ref[...], w_ref)
        _, vjp = jax.vjp(functools.partial(_conv_post, mode=mode), pre)
        (dpre,) = vjp(sum(r[...] for r in dy_refs))
        rows = _iota(dpre.shape, 0)
        dx = w_ref[B_CONV - 1:B_CONV, :] * dpre
        for j in range(B_CONV):
            s = B_CONV - 1 - j
            dw_ref[j:j + 1, :] = jnp.sum(dpre * taps[j], axis=0, keepdims=True)
            if s > 0:
                dx = dx + w_ref[j:j + 1, :] * jnp.where(rows < T - s, pltpu.roll(dpre, T - s, 0), 0.0)
        dx_ref[...] = dx.astype(BF16)

    return pl.pallas_call(
        body, name=f"conv_bwd_{mode}", grid=(B_HEADS,), in_specs=[xcol, wcol] + [hcol] * len(dys), out_specs=[hcol, wout],
        out_shape=[jax.ShapeDtypeStruct((T, B_WIDTH), BF16), jax.ShapeDtypeStruct((B_CONV, B_WIDTH), F32)],
        compiler_params=_cparams(("arbitrary",)),
    )(p2, conv_w, *dys)


def _gates_fn(bg, al, dtb, h):
    r = _iota((HD, HD), 0)
    logit = _mm(bg, (r == h).astype(F32), hi=True)
    a = _mm(bg, (r == h + B_HEADS).astype(F32), hi=True)
    beta = _sigmoid(logit)
    graw = -jnp.exp(al) * _softplus(a + dtb)
    tri = (_iota((_NCH, B_CHUNK, B_CHUNK), 1) >= _iota((_NCH, B_CHUNK, B_CHUNK), 2)).astype(F32)
    g = _mm(tri, graw.reshape(_NCH, B_CHUNK, HD), hi=True).reshape(T, HD)
    return beta, g


def _gates_specs():
    bg = pl.BlockSpec((T, HD), lambda h: (0, P_BB // HD))
    par = pl.BlockSpec((None, 1, HD), lambda h: (h, 0, 0))
    out = pl.BlockSpec((None, T, HD), lambda h: (h, 0, 0))
    return bg, par, out


def gates_fwd(p2, al, dtb):
    bg, par, out = _gates_specs()

    def body(bg_ref, al_ref, dtb_ref, beta_ref, g_ref):
        beta, g = _gates_fn(bg_ref[...], al_ref[...], dtb_ref[...], pl.program_id(0))
        beta_ref[...] = beta
        g_ref[...] = g

    return pl.pallas_call(
        body, name="gates_fwd", grid=(B_HEADS,), in_specs=[bg, par, par], out_specs=[out, out],
        out_shape=[jax.ShapeDtypeStruct((B_HEADS, T, HD), F32)] * 2, compiler_params=_cparams(("arbitrary",)),
    )(p2, al, dtb)


def gates_bwd(p2, al, dtb, dbeta, dg1, dg2):
    bg, par, out = _gates_specs()
    acc = pl.BlockSpec((T, HD), lambda h: (0, 0))

    def body(bg_ref, al_ref, dtb_ref, dbeta_ref, dg1_ref, dg2_ref, dbg_ref, dal_ref, ddtb_ref, acc_ref):
        h = pl.program_id(0)
        _, vjp = jax.vjp(lambda a, b, c: _gates_fn(a, b, c, h), bg_ref[...], al_ref[...], dtb_ref[...])
        dbg, dal, ddtb = vjp((dbeta_ref[...], dg1_ref[...] + dg2_ref[...]))

        @pl.when(h == 0)
        def _():
            acc_ref[...] = jnp.zeros_like(acc_ref)

        acc_ref[...] += dbg
        dbg_ref[...] = acc_ref[...].astype(BF16)
        dal_ref[...] = jnp.broadcast_to(jnp.sum(dal, axis=-1, keepdims=True), (1, HD))
        ddtb_ref[...] = jnp.broadcast_to(jnp.sum(ddtb, axis=-1, keepdims=True), (1, HD))

    return pl.pallas_call(
        body, name="gates_bwd", grid=(B_HEADS,), in_specs=[bg, par, par, out, out, out], out_specs=[acc, par, par],
        out_shape=[jax.ShapeDtypeStruct((T, HD), BF16)] + [jax.ShapeDtypeStruct((B_HEADS, 1, HD), F32)] * 2,
        scratch_shapes=[pltpu.VMEM((T, HD), F32)], compiler_params=_cparams(("arbitrary",)),
    )(p2, al, dtb, dbeta, dg1, dg2)


def _unit_lower_inverse(a):
    eye = (_iota(a.shape, 1) == _iota(a.shape, 2)).astype(F32)
    x = eye - a
    p = _mm(a, a, hi=True)
    for i in range(5):
        x = x + _mm(x, p, hi=True)
        if i < 4:
            p = _mm(p, p, hi=True)
    return x


_WY_CH = 8
_WY_ROWS = _WY_CH * B_CHUNK


def _wy_fn(q, k, v, beta, g):
    sh = (q.shape[0] // B_CHUNK, B_CHUNK, HD)
    q3, k3, v3, b3, g3 = (t.reshape(sh) for t in (q, k, v, beta, g))
    gd = g3[:, :, :B_CHUNK] - jnp.swapaxes(g3, 1, 2)[:, :B_CHUNK, :]
    ii, jj = _iota(gd.shape, 1), _iota(gd.shape, 2)
    decay = jnp.exp(jnp.where(ii >= jj, gd, -jnp.inf))
    kb = k3 * b3
    a = _mm(kb, k3, tb=True) * jnp.where(ii > jj, decay, 0.0)
    tinv = _unit_lower_inverse(a)
    u = _mm(tinv, v3 * b3, hi=True)
    w = _mm(tinv, kb * jnp.exp(g3), hi=True)
    attn = _mm(q3, k3, tb=True) * decay
    return u.reshape(q.shape), w.reshape(q.shape), attn


def _wy_specs():
    hcol = pl.BlockSpec((_WY_ROWS, HD), lambda h, i: (i, h))
    hb = pl.BlockSpec((None, _WY_ROWS, HD), lambda h, i: (h, i, 0))
    at = pl.BlockSpec((None, _WY_CH, B_CHUNK, B_CHUNK), lambda h, i: (h, i, 0, 0))
    return hcol, hb, at


_WY_GRID = (B_HEADS, _NCH // _WY_CH)


def wy_fwd(q, k, v, beta, g):
    hcol, hb, at = _wy_specs()

    def body(q_ref, k_ref, v_ref, b_ref, g_ref, u_ref, w_ref, a_ref):
        u, w, a = _wy_fn(q_ref[...], k_ref[...], v_ref[...], b_ref[...], g_ref[...])
        u_ref[...] = u
        w_ref[...] = w
        a_ref[...] = a

    return pl.pallas_call(
        body, name="wy_fwd", grid=_WY_GRID, in_specs=[hcol, hcol, hcol, hb, hb], out_specs=[hcol, hcol, at],
        out_shape=[jax.ShapeDtypeStruct((T, B_WIDTH), F32)] * 2 + [jax.ShapeDtypeStruct((B_HEADS, _NCH, B_CHUNK, B_CHUNK), F32)],
        compiler_params=_cparams(("arbitrary", "arbitrary")),
    )(q, k, v, beta, g)


def wy_bwd(q, k, v, beta, g, du, dw, dattn):
    hcol, hb, at = _wy_specs()

    def body(q_ref, k_ref, v_ref, b_ref, g_ref, du_ref, dw_ref, da_ref, dq_ref, dk_ref, dv_ref, db_ref, dg_ref):
        _, vjp = jax.vjp(_wy_fn, q_ref[...], k_ref[...], v_ref[...], b_ref[...], g_ref[...])
        for r, t in zip((dq_ref, dk_ref, dv_ref, db_ref, dg_ref), vjp((du_ref[...], dw_ref[...], da_ref[...]))):
            r[...] = t

    return pl.pallas_call(
        body, name="wy_bwd", grid=_WY_GRID, in_specs=[hcol, hcol, hcol, hb, hb, hcol, hcol, at],
        out_specs=[hcol, hcol, hcol, hb, hb],
        out_shape=[jax.ShapeDtypeStruct((T, B_WIDTH), F32)] * 3 + [jax.ShapeDtypeStruct((B_HEADS, T, HD), F32)] * 2,
        compiler_params=_cparams(("arbitrary", "arbitrary")),
    )(q, k, v, beta, g, du, dw, dattn)


def _scan_step_fn(q, k, u, w, g, attn, gate, og, s):
    v_new = u - _mm(w, s)
    o = _mm(q * jnp.exp(g), s) + _mm(attn, v_new)
    g_last = jnp.sum(jnp.where(_iota(g.shape, 0) == B_CHUNK - 1, g, 0.0), axis=0, keepdims=True)
    s_new = s * jnp.exp(g_last) + _mm(k * jnp.exp(g_last - g), v_new, ta=True)
    return _rms(o, og) * _silu(gate), s_new


def _scan_specs(rev):
    ch = (lambda n: _NCH - 1 - n) if rev else (lambda n: n)
    rows = pl.BlockSpec((B_CHUNK, B_WIDTH), lambda n: (ch(n), 0))
    gb = pl.BlockSpec((B_HEADS, B_CHUNK, HD), lambda n: (0, ch(n), 0))
    at = pl.BlockSpec((B_HEADS, None, B_CHUNK, B_CHUNK), lambda n: (0, ch(n), 0, 0))
    og = pl.BlockSpec((1, HD), lambda n: (0, 0))
    st = pl.BlockSpec((None, B_HEADS, HD, HD), lambda n: (ch(n), 0, 0, 0))
    return rows, gb, at, og, st


def scan_fwd(q, k, u, w, g, attn, gate, og):
    rows, gb, at, ogs, st = _scan_specs(False)

    def body(q_ref, k_ref, u_ref, w_ref, g_ref, a_ref, gate_ref, og_ref, y_ref, st_ref, s_ref):
        @pl.when(pl.program_id(0) == 0)
        def _():
            s_ref[...] = jnp.zeros_like(s_ref)

        for h in range(B_HEADS):
            c = slice(h * HD, (h + 1) * HD)
            s = s_ref[h]
            st_ref[h] = s
            y, s_new = _scan_step_fn(q_ref[:, c], k_ref[:, c], u_ref[:, c], w_ref[:, c], g_ref[h], a_ref[h],
                                     gate_ref[:, c], og_ref[...], s)
            y_ref[:, c] = y.astype(BF16)
            s_ref[h] = s_new

    return pl.pallas_call(
        body, name="scan_fwd", grid=(_NCH,), in_specs=[rows, rows, rows, rows, gb, at, rows, ogs], out_specs=[rows, st],
        out_shape=[jax.ShapeDtypeStruct((T, B_WIDTH), BF16), jax.ShapeDtypeStruct((_NCH, B_HEADS, HD, HD), F32)],
        scratch_shapes=[pltpu.VMEM((B_HEADS, HD, HD), F32)], compiler_params=_cparams(("arbitrary",)),
    )(q, k, u, w, g, attn, gate, og)


def scan_bwd(q, k, u, w, g, attn, gate, og, states, dmix):
    rows, gb, at, ogs, st = _scan_specs(True)
    dyb = pl.BlockSpec((B_CHUNK, HD), lambda n: (_NCH - 1 - n, 0))

    def body(q_ref, k_ref, u_ref, w_ref, g_ref, a_ref, gate_ref, og_ref, st_ref, *rest):
        dy_refs, (dq_ref, dk_ref, du_ref, dw_ref, dgate_ref, dg_ref, da_ref, dog_ref, ds_ref) = rest[:B_HEADS], rest[B_HEADS:]

        @pl.when(pl.program_id(0) == 0)
        def _():
            ds_ref[...] = jnp.zeros_like(ds_ref)
            dog_ref[...] = jnp.zeros_like(dog_ref)

        for h in range(B_HEADS):
            c = slice(h * HD, (h + 1) * HD)
            _, vjp = jax.vjp(_scan_step_fn, q_ref[:, c], k_ref[:, c], u_ref[:, c], w_ref[:, c], g_ref[h], a_ref[h],
                             gate_ref[:, c], og_ref[...], st_ref[h])
            dq, dk, du, dw, dg, da, dgate, dog, ds = vjp((dy_refs[h][...], ds_ref[h]))
            dq_ref[:, c] = dq
            dk_ref[:, c] = dk
            du_ref[:, c] = du
            dw_ref[:, c] = dw
            dgate_ref[:, c] = dgate.astype(BF16)
            dg_ref[h] = dg
            da_ref[h] = da
            dog_ref[...] += dog
            ds_ref[h] = ds

    dy_specs = [pl.BlockSpec((B_CHUNK, HD), lambda n, h=h: (_NCH - 1 - n, A_WIDTH // HD + h)) for h in range(B_HEADS)]
    return pl.pallas_call(
        body, name="scan_bwd", grid=(_NCH,),
        in_specs=[rows, rows, rows, rows, gb, at, rows, ogs, st] + dy_specs,
        out_specs=[rows] * 5 + [gb, at, ogs],
        out_shape=[jax.ShapeDtypeStruct((T, B_WIDTH), F32)] * 4 + [jax.ShapeDtypeStruct((T, B_WIDTH), BF16)]
        + [jax.ShapeDtypeStruct((B_HEADS, T, HD), F32), jax.ShapeDtypeStruct((B_HEADS, _NCH, B_CHUNK, B_CHUNK), F32),
           jax.ShapeDtypeStruct((1, HD), F32)],
        scratch_shapes=[pltpu.VMEM((B_HEADS, HD, HD), F32)], compiler_params=_cparams(("arbitrary",)),
    )(q, k, u, w, g, attn, gate, og, states, *([dmix] * B_HEADS))


def _lanes(vec):
    return jnp.broadcast_to(vec[:, None, None], (vec.shape[0], 1, HD))


def gdn_forward(p2, conv_w, a_log, dt_bias, og):
    qa, ka, va = (conv_fwd(p2, conv_w, m) for m in "qkv")
    beta, g = gates_fwd(p2, _lanes(a_log), _lanes(dt_bias))
    u, w, attn = wy_fwd(qa, ka, va, beta, g)
    gate = p2[:, P_BG:P_BB]
    y, states = scan_fwd(qa, ka, u, w, g, attn, gate, og)
    return y, (qa, ka, va, beta, g, u, w, attn, gate, states)


def gdn_backward(p2, conv_w, a_log, dt_bias, og, saved, dmix):
    qa, ka, va, beta, g, u, w, attn, gate, states = saved
    dq1, dk1, du, dw, dgate, dg1, dattn, dog = scan_bwd(qa, ka, u, w, g, attn, gate, og, states, dmix)
    dq2, dk2, dv, dbeta, dg2 = wy_bwd(qa, ka, va, beta, g, du, dw, dattn)
    dbg, dal, ddtb = gates_bwd(p2, _lanes(a_log), _lanes(dt_bias), dbeta, dg1, dg2)
    dxq, dwq = conv_bwd(p2, conv_w, [dq1, dq2], "q")
    dxk, dwk = conv_bwd(p2, conv_w, [dk1, dk2], "k")
    dxv, dwv = conv_bwd(p2, conv_w, [dv], "v")
    return [dxq, dxk, dxv, dgate, dbg], jnp.concatenate([dwq, dwk, dwv], axis=1), dal[:, 0, 0], ddtb[:, 0, 0], dog


_SLOPES = np.exp2(-8.0 * (np.arange(C_HEADS, dtype=np.float64) + 1.0) / C_HEADS).astype(np.float32)


def _alibi_slopes():
    return _lanes(jnp.asarray(_SLOPES))


_ROWS = 256
_TM = 1024
_TM_FFN = 512


def _dep_specs(dep, ngrid):
    if dep is None:
        return [], []
    return [dep], [pl.BlockSpec((8, HD), lambda *_: (0, 0))]


def rmsnorm_fwd(x, g, dep=None):
    blk = pl.BlockSpec((_ROWS, D), lambda i: (i, 0))
    deps, dspecs = _dep_specs(dep, 1)

    def body(x_ref, g_ref, *rest):
        rest[-1][...] = _rms(x_ref[...], g_ref[...]).astype(BF16)

    return pl.pallas_call(
        body, name="rmsnorm_fwd", grid=(T // _ROWS,), in_specs=[blk, pl.BlockSpec((1, D), lambda i: (0, 0))] + dspecs,
        out_specs=blk, out_shape=jax.ShapeDtypeStruct((T, D), BF16), compiler_params=_cparams(("arbitrary",)),
    )(x, g, *deps)


def rmsnorm_bwd(x, g, dh, dres):
    blk = pl.BlockSpec((_ROWS, D), lambda i: (i, 0))
    row = pl.BlockSpec((1, D), lambda i: (0, 0))

    def body(x_ref, g_ref, dh_ref, dres_ref, dx_ref, dg_ref):
        _, vjp = jax.vjp(_rms, x_ref[...], g_ref[...])
        dx, dg = vjp(dh_ref[...])
        dx_ref[...] = dres_ref[...] + dx

        @pl.when(pl.program_id(0) == 0)
        def _():
            dg_ref[...] = jnp.zeros_like(dg_ref)

        dg_ref[...] += dg

    return pl.pallas_call(
        body, name="rmsnorm_bwd", grid=(T // _ROWS,), in_specs=[blk, row, blk, blk], out_specs=[blk, row],
        out_shape=[jax.ShapeDtypeStruct((T, D), F32), jax.ShapeDtypeStruct((1, D), F32)],
        compiler_params=_cparams(("arbitrary",)),
    )(x, g, dh, dres)


def _matmul(name, a, b, *, grid, a_spec, b_spec, o_spec, out_shape, ta=False, tb=False, k_axis=None, res=None, dep=None):
    dims = _dims(2, ta, tb)
    deps, dspecs = _dep_specs(dep, len(grid))

    def body(a_ref, b_ref, *rest):
        o_ref = rest[-1]
        prod = lax.dot_general(a_ref[...].astype(BF16), b_ref[...].astype(BF16), dims, preferred_element_type=F32)
        if res is not None:
            prod = prod + rest[0][...]
        if k_axis is None:
            o_ref[...] = prod.astype(o_ref.dtype)
        else:
            @pl.when(pl.program_id(k_axis) == 0)
            def _():
                o_ref[...] = prod

            @pl.when(pl.program_id(k_axis) > 0)
            def _():
                o_ref[...] += prod

    sem = tuple("arbitrary" for _ in grid)
    ins = [a, b] + ([res] if res is not None else []) + deps
    specs = [a_spec, b_spec] + ([o_spec] if res is not None else []) + dspecs
    return pl.pallas_call(
        body, name=name, grid=grid, in_specs=specs, out_specs=o_spec, out_shape=out_shape, compiler_params=_cparams(sem),
    )(*ins)


_IN_TN = P_END // 3


def mm_proj(h1, wp_in, l):
    return _matmul(
        "mm_proj", h1, wp_in, grid=(P_END // _IN_TN, T // _TM),
        a_spec=pl.BlockSpec((_TM, D), lambda j, i: (i, 0)),
        b_spec=pl.BlockSpec((None, D, _IN_TN), lambda j, i: (l, 0, j)),
        o_spec=pl.BlockSpec((_TM, _IN_TN), lambda j, i: (i, j)), out_shape=jax.ShapeDtypeStruct((T, P_END), F32))


def mm_dh1(dp2, wp_in, l):
    return _matmul(
        "mm_dh1", dp2, wp_in, grid=(T // _TM, P_END // _IN_TN), tb=True, k_axis=1,
        a_spec=pl.BlockSpec((_TM, _IN_TN), lambda i, k: (i, k)),
        b_spec=pl.BlockSpec((None, D, _IN_TN), lambda i, k: (l, 0, k)),
        o_spec=pl.BlockSpec((_TM, D), lambda i, k: (i, 0)), out_shape=jax.ShapeDtypeStruct((T, D), F32))


def mm_dwin(h1, dp2):
    return _matmul(
        "mm_dwin", h1, dp2, grid=(P_END // _IN_TN, D // _TM), ta=True,
        a_spec=pl.BlockSpec((T, _TM), lambda j, i: (0, i)),
        b_spec=pl.BlockSpec((T, _IN_TN), lambda j, i: (0, j)),
        o_spec=pl.BlockSpec((_TM, _IN_TN), lambda j, i: (i, j)), out_shape=jax.ShapeDtypeStruct((D, P_END), BF16))


def _mm_square(name, a, w, l, res, tb, dep=None):
    tn = 1024
    b_spec = (pl.BlockSpec((None, tn, D), lambda j, i: (l, j, 0)) if tb else pl.BlockSpec((None, D, tn), lambda j, i: (l, 0, j)))
    return _matmul(
        name, a, w, grid=(D // tn, T // _TM), tb=tb, res=res, dep=dep,
        a_spec=pl.BlockSpec((_TM, D), lambda j, i: (i, 0)), b_spec=b_spec,
        o_spec=pl.BlockSpec((_TM, tn), lambda j, i: (i, j)), out_shape=jax.ShapeDtypeStruct((T, D), F32))


def mm_out(mix, wg_out, l, x):
    return _mm_square("mm_out", mix, wg_out, l, x, False)


def mm_dmix(dx1, wg_out, l, dep=None):
    return _mm_square("mm_dmix", dx1, wg_out, l, None, True, dep)


def mm_dwout(mix, dx1):
    tn = 1024
    return _matmul(
        "mm_dwout", mix, dx1, grid=(D // tn, D // _TM), ta=True,
        a_spec=pl.BlockSpec((T, _TM), lambda j, i: (0, i)), b_spec=pl.BlockSpec((T, tn), lambda j, i: (0, j)),
        o_spec=pl.BlockSpec((_TM, tn), lambda j, i: (i, j)), out_shape=jax.ShapeDtypeStruct((D, D), BF16))


_GU_TN = GU_SHARD // 2


_GU_NJ = FFN // _GU_TN


def mm_dh2(dgu, wg_gu, l):
    return _matmul(
        "mm_dh2", dgu, wg_gu, grid=(T // _TM, 2 * N_CHIPS), tb=True, k_axis=1,
        a_spec=pl.BlockSpec((None, _TM, _GU_TN), lambda i, k: (k // _GU_NJ, i, k % _GU_NJ)),
        b_spec=pl.BlockSpec((None, None, D, _GU_TN), lambda i, k: (l, k // 2, 0, k % 2)),
        o_spec=pl.BlockSpec((_TM, D), lambda i, k: (i, 0)), out_shape=jax.ShapeDtypeStruct((T, D), F32))


def mm_dwgu(h2, dgu):
    return _matmul(
        "mm_dwgu", h2, dgu, grid=(N_CHIPS, 2, D // _TM), ta=True,
        a_spec=pl.BlockSpec((T, _TM), lambda s, j, i: (0, i)),
        b_spec=pl.BlockSpec((None, T, _GU_TN), lambda s, j, i: ((2 * s + j) // _GU_NJ, 0, (2 * s + j) % _GU_NJ)),
        o_spec=pl.BlockSpec((None, _TM, _GU_TN), lambda s, j, i: (s, i, j)),
        out_shape=jax.ShapeDtypeStruct((N_CHIPS, D, GU_SHARD), BF16))


def mm_down(act, wg_down, l, x1):
    tn = 512
    return _matmul(
        "mm_down", act, wg_down, grid=(D // tn, T // _TM), res=x1,
        a_spec=pl.BlockSpec((_TM, FFN), lambda j, i: (i, 0)),
        b_spec=pl.BlockSpec((None, FFN, tn), lambda j, i: (l, 0, j)),
        o_spec=pl.BlockSpec((_TM, tn), lambda j, i: (i, j)), out_shape=jax.ShapeDtypeStruct((T, D), F32))


def mm_dwdown(act, dx2):
    tm, tn = DOWN_SHARD, 512
    return _matmul(
        "mm_dwdown", act, dx2, grid=(D // tn, FFN // tm), ta=True,
        a_spec=pl.BlockSpec((T, tm), lambda j, i: (0, i)), b_spec=pl.BlockSpec((T, tn), lambda j, i: (0, j)),
        o_spec=pl.BlockSpec((tm, tn), lambda j, i: (i, j)), out_shape=jax.ShapeDtypeStruct((FFN, D), BF16))


_FF_TN = 1408


def _swiglu_fn(gt, up):
    return _silu(gt) * up


def _gate_up_specs():
    gate = pl.BlockSpec((None, None, D, _FF_TN), lambda j, i: (0, j // 2, 0, j % 2))
    up = pl.BlockSpec((None, None, D, _FF_TN), lambda j, i: (0, N_CHIPS // 2 + j // 2, 0, j % 2))
    both = pl.BlockSpec((2, _TM_FFN, _FF_TN), lambda j, i: (0, i, j))
    return gate, up, both


def mm_gu_swiglu(h2, wg_gu):
    gate, up, both = _gate_up_specs()

    def body(h_ref, wg_ref, wu_ref, gu_ref, act_ref):
        h = h_ref[...]
        gt = jnp.dot(h, wg_ref[...], preferred_element_type=F32)
        u = jnp.dot(h, wu_ref[...], preferred_element_type=F32)
        gu_ref[0] = gt
        gu_ref[1] = u
        act_ref[...] = _swiglu_fn(gt, u).astype(BF16)

    return pl.pallas_call(
        body, name="mm_gu_swiglu", grid=(FFN // _FF_TN, T // _TM_FFN),
        in_specs=[pl.BlockSpec((_TM_FFN, D), lambda j, i: (i, 0)), gate, up],
        out_specs=[both, pl.BlockSpec((_TM_FFN, _FF_TN), lambda j, i: (i, j))],
        out_shape=[jax.ShapeDtypeStruct((2, T, FFN), F32), jax.ShapeDtypeStruct((T, FFN), BF16)],
        compiler_params=_cparams(("arbitrary", "arbitrary")),
    )(h2, wg_gu, wg_gu)


def mm_dact_swiglu(dx2, wg_down, gu, dep=None):
    _, _, both = _gate_up_specs()
    deps, dspecs = _dep_specs(dep, 2)

    def body(dx_ref, w_ref, gu_ref, *rest):
        dact = lax.dot_general(dx_ref[...].astype(BF16), w_ref[...], _dims(2, False, True), preferred_element_type=F32)
        _, vjp = jax.vjp(_swiglu_fn, gu_ref[0], gu_ref[1])
        dgt, dup = vjp(dact)
        rest[-1][0] = dgt.astype(BF16)
        rest[-1][1] = dup.astype(BF16)

    return pl.pallas_call(
        body, name="mm_dact_swiglu", grid=(FFN // _FF_TN, T // _TM_FFN),
        in_specs=[pl.BlockSpec((_TM_FFN, D), lambda j, i: (i, 0)), pl.BlockSpec((None, _FF_TN, D), lambda j, i: (0, j, 0)),
                  both]
        + dspecs,
        out_specs=both, out_shape=jax.ShapeDtypeStruct((2, T, FFN), BF16),
        compiler_params=_cparams(("arbitrary", "arbitrary")),
    )(dx2, wg_down, gu, *deps)


def loss_and_grad(y, target):
    blk = pl.BlockSpec((_ROWS, D), lambda i: (i, 0))
    acc = pl.BlockSpec((8, HD), lambda i: (0, 0))

    def body(y_ref, t_ref, dy_ref, l_ref):
        err = y_ref[...] - t_ref[...]
        dy_ref[...] = err * (1.0 / D)

        @pl.when(pl.program_id(0) == 0)
        def _():
            l_ref[...] = jnp.zeros_like(l_ref)

        l_ref[...] += (0.5 / D) * jnp.sum(err * err)

    return pl.pallas_call(
        body, name="loss_and_grad", grid=(T // _ROWS,), in_specs=[blk, blk], out_specs=[blk, acc],
        out_shape=[jax.ShapeDtypeStruct((T, D), F32), jax.ShapeDtypeStruct((8, HD), F32)],
        compiler_params=_cparams(("arbitrary",)),
    )(y, target)


def adamw(w, g, m, v, name):
    rows, cols = w.shape
    tr = _ROWS if rows % _ROWS == 0 else rows
    blk = pl.BlockSpec((tr, cols), lambda i: (i, 0))

    def body(w_ref, g_ref, m_ref, v_ref, d_ref, nm_ref, nv_ref):
        gg = g_ref[...]
        nm = ADAM_B1 * m_ref[...] + (1.0 - ADAM_B1) * gg
        nv = ADAM_B2 * v_ref[...] + (1.0 - ADAM_B2) * (gg * gg)
        m_hat = nm / (1.0 - ADAM_B1 ** ADAM_STEP)
        v_hat = nv / (1.0 - ADAM_B2 ** ADAM_STEP)
        d_ref[...] = -ADAM_LR * (m_hat / (jnp.sqrt(v_hat) + ADAM_EPS) + ADAM_WD * w_ref[...])
        nm_ref[...] = nm
        nv_ref[...] = nv

    return pl.pallas_call(
        body, name=name, grid=(rows // tr,), in_specs=[blk] * 4, out_specs=[blk] * 3,
        out_shape=[jax.ShapeDtypeStruct(w.shape, F32)] * 3, compiler_params=_cparams(("arbitrary",)),
    )(w, g, m, v)


_LANE = 128


def _segment_of_shard_column():
    flat = np.full(P_END, -1, np.int64)
    for o in range(P_END):
        if GATE_COLS <= o < P_CQ:
            continue
        c = o if o < GATE_COLS else o - (P_CQ - GATE_COLS)
        flat[o] = (c // IN_SHARD) * IN_SHARD_PAD + c % IN_SHARD
    return flat


def _block_pairs(src_of_dst):
    return [sorted({int(c) // _LANE for c in src_of_dst[db * _LANE:(db + 1) * _LANE] if c >= 0})
            for db in range(len(src_of_dst) // _LANE)]


_RELAYOUT_ROWS = 512
_SHARD_BLOCKS = IN_SHARD_PAD // _LANE


def _relayout(name, x, to_segments):
    seg_of = _segment_of_shard_column()
    if to_segments:
        src_of_dst = seg_of
    else:
        src_of_dst = np.full(N_CHIPS * IN_SHARD_PAD, -1, np.int64)
        src_of_dst[seg_of[seg_of >= 0]] = np.nonzero(seg_of >= 0)[0]
    sources = _block_pairs(src_of_dst)
    n_dst = len(sources)
    col_map = jnp.asarray(src_of_dst.reshape(n_dst, 1, _LANE), jnp.int32)
    shard_blk = pl.BlockSpec((N_CHIPS, _RELAYOUT_ROWS, IN_SHARD_PAD), lambda i: (0, i, 0))
    seg_blk = pl.BlockSpec((_RELAYOUT_ROWS, P_END), lambda i: (i, 0))

    def shard_cols(ref, b):
        return ref.at[b // _SHARD_BLOCKS, :, pl.ds((b % _SHARD_BLOCKS) * _LANE, _LANE)]

    def seg_cols(ref, b):
        return ref.at[:, pl.ds(b * _LANE, _LANE)]

    src_cols, dst_cols = (shard_cols, seg_cols) if to_segments else (seg_cols, shard_cols)

    def body(x_ref, map_ref, o_ref):
        src_row = _iota((_LANE, _LANE), 0)
        for d in range(n_dst):
            acc = jnp.zeros((_RELAYOUT_ROWS, _LANE), F32)
            for sb in sources[d]:
                sel = (src_row + sb * _LANE == map_ref[d]).astype(x_ref.dtype)
                acc = acc + jnp.dot(src_cols(x_ref, sb)[...], sel, preferred_element_type=F32)
            dst_cols(o_ref, d)[...] = acc.astype(o_ref.dtype)

    rows = x.shape[-2]
    out_shape = (rows, P_END) if to_segments else (N_CHIPS, rows, IN_SHARD_PAD)
    return pl.pallas_call(
        body, name=name, grid=(rows // _RELAYOUT_ROWS,),
        in_specs=[shard_blk if to_segments else seg_blk, pl.BlockSpec(col_map.shape, lambda i: (0, 0, 0))],
        out_specs=seg_blk if to_segments else shard_blk, out_shape=jax.ShapeDtypeStruct(out_shape, x.dtype),
        compiler_params=_cparams(("arbitrary",)),
    )(x, col_map)


def shards_to_segments(w):
    return _relayout("shards_to_segments", w, True)


def segments_to_shards(w):
    return _relayout("segments_to_shards", w, False)


def mixers_forward(x, w_in, sp, dep=None, h1=None):
    if h1 is None:
        h1 = rmsnorm_fwd(x, sp["norm1_g"], dep)
    p2 = mm_proj(h1, w_in, 0)
    y_a = sgu_fwd(p2, sp["sgu_norm_g"], sp["w_spatial"], sp["b_spatial"])
    y_b, saved_b = gdn_forward(p2, sp["conv_w"], sp["a_log"], sp["dt_bias"], sp["o_norm_g"])
    y_c, outs_c, lses_c = dattn_fwd(p2, sp["q_norm_g"], sp["k_norm_g"], _alibi_slopes())
    mix = jnp.concatenate([y_a, y_b, y_c], axis=1)
    return mix, (x, h1, p2, saved_b, (outs_c, lses_c), mix)


def ffn_up(x, mix, w_out, w_gu, sp):
    x1 = mm_out(mix, w_out, 0, x)
    h2 = rmsnorm_fwd(x1, sp["norm2_g"])
    gu, act = mm_gu_swiglu(h2, w_gu)
    return x1, h2, gu, act


def ffn_forward(x, mix, wg, sp):
    x1, h2, gu, act = ffn_up(x, mix, wg["out"], wg["gu"], sp)
    x2 = mm_down(act, wg["down"], 0, x1)
    return x2, (x1, h2, gu, act)


def ffn_backward(dx2, wg, sp, saved, dep=None):
    x1, h2, gu, act = saved
    dgu = mm_dact_swiglu(dx2, wg["down"], gu, dep)
    dw_down = mm_dwdown(act, dx2)
    dw_gu = mm_dwgu(h2, dgu)
    dh2 = mm_dh2(dgu, wg["gu"], 0)
    dx1, dnorm2 = rmsnorm_bwd(x1, sp["norm2_g"], dh2, dx2)
    return dx1, dnorm2, dw_gu, dw_down


def mixers_backward(dx1, wg, sp, saved, dep=None):
    x, h1, p2, saved_b, saved_c, mix = saved
    dmix = mm_dmix(dx1, wg["out"], 0, dep)
    dw_out = mm_dwout(mix, dx1)
    du, dv, dsg, dws, dbs = sgu_bwd(p2, sp["sgu_norm_g"], sp["w_spatial"], sp["b_spatial"], dmix)
    dseg_b, dconv, dal, ddtb, dog = gdn_backward(p2, sp["conv_w"], sp["a_log"], sp["dt_bias"], sp["o_norm_g"], saved_b, dmix)
    dcq, dck, dcv, dqg, dkg = dattn_bwd(p2, sp["q_norm_g"], sp["k_norm_g"], _alibi_slopes(), *saved_c, dmix)
    dp2 = jnp.concatenate([du, dv] + dseg_b + [dcq, dck, dcv], axis=1)
    dw_in = segments_to_shards(mm_dwin(h1, dp2))
    dh1 = mm_dh1(dp2, wg["in"], 0)
    dx, dnorm1 = rmsnorm_bwd(x, sp["norm1_g"], dh1, dx1)
    small = {"norm1_g": dnorm1, "sgu_norm_g": dsg, "w_spatial": dws, "b_spatial": dbs, "conv_w": dconv, "a_log": dal,
             "dt_bias": ddtb, "o_norm_g": dog, "q_norm_g": dqg, "k_norm_g": dkg}
    return dx, dw_in, dw_out, small


def layer_forward(x, wg, sp, dep=None):
    mix, saved_m = mixers_forward(x, wg["in"], sp, dep)
    x2, saved_f = ffn_forward(x, mix, wg, sp)
    return x2, (saved_m, saved_f)


def layer_backward(dx2, wg, sp, saved, dep=None):
    dx1, dnorm2, dw_gu, dw_down = ffn_backward(dx2, wg, sp, saved[1], dep)
    dx, dw_in, dw_out, small = mixers_backward(dx1, wg, sp, saved[0])
    return dx, {"in": dw_in, "out": dw_out, "gu": dw_gu, "down": dw_down}, {**small, "norm2_g": dnorm2}


_HBM = pl.BlockSpec(memory_space=pltpu.HBM)
_MESH = pl.DeviceIdType.MESH


def _place():
    x, y, c = lax.axis_index("x"), lax.axis_index("y"), lax.axis_index("c")
    chips = [(1 - x, y), (x, 1 - y), (1 - x, 1 - y)]
    return x, y, c, chips


def _rcopy(src, dst, ssem, rsem, dev):
    return pltpu.make_async_remote_copy(src_ref=src, dst_ref=dst, send_sem=ssem, recv_sem=rsem, device_id=dev,
                                        device_id_type=_MESH)


_SEM = pl.BlockSpec(memory_space=pltpu.SEMAPHORE)
_SIDE_EFFECT = pltpu.SideEffectType.DATAFLOW_SIDE_EFFECTING


def _in_hbm(a):
    return pltpu.with_memory_space_constraint(a, pltpu.HBM)


def _split_copy(name, srcs, land_shapes, n_sems, copies):
    n, m = len(srcs), len(land_shapes)
    thru = [pltpu.HBM(a.shape, a.dtype) for a in srcs] + [pltpu.HBM(s.shape, s.dtype) for s in land_shapes]
    sems = (pltpu.SemaphoreType.DMA((n_sems,)), pltpu.SemaphoreType.DMA((n_sems,)))

    def start(dep=None):
        deps = [] if dep is None else [dep]

        def body(*refs):
            ins, lands = refs[:n], refs[n:n + m]
            ssem, rsem, token = refs[n + m + len(deps)], refs[n + m + len(deps) + 1], refs[-1]
            for cp in copies(ins, lands, ssem, rsem)[0]:
                cp.start()
            token[...] = jnp.zeros_like(token)

        out = pl.pallas_call(
            body, name=name + "_start", out_shape=(*sems, *thru, jax.ShapeDtypeStruct((8, HD), F32)),
            in_specs=[_HBM] * (n + m) + [pl.BlockSpec(memory_space=pl.ANY)] * len(deps),
            out_specs=(_SEM, _SEM, *[_HBM] * (n + m), pl.BlockSpec(memory_space=pltpu.VMEM)),
            input_output_aliases={i: 2 + i for i in range(n + m)},
            compiler_params=pltpu.CompilerParams(has_side_effects=_SIDE_EFFECT),
        )(*[_in_hbm(a) for a in srcs], *[_in_hbm(lax.empty(s.shape, s.dtype)) for s in land_shapes], *deps)
        return out[:-1], out[-1]

    def wait(state, after):
        def body(*refs):
            ins, lands, ssem, rsem = refs[:n], refs[n:n + m], refs[n + m], refs[n + m + 1]
            sent, arrivals = copies(ins, lands, ssem, rsem)
            for cp in sent:
                cp.wait_send()
            for cp in arrivals:
                cp.wait_recv()

        out = pl.pallas_call(
            body, name=name + "_wait", out_shape=tuple(thru),
            in_specs=[_HBM] * (n + m) + [_SEM, _SEM, pl.BlockSpec(memory_space=pl.ANY)], out_specs=[_HBM] * (n + m),
            input_output_aliases={i: i for i in range(n + m)},
            compiler_params=pltpu.CompilerParams(has_side_effects=_SIDE_EFFECT),
        )(*state[2:], state[0], state[1], after)
        return list(out[:n]), list(out[n:])

    return start, wait


def gather_direct(shards, tag):
    n = len(shards)

    def copies(ins, lands, ssem, rsem):
        x, y, c, chips = _place()
        s = 2 * x + y
        sibling = (x, y, 1 - c)
        sent, arrivals = [], []
        for a in range(n):
            for u in range(2):
                cp = _rcopy(ins[a].at[u], lands[a].at[s, u], ssem.at[5 * a + u], rsem.at[5 * a + u], sibling)
                sent.append(cp)
                arrivals.append(cp)
            for j, (cx, cy) in enumerate(chips):
                k = 5 * a + 2 + j
                sent.append(_rcopy(ins[a].at[c], lands[a].at[s, c], ssem.at[k], rsem.at[k], (cx, cy, c)))
                arrivals.append(_rcopy(ins[a].at[c], lands[a].at[2 * cx + cy, c], ssem.at[k], rsem.at[k], (cx, cy, c)))
        return sent, arrivals

    lands = [jax.ShapeDtypeStruct((N_CHIPS,) + w.shape, w.dtype) for w in shards]
    return _split_copy("gather_direct_" + tag, shards, lands, 5 * n, copies)


def pass_to_sibling(lands):
    n = len(lands)

    def body(*refs):
        ins = refs[:n]
        ssem, rsem = refs[2 * n:]
        x, y, c, chips = _place()
        sibling = (x, y, 1 - c)
        cps, arrivals = [], []
        for a in range(n):
            for j, (cx, cy) in enumerate(chips):
                t = 2 * cx + cy
                cps.append(_rcopy(ins[a].at[t, c], ins[a].at[t, c], ssem.at[a, j], rsem.at[a, j], sibling))
                arrivals.append(_rcopy(ins[a].at[t, c], ins[a].at[t, 1 - c], ssem.at[a, j], rsem.at[a, j], sibling))
        for cp in cps:
            cp.start()
        for cp, ar in zip(cps, arrivals):
            cp.wait_send()
            ar.wait_recv()

    return pl.pallas_call(
        body, name="pass_to_sibling", in_specs=[_HBM] * n, out_specs=[_HBM] * n,
        out_shape=[jax.ShapeDtypeStruct(a.shape, a.dtype) for a in lands], input_output_aliases={a: a for a in range(n)},
        scratch_shapes=[pltpu.SemaphoreType.DMA((n, 3)), pltpu.SemaphoreType.DMA((n, 3))],
    )(*lands)


def exchange_halves(grads):
    n = len(grads)

    def body(*refs):
        ins, outs = refs[:n], refs[n:2 * n]
        ssem, rsem = refs[2 * n:]
        x, y, c, _ = _place()
        cps = []
        for a in range(n):
            h = grads[a].shape[1] // 2
            cps.append(_rcopy(ins[a].at[:, pl.ds((1 - c) * h, h)], outs[a], ssem.at[a], rsem.at[a], (x, y, 1 - c)))
        for cp in cps:
            cp.start()
        for cp in cps:
            cp.wait()

    return pl.pallas_call(
        body, name="exchange_halves", in_specs=[_HBM] * n, out_specs=[_HBM] * n,
        out_shape=[jax.ShapeDtypeStruct((g.shape[0], g.shape[1] // 2, g.shape[2]), g.dtype) for g in grads],
        scratch_shapes=[pltpu.SemaphoreType.DMA((n,)), pltpu.SemaphoreType.DMA((n,))],
    )(*grads)


def scatter_to_chips(parts):
    n = len(parts)

    def body(*refs):
        ins, outs = refs[:n], refs[n:2 * n]
        ssem, rsem = refs[2 * n:]
        x, y, c, chips = _place()
        cps = [_rcopy(ins[a].at[2 * cx + cy], outs[a].at[j], ssem.at[a, j], rsem.at[a, j], (cx, cy, c))
               for a in range(n) for j, (cx, cy) in enumerate(chips)]
        for cp in cps:
            cp.start()
        for cp in cps:
            cp.wait()

    return pl.pallas_call(
        body, name="scatter_to_chips", in_specs=[_HBM] * n, out_specs=[_HBM] * n,
        out_shape=[jax.ShapeDtypeStruct((3,) + p.shape[1:], p.dtype) for p in parts],
        scratch_shapes=[pltpu.SemaphoreType.DMA((n, 3)), pltpu.SemaphoreType.DMA((n, 3))],
    )(*parts)


def scatter_direct(parts, tag):
    n = len(parts)

    def copies(ins, lands, ssem, rsem):
        x, y, c, chips = _place()
        cps = [_rcopy(ins[a].at[2 * cx + cy], lands[a].at[j], ssem.at[3 * a + j], rsem.at[3 * a + j], (cx, cy, c))
               for a in range(n) for j, (cx, cy) in enumerate(chips)]
        return cps, cps

    lands = [jax.ShapeDtypeStruct((3,) + p.shape[1:], p.dtype) for p in parts]
    return _split_copy("scatter_direct_" + tag, parts, lands, 3 * n, copies)


def share_halves(halves):
    n = len(halves)

    def body(*refs):
        ins, outs = refs[:n], refs[n:2 * n]
        ssem, rsem = refs[2 * n:]
        x, y, c, _ = _place()
        cps = [_rcopy(ins[i], outs[i], ssem.at[i], rsem.at[i], (x, y, 1 - c)) for i in range(n)]
        for cp in cps:
            cp.start()
        for cp in cps:
            cp.wait()

    return pl.pallas_call(
        body, name="share_halves", in_specs=[_HBM] * n, out_specs=[_HBM] * n,
        out_shape=[jax.ShapeDtypeStruct(h.shape, h.dtype) for h in halves],
        scratch_shapes=[pltpu.SemaphoreType.DMA((n,)), pltpu.SemaphoreType.DMA((n,))],
    )(*halves)


def adamw_shard(w, m, v, mine, theirs, c, name):
    _, r, cw = w.shape
    h, cg = mine[0].shape
    tr = next(t for t in (256, 176, 128) if h % t == 0 and t * cg * 4 <= (3 << 19))
    nb = h // tr
    wblk = pl.BlockSpec((None, tr, cw), lambda l, i, c_ref: (l, i, 0))
    gblk = lambda layer, own: pl.BlockSpec((tr, cg), lambda l, i, c_ref: (_held_block(l, i, c_ref, layer, own, nb), 0))
    return _adamw_halves(w, m, v, mine, theirs, c, name, (DEPTH, r // tr), wblk, gblk, nb, cw)


def _held_block(l, i, c_ref, layer, own, nb):
    in_use = (l == layer) & (((i // nb) == c_ref[0]) == own)
    return jnp.where(in_use, i % nb, 0)


def _adamw_halves(w, m, v, mine, theirs, c, name, grid, wblk, gblk, nb, cw):
    def body(c_ref, w_ref, m_ref, v_ref, m0, m1, t0, t1, g_ref, d_ref, nm_ref, nv_ref):
        is_mine = (pl.program_id(1) // nb) == c_ref[0]
        first = pl.program_id(0) == 0
        gg = jnp.where(is_mine, jnp.where(first, m0[:, :cw], m1[:, :cw]), jnp.where(first, t0[:, :cw], t1[:, :cw]))
        nm = ADAM_B1 * m_ref[...] + (1.0 - ADAM_B1) * gg
        nv = ADAM_B2 * v_ref[...] + (1.0 - ADAM_B2) * (gg * gg)
        m_hat = nm / (1.0 - ADAM_B1 ** ADAM_STEP)
        v_hat = nv / (1.0 - ADAM_B2 ** ADAM_STEP)
        g_ref[...] = gg
        d_ref[...] = -ADAM_LR * (m_hat / (jnp.sqrt(v_hat) + ADAM_EPS) + ADAM_WD * w_ref[...])
        nm_ref[...] = nm
        nv_ref[...] = nv

    return pl.pallas_call(
        body, name=name,
        grid_spec=pltpu.PrefetchScalarGridSpec(
            num_scalar_prefetch=1, grid=grid,
            in_specs=[wblk] * 3 + [gblk(0, True), gblk(1, True), gblk(0, False), gblk(1, False)], out_specs=[wblk] * 4),
        out_shape=[jax.ShapeDtypeStruct(w.shape, F32)] * 4, compiler_params=_cparams(("arbitrary", "arbitrary")),
    )(c, w, m, v, mine[0], mine[1], theirs[0], theirs[1])


def adamw_shard_t(wt, mt, vt, mine_t, theirs_t, c, name):
    _, cw, r = wt.shape
    h = mine_t[0].shape[1]
    tc = 256
    nb = h // tc
    wblk = pl.BlockSpec((None, cw, tc), lambda l, j, c_ref: (l, 0, j))
    gblk = lambda layer, own: pl.BlockSpec((cw, tc), lambda l, j, c_ref: (0, _held_block(l, j, c_ref, layer, own, nb)))
    return _adamw_halves(wt, mt, vt, mine_t, theirs_t, c, name, (DEPTH, r // tc), wblk, gblk, nb, cw)


def _half_rows(h, cols):
    for tr in (512, 256, 352, 128, 64):
        if h % tr == 0 and tr * cols * 4 <= 6 * 1024 * 1024:
            return tr
    raise ValueError((h, cols))


def add_sibling(grad, recv, c):
    _, r, cols = grad.shape
    h = r // 2
    tr = _half_rows(h, cols)
    nb = h // tr

    def body(c_ref, g_ref, r_ref, o_ref):
        o_ref[...] = (g_ref[...].astype(F32) + r_ref[...].astype(F32)).astype(BF16)

    return pl.pallas_call(
        body, name="add_sibling",
        grid_spec=pltpu.PrefetchScalarGridSpec(
            num_scalar_prefetch=1, grid=(N_CHIPS, nb),
            in_specs=[pl.BlockSpec((None, tr, cols), lambda t, i, c_ref: (t, c_ref[0] * nb + i, 0)),
                      pl.BlockSpec((None, tr, cols), lambda t, i, c_ref: (t, i, 0))],
            out_specs=pl.BlockSpec((None, tr, cols), lambda t, i, c_ref: (t, i, 0))),
        out_shape=jax.ShapeDtypeStruct((N_CHIPS, h, cols), BF16), compiler_params=_cparams(("arbitrary", "arbitrary")),
    )(c, grad, recv)


def add_chips(part, recv, s):
    _, h, cols = part.shape
    tr = _half_rows(h, cols)

    def body(s_ref, p_ref, r_ref, o_ref):
        o_ref[...] = ((p_ref[...].astype(F32) + r_ref[0].astype(F32)) + r_ref[1].astype(F32)) + r_ref[2].astype(F32)

    return pl.pallas_call(
        body, name="add_chips",
        grid_spec=pltpu.PrefetchScalarGridSpec(
            num_scalar_prefetch=1, grid=(h // tr,),
            in_specs=[pl.BlockSpec((None, tr, cols), lambda i, s_ref: (s_ref[0], i, 0)),
                      pl.BlockSpec((3, tr, cols), lambda i, s_ref: (0, i, 0))],
            out_specs=pl.BlockSpec((tr, cols), lambda i, s_ref: (i, 0))),
        out_shape=jax.ShapeDtypeStruct((h, cols), F32), compiler_params=_cparams(("arbitrary",)),
    )(s, part, recv)


def allreduce_small(vec):
    rows = vec.shape[0]

    def body(v_ref, o_ref, buf, ssem, rsem, lsem):
        x, y, c, chips = _place()
        me, sibling = (x, y, c), (x, y, 1 - c)

        def blk(px, py, pc):
            return buf.at[4 * px + 2 * py + pc]

        def copy(k, block, to, src=None):
            return _rcopy(blk(*block) if src is None else src, blk(*block), ssem.at[k], rsem.at[k], to)

        mine = pltpu.make_async_copy(v_ref, blk(*me), lsem)
        mine.start()
        first = [copy(0, me, sibling, src=v_ref)] + [copy(1 + j, me, (*chip, c), src=v_ref) for j, chip in enumerate(chips)]
        for cp in first:
            cp.start()
        passed = [copy(4 + j, (*chip, c), sibling) for j, chip in enumerate(chips)]
        for j, chip in enumerate(chips):
            copy(1 + j, (*chip, c), me).wait_recv()
            passed[j].start()
        copy(0, sibling, me).wait_recv()
        for j, chip in enumerate(chips):
            copy(4 + j, (*chip, 1 - c), me).wait_recv()
        for cp in first + passed:
            cp.wait_send()
        mine.wait()
        acc = buf[0]
        for d in range(1, N_DEV):
            acc = acc + buf[d]
        o_ref[...] = acc

    vm = pl.BlockSpec(memory_space=pltpu.VMEM)
    return pl.pallas_call(
        body, name="allreduce_small", in_specs=[vm], out_specs=vm, out_shape=jax.ShapeDtypeStruct(vec.shape, F32),
        scratch_shapes=[pltpu.VMEM((N_DEV, rows, HD), F32), pltpu.SemaphoreType.DMA((7,)), pltpu.SemaphoreType.DMA((7,)),
                        pltpu.SemaphoreType.DMA],
        compiler_params=pltpu.CompilerParams(vmem_limit_bytes=VMEM_LIMIT),
    )(vec)


SMALL_NAMES = ("norm1_g", "sgu_norm_g", "w_spatial", "b_spatial", "conv_w", "a_log", "dt_bias", "o_norm_g", "q_norm_g",
               "k_norm_g", "norm2_g")


def small_params(l, p, conv_full):
    return {"norm1_g": p["norm1_g"][l][None], "sgu_norm_g": p["sgu_norm_g"][l][:, None, :], "w_spatial": p["w_spatial"][l],
            "b_spatial": p["b_spatial"][l][..., None], "conv_w": conv_full[l], "a_log": p["a_log"][l], "dt_bias": p["dt_bias"][l],
            "o_norm_g": p["o_norm_g"][l][None], "q_norm_g": p["q_norm_g"][l][None], "k_norm_g": p["k_norm_g"][l][None],
            "norm2_g": p["norm2_g"][l][None]}


def local_step(x, target, wg, sps):
    saved = []
    for l in range(DEPTH):
        x, s = layer_forward(x, wg[l], sps[l])
        saved.append(s)
    dx, loss = loss_and_grad(x, target)
    bigs, smalls = [None] * DEPTH, [None] * DEPTH
    for l in reversed(range(DEPTH)):
        dx, bigs[l], smalls[l] = layer_backward(dx, wg[l], sps[l], saved[l])
    return loss, dx, bigs, smalls


_PACK_TILE = 8 * HD


def _pack(arrays):
    flat = jnp.concatenate([a.reshape(-1) for a in arrays])
    pad = -flat.shape[0] % _PACK_TILE
    return jnp.pad(flat, (0, pad)).reshape(-1, HD)


def _unpack(packed, shapes):
    flat, out, off = packed.reshape(-1), [], 0
    for shp in shapes:
        n = int(np.prod(shp))
        out.append(flat[off:off + n].reshape(shp))
        off += n
    return out


BIG_NAMES = ("in", "out", "gu", "down")
WEIGHT_ORDER = ("norm1_g", "w_in", "sgu_norm_g", "w_spatial", "b_spatial", "conv_w", "a_log", "dt_bias", "o_norm_g", "q_norm_g",
                "k_norm_g", "w_out", "norm2_g", "w_gate_up", "w_down")


def kernel(x, norm1_g, w_in, sgu_norm_g, w_spatial, b_spatial, conv_w, a_log, dt_bias, o_norm_g, q_norm_g, k_norm_g, w_out, norm2_g, w_gate_up, w_down, loss_target, m_norm1_g, m_w_in, m_sgu_norm_g, m_w_spatial, m_b_spatial, m_conv_w, m_a_log, m_dt_bias, m_o_norm_g, m_q_norm_g, m_k_norm_g, m_w_out, m_norm2_g, m_w_gate_up, m_w_down, v_norm1_g, v_w_in, v_sgu_norm_g, v_w_spatial, v_b_spatial, v_conv_w, v_a_log, v_dt_bias, v_o_norm_g, v_q_norm_g, v_k_norm_g, v_w_out, v_norm2_g, v_w_gate_up, v_w_down):
    w = dict(norm1_g=norm1_g, w_in=w_in, sgu_norm_g=sgu_norm_g, w_spatial=w_spatial, b_spatial=b_spatial, conv_w=conv_w,
             a_log=a_log, dt_bias=dt_bias, o_norm_g=o_norm_g, q_norm_g=q_norm_g, k_norm_g=k_norm_g, w_out=w_out,
             norm2_g=norm2_g, w_gate_up=w_gate_up, w_down=w_down)
    m = dict(norm1_g=m_norm1_g, w_in=m_w_in, sgu_norm_g=m_sgu_norm_g, w_spatial=m_w_spatial, b_spatial=m_b_spatial,
             conv_w=m_conv_w, a_log=m_a_log, dt_bias=m_dt_bias, o_norm_g=m_o_norm_g, q_norm_g=m_q_norm_g, k_norm_g=m_k_norm_g,
             w_out=m_w_out, norm2_g=m_norm2_g, w_gate_up=m_w_gate_up, w_down=m_w_down)
    v = dict(norm1_g=v_norm1_g, w_in=v_w_in, sgu_norm_g=v_sgu_norm_g, w_spatial=v_w_spatial, b_spatial=v_b_spatial,
             conv_w=v_conv_w, a_log=v_a_log, dt_bias=v_dt_bias, o_norm_g=v_o_norm_g, q_norm_g=v_q_norm_g, k_norm_g=v_k_norm_g,
             w_out=v_w_out, norm2_g=v_norm2_g, w_gate_up=v_w_gate_up, w_down=v_w_down)
    chip = (2 * lax.axis_index("x") + lax.axis_index("y")).astype(jnp.int32)
    core = lax.axis_index("c").astype(jnp.int32)

    in_pad = IN_SHARD_PAD - IN_SHARD
    w_in_pad = jnp.pad(w_in, ((0, 0), (0, 0), (0, in_pad)))

    halves_of = lambda a: a.reshape(2, a.shape[0] // 2, a.shape[1])
    start_0, wait_0 = gather_direct([halves_of(w_in_pad[0].astype(BF16)), halves_of(conv_w[0])], "mix0")
    state_0, token_0 = start_0()
    bf_halves = lambda a: halves_of((a + token_0[0, 0]).astype(BF16))

    def ffn_shards(l):
        return [bf_halves(w_gate_up[l]), bf_halves(w_down[l]), bf_halves(w_out[l])]

    def mixer_shards(l):
        return [bf_halves(w_in_pad[l]), halves_of(conv_w[l])]

    def mixer_weights(g):
        g_in, g_conv = g
        return (shards_to_segments(g_in.reshape(N_CHIPS, D, IN_SHARD_PAD))[None],
                g_conv.reshape(N_CHIPS, B_CONV, -1).transpose(1, 0, 2).reshape(B_CONV, 3 * B_WIDTH))

    def ffn_weights(g, w_in_seg):
        g_gu, g_down, g_out = g
        return {"in": w_in_seg, "out": g_out.reshape(1, D, D), "gu": g_gu.reshape(1, N_CHIPS, D, GU_SHARD),
                "down": None if g_down is None else g_down.reshape(1, FFN, D)}

    def layer_params(l, conv_full):
        return small_params(0, {n: w[n][l:l + 1] for n in SMALL_NAMES if n != "conv_w"}, conv_full[None])

    gu0, down0, out0 = ffn_shards(0)
    start_a, wait_a = gather_direct([gu0, out0], "ffn0")
    start_b, wait_b = gather_direct([down0] + mixer_shards(1), "mid")
    start_c, wait_c = gather_direct(ffn_shards(1), "ffn1")
    state_a, token_a = start_a(token_0)
    state_b, token_b = start_b(token_a)
    state_c, token_c = start_c(token_b)
    h1_0 = rmsnorm_fwd(x[0], norm1_g[0][None], token_c)
    w_in0, conv0 = mixer_weights(pass_to_sibling(wait_0(state_0, h1_0)[1]))
    sps = [layer_params(0, conv0), None]
    mix0, saved_m0 = mixers_forward(x[0], w_in0, sps[0], h1=h1_0)
    g_gu0, g_out0 = pass_to_sibling(wait_a(state_a, mix0)[1])
    wg0 = ffn_weights((g_gu0, None, g_out0), w_in0)
    x1_0, h2_0, gu_0, act_0 = ffn_up(x[0], mix0, wg0["out"], wg0["gu"], sps[0])
    g_down0, g_in1, g_conv1 = pass_to_sibling(wait_b(state_b, act_0)[1])
    wg0["down"] = g_down0.reshape(1, FFN, D)
    x1 = mm_down(act_0, wg0["down"], 0, x1_0)
    saved_f0 = (x1_0, h2_0, gu_0, act_0)
    w_in1, conv1 = mixer_weights((g_in1, g_conv1))
    sps[1] = layer_params(1, conv1)
    mix1, saved_m1 = mixers_forward(x1, w_in1, sps[1])
    wg1 = ffn_weights(pass_to_sibling(wait_c(state_c, mix1)[1]), w_in1)
    x2, saved_f1 = ffn_forward(x1, mix1, wg1, sps[1])
    saved1 = (saved_m1, saved_f1)
    dx, loss_tile = loss_and_grad(x2, loss_target[0])

    def to_chip_parts(grads):
        return [add_sibling(g, r, core.reshape(1)) for g, r in zip(grads, exchange_halves(grads))]

    def start_scatter(grads, tag, dep=None):
        start, wait = scatter_direct(to_chip_parts(grads), tag)
        state, token = start(dep)
        return functools.partial(wait, state), token

    smalls = [None] * DEPTH
    by_chip_out = lambda t: t.reshape(N_CHIPS, OUT_SHARD, D)
    by_chip_down = lambda t: t.reshape(N_CHIPS, DOWN_SHARD, D)
    dx1, dnorm2_1, dw_gu1, dw_down1 = ffn_backward(dx, wg1, sps[1], saved1[1])
    wait_f1, tok_f1 = start_scatter([dw_gu1, by_chip_down(dw_down1)], "ffn1")
    dx, dw_in1, dw_out1, small1 = mixers_backward(dx1, wg1, sps[1], saved1[0], dep=tok_f1)
    smalls[1] = {**small1, "norm2_g": dnorm2_1}
    wait_m1, tok_m1 = start_scatter([dw_in1, by_chip_out(dw_out1)], "mix1")
    dx1, dnorm2_0, dw_gu0, dw_down0 = ffn_backward(dx, wg0, sps[0], saved_f0, dep=tok_m1)
    wait_f0, tok_f0 = start_scatter([dw_gu0, by_chip_down(dw_down0)], "ffn0")
    dx, dw_in0, dw_out0, small0 = mixers_backward(dx1, wg0, sps[0], saved_m0, dep=tok_f0)
    smalls[0] = {**small0, "norm2_g": dnorm2_0}

    grad, delta, new_m, new_v = {}, {}, {}, {}
    stacked = [jnp.stack([smalls[l][n] for l in range(DEPTH)]) for n in SMALL_NAMES]
    total = allreduce_small(_pack(stacked + [loss_tile[0, :1]]))
    shapes = [(DEPTH, B_CONV, 3 * B_WIDTH) if n == "conv_w" else w[n].shape for n in SMALL_NAMES]
    small_grads = dict(zip(SMALL_NAMES, _unpack(total, shapes + [(1,)])[:-1]))
    loss = _unpack(total, shapes + [(1,)])[-1][0]
    conv_cols = conv_w.shape[-1]
    small_grads["conv_w"] = lax.dynamic_slice_in_dim(small_grads["conv_w"], chip * conv_cols, conv_cols, axis=2)
    grad.update(small_grads)
    sshapes = [w[n].shape for n in SMALL_NAMES]
    packed = [_pack([d[n] for n in SMALL_NAMES]) for d in (w, grad, m, v)]
    for dst, t in zip((delta, new_m, new_v), adamw(*packed, "adamw_small")):
        dst.update(zip(SMALL_NAMES, _unpack(t, sshapes)))

    wait_m0, tok_m0 = start_scatter([dw_in0, by_chip_out(dw_out0)], "mix0", dep=total)
    (pf0, rf0), (pm1, rm1), (pf1, rf1) = (wt(tok_m0) for wt in (wait_f0, wait_m1, wait_f1))

    def reduce_group(parts, from_chips):
        mine = [add_chips(p, r, chip.reshape(1)) for p, r in zip(parts, from_chips)]
        return mine, list(share_halves(mine))

    mine_f, theirs_f = reduce_group(pf0 + pf1 + pm1, rf0 + rf1 + rm1)
    for a, n in enumerate(("w_gate_up", "w_down")):
        grad[n], delta[n], new_m[n], new_v[n] = adamw_shard(w[n], m[n], v[n], [mine_f[a], mine_f[2 + a]],
                                                            [theirs_f[a], theirs_f[2 + a]], core.reshape(1), "adamw_" + n)

    mine_m0, theirs_m0 = reduce_group(*wait_m0(new_v["w_down"]))
    tr_ = lambda t: jnp.swapaxes(t, -1, -2)
    cut = lambda t: tr_(t[:, :IN_SHARD])
    res = adamw_shard_t(tr_(w_in), tr_(m_w_in), tr_(v_w_in), [cut(mine_m0[0]), cut(mine_f[4])],
                        [cut(theirs_m0[0]), cut(theirs_f[4])], core.reshape(1), "adamw_w_in")
    grad["w_in"], delta["w_in"], new_m["w_in"], new_v["w_in"] = (tr_(t) for t in res)
    grad["w_out"], delta["w_out"], new_m["w_out"], new_v["w_out"] = adamw_shard(
        w_out, m_w_out, v_w_out, [mine_m0[1], mine_f[5]], [theirs_m0[1], theirs_f[5]], core.reshape(1), "adamw_w_out")

    out = [loss, dx[None]]
    for d in (grad, delta, new_m, new_v):
        out += [d[n] for n in WEIGHT_ORDER]
    return tuple(out)
```

```python
import functools
import math

import numpy as np
import jax
import jax.numpy as jnp
from jax import lax
from jax.experimental import pallas as pl
from jax.experimental.pallas import tpu as pltpu

F32 = jnp.float32
BF16 = jnp.bfloat16
HI = lax.Precision.HIGH

T = 2048
D = 2048
DEPTH = 2
HD = 128
A_GROUPS, A_WIDTH, A_CHUNK = 4, 512, 128
B_HEADS, B_WIDTH, B_CONV, B_CHUNK = 6, 768, 4, 64
C_HEADS, C_WIDTH, C_BLOCK = 6, 768, 128
C_BRANCHES = ((128, 1), (512, 4), (2048, 16))
FFN = 5632
IN_TOTAL = 6412
EPS = 1e-6
N_CHIPS = 4
N_DEV = 8
IN_SHARD = IN_TOTAL // N_CHIPS
IN_SHARD_PAD = 1664
GU_SHARD = 2 * FFN // N_CHIPS
OUT_SHARD = D // N_CHIPS
DOWN_SHARD = FFN // N_CHIPS
P_AU, P_AV, P_BQ, P_BK, P_BV, P_BG, P_BB, P_CQ, P_CK, P_CV, P_END = (
    0, 512, 1024, 1792, 2560, 3328, 4096, 4224, 4992, 5760, 6528)
GATE_COLS = 4108
VMEM_LIMIT = 56 * 1024 * 1024

ADAM_LR, ADAM_B1, ADAM_B2, ADAM_EPS, ADAM_WD, ADAM_STEP = 0.001, 0.9, 0.999, 1e-08, 0.01, 10


def _cparams(sem, vmem=VMEM_LIMIT):
    return pltpu.CompilerParams(dimension_semantics=sem, vmem_limit_bytes=vmem)


def _dims(nd, ta, tb):
    off = nd - 2
    ca = off + (0 if ta else 1)
    cb = off + (1 if tb else 0)
    batch = ((0,), (0,)) if nd == 3 else ((), ())
    return (((ca,), (cb,)), batch)


def _raw_mm(a, b, ta, tb, hi):
    if hi:
        return lax.dot_general(a, b, _dims(a.ndim, ta, tb), precision=HI, preferred_element_type=F32)
    return lax.dot_general(a.astype(BF16), b.astype(BF16), _dims(a.ndim, ta, tb), preferred_element_type=F32)


@functools.partial(jax.custom_vjp, nondiff_argnums=(2, 3, 4))
def _mm(a, b, ta=False, tb=False, hi=False):
    return _raw_mm(a, b, ta, tb, hi)


def _mm_fwd(a, b, ta, tb, hi):
    return _raw_mm(a, b, ta, tb, hi), (a, b)


def _mm_bwd(ta, tb, hi, res, g):
    a, b = res
    da = _raw_mm(g, b, False, not tb, False) if not ta else _raw_mm(b, g, tb, True, False)
    db = _raw_mm(a, g, not ta, False, False) if not tb else _raw_mm(g, a, True, ta, False)
    return da.astype(a.dtype), db.astype(b.dtype)


_mm.defvjp(_mm_fwd, _mm_bwd)


def _rms(x, g):
    return x * lax.rsqrt(jnp.mean(x * x, axis=-1, keepdims=True) + EPS) * g


def _gelu(x):
    return 0.5 * x * (1.0 + jnp.tanh(math.sqrt(2.0 / math.pi) * (x + 0.044715 * (x * x * x))))


def _sigmoid(x):
    return 1.0 / (1.0 + jnp.exp(-x))


def _silu(x):
    return x * _sigmoid(x)


def _softplus(x):
    return jnp.maximum(x, 0.0) + jnp.log(1.0 + jnp.exp(-jnp.abs(x)))


def _iota(shape, dim):
    return lax.broadcasted_iota(jnp.int32, shape, dim)


def _sgu_fn(u, v, sg, w, b):
    nc = T // A_CHUNK
    ug = _gelu(u)
    vn = _rms(_gelu(v), sg)
    causal = _iota((A_CHUNK, A_CHUNK), 0) >= _iota((A_CHUNK, A_CHUNK), 1)
    wm = jnp.where(causal, w, 0.0)
    wb = jnp.broadcast_to(wm[None], (nc, A_CHUNK, A_CHUNK))
    z = _mm(wb, vn.reshape(nc, A_CHUNK, HD)) + b[None]
    return ug * z.reshape(T, HD)


def _sgu_specs():
    col = lambda off: pl.BlockSpec((T, HD), lambda g, off=off: (0, off + g))
    par = [pl.BlockSpec((None, 1, HD), lambda g: (g, 0, 0)),
           pl.BlockSpec((None, A_CHUNK, A_CHUNK), lambda g: (g, 0, 0)),
           pl.BlockSpec((None, A_CHUNK, 1), lambda g: (g, 0, 0))]
    return col, par


def sgu_fwd(p2, sg, w, b):
    col, par = _sgu_specs()

    def body(u_ref, v_ref, sg_ref, w_ref, b_ref, y_ref):
        y_ref[...] = _sgu_fn(u_ref[...], v_ref[...], sg_ref[...], w_ref[...], b_ref[...]).astype(BF16)

    return pl.pallas_call(
        body, name="sgu_fwd", grid=(A_GROUPS,),
        in_specs=[col(P_AU // HD), col(P_AV // HD)] + par,
        out_specs=pl.BlockSpec((T, HD), lambda g: (0, g)),
        out_shape=jax.ShapeDtypeStruct((T, A_WIDTH), BF16),
        compiler_params=_cparams(("arbitrary",)),
    )(p2, p2, sg, w, b)


def sgu_bwd(p2, sg, w, b, dmix):
    col, par = _sgu_specs()

    def body(u_ref, v_ref, sg_ref, w_ref, b_ref, dy_ref, du_ref, dv_ref, dsg_ref, dw_ref, db_ref):
        _, vjp = jax.vjp(_sgu_fn, u_ref[...], v_ref[...], sg_ref[...], w_ref[...], b_ref[...])
        du, dv, dsg, dw, db = vjp(dy_ref[...])
        du_ref[...] = du.astype(BF16)
        dv_ref[...] = dv.astype(BF16)
        dsg_ref[...] = dsg
        dw_ref[...] = dw
        db_ref[...] = db

    gcol = pl.BlockSpec((T, HD), lambda g: (0, g))
    return pl.pallas_call(
        body, name="sgu_bwd", grid=(A_GROUPS,),
        in_specs=[col(P_AU // HD), col(P_AV // HD)] + par + [gcol],
        out_specs=[gcol, gcol] + par,
        out_shape=[jax.ShapeDtypeStruct((T, A_WIDTH), BF16), jax.ShapeDtypeStruct((T, A_WIDTH), BF16),
                   jax.ShapeDtypeStruct((A_GROUPS, 1, HD), F32), jax.ShapeDtypeStruct((A_GROUPS, A_CHUNK, A_CHUNK), F32),
                   jax.ShapeDtypeStruct((A_GROUPS, A_CHUNK, 1), F32)],
        compiler_params=_cparams(("arbitrary",)),
    )(p2, p2, sg, w, b, dmix)


def _attn_fn(q, k, v, qg, kg, slope, *, dil, nb):
    n = T // C_BLOCK
    qb = _rms(q, qg).reshape(n, C_BLOCK, HD)
    kb = _rms(k, kg).reshape(n, C_BLOCK, HD)
    vb = v.reshape(n, C_BLOCK, HD)
    scale = HD ** -0.5
    qi = _iota((n, C_BLOCK, C_BLOCK), 1)
    kj = _iota((n, C_BLOCK, C_BLOCK), 2)
    sl = slope[None] * float(dil)
    d_cur = qi - kj
    sc = jnp.where(d_cur >= 0, _mm(qb, kb, tb=True) * scale - sl * d_cur.astype(F32), -jnp.inf)
    mx = jnp.max(sc, axis=-1, keepdims=True)
    if nb > 1:
        kp = jnp.concatenate([jnp.zeros((1, C_BLOCK, HD), F32), kb[:-1]], axis=0)
        vp = jnp.concatenate([jnp.zeros((1, C_BLOCK, HD), F32), vb[:-1]], axis=0)
        has_prev = (_iota((n, C_BLOCK, C_BLOCK), 0) % nb) > 0
        d_prev = C_BLOCK + qi - kj
        sp = jnp.where((kj >= qi) & has_prev, _mm(qb, kp, tb=True) * scale - sl * d_prev.astype(F32), -jnp.inf)
        mx = jnp.maximum(mx, jnp.max(sp, axis=-1, keepdims=True))
    p = jnp.exp(sc - mx)
    den = jnp.sum(p, axis=-1, keepdims=True)
    if nb > 1:
        pp = jnp.exp(sp - mx)
        den = den + jnp.sum(pp, axis=-1, keepdims=True)
    out = _mm(p / den, vb)
    if nb > 1:
        out = out + _mm(pp / den, vp)
    lse = mx + jnp.log(den)
    return out.reshape(T, HD), jnp.broadcast_to(lse, (n, C_BLOCK, HD)).reshape(T, HD)


def _combine_fn(o1, o2, o3, l1, l2, l3):
    mx = jnp.maximum(jnp.maximum(l1, l2), l3)
    e1, e2, e3 = jnp.exp(l1 - mx), jnp.exp(l2 - mx), jnp.exp(l3 - mx)
    s = e1 + e2 + e3
    return (e1 / s) * o1 + (e2 / s) * o2 + (e3 / s) * o3


def _branch_blocks(dil):
    return -(-(T // dil) // C_BLOCK)


def _load_branch_order(ref, dil):
    if dil == 1:
        return ref[...]
    seg = T // dil
    return jnp.concatenate([ref[pl.ds(r, seg, stride=dil), :] for r in range(dil)], axis=0)


def _store_position_order(ref, val, dil, add=False):
    seg = T // dil
    for r in range(dil):
        rows = slice(None) if dil == 1 else pl.ds(r, seg, stride=dil)
        piece = val if dil == 1 else val[r * seg:(r + 1) * seg]
        if add:
            ref[rows, :] += piece
        else:
            ref[rows, :] = piece


def _dattn_specs():
    col = lambda off: pl.BlockSpec((T, HD), lambda h, off=off: (0, off // HD + h))
    row = pl.BlockSpec((1, HD), lambda h: (0, 0))
    slope = pl.BlockSpec((None, 1, HD), lambda h: (h, 0, 0))
    return [col(P_CQ), col(P_CK), col(P_CV), row, row, slope]


def _dattn_branches(q_ref, k_ref, v_ref, qg, kg, slope, o_scr, l_scr):
    for b, (_, dil) in enumerate(C_BRANCHES):
        q, k, v = (_load_branch_order(r, dil) for r in (q_ref, k_ref, v_ref))
        o, l = _attn_fn(q, k, v, qg, kg, slope, dil=dil, nb=_branch_blocks(dil))
        _store_position_order(o_scr.at[b], o, dil)
        _store_position_order(l_scr.at[b], l, dil)


def dattn_fwd(p2, qg, kg, slopes):
    per_branch = pl.BlockSpec((3, T, HD), lambda h: (0, 0, h))

    def body(q_ref, k_ref, v_ref, qg_ref, kg_ref, s_ref, y_ref, o_ref, l_ref):
        _dattn_branches(q_ref, k_ref, v_ref, qg_ref[...], kg_ref[...], s_ref[...], o_ref, l_ref)
        y_ref[...] = _combine_fn(o_ref[0], o_ref[1], o_ref[2], l_ref[0], l_ref[1], l_ref[2]).astype(BF16)

    return pl.pallas_call(
        body, name="dattn_fwd", grid=(C_HEADS,), in_specs=_dattn_specs(),
        out_specs=[pl.BlockSpec((T, HD), lambda h: (0, h)), per_branch, per_branch],
        out_shape=[jax.ShapeDtypeStruct((T, C_WIDTH), BF16)] + [jax.ShapeDtypeStruct((3, T, C_WIDTH), F32)] * 2,
        compiler_params=_cparams(("arbitrary",)),
    )(p2, p2, p2, qg, kg, slopes)


def dattn_bwd(p2, qg, kg, slopes, outs, lses, dmix):
    hcol = pl.BlockSpec((T, HD), lambda h: (0, h))
    row = pl.BlockSpec((1, HD), lambda h: (0, 0))
    dy = pl.BlockSpec((T, HD), lambda h: (0, (A_WIDTH + B_WIDTH) // HD + h))
    per_branch = pl.BlockSpec((3, T, HD), lambda h: (0, 0, h))

    def body(q_ref, k_ref, v_ref, qg_ref, kg_ref, s_ref, o_scr, l_scr, dy_ref, dq_ref, dk_ref, dv_ref, dqg_ref, dkg_ref, g_scr,
             acc):
        qg, kg, slope = qg_ref[...], kg_ref[...], s_ref[...]
        _, vjp = jax.vjp(_combine_fn, o_scr[0], o_scr[1], o_scr[2], l_scr[0], l_scr[1], l_scr[2])
        for i, g in enumerate(vjp(dy_ref[...])):
            g_scr[i] = g

        @pl.when(pl.program_id(0) == 0)
        def _():
            dqg_ref[...] = jnp.zeros_like(dqg_ref)
            dkg_ref[...] = jnp.zeros_like(dkg_ref)

        for b, (_, dil) in enumerate(C_BRANCHES):
            q, k, v = (_load_branch_order(r, dil) for r in (q_ref, k_ref, v_ref))
            do, dl = _load_branch_order(g_scr.at[b], dil), _load_branch_order(g_scr.at[3 + b], dil)
            fn = functools.partial(_attn_fn, dil=dil, nb=_branch_blocks(dil))
            _, vjp_b = jax.vjp(lambda a, b_, c, d, e, fn=fn: fn(a, b_, c, d, e, slope), q, k, v, qg, kg)
            dq, dk, dv, dqg, dkg = vjp_b((do, dl))
            for i, val in enumerate((dq, dk, dv)):
                _store_position_order(acc.at[i], val, dil, add=b > 0)
            dqg_ref[...] += dqg
            dkg_ref[...] += dkg
        for i, ref in enumerate((dq_ref, dk_ref, dv_ref)):
            ref[...] = acc[i].astype(BF16)

    scr = lambda n: pltpu.VMEM((n, T, HD), F32)
    return pl.pallas_call(
        body, name="dattn_bwd", grid=(C_HEADS,), in_specs=_dattn_specs() + [per_branch, per_branch, dy],
        out_specs=[hcol, hcol, hcol, row, row],
        out_shape=[jax.ShapeDtypeStruct((T, C_WIDTH), BF16)] * 3 + [jax.ShapeDtypeStruct((1, HD), F32)] * 2,
        scratch_shapes=[scr(6), scr(3)], compiler_params=_cparams(("arbitrary",)),
    )(p2, p2, p2, qg, kg, slopes, outs, lses, dmix)


_NCH = T // B_CHUNK


def _conv_taps(x, w_ref):
    rows = _iota(x.shape, 0)
    taps = []
    for j in range(B_CONV):
        s = B_CONV - 1 - j
        taps.append(x if s == 0 else jnp.where(rows >= s, pltpu.roll(x, s, 0), 0.0))
    pre = sum(w_ref[j:j + 1, :] * taps[j] for j in range(B_CONV))
    return pre, taps


def _conv_post(pre, mode):
    y = _silu(pre)
    if mode == "v":
        return y
    y = y * lax.rsqrt(jnp.sum(y * y, axis=-1, keepdims=True) + EPS)
    return y * (HD ** -0.5) if mode == "q" else y


def conv_fwd(p2, conv_w, mode):
    idx = "qkv".index(mode)
    xcol = pl.BlockSpec((T, HD), lambda h: (0, P_BQ // HD + B_HEADS * idx + h))
    wcol = pl.BlockSpec((B_CONV, HD), lambda h: (0, B_HEADS * idx + h))
    hcol = pl.BlockSpec((T, HD), lambda h: (0, h))

    def body(x_ref, w_ref, y_ref):
        pre, _ = _conv_taps(x_ref[...], w_ref)
        y_ref[...] = _conv_post(pre, mode)

    return pl.pallas_call(
        body, name=f"conv_fwd_{mode}", grid=(B_HEADS,), in_specs=[xcol, wcol], out_specs=hcol,
        out_shape=jax.ShapeDtypeStruct((T, B_WIDTH), F32), compiler_params=_cparams(("arbitrary",)),
    )(p2, conv_w)


def conv_bwd(p2, conv_w, dys, mode):
    idx = "qkv".index(mode)
    xcol = pl.BlockSpec((T, HD), lambda h: (0, P_BQ // HD + B_HEADS * idx + h))
    wcol = pl.BlockSpec((B_CONV, HD), lambda h: (0, B_HEADS * idx + h))
    hcol = pl.BlockSpec((T, HD), lambda h: (0, h))
    wout = pl.BlockSpec((B_CONV, HD), lambda h: (0, h))

    def body(x_ref, w_ref, *rest):
        dy_refs, (dx_ref, dw_ref) = rest[:-2], rest[-2:]
        pre, taps = _conv_taps(x_ref[...], w_ref)
        _, vjp = jax.vjp(functools.partial(_conv_post, mode=mode), pre)
        (dpre,) = vjp(sum(r[...] for r in dy_refs))
        rows = _iota(dpre.shape, 0)
        dx = w_ref[B_CONV - 1:B_CONV, :] * dpre
        for j in range(B_CONV):
            s = B_CONV - 1 - j
            dw_ref[j:j + 1, :] = jnp.sum(dpre * taps[j], axis=0, keepdims=True)
            if s > 0:
                dx = dx + w_ref[j:j + 1, :] * jnp.where(rows < T - s, pltpu.roll(dpre, T - s, 0), 0.0)
        dx_ref[...] = dx.astype(BF16)

    return pl.pallas_call(
        body, name=f"conv_bwd_{mode}", grid=(B_HEADS,), in_specs=[xcol, wcol] + [hcol] * len(dys), out_specs=[hcol, wout],
        out_shape=[jax.ShapeDtypeStruct((T, B_WIDTH), BF16), jax.ShapeDtypeStruct((B_CONV, B_WIDTH), F32)],
        compiler_params=_cparams(("arbitrary",)),
    )(p2, conv_w, *dys)


def _gates_fn(bg, al, dtb, h):
    r = _iota((HD, HD), 0)
    logit = _mm(bg, (r == h).astype(F32), hi=True)
    a = _mm(bg, (r == h + B_HEADS).astype(F32), hi=True)
    beta = _sigmoid(logit)
    graw = -jnp.exp(al) * _softplus(a + dtb)
    tri = (_iota((_NCH, B_CHUNK, B_CHUNK), 1) >= _iota((_NCH, B_CHUNK, B_CHUNK), 2)).astype(F32)
    g = _mm(tri, graw.reshape(_NCH, B_CHUNK, HD), hi=True).reshape(T, HD)
    return beta, g


def _gates_specs():
    bg = pl.BlockSpec((T, HD), lambda h: (0, P_BB // HD))
    par = pl.BlockSpec((None, 1, HD), lambda h: (h, 0, 0))
    out = pl.BlockSpec((None, T, HD), lambda h: (h, 0, 0))
    return bg, par, out


def gates_fwd(p2, al, dtb):
    bg, par, out = _gates_specs()

    def body(bg_ref, al_ref, dtb_ref, beta_ref, g_ref):
        beta, g = _gates_fn(bg_ref[...], al_ref[...], dtb_ref[...], pl.program_id(0))
        beta_ref[...] = beta
        g_ref[...] = g

    return pl.pallas_call(
        body, name="gates_fwd", grid=(B_HEADS,), in_specs=[bg, par, par], out_specs=[out, out],
        out_shape=[jax.ShapeDtypeStruct((B_HEADS, T, HD), F32)] * 2, compiler_params=_cparams(("arbitrary",)),
    )(p2, al, dtb)


def gates_bwd(p2, al, dtb, dbeta, dg1, dg2):
    bg, par, out = _gates_specs()
    acc = pl.BlockSpec((T, HD), lambda h: (0, 0))

    def body(bg_ref, al_ref, dtb_ref, dbeta_ref, dg1_ref, dg2_ref, dbg_ref, dal_ref, ddtb_ref, acc_ref):
        h = pl.program_id(0)
        _, vjp = jax.vjp(lambda a, b, c: _gates_fn(a, b, c, h), bg_ref[...], al_ref[...], dtb_ref[...])
        dbg, dal, ddtb = vjp((dbeta_ref[...], dg1_ref[...] + dg2_ref[...]))

        @pl.when(h == 0)
        def _():
            acc_ref[...] = jnp.zeros_like(acc_ref)

        acc_ref[...] += dbg
        dbg_ref[...] = acc_ref[...].astype(BF16)
        dal_ref[...] = jnp.broadcast_to(jnp.sum(dal, axis=-1, keepdims=True), (1, HD))
        ddtb_ref[...] = jnp.broadcast_to(jnp.sum(ddtb, axis=-1, keepdims=True), (1, HD))

    return pl.pallas_call(
        body, name="gates_bwd", grid=(B_HEADS,), in_specs=[bg, par, par, out, out, out], out_specs=[acc, par, par],
        out_shape=[jax.ShapeDtypeStruct((T, HD), BF16)] + [jax.ShapeDtypeStruct((B_HEADS, 1, HD), F32)] * 2,
        scratch_shapes=[pltpu.VMEM((T, HD), F32)], compiler_params=_cparams(("arbitrary",)),
    )(p2, al, dtb, dbeta, dg1, dg2)


def _unit_lower_inverse(a):
    eye = (_iota(a.shape, 1) == _iota(a.shape, 2)).astype(F32)
    x = eye - a
    p = _mm(a, a, hi=True)
    for i in range(5):
        x = x + _mm(x, p, hi=True)
        if i < 4:
            p = _mm(p, p, hi=True)
    return x


_WY_CH = 16
_WY_ROWS = _WY_CH * B_CHUNK


def _wy_fn(q, k, v, beta, g):
    sh = (q.shape[0] // B_CHUNK, B_CHUNK, HD)
    q3, k3, v3, b3, g3 = (t.reshape(sh) for t in (q, k, v, beta, g))
    gd = g3[:, :, :B_CHUNK] - jnp.swapaxes(g3, 1, 2)[:, :B_CHUNK, :]
    ii, jj = _iota(gd.shape, 1), _iota(gd.shape, 2)
    decay = jnp.exp(jnp.where(ii >= jj, gd, -jnp.inf))
    kb = k3 * b3
    a = _mm(kb, k3, tb=True) * jnp.where(ii > jj, decay, 0.0)
    tinv = _unit_lower_inverse(a)
    u = _mm(tinv, v3 * b3, hi=True)
    w = _mm(tinv, kb * jnp.exp(g3), hi=True)
    attn = _mm(q3, k3, tb=True) * decay
    return u.reshape(q.shape), w.reshape(q.shape), attn


def _wy_specs():
    hcol = pl.BlockSpec((_WY_ROWS, HD), lambda h, i: (i, h))
    hb = pl.BlockSpec((None, _WY_ROWS, HD), lambda h, i: (h, i, 0))
    at = pl.BlockSpec((None, _WY_CH, B_CHUNK, B_CHUNK), lambda h, i: (h, i, 0, 0))
    return hcol, hb, at


_WY_GRID = (B_HEADS, _NCH // _WY_CH)


def wy_fwd(q, k, v, beta, g):
    hcol, hb, at = _wy_specs()

    def body(q_ref, k_ref, v_ref, b_ref, g_ref, u_ref, w_ref, a_ref):
        u, w, a = _wy_fn(q_ref[...], k_ref[...], v_ref[...], b_ref[...], g_ref[...])
        u_ref[...] = u
        w_ref[...] = w
        a_ref[...] = a

    return pl.pallas_call(
        body, name="wy_fwd", grid=_WY_GRID, in_specs=[hcol, hcol, hcol, hb, hb], out_specs=[hcol, hcol, at],
        out_shape=[jax.ShapeDtypeStruct((T, B_WIDTH), F32)] * 2 + [jax.ShapeDtypeStruct((B_HEADS, _NCH, B_CHUNK, B_CHUNK), F32)],
        compiler_params=_cparams(("arbitrary", "arbitrary")),
    )(q, k, v, beta, g)


def wy_bwd(q, k, v, beta, g, du, dw, dattn):
    hcol, hb, at = _wy_specs()

    def body(q_ref, k_ref, v_ref, b_ref, g_ref, du_ref, dw_ref, da_ref, dq_ref, dk_ref, dv_ref, db_ref, dg_ref):
        _, vjp = jax.vjp(_wy_fn, q_ref[...], k_ref[...], v_ref[...], b_ref[...], g_ref[...])
        for r, t in zip((dq_ref, dk_ref, dv_ref, db_ref, dg_ref), vjp((du_ref[...], dw_ref[...], da_ref[...]))):
            r[...] = t

    return pl.pallas_call(
        body, name="wy_bwd", grid=_WY_GRID, in_specs=[hcol, hcol, hcol, hb, hb, hcol, hcol, at],
        out_specs=[hcol, hcol, hcol, hb, hb],
        out_shape=[jax.ShapeDtypeStruct((T, B_WIDTH), F32)] * 3 + [jax.ShapeDtypeStruct((B_HEADS, T, HD), F32)] * 2,
        compiler_params=_cparams(("arbitrary", "arbitrary")),
    )(q, k, v, beta, g, du, dw, dattn)


def _scan_step_fn(q, k, u, w, g, attn, gate, og, s):
    v_new = u - _mm(w, s)
    o = _mm(q * jnp.exp(g), s) + _mm(attn, v_new)
    g_last = jnp.sum(jnp.where(_iota(g.shape, 0) == B_CHUNK - 1, g, 0.0), axis=0, keepdims=True)
    s_new = s * jnp.exp(g_last) + _mm(k * jnp.exp(g_last - g), v_new, ta=True)
    return _rms(o, og) * _silu(gate), s_new


def _scan_specs(rev):
    ch = (lambda n: _NCH - 1 - n) if rev else (lambda n: n)
    rows = pl.BlockSpec((B_CHUNK, B_WIDTH), lambda n: (ch(n), 0))
    gb = pl.BlockSpec((B_HEADS, B_CHUNK, HD), lambda n: (0, ch(n), 0))
    at = pl.BlockSpec((B_HEADS, None, B_CHUNK, B_CHUNK), lambda n: (0, ch(n), 0, 0))
    og = pl.BlockSpec((1, HD), lambda n: (0, 0))
    st = pl.BlockSpec((None, B_HEADS, HD, HD), lambda n: (ch(n), 0, 0, 0))
    return rows, gb, at, og, st


def scan_fwd(q, k, u, w, g, attn, gate, og):
    rows, gb, at, ogs, st = _scan_specs(False)

    def body(q_ref, k_ref, u_ref, w_ref, g_ref, a_ref, gate_ref, og_ref, y_ref, st_ref, s_ref):
        @pl.when(pl.program_id(0) == 0)
        def _():
            s_ref[...] = jnp.zeros_like(s_ref)

        for h in range(B_HEADS):
            c = slice(h * HD, (h + 1) * HD)
            s = s_ref[h]
            st_ref[h] = s
            y, s_new = _scan_step_fn(q_ref[:, c], k_ref[:, c], u_ref[:, c], w_ref[:, c], g_ref[h], a_ref[h],
                                     gate_ref[:, c], og_ref[...], s)
            y_ref[:, c] = y.astype(BF16)
            s_ref[h] = s_new

    return pl.pallas_call(
        body, name="scan_fwd", grid=(_NCH,), in_specs=[rows, rows, rows, rows, gb, at, rows, ogs], out_specs=[rows, st],
        out_shape=[jax.ShapeDtypeStruct((T, B_WIDTH), BF16), jax.ShapeDtypeStruct((_NCH, B_HEADS, HD, HD), F32)],
        scratch_shapes=[pltpu.VMEM((B_HEADS, HD, HD), F32)], compiler_params=_cparams(("arbitrary",)),
    )(q, k, u, w, g, attn, gate, og)


def scan_bwd(q, k, u, w, g, attn, gate, og, states, dmix):
    rows, gb, at, ogs, st = _scan_specs(True)
    dyb = pl.BlockSpec((B_CHUNK, HD), lambda n: (_NCH - 1 - n, 0))

    def body(q_ref, k_ref, u_ref, w_ref, g_ref, a_ref, gate_ref, og_ref, st_ref, *rest):
        dy_refs, (dq_ref, dk_ref, du_ref, dw_ref, dgate_ref, dg_ref, da_ref, dog_ref, ds_ref) = rest[:B_HEADS], rest[B_HEADS:]

        @pl.when(pl.program_id(0) == 0)
        def _():
            ds_ref[...] = jnp.zeros_like(ds_ref)
            dog_ref[...] = jnp.zeros_like(dog_ref)

        for h in range(B_HEADS):
            c = slice(h * HD, (h + 1) * HD)
            _, vjp = jax.vjp(_scan_step_fn, q_ref[:, c], k_ref[:, c], u_ref[:, c], w_ref[:, c], g_ref[h], a_ref[h],
                             gate_ref[:, c], og_ref[...], st_ref[h])
            dq, dk, du, dw, dg, da, dgate, dog, ds = vjp((dy_refs[h][...], ds_ref[h]))
            dq_ref[:, c] = dq
            dk_ref[:, c] = dk
            du_ref[:, c] = du
            dw_ref[:, c] = dw
            dgate_ref[:, c] = dgate.astype(BF16)
            dg_ref[h] = dg
            da_ref[h] = da
            dog_ref[...] += dog
            ds_ref[h] = ds

    dy_specs = [pl.BlockSpec((B_CHUNK, HD), lambda n, h=h: (_NCH - 1 - n, A_WIDTH // HD + h)) for h in range(B_HEADS)]
    return pl.pallas_call(
        body, name="scan_bwd", grid=(_NCH,),
        in_specs=[rows, rows, rows, rows, gb, at, rows, ogs, st] + dy_specs,
        out_specs=[rows] * 5 + [gb, at, ogs],
        out_shape=[jax.ShapeDtypeStruct((T, B_WIDTH), F32)] * 4 + [jax.ShapeDtypeStruct((T, B_WIDTH), BF16)]
        + [jax.ShapeDtypeStruct((B_HEADS, T, HD), F32), jax.ShapeDtypeStruct((B_HEADS, _NCH, B_CHUNK, B_CHUNK), F32),
           jax.ShapeDtypeStruct((1, HD), F32)],
        scratch_shapes=[pltpu.VMEM((B_HEADS, HD, HD), F32)], compiler_params=_cparams(("arbitrary",)),
    )(q, k, u, w, g, attn, gate, og, states, *([dmix] * B_HEADS))


def _lanes(vec):
    return jnp.broadcast_to(vec[:, None, None], (vec.shape[0], 1, HD))


def gdn_forward(p2, conv_w, a_log, dt_bias, og):
    qa, ka, va = (conv_fwd(p2, conv_w, m) for m in "qkv")
    beta, g = gates_fwd(p2, _lanes(a_log), _lanes(dt_bias))
    u, w, attn = wy_fwd(qa, ka, va, beta, g)
    gate = p2[:, P_BG:P_BB]
    y, states = scan_fwd(qa, ka, u, w, g, attn, gate, og)
    return y, (qa, ka, va, beta, g, u, w, attn, gate, states)


def gdn_backward(p2, conv_w, a_log, dt_bias, og, saved, dmix):
    qa, ka, va, beta, g, u, w, attn, gate, states = saved
    dq1, dk1, du, dw, dgate, dg1, dattn, dog = scan_bwd(qa, ka, u, w, g, attn, gate, og, states, dmix)
    dq2, dk2, dv, dbeta, dg2 = wy_bwd(qa, ka, va, beta, g, du, dw, dattn)
    dbg, dal, ddtb = gates_bwd(p2, _lanes(a_log), _lanes(dt_bias), dbeta, dg1, dg2)
    dxq, dwq = conv_bwd(p2, conv_w, [dq1, dq2], "q")
    dxk, dwk = conv_bwd(p2, conv_w, [dk1, dk2], "k")
    dxv, dwv = conv_bwd(p2, conv_w, [dv], "v")
    return [dxq, dxk, dxv, dgate, dbg], jnp.concatenate([dwq, dwk, dwv], axis=1), dal[:, 0, 0], ddtb[:, 0, 0], dog


_SLOPES = np.exp2(-8.0 * (np.arange(C_HEADS, dtype=np.float64) + 1.0) / C_HEADS).astype(np.float32)


def _alibi_slopes():
    return _lanes(jnp.asarray(_SLOPES))


_ROWS = 512
_TM = 1024
_TM_FFN = 512


def _dep_specs(dep, ngrid):
    if dep is None:
        return [], []
    return [dep], [pl.BlockSpec((8, HD), lambda *_: (0, 0))]


def rmsnorm_fwd(x, g, dep=None):
    blk = pl.BlockSpec((_ROWS, D), lambda i: (i, 0))
    deps, dspecs = _dep_specs(dep, 1)

    def body(x_ref, g_ref, *rest):
        rest[-1][...] = _rms(x_ref[...], g_ref[...]).astype(BF16)

    return pl.pallas_call(
        body, name="rmsnorm_fwd", grid=(T // _ROWS,), in_specs=[blk, pl.BlockSpec((1, D), lambda i: (0, 0))] + dspecs,
        out_specs=blk, out_shape=jax.ShapeDtypeStruct((T, D), BF16), compiler_params=_cparams(("arbitrary",)),
    )(x, g, *deps)


def rmsnorm_bwd(x, g, dh, dres):
    blk = pl.BlockSpec((_ROWS, D), lambda i: (i, 0))
    row = pl.BlockSpec((1, D), lambda i: (0, 0))

    def body(x_ref, g_ref, dh_ref, dres_ref, dx_ref, dg_ref):
        _, vjp = jax.vjp(_rms, x_ref[...], g_ref[...])
        dx, dg = vjp(dh_ref[...])
        dx_ref[...] = dres_ref[...] + dx

        @pl.when(pl.program_id(0) == 0)
        def _():
            dg_ref[...] = jnp.zeros_like(dg_ref)

        dg_ref[...] += dg

    return pl.pallas_call(
        body, name="rmsnorm_bwd", grid=(T // _ROWS,), in_specs=[blk, row, blk, blk], out_specs=[blk, row],
        out_shape=[jax.ShapeDtypeStruct((T, D), F32), jax.ShapeDtypeStruct((1, D), F32)],
        compiler_params=_cparams(("arbitrary",)),
    )(x, g, dh, dres)


def _matmul(name, a, b, *, grid, a_spec, b_spec, o_spec, out_shape, ta=False, tb=False, k_axis=None, res=None, dep=None):
    dims = _dims(2, ta, tb)
    deps, dspecs = _dep_specs(dep, len(grid))

    def body(a_ref, b_ref, *rest):
        o_ref = rest[-1]
        prod = lax.dot_general(a_ref[...].astype(BF16), b_ref[...].astype(BF16), dims, preferred_element_type=F32)
        if res is not None:
            prod = prod + rest[0][...]
        if k_axis is None:
            o_ref[...] = prod.astype(o_ref.dtype)
        else:
            @pl.when(pl.program_id(k_axis) == 0)
            def _():
                o_ref[...] = prod

            @pl.when(pl.program_id(k_axis) > 0)
            def _():
                o_ref[...] += prod

    sem = tuple("arbitrary" for _ in grid)
    ins = [a, b] + ([res] if res is not None else []) + deps
    specs = [a_spec, b_spec] + ([o_spec] if res is not None else []) + dspecs
    return pl.pallas_call(
        body, name=name, grid=grid, in_specs=specs, out_specs=o_spec, out_shape=out_shape, compiler_params=_cparams(sem),
    )(*ins)


_IN_TN = P_END // 3


def mm_proj(h1, wp_in, l):
    return _matmul(
        "mm_proj", h1, wp_in, grid=(P_END // _IN_TN, T // _TM),
        a_spec=pl.BlockSpec((_TM, D), lambda j, i: (i, 0)),
        b_spec=pl.BlockSpec((None, D, _IN_TN), lambda j, i: (l, 0, j)),
        o_spec=pl.BlockSpec((_TM, _IN_TN), lambda j, i: (i, j)), out_shape=jax.ShapeDtypeStruct((T, P_END), F32))


def mm_dh1(dp2, wp_in, l):
    return _matmul(
        "mm_dh1", dp2, wp_in, grid=(T // _TM, P_END // _IN_TN), tb=True, k_axis=1,
        a_spec=pl.BlockSpec((_TM, _IN_TN), lambda i, k: (i, k)),
        b_spec=pl.BlockSpec((None, D, _IN_TN), lambda i, k: (l, 0, k)),
        o_spec=pl.BlockSpec((_TM, D), lambda i, k: (i, 0)), out_shape=jax.ShapeDtypeStruct((T, D), F32))


def mm_dwin(h1, dp2):
    return _matmul(
        "mm_dwin", h1, dp2, grid=(P_END // _IN_TN, D // _TM), ta=True,
        a_spec=pl.BlockSpec((T, _TM), lambda j, i: (0, i)),
        b_spec=pl.BlockSpec((T, _IN_TN), lambda j, i: (0, j)),
        o_spec=pl.BlockSpec((_TM, _IN_TN), lambda j, i: (i, j)), out_shape=jax.ShapeDtypeStruct((D, P_END), BF16))


def _mm_square(name, a, w, l, res, tb, dep=None):
    tn = 1024
    b_spec = (pl.BlockSpec((None, tn, D), lambda j, i: (l, j, 0)) if tb else pl.BlockSpec((None, D, tn), lambda j, i: (l, 0, j)))
    return _matmul(
        name, a, w, grid=(D // tn, T // _TM), tb=tb, res=res, dep=dep,
        a_spec=pl.BlockSpec((_TM, D), lambda j, i: (i, 0)), b_spec=b_spec,
        o_spec=pl.BlockSpec((_TM, tn), lambda j, i: (i, j)), out_shape=jax.ShapeDtypeStruct((T, D), F32))


def mm_out(mix, wg_out, l, x):
    return _mm_square("mm_out", mix, wg_out, l, x, False)


def mm_dmix(dx1, wg_out, l, dep=None):
    return _mm_square("mm_dmix", dx1, wg_out, l, None, True, dep)


def mm_dwout(mix, dx1):
    tn = 1024
    return _matmul(
        "mm_dwout", mix, dx1, grid=(D // tn, D // _TM), ta=True,
        a_spec=pl.BlockSpec((T, _TM), lambda j, i: (0, i)), b_spec=pl.BlockSpec((T, tn), lambda j, i: (0, j)),
        o_spec=pl.BlockSpec((_TM, tn), lambda j, i: (i, j)), out_shape=jax.ShapeDtypeStruct((D, D), BF16))


_GU_TN = GU_SHARD // 2


_GU_NJ = FFN // _GU_TN


def mm_dh2(dgu, wg_gu, l):
    return _matmul(
        "mm_dh2", dgu, wg_gu, grid=(T // _TM, 2 * N_CHIPS), tb=True, k_axis=1,
        a_spec=pl.BlockSpec((None, _TM, _GU_TN), lambda i, k: (k // _GU_NJ, i, k % _GU_NJ)),
        b_spec=pl.BlockSpec((None, None, D, _GU_TN), lambda i, k: (l, k // 2, 0, k % 2)),
        o_spec=pl.BlockSpec((_TM, D), lambda i, k: (i, 0)), out_shape=jax.ShapeDtypeStruct((T, D), F32))


def mm_dwgu(h2, dgu):
    return _matmul(
        "mm_dwgu", h2, dgu, grid=(N_CHIPS, 2, D // _TM), ta=True,
        a_spec=pl.BlockSpec((T, _TM), lambda s, j, i: (0, i)),
        b_spec=pl.BlockSpec((None, T, _GU_TN), lambda s, j, i: ((2 * s + j) // _GU_NJ, 0, (2 * s + j) % _GU_NJ)),
        o_spec=pl.BlockSpec((None, _TM, _GU_TN), lambda s, j, i: (s, i, j)),
        out_shape=jax.ShapeDtypeStruct((N_CHIPS, D, GU_SHARD), BF16))


def mm_down(act, wg_down, l, x1):
    tn = 512
    return _matmul(
        "mm_down", act, wg_down, grid=(D // tn, T // _TM), res=x1,
        a_spec=pl.BlockSpec((_TM, FFN), lambda j, i: (i, 0)),
        b_spec=pl.BlockSpec((None, FFN, tn), lambda j, i: (l, 0, j)),
        o_spec=pl.BlockSpec((_TM, tn), lambda j, i: (i, j)), out_shape=jax.ShapeDtypeStruct((T, D), F32))


def mm_dwdown(act, dx2):
    tm, tn = DOWN_SHARD, 512
    return _matmul(
        "mm_dwdown", act, dx2, grid=(D // tn, FFN // tm), ta=True,
        a_spec=pl.BlockSpec((T, tm), lambda j, i: (0, i)), b_spec=pl.BlockSpec((T, tn), lambda j, i: (0, j)),
        o_spec=pl.BlockSpec((tm, tn), lambda j, i: (i, j)), out_shape=jax.ShapeDtypeStruct((FFN, D), BF16))


_FF_TN = 1408


def _swiglu_fn(gt, up):
    return _silu(gt) * up


def _gate_up_specs():
    gate = pl.BlockSpec((None, None, D, _FF_TN), lambda j, i: (0, j // 2, 0, j % 2))
    up = pl.BlockSpec((None, None, D, _FF_TN), lambda j, i: (0, N_CHIPS // 2 + j // 2, 0, j % 2))
    both = pl.BlockSpec((2, _TM_FFN, _FF_TN), lambda j, i: (0, i, j))
    return gate, up, both


def mm_gu_swiglu(h2, wg_gu):
    gate, up, both = _gate_up_specs()

    def body(h_ref, wg_ref, wu_ref, gu_ref, act_ref):
        h = h_ref[...]
        gt = jnp.dot(h, wg_ref[...], preferred_element_type=F32)
        u = jnp.dot(h, wu_ref[...], preferred_element_type=F32)
        gu_ref[0] = gt
        gu_ref[1] = u
        act_ref[...] = _swiglu_fn(gt, u).astype(BF16)

    return pl.pallas_call(
        body, name="mm_gu_swiglu", grid=(FFN // _FF_TN, T // _TM_FFN),
        in_specs=[pl.BlockSpec((_TM_FFN, D), lambda j, i: (i, 0)), gate, up],
        out_specs=[both, pl.BlockSpec((_TM_FFN, _FF_TN), lambda j, i: (i, j))],
        out_shape=[jax.ShapeDtypeStruct((2, T, FFN), F32), jax.ShapeDtypeStruct((T, FFN), BF16)],
        compiler_params=_cparams(("arbitrary", "arbitrary")),
    )(h2, wg_gu, wg_gu)


def mm_dact_swiglu(dx2, wg_down, gu, dep=None):
    _, _, both = _gate_up_specs()
    deps, dspecs = _dep_specs(dep, 2)

    def body(dx_ref, w_ref, gu_ref, *rest):
        dact = lax.dot_general(dx_ref[...].astype(BF16), w_ref[...], _dims(2, False, True), preferred_element_type=F32)
        _, vjp = jax.vjp(_swiglu_fn, gu_ref[0], gu_ref[1])
        dgt, dup = vjp(dact)
        rest[-1][0] = dgt.astype(BF16)
        rest[-1][1] = dup.astype(BF16)

    return pl.pallas_call(
        body, name="mm_dact_swiglu", grid=(FFN // _FF_TN, T // _TM_FFN),
        in_specs=[pl.BlockSpec((_TM_FFN, D), lambda j, i: (i, 0)), pl.BlockSpec((None, _FF_TN, D), lambda j, i: (0, j, 0)),
                  both]
        + dspecs,
        out_specs=both, out_shape=jax.ShapeDtypeStruct((2, T, FFN), BF16),
        compiler_params=_cparams(("arbitrary", "arbitrary")),
    )(dx2, wg_down, gu, *deps)


def loss_and_grad(y, target):
    blk = pl.BlockSpec((_ROWS, D), lambda i: (i, 0))
    acc = pl.BlockSpec((8, HD), lambda i: (0, 0))

    def body(y_ref, t_ref, dy_ref, l_ref):
        err = y_ref[...] - t_ref[...]
        dy_ref[...] = err * (1.0 / D)

        @pl.when(pl.program_id(0) == 0)
        def _():
            l_ref[...] = jnp.zeros_like(l_ref)

        l_ref[...] += (0.5 / D) * jnp.sum(err * err)

    return pl.pallas_call(
        body, name="loss_and_grad", grid=(T // _ROWS,), in_specs=[blk, blk], out_specs=[blk, acc],
        out_shape=[jax.ShapeDtypeStruct((T, D), F32), jax.ShapeDtypeStruct((8, HD), F32)],
        compiler_params=_cparams(("arbitrary",)),
    )(y, target)


def adamw(w, g, m, v, name):
    rows, cols = w.shape
    tr = _ROWS if rows % _ROWS == 0 else rows
    blk = pl.BlockSpec((tr, cols), lambda i: (i, 0))

    def body(w_ref, g_ref, m_ref, v_ref, d_ref, nm_ref, nv_ref):
        gg = g_ref[...]
        nm = ADAM_B1 * m_ref[...] + (1.0 - ADAM_B1) * gg
        nv = ADAM_B2 * v_ref[...] + (1.0 - ADAM_B2) * (gg * gg)
        m_hat = nm / (1.0 - ADAM_B1 ** ADAM_STEP)
        v_hat = nv / (1.0 - ADAM_B2 ** ADAM_STEP)
        d_ref[...] = -ADAM_LR * (m_hat / (jnp.sqrt(v_hat) + ADAM_EPS) + ADAM_WD * w_ref[...])
        nm_ref[...] = nm
        nv_ref[...] = nv

    return pl.pallas_call(
        body, name=name, grid=(rows // tr,), in_specs=[blk] * 4, out_specs=[blk] * 3,
        out_shape=[jax.ShapeDtypeStruct(w.shape, F32)] * 3, compiler_params=_cparams(("arbitrary",)),
    )(w, g, m, v)


_LANE = 128


def _segment_of_shard_column():
    flat = np.full(P_END, -1, np.int64)
    for o in range(P_END):
        if GATE_COLS <= o < P_CQ:
            continue
        c = o if o < GATE_COLS else o - (P_CQ - GATE_COLS)
        flat[o] = (c // IN_SHARD) * IN_SHARD_PAD + c % IN_SHARD
    return flat


def _block_pairs(src_of_dst):
    return [sorted({int(c) // _LANE for c in src_of_dst[db * _LANE:(db + 1) * _LANE] if c >= 0})
            for db in range(len(src_of_dst) // _LANE)]


_RELAYOUT_ROWS = 512
_SHARD_BLOCKS = IN_SHARD_PAD // _LANE


def _relayout(name, x, to_segments):
    seg_of = _segment_of_shard_column()
    if to_segments:
        src_of_dst = seg_of
    else:
        src_of_dst = np.full(N_CHIPS * IN_SHARD_PAD, -1, np.int64)
        src_of_dst[seg_of[seg_of >= 0]] = np.nonzero(seg_of >= 0)[0]
    sources = _block_pairs(src_of_dst)
    n_dst = len(sources)
    col_map = jnp.asarray(src_of_dst.reshape(n_dst, 1, _LANE), jnp.int32)
    shard_blk = pl.BlockSpec((N_CHIPS, _RELAYOUT_ROWS, IN_SHARD_PAD), lambda i: (0, i, 0))
    seg_blk = pl.BlockSpec((_RELAYOUT_ROWS, P_END), lambda i: (i, 0))

    def shard_cols(ref, b):
        return ref.at[b // _SHARD_BLOCKS, :, pl.ds((b % _SHARD_BLOCKS) * _LANE, _LANE)]

    def seg_cols(ref, b):
        return ref.at[:, pl.ds(b * _LANE, _LANE)]

    src_cols, dst_cols = (shard_cols, seg_cols) if to_segments else (seg_cols, shard_cols)

    def body(x_ref, map_ref, o_ref):
        src_row = _iota((_LANE, _LANE), 0)
        for d in range(n_dst):
            acc = jnp.zeros((_RELAYOUT_ROWS, _LANE), F32)
            for sb in sources[d]:
                sel = (src_row + sb * _LANE == map_ref[d]).astype(x_ref.dtype)
                acc = acc + jnp.dot(src_cols(x_ref, sb)[...], sel, preferred_element_type=F32)
            dst_cols(o_ref, d)[...] = acc.astype(o_ref.dtype)

    rows = x.shape[-2]
    out_shape = (rows, P_END) if to_segments else (N_CHIPS, rows, IN_SHARD_PAD)
    return pl.pallas_call(
        body, name=name, grid=(rows // _RELAYOUT_ROWS,),
        in_specs=[shard_blk if to_segments else seg_blk, pl.BlockSpec(col_map.shape, lambda i: (0, 0, 0))],
        out_specs=seg_blk if to_segments else shard_blk, out_shape=jax.ShapeDtypeStruct(out_shape, x.dtype),
        compiler_params=_cparams(("arbitrary",)),
    )(x, col_map)


def shards_to_segments(w):
    return _relayout("shards_to_segments", w, True)


def segments_to_shards(w):
    return _relayout("segments_to_shards", w, False)


def mixers_forward(x, w_in, sp, dep=None, h1=None):
    if h1 is None:
        h1 = rmsnorm_fwd(x, sp["norm1_g"], dep)
    p2 = mm_proj(h1, w_in, 0)
    y_a = sgu_fwd(p2, sp["sgu_norm_g"], sp["w_spatial"], sp["b_spatial"])
    y_b, saved_b = gdn_forward(p2, sp["conv_w"], sp["a_log"], sp["dt_bias"], sp["o_norm_g"])
    y_c, outs_c, lses_c = dattn_fwd(p2, sp["q_norm_g"], sp["k_norm_g"], _alibi_slopes())
    mix = jnp.concatenate([y_a, y_b, y_c], axis=1)
    return mix, (x, h1, p2, saved_b, (outs_c, lses_c), mix)


def ffn_up(x, mix, w_out, w_gu, sp):
    x1 = mm_out(mix, w_out, 0, x)
    h2 = rmsnorm_fwd(x1, sp["norm2_g"])
    gu, act = mm_gu_swiglu(h2, w_gu)
    return x1, h2, gu, act


def ffn_forward(x, mix, wg, sp):
    x1, h2, gu, act = ffn_up(x, mix, wg["out"], wg["gu"], sp)
    x2 = mm_down(act, wg["down"], 0, x1)
    return x2, (x1, h2, gu, act)


def ffn_backward(dx2, wg, sp, saved, dep=None):
    x1, h2, gu, act = saved
    dgu = mm_dact_swiglu(dx2, wg["down"], gu, dep)
    dw_down = mm_dwdown(act, dx2)
    dw_gu = mm_dwgu(h2, dgu)
    dh2 = mm_dh2(dgu, wg["gu"], 0)
    dx1, dnorm2 = rmsnorm_bwd(x1, sp["norm2_g"], dh2, dx2)
    return dx1, dnorm2, dw_gu, dw_down


def mixers_backward(dx1, wg, sp, saved, dep=None):
    x, h1, p2, saved_b, saved_c, mix = saved
    dmix = mm_dmix(dx1, wg["out"], 0, dep)
    dw_out = mm_dwout(mix, dx1)
    du, dv, dsg, dws, dbs = sgu_bwd(p2, sp["sgu_norm_g"], sp["w_spatial"], sp["b_spatial"], dmix)
    dseg_b, dconv, dal, ddtb, dog = gdn_backward(p2, sp["conv_w"], sp["a_log"], sp["dt_bias"], sp["o_norm_g"], saved_b, dmix)
    dcq, dck, dcv, dqg, dkg = dattn_bwd(p2, sp["q_norm_g"], sp["k_norm_g"], _alibi_slopes(), *saved_c, dmix)
    dp2 = jnp.concatenate([du, dv] + dseg_b + [dcq, dck, dcv], axis=1)
    dw_in = segments_to_shards(mm_dwin(h1, dp2))
    dh1 = mm_dh1(dp2, wg["in"], 0)
    dx, dnorm1 = rmsnorm_bwd(x, sp["norm1_g"], dh1, dx1)
    small = {"norm1_g": dnorm1, "sgu_norm_g": dsg, "w_spatial": dws, "b_spatial": dbs, "conv_w": dconv, "a_log": dal,
             "dt_bias": ddtb, "o_norm_g": dog, "q_norm_g": dqg, "k_norm_g": dkg}
    return dx, dw_in, dw_out, small


_HBM = pl.BlockSpec(memory_space=pltpu.HBM)
_MESH = pl.DeviceIdType.MESH


def _place():
    x, y, c = lax.axis_index("x"), lax.axis_index("y"), lax.axis_index("c")
    chips = [(1 - x, y), (x, 1 - y), (1 - x, 1 - y)]
    return x, y, c, chips


def _rcopy(src, dst, ssem, rsem, dev):
    return pltpu.make_async_remote_copy(src_ref=src, dst_ref=dst, send_sem=ssem, recv_sem=rsem, device_id=dev,
                                        device_id_type=_MESH)


_SEM = pl.BlockSpec(memory_space=pltpu.SEMAPHORE)
_SIDE_EFFECT = pltpu.SideEffectType.DATAFLOW_SIDE_EFFECTING


def _in_hbm(a):
    return pltpu.with_memory_space_constraint(a, pltpu.HBM)


def _split_copy(name, srcs, land_shapes, n_sems, copies):
    n, m = len(srcs), len(land_shapes)
    thru = [pltpu.HBM(a.shape, a.dtype) for a in srcs] + [pltpu.HBM(s.shape, s.dtype) for s in land_shapes]
    sems = (pltpu.SemaphoreType.DMA((n_sems,)), pltpu.SemaphoreType.DMA((n_sems,)))

    def start(dep=None):
        deps = [] if dep is None else [dep]

        def body(*refs):
            ins, lands = refs[:n], refs[n:n + m]
            ssem, rsem, token = refs[n + m + len(deps)], refs[n + m + len(deps) + 1], refs[-1]
            for cp in copies(ins, lands, ssem, rsem)[0]:
                cp.start()
            token[...] = jnp.zeros_like(token)

        out = pl.pallas_call(
            body, name=name + "_start", out_shape=(*sems, *thru, jax.ShapeDtypeStruct((8, HD), F32)),
            in_specs=[_HBM] * (n + m) + [pl.BlockSpec(memory_space=pl.ANY)] * len(deps),
            out_specs=(_SEM, _SEM, *[_HBM] * (n + m), pl.BlockSpec(memory_space=pltpu.VMEM)),
            input_output_aliases={i: 2 + i for i in range(n + m)},
            compiler_params=pltpu.CompilerParams(has_side_effects=_SIDE_EFFECT),
        )(*[_in_hbm(a) for a in srcs], *[_in_hbm(lax.empty(s.shape, s.dtype)) for s in land_shapes], *deps)
        return out[:-1], out[-1]

    def wait(state, after):
        def body(*refs):
            ins, lands, ssem, rsem = refs[:n], refs[n:n + m], refs[n + m], refs[n + m + 1]
            sent, arrivals = copies(ins, lands, ssem, rsem)
            for cp in sent:
                cp.wait_send()
            for cp in arrivals:
                cp.wait_recv()

        out = pl.pallas_call(
            body, name=name + "_wait", out_shape=tuple(thru),
            in_specs=[_HBM] * (n + m) + [_SEM, _SEM, pl.BlockSpec(memory_space=pl.ANY)], out_specs=[_HBM] * (n + m),
            input_output_aliases={i: i for i in range(n + m)},
            compiler_params=pltpu.CompilerParams(has_side_effects=_SIDE_EFFECT),
        )(*state[2:], state[0], state[1], after)
        return list(out[:n]), list(out[n:])

    return start, wait


def gather_direct(shards, tag):
    n = len(shards)

    def copies(ins, lands, ssem, rsem):
        x, y, c, chips = _place()
        s = 2 * x + y
        sibling = (x, y, 1 - c)
        sent, arrivals = [], []
        for a in range(n):
            for u in range(2):
                cp = _rcopy(ins[a].at[u], lands[a].at[s, u], ssem.at[5 * a + u], rsem.at[5 * a + u], sibling)
                sent.append(cp)
                arrivals.append(cp)
            for j, (cx, cy) in enumerate(chips):
                k = 5 * a + 2 + j
                sent.append(_rcopy(ins[a].at[c], lands[a].at[s, c], ssem.at[k], rsem.at[k], (cx, cy, c)))
                arrivals.append(_rcopy(ins[a].at[c], lands[a].at[2 * cx + cy, c], ssem.at[k], rsem.at[k], (cx, cy, c)))
        return sent, arrivals

    lands = [jax.ShapeDtypeStruct((N_CHIPS,) + w.shape, w.dtype) for w in shards]
    return _split_copy("gather_direct_" + tag, shards, lands, 5 * n, copies)


def pass_to_sibling(lands):
    n = len(lands)

    def body(*refs):
        ins = refs[:n]
        ssem, rsem = refs[2 * n:]
        x, y, c, chips = _place()
        sibling = (x, y, 1 - c)
        cps, arrivals = [], []
        for a in range(n):
            for j, (cx, cy) in enumerate(chips):
                t = 2 * cx + cy
                cps.append(_rcopy(ins[a].at[t, c], ins[a].at[t, c], ssem.at[a, j], rsem.at[a, j], sibling))
                arrivals.append(_rcopy(ins[a].at[t, c], ins[a].at[t, 1 - c], ssem.at[a, j], rsem.at[a, j], sibling))
        for cp in cps:
            cp.start()
        for cp, ar in zip(cps, arrivals):
            cp.wait_send()
            ar.wait_recv()

    return pl.pallas_call(
        body, name="pass_to_sibling", in_specs=[_HBM] * n, out_specs=[_HBM] * n,
        out_shape=[jax.ShapeDtypeStruct(a.shape, a.dtype) for a in lands], input_output_aliases={a: a for a in range(n)},
        scratch_shapes=[pltpu.SemaphoreType.DMA((n, 3)), pltpu.SemaphoreType.DMA((n, 3))],
    )(*lands)


def exchange_halves(grads):
    n = len(grads)

    def body(*refs):
        ins, outs = refs[:n], refs[n:2 * n]
        ssem, rsem = refs[2 * n:]
        x, y, c, _ = _place()
        cps = []
        for a in range(n):
            h = grads[a].shape[1] // 2
            cps.append(_rcopy(ins[a].at[:, pl.ds((1 - c) * h, h)], outs[a], ssem.at[a], rsem.at[a], (x, y, 1 - c)))
        for cp in cps:
            cp.start()
        for cp in cps:
            cp.wait()

    return pl.pallas_call(
        body, name="exchange_halves", in_specs=[_HBM] * n, out_specs=[_HBM] * n,
        out_shape=[jax.ShapeDtypeStruct((g.shape[0], g.shape[1] // 2, g.shape[2]), g.dtype) for g in grads],
        scratch_shapes=[pltpu.SemaphoreType.DMA((n,)), pltpu.SemaphoreType.DMA((n,))],
    )(*grads)


def scatter_direct(parts, tag):
    n = len(parts)

    def copies(ins, lands, ssem, rsem):
        x, y, c, chips = _place()
        cps = [_rcopy(ins[a].at[2 * cx + cy], lands[a].at[j], ssem.at[3 * a + j], rsem.at[3 * a + j], (cx, cy, c))
               for a in range(n) for j, (cx, cy) in enumerate(chips)]
        return cps, cps

    lands = [jax.ShapeDtypeStruct((3,) + p.shape[1:], p.dtype) for p in parts]
    return _split_copy("scatter_direct_" + tag, parts, lands, 3 * n, copies)


def share_halves(halves):
    n = len(halves)

    def body(*refs):
        ins, outs = refs[:n], refs[n:2 * n]
        ssem, rsem = refs[2 * n:]
        x, y, c, _ = _place()
        cps = [_rcopy(ins[i], outs[i], ssem.at[i], rsem.at[i], (x, y, 1 - c)) for i in range(n)]
        for cp in cps:
            cp.start()
        for cp in cps:
            cp.wait()

    return pl.pallas_call(
        body, name="share_halves", in_specs=[_HBM] * n, out_specs=[_HBM] * n,
        out_shape=[jax.ShapeDtypeStruct(h.shape, h.dtype) for h in halves],
        scratch_shapes=[pltpu.SemaphoreType.DMA((n,)), pltpu.SemaphoreType.DMA((n,))],
    )(*halves)


def adamw_shard(w, m, v, mine, theirs, c, name):
    _, r, cw = w.shape
    h, cg = mine[0].shape
    tr = next(t for t in (256, 176, 128) if h % t == 0 and t * cg * 4 <= (3 << 19))
    nb = h // tr
    wblk = pl.BlockSpec((None, tr, cw), lambda l, i, c_ref: (l, i, 0))
    gblk = lambda layer, own: pl.BlockSpec((tr, cg), lambda l, i, c_ref: (_held_block(l, i, c_ref, layer, own, nb), 0))
    return _adamw_halves(w, m, v, mine, theirs, c, name, (DEPTH, r // tr), wblk, gblk, nb, cw)


def _held_block(l, i, c_ref, layer, own, nb):
    in_use = (l == layer) & (((i // nb) == c_ref[0]) == own)
    return jnp.where(in_use, i % nb, 0)


def _adamw_halves(w, m, v, mine, theirs, c, name, grid, wblk, gblk, nb, cw):
    def body(c_ref, w_ref, m_ref, v_ref, m0, m1, t0, t1, g_ref, d_ref, nm_ref, nv_ref):
        is_mine = (pl.program_id(1) // nb) == c_ref[0]
        first = pl.program_id(0) == 0
        gg = jnp.where(is_mine, jnp.where(first, m0[:, :cw], m1[:, :cw]), jnp.where(first, t0[:, :cw], t1[:, :cw]))
        nm = ADAM_B1 * m_ref[...] + (1.0 - ADAM_B1) * gg
        nv = ADAM_B2 * v_ref[...] + (1.0 - ADAM_B2) * (gg * gg)
        m_hat = nm / (1.0 - ADAM_B1 ** ADAM_STEP)
        v_hat = nv / (1.0 - ADAM_B2 ** ADAM_STEP)
        g_ref[...] = gg
        d_ref[...] = -ADAM_LR * (m_hat / (jnp.sqrt(v_hat) + ADAM_EPS) + ADAM_WD * w_ref[...])
        nm_ref[...] = nm
        nv_ref[...] = nv

    return pl.pallas_call(
        body, name=name,
        grid_spec=pltpu.PrefetchScalarGridSpec(
            num_scalar_prefetch=1, grid=grid,
            in_specs=[wblk] * 3 + [gblk(0, True), gblk(1, True), gblk(0, False), gblk(1, False)], out_specs=[wblk] * 4),
        out_shape=[jax.ShapeDtypeStruct(w.shape, F32)] * 4, compiler_params=_cparams(("arbitrary", "arbitrary")),
    )(c, w, m, v, mine[0], mine[1], theirs[0], theirs[1])


def adamw_shard_t(wt, mt, vt, mine_t, theirs_t, c, name):
    _, cw, r = wt.shape
    h = mine_t[0].shape[1]
    tc = 256
    nb = h // tc
    wblk = pl.BlockSpec((None, cw, tc), lambda l, j, c_ref: (l, 0, j))
    gblk = lambda layer, own: pl.BlockSpec((cw, tc), lambda l, j, c_ref: (0, _held_block(l, j, c_ref, layer, own, nb)))
    return _adamw_halves(wt, mt, vt, mine_t, theirs_t, c, name, (DEPTH, r // tc), wblk, gblk, nb, cw)


def _half_rows(h, cols):
    for tr in (512, 256, 352, 128, 64):
        if h % tr == 0 and tr * cols * 4 <= 6 * 1024 * 1024:
            return tr
    raise ValueError((h, cols))


def add_sibling(grad, recv, c):
    _, r, cols = grad.shape
    h = r // 2
    tr = _half_rows(h, cols)
    nb = h // tr

    def body(c_ref, g_ref, r_ref, o_ref):
        o_ref[...] = (g_ref[...].astype(F32) + r_ref[...].astype(F32)).astype(BF16)

    return pl.pallas_call(
        body, name="add_sibling",
        grid_spec=pltpu.PrefetchScalarGridSpec(
            num_scalar_prefetch=1, grid=(N_CHIPS, nb),
            in_specs=[pl.BlockSpec((None, tr, cols), lambda t, i, c_ref: (t, c_ref[0] * nb + i, 0)),
                      pl.BlockSpec((None, tr, cols), lambda t, i, c_ref: (t, i, 0))],
            out_specs=pl.BlockSpec((None, tr, cols), lambda t, i, c_ref: (t, i, 0))),
        out_shape=jax.ShapeDtypeStruct((N_CHIPS, h, cols), BF16), compiler_params=_cparams(("arbitrary", "arbitrary")),
    )(c, grad, recv)


def add_chips(part, recv, s):
    _, h, cols = part.shape
    tr = _half_rows(h, cols)

    def body(s_ref, p_ref, r_ref, o_ref):
        o_ref[...] = ((p_ref[...].astype(F32) + r_ref[0].astype(F32)) + r_ref[1].astype(F32)) + r_ref[2].astype(F32)

    return pl.pallas_call(
        body, name="add_chips",
        grid_spec=pltpu.PrefetchScalarGridSpec(
            num_scalar_prefetch=1, grid=(h // tr,),
            in_specs=[pl.BlockSpec((None, tr, cols), lambda i, s_ref: (s_ref[0], i, 0)),
                      pl.BlockSpec((3, tr, cols), lambda i, s_ref: (0, i, 0))],
            out_specs=pl.BlockSpec((tr, cols), lambda i, s_ref: (i, 0))),
        out_shape=jax.ShapeDtypeStruct((h, cols), F32), compiler_params=_cparams(("arbitrary",)),
    )(s, part, recv)


def allreduce_small(vec):
    rows = vec.shape[0]

    def body(v_ref, o_ref, buf, ssem, rsem, lsem):
        x, y, c, chips = _place()
        me, sibling = (x, y, c), (x, y, 1 - c)

        def blk(px, py, pc):
            return buf.at[4 * px + 2 * py + pc]

        def copy(k, block, to, src=None):
            return _rcopy(blk(*block) if src is None else src, blk(*block), ssem.at[k], rsem.at[k], to)

        mine = pltpu.make_async_copy(v_ref, blk(*me), lsem)
        mine.start()
        first = [copy(0, me, sibling, src=v_ref)] + [copy(1 + j, me, (*chip, c), src=v_ref) for j, chip in enumerate(chips)]
        for cp in first:
            cp.start()
        passed = [copy(4 + j, (*chip, c), sibling) for j, chip in enumerate(chips)]
        for j, chip in enumerate(chips):
            copy(1 + j, (*chip, c), me).wait_recv()
            passed[j].start()
        copy(0, sibling, me).wait_recv()
        for j, chip in enumerate(chips):
            copy(4 + j, (*chip, 1 - c), me).wait_recv()
        for cp in first + passed:
            cp.wait_send()
        mine.wait()
        acc = buf[0]
        for d in range(1, N_DEV):
            acc = acc + buf[d]
        o_ref[...] = acc

    vm = pl.BlockSpec(memory_space=pltpu.VMEM)
    return pl.pallas_call(
        body, name="allreduce_small", in_specs=[vm], out_specs=vm, out_shape=jax.ShapeDtypeStruct(vec.shape, F32),
        scratch_shapes=[pltpu.VMEM((N_DEV, rows, HD), F32), pltpu.SemaphoreType.DMA((7,)), pltpu.SemaphoreType.DMA((7,)),
                        pltpu.SemaphoreType.DMA],
        compiler_params=pltpu.CompilerParams(vmem_limit_bytes=VMEM_LIMIT),
    )(vec)


SMALL_NAMES = ("norm1_g", "sgu_norm_g", "w_spatial", "b_spatial", "conv_w", "a_log", "dt_bias", "o_norm_g", "q_norm_g",
               "k_norm_g", "norm2_g")


def small_params(l, p, conv_full):
    return {"norm1_g": p["norm1_g"][l][None], "sgu_norm_g": p["sgu_norm_g"][l][:, None, :], "w_spatial": p["w_spatial"][l],
            "b_spatial": p["b_spatial"][l][..., None], "conv_w": conv_full[l], "a_log": p["a_log"][l], "dt_bias": p["dt_bias"][l],
            "o_norm_g": p["o_norm_g"][l][None], "q_norm_g": p["q_norm_g"][l][None], "k_norm_g": p["k_norm_g"][l][None],
            "norm2_g": p["norm2_g"][l][None]}


_PACK_TILE = 8 * HD


def _pack(arrays):
    flat = jnp.concatenate([a.reshape(-1) for a in arrays])
    pad = -flat.shape[0] % _PACK_TILE
    return jnp.pad(flat, (0, pad)).reshape(-1, HD)


def _unpack(packed, shapes):
    flat, out, off = packed.reshape(-1), [], 0
    for shp in shapes:
        n = int(np.prod(shp))
        out.append(flat[off:off + n].reshape(shp))
        off += n
    return out


BIG_NAMES = ("in", "out", "gu", "down")
WEIGHT_ORDER = ("norm1_g", "w_in", "sgu_norm_g", "w_spatial", "b_spatial", "conv_w", "a_log", "dt_bias", "o_norm_g", "q_norm_g",
                "k_norm_g", "w_out", "norm2_g", "w_gate_up", "w_down")


def kernel(x, norm1_g, w_in, sgu_norm_g, w_spatial, b_spatial, conv_w, a_log, dt_bias, o_norm_g, q_norm_g, k_norm_g, w_out, norm2_g, w_gate_up, w_down, loss_target, m_norm1_g, m_w_in, m_sgu_norm_g, m_w_spatial, m_b_spatial, m_conv_w, m_a_log, m_dt_bias, m_o_norm_g, m_q_norm_g, m_k_norm_g, m_w_out, m_norm2_g, m_w_gate_up, m_w_down, v_norm1_g, v_w_in, v_sgu_norm_g, v_w_spatial, v_b_spatial, v_conv_w, v_a_log, v_dt_bias, v_o_norm_g, v_q_norm_g, v_k_norm_g, v_w_out, v_norm2_g, v_w_gate_up, v_w_down):
    w = dict(norm1_g=norm1_g, w_in=w_in, sgu_norm_g=sgu_norm_g, w_spatial=w_spatial, b_spatial=b_spatial, conv_w=conv_w,
             a_log=a_log, dt_bias=dt_bias, o_norm_g=o_norm_g, q_norm_g=q_norm_g, k_norm_g=k_norm_g, w_out=w_out,
             norm2_g=norm2_g, w_gate_up=w_gate_up, w_down=w_down)
    m = dict(norm1_g=m_norm1_g, w_in=m_w_in, sgu_norm_g=m_sgu_norm_g, w_spatial=m_w_spatial, b_spatial=m_b_spatial,
             conv_w=m_conv_w, a_log=m_a_log, dt_bias=m_dt_bias, o_norm_g=m_o_norm_g, q_norm_g=m_q_norm_g, k_norm_g=m_k_norm_g,
             w_out=m_w_out, norm2_g=m_norm2_g, w_gate_up=m_w_gate_up, w_down=m_w_down)
    v = dict(norm1_g=v_norm1_g, w_in=v_w_in, sgu_norm_g=v_sgu_norm_g, w_spatial=v_w_spatial, b_spatial=v_b_spatial,
             conv_w=v_conv_w, a_log=v_a_log, dt_bias=v_dt_bias, o_norm_g=v_o_norm_g, q_norm_g=v_q_norm_g, k_norm_g=v_k_norm_g,
             w_out=v_w_out, norm2_g=v_norm2_g, w_gate_up=v_w_gate_up, w_down=v_w_down)
    chip = (2 * lax.axis_index("x") + lax.axis_index("y")).astype(jnp.int32)
    core = lax.axis_index("c").astype(jnp.int32)

    in_pad = IN_SHARD_PAD - IN_SHARD
    w_in_pad = jnp.pad(w_in, ((0, 0), (0, 0), (0, in_pad)))

    halves_of = lambda a: a.reshape(2, a.shape[0] // 2, a.shape[1])
    start_0, wait_0 = gather_direct([halves_of(w_in_pad[0].astype(BF16)), halves_of(conv_w[0])], "mix0")
    state_0, token_0 = start_0()
    bf_halves = lambda a: halves_of((a + token_0[0, 0]).astype(BF16))

    def ffn_shards(l):
        return [bf_halves(w_gate_up[l]), bf_halves(w_down[l]), bf_halves(w_out[l])]

    def mixer_shards(l):
        return [bf_halves(w_in_pad[l]), halves_of(conv_w[l])]

    def mixer_weights(g):
        g_in, g_conv = g
        return (shards_to_segments(g_in.reshape(N_CHIPS, D, IN_SHARD_PAD))[None],
                g_conv.reshape(N_CHIPS, B_CONV, -1).transpose(1, 0, 2).reshape(B_CONV, 3 * B_WIDTH))

    def ffn_weights(g, w_in_seg):
        g_gu, g_down, g_out = g
        return {"in": w_in_seg, "out": g_out.reshape(1, D, D), "gu": g_gu.reshape(1, N_CHIPS, D, GU_SHARD),
                "down": None if g_down is None else g_down.reshape(1, FFN, D)}

    def layer_params(l, conv_full):
        return small_params(0, {n: w[n][l:l + 1] for n in SMALL_NAMES if n != "conv_w"}, conv_full[None])

    gu0, down0, out0 = ffn_shards(0)
    start_a, wait_a = gather_direct([gu0, out0], "ffn0")
    start_b, wait_b = gather_direct([down0] + mixer_shards(1), "mid")
    start_c, wait_c = gather_direct(ffn_shards(1), "ffn1")
    state_a, token_a = start_a(token_0)
    state_b, token_b = start_b(token_a)
    state_c, token_c = start_c(token_b)
    h1_0 = rmsnorm_fwd(x[0], norm1_g[0][None], token_c)
    w_in0, conv0 = mixer_weights(pass_to_sibling(wait_0(state_0, h1_0)[1]))
    sps = [layer_params(0, conv0), None]
    mix0, saved_m0 = mixers_forward(x[0], w_in0, sps[0], h1=h1_0)
    g_gu0, g_out0 = pass_to_sibling(wait_a(state_a, mix0)[1])
    wg0 = ffn_weights((g_gu0, None, g_out0), w_in0)
    x1_0, h2_0, gu_0, act_0 = ffn_up(x[0], mix0, wg0["out"], wg0["gu"], sps[0])
    g_down0, g_in1, g_conv1 = pass_to_sibling(wait_b(state_b, act_0)[1])
    wg0["down"] = g_down0.reshape(1, FFN, D)
    x1 = mm_down(act_0, wg0["down"], 0, x1_0)
    saved_f0 = (x1_0, h2_0, gu_0, act_0)
    w_in1, conv1 = mixer_weights((g_in1, g_conv1))
    sps[1] = layer_params(1, conv1)
    mix1, saved_m1 = mixers_forward(x1, w_in1, sps[1])
    wg1 = ffn_weights(pass_to_sibling(wait_c(state_c, mix1)[1]), w_in1)
    x2, saved_f1 = ffn_forward(x1, mix1, wg1, sps[1])
    saved1 = (saved_m1, saved_f1)
    dx, loss_tile = loss_and_grad(x2, loss_target[0])

    def to_chip_parts(grads):
        return [add_sibling(g, r, core.reshape(1)) for g, r in zip(grads, exchange_halves(grads))]

    def start_scatter(grads, tag, dep=None):
        start, wait = scatter_direct(to_chip_parts(grads), tag)
        state, token = start(dep)
        return functools.partial(wait, state), token

    smalls = [None] * DEPTH
    by_chip_out = lambda t: t.reshape(N_CHIPS, OUT_SHARD, D)
    by_chip_down = lambda t: t.reshape(N_CHIPS, DOWN_SHARD, D)
    dx1, dnorm2_1, dw_gu1, dw_down1 = ffn_backward(dx, wg1, sps[1], saved1[1])
    wait_f1, tok_f1 = start_scatter([dw_gu1, by_chip_down(dw_down1)], "ffn1")
    dx, dw_in1, dw_out1, small1 = mixers_backward(dx1, wg1, sps[1], saved1[0], dep=tok_f1)
    smalls[1] = {**small1, "norm2_g": dnorm2_1}
    wait_m1, tok_m1 = start_scatter([dw_in1, by_chip_out(dw_out1)], "mix1")
    dx1, dnorm2_0, dw_gu0, dw_down0 = ffn_backward(dx, wg0, sps[0], saved_f0, dep=tok_m1)
    wait_f0, tok_f0 = start_scatter([dw_gu0, by_chip_down(dw_down0)], "ffn0")
    dx, dw_in0, dw_out0, small0 = mixers_backward(dx1, wg0, sps[0], saved_m0, dep=tok_f0)
    smalls[0] = {**small0, "norm2_g": dnorm2_0}

    grad, delta, new_m, new_v = {}, {}, {}, {}
    stacked = [jnp.stack([smalls[l][n] for l in range(DEPTH)]) for n in SMALL_NAMES]
    total = allreduce_small(_pack(stacked + [loss_tile[0, :1]]))
    shapes = [(DEPTH, B_CONV, 3 * B_WIDTH) if n == "conv_w" else w[n].shape for n in SMALL_NAMES]
    small_grads = dict(zip(SMALL_NAMES, _unpack(total, shapes + [(1,)])[:-1]))
    loss = _unpack(total, shapes + [(1,)])[-1][0]
    conv_cols = conv_w.shape[-1]
    small_grads["conv_w"] = lax.dynamic_slice_in_dim(small_grads["conv_w"], chip * conv_cols, conv_cols, axis=2)
    grad.update(small_grads)
    sshapes = [w[n].shape for n in SMALL_NAMES]
    packed = [_pack([d[n] for n in SMALL_NAMES]) for d in (w, grad, m, v)]
    for dst, t in zip((delta, new_m, new_v), adamw(*packed, "adamw_small")):
        dst.update(zip(SMALL_NAMES, _unpack(t, sshapes)))

    wait_m0, tok_m0 = start_scatter([dw_in0, by_chip_out(dw_out0)], "mix0", dep=total)
    (pf0, rf0), (pm1, rm1), (pf1, rf1) = (wt(tok_m0) for wt in (wait_f0, wait_m1, wait_f1))

    def reduce_group(parts, from_chips):
        mine = [add_chips(p, r, chip.reshape(1)) for p, r in zip(parts, from_chips)]
        return mine, list(share_halves(mine))

    mine_f, theirs_f = reduce_group(pf0 + pf1 + pm1, rf0 + rf1 + rm1)
    for a, n in enumerate(("w_gate_up", "w_down")):
        grad[n], delta[n], new_m[n], new_v[n] = adamw_shard(w[n], m[n], v[n], [mine_f[a], mine_f[2 + a]],
                                                            [theirs_f[a], theirs_f[2 + a]], core.reshape(1), "adamw_" + n)

    mine_m0, theirs_m0 = reduce_group(*wait_m0(new_v["w_down"]))
    tr_ = lambda t: jnp.swapaxes(t, -1, -2)
    cut = lambda t: tr_(t[:, :IN_SHARD])
    res = adamw_shard_t(tr_(w_in), tr_(m_w_in), tr_(v_w_in), [cut(mine_m0[0]), cut(mine_f[4])],
                        [cut(theirs_m0[0]), cut(theirs_f[4])], core.reshape(1), "adamw_w_in")
    grad["w_in"], delta["w_in"], new_m["w_in"], new_v["w_in"] = (tr_(t) for t in res)
    grad["w_out"], delta["w_out"], new_m["w_out"], new_v["w_out"] = adamw_shard(
        w_out, m_w_out, v_w_out, [mine_m0[1], mine_f[5]], [theirs_m0[1], theirs_f[5]], core.reshape(1), "adamw_w_out")

    out = [loss, dx[None]]
    for d in (grad, delta, new_m, new_v):
        out += [d[n] for n in WEIGHT_ORDER]
    return tuple(out)
```

```python
import functools
import math

import numpy as np
import jax
import jax.numpy as jnp
from jax import lax
from jax.experimental import pallas as pl
from jax.experimental.pallas import tpu as pltpu

F32 = jnp.float32
BF16 = jnp.bfloat16
HI = lax.Precision.HIGH

T = 2048
D = 2048
DEPTH = 2
HD = 128
A_GROUPS, A_WIDTH, A_CHUNK = 4, 512, 128
B_HEADS, B_WIDTH, B_CONV, B_CHUNK = 6, 768, 4, 64
C_HEADS, C_WIDTH, C_BLOCK = 6, 768, 128
C_BRANCHES = ((128, 1), (512, 4), (2048, 16))
FFN = 5632
IN_TOTAL = 6412
EPS = 1e-6
N_CHIPS = 4
N_DEV = 8
IN_SHARD = IN_TOTAL // N_CHIPS
IN_SHARD_PAD = 1664
GU_SHARD = 2 * FFN // N_CHIPS
OUT_SHARD = D // N_CHIPS
DOWN_SHARD = FFN // N_CHIPS
P_AU, P_AV, P_BQ, P_BK, P_BV, P_BG, P_BB, P_CQ, P_CK, P_CV, P_END = (
    0, 512, 1024, 1792, 2560, 3328, 4096, 4224, 4992, 5760, 6528)
GATE_COLS = 4108
VMEM_LIMIT = 56 * 1024 * 1024

ADAM_LR, ADAM_B1, ADAM_B2, ADAM_EPS, ADAM_WD, ADAM_STEP = 0.001, 0.9, 0.999, 1e-08, 0.01, 10


def _cparams(sem, vmem=VMEM_LIMIT):
    return pltpu.CompilerParams(dimension_semantics=sem, vmem_limit_bytes=vmem)


def _dims(nd, ta, tb):
    off = nd - 2
    ca = off + (0 if ta else 1)
    cb = off + (1 if tb else 0)
    batch = ((0,), (0,)) if nd == 3 else ((), ())
    return (((ca,), (cb,)), batch)


def _raw_mm(a, b, ta, tb, hi):
    if hi:
        return lax.dot_general(a, b, _dims(a.ndim, ta, tb), precision=HI, preferred_element_type=F32)
    return lax.dot_general(a.astype(BF16), b.astype(BF16), _dims(a.ndim, ta, tb), preferred_element_type=F32)


@functools.partial(jax.custom_vjp, nondiff_argnums=(2, 3, 4))
def _mm(a, b, ta=False, tb=False, hi=False):
    return _raw_mm(a, b, ta, tb, hi)


def _mm_fwd(a, b, ta, tb, hi):
    return _raw_mm(a, b, ta, tb, hi), (a, b)


def _mm_bwd(ta, tb, hi, res, g):
    a, b = res
    da = _raw_mm(g, b, False, not tb, False) if not ta else _raw_mm(b, g, tb, True, False)
    db = _raw_mm(a, g, not ta, False, False) if not tb else _raw_mm(g, a, True, ta, False)
    return da.astype(a.dtype), db.astype(b.dtype)


_mm.defvjp(_mm_fwd, _mm_bwd)


def _rms(x, g):
    return x * lax.rsqrt(jnp.mean(x * x, axis=-1, keepdims=True) + EPS) * g


def _gelu(x):
    return 0.5 * x * (1.0 + jnp.tanh(math.sqrt(2.0 / math.pi) * (x + 0.044715 * (x * x * x))))


def _sigmoid(x):
    return 1.0 / (1.0 + jnp.exp(-x))


def _silu(x):
    return x * _sigmoid(x)


def _softplus(x):
    return jnp.maximum(x, 0.0) + jnp.log(1.0 + jnp.exp(-jnp.abs(x)))


def _iota(shape, dim):
    return lax.broadcasted_iota(jnp.int32, shape, dim)


def _sgu_fn(u, v, sg, w, b):
    nc = T // A_CHUNK
    ug = _gelu(u)
    vn = _rms(_gelu(v), sg)
    causal = _iota((A_CHUNK, A_CHUNK), 0) >= _iota((A_CHUNK, A_CHUNK), 1)
    wm = jnp.where(causal, w, 0.0)
    wb = jnp.broadcast_to(wm[None], (nc, A_CHUNK, A_CHUNK))
    z = _mm(wb, vn.reshape(nc, A_CHUNK, HD)) + b[None]
    return ug * z.reshape(T, HD)


def _sgu_specs():
    col = lambda off: pl.BlockSpec((T, HD), lambda g, off=off: (0, off + g))
    par = [pl.BlockSpec((None, 1, HD), lambda g: (g, 0, 0)),
           pl.BlockSpec((None, A_CHUNK, A_CHUNK), lambda g: (g, 0, 0)),
           pl.BlockSpec((None, A_CHUNK, 1), lambda g: (g, 0, 0))]
    return col, par


def sgu_fwd(p2, sg, w, b):
    col, par = _sgu_specs()

    def body(u_ref, v_ref, sg_ref, w_ref, b_ref, y_ref):
        y_ref[...] = _sgu_fn(u_ref[...], v_ref[...], sg_ref[...], w_ref[...], b_ref[...]).astype(BF16)

    return pl.pallas_call(
        body, name="sgu_fwd", grid=(A_GROUPS,),
        in_specs=[col(P_AU // HD), col(P_AV // HD)] + par,
        out_specs=pl.BlockSpec((T, HD), lambda g: (0, g)),
        out_shape=jax.ShapeDtypeStruct((T, A_WIDTH), BF16),
        compiler_params=_cparams(("arbitrary",)),
    )(p2, p2, sg, w, b)


def sgu_bwd(p2, sg, w, b, dmix):
    col, par = _sgu_specs()

    def body(u_ref, v_ref, sg_ref, w_ref, b_ref, dy_ref, du_ref, dv_ref, dsg_ref, dw_ref, db_ref):
        _, vjp = jax.vjp(_sgu_fn, u_ref[...], v_ref[...], sg_ref[...], w_ref[...], b_ref[...])
        du, dv, dsg, dw, db = vjp(dy_ref[...])
        du_ref[...] = du.astype(BF16)
        dv_ref[...] = dv.astype(BF16)
        dsg_ref[...] = dsg
        dw_ref[...] = dw
        db_ref[...] = db

    gcol = pl.BlockSpec((T, HD), lambda g: (0, g))
    return pl.pallas_call(
        body, name="sgu_bwd", grid=(A_GROUPS,),
        in_specs=[col(P_AU // HD), col(P_AV // HD)] + par + [gcol],
        out_specs=[gcol, gcol] + par,
        out_shape=[jax.ShapeDtypeStruct((T, A_WIDTH), BF16), jax.ShapeDtypeStruct((T, A_WIDTH), BF16),
                   jax.ShapeDtypeStruct((A_GROUPS, 1, HD), F32), jax.ShapeDtypeStruct((A_GROUPS, A_CHUNK, A_CHUNK), F32),
                   jax.ShapeDtypeStruct((A_GROUPS, A_CHUNK, 1), F32)],
        compiler_params=_cparams(("arbitrary",)),
    )(p2, p2, sg, w, b, dmix)


def _attn_fn(q, k, v, qg, kg, slope, *, dil, nb):
    n = T // C_BLOCK
    qb = _rms(q, qg).reshape(n, C_BLOCK, HD)
    kb = _rms(k, kg).reshape(n, C_BLOCK, HD)
    vb = v.reshape(n, C_BLOCK, HD)
    scale = HD ** -0.5
    qi = _iota((n, C_BLOCK, C_BLOCK), 1)
    kj = _iota((n, C_BLOCK, C_BLOCK), 2)
    sl = slope[None] * float(dil)
    d_cur = qi - kj
    sc = jnp.where(d_cur >= 0, _mm(qb, kb, tb=True) * scale - sl * d_cur.astype(F32), -jnp.inf)
    mx = jnp.max(sc, axis=-1, keepdims=True)
    if nb > 1:
        kp = jnp.concatenate([jnp.zeros((1, C_BLOCK, HD), F32), kb[:-1]], axis=0)
        vp = jnp.concatenate([jnp.zeros((1, C_BLOCK, HD), F32), vb[:-1]], axis=0)
        has_prev = (_iota((n, C_BLOCK, C_BLOCK), 0) % nb) > 0
        d_prev = C_BLOCK + qi - kj
        sp = jnp.where((kj >= qi) & has_prev, _mm(qb, kp, tb=True) * scale - sl * d_prev.astype(F32), -jnp.inf)
        mx = jnp.maximum(mx, jnp.max(sp, axis=-1, keepdims=True))
    p = jnp.exp(sc - mx)
    den = jnp.sum(p, axis=-1, keepdims=True)
    if nb > 1:
        pp = jnp.exp(sp - mx)
        den = den + jnp.sum(pp, axis=-1, keepdims=True)
    out = _mm(p / den, vb)
    if nb > 1:
        out = out + _mm(pp / den, vp)
    lse = mx + jnp.log(den)
    return out.reshape(T, HD), jnp.broadcast_to(lse, (n, C_BLOCK, HD)).reshape(T, HD)


def _combine_fn(o1, o2, o3, l1, l2, l3):
    mx = jnp.maximum(jnp.maximum(l1, l2), l3)
    e1, e2, e3 = jnp.exp(l1 - mx), jnp.exp(l2 - mx), jnp.exp(l3 - mx)
    s = e1 + e2 + e3
    return (e1 / s) * o1 + (e2 / s) * o2 + (e3 / s) * o3


def _branch_blocks(dil):
    return -(-(T // dil) // C_BLOCK)


def _load_branch_order(ref, dil):
    if dil == 1:
        return ref[...]
    seg = T // dil
    return jnp.concatenate([ref[pl.ds(r, seg, stride=dil), :] for r in range(dil)], axis=0)


def _store_position_order(ref, val, dil, add=False):
    seg = T // dil
    for r in range(dil):
        rows = slice(None) if dil == 1 else pl.ds(r, seg, stride=dil)
        piece = val if dil == 1 else val[r * seg:(r + 1) * seg]
        if add:
            ref[rows, :] += piece
        else:
            ref[rows, :] = piece


def _dattn_specs():
    col = lambda off: pl.BlockSpec((T, HD), lambda h, off=off: (0, off // HD + h))
    row = pl.BlockSpec((1, HD), lambda h: (0, 0))
    slope = pl.BlockSpec((None, 1, HD), lambda h: (h, 0, 0))
    return [col(P_CQ), col(P_CK), col(P_CV), row, row, slope]


def _dattn_branches(q_ref, k_ref, v_ref, qg, kg, slope, o_scr, l_scr):
    for b, (_, dil) in enumerate(C_BRANCHES):
        q, k, v = (_load_branch_order(r, dil) for r in (q_ref, k_ref, v_ref))
        o, l = _attn_fn(q, k, v, qg, kg, slope, dil=dil, nb=_branch_blocks(dil))
        _store_position_order(o_scr.at[b], o, dil)
        _store_position_order(l_scr.at[b], l, dil)


def dattn_fwd(p2, qg, kg, slopes):
    per_branch = pl.BlockSpec((3, T, HD), lambda h: (0, 0, h))

    def body(q_ref, k_ref, v_ref, qg_ref, kg_ref, s_ref, y_ref, o_ref, l_ref):
        _dattn_branches(q_ref, k_ref, v_ref, qg_ref[...], kg_ref[...], s_ref[...], o_ref, l_ref)
        y_ref[...] = _combine_fn(o_ref[0], o_ref[1], o_ref[2], l_ref[0], l_ref[1], l_ref[2]).astype(BF16)

    return pl.pallas_call(
        body, name="dattn_fwd", grid=(C_HEADS,), in_specs=_dattn_specs(),
        out_specs=[pl.BlockSpec((T, HD), lambda h: (0, h)), per_branch, per_branch],
        out_shape=[jax.ShapeDtypeStruct((T, C_WIDTH), BF16)] + [jax.ShapeDtypeStruct((3, T, C_WIDTH), F32)] * 2,
        compiler_params=_cparams(("arbitrary",)),
    )(p2, p2, p2, qg, kg, slopes)


def dattn_bwd(p2, qg, kg, slopes, outs, lses, dmix):
    hcol = pl.BlockSpec((T, HD), lambda h: (0, h))
    row = pl.BlockSpec((1, HD), lambda h: (0, 0))
    dy = pl.BlockSpec((T, HD), lambda h: (0, (A_WIDTH + B_WIDTH) // HD + h))
    per_branch = pl.BlockSpec((3, T, HD), lambda h: (0, 0, h))

    def body(q_ref, k_ref, v_ref, qg_ref, kg_ref, s_ref, o_scr, l_scr, dy_ref, dq_ref, dk_ref, dv_ref, dqg_ref, dkg_ref, g_scr,
             acc):
        qg, kg, slope = qg_ref[...], kg_ref[...], s_ref[...]
        _, vjp = jax.vjp(_combine_fn, o_scr[0], o_scr[1], o_scr[2], l_scr[0], l_scr[1], l_scr[2])
        for i, g in enumerate(vjp(dy_ref[...])):
            g_scr[i] = g

        @pl.when(pl.program_id(0) == 0)
        def _():
            dqg_ref[...] = jnp.zeros_like(dqg_ref)
            dkg_ref[...] = jnp.zeros_like(dkg_ref)

        for b, (_, dil) in enumerate(C_BRANCHES):
            q, k, v = (_load_branch_order(r, dil) for r in (q_ref, k_ref, v_ref))
            do, dl = _load_branch_order(g_scr.at[b], dil), _load_branch_order(g_scr.at[3 + b], dil)
            fn = functools.partial(_attn_fn, dil=dil, nb=_branch_blocks(dil))
            _, vjp_b = jax.vjp(lambda a, b_, c, d, e, fn=fn: fn(a, b_, c, d, e, slope), q, k, v, qg, kg)
            dq, dk, dv, dqg, dkg = vjp_b((do, dl))
            for i, val in enumerate((dq, dk, dv)):
                _store_position_order(acc.at[i], val, dil, add=b > 0)
            dqg_ref[...] += dqg
            dkg_ref[...] += dkg
        for i, ref in enumerate((dq_ref, dk_ref, dv_ref)):
            ref[...] = acc[i].astype(BF16)

    scr = lambda n: pltpu.VMEM((n, T, HD), F32)
    return pl.pallas_call(
        body, name="dattn_bwd", grid=(C_HEADS,), in_specs=_dattn_specs() + [per_branch, per_branch, dy],
        out_specs=[hcol, hcol, hcol, row, row],
        out_shape=[jax.ShapeDtypeStruct((T, C_WIDTH), BF16)] * 3 + [jax.ShapeDtypeStruct((1, HD), F32)] * 2,
        scratch_shapes=[scr(6), scr(3)], compiler_params=_cparams(("arbitrary",)),
    )(p2, p2, p2, qg, kg, slopes, outs, lses, dmix)


_NCH = T // B_CHUNK


def _conv_taps(x, w_ref):
    rows = _iota(x.shape, 0)
    taps = []
    for j in range(B_CONV):
        s = B_CONV - 1 - j
        taps.append(x if s == 0 else jnp.where(rows >= s, pltpu.roll(x, s, 0), 0.0))
    pre = sum(w_ref[j:j + 1, :] * taps[j] for j in range(B_CONV))
    return pre, taps


def _conv_post(pre, mode):
    y = _silu(pre)
    if mode == "v":
        return y
    y = y * lax.rsqrt(jnp.sum(y * y, axis=-1, keepdims=True) + EPS)
    return y * (HD ** -0.5) if mode == "q" else y


def conv_fwd(p2, conv_w, mode):
    idx = "qkv".index(mode)
    xcol = pl.BlockSpec((T, HD), lambda h: (0, P_BQ // HD + B_HEADS * idx + h))
    wcol = pl.BlockSpec((B_CONV, HD), lambda h: (0, B_HEADS * idx + h))
    hcol = pl.BlockSpec((T, HD), lambda h: (0, h))

    def body(x_ref, w_ref, y_ref):
        pre, _ = _conv_taps(x_ref[...], w_ref)
        y_ref[...] = _conv_post(pre, mode)

    return pl.pallas_call(
        body, name=f"conv_fwd_{mode}", grid=(B_HEADS,), in_specs=[xcol, wcol], out_specs=hcol,
        out_shape=jax.ShapeDtypeStruct((T, B_WIDTH), F32), compiler_params=_cparams(("arbitrary",)),
    )(p2, conv_w)


def conv_bwd(p2, conv_w, dys, mode):
    idx = "qkv".index(mode)
    xcol = pl.BlockSpec((T, HD), lambda h: (0, P_BQ // HD + B_HEADS * idx + h))
    wcol = pl.BlockSpec((B_CONV, HD), lambda h: (0, B_HEADS * idx + h))
    hcol = pl.BlockSpec((T, HD), lambda h: (0, h))
    wout = pl.BlockSpec((B_CONV, HD), lambda h: (0, h))

    def body(x_ref, w_ref, *rest):
        dy_refs, (dx_ref, dw_ref) = rest[:-2], rest[-2:]
        pre, taps = _conv_taps(x_ref[...], w_ref)
        _, vjp = jax.vjp(functools.partial(_conv_post, mode=mode), pre)
        (dpre,) = vjp(sum(r[...] for r in dy_refs))
        rows = _iota(dpre.shape, 0)
        dx = w_ref[B_CONV - 1:B_CONV, :] * dpre
        for j in range(B_CONV):
            s = B_CONV - 1 - j
            dw_ref[j:j + 1, :] = jnp.sum(dpre * taps[j], axis=0, keepdims=True)
            if s > 0:
                dx = dx + w_ref[j:j + 1, :] * jnp.where(rows < T - s, pltpu.roll(dpre, T - s, 0), 0.0)
        dx_ref[...] = dx.astype(BF16)

    return pl.pallas_call(
        body, name=f"conv_bwd_{mode}", grid=(B_HEADS,), in_specs=[xcol, wcol] + [hcol] * len(dys), out_specs=[hcol, wout],
        out_shape=[jax.ShapeDtypeStruct((T, B_WIDTH), BF16), jax.ShapeDtypeStruct((B_CONV, B_WIDTH), F32)],
        compiler_params=_cparams(("arbitrary",)),
    )(p2, conv_w, *dys)


def _gates_fn(bg, al, dtb, h):
    r = _iota((HD, HD), 0)
    logit = _mm(bg, (r == h).astype(F32), hi=True)
    a = _mm(bg, (r == h + B_HEADS).astype(F32), hi=True)
    beta = _sigmoid(logit)
    graw = -jnp.exp(al) * _softplus(a + dtb)
    tri = (_iota((_NCH, B_CHUNK, B_CHUNK), 1) >= _iota((_NCH, B_CHUNK, B_CHUNK), 2)).astype(F32)
    g = _mm(tri, graw.reshape(_NCH, B_CHUNK, HD), hi=True).reshape(T, HD)
    return beta, g


def _gates_specs():
    bg = pl.BlockSpec((T, HD), lambda h: (0, P_BB // HD))
    par = pl.BlockSpec((None, 1, HD), lambda h: (h, 0, 0))
    out = pl.BlockSpec((None, T, HD), lambda h: (h, 0, 0))
    return bg, par, out


def gates_fwd(p2, al, dtb):
    bg, par, out = _gates_specs()

    def body(bg_ref, al_ref, dtb_ref, beta_ref, g_ref):
        beta, g = _gates_fn(bg_ref[...], al_ref[...], dtb_ref[...], pl.program_id(0))
        beta_ref[...] = beta
        g_ref[...] = g

    return pl.pallas_call(
        body, name="gates_fwd", grid=(B_HEADS,), in_specs=[bg, par, par], out_specs=[out, out],
        out_shape=[jax.ShapeDtypeStruct((B_HEADS, T, HD), F32)] * 2, compiler_params=_cparams(("arbitrary",)),
    )(p2, al, dtb)


def gates_bwd(p2, al, dtb, dbeta, dg1, dg2):
    bg, par, out = _gates_specs()
    acc = pl.BlockSpec((T, HD), lambda h: (0, 0))

    def body(bg_ref, al_ref, dtb_ref, dbeta_ref, dg1_ref, dg2_ref, dbg_ref, dal_ref, ddtb_ref, acc_ref):
        h = pl.program_id(0)
        _, vjp = jax.vjp(lambda a, b, c: _gates_fn(a, b, c, h), bg_ref[...], al_ref[...], dtb_ref[...])
        dbg, dal, ddtb = vjp((dbeta_ref[...], dg1_ref[...] + dg2_ref[...]))

        @pl.when(h == 0)
        def _():
            acc_ref[...] = jnp.zeros_like(acc_ref)

        acc_ref[...] += dbg
        dbg_ref[...] = acc_ref[...].astype(BF16)
        dal_ref[...] = jnp.broadcast_to(jnp.sum(dal, axis=-1, keepdims=True), (1, HD))
        ddtb_ref[...] = jnp.broadcast_to(jnp.sum(ddtb, axis=-1, keepdims=True), (1, HD))

    return pl.pallas_call(
        body, name="gates_bwd", grid=(B_HEADS,), in_specs=[bg, par, par, out, out, out], out_specs=[acc, par, par],
        out_shape=[jax.ShapeDtypeStruct((T, HD), BF16)] + [jax.ShapeDtypeStruct((B_HEADS, 1, HD), F32)] * 2,
        scratch_shapes=[pltpu.VMEM((T, HD), F32)], compiler_params=_cparams(("arbitrary",)),
    )(p2, al, dtb, dbeta, dg1, dg2)


def _unit_lower_inverse(a):
    eye = (_iota(a.shape, 1) == _iota(a.shape, 2)).astype(F32)
    x = eye - a
    p = _mm(a, a, hi=True)
    for i in range(5):
        x = x + _mm(x, p, hi=True)
        if i < 4:
            p = _mm(p, p, hi=True)
    return x


_WY_CH = 16
_WY_ROWS = _WY_CH * B_CHUNK


def _wy_fn(q, k, v, beta, g):
    sh = (q.shape[0] // B_CHUNK, B_CHUNK, HD)
    q3, k3, v3, b3, g3 = (t.reshape(sh) for t in (q, k, v, beta, g))
    gd = g3[:, :, :B_CHUNK] - jnp.swapaxes(g3, 1, 2)[:, :B_CHUNK, :]
    ii, jj = _iota(gd.shape, 1), _iota(gd.shape, 2)
    decay = jnp.exp(jnp.where(ii >= jj, gd, -jnp.inf))
    kb = k3 * b3
    a = _mm(kb, k3, tb=True) * jnp.where(ii > jj, decay, 0.0)
    tinv = _unit_lower_inverse(a)
    u = _mm(tinv, v3 * b3, hi=True)
    w = _mm(tinv, kb * jnp.exp(g3), hi=True)
    attn = _mm(q3, k3, tb=True) * decay
    return u.reshape(q.shape), w.reshape(q.shape), attn


def _wy_specs():
    hcol = pl.BlockSpec((_WY_ROWS, HD), lambda h, i: (i, h))
    hb = pl.BlockSpec((None, _WY_ROWS, HD), lambda h, i: (h, i, 0))
    at = pl.BlockSpec((None, _WY_CH, B_CHUNK, B_CHUNK), lambda h, i: (h, i, 0, 0))
    return hcol, hb, at


_WY_GRID = (B_HEADS, _NCH // _WY_CH)


def wy_fwd(q, k, v, beta, g):
    hcol, hb, at = _wy_specs()

    def body(q_ref, k_ref, v_ref, b_ref, g_ref, u_ref, w_ref, a_ref):
        u, w, a = _wy_fn(q_ref[...], k_ref[...], v_ref[...], b_ref[...], g_ref[...])
        u_ref[...] = u
        w_ref[...] = w
        a_ref[...] = a

    return pl.pallas_call(
        body, name="wy_fwd", grid=_WY_GRID, in_specs=[hcol, hcol, hcol, hb, hb], out_specs=[hcol, hcol, at],
        out_shape=[jax.ShapeDtypeStruct((T, B_WIDTH), F32)] * 2 + [jax.ShapeDtypeStruct((B_HEADS, _NCH, B_CHUNK, B_CHUNK), F32)],
        compiler_params=_cparams(("arbitrary", "arbitrary")),
    )(q, k, v, beta, g)


def wy_bwd(q, k, v, beta, g, du, dw, dattn):
    hcol, hb, at = _wy_specs()

    def body(q_ref, k_ref, v_ref, b_ref, g_ref, du_ref, dw_ref, da_ref, dq_ref, dk_ref, dv_ref, db_ref, dg_ref):
        _, vjp = jax.vjp(_wy_fn, q_ref[...], k_ref[...], v_ref[...], b_ref[...], g_ref[...])
        for r, t in zip((dq_ref, dk_ref, dv_ref, db_ref, dg_ref), vjp((du_ref[...], dw_ref[...], da_ref[...]))):
            r[...] = t

    return pl.pallas_call(
        body, name="wy_bwd", grid=_WY_GRID, in_specs=[hcol, hcol, hcol, hb, hb, hcol, hcol, at],
        out_specs=[hcol, hcol, hcol, hb, hb],
        out_shape=[jax.ShapeDtypeStruct((T, B_WIDTH), F32)] * 3 + [jax.ShapeDtypeStruct((B_HEADS, T, HD), F32)] * 2,
        compiler_params=_cparams(("arbitrary", "arbitrary")),
    )(q, k, v, beta, g, du, dw, dattn)


def _scan_step_fn(q, k, u, w, g, attn, gate, og, s):
    v_new = u - _mm(w, s)
    o = _mm(q * jnp.exp(g), s) + _mm(attn, v_new)
    g_last = jnp.sum(jnp.where(_iota(g.shape, 0) == B_CHUNK - 1, g, 0.0), axis=0, keepdims=True)
    s_new = s * jnp.exp(g_last) + _mm(k * jnp.exp(g_last - g), v_new, ta=True)
    return _rms(o, og) * _silu(gate), s_new


def _scan_specs(rev):
    ch = (lambda n: _NCH - 1 - n) if rev else (lambda n: n)
    rows = pl.BlockSpec((B_CHUNK, B_WIDTH), lambda n: (ch(n), 0))
    gb = pl.BlockSpec((B_HEADS, B_CHUNK, HD), lambda n: (0, ch(n), 0))
    at = pl.BlockSpec((B_HEADS, None, B_CHUNK, B_CHUNK), lambda n: (0, ch(n), 0, 0))
    og = pl.BlockSpec((1, HD), lambda n: (0, 0))
    st = pl.BlockSpec((None, B_HEADS, HD, HD), lambda n: (ch(n), 0, 0, 0))
    return rows, gb, at, og, st


def scan_fwd(q, k, u, w, g, attn, gate, og):
    rows, gb, at, ogs, st = _scan_specs(False)

    def body(q_ref, k_ref, u_ref, w_ref, g_ref, a_ref, gate_ref, og_ref, y_ref, st_ref, s_ref):
        @pl.when(pl.program_id(0) == 0)
        def _():
            s_ref[...] = jnp.zeros_like(s_ref)

        for h in range(B_HEADS):
            c = slice(h * HD, (h + 1) * HD)
            s = s_ref[h]
            st_ref[h] = s
            y, s_new = _scan_step_fn(q_ref[:, c], k_ref[:, c], u_ref[:, c], w_ref[:, c], g_ref[h], a_ref[h],
                                     gate_ref[:, c], og_ref[...], s)
            y_ref[:, c] = y.astype(BF16)
            s_ref[h] = s_new

    return pl.pallas_call(
        body, name="scan_fwd", grid=(_NCH,), in_specs=[rows, rows, rows, rows, gb, at, rows, ogs], out_specs=[rows, st],
        out_shape=[jax.ShapeDtypeStruct((T, B_WIDTH), BF16), jax.ShapeDtypeStruct((_NCH, B_HEADS, HD, HD), F32)],
        scratch_shapes=[pltpu.VMEM((B_HEADS, HD, HD), F32)], compiler_params=_cparams(("arbitrary",)),
    )(q, k, u, w, g, attn, gate, og)


def scan_bwd(q, k, u, w, g, attn, gate, og, states, dmix):
    rows, gb, at, ogs, st = _scan_specs(True)
    dyb = pl.BlockSpec((B_CHUNK, HD), lambda n: (_NCH - 1 - n, 0))

    def body(q_ref, k_ref, u_ref, w_ref, g_ref, a_ref, gate_ref, og_ref, st_ref, *rest):
        dy_refs, (dq_ref, dk_ref, du_ref, dw_ref, dgate_ref, dg_ref, da_ref, dog_ref, ds_ref) = rest[:B_HEADS], rest[B_HEADS:]

        @pl.when(pl.program_id(0) == 0)
        def _():
            ds_ref[...] = jnp.zeros_like(ds_ref)
            dog_ref[...] = jnp.zeros_like(dog_ref)

        for h in range(B_HEADS):
            c = slice(h * HD, (h + 1) * HD)
            _, vjp = jax.vjp(_scan_step_fn, q_ref[:, c], k_ref[:, c], u_ref[:, c], w_ref[:, c], g_ref[h], a_ref[h],
                             gate_ref[:, c], og_ref[...], st_ref[h])
            dq, dk, du, dw, dg, da, dgate, dog, ds = vjp((dy_refs[h][...], ds_ref[h]))
            dq_ref[:, c] = dq
            dk_ref[:, c] = dk
            du_ref[:, c] = du
            dw_ref[:, c] = dw
            dgate_ref[:, c] = dgate.astype(BF16)
            dg_ref[h] = dg
            da_ref[h] = da
            dog_ref[...] += dog
            ds_ref[h] = ds

    dy_specs = [pl.BlockSpec((B_CHUNK, HD), lambda n, h=h: (_NCH - 1 - n, A_WIDTH // HD + h)) for h in range(B_HEADS)]
    return pl.pallas_call(
        body, name="scan_bwd", grid=(_NCH,),
        in_specs=[rows, rows, rows, rows, gb, at, rows, ogs, st] + dy_specs,
        out_specs=[rows] * 5 + [gb, at, ogs],
        out_shape=[jax.ShapeDtypeStruct((T, B_WIDTH), F32)] * 4 + [jax.ShapeDtypeStruct((T, B_WIDTH), BF16)]
        + [jax.ShapeDtypeStruct((B_HEADS, T, HD), F32), jax.ShapeDtypeStruct((B_HEADS, _NCH, B_CHUNK, B_CHUNK), F32),
           jax.ShapeDtypeStruct((1, HD), F32)],
        scratch_shapes=[pltpu.VMEM((B_HEADS, HD, HD), F32)], compiler_params=_cparams(("arbitrary",)),
    )(q, k, u, w, g, attn, gate, og, states, *([dmix] * B_HEADS))


def _lanes(vec):
    return jnp.broadcast_to(vec[:, None, None], (vec.shape[0], 1, HD))


def gdn_forward(p2, conv_w, a_log, dt_bias, og):
    qa, ka, va = (conv_fwd(p2, conv_w, m) for m in "qkv")
    beta, g = gates_fwd(p2, _lanes(a_log), _lanes(dt_bias))
    u, w, attn = wy_fwd(qa, ka, va, beta, g)
    gate = p2[:, P_BG:P_BB]
    y, states = scan_fwd(qa, ka, u, w, g, attn, gate, og)
    return y, (qa, ka, va, beta, g, u, w, attn, gate, states)


def gdn_backward(p2, conv_w, a_log, dt_bias, og, saved, dmix):
    qa, ka, va, beta, g, u, w, attn, gate, states = saved
    dq1, dk1, du, dw, dgate, dg1, dattn, dog = scan_bwd(qa, ka, u, w, g, attn, gate, og, states, dmix)
    dq2, dk2, dv, dbeta, dg2 = wy_bwd(qa, ka, va, beta, g, du, dw, dattn)
    dbg, dal, ddtb = gates_bwd(p2, _lanes(a_log), _lanes(dt_bias), dbeta, dg1, dg2)
    dxq, dwq = conv_bwd(p2, conv_w, [dq1, dq2], "q")
    dxk, dwk = conv_bwd(p2, conv_w, [dk1, dk2], "k")
    dxv, dwv = conv_bwd(p2, conv_w, [dv], "v")
    return [dxq, dxk, dxv, dgate, dbg], jnp.concatenate([dwq, dwk, dwv], axis=1), dal[:, 0, 0], ddtb[:, 0, 0], dog


_SLOPES = np.exp2(-8.0 * (np.arange(C_HEADS, dtype=np.float64) + 1.0) / C_HEADS).astype(np.float32)


def _alibi_slopes():
    return _lanes(jnp.asarray(_SLOPES))


_ROWS = 512
_TM = 1024
_TM_FFN = 512


def _dep_specs(dep, ngrid):
    if dep is None:
        return [], []
    return [dep], [pl.BlockSpec((8, HD), lambda *_: (0, 0))]


def rmsnorm_fwd(x, g, dep=None):
    blk = pl.BlockSpec((_ROWS, D), lambda i: (i, 0))
    deps, dspecs = _dep_specs(dep, 1)

    def body(x_ref, g_ref, *rest):
        rest[-1][...] = _rms(x_ref[...], g_ref[...]).astype(BF16)

    return pl.pallas_call(
        body, name="rmsnorm_fwd", grid=(T // _ROWS,), in_specs=[blk, pl.BlockSpec((1, D), lambda i: (0, 0))] + dspecs,
        out_specs=blk, out_shape=jax.ShapeDtypeStruct((T, D), BF16), compiler_params=_cparams(("arbitrary",)),
    )(x, g, *deps)


def rmsnorm_bwd(x, g, dh, dres):
    blk = pl.BlockSpec((_ROWS, D), lambda i: (i, 0))
    row = pl.BlockSpec((1, D), lambda i: (0, 0))

    def body(x_ref, g_ref, dh_ref, dres_ref, dx_ref, dg_ref):
        _, vjp = jax.vjp(_rms, x_ref[...], g_ref[...])
        dx, dg = vjp(dh_ref[...])
        dx_ref[...] = dres_ref[...] + dx

        @pl.when(pl.program_id(0) == 0)
        def _():
            dg_ref[...] = jnp.zeros_like(dg_ref)

        dg_ref[...] += dg

    return pl.pallas_call(
        body, name="rmsnorm_bwd", grid=(T // _ROWS,), in_specs=[blk, row, blk, blk], out_specs=[blk, row],
        out_shape=[jax.ShapeDtypeStruct((T, D), F32), jax.ShapeDtypeStruct((1, D), F32)],
        compiler_params=_cparams(("arbitrary",)),
    )(x, g, dh, dres)


def _matmul(name, a, b, *, grid, a_spec, b_spec, o_spec, out_shape, ta=False, tb=False, k_axis=None, res=None, dep=None):
    dims = _dims(2, ta, tb)
    deps, dspecs = _dep_specs(dep, len(grid))

    def body(a_ref, b_ref, *rest):
        o_ref = rest[-1]
        prod = lax.dot_general(a_ref[...].astype(BF16), b_ref[...].astype(BF16), dims, preferred_element_type=F32)
        if res is not None:
            prod = prod + rest[0][...]
        if k_axis is None:
            o_ref[...] = prod.astype(o_ref.dtype)
        else:
            @pl.when(pl.program_id(k_axis) == 0)
            def _():
                o_ref[...] = prod

            @pl.when(pl.program_id(k_axis) > 0)
            def _():
                o_ref[...] += prod

    sem = tuple("arbitrary" for _ in grid)
    ins = [a, b] + ([res] if res is not None else []) + deps
    specs = [a_spec, b_spec] + ([o_spec] if res is not None else []) + dspecs
    return pl.pallas_call(
        body, name=name, grid=grid, in_specs=specs, out_specs=o_spec, out_shape=out_shape, compiler_params=_cparams(sem),
    )(*ins)


_IN_TN = P_END // 3


def mm_proj(h1, wp_in, l):
    return _matmul(
        "mm_proj", h1, wp_in, grid=(P_END // _IN_TN, T // _TM),
        a_spec=pl.BlockSpec((_TM, D), lambda j, i: (i, 0)),
        b_spec=pl.BlockSpec((None, D, _IN_TN), lambda j, i: (l, 0, j)),
        o_spec=pl.BlockSpec((_TM, _IN_TN), lambda j, i: (i, j)), out_shape=jax.ShapeDtypeStruct((T, P_END), F32))


def mm_dh1(dp2, wp_in, l, dep=None):
    return _matmul(
        "mm_dh1", dp2, wp_in, grid=(T // _TM, P_END // _IN_TN), tb=True, k_axis=1, dep=dep,
        a_spec=pl.BlockSpec((_TM, _IN_TN), lambda i, k: (i, k)),
        b_spec=pl.BlockSpec((None, D, _IN_TN), lambda i, k: (l, 0, k)),
        o_spec=pl.BlockSpec((_TM, D), lambda i, k: (i, 0)), out_shape=jax.ShapeDtypeStruct((T, D), F32))


def mm_dwin(h1, dp2):
    return _matmul(
        "mm_dwin", h1, dp2, grid=(P_END // _IN_TN, D // _TM), ta=True,
        a_spec=pl.BlockSpec((T, _TM), lambda j, i: (0, i)),
        b_spec=pl.BlockSpec((T, _IN_TN), lambda j, i: (0, j)),
        o_spec=pl.BlockSpec((_TM, _IN_TN), lambda j, i: (i, j)), out_shape=jax.ShapeDtypeStruct((D, P_END), BF16))


def _mm_square(name, a, w, l, res, tb, dep=None):
    tn = 1024
    b_spec = (pl.BlockSpec((None, tn, D), lambda j, i: (l, j, 0)) if tb else pl.BlockSpec((None, D, tn), lambda j, i: (l, 0, j)))
    return _matmul(
        name, a, w, grid=(D // tn, T // _TM), tb=tb, res=res, dep=dep,
        a_spec=pl.BlockSpec((_TM, D), lambda j, i: (i, 0)), b_spec=b_spec,
        o_spec=pl.BlockSpec((_TM, tn), lambda j, i: (i, j)), out_shape=jax.ShapeDtypeStruct((T, D), F32))


def mm_out(mix, wg_out, l, x):
    return _mm_square("mm_out", mix, wg_out, l, x, False)


def mm_dmix(dx1, wg_out, l, dep=None):
    return _mm_square("mm_dmix", dx1, wg_out, l, None, True, dep)


def mm_dwout(mix, dx1):
    tn = 1024
    return _matmul(
        "mm_dwout", mix, dx1, grid=(D // tn, D // _TM), ta=True,
        a_spec=pl.BlockSpec((T, _TM), lambda j, i: (0, i)), b_spec=pl.BlockSpec((T, tn), lambda j, i: (0, j)),
        o_spec=pl.BlockSpec((_TM, tn), lambda j, i: (i, j)), out_shape=jax.ShapeDtypeStruct((D, D), BF16))


_GU_TN = GU_SHARD // 2


_GU_NJ = FFN // _GU_TN


def mm_dh2(dgu, wg_gu, l, dep=None):
    return _matmul(
        "mm_dh2", dgu, wg_gu, grid=(T // _TM, 2 * N_CHIPS), tb=True, k_axis=1, dep=dep,
        a_spec=pl.BlockSpec((None, _TM, _GU_TN), lambda i, k: (k // _GU_NJ, i, k % _GU_NJ)),
        b_spec=pl.BlockSpec((None, None, D, _GU_TN), lambda i, k: (l, k // 2, 0, k % 2)),
        o_spec=pl.BlockSpec((_TM, D), lambda i, k: (i, 0)), out_shape=jax.ShapeDtypeStruct((T, D), F32))


def mm_dwgu(h2, dgu):
    return _matmul(
        "mm_dwgu", h2, dgu, grid=(N_CHIPS, 2, D // _TM), ta=True,
        a_spec=pl.BlockSpec((T, _TM), lambda s, j, i: (0, i)),
        b_spec=pl.BlockSpec((None, T, _GU_TN), lambda s, j, i: ((2 * s + j) // _GU_NJ, 0, (2 * s + j) % _GU_NJ)),
        o_spec=pl.BlockSpec((None, _TM, _GU_TN), lambda s, j, i: (s, i, j)),
        out_shape=jax.ShapeDtypeStruct((N_CHIPS, D, GU_SHARD), BF16))


def mm_down(act, wg_down, l, x1):
    tn = 512
    return _matmul(
        "mm_down", act, wg_down, grid=(D // tn, T // _TM), res=x1,
        a_spec=pl.BlockSpec((_TM, FFN), lambda j, i: (i, 0)),
        b_spec=pl.BlockSpec((None, FFN, tn), lambda j, i: (l, 0, j)),
        o_spec=pl.BlockSpec((_TM, tn), lambda j, i: (i, j)), out_shape=jax.ShapeDtypeStruct((T, D), F32))


def mm_dwdown(act, dx2):
    tm, tn = DOWN_SHARD, 512
    return _matmul(
        "mm_dwdown", act, dx2, grid=(D // tn, FFN // tm), ta=True,
        a_spec=pl.BlockSpec((T, tm), lambda j, i: (0, i)), b_spec=pl.BlockSpec((T, tn), lambda j, i: (0, j)),
        o_spec=pl.BlockSpec((tm, tn), lambda j, i: (i, j)), out_shape=jax.ShapeDtypeStruct((FFN, D), BF16))


_FF_TN = 1408


def _swiglu_fn(gt, up):
    return _silu(gt) * up


def _gate_up_specs():
    gate = pl.BlockSpec((None, None, D, _FF_TN), lambda j, i: (0, j // 2, 0, j % 2))
    up = pl.BlockSpec((None, None, D, _FF_TN), lambda j, i: (0, N_CHIPS // 2 + j // 2, 0, j % 2))
    both = pl.BlockSpec((2, _TM_FFN, _FF_TN), lambda j, i: (0, i, j))
    return gate, up, both


def mm_gu_swiglu(h2, wg_gu):
    gate, up, both = _gate_up_specs()

    def body(h_ref, wg_ref, wu_ref, gu_ref, act_ref):
        h = h_ref[...]
        gt = jnp.dot(h, wg_ref[...], preferred_element_type=F32)
        u = jnp.dot(h, wu_ref[...], preferred_element_type=F32)
        gu_ref[0] = gt
        gu_ref[1] = u
        act_ref[...] = _swiglu_fn(gt, u).astype(BF16)

    return pl.pallas_call(
        body, name="mm_gu_swiglu", grid=(FFN // _FF_TN, T // _TM_FFN),
        in_specs=[pl.BlockSpec((_TM_FFN, D), lambda j, i: (i, 0)), gate, up],
        out_specs=[both, pl.BlockSpec((_TM_FFN, _FF_TN), lambda j, i: (i, j))],
        out_shape=[jax.ShapeDtypeStruct((2, T, FFN), F32), jax.ShapeDtypeStruct((T, FFN), BF16)],
        compiler_params=_cparams(("arbitrary", "arbitrary")),
    )(h2, wg_gu, wg_gu)


def mm_dact_swiglu(dx2, wg_down, gu, dep=None):
    _, _, both = _gate_up_specs()
    deps, dspecs = _dep_specs(dep, 2)

    def body(dx_ref, w_ref, gu_ref, *rest):
        dact = lax.dot_general(dx_ref[...].astype(BF16), w_ref[...], _dims(2, False, True), preferred_element_type=F32)
        _, vjp = jax.vjp(_swiglu_fn, gu_ref[0], gu_ref[1])
        dgt, dup = vjp(dact)
        rest[-1][0] = dgt.astype(BF16)
        rest[-1][1] = dup.astype(BF16)

    return pl.pallas_call(
        body, name="mm_dact_swiglu", grid=(FFN // _FF_TN, T // _TM_FFN),
        in_specs=[pl.BlockSpec((_TM_FFN, D), lambda j, i: (i, 0)), pl.BlockSpec((None, _FF_TN, D), lambda j, i: (0, j, 0)),
                  both]
        + dspecs,
        out_specs=both, out_shape=jax.ShapeDtypeStruct((2, T, FFN), BF16),
        compiler_params=_cparams(("arbitrary", "arbitrary")),
    )(dx2, wg_down, gu, *deps)


def loss_and_grad(y, target):
    blk = pl.BlockSpec((_ROWS, D), lambda i: (i, 0))
    acc = pl.BlockSpec((8, HD), lambda i: (0, 0))

    def body(y_ref, t_ref, dy_ref, l_ref):
        err = y_ref[...] - t_ref[...]
        dy_ref[...] = err * (1.0 / D)

        @pl.when(pl.program_id(0) == 0)
        def _():
            l_ref[...] = jnp.zeros_like(l_ref)

        l_ref[...] += (0.5 / D) * jnp.sum(err * err)

    return pl.pallas_call(
        body, name="loss_and_grad", grid=(T // _ROWS,), in_specs=[blk, blk], out_specs=[blk, acc],
        out_shape=[jax.ShapeDtypeStruct((T, D), F32), jax.ShapeDtypeStruct((8, HD), F32)],
        compiler_params=_cparams(("arbitrary",)),
    )(y, target)


def adamw(w, g, m, v, name):
    rows, cols = w.shape
    tr = _ROWS if rows % _ROWS == 0 else rows
    blk = pl.BlockSpec((tr, cols), lambda i: (i, 0))

    def body(w_ref, g_ref, m_ref, v_ref, d_ref, nm_ref, nv_ref):
        gg = g_ref[...]
        nm = ADAM_B1 * m_ref[...] + (1.0 - ADAM_B1) * gg
        nv = ADAM_B2 * v_ref[...] + (1.0 - ADAM_B2) * (gg * gg)
        m_hat = nm / (1.0 - ADAM_B1 ** ADAM_STEP)
        v_hat = nv / (1.0 - ADAM_B2 ** ADAM_STEP)
        d_ref[...] = -ADAM_LR * (m_hat / (jnp.sqrt(v_hat) + ADAM_EPS) + ADAM_WD * w_ref[...])
        nm_ref[...] = nm
        nv_ref[...] = nv

    return pl.pallas_call(
        body, name=name, grid=(rows // tr,), in_specs=[blk] * 4, out_specs=[blk] * 3,
        out_shape=[jax.ShapeDtypeStruct(w.shape, F32)] * 3, compiler_params=_cparams(("arbitrary",)),
    )(w, g, m, v)


_LANE = 128


def _segment_of_shard_column():
    flat = np.full(P_END, -1, np.int64)
    for o in range(P_END):
        if GATE_COLS <= o < P_CQ:
            continue
        c = o if o < GATE_COLS else o - (P_CQ - GATE_COLS)
        flat[o] = (c // IN_SHARD) * IN_SHARD_PAD + c % IN_SHARD
    return flat


def _block_pairs(src_of_dst):
    return [sorted({int(c) // _LANE for c in src_of_dst[db * _LANE:(db + 1) * _LANE] if c >= 0})
            for db in range(len(src_of_dst) // _LANE)]


_RELAYOUT_ROWS = 512
_SHARD_BLOCKS = IN_SHARD_PAD // _LANE


def _relayout(name, x, to_segments):
    seg_of = _segment_of_shard_column()
    if to_segments:
        src_of_dst = seg_of
    else:
        src_of_dst = np.full(N_CHIPS * IN_SHARD_PAD, -1, np.int64)
        src_of_dst[seg_of[seg_of >= 0]] = np.nonzero(seg_of >= 0)[0]
    sources = _block_pairs(src_of_dst)
    n_dst = len(sources)
    col_map = jnp.asarray(src_of_dst.reshape(n_dst, 1, _LANE), jnp.int32)
    shard_blk = pl.BlockSpec((N_CHIPS, _RELAYOUT_ROWS, IN_SHARD_PAD), lambda i: (0, i, 0))
    seg_blk = pl.BlockSpec((_RELAYOUT_ROWS, P_END), lambda i: (i, 0))

    def shard_cols(ref, b):
        return ref.at[b // _SHARD_BLOCKS, :, pl.ds((b % _SHARD_BLOCKS) * _LANE, _LANE)]

    def seg_cols(ref, b):
        return ref.at[:, pl.ds(b * _LANE, _LANE)]

    src_cols, dst_cols = (shard_cols, seg_cols) if to_segments else (seg_cols, shard_cols)

    def body(x_ref, map_ref, o_ref):
        src_row = _iota((_LANE, _LANE), 0)
        for d in range(n_dst):
            acc = jnp.zeros((_RELAYOUT_ROWS, _LANE), F32)
            for sb in sources[d]:
                sel = (src_row + sb * _LANE == map_ref[d]).astype(x_ref.dtype)
                acc = acc + jnp.dot(src_cols(x_ref, sb)[...], sel, preferred_element_type=F32)
            dst_cols(o_ref, d)[...] = acc.astype(o_ref.dtype)

    rows = x.shape[-2]
    out_shape = (rows, P_END) if to_segments else (N_CHIPS, rows, IN_SHARD_PAD)
    return pl.pallas_call(
        body, name=name, grid=(rows // _RELAYOUT_ROWS,),
        in_specs=[shard_blk if to_segments else seg_blk, pl.BlockSpec(col_map.shape, lambda i: (0, 0, 0))],
        out_specs=seg_blk if to_segments else shard_blk, out_shape=jax.ShapeDtypeStruct(out_shape, x.dtype),
        compiler_params=_cparams(("arbitrary",)),
    )(x, col_map)


def shards_to_segments(w):
    return _relayout("shards_to_segments", w, True)


def segments_to_shards(w):
    return _relayout("segments_to_shards", w, False)


def mixers_forward(x, w_in, sp, dep=None, h1=None):
    if h1 is None:
        h1 = rmsnorm_fwd(x, sp["norm1_g"], dep)
    p2 = mm_proj(h1, w_in, 0)
    y_a = sgu_fwd(p2, sp["sgu_norm_g"], sp["w_spatial"], sp["b_spatial"])
    y_b, saved_b = gdn_forward(p2, sp["conv_w"], sp["a_log"], sp["dt_bias"], sp["o_norm_g"])
    y_c, outs_c, lses_c = dattn_fwd(p2, sp["q_norm_g"], sp["k_norm_g"], _alibi_slopes())
    mix = jnp.concatenate([y_a, y_b, y_c], axis=1)
    return mix, (x, h1, p2, saved_b, (outs_c, lses_c), mix)


def ffn_up(x, mix, w_out, w_gu, sp):
    x1 = mm_out(mix, w_out, 0, x)
    h2 = rmsnorm_fwd(x1, sp["norm2_g"])
    gu, act = mm_gu_swiglu(h2, w_gu)
    return x1, h2, gu, act


def ffn_forward(x, mix, wg, sp):
    x1, h2, gu, act = ffn_up(x, mix, wg["out"], wg["gu"], sp)
    x2 = mm_down(act, wg["down"], 0, x1)
    return x2, (x1, h2, gu, act)


def ffn_backward(dx2, wg, sp, saved, dep=None, on_weight_grads=None):
    x1, h2, gu, act = saved
    dgu = mm_dact_swiglu(dx2, wg["down"], gu, dep)
    dw_down = mm_dwdown(act, dx2)
    dw_gu = mm_dwgu(h2, dgu)
    tok = None if on_weight_grads is None else on_weight_grads(dw_gu, dw_down)
    dh2 = mm_dh2(dgu, wg["gu"], 0, tok)
    dx1, dnorm2 = rmsnorm_bwd(x1, sp["norm2_g"], dh2, dx2)
    return dx1, dnorm2, dw_gu, dw_down


def mixers_backward(dx1, wg, sp, saved, dep=None, on_weight_grads=None):
    x, h1, p2, saved_b, saved_c, mix = saved
    dmix = mm_dmix(dx1, wg["out"], 0, dep)
    dw_out = mm_dwout(mix, dx1)
    du, dv, dsg, dws, dbs = sgu_bwd(p2, sp["sgu_norm_g"], sp["w_spatial"], sp["b_spatial"], dmix)
    dseg_b, dconv, dal, ddtb, dog = gdn_backward(p2, sp["conv_w"], sp["a_log"], sp["dt_bias"], sp["o_norm_g"], saved_b, dmix)
    dcq, dck, dcv, dqg, dkg = dattn_bwd(p2, sp["q_norm_g"], sp["k_norm_g"], _alibi_slopes(), *saved_c, dmix)
    dp2 = jnp.concatenate([du, dv] + dseg_b + [dcq, dck, dcv], axis=1)
    dw_in = segments_to_shards(mm_dwin(h1, dp2))
    tok = None if on_weight_grads is None else on_weight_grads(dw_in, dw_out)
    dh1 = mm_dh1(dp2, wg["in"], 0, tok)
    dx, dnorm1 = rmsnorm_bwd(x, sp["norm1_g"], dh1, dx1)
    small = {"norm1_g": dnorm1, "sgu_norm_g": dsg, "w_spatial": dws, "b_spatial": dbs, "conv_w": dconv, "a_log": dal,
             "dt_bias": ddtb, "o_norm_g": dog, "q_norm_g": dqg, "k_norm_g": dkg}
    return dx, dw_in, dw_out, small


_HBM = pl.BlockSpec(memory_space=pltpu.HBM)
_MESH = pl.DeviceIdType.MESH


def _place():
    x, y, c = lax.axis_index("x"), lax.axis_index("y"), lax.axis_index("c")
    chips = [(1 - x, y), (x, 1 - y), (1 - x, 1 - y)]
    return x, y, c, chips


def _rcopy(src, dst, ssem, rsem, dev):
    return pltpu.make_async_remote_copy(src_ref=src, dst_ref=dst, send_sem=ssem, recv_sem=rsem, device_id=dev,
                                        device_id_type=_MESH)


_SEM = pl.BlockSpec(memory_space=pltpu.SEMAPHORE)
_SIDE_EFFECT = pltpu.SideEffectType.DATAFLOW_SIDE_EFFECTING


def _in_hbm(a):
    return pltpu.with_memory_space_constraint(a, pltpu.HBM)


def _split_copy(name, srcs, land_shapes, n_sems, copies):
    n, m = len(srcs), len(land_shapes)
    thru = [pltpu.HBM(a.shape, a.dtype) for a in srcs] + [pltpu.HBM(s.shape, s.dtype) for s in land_shapes]
    sems = (pltpu.SemaphoreType.DMA((n_sems,)), pltpu.SemaphoreType.DMA((n_sems,)))

    def start(dep=None):
        deps = [] if dep is None else [dep]

        def body(*refs):
            ins, lands = refs[:n], refs[n:n + m]
            ssem, rsem, token = refs[n + m + len(deps)], refs[n + m + len(deps) + 1], refs[-1]
            for cp in copies(ins, lands, ssem, rsem)[0]:
                cp.start()
            token[...] = jnp.zeros_like(token)

        out = pl.pallas_call(
            body, name=name + "_start", out_shape=(*sems, *thru, jax.ShapeDtypeStruct((8, HD), F32)),
            in_specs=[_HBM] * (n + m) + [pl.BlockSpec(memory_space=pl.ANY)] * len(deps),
            out_specs=(_SEM, _SEM, *[_HBM] * (n + m), pl.BlockSpec(memory_space=pltpu.VMEM)),
            input_output_aliases={i: 2 + i for i in range(n + m)},
            compiler_params=pltpu.CompilerParams(has_side_effects=_SIDE_EFFECT),
        )(*[_in_hbm(a) for a in srcs], *[_in_hbm(lax.empty(s.shape, s.dtype)) for s in land_shapes], *deps)
        return out[:-1], out[-1]

    def wait(state, after):
        def body(*refs):
            ins, lands, ssem, rsem = refs[:n], refs[n:n + m], refs[n + m], refs[n + m + 1]
            sent, arrivals = copies(ins, lands, ssem, rsem)
            for cp in sent:
                cp.wait_send()
            for cp in arrivals:
                cp.wait_recv()

        out = pl.pallas_call(
            body, name=name + "_wait", out_shape=tuple(thru),
            in_specs=[_HBM] * (n + m) + [_SEM, _SEM, pl.BlockSpec(memory_space=pl.ANY)], out_specs=[_HBM] * (n + m),
            input_output_aliases={i: i for i in range(n + m)},
            compiler_params=pltpu.CompilerParams(has_side_effects=_SIDE_EFFECT),
        )(*state[2:], state[0], state[1], after)
        return list(out[:n]), list(out[n:])

    return start, wait


def gather_direct(shards, tag):
    n = len(shards)

    def copies(ins, lands, ssem, rsem):
        x, y, c, chips = _place()
        s = 2 * x + y
        sibling = (x, y, 1 - c)
        sent, arrivals = [], []
        for a in range(n):
            for u in range(2):
                cp = _rcopy(ins[a].at[u], lands[a].at[s, u], ssem.at[5 * a + u], rsem.at[5 * a + u], sibling)
                sent.append(cp)
                arrivals.append(cp)
            for j, (cx, cy) in enumerate(chips):
                k = 5 * a + 2 + j
                sent.append(_rcopy(ins[a].at[c], lands[a].at[s, c], ssem.at[k], rsem.at[k], (cx, cy, c)))
                arrivals.append(_rcopy(ins[a].at[c], lands[a].at[2 * cx + cy, c], ssem.at[k], rsem.at[k], (cx, cy, c)))
        return sent, arrivals

    lands = [jax.ShapeDtypeStruct((N_CHIPS,) + w.shape, w.dtype) for w in shards]
    return _split_copy("gather_direct_" + tag, shards, lands, 5 * n, copies)


def pass_to_sibling(lands):
    n = len(lands)

    def body(*refs):
        ins = refs[:n]
        ssem, rsem = refs[2 * n:]
        x, y, c, chips = _place()
        sibling = (x, y, 1 - c)
        cps, arrivals = [], []
        for a in range(n):
            for j, (cx, cy) in enumerate(chips):
                t = 2 * cx + cy
                cps.append(_rcopy(ins[a].at[t, c], ins[a].at[t, c], ssem.at[a, j], rsem.at[a, j], sibling))
                arrivals.append(_rcopy(ins[a].at[t, c], ins[a].at[t, 1 - c], ssem.at[a, j], rsem.at[a, j], sibling))
        for cp in cps:
            cp.start()
        for cp, ar in zip(cps, arrivals):
            cp.wait_send()
            ar.wait_recv()

    return pl.pallas_call(
        body, name="pass_to_sibling", in_specs=[_HBM] * n, out_specs=[_HBM] * n,
        out_shape=[jax.ShapeDtypeStruct(a.shape, a.dtype) for a in lands], input_output_aliases={a: a for a in range(n)},
        scratch_shapes=[pltpu.SemaphoreType.DMA((n, 3)), pltpu.SemaphoreType.DMA((n, 3))],
    )(*lands)


def exchange_halves(grads, tag):
    n = len(grads)

    def copies(ins, lands, ssem, rsem):
        x, y, c, _ = _place()
        cps = []
        for a in range(n):
            h = grads[a].shape[1] // 2
            cps.append(_rcopy(ins[a].at[:, pl.ds((1 - c) * h, h)], lands[a], ssem.at[a], rsem.at[a], (x, y, 1 - c)))
        return cps, cps

    lands = [jax.ShapeDtypeStruct((g.shape[0], g.shape[1] // 2, g.shape[2]), g.dtype) for g in grads]
    return _split_copy("exchange_halves_" + tag, grads, lands, n, copies)


def scatter_direct(parts, tag):
    n = len(parts)

    def copies(ins, lands, ssem, rsem):
        x, y, c, chips = _place()
        cps = [_rcopy(ins[a].at[2 * cx + cy], lands[a].at[j], ssem.at[3 * a + j], rsem.at[3 * a + j], (cx, cy, c))
               for a in range(n) for j, (cx, cy) in enumerate(chips)]
        return cps, cps

    lands = [jax.ShapeDtypeStruct((3,) + p.shape[1:], p.dtype) for p in parts]
    return _split_copy("scatter_direct_" + tag, parts, lands, 3 * n, copies)


def share_halves(halves):
    n = len(halves)

    def body(*refs):
        ins, outs = refs[:n], refs[n:2 * n]
        ssem, rsem = refs[2 * n:]
        x, y, c, _ = _place()
        cps = [_rcopy(ins[i], outs[i], ssem.at[i], rsem.at[i], (x, y, 1 - c)) for i in range(n)]
        for cp in cps:
            cp.start()
        for cp in cps:
            cp.wait()

    return pl.pallas_call(
        body, name="share_halves", in_specs=[_HBM] * n, out_specs=[_HBM] * n,
        out_shape=[jax.ShapeDtypeStruct(h.shape, h.dtype) for h in halves],
        scratch_shapes=[pltpu.SemaphoreType.DMA((n,)), pltpu.SemaphoreType.DMA((n,))],
    )(*halves)


_ADAMW_BLOCK_BYTES = 3 << 19


def adamw_shard(w, m, v, mine, theirs, c, name):
    _, r, cw = w.shape
    h, cg = mine[0].shape
    tr = next(t for t in (256, 176, 128) if h % t == 0 and t * cg * 4 <= _ADAMW_BLOCK_BYTES)
    nb = h // tr
    wblk = pl.BlockSpec((None, tr, cw), lambda l, i, c_ref: (l, i, 0))
    gblk = lambda layer, own: pl.BlockSpec((tr, cg), lambda l, i, c_ref: (_held_block(l, i, c_ref, layer, own, nb), 0))
    return _adamw_halves(w, m, v, mine, theirs, c, name, (DEPTH, r // tr), wblk, gblk, nb, cw)


def _held_block(l, i, c_ref, layer, own, nb):
    in_use = (l == layer) & (((i // nb) == c_ref[0]) == own)
    return jnp.where(in_use, i % nb, 0)


def _adamw_halves(w, m, v, mine, theirs, c, name, grid, wblk, gblk, nb, cw):
    def body(c_ref, w_ref, m_ref, v_ref, m0, m1, t0, t1, g_ref, d_ref, nm_ref, nv_ref):
        is_mine = (pl.program_id(1) // nb) == c_ref[0]
        first = pl.program_id(0) == 0
        gg = jnp.where(is_mine, jnp.where(first, m0[:, :cw], m1[:, :cw]), jnp.where(first, t0[:, :cw], t1[:, :cw]))
        nm = ADAM_B1 * m_ref[...] + (1.0 - ADAM_B1) * gg
        nv = ADAM_B2 * v_ref[...] + (1.0 - ADAM_B2) * (gg * gg)
        m_hat = nm / (1.0 - ADAM_B1 ** ADAM_STEP)
        v_hat = nv / (1.0 - ADAM_B2 ** ADAM_STEP)
        g_ref[...] = gg
        d_ref[...] = -ADAM_LR * (m_hat / (jnp.sqrt(v_hat) + ADAM_EPS) + ADAM_WD * w_ref[...])
        nm_ref[...] = nm
        nv_ref[...] = nv

    return pl.pallas_call(
        body, name=name,
        grid_spec=pltpu.PrefetchScalarGridSpec(
            num_scalar_prefetch=1, grid=grid,
            in_specs=[wblk] * 3 + [gblk(0, True), gblk(1, True), gblk(0, False), gblk(1, False)], out_specs=[wblk] * 4),
        out_shape=[jax.ShapeDtypeStruct(w.shape, F32)] * 4, compiler_params=_cparams(("arbitrary", "arbitrary")),
    )(c, w, m, v, mine[0], mine[1], theirs[0], theirs[1])


def adamw_shard_t(wt, mt, vt, mine_t, theirs_t, c, name):
    _, cw, r = wt.shape
    h = mine_t[0].shape[1]
    tc = 256
    nb = h // tc
    wblk = pl.BlockSpec((None, cw, tc), lambda l, j, c_ref: (l, 0, j))
    gblk = lambda layer, own: pl.BlockSpec((cw, tc), lambda l, j, c_ref: (0, _held_block(l, j, c_ref, layer, own, nb)))
    return _adamw_halves(wt, mt, vt, mine_t, theirs_t, c, name, (DEPTH, r // tc), wblk, gblk, nb, cw)


def _half_rows(h, cols):
    for tr in (512, 256, 352, 128, 64):
        if h % tr == 0 and tr * cols * 4 <= 6 * 1024 * 1024:
            return tr
    raise ValueError((h, cols))


def add_sibling(grad, recv, c):
    _, r, cols = grad.shape
    h = r // 2
    tr = _half_rows(h, cols)
    nb = h // tr

    def body(c_ref, g_ref, r_ref, o_ref):
        o_ref[...] = (g_ref[...].astype(F32) + r_ref[...].astype(F32)).astype(BF16)

    return pl.pallas_call(
        body, name="add_sibling",
        grid_spec=pltpu.PrefetchScalarGridSpec(
            num_scalar_prefetch=1, grid=(N_CHIPS, nb),
            in_specs=[pl.BlockSpec((None, tr, cols), lambda t, i, c_ref: (t, c_ref[0] * nb + i, 0)),
                      pl.BlockSpec((None, tr, cols), lambda t, i, c_ref: (t, i, 0))],
            out_specs=pl.BlockSpec((None, tr, cols), lambda t, i, c_ref: (t, i, 0))),
        out_shape=jax.ShapeDtypeStruct((N_CHIPS, h, cols), BF16), compiler_params=_cparams(("arbitrary", "arbitrary")),
    )(c, grad, recv)


def add_chips(part, recv, s):
    _, h, cols = part.shape
    tr = _half_rows(h, cols)

    def body(s_ref, p_ref, r_ref, o_ref):
        o_ref[...] = ((p_ref[...].astype(F32) + r_ref[0].astype(F32)) + r_ref[1].astype(F32)) + r_ref[2].astype(F32)

    return pl.pallas_call(
        body, name="add_chips",
        grid_spec=pltpu.PrefetchScalarGridSpec(
            num_scalar_prefetch=1, grid=(h // tr,),
            in_specs=[pl.BlockSpec((None, tr, cols), lambda i, s_ref: (s_ref[0], i, 0)),
                      pl.BlockSpec((3, tr, cols), lambda i, s_ref: (0, i, 0))],
            out_specs=pl.BlockSpec((tr, cols), lambda i, s_ref: (i, 0))),
        out_shape=jax.ShapeDtypeStruct((h, cols), F32), compiler_params=_cparams(("arbitrary",)),
    )(s, part, recv)


def allreduce_small(vec):
    rows = vec.shape[0]

    def body(v_ref, o_ref, buf, ssem, rsem, lsem):
        x, y, c, chips = _place()
        me, sibling = (x, y, c), (x, y, 1 - c)

        def blk(px, py, pc):
            return buf.at[4 * px + 2 * py + pc]

        def copy(k, block, to, src=None):
            return _rcopy(blk(*block) if src is None else src, blk(*block), ssem.at[k], rsem.at[k], to)

        mine = pltpu.make_async_copy(v_ref, blk(*me), lsem)
        mine.start()
        first = [copy(0, me, sibling, src=v_ref)] + [copy(1 + j, me, (*chip, c), src=v_ref) for j, chip in enumerate(chips)]
        for cp in first:
            cp.start()
        passed = [copy(4 + j, (*chip, c), sibling) for j, chip in enumerate(chips)]
        for j, chip in enumerate(chips):
            copy(1 + j, (*chip, c), me).wait_recv()
            passed[j].start()
        copy(0, sibling, me).wait_recv()
        for j, chip in enumerate(chips):
            copy(4 + j, (*chip, 1 - c), me).wait_recv()
        for cp in first + passed:
            cp.wait_send()
        mine.wait()
        acc = buf[0]
        for d in range(1, N_DEV):
            acc = acc + buf[d]
        o_ref[...] = acc

    vm = pl.BlockSpec(memory_space=pltpu.VMEM)
    return pl.pallas_call(
        body, name="allreduce_small", in_specs=[vm], out_specs=vm, out_shape=jax.ShapeDtypeStruct(vec.shape, F32),
        scratch_shapes=[pltpu.VMEM((N_DEV, rows, HD), F32), pltpu.SemaphoreType.DMA((7,)), pltpu.SemaphoreType.DMA((7,)),
                        pltpu.SemaphoreType.DMA],
        compiler_params=pltpu.CompilerParams(vmem_limit_bytes=VMEM_LIMIT),
    )(vec)


SMALL_NAMES = ("norm1_g", "sgu_norm_g", "w_spatial", "b_spatial", "conv_w", "a_log", "dt_bias", "o_norm_g", "q_norm_g",
               "k_norm_g", "norm2_g")


def small_params(l, p, conv_full):
    return {"norm1_g": p["norm1_g"][l][None], "sgu_norm_g": p["sgu_norm_g"][l][:, None, :], "w_spatial": p["w_spatial"][l],
            "b_spatial": p["b_spatial"][l][..., None], "conv_w": conv_full[l], "a_log": p["a_log"][l], "dt_bias": p["dt_bias"][l],
            "o_norm_g": p["o_norm_g"][l][None], "q_norm_g": p["q_norm_g"][l][None], "k_norm_g": p["k_norm_g"][l][None],
            "norm2_g": p["norm2_g"][l][None]}


_PACK_TILE = 8 * HD


def _pack(arrays):
    flat = jnp.concatenate([a.reshape(-1) for a in arrays])
    pad = -flat.shape[0] % _PACK_TILE
    return jnp.pad(flat, (0, pad)).reshape(-1, HD)


def _unpack(packed, shapes):
    flat, out, off = packed.reshape(-1), [], 0
    for shp in shapes:
        n = int(np.prod(shp))
        out.append(flat[off:off + n].reshape(shp))
        off += n
    return out


WEIGHT_ORDER = ("norm1_g", "w_in", "sgu_norm_g", "w_spatial", "b_spatial", "conv_w", "a_log", "dt_bias", "o_norm_g", "q_norm_g",
                "k_norm_g", "w_out", "norm2_g", "w_gate_up", "w_down")


def kernel(x, norm1_g, w_in, sgu_norm_g, w_spatial, b_spatial, conv_w, a_log, dt_bias, o_norm_g, q_norm_g, k_norm_g, w_out, norm2_g, w_gate_up, w_down, loss_target, m_norm1_g, m_w_in, m_sgu_norm_g, m_w_spatial, m_b_spatial, m_conv_w, m_a_log, m_dt_bias, m_o_norm_g, m_q_norm_g, m_k_norm_g, m_w_out, m_norm2_g, m_w_gate_up, m_w_down, v_norm1_g, v_w_in, v_sgu_norm_g, v_w_spatial, v_b_spatial, v_conv_w, v_a_log, v_dt_bias, v_o_norm_g, v_q_norm_g, v_k_norm_g, v_w_out, v_norm2_g, v_w_gate_up, v_w_down):
    w = dict(norm1_g=norm1_g, w_in=w_in, sgu_norm_g=sgu_norm_g, w_spatial=w_spatial, b_spatial=b_spatial, conv_w=conv_w,
             a_log=a_log, dt_bias=dt_bias, o_norm_g=o_norm_g, q_norm_g=q_norm_g, k_norm_g=k_norm_g, w_out=w_out,
             norm2_g=norm2_g, w_gate_up=w_gate_up, w_down=w_down)
    m = dict(norm1_g=m_norm1_g, w_in=m_w_in, sgu_norm_g=m_sgu_norm_g, w_spatial=m_w_spatial, b_spatial=m_b_spatial,
             conv_w=m_conv_w, a_log=m_a_log, dt_bias=m_dt_bias, o_norm_g=m_o_norm_g, q_norm_g=m_q_norm_g, k_norm_g=m_k_norm_g,
             w_out=m_w_out, norm2_g=m_norm2_g, w_gate_up=m_w_gate_up, w_down=m_w_down)
    v = dict(norm1_g=v_norm1_g, w_in=v_w_in, sgu_norm_g=v_sgu_norm_g, w_spatial=v_w_spatial, b_spatial=v_b_spatial,
             conv_w=v_conv_w, a_log=v_a_log, dt_bias=v_dt_bias, o_norm_g=v_o_norm_g, q_norm_g=v_q_norm_g, k_norm_g=v_k_norm_g,
             w_out=v_w_out, norm2_g=v_norm2_g, w_gate_up=v_w_gate_up, w_down=v_w_down)
    chip = (2 * lax.axis_index("x") + lax.axis_index("y")).astype(jnp.int32)
    core = lax.axis_index("c").astype(jnp.int32)

    in_pad = IN_SHARD_PAD - IN_SHARD
    w_in_pad = jnp.pad(w_in, ((0, 0), (0, 0), (0, in_pad)))

    halves_of = lambda a: a.reshape(2, a.shape[0] // 2, a.shape[1])
    start_0, wait_0 = gather_direct([halves_of(w_in_pad[0].astype(BF16)), halves_of(conv_w[0])], "mix0")
    state_0, token_0 = start_0()
    bf_halves = lambda a: halves_of((a + token_0[0, 0]).astype(BF16))

    def ffn_shards(l):
        return [bf_halves(w_gate_up[l]), bf_halves(w_down[l]), bf_halves(w_out[l])]

    def mixer_shards(l):
        return [bf_halves(w_in_pad[l]), halves_of(conv_w[l])]

    def mixer_weights(g):
        g_in, g_conv = g
        return (shards_to_segments(g_in.reshape(N_CHIPS, D, IN_SHARD_PAD))[None],
                g_conv.reshape(N_CHIPS, B_CONV, -1).transpose(1, 0, 2).reshape(B_CONV, 3 * B_WIDTH))

    def ffn_weights(g, w_in_seg):
        g_gu, g_down, g_out = g
        return {"in": w_in_seg, "out": g_out.reshape(1, D, D), "gu": g_gu.reshape(1, N_CHIPS, D, GU_SHARD),
                "down": None if g_down is None else g_down.reshape(1, FFN, D)}

    def layer_params(l, conv_full):
        return small_params(0, {n: w[n][l:l + 1] for n in SMALL_NAMES if n != "conv_w"}, conv_full[None])

    gu0, down0, out0 = ffn_shards(0)
    start_a, wait_a = gather_direct([gu0, out0], "ffn0")
    start_b, wait_b = gather_direct([down0] + mixer_shards(1), "mid")
    start_c, wait_c = gather_direct(ffn_shards(1), "ffn1")
    state_a, token_a = start_a(token_0)
    state_b, token_b = start_b(token_a)
    state_c, token_c = start_c(token_b)
    h1_0 = rmsnorm_fwd(x[0], norm1_g[0][None], token_c)
    w_in0, conv0 = mixer_weights(pass_to_sibling(wait_0(state_0, h1_0)[1]))
    sps = [layer_params(0, conv0), None]
    mix0, saved_m0 = mixers_forward(x[0], w_in0, sps[0], h1=h1_0)
    g_gu0, g_out0 = pass_to_sibling(wait_a(state_a, mix0)[1])
    wg0 = ffn_weights((g_gu0, None, g_out0), w_in0)
    x1_0, h2_0, gu_0, act_0 = ffn_up(x[0], mix0, wg0["out"], wg0["gu"], sps[0])
    g_down0, g_in1, g_conv1 = pass_to_sibling(wait_b(state_b, act_0)[1])
    wg0["down"] = g_down0.reshape(1, FFN, D)
    x1 = mm_down(act_0, wg0["down"], 0, x1_0)
    saved_f0 = (x1_0, h2_0, gu_0, act_0)
    w_in1, conv1 = mixer_weights((g_in1, g_conv1))
    sps[1] = layer_params(1, conv1)
    mix1, saved_m1 = mixers_forward(x1, w_in1, sps[1])
    wg1 = ffn_weights(pass_to_sibling(wait_c(state_c, mix1)[1]), w_in1)
    x2, saved_f1 = ffn_forward(x1, mix1, wg1, sps[1])
    saved1 = (saved_m1, saved_f1)
    dx, loss_tile = loss_and_grad(x2, loss_target[0])

    def sibling_exchange(tag, by_chip):
        cell = {}

        def on_weight_grads(*dws):
            start, wait = exchange_halves([f(t) for f, t in zip(by_chip, dws)], tag)
            state, token = start()
            cell["wait"] = functools.partial(wait, state)
            return token

        return on_weight_grads, cell

    def start_scatter(cell, after, tag, dep=None):
        grads, from_sibling = cell["wait"](after)
        parts = [add_sibling(g, r, core.reshape(1)) for g, r in zip(grads, from_sibling)]
        start, wait = scatter_direct(parts, tag)
        state, token = start(dep)
        return functools.partial(wait, state), token

    smalls = [None] * DEPTH
    same = lambda t: t
    ffn_by_chip = (same, lambda t: t.reshape(N_CHIPS, DOWN_SHARD, D))
    mix_by_chip = (same, lambda t: t.reshape(N_CHIPS, OUT_SHARD, D))
    hook, cell_f1 = sibling_exchange("ffn1", ffn_by_chip)
    dx1, dnorm2_1, _, _ = ffn_backward(dx, wg1, sps[1], saved1[1], on_weight_grads=hook)
    wait_f1, tok_f1 = start_scatter(cell_f1, dx1, "ffn1")
    hook, cell_m1 = sibling_exchange("mix1", mix_by_chip)
    dx, _, _, small1 = mixers_backward(dx1, wg1, sps[1], saved1[0], dep=tok_f1, on_weight_grads=hook)
    smalls[1] = {**small1, "norm2_g": dnorm2_1}
    wait_m1, tok_m1 = start_scatter(cell_m1, dx, "mix1")
    hook, cell_f0 = sibling_exchange("ffn0", ffn_by_chip)
    dx1, dnorm2_0, _, _ = ffn_backward(dx, wg0, sps[0], saved_f0, dep=tok_m1, on_weight_grads=hook)
    wait_f0, tok_f0 = start_scatter(cell_f0, dx1, "ffn0")
    hook, cell_m0 = sibling_exchange("mix0", mix_by_chip)
    dx, _, _, small0 = mixers_backward(dx1, wg0, sps[0], saved_m0, dep=tok_f0, on_weight_grads=hook)
    smalls[0] = {**small0, "norm2_g": dnorm2_0}

    grad, delta, new_m, new_v = {}, {}, {}, {}
    stacked = [jnp.stack([smalls[l][n] for l in range(DEPTH)]) for n in SMALL_NAMES]
    total = allreduce_small(_pack(stacked + [loss_tile[0, :1]]))
    shapes = [(DEPTH, B_CONV, 3 * B_WIDTH) if n == "conv_w" else w[n].shape for n in SMALL_NAMES]
    small_grads = dict(zip(SMALL_NAMES, _unpack(total, shapes + [(1,)])[:-1]))
    loss = _unpack(total, shapes + [(1,)])[-1][0]
    conv_cols = conv_w.shape[-1]
    small_grads["conv_w"] = lax.dynamic_slice_in_dim(small_grads["conv_w"], chip * conv_cols, conv_cols, axis=2)
    grad.update(small_grads)
    sshapes = [w[n].shape for n in SMALL_NAMES]
    packed = [_pack([d[n] for n in SMALL_NAMES]) for d in (w, grad, m, v)]
    for dst, t in zip((delta, new_m, new_v), adamw(*packed, "adamw_small")):
        dst.update(zip(SMALL_NAMES, _unpack(t, sshapes)))

    wait_m0, tok_m0 = start_scatter(cell_m0, dx, "mix0", dep=total)
    (pf0, rf0), (pm1, rm1), (pf1, rf1) = (wt(tok_m0) for wt in (wait_f0, wait_m1, wait_f1))

    def reduce_group(parts, from_chips):
        mine = [add_chips(p, r, chip.reshape(1)) for p, r in zip(parts, from_chips)]
        return mine, list(share_halves(mine))

    mine_f, theirs_f = reduce_group(pf0 + pf1 + pm1, rf0 + rf1 + rm1)
    for a, n in enumerate(("w_gate_up", "w_down")):
        grad[n], delta[n], new_m[n], new_v[n] = adamw_shard(w[n], m[n], v[n], [mine_f[a], mine_f[2 + a]],
                                                            [theirs_f[a], theirs_f[2 + a]], core.reshape(1), "adamw_" + n)

    mine_m0, theirs_m0 = reduce_group(*wait_m0(new_v["w_down"]))
    tr_ = lambda t: jnp.swapaxes(t, -1, -2)
    cut = lambda t: tr_(t[:, :IN_SHARD])
    res = adamw_shard_t(tr_(w_in), tr_(m_w_in), tr_(v_w_in), [cut(mine_m0[0]), cut(mine_f[4])],
                        [cut(theirs_m0[0]), cut(theirs_f[4])], core.reshape(1), "adamw_w_in")
    grad["w_in"], delta["w_in"], new_m["w_in"], new_v["w_in"] = (tr_(t) for t in res)
    grad["w_out"], delta["w_out"], new_m["w_out"], new_v["w_out"] = adamw_shard(
        w_out, m_w_out, v_w_out, [mine_m0[1], mine_f[5]], [theirs_m0[1], theirs_f[5]], core.reshape(1), "adamw_w_out")

    out = [loss, dx[None]]
    for d in (grad, delta, new_m, new_v):
        out += [d[n] for n in WEIGHT_ORDER]
    return tuple(out)
```

```python
import functools
import math

import numpy as np
import jax
import jax.numpy as jnp
from jax import lax
from jax.experimental import pallas as pl
from jax.experimental.pallas import tpu as pltpu

F32 = jnp.float32
BF16 = jnp.bfloat16
HI = lax.Precision.HIGH

T = 2048
D = 2048
DEPTH = 2
HD = 128
A_GROUPS, A_WIDTH, A_CHUNK = 4, 512, 128
B_HEADS, B_WIDTH, B_CONV, B_CHUNK = 6, 768, 4, 64
C_HEADS, C_WIDTH, C_BLOCK = 6, 768, 128
C_BRANCHES = ((128, 1), (512, 4), (2048, 16))
FFN = 5632
IN_TOTAL = 6412
EPS = 1e-6
N_CHIPS = 4
N_DEV = 8
IN_SHARD = IN_TOTAL // N_CHIPS
IN_SHARD_PAD = 1664
GU_SHARD = 2 * FFN // N_CHIPS
OUT_SHARD = D // N_CHIPS
DOWN_SHARD = FFN // N_CHIPS
P_AU, P_AV, P_BQ, P_BK, P_BV, P_BG, P_BB, P_CQ, P_CK, P_CV, P_END = (
    0, 512, 1024, 1792, 2560, 3328, 4096, 4224, 4992, 5760, 6528)
GATE_COLS = 4108
VMEM_LIMIT = 56 * 1024 * 1024

ADAM_LR, ADAM_B1, ADAM_B2, ADAM_EPS, ADAM_WD, ADAM_STEP = 0.001, 0.9, 0.999, 1e-08, 0.01, 10


def _cparams(sem, vmem=VMEM_LIMIT):
    return pltpu.CompilerParams(dimension_semantics=sem, vmem_limit_bytes=vmem)


def _dims(nd, ta, tb):
    off = nd - 2
    ca = off + (0 if ta else 1)
    cb = off + (1 if tb else 0)
    batch = ((0,), (0,)) if nd == 3 else ((), ())
    return (((ca,), (cb,)), batch)


def _raw_mm(a, b, ta, tb, hi):
    if hi:
        return lax.dot_general(a, b, _dims(a.ndim, ta, tb), precision=HI, preferred_element_type=F32)
    return lax.dot_general(a.astype(BF16), b.astype(BF16), _dims(a.ndim, ta, tb), preferred_element_type=F32)


@functools.partial(jax.custom_vjp, nondiff_argnums=(2, 3, 4))
def _mm(a, b, ta=False, tb=False, hi=False):
    return _raw_mm(a, b, ta, tb, hi)


def _mm_fwd(a, b, ta, tb, hi):
    return _raw_mm(a, b, ta, tb, hi), (a, b)


def _mm_bwd(ta, tb, hi, res, g):
    a, b = res
    da = _raw_mm(g, b, False, not tb, False) if not ta else _raw_mm(b, g, tb, True, False)
    db = _raw_mm(a, g, not ta, False, False) if not tb else _raw_mm(g, a, True, ta, False)
    return da.astype(a.dtype), db.astype(b.dtype)


_mm.defvjp(_mm_fwd, _mm_bwd)


def _rms(x, g):
    return x * lax.rsqrt(jnp.mean(x * x, axis=-1, keepdims=True) + EPS) * g


def _gelu(x):
    return 0.5 * x * (1.0 + jnp.tanh(math.sqrt(2.0 / math.pi) * (x + 0.044715 * (x * x * x))))


def _sigmoid(x):
    return 1.0 / (1.0 + jnp.exp(-x))


def _silu(x):
    return x * _sigmoid(x)


def _softplus(x):
    return jnp.maximum(x, 0.0) + jnp.log(1.0 + jnp.exp(-jnp.abs(x)))


def _iota(shape, dim):
    return lax.broadcasted_iota(jnp.int32, shape, dim)


def _sgu_fn(u, v, sg, w, b):
    nc = T // A_CHUNK
    ug = _gelu(u)
    vn = _rms(_gelu(v), sg)
    causal = _iota((A_CHUNK, A_CHUNK), 0) >= _iota((A_CHUNK, A_CHUNK), 1)
    wm = jnp.where(causal, w, 0.0)
    wb = jnp.broadcast_to(wm[None], (nc, A_CHUNK, A_CHUNK))
    z = _mm(wb, vn.reshape(nc, A_CHUNK, HD)) + b[None]
    return ug * z.reshape(T, HD)


def _sgu_specs():
    col = lambda off: pl.BlockSpec((T, HD), lambda g, off=off: (0, off + g))
    par = [pl.BlockSpec((None, 1, HD), lambda g: (g, 0, 0)),
           pl.BlockSpec((None, A_CHUNK, A_CHUNK), lambda g: (g, 0, 0)),
           pl.BlockSpec((None, A_CHUNK, 1), lambda g: (g, 0, 0))]
    return col, par


def sgu_fwd(p2, sg, w, b):
    col, par = _sgu_specs()

    def body(u_ref, v_ref, sg_ref, w_ref, b_ref, y_ref):
        y_ref[...] = _sgu_fn(u_ref[...], v_ref[...], sg_ref[...], w_ref[...], b_ref[...]).astype(BF16)

    return pl.pallas_call(
        body, name="sgu_fwd", grid=(A_GROUPS,),
        in_specs=[col(P_AU // HD), col(P_AV // HD)] + par,
        out_specs=pl.BlockSpec((T, HD), lambda g: (0, g)),
        out_shape=jax.ShapeDtypeStruct((T, A_WIDTH), BF16),
        compiler_params=_cparams(("arbitrary",)),
    )(p2, p2, sg, w, b)


def sgu_bwd(p2, sg, w, b, dmix):
    col, par = _sgu_specs()

    def body(u_ref, v_ref, sg_ref, w_ref, b_ref, dy_ref, du_ref, dv_ref, dsg_ref, dw_ref, db_ref):
        _, vjp = jax.vjp(_sgu_fn, u_ref[...], v_ref[...], sg_ref[...], w_ref[...], b_ref[...])
        du, dv, dsg, dw, db = vjp(dy_ref[...])
        du_ref[...] = du.astype(BF16)
        dv_ref[...] = dv.astype(BF16)
        dsg_ref[...] = dsg
        dw_ref[...] = dw
        db_ref[...] = db

    gcol = pl.BlockSpec((T, HD), lambda g: (0, g))
    return pl.pallas_call(
        body, name="sgu_bwd", grid=(A_GROUPS,),
        in_specs=[col(P_AU // HD), col(P_AV // HD)] + par + [gcol],
        out_specs=[gcol, gcol] + par,
        out_shape=[jax.ShapeDtypeStruct((T, A_WIDTH), BF16), jax.ShapeDtypeStruct((T, A_WIDTH), BF16),
                   jax.ShapeDtypeStruct((A_GROUPS, 1, HD), F32), jax.ShapeDtypeStruct((A_GROUPS, A_CHUNK, A_CHUNK), F32),
                   jax.ShapeDtypeStruct((A_GROUPS, A_CHUNK, 1), F32)],
        compiler_params=_cparams(("arbitrary",)),
    )(p2, p2, sg, w, b, dmix)


def _attn_fn(q, k, v, qg, kg, slope, *, dil, nb):
    n = T // C_BLOCK
    qb = _rms(q, qg).reshape(n, C_BLOCK, HD)
    kb = _rms(k, kg).reshape(n, C_BLOCK, HD)
    vb = v.reshape(n, C_BLOCK, HD)
    scale = HD ** -0.5
    qi = _iota((n, C_BLOCK, C_BLOCK), 1)
    kj = _iota((n, C_BLOCK, C_BLOCK), 2)
    sl = slope[None] * float(dil)
    d_cur = qi - kj
    sc = jnp.where(d_cur >= 0, _mm(qb, kb, tb=True) * scale - sl * d_cur.astype(F32), -jnp.inf)
    mx = jnp.max(sc, axis=-1, keepdims=True)
    if nb > 1:
        kp = jnp.concatenate([jnp.zeros((1, C_BLOCK, HD), F32), kb[:-1]], axis=0)
        vp = jnp.concatenate([jnp.zeros((1, C_BLOCK, HD), F32), vb[:-1]], axis=0)
        has_prev = (_iota((n, C_BLOCK, C_BLOCK), 0) % nb) > 0
        d_prev = C_BLOCK + qi - kj
        sp = jnp.where((kj >= qi) & has_prev, _mm(qb, kp, tb=True) * scale - sl * d_prev.astype(F32), -jnp.inf)
        mx = jnp.maximum(mx, jnp.max(sp, axis=-1, keepdims=True))
    p = jnp.exp(sc - mx)
    den = jnp.sum(p, axis=-1, keepdims=True)
    if nb > 1:
        pp = jnp.exp(sp - mx)
        den = den + jnp.sum(pp, axis=-1, keepdims=True)
    out = _mm(p / den, vb)
    if nb > 1:
        out = out + _mm(pp / den, vp)
    lse = mx + jnp.log(den)
    return out.reshape(T, HD), jnp.broadcast_to(lse, (n, C_BLOCK, HD)).reshape(T, HD)


def _combine_fn(o1, o2, o3, l1, l2, l3):
    mx = jnp.maximum(jnp.maximum(l1, l2), l3)
    e1, e2, e3 = jnp.exp(l1 - mx), jnp.exp(l2 - mx), jnp.exp(l3 - mx)
    s = e1 + e2 + e3
    return (e1 / s) * o1 + (e2 / s) * o2 + (e3 / s) * o3


def _branch_blocks(dil):
    return -(-(T // dil) // C_BLOCK)


def _load_branch_order(ref, dil):
    if dil == 1:
        return ref[...]
    seg = T // dil
    return jnp.concatenate([ref[pl.ds(r, seg, stride=dil), :] for r in range(dil)], axis=0)


def _store_position_order(ref, val, dil, add=False):
    seg = T // dil
    for r in range(dil):
        rows = slice(None) if dil == 1 else pl.ds(r, seg, stride=dil)
        piece = val if dil == 1 else val[r * seg:(r + 1) * seg]
        if add:
            ref[rows, :] += piece
        else:
            ref[rows, :] = piece


def _dattn_specs():
    col = lambda off: pl.BlockSpec((T, HD), lambda h, off=off: (0, off // HD + h))
    row = pl.BlockSpec((1, HD), lambda h: (0, 0))
    slope = pl.BlockSpec((None, 1, HD), lambda h: (h, 0, 0))
    return [col(P_CQ), col(P_CK), col(P_CV), row, row, slope]


def _dattn_branches(q_ref, k_ref, v_ref, qg, kg, slope, o_scr, l_scr):
    for b, (_, dil) in enumerate(C_BRANCHES):
        q, k, v = (_load_branch_order(r, dil) for r in (q_ref, k_ref, v_ref))
        o, l = _attn_fn(q, k, v, qg, kg, slope, dil=dil, nb=_branch_blocks(dil))
        _store_position_order(o_scr.at[b], o, dil)
        _store_position_order(l_scr.at[b], l, dil)


def dattn_fwd(p2, qg, kg, slopes, dep=None):
    per_branch = pl.BlockSpec((3, T, HD), lambda h: (0, 0, h))
    deps, dspecs = _dep_specs(dep, 1)

    def body(q_ref, k_ref, v_ref, qg_ref, kg_ref, s_ref, *rest):
        y_ref, o_ref, l_ref = rest[-3:]
        _dattn_branches(q_ref, k_ref, v_ref, qg_ref[...], kg_ref[...], s_ref[...], o_ref, l_ref)
        y_ref[...] = _combine_fn(o_ref[0], o_ref[1], o_ref[2], l_ref[0], l_ref[1], l_ref[2]).astype(BF16)

    return pl.pallas_call(
        body, name="dattn_fwd", grid=(C_HEADS,), in_specs=_dattn_specs() + dspecs,
        out_specs=[pl.BlockSpec((T, HD), lambda h: (0, h)), per_branch, per_branch],
        out_shape=[jax.ShapeDtypeStruct((T, C_WIDTH), BF16)] + [jax.ShapeDtypeStruct((3, T, C_WIDTH), F32)] * 2,
        compiler_params=_cparams(("arbitrary",)),
    )(p2, p2, p2, qg, kg, slopes, *deps)


def dattn_bwd(p2, qg, kg, slopes, outs, lses, dmix):
    hcol = pl.BlockSpec((T, HD), lambda h: (0, h))
    row = pl.BlockSpec((1, HD), lambda h: (0, 0))
    dy = pl.BlockSpec((T, HD), lambda h: (0, (A_WIDTH + B_WIDTH) // HD + h))
    per_branch = pl.BlockSpec((3, T, HD), lambda h: (0, 0, h))

    def body(q_ref, k_ref, v_ref, qg_ref, kg_ref, s_ref, o_scr, l_scr, dy_ref, dq_ref, dk_ref, dv_ref, dqg_ref, dkg_ref, g_scr,
             acc):
        qg, kg, slope = qg_ref[...], kg_ref[...], s_ref[...]
        _, vjp = jax.vjp(_combine_fn, o_scr[0], o_scr[1], o_scr[2], l_scr[0], l_scr[1], l_scr[2])
        for i, g in enumerate(vjp(dy_ref[...])):
            g_scr[i] = g

        @pl.when(pl.program_id(0) == 0)
        def _():
            dqg_ref[...] = jnp.zeros_like(dqg_ref)
            dkg_ref[...] = jnp.zeros_like(dkg_ref)

        for b, (_, dil) in enumerate(C_BRANCHES):
            q, k, v = (_load_branch_order(r, dil) for r in (q_ref, k_ref, v_ref))
            do, dl = _load_branch_order(g_scr.at[b], dil), _load_branch_order(g_scr.at[3 + b], dil)
            fn = functools.partial(_attn_fn, dil=dil, nb=_branch_blocks(dil))
            _, vjp_b = jax.vjp(lambda a, b_, c, d, e, fn=fn: fn(a, b_, c, d, e, slope), q, k, v, qg, kg)
            dq, dk, dv, dqg, dkg = vjp_b((do, dl))
            for i, val in enumerate((dq, dk, dv)):
                _store_position_order(acc.at[i], val, dil, add=b > 0)
            dqg_ref[...] += dqg
            dkg_ref[...] += dkg
        for i, ref in enumerate((dq_ref, dk_ref, dv_ref)):
            ref[...] = acc[i].astype(BF16)

    scr = lambda n: pltpu.VMEM((n, T, HD), F32)
    return pl.pallas_call(
        body, name="dattn_bwd", grid=(C_HEADS,), in_specs=_dattn_specs() + [per_branch, per_branch, dy],
        out_specs=[hcol, hcol, hcol, row, row],
        out_shape=[jax.ShapeDtypeStruct((T, C_WIDTH), BF16)] * 3 + [jax.ShapeDtypeStruct((1, HD), F32)] * 2,
        scratch_shapes=[scr(6), scr(3)], compiler_params=_cparams(("arbitrary",)),
    )(p2, p2, p2, qg, kg, slopes, outs, lses, dmix)


_NCH = T // B_CHUNK


def _conv_taps(x, w_ref):
    rows = _iota(x.shape, 0)
    taps = []
    for j in range(B_CONV):
        s = B_CONV - 1 - j
        taps.append(x if s == 0 else jnp.where(rows >= s, pltpu.roll(x, s, 0), 0.0))
    pre = sum(w_ref[j:j + 1, :] * taps[j] for j in range(B_CONV))
    return pre, taps


def _conv_post(pre, mode):
    y = _silu(pre)
    if mode == "v":
        return y
    y = y * lax.rsqrt(jnp.sum(y * y, axis=-1, keepdims=True) + EPS)
    return y * (HD ** -0.5) if mode == "q" else y


def conv_fwd(p2, conv_w, mode):
    idx = "qkv".index(mode)
    xcol = pl.BlockSpec((T, HD), lambda h: (0, P_BQ // HD + B_HEADS * idx + h))
    wcol = pl.BlockSpec((B_CONV, HD), lambda h: (0, B_HEADS * idx + h))
    hcol = pl.BlockSpec((T, HD), lambda h: (0, h))

    def body(x_ref, w_ref, y_ref):
        pre, _ = _conv_taps(x_ref[...], w_ref)
        y_ref[...] = _conv_post(pre, mode)

    return pl.pallas_call(
        body, name=f"conv_fwd_{mode}", grid=(B_HEADS,), in_specs=[xcol, wcol], out_specs=hcol,
        out_shape=jax.ShapeDtypeStruct((T, B_WIDTH), F32), compiler_params=_cparams(("arbitrary",)),
    )(p2, conv_w)


def conv_bwd(p2, conv_w, dys, mode):
    idx = "qkv".index(mode)
    xcol = pl.BlockSpec((T, HD), lambda h: (0, P_BQ // HD + B_HEADS * idx + h))
    wcol = pl.BlockSpec((B_CONV, HD), lambda h: (0, B_HEADS * idx + h))
    hcol = pl.BlockSpec((T, HD), lambda h: (0, h))
    wout = pl.BlockSpec((B_CONV, HD), lambda h: (0, h))

    def body(x_ref, w_ref, *rest):
        dy_refs, (dx_ref, dw_ref) = rest[:-2], rest[-2:]
        pre, taps = _conv_taps(x_ref[...], w_ref)
        _, vjp = jax.vjp(functools.partial(_conv_post, mode=mode), pre)
        (dpre,) = vjp(sum(r[...] for r in dy_refs))
        rows = _iota(dpre.shape, 0)
        dx = w_ref[B_CONV - 1:B_CONV, :] * dpre
        for j in range(B_CONV):
            s = B_CONV - 1 - j
            dw_ref[j:j + 1, :] = jnp.sum(dpre * taps[j], axis=0, keepdims=True)
            if s > 0:
                dx = dx + w_ref[j:j + 1, :] * jnp.where(rows < T - s, pltpu.roll(dpre, T - s, 0), 0.0)
        dx_ref[...] = dx.astype(BF16)

    return pl.pallas_call(
        body, name=f"conv_bwd_{mode}", grid=(B_HEADS,), in_specs=[xcol, wcol] + [hcol] * len(dys), out_specs=[hcol, wout],
        out_shape=[jax.ShapeDtypeStruct((T, B_WIDTH), BF16), jax.ShapeDtypeStruct((B_CONV, B_WIDTH), F32)],
        compiler_params=_cparams(("arbitrary",)),
    )(p2, conv_w, *dys)


def _gates_fn(bg, al, dtb, h):
    r = _iota((HD, HD), 0)
    logit = _mm(bg, (r == h).astype(F32), hi=True)
    a = _mm(bg, (r == h + B_HEADS).astype(F32), hi=True)
    beta = _sigmoid(logit)
    graw = -jnp.exp(al) * _softplus(a + dtb)
    tri = (_iota((_NCH, B_CHUNK, B_CHUNK), 1) >= _iota((_NCH, B_CHUNK, B_CHUNK), 2)).astype(F32)
    g = _mm(tri, graw.reshape(_NCH, B_CHUNK, HD), hi=True).reshape(T, HD)
    return beta, g


def _gates_specs():
    bg = pl.BlockSpec((T, HD), lambda h: (0, P_BB // HD))
    par = pl.BlockSpec((None, 1, HD), lambda h: (h, 0, 0))
    out = pl.BlockSpec((None, T, HD), lambda h: (h, 0, 0))
    return bg, par, out


def gates_fwd(p2, al, dtb):
    bg, par, out = _gates_specs()

    def body(bg_ref, al_ref, dtb_ref, beta_ref, g_ref):
        beta, g = _gates_fn(bg_ref[...], al_ref[...], dtb_ref[...], pl.program_id(0))
        beta_ref[...] = beta
        g_ref[...] = g

    return pl.pallas_call(
        body, name="gates_fwd", grid=(B_HEADS,), in_specs=[bg, par, par], out_specs=[out, out],
        out_shape=[jax.ShapeDtypeStruct((B_HEADS, T, HD), F32)] * 2, compiler_params=_cparams(("arbitrary",)),
    )(p2, al, dtb)


def gates_bwd(p2, al, dtb, dbeta, dg1, dg2):
    bg, par, out = _gates_specs()
    acc = pl.BlockSpec((T, HD), lambda h: (0, 0))

    def body(bg_ref, al_ref, dtb_ref, dbeta_ref, dg1_ref, dg2_ref, dbg_ref, dal_ref, ddtb_ref, acc_ref):
        h = pl.program_id(0)
        _, vjp = jax.vjp(lambda a, b, c: _gates_fn(a, b, c, h), bg_ref[...], al_ref[...], dtb_ref[...])
        dbg, dal, ddtb = vjp((dbeta_ref[...], dg1_ref[...] + dg2_ref[...]))

        @pl.when(h == 0)
        def _():
            acc_ref[...] = jnp.zeros_like(acc_ref)

        acc_ref[...] += dbg
        dbg_ref[...] = acc_ref[...].astype(BF16)
        dal_ref[...] = jnp.broadcast_to(jnp.sum(dal, axis=-1, keepdims=True), (1, HD))
        ddtb_ref[...] = jnp.broadcast_to(jnp.sum(ddtb, axis=-1, keepdims=True), (1, HD))

    return pl.pallas_call(
        body, name="gates_bwd", grid=(B_HEADS,), in_specs=[bg, par, par, out, out, out], out_specs=[acc, par, par],
        out_shape=[jax.ShapeDtypeStruct((T, HD), BF16)] + [jax.ShapeDtypeStruct((B_HEADS, 1, HD), F32)] * 2,
        scratch_shapes=[pltpu.VMEM((T, HD), F32)], compiler_params=_cparams(("arbitrary",)),
    )(p2, al, dtb, dbeta, dg1, dg2)


def _unit_lower_inverse(a):
    eye = (_iota(a.shape, 1) == _iota(a.shape, 2)).astype(F32)
    x = eye - a
    p = _mm(a, a, hi=True)
    for i in range(5):
        x = x + _mm(x, p, hi=True)
        if i < 4:
            p = _mm(p, p, hi=True)
    return x


_WY_CH = 16
_WY_ROWS = _WY_CH * B_CHUNK


def _wy_fn(q, k, v, beta, g):
    sh = (q.shape[0] // B_CHUNK, B_CHUNK, HD)
    q3, k3, v3, b3, g3 = (t.reshape(sh) for t in (q, k, v, beta, g))
    gd = g3[:, :, :B_CHUNK] - jnp.swapaxes(g3, 1, 2)[:, :B_CHUNK, :]
    ii, jj = _iota(gd.shape, 1), _iota(gd.shape, 2)
    decay = jnp.exp(jnp.where(ii >= jj, gd, -jnp.inf))
    kb = k3 * b3
    a = _mm(kb, k3, tb=True) * jnp.where(ii > jj, decay, 0.0)
    tinv = _unit_lower_inverse(a)
    u = _mm(tinv, v3 * b3, hi=True)
    w = _mm(tinv, kb * jnp.exp(g3), hi=True)
    attn = _mm(q3, k3, tb=True) * decay
    return u.reshape(q.shape), w.reshape(q.shape), attn


def _wy_specs():
    hcol = pl.BlockSpec((_WY_ROWS, HD), lambda h, i: (i, h))
    hb = pl.BlockSpec((None, _WY_ROWS, HD), lambda h, i: (h, i, 0))
    at = pl.BlockSpec((None, _WY_CH, B_CHUNK, B_CHUNK), lambda h, i: (h, i, 0, 0))
    return hcol, hb, at


_WY_GRID = (B_HEADS, _NCH // _WY_CH)


def wy_fwd(q, k, v, beta, g):
    hcol, hb, at = _wy_specs()

    def body(q_ref, k_ref, v_ref, b_ref, g_ref, u_ref, w_ref, a_ref):
        u, w, a = _wy_fn(q_ref[...], k_ref[...], v_ref[...], b_ref[...], g_ref[...])
        u_ref[...] = u
        w_ref[...] = w
        a_ref[...] = a

    return pl.pallas_call(
        body, name="wy_fwd", grid=_WY_GRID, in_specs=[hcol, hcol, hcol, hb, hb], out_specs=[hcol, hcol, at],
        out_shape=[jax.ShapeDtypeStruct((T, B_WIDTH), F32)] * 2 + [jax.ShapeDtypeStruct((B_HEADS, _NCH, B_CHUNK, B_CHUNK), F32)],
        compiler_params=_cparams(("arbitrary", "arbitrary")),
    )(q, k, v, beta, g)


def wy_bwd(q, k, v, beta, g, du, dw, dattn):
    hcol, hb, at = _wy_specs()

    def body(q_ref, k_ref, v_ref, b_ref, g_ref, du_ref, dw_ref, da_ref, dq_ref, dk_ref, dv_ref, db_ref, dg_ref):
        _, vjp = jax.vjp(_wy_fn, q_ref[...], k_ref[...], v_ref[...], b_ref[...], g_ref[...])
        for r, t in zip((dq_ref, dk_ref, dv_ref, db_ref, dg_ref), vjp((du_ref[...], dw_ref[...], da_ref[...]))):
            r[...] = t

    return pl.pallas_call(
        body, name="wy_bwd", grid=_WY_GRID, in_specs=[hcol, hcol, hcol, hb, hb, hcol, hcol, at],
        out_specs=[hcol, hcol, hcol, hb, hb],
        out_shape=[jax.ShapeDtypeStruct((T, B_WIDTH), F32)] * 3 + [jax.ShapeDtypeStruct((B_HEADS, T, HD), F32)] * 2,
        compiler_params=_cparams(("arbitrary", "arbitrary")),
    )(q, k, v, beta, g, du, dw, dattn)


def _scan_step_fn(q, k, u, w, g, attn, gate, og, s):
    v_new = u - _mm(w, s)
    o = _mm(q * jnp.exp(g), s) + _mm(attn, v_new)
    g_last = jnp.sum(jnp.where(_iota(g.shape, 0) == B_CHUNK - 1, g, 0.0), axis=0, keepdims=True)
    s_new = s * jnp.exp(g_last) + _mm(k * jnp.exp(g_last - g), v_new, ta=True)
    return _rms(o, og) * _silu(gate), s_new


def _scan_specs(rev):
    ch = (lambda n: _NCH - 1 - n) if rev else (lambda n: n)
    rows = pl.BlockSpec((B_CHUNK, B_WIDTH), lambda n: (ch(n), 0))
    gb = pl.BlockSpec((B_HEADS, B_CHUNK, HD), lambda n: (0, ch(n), 0))
    at = pl.BlockSpec((B_HEADS, None, B_CHUNK, B_CHUNK), lambda n: (0, ch(n), 0, 0))
    og = pl.BlockSpec((1, HD), lambda n: (0, 0))
    st = pl.BlockSpec((None, B_HEADS, HD, HD), lambda n: (ch(n), 0, 0, 0))
    return rows, gb, at, og, st


def scan_fwd(q, k, u, w, g, attn, gate, og):
    rows, gb, at, ogs, st = _scan_specs(False)

    def body(q_ref, k_ref, u_ref, w_ref, g_ref, a_ref, gate_ref, og_ref, y_ref, st_ref, s_ref):
        @pl.when(pl.program_id(0) == 0)
        def _():
            s_ref[...] = jnp.zeros_like(s_ref)

        for h in range(B_HEADS):
            c = slice(h * HD, (h + 1) * HD)
            s = s_ref[h]
            st_ref[h] = s
            y, s_new = _scan_step_fn(q_ref[:, c], k_ref[:, c], u_ref[:, c], w_ref[:, c], g_ref[h], a_ref[h],
                                     gate_ref[:, c], og_ref[...], s)
            y_ref[:, c] = y.astype(BF16)
            s_ref[h] = s_new

    return pl.pallas_call(
        body, name="scan_fwd", grid=(_NCH,), in_specs=[rows, rows, rows, rows, gb, at, rows, ogs], out_specs=[rows, st],
        out_shape=[jax.ShapeDtypeStruct((T, B_WIDTH), BF16), jax.ShapeDtypeStruct((_NCH, B_HEADS, HD, HD), F32)],
        scratch_shapes=[pltpu.VMEM((B_HEADS, HD, HD), F32)], compiler_params=_cparams(("arbitrary",)),
    )(q, k, u, w, g, attn, gate, og)


def scan_bwd(q, k, u, w, g, attn, gate, og, states, dmix):
    rows, gb, at, ogs, st = _scan_specs(True)
    dyb = pl.BlockSpec((B_CHUNK, HD), lambda n: (_NCH - 1 - n, 0))

    def body(q_ref, k_ref, u_ref, w_ref, g_ref, a_ref, gate_ref, og_ref, st_ref, *rest):
        dy_refs, (dq_ref, dk_ref, du_ref, dw_ref, dgate_ref, dg_ref, da_ref, dog_ref, ds_ref) = rest[:B_HEADS], rest[B_HEADS:]

        @pl.when(pl.program_id(0) == 0)
        def _():
            ds_ref[...] = jnp.zeros_like(ds_ref)
            dog_ref[...] = jnp.zeros_like(dog_ref)

        for h in range(B_HEADS):
            c = slice(h * HD, (h + 1) * HD)
            _, vjp = jax.vjp(_scan_step_fn, q_ref[:, c], k_ref[:, c], u_ref[:, c], w_ref[:, c], g_ref[h], a_ref[h],
                             gate_ref[:, c], og_ref[...], st_ref[h])
            dq, dk, du, dw, dg, da, dgate, dog, ds = vjp((dy_refs[h][...], ds_ref[h]))
            dq_ref[:, c] = dq
            dk_ref[:, c] = dk
            du_ref[:, c] = du
            dw_ref[:, c] = dw
            dgate_ref[:, c] = dgate.astype(BF16)
            dg_ref[h] = dg
            da_ref[h] = da
            dog_ref[...] += dog
            ds_ref[h] = ds

    dy_specs = [pl.BlockSpec((B_CHUNK, HD), lambda n, h=h: (_NCH - 1 - n, A_WIDTH // HD + h)) for h in range(B_HEADS)]
    return pl.pallas_call(
        body, name="scan_bwd", grid=(_NCH,),
        in_specs=[rows, rows, rows, rows, gb, at, rows, ogs, st] + dy_specs,
        out_specs=[rows] * 5 + [gb, at, ogs],
        out_shape=[jax.ShapeDtypeStruct((T, B_WIDTH), F32)] * 4 + [jax.ShapeDtypeStruct((T, B_WIDTH), BF16)]
        + [jax.ShapeDtypeStruct((B_HEADS, T, HD), F32), jax.ShapeDtypeStruct((B_HEADS, _NCH, B_CHUNK, B_CHUNK), F32),
           jax.ShapeDtypeStruct((1, HD), F32)],
        scratch_shapes=[pltpu.VMEM((B_HEADS, HD, HD), F32)], compiler_params=_cparams(("arbitrary",)),
    )(q, k, u, w, g, attn, gate, og, states, *([dmix] * B_HEADS))


def _lanes(vec):
    return jnp.broadcast_to(vec[:, None, None], (vec.shape[0], 1, HD))


def gdn_forward(p2, conv_w, a_log, dt_bias, og):
    qa, ka, va = (conv_fwd(p2, conv_w, m) for m in "qkv")
    beta, g = gates_fwd(p2, _lanes(a_log), _lanes(dt_bias))
    u, w, attn = wy_fwd(qa, ka, va, beta, g)
    gate = p2[:, P_BG:P_BB]
    y, states = scan_fwd(qa, ka, u, w, g, attn, gate, og)
    return y, (qa, ka, va, beta, g, u, w, attn, gate, states)


def gdn_backward(p2, conv_w, a_log, dt_bias, og, saved, dmix):
    qa, ka, va, beta, g, u, w, attn, gate, states = saved
    dq1, dk1, du, dw, dgate, dg1, dattn, dog = scan_bwd(qa, ka, u, w, g, attn, gate, og, states, dmix)
    dq2, dk2, dv, dbeta, dg2 = wy_bwd(qa, ka, va, beta, g, du, dw, dattn)
    dbg, dal, ddtb = gates_bwd(p2, _lanes(a_log), _lanes(dt_bias), dbeta, dg1, dg2)
    dxq, dwq = conv_bwd(p2, conv_w, [dq1, dq2], "q")
    dxk, dwk = conv_bwd(p2, conv_w, [dk1, dk2], "k")
    dxv, dwv = conv_bwd(p2, conv_w, [dv], "v")
    return [dxq, dxk, dxv, dgate, dbg], jnp.concatenate([dwq, dwk, dwv], axis=1), dal[:, 0, 0], ddtb[:, 0, 0], dog


_SLOPES = np.exp2(-8.0 * (np.arange(C_HEADS, dtype=np.float64) + 1.0) / C_HEADS).astype(np.float32)


def _alibi_slopes():
    return _lanes(jnp.asarray(_SLOPES))


_ROWS = 512
_TM = 1024
_TM_FFN = 512


def _dep_specs(dep, ngrid):
    if dep is None:
        return [], []
    return [dep], [pl.BlockSpec((8, HD), lambda *_: (0, 0))]


def rmsnorm_fwd(x, g, dep=None):
    blk = pl.BlockSpec((_ROWS, D), lambda i: (i, 0))
    deps, dspecs = _dep_specs(dep, 1)

    def body(x_ref, g_ref, *rest):
        rest[-1][...] = _rms(x_ref[...], g_ref[...]).astype(BF16)

    return pl.pallas_call(
        body, name="rmsnorm_fwd", grid=(T // _ROWS,), in_specs=[blk, pl.BlockSpec((1, D), lambda i: (0, 0))] + dspecs,
        out_specs=blk, out_shape=jax.ShapeDtypeStruct((T, D), BF16), compiler_params=_cparams(("arbitrary",)),
    )(x, g, *deps)


def rmsnorm_bwd(x, g, dh, dres):
    blk = pl.BlockSpec((_ROWS, D), lambda i: (i, 0))
    row = pl.BlockSpec((1, D), lambda i: (0, 0))

    def body(x_ref, g_ref, dh_ref, dres_ref, dx_ref, dg_ref):
        _, vjp = jax.vjp(_rms, x_ref[...], g_ref[...])
        dx, dg = vjp(dh_ref[...])
        dx_ref[...] = dres_ref[...] + dx

        @pl.when(pl.program_id(0) == 0)
        def _():
            dg_ref[...] = jnp.zeros_like(dg_ref)

        dg_ref[...] += dg

    return pl.pallas_call(
        body, name="rmsnorm_bwd", grid=(T // _ROWS,), in_specs=[blk, row, blk, blk], out_specs=[blk, row],
        out_shape=[jax.ShapeDtypeStruct((T, D), F32), jax.ShapeDtypeStruct((1, D), F32)],
        compiler_params=_cparams(("arbitrary",)),
    )(x, g, dh, dres)


def _matmul(name, a, b, *, grid, a_spec, b_spec, o_spec, out_shape, ta=False, tb=False, k_axis=None, res=None, dep=None):
    dims = _dims(2, ta, tb)
    deps, dspecs = _dep_specs(dep, len(grid))

    def body(a_ref, b_ref, *rest):
        o_ref = rest[-1]
        prod = lax.dot_general(a_ref[...].astype(BF16), b_ref[...].astype(BF16), dims, preferred_element_type=F32)
        if res is not None:
            prod = prod + rest[0][...]
        if k_axis is None:
            o_ref[...] = prod.astype(o_ref.dtype)
        else:
            @pl.when(pl.program_id(k_axis) == 0)
            def _():
                o_ref[...] = prod

            @pl.when(pl.program_id(k_axis) > 0)
            def _():
                o_ref[...] += prod

    sem = tuple("arbitrary" for _ in grid)
    ins = [a, b] + ([res] if res is not None else []) + deps
    specs = [a_spec, b_spec] + ([o_spec] if res is not None else []) + dspecs
    return pl.pallas_call(
        body, name=name, grid=grid, in_specs=specs, out_specs=o_spec, out_shape=out_shape, compiler_params=_cparams(sem),
    )(*ins)


_IN_TN = P_END // 3


def mm_proj(h1, wp_in, l):
    return _matmul(
        "mm_proj", h1, wp_in, grid=(P_END // _IN_TN, T // _TM),
        a_spec=pl.BlockSpec((_TM, D), lambda j, i: (i, 0)),
        b_spec=pl.BlockSpec((None, D, _IN_TN), lambda j, i: (l, 0, j)),
        o_spec=pl.BlockSpec((_TM, _IN_TN), lambda j, i: (i, j)), out_shape=jax.ShapeDtypeStruct((T, P_END), F32))


def mm_dh1(dp2, wp_in, l, dep=None):
    return _matmul(
        "mm_dh1", dp2, wp_in, grid=(T // _TM, P_END // _IN_TN), tb=True, k_axis=1, dep=dep,
        a_spec=pl.BlockSpec((_TM, _IN_TN), lambda i, k: (i, k)),
        b_spec=pl.BlockSpec((None, D, _IN_TN), lambda i, k: (l, 0, k)),
        o_spec=pl.BlockSpec((_TM, D), lambda i, k: (i, 0)), out_shape=jax.ShapeDtypeStruct((T, D), F32))


def mm_dwin(h1, dp2):
    return _matmul(
        "mm_dwin", h1, dp2, grid=(P_END // _IN_TN, D // _TM), ta=True,
        a_spec=pl.BlockSpec((T, _TM), lambda j, i: (0, i)),
        b_spec=pl.BlockSpec((T, _IN_TN), lambda j, i: (0, j)),
        o_spec=pl.BlockSpec((_TM, _IN_TN), lambda j, i: (i, j)), out_shape=jax.ShapeDtypeStruct((D, P_END), BF16))


def _mm_square(name, a, w, l, res, tb, dep=None):
    tn = 1024
    b_spec = (pl.BlockSpec((None, tn, D), lambda j, i: (l, j, 0)) if tb else pl.BlockSpec((None, D, tn), lambda j, i: (l, 0, j)))
    return _matmul(
        name, a, w, grid=(D // tn, T // _TM), tb=tb, res=res, dep=dep,
        a_spec=pl.BlockSpec((_TM, D), lambda j, i: (i, 0)), b_spec=b_spec,
        o_spec=pl.BlockSpec((_TM, tn), lambda j, i: (i, j)), out_shape=jax.ShapeDtypeStruct((T, D), F32))


def mm_out(mix, wg_out, l, x):
    return _mm_square("mm_out", mix, wg_out, l, x, False)


def mm_dmix(dx1, wg_out, l, dep=None):
    return _mm_square("mm_dmix", dx1, wg_out, l, None, True, dep)


def mm_dwout(mix, dx1):
    tn = 1024
    return _matmul(
        "mm_dwout", mix, dx1, grid=(D // tn, D // _TM), ta=True,
        a_spec=pl.BlockSpec((T, _TM), lambda j, i: (0, i)), b_spec=pl.BlockSpec((T, tn), lambda j, i: (0, j)),
        o_spec=pl.BlockSpec((_TM, tn), lambda j, i: (i, j)), out_shape=jax.ShapeDtypeStruct((D, D), BF16))


_GU_TN = GU_SHARD // 2


_GU_NJ = FFN // _GU_TN


def mm_dh2(dgu, wg_gu, l, dep=None):
    return _matmul(
        "mm_dh2", dgu, wg_gu, grid=(T // _TM, 2 * N_CHIPS), tb=True, k_axis=1, dep=dep,
        a_spec=pl.BlockSpec((None, _TM, _GU_TN), lambda i, k: (k // _GU_NJ, i, k % _GU_NJ)),
        b_spec=pl.BlockSpec((None, None, D, _GU_TN), lambda i, k: (l, k // 2, 0, k % 2)),
        o_spec=pl.BlockSpec((_TM, D), lambda i, k: (i, 0)), out_shape=jax.ShapeDtypeStruct((T, D), F32))


def mm_dwgu(h2, dgu):
    return _matmul(
        "mm_dwgu", h2, dgu, grid=(N_CHIPS, 2, D // _TM), ta=True,
        a_spec=pl.BlockSpec((T, _TM), lambda s, j, i: (0, i)),
        b_spec=pl.BlockSpec((None, T, _GU_TN), lambda s, j, i: ((2 * s + j) // _GU_NJ, 0, (2 * s + j) % _GU_NJ)),
        o_spec=pl.BlockSpec((None, _TM, _GU_TN), lambda s, j, i: (s, i, j)),
        out_shape=jax.ShapeDtypeStruct((N_CHIPS, D, GU_SHARD), BF16))


def mm_down(act, wg_down, l, x1):
    tn = 512
    return _matmul(
        "mm_down", act, wg_down, grid=(D // tn, T // _TM), res=x1,
        a_spec=pl.BlockSpec((_TM, FFN), lambda j, i: (i, 0)),
        b_spec=pl.BlockSpec((None, FFN, tn), lambda j, i: (l, 0, j)),
        o_spec=pl.BlockSpec((_TM, tn), lambda j, i: (i, j)), out_shape=jax.ShapeDtypeStruct((T, D), F32))


def mm_dwdown(act, dx2):
    tm, tn = DOWN_SHARD, 512
    return _matmul(
        "mm_dwdown", act, dx2, grid=(D // tn, FFN // tm), ta=True,
        a_spec=pl.BlockSpec((T, tm), lambda j, i: (0, i)), b_spec=pl.BlockSpec((T, tn), lambda j, i: (0, j)),
        o_spec=pl.BlockSpec((tm, tn), lambda j, i: (i, j)), out_shape=jax.ShapeDtypeStruct((FFN, D), BF16))


_FF_TN = 1408


def _swiglu_fn(gt, up):
    return _silu(gt) * up


def _gate_up_specs():
    gate = pl.BlockSpec((None, None, D, _FF_TN), lambda j, i: (0, j // 2, 0, j % 2))
    up = pl.BlockSpec((None, None, D, _FF_TN), lambda j, i: (0, N_CHIPS // 2 + j // 2, 0, j % 2))
    both = pl.BlockSpec((2, _TM_FFN, _FF_TN), lambda j, i: (0, i, j))
    return gate, up, both


def mm_gu_swiglu(h2, wg_gu):
    gate, up, both = _gate_up_specs()

    def body(h_ref, wg_ref, wu_ref, gu_ref, act_ref):
        h = h_ref[...]
        gt = jnp.dot(h, wg_ref[...], preferred_element_type=F32)
        u = jnp.dot(h, wu_ref[...], preferred_element_type=F32)
        gu_ref[0] = gt
        gu_ref[1] = u
        act_ref[...] = _swiglu_fn(gt, u).astype(BF16)

    return pl.pallas_call(
        body, name="mm_gu_swiglu", grid=(FFN // _FF_TN, T // _TM_FFN),
        in_specs=[pl.BlockSpec((_TM_FFN, D), lambda j, i: (i, 0)), gate, up],
        out_specs=[both, pl.BlockSpec((_TM_FFN, _FF_TN), lambda j, i: (i, j))],
        out_shape=[jax.ShapeDtypeStruct((2, T, FFN), F32), jax.ShapeDtypeStruct((T, FFN), BF16)],
        compiler_params=_cparams(("arbitrary", "arbitrary")),
    )(h2, wg_gu, wg_gu)


def mm_dact_swiglu(dx2, wg_down, gu, dep=None):
    _, _, both = _gate_up_specs()
    deps, dspecs = _dep_specs(dep, 2)

    def body(dx_ref, w_ref, gu_ref, *rest):
        dact = lax.dot_general(dx_ref[...].astype(BF16), w_ref[...], _dims(2, False, True), preferred_element_type=F32)
        _, vjp = jax.vjp(_swiglu_fn, gu_ref[0], gu_ref[1])
        dgt, dup = vjp(dact)
        rest[-1][0] = dgt.astype(BF16)
        rest[-1][1] = dup.astype(BF16)

    return pl.pallas_call(
        body, name="mm_dact_swiglu", grid=(FFN // _FF_TN, T // _TM_FFN),
        in_specs=[pl.BlockSpec((_TM_FFN, D), lambda j, i: (i, 0)), pl.BlockSpec((None, _FF_TN, D), lambda j, i: (0, j, 0)),
                  both]
        + dspecs,
        out_specs=both, out_shape=jax.ShapeDtypeStruct((2, T, FFN), BF16),
        compiler_params=_cparams(("arbitrary", "arbitrary")),
    )(dx2, wg_down, gu, *deps)


def loss_and_grad(y, target):
    blk = pl.BlockSpec((_ROWS, D), lambda i: (i, 0))
    acc = pl.BlockSpec((8, HD), lambda i: (0, 0))

    def body(y_ref, t_ref, dy_ref, l_ref):
        err = y_ref[...] - t_ref[...]
        dy_ref[...] = err * (1.0 / D)

        @pl.when(pl.program_id(0) == 0)
        def _():
            l_ref[...] = jnp.zeros_like(l_ref)

        l_ref[...] += (0.5 / D) * jnp.sum(err * err)

    return pl.pallas_call(
        body, name="loss_and_grad", grid=(T // _ROWS,), in_specs=[blk, blk], out_specs=[blk, acc],
        out_shape=[jax.ShapeDtypeStruct((T, D), F32), jax.ShapeDtypeStruct((8, HD), F32)],
        compiler_params=_cparams(("arbitrary",)),
    )(y, target)


def adamw(w, g, m, v, name):
    rows, cols = w.shape
    tr = _ROWS if rows % _ROWS == 0 else rows
    blk = pl.BlockSpec((tr, cols), lambda i: (i, 0))

    def body(w_ref, g_ref, m_ref, v_ref, d_ref, nm_ref, nv_ref):
        gg = g_ref[...]
        nm = ADAM_B1 * m_ref[...] + (1.0 - ADAM_B1) * gg
        nv = ADAM_B2 * v_ref[...] + (1.0 - ADAM_B2) * (gg * gg)
        m_hat = nm / (1.0 - ADAM_B1 ** ADAM_STEP)
        v_hat = nv / (1.0 - ADAM_B2 ** ADAM_STEP)
        d_ref[...] = -ADAM_LR * (m_hat / (jnp.sqrt(v_hat) + ADAM_EPS) + ADAM_WD * w_ref[...])
        nm_ref[...] = nm
        nv_ref[...] = nv

    return pl.pallas_call(
        body, name=name, grid=(rows // tr,), in_specs=[blk] * 4, out_specs=[blk] * 3,
        out_shape=[jax.ShapeDtypeStruct(w.shape, F32)] * 3, compiler_params=_cparams(("arbitrary",)),
    )(w, g, m, v)


_LANE = 128


def _segment_of_shard_column():
    flat = np.full(P_END, -1, np.int64)
    for o in range(P_END):
        if GATE_COLS <= o < P_CQ:
            continue
        c = o if o < GATE_COLS else o - (P_CQ - GATE_COLS)
        flat[o] = (c // IN_SHARD) * IN_SHARD_PAD + c % IN_SHARD
    return flat


def _block_pairs(src_of_dst):
    return [sorted({int(c) // _LANE for c in src_of_dst[db * _LANE:(db + 1) * _LANE] if c >= 0})
            for db in range(len(src_of_dst) // _LANE)]


_RELAYOUT_ROWS = 512
_SHARD_BLOCKS = IN_SHARD_PAD // _LANE


def _relayout(name, x, to_segments):
    seg_of = _segment_of_shard_column()
    if to_segments:
        src_of_dst = seg_of
    else:
        src_of_dst = np.full(N_CHIPS * IN_SHARD_PAD, -1, np.int64)
        src_of_dst[seg_of[seg_of >= 0]] = np.nonzero(seg_of >= 0)[0]
    sources = _block_pairs(src_of_dst)
    n_dst = len(sources)
    col_map = jnp.asarray(src_of_dst.reshape(n_dst, 1, _LANE), jnp.int32)
    shard_blk = pl.BlockSpec((N_CHIPS, _RELAYOUT_ROWS, IN_SHARD_PAD), lambda i: (0, i, 0))
    seg_blk = pl.BlockSpec((_RELAYOUT_ROWS, P_END), lambda i: (i, 0))

    def shard_cols(ref, b):
        return ref.at[b // _SHARD_BLOCKS, :, pl.ds((b % _SHARD_BLOCKS) * _LANE, _LANE)]

    def seg_cols(ref, b):
        return ref.at[:, pl.ds(b * _LANE, _LANE)]

    src_cols, dst_cols = (shard_cols, seg_cols) if to_segments else (seg_cols, shard_cols)

    def body(x_ref, map_ref, o_ref):
        src_row = _iota((_LANE, _LANE), 0)
        for d in range(n_dst):
            acc = jnp.zeros((_RELAYOUT_ROWS, _LANE), F32)
            for sb in sources[d]:
                sel = (src_row + sb * _LANE == map_ref[d]).astype(x_ref.dtype)
                acc = acc + jnp.dot(src_cols(x_ref, sb)[...], sel, preferred_element_type=F32)
            dst_cols(o_ref, d)[...] = acc.astype(o_ref.dtype)

    rows = x.shape[-2]
    out_shape = (rows, P_END) if to_segments else (N_CHIPS, rows, IN_SHARD_PAD)
    return pl.pallas_call(
        body, name=name, grid=(rows // _RELAYOUT_ROWS,),
        in_specs=[shard_blk if to_segments else seg_blk, pl.BlockSpec(col_map.shape, lambda i: (0, 0, 0))],
        out_specs=seg_blk if to_segments else shard_blk, out_shape=jax.ShapeDtypeStruct(out_shape, x.dtype),
        compiler_params=_cparams(("arbitrary",)),
    )(x, col_map)


def shards_to_segments(w):
    return _relayout("shards_to_segments", w, True)


def segments_to_shards(w):
    return _relayout("segments_to_shards", w, False)


def mixers_forward(x, w_in, sp, dep=None, h1=None, midway=None):
    if h1 is None:
        h1 = rmsnorm_fwd(x, sp["norm1_g"], dep)
    p2 = mm_proj(h1, w_in, 0)
    y_a = sgu_fwd(p2, sp["sgu_norm_g"], sp["w_spatial"], sp["b_spatial"])
    y_b, saved_b = gdn_forward(p2, sp["conv_w"], sp["a_log"], sp["dt_bias"], sp["o_norm_g"])
    tok = None if midway is None else midway(y_b)
    y_c, outs_c, lses_c = dattn_fwd(p2, sp["q_norm_g"], sp["k_norm_g"], _alibi_slopes(), tok)
    mix = jnp.concatenate([y_a, y_b, y_c], axis=1)
    return mix, (x, h1, p2, saved_b, (outs_c, lses_c), mix)


def ffn_up(x, mix, w_out, w_gu, sp):
    x1 = mm_out(mix, w_out, 0, x)
    h2 = rmsnorm_fwd(x1, sp["norm2_g"])
    gu, act = mm_gu_swiglu(h2, w_gu)
    return x1, h2, gu, act


def ffn_forward(x, mix, wg, sp):
    x1, h2, gu, act = ffn_up(x, mix, wg["out"], wg["gu"], sp)
    x2 = mm_down(act, wg["down"], 0, x1)
    return x2, (x1, h2, gu, act)


def ffn_backward(dx2, wg, sp, saved, dep=None, on_weight_grads=None):
    x1, h2, gu, act = saved
    dgu = mm_dact_swiglu(dx2, wg["down"], gu, dep)
    dw_down = mm_dwdown(act, dx2)
    dw_gu = mm_dwgu(h2, dgu)
    tok = None if on_weight_grads is None else on_weight_grads(dw_gu, dw_down)
    dh2 = mm_dh2(dgu, wg["gu"], 0, tok)
    dx1, dnorm2 = rmsnorm_bwd(x1, sp["norm2_g"], dh2, dx2)
    return dx1, dnorm2, dw_gu, dw_down


def mixers_backward(dx1, wg, sp, saved, dep=None, on_weight_grads=None):
    x, h1, p2, saved_b, saved_c, mix = saved
    dmix = mm_dmix(dx1, wg["out"], 0, dep)
    dw_out = mm_dwout(mix, dx1)
    du, dv, dsg, dws, dbs = sgu_bwd(p2, sp["sgu_norm_g"], sp["w_spatial"], sp["b_spatial"], dmix)
    dseg_b, dconv, dal, ddtb, dog = gdn_backward(p2, sp["conv_w"], sp["a_log"], sp["dt_bias"], sp["o_norm_g"], saved_b, dmix)
    dcq, dck, dcv, dqg, dkg = dattn_bwd(p2, sp["q_norm_g"], sp["k_norm_g"], _alibi_slopes(), *saved_c, dmix)
    dp2 = jnp.concatenate([du, dv] + dseg_b + [dcq, dck, dcv], axis=1)
    dw_in = segments_to_shards(mm_dwin(h1, dp2))
    tok = None if on_weight_grads is None else on_weight_grads(dw_in, dw_out)
    dh1 = mm_dh1(dp2, wg["in"], 0, tok)
    dx, dnorm1 = rmsnorm_bwd(x, sp["norm1_g"], dh1, dx1)
    small = {"norm1_g": dnorm1, "sgu_norm_g": dsg, "w_spatial": dws, "b_spatial": dbs, "conv_w": dconv, "a_log": dal,
             "dt_bias": ddtb, "o_norm_g": dog, "q_norm_g": dqg, "k_norm_g": dkg}
    return dx, dw_in, dw_out, small


_HBM = pl.BlockSpec(memory_space=pltpu.HBM)
_MESH = pl.DeviceIdType.MESH


def _place():
    x, y, c = lax.axis_index("x"), lax.axis_index("y"), lax.axis_index("c")
    chips = [(1 - x, y), (x, 1 - y), (1 - x, 1 - y)]
    return x, y, c, chips


def _rcopy(src, dst, ssem, rsem, dev):
    return pltpu.make_async_remote_copy(src_ref=src, dst_ref=dst, send_sem=ssem, recv_sem=rsem, device_id=dev,
                                        device_id_type=_MESH)


_SEM = pl.BlockSpec(memory_space=pltpu.SEMAPHORE)
_SIDE_EFFECT = pltpu.SideEffectType.DATAFLOW_SIDE_EFFECTING


def _in_hbm(a):
    return pltpu.with_memory_space_constraint(a, pltpu.HBM)


def _split_copy(name, srcs, land_shapes, n_sems, copies):
    n, m = len(srcs), len(land_shapes)
    thru = [pltpu.HBM(a.shape, a.dtype) for a in srcs] + [pltpu.HBM(s.shape, s.dtype) for s in land_shapes]
    sems = (pltpu.SemaphoreType.DMA((n_sems,)), pltpu.SemaphoreType.DMA((n_sems,)))

    def start(dep=None):
        deps = [] if dep is None else [dep]

        def body(*refs):
            ins, lands = refs[:n], refs[n:n + m]
            ssem, rsem, token = refs[n + m + len(deps)], refs[n + m + len(deps) + 1], refs[-1]
            for cp in copies(ins, lands, ssem, rsem)[0]:
                cp.start()
            token[...] = jnp.zeros_like(token)

        out = pl.pallas_call(
            body, name=name + "_start", out_shape=(*sems, *thru, jax.ShapeDtypeStruct((8, HD), F32)),
            in_specs=[_HBM] * (n + m) + [pl.BlockSpec(memory_space=pl.ANY)] * len(deps),
            out_specs=(_SEM, _SEM, *[_HBM] * (n + m), pl.BlockSpec(memory_space=pltpu.VMEM)),
            input_output_aliases={i: 2 + i for i in range(n + m)},
            compiler_params=pltpu.CompilerParams(has_side_effects=_SIDE_EFFECT),
        )(*[_in_hbm(a) for a in srcs], *[_in_hbm(lax.empty(s.shape, s.dtype)) for s in land_shapes], *deps)
        return out[:-1], out[-1]

    def wait(state, after):
        def body(*refs):
            ins, lands, ssem, rsem = refs[:n], refs[n:n + m], refs[n + m], refs[n + m + 1]
            sent, arrivals = copies(ins, lands, ssem, rsem)
            for cp in sent:
                cp.wait_send()
            for cp in arrivals:
                cp.wait_recv()

        out = pl.pallas_call(
            body, name=name + "_wait", out_shape=tuple(thru),
            in_specs=[_HBM] * (n + m) + [_SEM, _SEM, pl.BlockSpec(memory_space=pl.ANY)], out_specs=[_HBM] * (n + m),
            input_output_aliases={i: i for i in range(n + m)},
            compiler_params=pltpu.CompilerParams(has_side_effects=_SIDE_EFFECT),
        )(*state[2:], state[0], state[1], after)
        return list(out[:n]), list(out[n:])

    return start, wait


def gather_direct(shards, tag):
    n = len(shards)

    def copies(ins, lands, ssem, rsem):
        x, y, c, chips = _place()
        s = 2 * x + y
        sibling = (x, y, 1 - c)
        sent, arrivals = [], []
        for a in range(n):
            for u in range(2):
                cp = _rcopy(ins[a].at[u], lands[a].at[s, u], ssem.at[5 * a + u], rsem.at[5 * a + u], sibling)
                sent.append(cp)
                arrivals.append(cp)
            for j, (cx, cy) in enumerate(chips):
                k = 5 * a + 2 + j
                sent.append(_rcopy(ins[a].at[c], lands[a].at[s, c], ssem.at[k], rsem.at[k], (cx, cy, c)))
                arrivals.append(_rcopy(ins[a].at[c], lands[a].at[2 * cx + cy, c], ssem.at[k], rsem.at[k], (cx, cy, c)))
        return sent, arrivals

    lands = [jax.ShapeDtypeStruct((N_CHIPS,) + w.shape, w.dtype) for w in shards]
    return _split_copy("gather_direct_" + tag, shards, lands, 5 * n, copies)


def pass_to_sibling(lands):
    n = len(lands)

    def body(*refs):
        ins = refs[:n]
        ssem, rsem = refs[2 * n:]
        x, y, c, chips = _place()
        sibling = (x, y, 1 - c)
        cps, arrivals = [], []
        for a in range(n):
            for j, (cx, cy) in enumerate(chips):
                t = 2 * cx + cy
                cps.append(_rcopy(ins[a].at[t, c], ins[a].at[t, c], ssem.at[a, j], rsem.at[a, j], sibling))
                arrivals.append(_rcopy(ins[a].at[t, c], ins[a].at[t, 1 - c], ssem.at[a, j], rsem.at[a, j], sibling))
        for cp in cps:
            cp.start()
        for cp, ar in zip(cps, arrivals):
            cp.wait_send()
            ar.wait_recv()

    return pl.pallas_call(
        body, name="pass_to_sibling", in_specs=[_HBM] * n, out_specs=[_HBM] * n,
        out_shape=[jax.ShapeDtypeStruct(a.shape, a.dtype) for a in lands], input_output_aliases={a: a for a in range(n)},
        scratch_shapes=[pltpu.SemaphoreType.DMA((n, 3)), pltpu.SemaphoreType.DMA((n, 3))],
    )(*lands)


def pass_direct(lands, tag):
    n = len(lands)

    def copies(ins, _, ssem, rsem):
        x, y, c, chips = _place()
        sibling = (x, y, 1 - c)
        sent, arrivals = [], []
        for a in range(n):
            for j, (cx, cy) in enumerate(chips):
                t, k = 2 * cx + cy, 3 * a + j
                sent.append(_rcopy(ins[a].at[t, c], ins[a].at[t, c], ssem.at[k], rsem.at[k], sibling))
                arrivals.append(_rcopy(ins[a].at[t, c], ins[a].at[t, 1 - c], ssem.at[k], rsem.at[k], sibling))
        return sent, arrivals

    return _split_copy("pass_" + tag, lands, [], 3 * n, copies)


def exchange_halves(grads, tag):
    n = len(grads)

    def copies(ins, lands, ssem, rsem):
        x, y, c, _ = _place()
        cps = []
        for a in range(n):
            h = grads[a].shape[1] // 2
            cps.append(_rcopy(ins[a].at[:, pl.ds((1 - c) * h, h)], lands[a], ssem.at[a], rsem.at[a], (x, y, 1 - c)))
        return cps, cps

    lands = [jax.ShapeDtypeStruct((g.shape[0], g.shape[1] // 2, g.shape[2]), g.dtype) for g in grads]
    return _split_copy("exchange_halves_" + tag, grads, lands, n, copies)


def scatter_direct(parts, tag):
    n = len(parts)

    def copies(ins, lands, ssem, rsem):
        x, y, c, chips = _place()
        cps = [_rcopy(ins[a].at[2 * cx + cy], lands[a].at[j], ssem.at[3 * a + j], rsem.at[3 * a + j], (cx, cy, c))
               for a in range(n) for j, (cx, cy) in enumerate(chips)]
        return cps, cps

    lands = [jax.ShapeDtypeStruct((3,) + p.shape[1:], p.dtype) for p in parts]
    return _split_copy("scatter_direct_" + tag, parts, lands, 3 * n, copies)


def share_halves(halves):
    n = len(halves)

    def body(*refs):
        ins, outs = refs[:n], refs[n:2 * n]
        ssem, rsem = refs[2 * n:]
        x, y, c, _ = _place()
        cps = [_rcopy(ins[i], outs[i], ssem.at[i], rsem.at[i], (x, y, 1 - c)) for i in range(n)]
        for cp in cps:
            cp.start()
        for cp in cps:
            cp.wait()

    return pl.pallas_call(
        body, name="share_halves", in_specs=[_HBM] * n, out_specs=[_HBM] * n,
        out_shape=[jax.ShapeDtypeStruct(h.shape, h.dtype) for h in halves],
        scratch_shapes=[pltpu.SemaphoreType.DMA((n,)), pltpu.SemaphoreType.DMA((n,))],
    )(*halves)


_ADAMW_BLOCK_BYTES = 3 << 19


def adamw_shard(w, m, v, mine, theirs, c, name):
    _, r, cw = w.shape
    h, cg = mine[0].shape
    tr = next(t for t in (256, 176, 128) if h % t == 0 and t * cg * 4 <= _ADAMW_BLOCK_BYTES)
    nb = h // tr
    wblk = pl.BlockSpec((None, tr, cw), lambda l, i, c_ref: (l, i, 0))
    gblk = lambda layer, own: pl.BlockSpec((tr, cg), lambda l, i, c_ref: (_held_block(l, i, c_ref, layer, own, nb), 0))
    return _adamw_halves(w, m, v, mine, theirs, c, name, (DEPTH, r // tr), wblk, gblk, nb, cw)


def _held_block(l, i, c_ref, layer, own, nb):
    in_use = (l == layer) & (((i // nb) == c_ref[0]) == own)
    return jnp.where(in_use, i % nb, 0)


def _adamw_halves(w, m, v, mine, theirs, c, name, grid, wblk, gblk, nb, cw):
    def body(c_ref, w_ref, m_ref, v_ref, m0, m1, t0, t1, g_ref, d_ref, nm_ref, nv_ref):
        is_mine = (pl.program_id(1) // nb) == c_ref[0]
        first = pl.program_id(0) == 0
        gg = jnp.where(is_mine, jnp.where(first, m0[:, :cw], m1[:, :cw]), jnp.where(first, t0[:, :cw], t1[:, :cw]))
        nm = ADAM_B1 * m_ref[...] + (1.0 - ADAM_B1) * gg
        nv = ADAM_B2 * v_ref[...] + (1.0 - ADAM_B2) * (gg * gg)
        m_hat = nm / (1.0 - ADAM_B1 ** ADAM_STEP)
        v_hat = nv / (1.0 - ADAM_B2 ** ADAM_STEP)
        g_ref[...] = gg
        d_ref[...] = -ADAM_LR * (m_hat / (jnp.sqrt(v_hat) + ADAM_EPS) + ADAM_WD * w_ref[...])
        nm_ref[...] = nm
        nv_ref[...] = nv

    return pl.pallas_call(
        body, name=name,
        grid_spec=pltpu.PrefetchScalarGridSpec(
            num_scalar_prefetch=1, grid=grid,
            in_specs=[wblk] * 3 + [gblk(0, True), gblk(1, True), gblk(0, False), gblk(1, False)], out_specs=[wblk] * 4),
        out_shape=[jax.ShapeDtypeStruct(w.shape, F32)] * 4, compiler_params=_cparams(("arbitrary", "arbitrary")),
    )(c, w, m, v, mine[0], mine[1], theirs[0], theirs[1])


def adamw_shard_t(wt, mt, vt, mine_t, theirs_t, c, name):
    _, cw, r = wt.shape
    h = mine_t[0].shape[1]
    tc = 256
    nb = h // tc
    wblk = pl.BlockSpec((None, cw, tc), lambda l, j, c_ref: (l, 0, j))
    gblk = lambda layer, own: pl.BlockSpec((cw, tc), lambda l, j, c_ref: (0, _held_block(l, j, c_ref, layer, own, nb)))
    return _adamw_halves(wt, mt, vt, mine_t, theirs_t, c, name, (DEPTH, r // tc), wblk, gblk, nb, cw)


def _half_rows(h, cols):
    for tr in (512, 256, 352, 128, 64):
        if h % tr == 0 and tr * cols * 4 <= 6 * 1024 * 1024:
            return tr
    raise ValueError((h, cols))


def add_sibling(grad, recv, c):
    _, r, cols = grad.shape
    h = r // 2
    tr = _half_rows(h, cols)
    nb = h // tr

    def body(c_ref, g_ref, r_ref, o_ref):
        o_ref[...] = (g_ref[...].astype(F32) + r_ref[...].astype(F32)).astype(BF16)

    return pl.pallas_call(
        body, name="add_sibling",
        grid_spec=pltpu.PrefetchScalarGridSpec(
            num_scalar_prefetch=1, grid=(N_CHIPS, nb),
            in_specs=[pl.BlockSpec((None, tr, cols), lambda t, i, c_ref: (t, c_ref[0] * nb + i, 0)),
                      pl.BlockSpec((None, tr, cols), lambda t, i, c_ref: (t, i, 0))],
            out_specs=pl.BlockSpec((None, tr, cols), lambda t, i, c_ref: (t, i, 0))),
        out_shape=jax.ShapeDtypeStruct((N_CHIPS, h, cols), BF16), compiler_params=_cparams(("arbitrary", "arbitrary")),
    )(c, grad, recv)


def add_chips(part, recv, s):
    _, h, cols = part.shape
    tr = _half_rows(h, cols)

    def body(s_ref, p_ref, r_ref, o_ref):
        o_ref[...] = ((p_ref[...].astype(F32) + r_ref[0].astype(F32)) + r_ref[1].astype(F32)) + r_ref[2].astype(F32)

    return pl.pallas_call(
        body, name="add_chips",
        grid_spec=pltpu.PrefetchScalarGridSpec(
            num_scalar_prefetch=1, grid=(h // tr,),
            in_specs=[pl.BlockSpec((None, tr, cols), lambda i, s_ref: (s_ref[0], i, 0)),
                      pl.BlockSpec((3, tr, cols), lambda i, s_ref: (0, i, 0))],
            out_specs=pl.BlockSpec((tr, cols), lambda i, s_ref: (i, 0))),
        out_shape=jax.ShapeDtypeStruct((h, cols), F32), compiler_params=_cparams(("arbitrary",)),
    )(s, part, recv)


def allreduce_small(vec):
    rows = vec.shape[0]

    def body(v_ref, o_ref, buf, ssem, rsem, lsem):
        x, y, c, chips = _place()
        me, sibling = (x, y, c), (x, y, 1 - c)

        def blk(px, py, pc):
            return buf.at[4 * px + 2 * py + pc]

        def copy(k, block, to, src=None):
            return _rcopy(blk(*block) if src is None else src, blk(*block), ssem.at[k], rsem.at[k], to)

        mine = pltpu.make_async_copy(v_ref, blk(*me), lsem)
        mine.start()
        first = [copy(0, me, sibling, src=v_ref)] + [copy(1 + j, me, (*chip, c), src=v_ref) for j, chip in enumerate(chips)]
        for cp in first:
            cp.start()
        passed = [copy(4 + j, (*chip, c), sibling) for j, chip in enumerate(chips)]
        for j, chip in enumerate(chips):
            copy(1 + j, (*chip, c), me).wait_recv()
            passed[j].start()
        copy(0, sibling, me).wait_recv()
        for j, chip in enumerate(chips):
            copy(4 + j, (*chip, 1 - c), me).wait_recv()
        for cp in first + passed:
            cp.wait_send()
        mine.wait()
        acc = buf[0]
        for d in range(1, N_DEV):
            acc = acc + buf[d]
        o_ref[...] = acc

    vm = pl.BlockSpec(memory_space=pltpu.VMEM)
    return pl.pallas_call(
        body, name="allreduce_small", in_specs=[vm], out_specs=vm, out_shape=jax.ShapeDtypeStruct(vec.shape, F32),
        scratch_shapes=[pltpu.VMEM((N_DEV, rows, HD), F32), pltpu.SemaphoreType.DMA((7,)), pltpu.SemaphoreType.DMA((7,)),
                        pltpu.SemaphoreType.DMA],
        compiler_params=pltpu.CompilerParams(vmem_limit_bytes=VMEM_LIMIT),
    )(vec)


SMALL_NAMES = ("norm1_g", "sgu_norm_g", "w_spatial", "b_spatial", "conv_w", "a_log", "dt_bias", "o_norm_g", "q_norm_g",
               "k_norm_g", "norm2_g")


def small_params(l, p, conv_full):
    return {"norm1_g": p["norm1_g"][l][None], "sgu_norm_g": p["sgu_norm_g"][l][:, None, :], "w_spatial": p["w_spatial"][l],
            "b_spatial": p["b_spatial"][l][..., None], "conv_w": conv_full[l], "a_log": p["a_log"][l], "dt_bias": p["dt_bias"][l],
            "o_norm_g": p["o_norm_g"][l][None], "q_norm_g": p["q_norm_g"][l][None], "k_norm_g": p["k_norm_g"][l][None],
            "norm2_g": p["norm2_g"][l][None]}


_PACK_TILE = 8 * HD


def _pack(arrays):
    flat = jnp.concatenate([a.reshape(-1) for a in arrays])
    pad = -flat.shape[0] % _PACK_TILE
    return jnp.pad(flat, (0, pad)).reshape(-1, HD)


def _unpack(packed, shapes):
    flat, out, off = packed.reshape(-1), [], 0
    for shp in shapes:
        n = int(np.prod(shp))
        out.append(flat[off:off + n].reshape(shp))
        off += n
    return out


WEIGHT_ORDER = ("norm1_g", "w_in", "sgu_norm_g", "w_spatial", "b_spatial", "conv_w", "a_log", "dt_bias", "o_norm_g", "q_norm_g",
                "k_norm_g", "w_out", "norm2_g", "w_gate_up", "w_down")


def kernel(x, norm1_g, w_in, sgu_norm_g, w_spatial, b_spatial, conv_w, a_log, dt_bias, o_norm_g, q_norm_g, k_norm_g, w_out, norm2_g, w_gate_up, w_down, loss_target, m_norm1_g, m_w_in, m_sgu_norm_g, m_w_spatial, m_b_spatial, m_conv_w, m_a_log, m_dt_bias, m_o_norm_g, m_q_norm_g, m_k_norm_g, m_w_out, m_norm2_g, m_w_gate_up, m_w_down, v_norm1_g, v_w_in, v_sgu_norm_g, v_w_spatial, v_b_spatial, v_conv_w, v_a_log, v_dt_bias, v_o_norm_g, v_q_norm_g, v_k_norm_g, v_w_out, v_norm2_g, v_w_gate_up, v_w_down):
    w = dict(norm1_g=norm1_g, w_in=w_in, sgu_norm_g=sgu_norm_g, w_spatial=w_spatial, b_spatial=b_spatial, conv_w=conv_w,
             a_log=a_log, dt_bias=dt_bias, o_norm_g=o_norm_g, q_norm_g=q_norm_g, k_norm_g=k_norm_g, w_out=w_out,
             norm2_g=norm2_g, w_gate_up=w_gate_up, w_down=w_down)
    m = dict(norm1_g=m_norm1_g, w_in=m_w_in, sgu_norm_g=m_sgu_norm_g, w_spatial=m_w_spatial, b_spatial=m_b_spatial,
             conv_w=m_conv_w, a_log=m_a_log, dt_bias=m_dt_bias, o_norm_g=m_o_norm_g, q_norm_g=m_q_norm_g, k_norm_g=m_k_norm_g,
             w_out=m_w_out, norm2_g=m_norm2_g, w_gate_up=m_w_gate_up, w_down=m_w_down)
    v = dict(norm1_g=v_norm1_g, w_in=v_w_in, sgu_norm_g=v_sgu_norm_g, w_spatial=v_w_spatial, b_spatial=v_b_spatial,
             conv_w=v_conv_w, a_log=v_a_log, dt_bias=v_dt_bias, o_norm_g=v_o_norm_g, q_norm_g=v_q_norm_g, k_norm_g=v_k_norm_g,
             w_out=v_w_out, norm2_g=v_norm2_g, w_gate_up=v_w_gate_up, w_down=v_w_down)
    chip = (2 * lax.axis_index("x") + lax.axis_index("y")).astype(jnp.int32)
    core = lax.axis_index("c").astype(jnp.int32)

    in_pad = IN_SHARD_PAD - IN_SHARD
    w_in_pad = jnp.pad(w_in, ((0, 0), (0, 0), (0, in_pad)))

    halves_of = lambda a: a.reshape(2, a.shape[0] // 2, a.shape[1])
    start_0, wait_0 = gather_direct([halves_of(w_in_pad[0].astype(BF16)), halves_of(conv_w[0])], "mix0")
    state_0, token_0 = start_0()
    bf_halves = lambda a: halves_of((a + token_0[0, 0]).astype(BF16))

    def ffn_shards(l):
        return [bf_halves(w_gate_up[l]), bf_halves(w_down[l]), bf_halves(w_out[l])]

    def mixer_shards(l):
        return [bf_halves(w_in_pad[l]), halves_of(conv_w[l])]

    def mixer_weights(g):
        g_in, g_conv = g
        return (shards_to_segments(g_in.reshape(N_CHIPS, D, IN_SHARD_PAD))[None],
                g_conv.reshape(N_CHIPS, B_CONV, -1).transpose(1, 0, 2).reshape(B_CONV, 3 * B_WIDTH))

    def ffn_weights(g, w_in_seg):
        g_gu, g_down, g_out = g
        return {"in": w_in_seg, "out": g_out.reshape(1, D, D), "gu": g_gu.reshape(1, N_CHIPS, D, GU_SHARD),
                "down": None if g_down is None else g_down.reshape(1, FFN, D)}

    def layer_params(l, conv_full):
        return small_params(0, {n: w[n][l:l + 1] for n in SMALL_NAMES if n != "conv_w"}, conv_full[None])

    gu0, down0, out0 = ffn_shards(0)
    start_a, wait_a = gather_direct([gu0, out0], "ffn0")
    start_b, wait_b = gather_direct([down0] + mixer_shards(1), "mid")
    start_c, wait_c = gather_direct(ffn_shards(1), "ffn1")
    state_a, token_a = start_a(token_0)
    state_b, token_b = start_b(token_a)
    state_c, token_c = start_c(token_b)
    h1_0 = rmsnorm_fwd(x[0], norm1_g[0][None], token_c)
    w_in0, conv0 = mixer_weights(pass_to_sibling(wait_0(state_0, h1_0)[1]))
    sps = [layer_params(0, conv0), None]

    def pass_midway(wait_gather, state, tag):
        cell = {}

        def midway(after):
            start, wait = pass_direct(wait_gather(state, after)[1], tag)
            pass_state, token = start()
            cell["wait"] = functools.partial(wait, pass_state)
            return token

        return midway, cell

    hook, cell_a = pass_midway(wait_a, state_a, "ffn0")
    mix0, saved_m0 = mixers_forward(x[0], w_in0, sps[0], h1=h1_0, midway=hook)
    g_gu0, g_out0 = cell_a["wait"](mix0)[0]
    wg0 = ffn_weights((g_gu0, None, g_out0), w_in0)
    x1_0, h2_0, gu_0, act_0 = ffn_up(x[0], mix0, wg0["out"], wg0["gu"], sps[0])
    g_down0, g_in1, g_conv1 = pass_to_sibling(wait_b(state_b, act_0)[1])
    wg0["down"] = g_down0.reshape(1, FFN, D)
    x1 = mm_down(act_0, wg0["down"], 0, x1_0)
    saved_f0 = (x1_0, h2_0, gu_0, act_0)
    w_in1, conv1 = mixer_weights((g_in1, g_conv1))
    sps[1] = layer_params(1, conv1)
    hook, cell_c = pass_midway(wait_c, state_c, "ffn1")
    mix1, saved_m1 = mixers_forward(x1, w_in1, sps[1], midway=hook)
    wg1 = ffn_weights(cell_c["wait"](mix1)[0], w_in1)
    x2, saved_f1 = ffn_forward(x1, mix1, wg1, sps[1])
    saved1 = (saved_m1, saved_f1)
    dx, loss_tile = loss_and_grad(x2, loss_target[0])

    def sibling_exchange(tag, by_chip):
        cell = {}

        def on_weight_grads(*dws):
            start, wait = exchange_halves([f(t) for f, t in zip(by_chip, dws)], tag)
            state, token = start()
            cell["wait"] = functools.partial(wait, state)
            return token

        return on_weight_grads, cell

    def start_scatter(cell, after, tag, dep=None):
        grads, from_sibling = cell["wait"](after)
        parts = [add_sibling(g, r, core.reshape(1)) for g, r in zip(grads, from_sibling)]
        start, wait = scatter_direct(parts, tag)
        state, token = start(dep)
        return functools.partial(wait, state), token

    smalls = [None] * DEPTH
    same = lambda t: t
    ffn_by_chip = (same, lambda t: t.reshape(N_CHIPS, DOWN_SHARD, D))
    mix_by_chip = (same, lambda t: t.reshape(N_CHIPS, OUT_SHARD, D))
    hook, cell_f1 = sibling_exchange("ffn1", ffn_by_chip)
    dx1, dnorm2_1, _, _ = ffn_backward(dx, wg1, sps[1], saved1[1], on_weight_grads=hook)
    wait_f1, tok_f1 = start_scatter(cell_f1, dx1, "ffn1")
    hook, cell_m1 = sibling_exchange("mix1", mix_by_chip)
    dx, _, _, small1 = mixers_backward(dx1, wg1, sps[1], saved1[0], dep=tok_f1, on_weight_grads=hook)
    smalls[1] = {**small1, "norm2_g": dnorm2_1}
    wait_m1, tok_m1 = start_scatter(cell_m1, dx, "mix1")
    hook, cell_f0 = sibling_exchange("ffn0", ffn_by_chip)
    dx1, dnorm2_0, _, _ = ffn_backward(dx, wg0, sps[0], saved_f0, dep=tok_m1, on_weight_grads=hook)
    wait_f0, tok_f0 = start_scatter(cell_f0, dx1, "ffn0")
    hook, cell_m0 = sibling_exchange("mix0", mix_by_chip)
    dx, _, _, small0 = mixers_backward(dx1, wg0, sps[0], saved_m0, dep=tok_f0, on_weight_grads=hook)
    smalls[0] = {**small0, "norm2_g": dnorm2_0}

    grad, delta, new_m, new_v = {}, {}, {}, {}
    stacked = [jnp.stack([smalls[l][n] for l in range(DEPTH)]) for n in SMALL_NAMES]
    total = allreduce_small(_pack(stacked + [loss_tile[0, :1]]))
    shapes = [(DEPTH, B_CONV, 3 * B_WIDTH) if n == "conv_w" else w[n].shape for n in SMALL_NAMES]
    small_grads = dict(zip(SMALL_NAMES, _unpack(total, shapes + [(1,)])[:-1]))
    loss = _unpack(total, shapes + [(1,)])[-1][0]
    conv_cols = conv_w.shape[-1]
    small_grads["conv_w"] = lax.dynamic_slice_in_dim(small_grads["conv_w"], chip * conv_cols, conv_cols, axis=2)
    grad.update(small_grads)
    sshapes = [w[n].shape for n in SMALL_NAMES]
    packed = [_pack([d[n] for n in SMALL_NAMES]) for d in (w, grad, m, v)]
    for dst, t in zip((delta, new_m, new_v), adamw(*packed, "adamw_small")):
        dst.update(zip(SMALL_NAMES, _unpack(t, sshapes)))

    wait_m0, tok_m0 = start_scatter(cell_m0, dx, "mix0", dep=total)
    (pf0, rf0), (pm1, rm1), (pf1, rf1) = (wt(tok_m0) for wt in (wait_f0, wait_m1, wait_f1))

    def reduce_group(parts, from_chips):
        mine = [add_chips(p, r, chip.reshape(1)) for p, r in zip(parts, from_chips)]
        return mine, list(share_halves(mine))

    mine_f, theirs_f = reduce_group(pf0 + pf1 + pm1, rf0 + rf1 + rm1)
    for a, n in enumerate(("w_gate_up", "w_down")):
        grad[n], delta[n], new_m[n], new_v[n] = adamw_shard(w[n], m[n], v[n], [mine_f[a], mine_f[2 + a]],
                                                            [theirs_f[a], theirs_f[2 + a]], core.reshape(1), "adamw_" + n)

    mine_m0, theirs_m0 = reduce_group(*wait_m0(new_v["w_down"]))
    tr_ = lambda t: jnp.swapaxes(t, -1, -2)
    cut = lambda t: tr_(t[:, :IN_SHARD])
    res = adamw_shard_t(tr_(w_in), tr_(m_w_in), tr_(v_w_in), [cut(mine_m0[0]), cut(mine_f[4])],
                        [cut(theirs_m0[0]), cut(theirs_f[4])], core.reshape(1), "adamw_w_in")
    grad["w_in"], delta["w_in"], new_m["w_in"], new_v["w_in"] = (tr_(t) for t in res)
    grad["w_out"], delta["w_out"], new_m["w_out"], new_v["w_out"] = adamw_shard(
        w_out, m_w_out, v_w_out, [mine_m0[1], mine_f[5]], [theirs_m0[1], theirs_f[5]], core.reshape(1), "adamw_w_out")

    out = [loss, dx[None]]
    for d in (grad, delta, new_m, new_v):
        out += [d[n] for n in WEIGHT_ORDER]
    return tuple(out)
```

```python
import functools
import math

import numpy as np
import jax
import jax.numpy as jnp
from jax import lax
from jax.experimental import pallas as pl
from jax.experimental.pallas import tpu as pltpu

F32 = jnp.float32
BF16 = jnp.bfloat16
HI = lax.Precision.HIGH

T = 2048
D = 2048
DEPTH = 2
HD = 128
A_GROUPS, A_WIDTH, A_CHUNK = 4, 512, 128
B_HEADS, B_WIDTH, B_CONV, B_CHUNK = 6, 768, 4, 64
C_HEADS, C_WIDTH, C_BLOCK = 6, 768, 128
C_BRANCHES = ((128, 1), (512, 4), (2048, 16))
FFN = 5632
IN_TOTAL = 6412
EPS = 1e-6
N_CHIPS = 4
N_DEV = 8
IN_SHARD = IN_TOTAL // N_CHIPS
IN_SHARD_PAD = 1664
GU_SHARD = 2 * FFN // N_CHIPS
OUT_SHARD = D // N_CHIPS
DOWN_SHARD = FFN // N_CHIPS
P_AU, P_AV, P_BQ, P_BK, P_BV, P_BG, P_BB, P_CQ, P_CK, P_CV, P_END = (
    0, 512, 1024, 1792, 2560, 3328, 4096, 4224, 4992, 5760, 6528)
GATE_COLS = 4108
VMEM_LIMIT = 56 * 1024 * 1024

ADAM_LR, ADAM_B1, ADAM_B2, ADAM_EPS, ADAM_WD, ADAM_STEP = 0.001, 0.9, 0.999, 1e-08, 0.01, 10


def _cparams(sem, vmem=VMEM_LIMIT):
    return pltpu.CompilerParams(dimension_semantics=sem, vmem_limit_bytes=vmem)


def _dims(nd, ta, tb):
    off = nd - 2
    ca = off + (0 if ta else 1)
    cb = off + (1 if tb else 0)
    batch = ((0,), (0,)) if nd == 3 else ((), ())
    return (((ca,), (cb,)), batch)


def _raw_mm(a, b, ta, tb, hi):
    if hi:
        return lax.dot_general(a, b, _dims(a.ndim, ta, tb), precision=HI, preferred_element_type=F32)
    return lax.dot_general(a.astype(BF16), b.astype(BF16), _dims(a.ndim, ta, tb), preferred_element_type=F32)


@functools.partial(jax.custom_vjp, nondiff_argnums=(2, 3, 4))
def _mm(a, b, ta=False, tb=False, hi=False):
    return _raw_mm(a, b, ta, tb, hi)


def _mm_fwd(a, b, ta, tb, hi):
    return _raw_mm(a, b, ta, tb, hi), (a, b)


def _mm_bwd(ta, tb, hi, res, g):
    a, b = res
    da = _raw_mm(g, b, False, not tb, False) if not ta else _raw_mm(b, g, tb, True, False)
    db = _raw_mm(a, g, not ta, False, False) if not tb else _raw_mm(g, a, True, ta, False)
    return da.astype(a.dtype), db.astype(b.dtype)


_mm.defvjp(_mm_fwd, _mm_bwd)


def _rms(x, g):
    return x * lax.rsqrt(jnp.mean(x * x, axis=-1, keepdims=True) + EPS) * g


def _gelu(x):
    return 0.5 * x * (1.0 + jnp.tanh(math.sqrt(2.0 / math.pi) * (x + 0.044715 * (x * x * x))))


def _sigmoid(x):
    return 1.0 / (1.0 + jnp.exp(-x))


def _silu(x):
    return x * _sigmoid(x)


def _softplus(x):
    return jnp.maximum(x, 0.0) + jnp.log(1.0 + jnp.exp(-jnp.abs(x)))


def _iota(shape, dim):
    return lax.broadcasted_iota(jnp.int32, shape, dim)


def _sgu_fn(u, v, sg, w, b):
    nc = T // A_CHUNK
    ug = _gelu(u)
    vn = _rms(_gelu(v), sg)
    causal = _iota((A_CHUNK, A_CHUNK), 0) >= _iota((A_CHUNK, A_CHUNK), 1)
    wm = jnp.where(causal, w, 0.0)
    wb = jnp.broadcast_to(wm[None], (nc, A_CHUNK, A_CHUNK))
    z = _mm(wb, vn.reshape(nc, A_CHUNK, HD)) + b[None]
    return ug * z.reshape(T, HD)


def _sgu_specs():
    col = lambda off: pl.BlockSpec((T, HD), lambda g, off=off: (0, off + g))
    par = [pl.BlockSpec((None, 1, HD), lambda g: (g, 0, 0)),
           pl.BlockSpec((None, A_CHUNK, A_CHUNK), lambda g: (g, 0, 0)),
           pl.BlockSpec((None, A_CHUNK, 1), lambda g: (g, 0, 0))]
    return col, par


def sgu_fwd(p2, sg, w, b):
    col, par = _sgu_specs()

    def body(u_ref, v_ref, sg_ref, w_ref, b_ref, y_ref):
        y_ref[...] = _sgu_fn(u_ref[...], v_ref[...], sg_ref[...], w_ref[...], b_ref[...]).astype(BF16)

    return pl.pallas_call(
        body, name="sgu_fwd", grid=(A_GROUPS,),
        in_specs=[col(P_AU // HD), col(P_AV // HD)] + par,
        out_specs=pl.BlockSpec((T, HD), lambda g: (0, g)),
        out_shape=jax.ShapeDtypeStruct((T, A_WIDTH), BF16),
        compiler_params=_cparams(("arbitrary",)),
    )(p2, p2, sg, w, b)


def sgu_bwd(p2, sg, w, b, dmix):
    col, par = _sgu_specs()

    def body(u_ref, v_ref, sg_ref, w_ref, b_ref, dy_ref, du_ref, dv_ref, dsg_ref, dw_ref, db_ref):
        _, vjp = jax.vjp(_sgu_fn, u_ref[...], v_ref[...], sg_ref[...], w_ref[...], b_ref[...])
        du, dv, dsg, dw, db = vjp(dy_ref[...])
        du_ref[...] = du.astype(BF16)
        dv_ref[...] = dv.astype(BF16)
        dsg_ref[...] = dsg
        dw_ref[...] = dw
        db_ref[...] = db

    gcol = pl.BlockSpec((T, HD), lambda g: (0, g))
    return pl.pallas_call(
        body, name="sgu_bwd", grid=(A_GROUPS,),
        in_specs=[col(P_AU // HD), col(P_AV // HD)] + par + [gcol],
        out_specs=[gcol, gcol] + par,
        out_shape=[jax.ShapeDtypeStruct((T, A_WIDTH), BF16), jax.ShapeDtypeStruct((T, A_WIDTH), BF16),
                   jax.ShapeDtypeStruct((A_GROUPS, 1, HD), F32), jax.ShapeDtypeStruct((A_GROUPS, A_CHUNK, A_CHUNK), F32),
                   jax.ShapeDtypeStruct((A_GROUPS, A_CHUNK, 1), F32)],
        compiler_params=_cparams(("arbitrary",)),
    )(p2, p2, sg, w, b, dmix)


def _attn_fn(q, k, v, qg, kg, slope, *, dil, nb):
    n = T // C_BLOCK
    qb = _rms(q, qg).reshape(n, C_BLOCK, HD)
    kb = _rms(k, kg).reshape(n, C_BLOCK, HD)
    vb = v.reshape(n, C_BLOCK, HD)
    scale = HD ** -0.5
    qi = _iota((n, C_BLOCK, C_BLOCK), 1)
    kj = _iota((n, C_BLOCK, C_BLOCK), 2)
    sl = slope[None] * float(dil)
    d_cur = qi - kj
    sc = jnp.where(d_cur >= 0, _mm(qb, kb, tb=True) * scale - sl * d_cur.astype(F32), -jnp.inf)
    mx = jnp.max(sc, axis=-1, keepdims=True)
    if nb > 1:
        kp = jnp.concatenate([jnp.zeros((1, C_BLOCK, HD), F32), kb[:-1]], axis=0)
        vp = jnp.concatenate([jnp.zeros((1, C_BLOCK, HD), F32), vb[:-1]], axis=0)
        has_prev = (_iota((n, C_BLOCK, C_BLOCK), 0) % nb) > 0
        d_prev = C_BLOCK + qi - kj
        sp = jnp.where((kj >= qi) & has_prev, _mm(qb, kp, tb=True) * scale - sl * d_prev.astype(F32), -jnp.inf)
        mx = jnp.maximum(mx, jnp.max(sp, axis=-1, keepdims=True))
    p = jnp.exp(sc - mx)
    den = jnp.sum(p, axis=-1, keepdims=True)
    if nb > 1:
        pp = jnp.exp(sp - mx)
        den = den + jnp.sum(pp, axis=-1, keepdims=True)
    inv = 1.0 / den
    out = _mm(p * inv, vb)
    if nb > 1:
        out = out + _mm(pp * inv, vp)
    lse = mx + jnp.log(den)
    return out.reshape(T, HD), jnp.broadcast_to(lse, (n, C_BLOCK, HD)).reshape(T, HD)


def _combine_fn(o1, o2, o3, l1, l2, l3):
    mx = jnp.maximum(jnp.maximum(l1, l2), l3)
    e1, e2, e3 = jnp.exp(l1 - mx), jnp.exp(l2 - mx), jnp.exp(l3 - mx)
    r = 1.0 / (e1 + e2 + e3)
    return (e1 * r) * o1 + (e2 * r) * o2 + (e3 * r) * o3


def _branch_blocks(dil):
    return -(-(T // dil) // C_BLOCK)


def _load_branch_order(ref, dil):
    if dil == 1:
        return ref[...]
    seg = T // dil
    return jnp.concatenate([ref[pl.ds(r, seg, stride=dil), :] for r in range(dil)], axis=0)


def _store_position_order(ref, val, dil, add=False):
    seg = T // dil
    for r in range(dil):
        rows = slice(None) if dil == 1 else pl.ds(r, seg, stride=dil)
        piece = val if dil == 1 else val[r * seg:(r + 1) * seg]
        if add:
            ref[rows, :] += piece
        else:
            ref[rows, :] = piece


def _dattn_specs():
    col = lambda off: pl.BlockSpec((T, HD), lambda h, off=off: (0, off // HD + h))
    row = pl.BlockSpec((1, HD), lambda h: (0, 0))
    slope = pl.BlockSpec((None, 1, HD), lambda h: (h, 0, 0))
    return [col(P_CQ), col(P_CK), col(P_CV), row, row, slope]


def _dattn_branches(q_ref, k_ref, v_ref, qg, kg, slope, o_scr, l_scr):
    for b, (_, dil) in enumerate(C_BRANCHES):
        q, k, v = (_load_branch_order(r, dil) for r in (q_ref, k_ref, v_ref))
        o, l = _attn_fn(q, k, v, qg, kg, slope, dil=dil, nb=_branch_blocks(dil))
        _store_position_order(o_scr.at[b], o, dil)
        _store_position_order(l_scr.at[b], l, dil)


def dattn_fwd(p2, qg, kg, slopes):
    per_branch = pl.BlockSpec((3, T, HD), lambda h: (0, 0, h))

    def body(q_ref, k_ref, v_ref, qg_ref, kg_ref, s_ref, y_ref, o_ref, l_ref):
        _dattn_branches(q_ref, k_ref, v_ref, qg_ref[...], kg_ref[...], s_ref[...], o_ref, l_ref)
        y_ref[...] = _combine_fn(o_ref[0], o_ref[1], o_ref[2], l_ref[0], l_ref[1], l_ref[2]).astype(BF16)

    return pl.pallas_call(
        body, name="dattn_fwd", grid=(C_HEADS,), in_specs=_dattn_specs(),
        out_specs=[pl.BlockSpec((T, HD), lambda h: (0, h)), per_branch, per_branch],
        out_shape=[jax.ShapeDtypeStruct((T, C_WIDTH), BF16)] + [jax.ShapeDtypeStruct((3, T, C_WIDTH), F32)] * 2,
        compiler_params=_cparams(("arbitrary",)),
    )(p2, p2, p2, qg, kg, slopes)


def dattn_bwd(p2, qg, kg, slopes, outs, lses, dmix):
    hcol = pl.BlockSpec((T, HD), lambda h: (0, h))
    row = pl.BlockSpec((1, HD), lambda h: (0, 0))
    dy = pl.BlockSpec((T, HD), lambda h: (0, (A_WIDTH + B_WIDTH) // HD + h))
    per_branch = pl.BlockSpec((3, T, HD), lambda h: (0, 0, h))

    def body(q_ref, k_ref, v_ref, qg_ref, kg_ref, s_ref, o_scr, l_scr, dy_ref, dq_ref, dk_ref, dv_ref, dqg_ref, dkg_ref, g_scr,
             acc):
        qg, kg, slope = qg_ref[...], kg_ref[...], s_ref[...]
        _, vjp = jax.vjp(_combine_fn, o_scr[0], o_scr[1], o_scr[2], l_scr[0], l_scr[1], l_scr[2])
        for i, g in enumerate(vjp(dy_ref[...])):
            g_scr[i] = g

        @pl.when(pl.program_id(0) == 0)
        def _():
            dqg_ref[...] = jnp.zeros_like(dqg_ref)
            dkg_ref[...] = jnp.zeros_like(dkg_ref)

        for b, (_, dil) in enumerate(C_BRANCHES):
            q, k, v = (_load_branch_order(r, dil) for r in (q_ref, k_ref, v_ref))
            do, dl = _load_branch_order(g_scr.at[b], dil), _load_branch_order(g_scr.at[3 + b], dil)
            fn = functools.partial(_attn_fn, dil=dil, nb=_branch_blocks(dil))
            _, vjp_b = jax.vjp(lambda a, b_, c, d, e, fn=fn: fn(a, b_, c, d, e, slope), q, k, v, qg, kg)
            dq, dk, dv, dqg, dkg = vjp_b((do, dl))
            for i, val in enumerate((dq, dk, dv)):
                _store_position_order(acc.at[i], val, dil, add=b > 0)
            dqg_ref[...] += dqg
            dkg_ref[...] += dkg
        for i, ref in enumerate((dq_ref, dk_ref, dv_ref)):
            ref[...] = acc[i].astype(BF16)

    scr = lambda n: pltpu.VMEM((n, T, HD), F32)
    return pl.pallas_call(
        body, name="dattn_bwd", grid=(C_HEADS,), in_specs=_dattn_specs() + [per_branch, per_branch, dy],
        out_specs=[hcol, hcol, hcol, row, row],
        out_shape=[jax.ShapeDtypeStruct((T, C_WIDTH), BF16)] * 3 + [jax.ShapeDtypeStruct((1, HD), F32)] * 2,
        scratch_shapes=[scr(6), scr(3)], compiler_params=_cparams(("arbitrary",)),
    )(p2, p2, p2, qg, kg, slopes, outs, lses, dmix)


_NCH = T // B_CHUNK


def _conv_taps(x, w_ref):
    rows = _iota(x.shape, 0)
    taps = []
    for j in range(B_CONV):
        s = B_CONV - 1 - j
        taps.append(x if s == 0 else jnp.where(rows >= s, pltpu.roll(x, s, 0), 0.0))
    pre = sum(w_ref[j:j + 1, :] * taps[j] for j in range(B_CONV))
    return pre, taps


def _conv_post(pre, mode):
    y = _silu(pre)
    if mode == "v":
        return y
    y = y * lax.rsqrt(jnp.sum(y * y, axis=-1, keepdims=True) + EPS)
    return y * (HD ** -0.5) if mode == "q" else y


def conv_fwd(p2, conv_w, mode):
    idx = "qkv".index(mode)
    xcol = pl.BlockSpec((T, HD), lambda h: (0, P_BQ // HD + B_HEADS * idx + h))
    wcol = pl.BlockSpec((B_CONV, HD), lambda h: (0, B_HEADS * idx + h))
    hcol = pl.BlockSpec((T, HD), lambda h: (0, h))

    def body(x_ref, w_ref, y_ref):
        pre, _ = _conv_taps(x_ref[...], w_ref)
        y_ref[...] = _conv_post(pre, mode)

    return pl.pallas_call(
        body, name=f"conv_fwd_{mode}", grid=(B_HEADS,), in_specs=[xcol, wcol], out_specs=hcol,
        out_shape=jax.ShapeDtypeStruct((T, B_WIDTH), F32), compiler_params=_cparams(("arbitrary",)),
    )(p2, conv_w)


def conv_bwd(p2, conv_w, dys, mode):
    idx = "qkv".index(mode)
    xcol = pl.BlockSpec((T, HD), lambda h: (0, P_BQ // HD + B_HEADS * idx + h))
    wcol = pl.BlockSpec((B_CONV, HD), lambda h: (0, B_HEADS * idx + h))
    hcol = pl.BlockSpec((T, HD), lambda h: (0, h))
    wout = pl.BlockSpec((B_CONV, HD), lambda h: (0, h))

    def body(x_ref, w_ref, *rest):
        dy_refs, (dx_ref, dw_ref) = rest[:-2], rest[-2:]
        pre, taps = _conv_taps(x_ref[...], w_ref)
        _, vjp = jax.vjp(functools.partial(_conv_post, mode=mode), pre)
        (dpre,) = vjp(sum(r[...] for r in dy_refs))
        rows = _iota(dpre.shape, 0)
        dx = w_ref[B_CONV - 1:B_CONV, :] * dpre
        for j in range(B_CONV):
            s = B_CONV - 1 - j
            dw_ref[j:j + 1, :] = jnp.sum(dpre * taps[j], axis=0, keepdims=True)
            if s > 0:
                dx = dx + w_ref[j:j + 1, :] * jnp.where(rows < T - s, pltpu.roll(dpre, T - s, 0), 0.0)
        dx_ref[...] = dx.astype(BF16)

    return pl.pallas_call(
        body, name=f"conv_bwd_{mode}", grid=(B_HEADS,), in_specs=[xcol, wcol] + [hcol] * len(dys), out_specs=[hcol, wout],
        out_shape=[jax.ShapeDtypeStruct((T, B_WIDTH), BF16), jax.ShapeDtypeStruct((B_CONV, B_WIDTH), F32)],
        compiler_params=_cparams(("arbitrary",)),
    )(p2, conv_w, *dys)


def _gates_fn(bg, al, dtb, h):
    r = _iota((HD, HD), 0)
    logit = _mm(bg, (r == h).astype(F32), hi=True)
    a = _mm(bg, (r == h + B_HEADS).astype(F32), hi=True)
    beta = _sigmoid(logit)
    graw = -jnp.exp(al) * _softplus(a + dtb)
    tri = (_iota((_NCH, B_CHUNK, B_CHUNK), 1) >= _iota((_NCH, B_CHUNK, B_CHUNK), 2)).astype(F32)
    g = _mm(tri, graw.reshape(_NCH, B_CHUNK, HD), hi=True).reshape(T, HD)
    return beta, g


def _gates_specs():
    bg = pl.BlockSpec((T, HD), lambda h: (0, P_BB // HD))
    par = pl.BlockSpec((None, 1, HD), lambda h: (h, 0, 0))
    out = pl.BlockSpec((None, T, HD), lambda h: (h, 0, 0))
    return bg, par, out


def gates_fwd(p2, al, dtb):
    bg, par, out = _gates_specs()

    def body(bg_ref, al_ref, dtb_ref, beta_ref, g_ref):
        beta, g = _gates_fn(bg_ref[...], al_ref[...], dtb_ref[...], pl.program_id(0))
        beta_ref[...] = beta
        g_ref[...] = g

    return pl.pallas_call(
        body, name="gates_fwd", grid=(B_HEADS,), in_specs=[bg, par, par], out_specs=[out, out],
        out_shape=[jax.ShapeDtypeStruct((B_HEADS, T, HD), F32)] * 2, compiler_params=_cparams(("arbitrary",)),
    )(p2, al, dtb)


def gates_bwd(p2, al, dtb, dbeta, dg1, dg2):
    bg, par, out = _gates_specs()
    acc = pl.BlockSpec((T, HD), lambda h: (0, 0))

    def body(bg_ref, al_ref, dtb_ref, dbeta_ref, dg1_ref, dg2_ref, dbg_ref, dal_ref, ddtb_ref, acc_ref):
        h = pl.program_id(0)
        _, vjp = jax.vjp(lambda a, b, c: _gates_fn(a, b, c, h), bg_ref[...], al_ref[...], dtb_ref[...])
        dbg, dal, ddtb = vjp((dbeta_ref[...], dg1_ref[...] + dg2_ref[...]))

        @pl.when(h == 0)
        def _():
            acc_ref[...] = jnp.zeros_like(acc_ref)

        acc_ref[...] += dbg
        dbg_ref[...] = acc_ref[...].astype(BF16)
        dal_ref[...] = jnp.broadcast_to(jnp.sum(dal, axis=-1, keepdims=True), (1, HD))
        ddtb_ref[...] = jnp.broadcast_to(jnp.sum(ddtb, axis=-1, keepdims=True), (1, HD))

    return pl.pallas_call(
        body, name="gates_bwd", grid=(B_HEADS,), in_specs=[bg, par, par, out, out, out], out_specs=[acc, par, par],
        out_shape=[jax.ShapeDtypeStruct((T, HD), BF16)] + [jax.ShapeDtypeStruct((B_HEADS, 1, HD), F32)] * 2,
        scratch_shapes=[pltpu.VMEM((T, HD), F32)], compiler_params=_cparams(("arbitrary",)),
    )(p2, al, dtb, dbeta, dg1, dg2)


def _unit_lower_inverse(a):
    eye = (_iota(a.shape, 1) == _iota(a.shape, 2)).astype(F32)
    x = eye - a
    p = _mm(a, a, hi=True)
    for i in range(5):
        x = x + _mm(x, p, hi=True)
        if i < 4:
            p = _mm(p, p, hi=True)
    return x


_WY_CH = 16
_WY_ROWS = _WY_CH * B_CHUNK


def _wy_fn(q, k, v, beta, g):
    sh = (q.shape[0] // B_CHUNK, B_CHUNK, HD)
    q3, k3, v3, b3, g3 = (t.reshape(sh) for t in (q, k, v, beta, g))
    gd = g3[:, :, :B_CHUNK] - jnp.swapaxes(g3, 1, 2)[:, :B_CHUNK, :]
    ii, jj = _iota(gd.shape, 1), _iota(gd.shape, 2)
    decay = jnp.exp(jnp.where(ii >= jj, gd, -jnp.inf))
    kb = k3 * b3
    a = _mm(kb, k3, tb=True) * jnp.where(ii > jj, decay, 0.0)
    tinv = _unit_lower_inverse(a)
    u = _mm(tinv, v3 * b3, hi=True)
    w = _mm(tinv, kb * jnp.exp(g3), hi=True)
    attn = _mm(q3, k3, tb=True) * decay
    return u.reshape(q.shape), w.reshape(q.shape), attn


def _wy_specs():
    hcol = pl.BlockSpec((_WY_ROWS, HD), lambda h, i: (i, h))
    hb = pl.BlockSpec((None, _WY_ROWS, HD), lambda h, i: (h, i, 0))
    at = pl.BlockSpec((None, _WY_CH, B_CHUNK, B_CHUNK), lambda h, i: (h, i, 0, 0))
    return hcol, hb, at


_WY_GRID = (B_HEADS, _NCH // _WY_CH)


def wy_fwd(q, k, v, beta, g):
    hcol, hb, at = _wy_specs()

    def body(q_ref, k_ref, v_ref, b_ref, g_ref, u_ref, w_ref, a_ref):
        u, w, a = _wy_fn(q_ref[...], k_ref[...], v_ref[...], b_ref[...], g_ref[...])
        u_ref[...] = u
        w_ref[...] = w
        a_ref[...] = a

    return pl.pallas_call(
        body, name="wy_fwd", grid=_WY_GRID, in_specs=[hcol, hcol, hcol, hb, hb], out_specs=[hcol, hcol, at],
        out_shape=[jax.ShapeDtypeStruct((T, B_WIDTH), F32)] * 2 + [jax.ShapeDtypeStruct((B_HEADS, _NCH, B_CHUNK, B_CHUNK), F32)],
        compiler_params=_cparams(("arbitrary", "arbitrary")),
    )(q, k, v, beta, g)


def wy_bwd(q, k, v, beta, g, du, dw, dattn):
    hcol, hb, at = _wy_specs()

    def body(q_ref, k_ref, v_ref, b_ref, g_ref, du_ref, dw_ref, da_ref, dq_ref, dk_ref, dv_ref, db_ref, dg_ref):
        _, vjp = jax.vjp(_wy_fn, q_ref[...], k_ref[...], v_ref[...], b_ref[...], g_ref[...])
        for r, t in zip((dq_ref, dk_ref, dv_ref, db_ref, dg_ref), vjp((du_ref[...], dw_ref[...], da_ref[...]))):
            r[...] = t

    return pl.pallas_call(
        body, name="wy_bwd", grid=_WY_GRID, in_specs=[hcol, hcol, hcol, hb, hb, hcol, hcol, at],
        out_specs=[hcol, hcol, hcol, hb, hb],
        out_shape=[jax.ShapeDtypeStruct((T, B_WIDTH), F32)] * 3 + [jax.ShapeDtypeStruct((B_HEADS, T, HD), F32)] * 2,
        compiler_params=_cparams(("arbitrary", "arbitrary")),
    )(q, k, v, beta, g, du, dw, dattn)


def _scan_step_fn(q, k, u, w, g, attn, gate, og, s):
    v_new = u - _mm(w, s)
    o = _mm(q * jnp.exp(g), s) + _mm(attn, v_new)
    g_last = jnp.sum(jnp.where(_iota(g.shape, 0) == B_CHUNK - 1, g, 0.0), axis=0, keepdims=True)
    s_new = s * jnp.exp(g_last) + _mm(k * jnp.exp(g_last - g), v_new, ta=True)
    return _rms(o, og) * _silu(gate), s_new


def _scan_specs(rev):
    ch = (lambda n: _NCH - 1 - n) if rev else (lambda n: n)
    rows = pl.BlockSpec((B_CHUNK, B_WIDTH), lambda n: (ch(n), 0))
    gb = pl.BlockSpec((B_HEADS, B_CHUNK, HD), lambda n: (0, ch(n), 0))
    at = pl.BlockSpec((B_HEADS, None, B_CHUNK, B_CHUNK), lambda n: (0, ch(n), 0, 0))
    og = pl.BlockSpec((1, HD), lambda n: (0, 0))
    st = pl.BlockSpec((None, B_HEADS, HD, HD), lambda n: (ch(n), 0, 0, 0))
    return rows, gb, at, og, st


def scan_fwd(q, k, u, w, g, attn, gate, og):
    rows, gb, at, ogs, st = _scan_specs(False)

    def body(q_ref, k_ref, u_ref, w_ref, g_ref, a_ref, gate_ref, og_ref, y_ref, st_ref, s_ref):
        @pl.when(pl.program_id(0) == 0)
        def _():
            s_ref[...] = jnp.zeros_like(s_ref)

        for h in range(B_HEADS):
            c = slice(h * HD, (h + 1) * HD)
            s = s_ref[h]
            st_ref[h] = s
            y, s_new = _scan_step_fn(q_ref[:, c], k_ref[:, c], u_ref[:, c], w_ref[:, c], g_ref[h], a_ref[h],
                                     gate_ref[:, c], og_ref[...], s)
            y_ref[:, c] = y.astype(BF16)
            s_ref[h] = s_new

    return pl.pallas_call(
        body, name="scan_fwd", grid=(_NCH,), in_specs=[rows, rows, rows, rows, gb, at, rows, ogs], out_specs=[rows, st],
        out_shape=[jax.ShapeDtypeStruct((T, B_WIDTH), BF16), jax.ShapeDtypeStruct((_NCH, B_HEADS, HD, HD), F32)],
        scratch_shapes=[pltpu.VMEM((B_HEADS, HD, HD), F32)], compiler_params=_cparams(("arbitrary",)),
    )(q, k, u, w, g, attn, gate, og)


def scan_bwd(q, k, u, w, g, attn, gate, og, states, dmix):
    rows, gb, at, ogs, st = _scan_specs(True)
    dyb = pl.BlockSpec((B_CHUNK, HD), lambda n: (_NCH - 1 - n, 0))

    def body(q_ref, k_ref, u_ref, w_ref, g_ref, a_ref, gate_ref, og_ref, st_ref, *rest):
        dy_refs, (dq_ref, dk_ref, du_ref, dw_ref, dgate_ref, dg_ref, da_ref, dog_ref, ds_ref) = rest[:B_HEADS], rest[B_HEADS:]

        @pl.when(pl.program_id(0) == 0)
        def _():
            ds_ref[...] = jnp.zeros_like(ds_ref)
            dog_ref[...] = jnp.zeros_like(dog_ref)

        for h in range(B_HEADS):
            c = slice(h * HD, (h + 1) * HD)
            _, vjp = jax.vjp(_scan_step_fn, q_ref[:, c], k_ref[:, c], u_ref[:, c], w_ref[:, c], g_ref[h], a_ref[h],
                             gate_ref[:, c], og_ref[...], st_ref[h])
            dq, dk, du, dw, dg, da, dgate, dog, ds = vjp((dy_refs[h][...], ds_ref[h]))
            dq_ref[:, c] = dq
            dk_ref[:, c] = dk
            du_ref[:, c] = du
            dw_ref[:, c] = dw
            dgate_ref[:, c] = dgate.astype(BF16)
            dg_ref[h] = dg
            da_ref[h] = da
            dog_ref[...] += dog
            ds_ref[h] = ds

    dy_specs = [pl.BlockSpec((B_CHUNK, HD), lambda n, h=h: (_NCH - 1 - n, A_WIDTH // HD + h)) for h in range(B_HEADS)]
    return pl.pallas_call(
        body, name="scan_bwd", grid=(_NCH,),
        in_specs=[rows, rows, rows, rows, gb, at, rows, ogs, st] + dy_specs,
        out_specs=[rows] * 5 + [gb, at, ogs],
        out_shape=[jax.ShapeDtypeStruct((T, B_WIDTH), F32)] * 4 + [jax.ShapeDtypeStruct((T, B_WIDTH), BF16)]
        + [jax.ShapeDtypeStruct((B_HEADS, T, HD), F32), jax.ShapeDtypeStruct((B_HEADS, _NCH, B_CHUNK, B_CHUNK), F32),
           jax.ShapeDtypeStruct((1, HD), F32)],
        scratch_shapes=[pltpu.VMEM((B_HEADS, HD, HD), F32)], compiler_params=_cparams(("arbitrary",)),
    )(q, k, u, w, g, attn, gate, og, states, *([dmix] * B_HEADS))


def _lanes(vec):
    return jnp.broadcast_to(vec[:, None, None], (vec.shape[0], 1, HD))


def gdn_forward(p2, conv_w, a_log, dt_bias, og):
    qa, ka, va = (conv_fwd(p2, conv_w, m) for m in "qkv")
    beta, g = gates_fwd(p2, _lanes(a_log), _lanes(dt_bias))
    u, w, attn = wy_fwd(qa, ka, va, beta, g)
    gate = p2[:, P_BG:P_BB]
    y, states = scan_fwd(qa, ka, u, w, g, attn, gate, og)
    return y, (qa, ka, va, beta, g, u, w, attn, gate, states)


def gdn_backward(p2, conv_w, a_log, dt_bias, og, saved, dmix):
    qa, ka, va, beta, g, u, w, attn, gate, states = saved
    dq1, dk1, du, dw, dgate, dg1, dattn, dog = scan_bwd(qa, ka, u, w, g, attn, gate, og, states, dmix)
    dq2, dk2, dv, dbeta, dg2 = wy_bwd(qa, ka, va, beta, g, du, dw, dattn)
    dbg, dal, ddtb = gates_bwd(p2, _lanes(a_log), _lanes(dt_bias), dbeta, dg1, dg2)
    dxq, dwq = conv_bwd(p2, conv_w, [dq1, dq2], "q")
    dxk, dwk = conv_bwd(p2, conv_w, [dk1, dk2], "k")
    dxv, dwv = conv_bwd(p2, conv_w, [dv], "v")
    return [dxq, dxk, dxv, dgate, dbg], jnp.concatenate([dwq, dwk, dwv], axis=1), dal[:, 0, 0], ddtb[:, 0, 0], dog


_SLOPES = np.exp2(-8.0 * (np.arange(C_HEADS, dtype=np.float64) + 1.0) / C_HEADS).astype(np.float32)


def _alibi_slopes():
    return _lanes(jnp.asarray(_SLOPES))


_ROWS = 512
_TM = 1024
_TM_FFN = 512


def _dep_specs(dep, ngrid):
    if dep is None:
        return [], []
    return [dep], [pl.BlockSpec((8, HD), lambda *_: (0, 0))]


def rmsnorm_fwd(x, g, dep=None):
    blk = pl.BlockSpec((_ROWS, D), lambda i: (i, 0))
    deps, dspecs = _dep_specs(dep, 1)

    def body(x_ref, g_ref, *rest):
        rest[-1][...] = _rms(x_ref[...], g_ref[...]).astype(BF16)

    return pl.pallas_call(
        body, name="rmsnorm_fwd", grid=(T // _ROWS,), in_specs=[blk, pl.BlockSpec((1, D), lambda i: (0, 0))] + dspecs,
        out_specs=blk, out_shape=jax.ShapeDtypeStruct((T, D), BF16), compiler_params=_cparams(("arbitrary",)),
    )(x, g, *deps)


def rmsnorm_bwd(x, g, dh, dres):
    blk = pl.BlockSpec((_ROWS, D), lambda i: (i, 0))
    row = pl.BlockSpec((1, D), lambda i: (0, 0))

    def body(x_ref, g_ref, dh_ref, dres_ref, dx_ref, dg_ref):
        _, vjp = jax.vjp(_rms, x_ref[...], g_ref[...])
        dx, dg = vjp(dh_ref[...])
        dx_ref[...] = dres_ref[...] + dx

        @pl.when(pl.program_id(0) == 0)
        def _():
            dg_ref[...] = jnp.zeros_like(dg_ref)

        dg_ref[...] += dg

    return pl.pallas_call(
        body, name="rmsnorm_bwd", grid=(T // _ROWS,), in_specs=[blk, row, blk, blk], out_specs=[blk, row],
        out_shape=[jax.ShapeDtypeStruct((T, D), F32), jax.ShapeDtypeStruct((1, D), F32)],
        compiler_params=_cparams(("arbitrary",)),
    )(x, g, dh, dres)


def _matmul(name, a, b, *, grid, a_spec, b_spec, o_spec, out_shape, ta=False, tb=False, k_axis=None, res=None, dep=None):
    dims = _dims(2, ta, tb)
    deps, dspecs = _dep_specs(dep, len(grid))

    def body(a_ref, b_ref, *rest):
        o_ref = rest[-1]
        prod = lax.dot_general(a_ref[...].astype(BF16), b_ref[...].astype(BF16), dims, preferred_element_type=F32)
        if res is not None:
            prod = prod + rest[0][...]
        if k_axis is None:
            o_ref[...] = prod.astype(o_ref.dtype)
        else:
            @pl.when(pl.program_id(k_axis) == 0)
            def _():
                o_ref[...] = prod

            @pl.when(pl.program_id(k_axis) > 0)
            def _():
                o_ref[...] += prod

    sem = tuple("arbitrary" for _ in grid)
    ins = [a, b] + ([res] if res is not None else []) + deps
    specs = [a_spec, b_spec] + ([o_spec] if res is not None else []) + dspecs
    return pl.pallas_call(
        body, name=name, grid=grid, in_specs=specs, out_specs=o_spec, out_shape=out_shape, compiler_params=_cparams(sem),
    )(*ins)


_IN_TN = P_END // 3


def mm_proj(h1, wp_in, l):
    return _matmul(
        "mm_proj", h1, wp_in, grid=(P_END // _IN_TN, T // _TM),
        a_spec=pl.BlockSpec((_TM, D), lambda j, i: (i, 0)),
        b_spec=pl.BlockSpec((None, D, _IN_TN), lambda j, i: (l, 0, j)),
        o_spec=pl.BlockSpec((_TM, _IN_TN), lambda j, i: (i, j)), out_shape=jax.ShapeDtypeStruct((T, P_END), F32))


def mm_dh1(dp2, wp_in, l, dep=None):
    return _matmul(
        "mm_dh1", dp2, wp_in, grid=(T // _TM, P_END // _IN_TN), tb=True, k_axis=1, dep=dep,
        a_spec=pl.BlockSpec((_TM, _IN_TN), lambda i, k: (i, k)),
        b_spec=pl.BlockSpec((None, D, _IN_TN), lambda i, k: (l, 0, k)),
        o_spec=pl.BlockSpec((_TM, D), lambda i, k: (i, 0)), out_shape=jax.ShapeDtypeStruct((T, D), F32))


def mm_dwin(h1, dp2):
    return _matmul(
        "mm_dwin", h1, dp2, grid=(P_END // _IN_TN, D // _TM), ta=True,
        a_spec=pl.BlockSpec((T, _TM), lambda j, i: (0, i)),
        b_spec=pl.BlockSpec((T, _IN_TN), lambda j, i: (0, j)),
        o_spec=pl.BlockSpec((_TM, _IN_TN), lambda j, i: (i, j)), out_shape=jax.ShapeDtypeStruct((D, P_END), BF16))


def _mm_square(name, a, w, l, res, tb, dep=None):
    tn = 1024
    b_spec = (pl.BlockSpec((None, tn, D), lambda j, i: (l, j, 0)) if tb else pl.BlockSpec((None, D, tn), lambda j, i: (l, 0, j)))
    return _matmul(
        name, a, w, grid=(D // tn, T // _TM), tb=tb, res=res, dep=dep,
        a_spec=pl.BlockSpec((_TM, D), lambda j, i: (i, 0)), b_spec=b_spec,
        o_spec=pl.BlockSpec((_TM, tn), lambda j, i: (i, j)), out_shape=jax.ShapeDtypeStruct((T, D), F32))


def mm_out(mix, wg_out, l, x):
    return _mm_square("mm_out", mix, wg_out, l, x, False)


def mm_dmix(dx1, wg_out, l, dep=None):
    return _mm_square("mm_dmix", dx1, wg_out, l, None, True, dep)


def mm_dwout(mix, dx1):
    tn = 1024
    return _matmul(
        "mm_dwout", mix, dx1, grid=(D // tn, D // _TM), ta=True,
        a_spec=pl.BlockSpec((T, _TM), lambda j, i: (0, i)), b_spec=pl.BlockSpec((T, tn), lambda j, i: (0, j)),
        o_spec=pl.BlockSpec((_TM, tn), lambda j, i: (i, j)), out_shape=jax.ShapeDtypeStruct((D, D), BF16))


_GU_TN = GU_SHARD // 2


_GU_NJ = FFN // _GU_TN


def mm_dh2(dgu, wg_gu, l, dep=None):
    return _matmul(
        "mm_dh2", dgu, wg_gu, grid=(T // _TM, 2 * N_CHIPS), tb=True, k_axis=1, dep=dep,
        a_spec=pl.BlockSpec((None, _TM, _GU_TN), lambda i, k: (k // _GU_NJ, i, k % _GU_NJ)),
        b_spec=pl.BlockSpec((None, None, D, _GU_TN), lambda i, k: (l, k // 2, 0, k % 2)),
        o_spec=pl.BlockSpec((_TM, D), lambda i, k: (i, 0)), out_shape=jax.ShapeDtypeStruct((T, D), F32))


def mm_dwgu(h2, dgu):
    return _matmul(
        "mm_dwgu", h2, dgu, grid=(N_CHIPS, 2, D // _TM), ta=True,
        a_spec=pl.BlockSpec((T, _TM), lambda s, j, i: (0, i)),
        b_spec=pl.BlockSpec((None, T, _GU_TN), lambda s, j, i: ((2 * s + j) // _GU_NJ, 0, (2 * s + j) % _GU_NJ)),
        o_spec=pl.BlockSpec((None, _TM, _GU_TN), lambda s, j, i: (s, i, j)),
        out_shape=jax.ShapeDtypeStruct((N_CHIPS, D, GU_SHARD), BF16))


def mm_down(act, wg_down, l, x1):
    tn = 512
    return _matmul(
        "mm_down", act, wg_down, grid=(D // tn, T // _TM), res=x1,
        a_spec=pl.BlockSpec((_TM, FFN), lambda j, i: (i, 0)),
        b_spec=pl.BlockSpec((None, FFN, tn), lambda j, i: (l, 0, j)),
        o_spec=pl.BlockSpec((_TM, tn), lambda j, i: (i, j)), out_shape=jax.ShapeDtypeStruct((T, D), F32))


def mm_dwdown(act, dx2):
    tm, tn = DOWN_SHARD, 512
    return _matmul(
        "mm_dwdown", act, dx2, grid=(D // tn, FFN // tm), ta=True,
        a_spec=pl.BlockSpec((T, tm), lambda j, i: (0, i)), b_spec=pl.BlockSpec((T, tn), lambda j, i: (0, j)),
        o_spec=pl.BlockSpec((tm, tn), lambda j, i: (i, j)), out_shape=jax.ShapeDtypeStruct((FFN, D), BF16))


_FF_TN = 1408


def _swiglu_fn(gt, up):
    return _silu(gt) * up


def _gate_up_specs():
    gate = pl.BlockSpec((None, None, D, _FF_TN), lambda j, i: (0, j // 2, 0, j % 2))
    up = pl.BlockSpec((None, None, D, _FF_TN), lambda j, i: (0, N_CHIPS // 2 + j // 2, 0, j % 2))
    both = pl.BlockSpec((2, _TM_FFN, _FF_TN), lambda j, i: (0, i, j))
    return gate, up, both


def mm_gu_swiglu(h2, wg_gu):
    gate, up, both = _gate_up_specs()

    def body(h_ref, wg_ref, wu_ref, gu_ref, act_ref):
        h = h_ref[...]
        gt = jnp.dot(h, wg_ref[...], preferred_element_type=F32)
        u = jnp.dot(h, wu_ref[...], preferred_element_type=F32)
        gu_ref[0] = gt
        gu_ref[1] = u
        act_ref[...] = _swiglu_fn(gt, u).astype(BF16)

    return pl.pallas_call(
        body, name="mm_gu_swiglu", grid=(FFN // _FF_TN, T // _TM_FFN),
        in_specs=[pl.BlockSpec((_TM_FFN, D), lambda j, i: (i, 0)), gate, up],
        out_specs=[both, pl.BlockSpec((_TM_FFN, _FF_TN), lambda j, i: (i, j))],
        out_shape=[jax.ShapeDtypeStruct((2, T, FFN), F32), jax.ShapeDtypeStruct((T, FFN), BF16)],
        compiler_params=_cparams(("arbitrary", "arbitrary")),
    )(h2, wg_gu, wg_gu)


def mm_dact_swiglu(dx2, wg_down, gu, dep=None):
    _, _, both = _gate_up_specs()
    deps, dspecs = _dep_specs(dep, 2)

    def body(dx_ref, w_ref, gu_ref, *rest):
        dact = lax.dot_general(dx_ref[...].astype(BF16), w_ref[...], _dims(2, False, True), preferred_element_type=F32)
        _, vjp = jax.vjp(_swiglu_fn, gu_ref[0], gu_ref[1])
        dgt, dup = vjp(dact)
        rest[-1][0] = dgt.astype(BF16)
        rest[-1][1] = dup.astype(BF16)

    return pl.pallas_call(
        body, name="mm_dact_swiglu", grid=(FFN // _FF_TN, T // _TM_FFN),
        in_specs=[pl.BlockSpec((_TM_FFN, D), lambda j, i: (i, 0)), pl.BlockSpec((None, _FF_TN, D), lambda j, i: (0, j, 0)),
                  both]
        + dspecs,
        out_specs=both, out_shape=jax.ShapeDtypeStruct((2, T, FFN), BF16),
        compiler_params=_cparams(("arbitrary", "arbitrary")),
    )(dx2, wg_down, gu, *deps)


def loss_and_grad(y, target):
    blk = pl.BlockSpec((_ROWS, D), lambda i: (i, 0))
    acc = pl.BlockSpec((8, HD), lambda i: (0, 0))

    def body(y_ref, t_ref, dy_ref, l_ref):
        err = y_ref[...] - t_ref[...]
        dy_ref[...] = err * (1.0 / D)

        @pl.when(pl.program_id(0) == 0)
        def _():
            l_ref[...] = jnp.zeros_like(l_ref)

        l_ref[...] += (0.5 / D) * jnp.sum(err * err)

    return pl.pallas_call(
        body, name="loss_and_grad", grid=(T // _ROWS,), in_specs=[blk, blk], out_specs=[blk, acc],
        out_shape=[jax.ShapeDtypeStruct((T, D), F32), jax.ShapeDtypeStruct((8, HD), F32)],
        compiler_params=_cparams(("arbitrary",)),
    )(y, target)


def adamw(w, g, m, v, name):
    rows, cols = w.shape
    tr = _ROWS if rows % _ROWS == 0 else rows
    blk = pl.BlockSpec((tr, cols), lambda i: (i, 0))

    def body(w_ref, g_ref, m_ref, v_ref, d_ref, nm_ref, nv_ref):
        gg = g_ref[...]
        nm = ADAM_B1 * m_ref[...] + (1.0 - ADAM_B1) * gg
        nv = ADAM_B2 * v_ref[...] + (1.0 - ADAM_B2) * (gg * gg)
        m_hat = nm / (1.0 - ADAM_B1 ** ADAM_STEP)
        v_hat = nv / (1.0 - ADAM_B2 ** ADAM_STEP)
        d_ref[...] = -ADAM_LR * (m_hat / (jnp.sqrt(v_hat) + ADAM_EPS) + ADAM_WD * w_ref[...])
        nm_ref[...] = nm
        nv_ref[...] = nv

    return pl.pallas_call(
        body, name=name, grid=(rows // tr,), in_specs=[blk] * 4, out_specs=[blk] * 3,
        out_shape=[jax.ShapeDtypeStruct(w.shape, F32)] * 3, compiler_params=_cparams(("arbitrary",)),
    )(w, g, m, v)


_LANE = 128


def _segment_of_shard_column():
    flat = np.full(P_END, -1, np.int64)
    for o in range(P_END):
        if GATE_COLS <= o < P_CQ:
            continue
        c = o if o < GATE_COLS else o - (P_CQ - GATE_COLS)
        flat[o] = (c // IN_SHARD) * IN_SHARD_PAD + c % IN_SHARD
    return flat


def _block_pairs(src_of_dst):
    return [sorted({int(c) // _LANE for c in src_of_dst[db * _LANE:(db + 1) * _LANE] if c >= 0})
            for db in range(len(src_of_dst) // _LANE)]


_RELAYOUT_ROWS = 512
_SHARD_BLOCKS = IN_SHARD_PAD // _LANE


def _relayout(name, x, to_segments):
    seg_of = _segment_of_shard_column()
    if to_segments:
        src_of_dst = seg_of
    else:
        src_of_dst = np.full(N_CHIPS * IN_SHARD_PAD, -1, np.int64)
        src_of_dst[seg_of[seg_of >= 0]] = np.nonzero(seg_of >= 0)[0]
    sources = _block_pairs(src_of_dst)
    n_dst = len(sources)
    col_map = jnp.asarray(src_of_dst.reshape(n_dst, 1, _LANE), jnp.int32)
    shard_blk = pl.BlockSpec((N_CHIPS, _RELAYOUT_ROWS, IN_SHARD_PAD), lambda i: (0, i, 0))
    seg_blk = pl.BlockSpec((_RELAYOUT_ROWS, P_END), lambda i: (i, 0))

    def shard_cols(ref, b):
        return ref.at[b // _SHARD_BLOCKS, :, pl.ds((b % _SHARD_BLOCKS) * _LANE, _LANE)]

    def seg_cols(ref, b):
        return ref.at[:, pl.ds(b * _LANE, _LANE)]

    src_cols, dst_cols = (shard_cols, seg_cols) if to_segments else (seg_cols, shard_cols)

    def body(x_ref, map_ref, o_ref):
        src_row = _iota((_LANE, _LANE), 0)
        for d in range(n_dst):
            acc = jnp.zeros((_RELAYOUT_ROWS, _LANE), F32)
            for sb in sources[d]:
                sel = (src_row + sb * _LANE == map_ref[d]).astype(x_ref.dtype)
                acc = acc + jnp.dot(src_cols(x_ref, sb)[...], sel, preferred_element_type=F32)
            dst_cols(o_ref, d)[...] = acc.astype(o_ref.dtype)

    rows = x.shape[-2]
    out_shape = (rows, P_END) if to_segments else (N_CHIPS, rows, IN_SHARD_PAD)
    return pl.pallas_call(
        body, name=name, grid=(rows // _RELAYOUT_ROWS,),
        in_specs=[shard_blk if to_segments else seg_blk, pl.BlockSpec(col_map.shape, lambda i: (0, 0, 0))],
        out_specs=seg_blk if to_segments else shard_blk, out_shape=jax.ShapeDtypeStruct(out_shape, x.dtype),
        compiler_params=_cparams(("arbitrary",)),
    )(x, col_map)


def shards_to_segments(w):
    return _relayout("shards_to_segments", w, True)


def segments_to_shards(w):
    return _relayout("segments_to_shards", w, False)


def mixers_forward(x, w_in, sp, dep=None, h1=None):
    if h1 is None:
        h1 = rmsnorm_fwd(x, sp["norm1_g"], dep)
    p2 = mm_proj(h1, w_in, 0)
    y_a = sgu_fwd(p2, sp["sgu_norm_g"], sp["w_spatial"], sp["b_spatial"])
    y_b, saved_b = gdn_forward(p2, sp["conv_w"], sp["a_log"], sp["dt_bias"], sp["o_norm_g"])
    y_c, outs_c, lses_c = dattn_fwd(p2, sp["q_norm_g"], sp["k_norm_g"], _alibi_slopes())
    mix = jnp.concatenate([y_a, y_b, y_c], axis=1)
    return mix, (x, h1, p2, saved_b, (outs_c, lses_c), mix)


def ffn_up(x, mix, w_out, w_gu, sp):
    x1 = mm_out(mix, w_out, 0, x)
    h2 = rmsnorm_fwd(x1, sp["norm2_g"])
    gu, act = mm_gu_swiglu(h2, w_gu)
    return x1, h2, gu, act


def ffn_forward(x, mix, wg, sp):
    x1, h2, gu, act = ffn_up(x, mix, wg["out"], wg["gu"], sp)
    x2 = mm_down(act, wg["down"], 0, x1)
    return x2, (x1, h2, gu, act)


def ffn_backward(dx2, wg, sp, saved, dep=None, on_weight_grads=None):
    x1, h2, gu, act = saved
    dgu = mm_dact_swiglu(dx2, wg["down"], gu, dep)
    dw_down = mm_dwdown(act, dx2)
    dw_gu = mm_dwgu(h2, dgu)
    tok = None if on_weight_grads is None else on_weight_grads(dw_gu, dw_down)
    dh2 = mm_dh2(dgu, wg["gu"], 0, tok)
    dx1, dnorm2 = rmsnorm_bwd(x1, sp["norm2_g"], dh2, dx2)
    return dx1, dnorm2, dw_gu, dw_down


def mixers_backward(dx1, wg, sp, saved, dep=None, on_weight_grads=None):
    x, h1, p2, saved_b, saved_c, mix = saved
    dmix = mm_dmix(dx1, wg["out"], 0, dep)
    dw_out = mm_dwout(mix, dx1)
    du, dv, dsg, dws, dbs = sgu_bwd(p2, sp["sgu_norm_g"], sp["w_spatial"], sp["b_spatial"], dmix)
    dseg_b, dconv, dal, ddtb, dog = gdn_backward(p2, sp["conv_w"], sp["a_log"], sp["dt_bias"], sp["o_norm_g"], saved_b, dmix)
    dcq, dck, dcv, dqg, dkg = dattn_bwd(p2, sp["q_norm_g"], sp["k_norm_g"], _alibi_slopes(), *saved_c, dmix)
    dp2 = jnp.concatenate([du, dv] + dseg_b + [dcq, dck, dcv], axis=1)
    dw_in = segments_to_shards(mm_dwin(h1, dp2))
    tok = None if on_weight_grads is None else on_weight_grads(dw_in, dw_out)
    dh1 = mm_dh1(dp2, wg["in"], 0, tok)
    dx, dnorm1 = rmsnorm_bwd(x, sp["norm1_g"], dh1, dx1)
    small = {"norm1_g": dnorm1, "sgu_norm_g": dsg, "w_spatial": dws, "b_spatial": dbs, "conv_w": dconv, "a_log": dal,
             "dt_bias": ddtb, "o_norm_g": dog, "q_norm_g": dqg, "k_norm_g": dkg}
    return dx, dw_in, dw_out, small


_HBM = pl.BlockSpec(memory_space=pltpu.HBM)
_MESH = pl.DeviceIdType.MESH


def _place():
    x, y, c = lax.axis_index("x"), lax.axis_index("y"), lax.axis_index("c")
    chips = [(1 - x, y), (x, 1 - y), (1 - x, 1 - y)]
    return x, y, c, chips


def _rcopy(src, dst, ssem, rsem, dev):
    return pltpu.make_async_remote_copy(src_ref=src, dst_ref=dst, send_sem=ssem, recv_sem=rsem, device_id=dev,
                                        device_id_type=_MESH)


_SEM = pl.BlockSpec(memory_space=pltpu.SEMAPHORE)
_SIDE_EFFECT = pltpu.SideEffectType.DATAFLOW_SIDE_EFFECTING


def _in_hbm(a):
    return pltpu.with_memory_space_constraint(a, pltpu.HBM)


def _split_copy(name, srcs, land_shapes, n_sems, copies):
    n, m = len(srcs), len(land_shapes)
    thru = [pltpu.HBM(a.shape, a.dtype) for a in srcs] + [pltpu.HBM(s.shape, s.dtype) for s in land_shapes]
    sems = (pltpu.SemaphoreType.DMA((n_sems,)), pltpu.SemaphoreType.DMA((n_sems,)))

    def start(dep=None):
        deps = [] if dep is None else [dep]

        def body(*refs):
            ins, lands = refs[:n], refs[n:n + m]
            ssem, rsem, token = refs[n + m + len(deps)], refs[n + m + len(deps) + 1], refs[-1]
            for cp in copies(ins, lands, ssem, rsem)[0]:
                cp.start()
            token[...] = jnp.zeros_like(token)

        out = pl.pallas_call(
            body, name=name + "_start", out_shape=(*sems, *thru, jax.ShapeDtypeStruct((8, HD), F32)),
            in_specs=[_HBM] * (n + m) + [pl.BlockSpec(memory_space=pl.ANY)] * len(deps),
            out_specs=(_SEM, _SEM, *[_HBM] * (n + m), pl.BlockSpec(memory_space=pltpu.VMEM)),
            input_output_aliases={i: 2 + i for i in range(n + m)},
            compiler_params=pltpu.CompilerParams(has_side_effects=_SIDE_EFFECT),
        )(*[_in_hbm(a) for a in srcs], *[_in_hbm(lax.empty(s.shape, s.dtype)) for s in land_shapes], *deps)
        return out[:-1], out[-1]

    def wait(state, after):
        def body(*refs):
            ins, lands, ssem, rsem = refs[:n], refs[n:n + m], refs[n + m], refs[n + m + 1]
            sent, arrivals = copies(ins, lands, ssem, rsem)
            for cp in sent:
                cp.wait_send()
            for cp in arrivals:
                cp.wait_recv()

        out = pl.pallas_call(
            body, name=name + "_wait", out_shape=tuple(thru),
            in_specs=[_HBM] * (n + m) + [_SEM, _SEM, pl.BlockSpec(memory_space=pl.ANY)], out_specs=[_HBM] * (n + m),
            input_output_aliases={i: i for i in range(n + m)},
            compiler_params=pltpu.CompilerParams(has_side_effects=_SIDE_EFFECT),
        )(*state[2:], state[0], state[1], after)
        return list(out[:n]), list(out[n:])

    return start, wait


def gather_direct(shards, tag):
    n = len(shards)

    def copies(ins, lands, ssem, rsem):
        x, y, c, chips = _place()
        s = 2 * x + y
        sibling = (x, y, 1 - c)
        sent, arrivals = [], []
        for a in range(n):
            for u in range(2):
                cp = _rcopy(ins[a].at[u], lands[a].at[s, u], ssem.at[5 * a + u], rsem.at[5 * a + u], sibling)
                sent.append(cp)
                arrivals.append(cp)
            for j, (cx, cy) in enumerate(chips):
                k = 5 * a + 2 + j
                sent.append(_rcopy(ins[a].at[c], lands[a].at[s, c], ssem.at[k], rsem.at[k], (cx, cy, c)))
                arrivals.append(_rcopy(ins[a].at[c], lands[a].at[2 * cx + cy, c], ssem.at[k], rsem.at[k], (cx, cy, c)))
        return sent, arrivals

    lands = [jax.ShapeDtypeStruct((N_CHIPS,) + w.shape, w.dtype) for w in shards]
    return _split_copy("gather_direct_" + tag, shards, lands, 5 * n, copies)


def pass_to_sibling(lands):
    n = len(lands)

    def body(*refs):
        ins = refs[:n]
        ssem, rsem = refs[2 * n:]
        x, y, c, chips = _place()
        sibling = (x, y, 1 - c)
        cps, arrivals = [], []
        for a in range(n):
            for j, (cx, cy) in enumerate(chips):
                t = 2 * cx + cy
                cps.append(_rcopy(ins[a].at[t, c], ins[a].at[t, c], ssem.at[a, j], rsem.at[a, j], sibling))
                arrivals.append(_rcopy(ins[a].at[t, c], ins[a].at[t, 1 - c], ssem.at[a, j], rsem.at[a, j], sibling))
        for cp in cps:
            cp.start()
        for cp, ar in zip(cps, arrivals):
            cp.wait_send()
            ar.wait_recv()

    return pl.pallas_call(
        body, name="pass_to_sibling", in_specs=[_HBM] * n, out_specs=[_HBM] * n,
        out_shape=[jax.ShapeDtypeStruct(a.shape, a.dtype) for a in lands], input_output_aliases={a: a for a in range(n)},
        scratch_shapes=[pltpu.SemaphoreType.DMA((n, 3)), pltpu.SemaphoreType.DMA((n, 3))],
    )(*lands)


def exchange_halves(grads, tag):
    n = len(grads)

    def copies(ins, lands, ssem, rsem):
        x, y, c, _ = _place()
        cps = []
        for a in range(n):
            h = grads[a].shape[1] // 2
            cps.append(_rcopy(ins[a].at[:, pl.ds((1 - c) * h, h)], lands[a], ssem.at[a], rsem.at[a], (x, y, 1 - c)))
        return cps, cps

    lands = [jax.ShapeDtypeStruct((g.shape[0], g.shape[1] // 2, g.shape[2]), g.dtype) for g in grads]
    return _split_copy("exchange_halves_" + tag, grads, lands, n, copies)


def scatter_direct(parts, tag):
    n = len(parts)

    def copies(ins, lands, ssem, rsem):
        x, y, c, chips = _place()
        cps = [_rcopy(ins[a].at[2 * cx + cy], lands[a].at[j], ssem.at[3 * a + j], rsem.at[3 * a + j], (cx, cy, c))
               for a in range(n) for j, (cx, cy) in enumerate(chips)]
        return cps, cps

    lands = [jax.ShapeDtypeStruct((3,) + p.shape[1:], p.dtype) for p in parts]
    return _split_copy("scatter_direct_" + tag, parts, lands, 3 * n, copies)


def share_halves(halves):
    n = len(halves)

    def body(*refs):
        ins, outs = refs[:n], refs[n:2 * n]
        ssem, rsem = refs[2 * n:]
        x, y, c, _ = _place()
        cps = [_rcopy(ins[i], outs[i], ssem.at[i], rsem.at[i], (x, y, 1 - c)) for i in range(n)]
        for cp in cps:
            cp.start()
        for cp in cps:
            cp.wait()

    return pl.pallas_call(
        body, name="share_halves", in_specs=[_HBM] * n, out_specs=[_HBM] * n,
        out_shape=[jax.ShapeDtypeStruct(h.shape, h.dtype) for h in halves],
        scratch_shapes=[pltpu.SemaphoreType.DMA((n,)), pltpu.SemaphoreType.DMA((n,))],
    )(*halves)


_ADAMW_BLOCK_BYTES = 3 << 19


def adamw_shard(w, m, v, mine, theirs, c, name):
    _, r, cw = w.shape
    h, cg = mine[0].shape
    tr = next(t for t in (256, 176, 128) if h % t == 0 and t * cg * 4 <= _ADAMW_BLOCK_BYTES)
    nb = h // tr
    wblk = pl.BlockSpec((None, tr, cw), lambda l, i, c_ref: (l, i, 0))
    gblk = lambda layer, own: pl.BlockSpec((tr, cg), lambda l, i, c_ref: (_held_block(l, i, c_ref, layer, own, nb), 0))
    return _adamw_halves(w, m, v, mine, theirs, c, name, (DEPTH, r // tr), wblk, gblk, nb, cw)


def _held_block(l, i, c_ref, layer, own, nb):
    in_use = (l == layer) & (((i // nb) == c_ref[0]) == own)
    return jnp.where(in_use, i % nb, 0)


def _adamw_halves(w, m, v, mine, theirs, c, name, grid, wblk, gblk, nb, cw):
    def body(c_ref, w_ref, m_ref, v_ref, m0, m1, t0, t1, g_ref, d_ref, nm_ref, nv_ref):
        is_mine = (pl.program_id(1) // nb) == c_ref[0]
        first = pl.program_id(0) == 0
        gg = jnp.where(is_mine, jnp.where(first, m0[:, :cw], m1[:, :cw]), jnp.where(first, t0[:, :cw], t1[:, :cw]))
        nm = ADAM_B1 * m_ref[...] + (1.0 - ADAM_B1) * gg
        nv = ADAM_B2 * v_ref[...] + (1.0 - ADAM_B2) * (gg * gg)
        m_hat = nm / (1.0 - ADAM_B1 ** ADAM_STEP)
        v_hat = nv / (1.0 - ADAM_B2 ** ADAM_STEP)
        g_ref[...] = gg
        d_ref[...] = -ADAM_LR * (m_hat / (jnp.sqrt(v_hat) + ADAM_EPS) + ADAM_WD * w_ref[...])
        nm_ref[...] = nm
        nv_ref[...] = nv

    return pl.pallas_call(
        body, name=name,
        grid_spec=pltpu.PrefetchScalarGridSpec(
            num_scalar_prefetch=1, grid=grid,
            in_specs=[wblk] * 3 + [gblk(0, True), gblk(1, True), gblk(0, False), gblk(1, False)], out_specs=[wblk] * 4),
        out_shape=[jax.ShapeDtypeStruct(w.shape, F32)] * 4, compiler_params=_cparams(("arbitrary", "arbitrary")),
    )(c, w, m, v, mine[0], mine[1], theirs[0], theirs[1])


def adamw_shard_t(wt, mt, vt, mine_t, theirs_t, c, name):
    _, cw, r = wt.shape
    h = mine_t[0].shape[1]
    tc = 256
    nb = h // tc
    wblk = pl.BlockSpec((None, cw, tc), lambda l, j, c_ref: (l, 0, j))
    gblk = lambda layer, own: pl.BlockSpec((cw, tc), lambda l, j, c_ref: (0, _held_block(l, j, c_ref, layer, own, nb)))
    return _adamw_halves(wt, mt, vt, mine_t, theirs_t, c, name, (DEPTH, r // tc), wblk, gblk, nb, cw)


def _half_rows(h, cols):
    for tr in (512, 256, 352, 128, 64):
        if h % tr == 0 and tr * cols * 4 <= 6 * 1024 * 1024:
            return tr
    raise ValueError((h, cols))


def add_sibling(grad, recv, c):
    _, r, cols = grad.shape
    h = r // 2
    tr = _half_rows(h, cols)
    nb = h // tr

    def body(c_ref, g_ref, r_ref, o_ref):
        o_ref[...] = (g_ref[...].astype(F32) + r_ref[...].astype(F32)).astype(BF16)

    return pl.pallas_call(
        body, name="add_sibling",
        grid_spec=pltpu.PrefetchScalarGridSpec(
            num_scalar_prefetch=1, grid=(N_CHIPS, nb),
            in_specs=[pl.BlockSpec((None, tr, cols), lambda t, i, c_ref: (t, c_ref[0] * nb + i, 0)),
                      pl.BlockSpec((None, tr, cols), lambda t, i, c_ref: (t, i, 0))],
            out_specs=pl.BlockSpec((None, tr, cols), lambda t, i, c_ref: (t, i, 0))),
        out_shape=jax.ShapeDtypeStruct((N_CHIPS, h, cols), BF16), compiler_params=_cparams(("arbitrary", "arbitrary")),
    )(c, grad, recv)


def add_chips(part, recv, s):
    _, h, cols = part.shape
    tr = _half_rows(h, cols)

    def body(s_ref, p_ref, r_ref, o_ref):
        o_ref[...] = ((p_ref[...].astype(F32) + r_ref[0].astype(F32)) + r_ref[1].astype(F32)) + r_ref[2].astype(F32)

    return pl.pallas_call(
        body, name="add_chips",
        grid_spec=pltpu.PrefetchScalarGridSpec(
            num_scalar_prefetch=1, grid=(h // tr,),
            in_specs=[pl.BlockSpec((None, tr, cols), lambda i, s_ref: (s_ref[0], i, 0)),
                      pl.BlockSpec((3, tr, cols), lambda i, s_ref: (0, i, 0))],
            out_specs=pl.BlockSpec((tr, cols), lambda i, s_ref: (i, 0))),
        out_shape=jax.ShapeDtypeStruct((h, cols), F32), compiler_params=_cparams(("arbitrary",)),
    )(s, part, recv)


def allreduce_small(vec):
    rows = vec.shape[0]

    def body(v_ref, o_ref, buf, ssem, rsem, lsem):
        x, y, c, chips = _place()
        me, sibling = (x, y, c), (x, y, 1 - c)

        def blk(px, py, pc):
            return buf.at[4 * px + 2 * py + pc]

        def copy(k, block, to, src=None):
            return _rcopy(blk(*block) if src is None else src, blk(*block), ssem.at[k], rsem.at[k], to)

        mine = pltpu.make_async_copy(v_ref, blk(*me), lsem)
        mine.start()
        first = [copy(0, me, sibling, src=v_ref)] + [copy(1 + j, me, (*chip, c), src=v_ref) for j, chip in enumerate(chips)]
        for cp in first:
            cp.start()
        passed = [copy(4 + j, (*chip, c), sibling) for j, chip in enumerate(chips)]
        for j, chip in enumerate(chips):
            copy(1 + j, (*chip, c), me).wait_recv()
            passed[j].start()
        copy(0, sibling, me).wait_recv()
        for j, chip in enumerate(chips):
            copy(4 + j, (*chip, 1 - c), me).wait_recv()
        for cp in first + passed:
            cp.wait_send()
        mine.wait()
        acc = buf[0]
        for d in range(1, N_DEV):
            acc = acc + buf[d]
        o_ref[...] = acc

    vm = pl.BlockSpec(memory_space=pltpu.VMEM)
    return pl.pallas_call(
        body, name="allreduce_small", in_specs=[vm], out_specs=vm, out_shape=jax.ShapeDtypeStruct(vec.shape, F32),
        scratch_shapes=[pltpu.VMEM((N_DEV, rows, HD), F32), pltpu.SemaphoreType.DMA((7,)), pltpu.SemaphoreType.DMA((7,)),
                        pltpu.SemaphoreType.DMA],
        compiler_params=pltpu.CompilerParams(vmem_limit_bytes=VMEM_LIMIT),
    )(vec)


SMALL_NAMES = ("norm1_g", "sgu_norm_g", "w_spatial", "b_spatial", "conv_w", "a_log", "dt_bias", "o_norm_g", "q_norm_g",
               "k_norm_g", "norm2_g")


def small_params(l, p, conv_full):
    return {"norm1_g": p["norm1_g"][l][None], "sgu_norm_g": p["sgu_norm_g"][l][:, None, :], "w_spatial": p["w_spatial"][l],
            "b_spatial": p["b_spatial"][l][..., None], "conv_w": conv_full[l], "a_log": p["a_log"][l], "dt_bias": p["dt_bias"][l],
            "o_norm_g": p["o_norm_g"][l][None], "q_norm_g": p["q_norm_g"][l][None], "k_norm_g": p["k_norm_g"][l][None],
            "norm2_g": p["norm2_g"][l][None]}


_PACK_TILE = 8 * HD


def _pack(arrays):
    flat = jnp.concatenate([a.reshape(-1) for a in arrays])
    pad = -flat.shape[0] % _PACK_TILE
    return jnp.pad(flat, (0, pad)).reshape(-1, HD)


def _unpack(packed, shapes):
    flat, out, off = packed.reshape(-1), [], 0
    for shp in shapes:
        n = int(np.prod(shp))
        out.append(flat[off:off + n].reshape(shp))
        off += n
    return out


WEIGHT_ORDER = ("norm1_g", "w_in", "sgu_norm_g", "w_spatial", "b_spatial", "conv_w", "a_log", "dt_bias", "o_norm_g", "q_norm_g",
                "k_norm_g", "w_out", "norm2_g", "w_gate_up", "w_down")


def kernel(x, norm1_g, w_in, sgu_norm_g, w_spatial, b_spatial, conv_w, a_log, dt_bias, o_norm_g, q_norm_g, k_norm_g, w_out, norm2_g, w_gate_up, w_down, loss_target, m_norm1_g, m_w_in, m_sgu_norm_g, m_w_spatial, m_b_spatial, m_conv_w, m_a_log, m_dt_bias, m_o_norm_g, m_q_norm_g, m_k_norm_g, m_w_out, m_norm2_g, m_w_gate_up, m_w_down, v_norm1_g, v_w_in, v_sgu_norm_g, v_w_spatial, v_b_spatial, v_conv_w, v_a_log, v_dt_bias, v_o_norm_g, v_q_norm_g, v_k_norm_g, v_w_out, v_norm2_g, v_w_gate_up, v_w_down):
    w = dict(norm1_g=norm1_g, w_in=w_in, sgu_norm_g=sgu_norm_g, w_spatial=w_spatial, b_spatial=b_spatial, conv_w=conv_w,
             a_log=a_log, dt_bias=dt_bias, o_norm_g=o_norm_g, q_norm_g=q_norm_g, k_norm_g=k_norm_g, w_out=w_out,
             norm2_g=norm2_g, w_gate_up=w_gate_up, w_down=w_down)
    m = dict(norm1_g=m_norm1_g, w_in=m_w_in, sgu_norm_g=m_sgu_norm_g, w_spatial=m_w_spatial, b_spatial=m_b_spatial,
             conv_w=m_conv_w, a_log=m_a_log, dt_bias=m_dt_bias, o_norm_g=m_o_norm_g, q_norm_g=m_q_norm_g, k_norm_g=m_k_norm_g,
             w_out=m_w_out, norm2_g=m_norm2_g, w_gate_up=m_w_gate_up, w_down=m_w_down)
    v = dict(norm1_g=v_norm1_g, w_in=v_w_in, sgu_norm_g=v_sgu_norm_g, w_spatial=v_w_spatial, b_spatial=v_b_spatial,
             conv_w=v_conv_w, a_log=v_a_log, dt_bias=v_dt_bias, o_norm_g=v_o_norm_g, q_norm_g=v_q_norm_g, k_norm_g=v_k_norm_g,
             w_out=v_w_out, norm2_g=v_norm2_g, w_gate_up=v_w_gate_up, w_down=v_w_down)
    chip = (2 * lax.axis_index("x") + lax.axis_index("y")).astype(jnp.int32)
    core = lax.axis_index("c").astype(jnp.int32)

    in_pad = IN_SHARD_PAD - IN_SHARD
    w_in_pad = jnp.pad(w_in, ((0, 0), (0, 0), (0, in_pad)))

    halves_of = lambda a: a.reshape(2, a.shape[0] // 2, a.shape[1])
    start_0, wait_0 = gather_direct([halves_of(w_in_pad[0].astype(BF16)), halves_of(conv_w[0])], "mix0")
    state_0, token_0 = start_0()
    bf_halves = lambda a: halves_of((a + token_0[0, 0]).astype(BF16))

    def ffn_shards(l):
        return [bf_halves(w_gate_up[l]), bf_halves(w_down[l]), bf_halves(w_out[l])]

    def mixer_shards(l):
        return [bf_halves(w_in_pad[l]), halves_of(conv_w[l])]

    def mixer_weights(g):
        g_in, g_conv = g
        return (shards_to_segments(g_in.reshape(N_CHIPS, D, IN_SHARD_PAD))[None],
                g_conv.reshape(N_CHIPS, B_CONV, -1).transpose(1, 0, 2).reshape(B_CONV, 3 * B_WIDTH))

    def ffn_weights(g, w_in_seg):
        g_gu, g_down, g_out = g
        return {"in": w_in_seg, "out": g_out.reshape(1, D, D), "gu": g_gu.reshape(1, N_CHIPS, D, GU_SHARD),
                "down": None if g_down is None else g_down.reshape(1, FFN, D)}

    def layer_params(l, conv_full):
        return small_params(0, {n: w[n][l:l + 1] for n in SMALL_NAMES if n != "conv_w"}, conv_full[None])

    gu0, down0, out0 = ffn_shards(0)
    start_a, wait_a = gather_direct([gu0, out0], "ffn0")
    start_b, wait_b = gather_direct([down0] + mixer_shards(1), "mid")
    start_c, wait_c = gather_direct(ffn_shards(1), "ffn1")
    state_a, token_a = start_a(token_0)
    state_b, token_b = start_b(token_a)
    state_c, token_c = start_c(token_b)
    h1_0 = rmsnorm_fwd(x[0], norm1_g[0][None], token_c)
    w_in0, conv0 = mixer_weights(pass_to_sibling(wait_0(state_0, h1_0)[1]))
    sps = [layer_params(0, conv0), None]
    mix0, saved_m0 = mixers_forward(x[0], w_in0, sps[0], h1=h1_0)
    g_gu0, g_out0 = pass_to_sibling(wait_a(state_a, mix0)[1])
    wg0 = ffn_weights((g_gu0, None, g_out0), w_in0)
    x1_0, h2_0, gu_0, act_0 = ffn_up(x[0], mix0, wg0["out"], wg0["gu"], sps[0])
    g_down0, g_in1, g_conv1 = pass_to_sibling(wait_b(state_b, act_0)[1])
    wg0["down"] = g_down0.reshape(1, FFN, D)
    x1 = mm_down(act_0, wg0["down"], 0, x1_0)
    saved_f0 = (x1_0, h2_0, gu_0, act_0)
    w_in1, conv1 = mixer_weights((g_in1, g_conv1))
    sps[1] = layer_params(1, conv1)
    mix1, saved_m1 = mixers_forward(x1, w_in1, sps[1])
    wg1 = ffn_weights(pass_to_sibling(wait_c(state_c, mix1)[1]), w_in1)
    x2, saved_f1 = ffn_forward(x1, mix1, wg1, sps[1])
    saved1 = (saved_m1, saved_f1)
    dx, loss_tile = loss_and_grad(x2, loss_target[0])

    def sibling_exchange(tag, by_chip):
        cell = {}

        def on_weight_grads(*dws):
            start, wait = exchange_halves([f(t) for f, t in zip(by_chip, dws)], tag)
            state, token = start()
            cell["wait"] = functools.partial(wait, state)
            return token

        return on_weight_grads, cell

    def start_scatter(cell, after, tag, dep=None):
        grads, from_sibling = cell["wait"](after)
        parts = [add_sibling(g, r, core.reshape(1)) for g, r in zip(grads, from_sibling)]
        start, wait = scatter_direct(parts, tag)
        state, token = start(dep)
        return functools.partial(wait, state), token

    smalls = [None] * DEPTH
    same = lambda t: t
    ffn_by_chip = (same, lambda t: t.reshape(N_CHIPS, DOWN_SHARD, D))
    mix_by_chip = (same, lambda t: t.reshape(N_CHIPS, OUT_SHARD, D))
    hook, cell_f1 = sibling_exchange("ffn1", ffn_by_chip)
    dx1, dnorm2_1, _, _ = ffn_backward(dx, wg1, sps[1], saved1[1], on_weight_grads=hook)
    wait_f1, tok_f1 = start_scatter(cell_f1, dx1, "ffn1")
    hook, cell_m1 = sibling_exchange("mix1", mix_by_chip)
    dx, _, _, small1 = mixers_backward(dx1, wg1, sps[1], saved1[0], dep=tok_f1, on_weight_grads=hook)
    smalls[1] = {**small1, "norm2_g": dnorm2_1}
    wait_m1, tok_m1 = start_scatter(cell_m1, dx, "mix1")
    hook, cell_f0 = sibling_exchange("ffn0", ffn_by_chip)
    dx1, dnorm2_0, _, _ = ffn_backward(dx, wg0, sps[0], saved_f0, dep=tok_m1, on_weight_grads=hook)
    wait_f0, tok_f0 = start_scatter(cell_f0, dx1, "ffn0")
    hook, cell_m0 = sibling_exchange("mix0", mix_by_chip)
    dx, _, _, small0 = mixers_backward(dx1, wg0, sps[0], saved_m0, dep=tok_f0, on_weight_grads=hook)
    smalls[0] = {**small0, "norm2_g": dnorm2_0}

    grad, delta, new_m, new_v = {}, {}, {}, {}
    stacked = [jnp.stack([smalls[l][n] for l in range(DEPTH)]) for n in SMALL_NAMES]
    total = allreduce_small(_pack(stacked + [loss_tile[0, :1]]))
    shapes = [(DEPTH, B_CONV, 3 * B_WIDTH) if n == "conv_w" else w[n].shape for n in SMALL_NAMES]
    small_grads = dict(zip(SMALL_NAMES, _unpack(total, shapes + [(1,)])[:-1]))
    loss = _unpack(total, shapes + [(1,)])[-1][0]
    conv_cols = conv_w.shape[-1]
    small_grads["conv_w"] = lax.dynamic_slice_in_dim(small_grads["conv_w"], chip * conv_cols, conv_cols, axis=2)
    grad.update(small_grads)
    sshapes = [w[n].shape for n in SMALL_NAMES]
    packed = [_pack([d[n] for n in SMALL_NAMES]) for d in (w, grad, m, v)]
    for dst, t in zip((delta, new_m, new_v), adamw(*packed, "adamw_small")):
        dst.update(zip(SMALL_NAMES, _unpack(t, sshapes)))

    wait_m0, tok_m0 = start_scatter(cell_m0, dx, "mix0", dep=total)
    (pf0, rf0), (pm1, rm1), (pf1, rf1) = (wt(tok_m0) for wt in (wait_f0, wait_m1, wait_f1))

    def reduce_group(parts, from_chips):
        mine = [add_chips(p, r, chip.reshape(1)) for p, r in zip(parts, from_chips)]
        return mine, list(share_halves(mine))

    mine_f, theirs_f = reduce_group(pf0 + pf1 + pm1, rf0 + rf1 + rm1)
    for a, n in enumerate(("w_gate_up", "w_down")):
        grad[n], delta[n], new_m[n], new_v[n] = adamw_shard(w[n], m[n], v[n], [mine_f[a], mine_f[2 + a]],
                                                            [theirs_f[a], theirs_f[2 + a]], core.reshape(1), "adamw_" + n)

    mine_m0, theirs_m0 = reduce_group(*wait_m0(new_v["w_down"]))
    tr_ = lambda t: jnp.swapaxes(t, -1, -2)
    cut = lambda t: tr_(t[:, :IN_SHARD])
    res = adamw_shard_t(tr_(w_in), tr_(m_w_in), tr_(v_w_in), [cut(mine_m0[0]), cut(mine_f[4])],
                        [cut(theirs_m0[0]), cut(theirs_f[4])], core.reshape(1), "adamw_w_in")
    grad["w_in"], delta["w_in"], new_m["w_in"], new_v["w_in"] = (tr_(t) for t in res)
    grad["w_out"], delta["w_out"], new_m["w_out"], new_v["w_out"] = adamw_shard(
        w_out, m_w_out, v_w_out, [mine_m0[1], mine_f[5]], [theirs_m0[1], theirs_f[5]], core.reshape(1), "adamw_w_out")

    out = [loss, dx[None]]
    for d in (grad, delta, new_m, new_v):
        out += [d[n] for n in WEIGHT_ORDER]
    return tuple(out)
```

```python
import functools
import math

import numpy as np
import jax
import jax.numpy as jnp
from jax import lax
from jax.experimental import pallas as pl
from jax.experimental.pallas import tpu as pltpu

F32 = jnp.float32
BF16 = jnp.bfloat16
HI = lax.Precision.HIGH

T = 2048
D = 2048
DEPTH = 2
HD = 128
A_GROUPS, A_WIDTH, A_CHUNK = 4, 512, 128
B_HEADS, B_WIDTH, B_CONV, B_CHUNK = 6, 768, 4, 64
C_HEADS, C_WIDTH, C_BLOCK = 6, 768, 128
C_BRANCHES = ((128, 1), (512, 4), (2048, 16))
FFN = 5632
IN_TOTAL = 6412
EPS = 1e-6
N_CHIPS = 4
N_DEV = 8
IN_SHARD = IN_TOTAL // N_CHIPS
IN_SHARD_PAD = 1664
GU_SHARD = 2 * FFN // N_CHIPS
OUT_SHARD = D // N_CHIPS
DOWN_SHARD = FFN // N_CHIPS
P_AU, P_AV, P_BQ, P_BK, P_BV, P_BG, P_BB, P_CQ, P_CK, P_CV, P_END = (
    0, 512, 1024, 1792, 2560, 3328, 4096, 4224, 4992, 5760, 6528)
GATE_COLS = 4108
VMEM_LIMIT = 56 * 1024 * 1024

ADAM_LR, ADAM_B1, ADAM_B2, ADAM_EPS, ADAM_WD, ADAM_STEP = 0.001, 0.9, 0.999, 1e-08, 0.01, 10


def _cparams(sem, vmem=VMEM_LIMIT):
    return pltpu.CompilerParams(dimension_semantics=sem, vmem_limit_bytes=vmem)


def _dims(nd, ta, tb):
    off = nd - 2
    ca = off + (0 if ta else 1)
    cb = off + (1 if tb else 0)
    batch = ((0,), (0,)) if nd == 3 else ((), ())
    return (((ca,), (cb,)), batch)


def _raw_mm(a, b, ta, tb, hi):
    if hi:
        return lax.dot_general(a, b, _dims(a.ndim, ta, tb), precision=HI, preferred_element_type=F32)
    return lax.dot_general(a.astype(BF16), b.astype(BF16), _dims(a.ndim, ta, tb), preferred_element_type=F32)


@functools.partial(jax.custom_vjp, nondiff_argnums=(2, 3, 4))
def _mm(a, b, ta=False, tb=False, hi=False):
    return _raw_mm(a, b, ta, tb, hi)


def _mm_fwd(a, b, ta, tb, hi):
    return _raw_mm(a, b, ta, tb, hi), (a, b)


def _mm_bwd(ta, tb, hi, res, g):
    a, b = res
    da = _raw_mm(g, b, False, not tb, False) if not ta else _raw_mm(b, g, tb, True, False)
    db = _raw_mm(a, g, not ta, False, False) if not tb else _raw_mm(g, a, True, ta, False)
    return da.astype(a.dtype), db.astype(b.dtype)


_mm.defvjp(_mm_fwd, _mm_bwd)


def _rms(x, g):
    return x * lax.rsqrt(jnp.mean(x * x, axis=-1, keepdims=True) + EPS) * g


def _gelu(x):
    return 0.5 * x * (1.0 + jnp.tanh(math.sqrt(2.0 / math.pi) * (x + 0.044715 * (x * x * x))))


def _sigmoid(x):
    return 1.0 / (1.0 + jnp.exp(-x))


def _silu(x):
    return x * _sigmoid(x)


def _softplus(x):
    return jnp.maximum(x, 0.0) + jnp.log(1.0 + jnp.exp(-jnp.abs(x)))


def _iota(shape, dim):
    return lax.broadcasted_iota(jnp.int32, shape, dim)


def _sgu_fn(u, v, sg, w, b):
    nc = T // A_CHUNK
    ug = _gelu(u)
    vn = _rms(_gelu(v), sg)
    causal = _iota((A_CHUNK, A_CHUNK), 0) >= _iota((A_CHUNK, A_CHUNK), 1)
    wm = jnp.where(causal, w, 0.0)
    wb = jnp.broadcast_to(wm[None], (nc, A_CHUNK, A_CHUNK))
    z = _mm(wb, vn.reshape(nc, A_CHUNK, HD)) + b[None]
    return ug * z.reshape(T, HD)


def _sgu_specs():
    col = lambda off: pl.BlockSpec((T, HD), lambda g, off=off: (0, off + g))
    par = [pl.BlockSpec((None, 1, HD), lambda g: (g, 0, 0)),
           pl.BlockSpec((None, A_CHUNK, A_CHUNK), lambda g: (g, 0, 0)),
           pl.BlockSpec((None, A_CHUNK, 1), lambda g: (g, 0, 0))]
    return col, par


def sgu_fwd(p2, sg, w, b):
    col, par = _sgu_specs()

    def body(u_ref, v_ref, sg_ref, w_ref, b_ref, y_ref):
        y_ref[...] = _sgu_fn(u_ref[...], v_ref[...], sg_ref[...], w_ref[...], b_ref[...]).astype(BF16)

    return pl.pallas_call(
        body, name="sgu_fwd", grid=(A_GROUPS,),
        in_specs=[col(P_AU // HD), col(P_AV // HD)] + par,
        out_specs=pl.BlockSpec((T, HD), lambda g: (0, g)),
        out_shape=jax.ShapeDtypeStruct((T, A_WIDTH), BF16),
        compiler_params=_cparams(("arbitrary",)),
    )(p2, p2, sg, w, b)


def sgu_bwd(p2, sg, w, b, dmix):
    col, par = _sgu_specs()

    def body(u_ref, v_ref, sg_ref, w_ref, b_ref, dy_ref, du_ref, dv_ref, dsg_ref, dw_ref, db_ref):
        _, vjp = jax.vjp(_sgu_fn, u_ref[...], v_ref[...], sg_ref[...], w_ref[...], b_ref[...])
        du, dv, dsg, dw, db = vjp(dy_ref[...])
        du_ref[...] = du.astype(BF16)
        dv_ref[...] = dv.astype(BF16)
        dsg_ref[...] = dsg
        dw_ref[...] = dw
        db_ref[...] = db

    gcol = pl.BlockSpec((T, HD), lambda g: (0, g))
    return pl.pallas_call(
        body, name="sgu_bwd", grid=(A_GROUPS,),
        in_specs=[col(P_AU // HD), col(P_AV // HD)] + par + [gcol],
        out_specs=[gcol, gcol] + par,
        out_shape=[jax.ShapeDtypeStruct((T, A_WIDTH), BF16), jax.ShapeDtypeStruct((T, A_WIDTH), BF16),
                   jax.ShapeDtypeStruct((A_GROUPS, 1, HD), F32), jax.ShapeDtypeStruct((A_GROUPS, A_CHUNK, A_CHUNK), F32),
                   jax.ShapeDtypeStruct((A_GROUPS, A_CHUNK, 1), F32)],
        compiler_params=_cparams(("arbitrary",)),
    )(p2, p2, sg, w, b, dmix)


def _attn_fn(q, k, v, qg, kg, slope, *, dil, nb):
    n = T // C_BLOCK
    qb = _rms(q, qg).reshape(n, C_BLOCK, HD)
    kb = _rms(k, kg).reshape(n, C_BLOCK, HD)
    vb = v.reshape(n, C_BLOCK, HD)
    scale = HD ** -0.5
    qi = _iota((n, C_BLOCK, C_BLOCK), 1)
    kj = _iota((n, C_BLOCK, C_BLOCK), 2)
    sl = slope[None] * float(dil)
    d_cur = qi - kj
    sc = jnp.where(d_cur >= 0, _mm(qb, kb, tb=True) * scale - sl * d_cur.astype(F32), -jnp.inf)
    mx = jnp.max(sc, axis=-1, keepdims=True)
    if nb > 1:
        kp = jnp.concatenate([jnp.zeros((1, C_BLOCK, HD), F32), kb[:-1]], axis=0)
        vp = jnp.concatenate([jnp.zeros((1, C_BLOCK, HD), F32), vb[:-1]], axis=0)
        has_prev = (_iota((n, C_BLOCK, C_BLOCK), 0) % nb) > 0
        d_prev = C_BLOCK + qi - kj
        sp = jnp.where((kj >= qi) & has_prev, _mm(qb, kp, tb=True) * scale - sl * d_prev.astype(F32), -jnp.inf)
        mx = jnp.maximum(mx, jnp.max(sp, axis=-1, keepdims=True))
    p = jnp.exp(sc - mx)
    den = jnp.sum(p, axis=-1, keepdims=True)
    if nb > 1:
        pp = jnp.exp(sp - mx)
        den = den + jnp.sum(pp, axis=-1, keepdims=True)
    inv = 1.0 / den
    out = _mm(p * inv, vb)
    if nb > 1:
        out = out + _mm(pp * inv, vp)
    lse = mx + jnp.log(den)
    return out.reshape(T, HD), jnp.broadcast_to(lse, (n, C_BLOCK, HD)).reshape(T, HD)


def _combine_fn(o1, o2, o3, l1, l2, l3):
    mx = jnp.maximum(jnp.maximum(l1, l2), l3)
    e1, e2, e3 = jnp.exp(l1 - mx), jnp.exp(l2 - mx), jnp.exp(l3 - mx)
    r = 1.0 / (e1 + e2 + e3)
    return (e1 * r) * o1 + (e2 * r) * o2 + (e3 * r) * o3


def _branch_blocks(dil):
    return -(-(T // dil) // C_BLOCK)


def _load_branch_order(ref, dil):
    if dil == 1:
        return ref[...]
    seg = T // dil
    return jnp.concatenate([ref[pl.ds(r, seg, stride=dil), :] for r in range(dil)], axis=0)


def _store_position_order(ref, val, dil, add=False):
    seg = T // dil
    for r in range(dil):
        rows = slice(None) if dil == 1 else pl.ds(r, seg, stride=dil)
        piece = val if dil == 1 else val[r * seg:(r + 1) * seg]
        if add:
            ref[rows, :] += piece
        else:
            ref[rows, :] = piece


def _dattn_specs():
    col = lambda off: pl.BlockSpec((T, HD), lambda h, off=off: (0, off // HD + h))
    row = pl.BlockSpec((1, HD), lambda h: (0, 0))
    slope = pl.BlockSpec((None, 1, HD), lambda h: (h, 0, 0))
    return [col(P_CQ), col(P_CK), col(P_CV), row, row, slope]


def _dattn_branches(q_ref, k_ref, v_ref, qg, kg, slope, o_scr, l_scr):
    for b, (_, dil) in enumerate(C_BRANCHES):
        q, k, v = (_load_branch_order(r, dil) for r in (q_ref, k_ref, v_ref))
        o, l = _attn_fn(q, k, v, qg, kg, slope, dil=dil, nb=_branch_blocks(dil))
        _store_position_order(o_scr.at[b], o, dil)
        _store_position_order(l_scr.at[b], l, dil)


def dattn_fwd(p2, qg, kg, slopes):
    per_branch = pl.BlockSpec((3, T, HD), lambda h: (0, 0, h))

    def body(q_ref, k_ref, v_ref, qg_ref, kg_ref, s_ref, y_ref, o_ref, l_ref):
        _dattn_branches(q_ref, k_ref, v_ref, qg_ref[...], kg_ref[...], s_ref[...], o_ref, l_ref)
        y_ref[...] = _combine_fn(o_ref[0], o_ref[1], o_ref[2], l_ref[0], l_ref[1], l_ref[2]).astype(BF16)

    return pl.pallas_call(
        body, name="dattn_fwd", grid=(C_HEADS,), in_specs=_dattn_specs(),
        out_specs=[pl.BlockSpec((T, HD), lambda h: (0, h)), per_branch, per_branch],
        out_shape=[jax.ShapeDtypeStruct((T, C_WIDTH), BF16)] + [jax.ShapeDtypeStruct((3, T, C_WIDTH), F32)] * 2,
        compiler_params=_cparams(("arbitrary",)),
    )(p2, p2, p2, qg, kg, slopes)


def dattn_bwd(p2, qg, kg, slopes, outs, lses, dmix):
    hcol = pl.BlockSpec((T, HD), lambda h: (0, h))
    row = pl.BlockSpec((1, HD), lambda h: (0, 0))
    dy = pl.BlockSpec((T, HD), lambda h: (0, (A_WIDTH + B_WIDTH) // HD + h))
    per_branch = pl.BlockSpec((3, T, HD), lambda h: (0, 0, h))

    def body(q_ref, k_ref, v_ref, qg_ref, kg_ref, s_ref, o_scr, l_scr, dy_ref, dq_ref, dk_ref, dv_ref, dqg_ref, dkg_ref, g_scr,
             acc):
        qg, kg, slope = qg_ref[...], kg_ref[...], s_ref[...]
        _, vjp = jax.vjp(_combine_fn, o_scr[0], o_scr[1], o_scr[2], l_scr[0], l_scr[1], l_scr[2])
        for i, g in enumerate(vjp(dy_ref[...])):
            g_scr[i] = g

        @pl.when(pl.program_id(0) == 0)
        def _():
            dqg_ref[...] = jnp.zeros_like(dqg_ref)
            dkg_ref[...] = jnp.zeros_like(dkg_ref)

        for b, (_, dil) in enumerate(C_BRANCHES):
            q, k, v = (_load_branch_order(r, dil) for r in (q_ref, k_ref, v_ref))
            do, dl = _load_branch_order(g_scr.at[b], dil), _load_branch_order(g_scr.at[3 + b], dil)
            fn = functools.partial(_attn_fn, dil=dil, nb=_branch_blocks(dil))
            _, vjp_b = jax.vjp(lambda a, b_, c, d, e, fn=fn: fn(a, b_, c, d, e, slope), q, k, v, qg, kg)
            dq, dk, dv, dqg, dkg = vjp_b((do, dl))
            for i, val in enumerate((dq, dk, dv)):
                _store_position_order(acc.at[i], val, dil, add=b > 0)
            dqg_ref[...] += dqg
            dkg_ref[...] += dkg
        for i, ref in enumerate((dq_ref, dk_ref, dv_ref)):
            ref[...] = acc[i].astype(BF16)

    scr = lambda n: pltpu.VMEM((n, T, HD), F32)
    return pl.pallas_call(
        body, name="dattn_bwd", grid=(C_HEADS,), in_specs=_dattn_specs() + [per_branch, per_branch, dy],
        out_specs=[hcol, hcol, hcol, row, row],
        out_shape=[jax.ShapeDtypeStruct((T, C_WIDTH), BF16)] * 3 + [jax.ShapeDtypeStruct((1, HD), F32)] * 2,
        scratch_shapes=[scr(6), scr(3)], compiler_params=_cparams(("arbitrary",)),
    )(p2, p2, p2, qg, kg, slopes, outs, lses, dmix)


_NCH = T // B_CHUNK


def _conv_taps(x, w_ref):
    rows = _iota(x.shape, 0)
    taps = []
    for j in range(B_CONV):
        s = B_CONV - 1 - j
        taps.append(x if s == 0 else jnp.where(rows >= s, pltpu.roll(x, s, 0), 0.0))
    pre = sum(w_ref[j:j + 1, :] * taps[j] for j in range(B_CONV))
    return pre, taps


def _conv_post(pre, mode):
    y = _silu(pre)
    if mode == "v":
        return y
    y = y * lax.rsqrt(jnp.sum(y * y, axis=-1, keepdims=True) + EPS)
    return y * (HD ** -0.5) if mode == "q" else y


def conv_fwd(p2, conv_w, mode):
    idx = "qkv".index(mode)
    xcol = pl.BlockSpec((T, HD), lambda h: (0, P_BQ // HD + B_HEADS * idx + h))
    wcol = pl.BlockSpec((B_CONV, HD), lambda h: (0, B_HEADS * idx + h))
    hcol = pl.BlockSpec((T, HD), lambda h: (0, h))

    def body(x_ref, w_ref, y_ref):
        pre, _ = _conv_taps(x_ref[...], w_ref)
        y_ref[...] = _conv_post(pre, mode)

    return pl.pallas_call(
        body, name=f"conv_fwd_{mode}", grid=(B_HEADS,), in_specs=[xcol, wcol], out_specs=hcol,
        out_shape=jax.ShapeDtypeStruct((T, B_WIDTH), F32), compiler_params=_cparams(("arbitrary",)),
    )(p2, conv_w)


def conv_bwd(p2, conv_w, dys, mode):
    idx = "qkv".index(mode)
    xcol = pl.BlockSpec((T, HD), lambda h: (0, P_BQ // HD + B_HEADS * idx + h))
    wcol = pl.BlockSpec((B_CONV, HD), lambda h: (0, B_HEADS * idx + h))
    hcol = pl.BlockSpec((T, HD), lambda h: (0, h))
    wout = pl.BlockSpec((B_CONV, HD), lambda h: (0, h))

    def body(x_ref, w_ref, *rest):
        dy_refs, (dx_ref, dw_ref) = rest[:-2], rest[-2:]
        pre, taps = _conv_taps(x_ref[...], w_ref)
        _, vjp = jax.vjp(functools.partial(_conv_post, mode=mode), pre)
        (dpre,) = vjp(sum(r[...] for r in dy_refs))
        rows = _iota(dpre.shape, 0)
        dx = w_ref[B_CONV - 1:B_CONV, :] * dpre
        for j in range(B_CONV):
            s = B_CONV - 1 - j
            dw_ref[j:j + 1, :] = jnp.sum(dpre * taps[j], axis=0, keepdims=True)
            if s > 0:
                dx = dx + w_ref[j:j + 1, :] * jnp.where(rows < T - s, pltpu.roll(dpre, T - s, 0), 0.0)
        dx_ref[...] = dx.astype(BF16)

    return pl.pallas_call(
        body, name=f"conv_bwd_{mode}", grid=(B_HEADS,), in_specs=[xcol, wcol] + [hcol] * len(dys), out_specs=[hcol, wout],
        out_shape=[jax.ShapeDtypeStruct((T, B_WIDTH), BF16), jax.ShapeDtypeStruct((B_CONV, B_WIDTH), F32)],
        compiler_params=_cparams(("arbitrary",)),
    )(p2, conv_w, *dys)


def _gates_fn(bg, al, dtb, h):
    r = _iota((HD, HD), 0)
    logit = _mm(bg, (r == h).astype(F32), hi=True)
    a = _mm(bg, (r == h + B_HEADS).astype(F32), hi=True)
    beta = _sigmoid(logit)
    graw = -jnp.exp(al) * _softplus(a + dtb)
    tri = (_iota((_NCH, B_CHUNK, B_CHUNK), 1) >= _iota((_NCH, B_CHUNK, B_CHUNK), 2)).astype(F32)
    g = _mm(tri, graw.reshape(_NCH, B_CHUNK, HD), hi=True).reshape(T, HD)
    return beta, g


def _gates_specs():
    bg = pl.BlockSpec((T, HD), lambda h: (0, P_BB // HD))
    par = pl.BlockSpec((None, 1, HD), lambda h: (h, 0, 0))
    out = pl.BlockSpec((None, T, HD), lambda h: (h, 0, 0))
    return bg, par, out


def gates_fwd(p2, al, dtb):
    bg, par, out = _gates_specs()

    def body(bg_ref, al_ref, dtb_ref, beta_ref, g_ref):
        beta, g = _gates_fn(bg_ref[...], al_ref[...], dtb_ref[...], pl.program_id(0))
        beta_ref[...] = beta
        g_ref[...] = g

    return pl.pallas_call(
        body, name="gates_fwd", grid=(B_HEADS,), in_specs=[bg, par, par], out_specs=[out, out],
        out_shape=[jax.ShapeDtypeStruct((B_HEADS, T, HD), F32)] * 2, compiler_params=_cparams(("arbitrary",)),
    )(p2, al, dtb)


def gates_bwd(p2, al, dtb, dbeta, dg1, dg2):
    bg, par, out = _gates_specs()
    acc = pl.BlockSpec((T, HD), lambda h: (0, 0))

    def body(bg_ref, al_ref, dtb_ref, dbeta_ref, dg1_ref, dg2_ref, dbg_ref, dal_ref, ddtb_ref, acc_ref):
        h = pl.program_id(0)
        _, vjp = jax.vjp(lambda a, b, c: _gates_fn(a, b, c, h), bg_ref[...], al_ref[...], dtb_ref[...])
        dbg, dal, ddtb = vjp((dbeta_ref[...], dg1_ref[...] + dg2_ref[...]))

        @pl.when(h == 0)
        def _():
            acc_ref[...] = jnp.zeros_like(acc_ref)

        acc_ref[...] += dbg
        dbg_ref[...] = acc_ref[...].astype(BF16)
        dal_ref[...] = jnp.broadcast_to(jnp.sum(dal, axis=-1, keepdims=True), (1, HD))
        ddtb_ref[...] = jnp.broadcast_to(jnp.sum(ddtb, axis=-1, keepdims=True), (1, HD))

    return pl.pallas_call(
        body, name="gates_bwd", grid=(B_HEADS,), in_specs=[bg, par, par, out, out, out], out_specs=[acc, par, par],
        out_shape=[jax.ShapeDtypeStruct((T, HD), BF16)] + [jax.ShapeDtypeStruct((B_HEADS, 1, HD), F32)] * 2,
        scratch_shapes=[pltpu.VMEM((T, HD), F32)], compiler_params=_cparams(("arbitrary",)),
    )(p2, al, dtb, dbeta, dg1, dg2)


def _unit_lower_inverse(a):
    eye = (_iota(a.shape, 1) == _iota(a.shape, 2)).astype(F32)
    x = eye - a
    p = _mm(a, a, hi=True)
    for i in range(5):
        x = x + _mm(x, p, hi=True)
        if i < 4:
            p = _mm(p, p, hi=True)
    return x


_WY_CH = 16
_WY_ROWS = _WY_CH * B_CHUNK


def _wy_fn(q, k, v, beta, g):
    sh = (q.shape[0] // B_CHUNK, B_CHUNK, HD)
    q3, k3, v3, b3, g3 = (t.reshape(sh) for t in (q, k, v, beta, g))
    gd = g3[:, :, :B_CHUNK] - jnp.swapaxes(g3, 1, 2)[:, :B_CHUNK, :]
    ii, jj = _iota(gd.shape, 1), _iota(gd.shape, 2)
    decay = jnp.exp(jnp.where(ii >= jj, gd, -jnp.inf))
    kb = k3 * b3
    a = _mm(kb, k3, tb=True) * jnp.where(ii > jj, decay, 0.0)
    tinv = _unit_lower_inverse(a)
    u = _mm(tinv, v3 * b3, hi=True)
    w = _mm(tinv, kb * jnp.exp(g3), hi=True)
    attn = _mm(q3, k3, tb=True) * decay
    return u.reshape(q.shape), w.reshape(q.shape), attn


def _wy_specs():
    hcol = pl.BlockSpec((_WY_ROWS, HD), lambda h, i: (i, h))
    hb = pl.BlockSpec((None, _WY_ROWS, HD), lambda h, i: (h, i, 0))
    at = pl.BlockSpec((None, _WY_CH, B_CHUNK, B_CHUNK), lambda h, i: (h, i, 0, 0))
    return hcol, hb, at


_WY_GRID = (B_HEADS, _NCH // _WY_CH)


def wy_fwd(q, k, v, beta, g):
    hcol, hb, at = _wy_specs()

    def body(q_ref, k_ref, v_ref, b_ref, g_ref, u_ref, w_ref, a_ref):
        u, w, a = _wy_fn(q_ref[...], k_ref[...], v_ref[...], b_ref[...], g_ref[...])
        u_ref[...] = u
        w_ref[...] = w
        a_ref[...] = a

    return pl.pallas_call(
        body, name="wy_fwd", grid=_WY_GRID, in_specs=[hcol, hcol, hcol, hb, hb], out_specs=[hcol, hcol, at],
        out_shape=[jax.ShapeDtypeStruct((T, B_WIDTH), F32)] * 2 + [jax.ShapeDtypeStruct((B_HEADS, _NCH, B_CHUNK, B_CHUNK), F32)],
        compiler_params=_cparams(("arbitrary", "arbitrary")),
    )(q, k, v, beta, g)


def wy_bwd(q, k, v, beta, g, du, dw, dattn):
    hcol, hb, at = _wy_specs()

    def body(q_ref, k_ref, v_ref, b_ref, g_ref, du_ref, dw_ref, da_ref, dq_ref, dk_ref, dv_ref, db_ref, dg_ref):
        _, vjp = jax.vjp(_wy_fn, q_ref[...], k_ref[...], v_ref[...], b_ref[...], g_ref[...])
        for r, t in zip((dq_ref, dk_ref, dv_ref, db_ref, dg_ref), vjp((du_ref[...], dw_ref[...], da_ref[...]))):
            r[...] = t

    return pl.pallas_call(
        body, name="wy_bwd", grid=_WY_GRID, in_specs=[hcol, hcol, hcol, hb, hb, hcol, hcol, at],
        out_specs=[hcol, hcol, hcol, hb, hb],
        out_shape=[jax.ShapeDtypeStruct((T, B_WIDTH), F32)] * 3 + [jax.ShapeDtypeStruct((B_HEADS, T, HD), F32)] * 2,
        compiler_params=_cparams(("arbitrary", "arbitrary")),
    )(q, k, v, beta, g, du, dw, dattn)


def _scan_step_fn(q, k, u, w, g, attn, gate, og, s):
    v_new = u - _mm(w, s)
    o = _mm(q * jnp.exp(g), s) + _mm(attn, v_new)
    g_last = jnp.sum(jnp.where(_iota(g.shape, 0) == B_CHUNK - 1, g, 0.0), axis=0, keepdims=True)
    s_new = s * jnp.exp(g_last) + _mm(k * jnp.exp(g_last - g), v_new, ta=True)
    return _rms(o, og) * _silu(gate), s_new


def _scan_specs(rev):
    ch = (lambda n: _NCH - 1 - n) if rev else (lambda n: n)
    rows = pl.BlockSpec((B_CHUNK, B_WIDTH), lambda n: (ch(n), 0))
    gb = pl.BlockSpec((B_HEADS, B_CHUNK, HD), lambda n: (0, ch(n), 0))
    at = pl.BlockSpec((B_HEADS, None, B_CHUNK, B_CHUNK), lambda n: (0, ch(n), 0, 0))
    og = pl.BlockSpec((1, HD), lambda n: (0, 0))
    st = pl.BlockSpec((None, B_HEADS, HD, HD), lambda n: (ch(n), 0, 0, 0))
    return rows, gb, at, og, st


def scan_fwd(q, k, u, w, g, attn, gate, og):
    rows, gb, at, ogs, st = _scan_specs(False)

    def body(q_ref, k_ref, u_ref, w_ref, g_ref, a_ref, gate_ref, og_ref, y_ref, st_ref, s_ref):
        @pl.when(pl.program_id(0) == 0)
        def _():
            s_ref[...] = jnp.zeros_like(s_ref)

        for h in range(B_HEADS):
            c = slice(h * HD, (h + 1) * HD)
            s = s_ref[h]
            st_ref[h] = s
            y, s_new = _scan_step_fn(q_ref[:, c], k_ref[:, c], u_ref[:, c], w_ref[:, c], g_ref[h], a_ref[h],
                                     gate_ref[:, c], og_ref[...], s)
            y_ref[:, c] = y.astype(BF16)
            s_ref[h] = s_new

    return pl.pallas_call(
        body, name="scan_fwd", grid=(_NCH,), in_specs=[rows, rows, rows, rows, gb, at, rows, ogs], out_specs=[rows, st],
        out_shape=[jax.ShapeDtypeStruct((T, B_WIDTH), BF16), jax.ShapeDtypeStruct((_NCH, B_HEADS, HD, HD), F32)],
        scratch_shapes=[pltpu.VMEM((B_HEADS, HD, HD), F32)], compiler_params=_cparams(("arbitrary",)),
    )(q, k, u, w, g, attn, gate, og)


def scan_bwd(q, k, u, w, g, attn, gate, og, states, dmix):
    rows, gb, at, ogs, st = _scan_specs(True)
    dyb = pl.BlockSpec((B_CHUNK, HD), lambda n: (_NCH - 1 - n, 0))

    def body(q_ref, k_ref, u_ref, w_ref, g_ref, a_ref, gate_ref, og_ref, st_ref, *rest):
        dy_refs, (dq_ref, dk_ref, du_ref, dw_ref, dgate_ref, dg_ref, da_ref, dog_ref, ds_ref) = rest[:B_HEADS], rest[B_HEADS:]

        @pl.when(pl.program_id(0) == 0)
        def _():
            ds_ref[...] = jnp.zeros_like(ds_ref)
            dog_ref[...] = jnp.zeros_like(dog_ref)

        for h in range(B_HEADS):
            c = slice(h * HD, (h + 1) * HD)
            _, vjp = jax.vjp(_scan_step_fn, q_ref[:, c], k_ref[:, c], u_ref[:, c], w_ref[:, c], g_ref[h], a_ref[h],
                             gate_ref[:, c], og_ref[...], st_ref[h])
            dq, dk, du, dw, dg, da, dgate, dog, ds = vjp((dy_refs[h][...], ds_ref[h]))
            dq_ref[:, c] = dq
            dk_ref[:, c] = dk
            du_ref[:, c] = du
            dw_ref[:, c] = dw
            dgate_ref[:, c] = dgate.astype(BF16)
            dg_ref[h] = dg
            da_ref[h] = da
            dog_ref[...] += dog
            ds_ref[h] = ds

    dy_specs = [pl.BlockSpec((B_CHUNK, HD), lambda n, h=h: (_NCH - 1 - n, A_WIDTH // HD + h)) for h in range(B_HEADS)]
    return pl.pallas_call(
        body, name="scan_bwd", grid=(_NCH,),
        in_specs=[rows, rows, rows, rows, gb, at, rows, ogs, st] + dy_specs,
        out_specs=[rows] * 5 + [gb, at, ogs],
        out_shape=[jax.ShapeDtypeStruct((T, B_WIDTH), F32)] * 4 + [jax.ShapeDtypeStruct((T, B_WIDTH), BF16)]
        + [jax.ShapeDtypeStruct((B_HEADS, T, HD), F32), jax.ShapeDtypeStruct((B_HEADS, _NCH, B_CHUNK, B_CHUNK), F32),
           jax.ShapeDtypeStruct((1, HD), F32)],
        scratch_shapes=[pltpu.VMEM((B_HEADS, HD, HD), F32)], compiler_params=_cparams(("arbitrary",)),
    )(q, k, u, w, g, attn, gate, og, states, *([dmix] * B_HEADS))


def _lanes(vec):
    return jnp.broadcast_to(vec[:, None, None], (vec.shape[0], 1, HD))


def gdn_forward(p2, conv_w, a_log, dt_bias, og):
    qa, ka, va = (conv_fwd(p2, conv_w, m) for m in "qkv")
    beta, g = gates_fwd(p2, _lanes(a_log), _lanes(dt_bias))
    u, w, attn = wy_fwd(qa, ka, va, beta, g)
    gate = p2[:, P_BG:P_BB]
    y, states = scan_fwd(qa, ka, u, w, g, attn, gate, og)
    return y, (qa, ka, va, beta, g, u, w, attn, gate, states)


def gdn_backward(p2, conv_w, a_log, dt_bias, og, saved, dmix):
    qa, ka, va, beta, g, u, w, attn, gate, states = saved
    dq1, dk1, du, dw, dgate, dg1, dattn, dog = scan_bwd(qa, ka, u, w, g, attn, gate, og, states, dmix)
    dq2, dk2, dv, dbeta, dg2 = wy_bwd(qa, ka, va, beta, g, du, dw, dattn)
    dbg, dal, ddtb = gates_bwd(p2, _lanes(a_log), _lanes(dt_bias), dbeta, dg1, dg2)
    dxq, dwq = conv_bwd(p2, conv_w, [dq1, dq2], "q")
    dxk, dwk = conv_bwd(p2, conv_w, [dk1, dk2], "k")
    dxv, dwv = conv_bwd(p2, conv_w, [dv], "v")
    return [dxq, dxk, dxv, dgate, dbg], jnp.concatenate([dwq, dwk, dwv], axis=1), dal[:, 0, 0], ddtb[:, 0, 0], dog


_SLOPES = np.exp2(-8.0 * (np.arange(C_HEADS, dtype=np.float64) + 1.0) / C_HEADS).astype(np.float32)


def _alibi_slopes():
    return _lanes(jnp.asarray(_SLOPES))


_ROWS = 512
_TM = 1024
_TM_FFN = 512


def _dep_specs(dep, ngrid):
    if dep is None:
        return [], []
    return [dep], [pl.BlockSpec((8, HD), lambda *_: (0, 0))]


def rmsnorm_fwd(x, g, dep=None):
    blk = pl.BlockSpec((_ROWS, D), lambda i: (i, 0))
    deps, dspecs = _dep_specs(dep, 1)

    def body(x_ref, g_ref, *rest):
        rest[-1][...] = _rms(x_ref[...], g_ref[...]).astype(BF16)

    return pl.pallas_call(
        body, name="rmsnorm_fwd", grid=(T // _ROWS,), in_specs=[blk, pl.BlockSpec((1, D), lambda i: (0, 0))] + dspecs,
        out_specs=blk, out_shape=jax.ShapeDtypeStruct((T, D), BF16), compiler_params=_cparams(("arbitrary",)),
    )(x, g, *deps)


def rmsnorm_bwd(x, g, dh, dres):
    blk = pl.BlockSpec((_ROWS, D), lambda i: (i, 0))
    row = pl.BlockSpec((1, D), lambda i: (0, 0))

    def body(x_ref, g_ref, dh_ref, dres_ref, dx_ref, dg_ref):
        _, vjp = jax.vjp(_rms, x_ref[...], g_ref[...])
        dx, dg = vjp(dh_ref[...])
        dx_ref[...] = dres_ref[...] + dx

        @pl.when(pl.program_id(0) == 0)
        def _():
            dg_ref[...] = jnp.zeros_like(dg_ref)

        dg_ref[...] += dg

    return pl.pallas_call(
        body, name="rmsnorm_bwd", grid=(T // _ROWS,), in_specs=[blk, row, blk, blk], out_specs=[blk, row],
        out_shape=[jax.ShapeDtypeStruct((T, D), F32), jax.ShapeDtypeStruct((1, D), F32)],
        compiler_params=_cparams(("arbitrary",)),
    )(x, g, dh, dres)


def _matmul(name, a, b, *, grid, a_spec, b_spec, o_spec, out_shape, ta=False, tb=False, k_axis=None, res=None, dep=None):
    dims = _dims(2, ta, tb)
    deps, dspecs = _dep_specs(dep, len(grid))

    def body(a_ref, b_ref, *rest):
        o_ref = rest[-1]
        prod = lax.dot_general(a_ref[...].astype(BF16), b_ref[...].astype(BF16), dims, preferred_element_type=F32)
        if res is not None:
            prod = prod + rest[0][...]
        if k_axis is None:
            o_ref[...] = prod.astype(o_ref.dtype)
        else:
            @pl.when(pl.program_id(k_axis) == 0)
            def _():
                o_ref[...] = prod

            @pl.when(pl.program_id(k_axis) > 0)
            def _():
                o_ref[...] += prod

    sem = tuple("arbitrary" for _ in grid)
    ins = [a, b] + ([res] if res is not None else []) + deps
    specs = [a_spec, b_spec] + ([o_spec] if res is not None else []) + dspecs
    return pl.pallas_call(
        body, name=name, grid=grid, in_specs=specs, out_specs=o_spec, out_shape=out_shape, compiler_params=_cparams(sem),
    )(*ins)


_IN_TN = P_END // 3


def mm_proj(h1, wp_in, l):
    return _matmul(
        "mm_proj", h1, wp_in, grid=(P_END // _IN_TN, T // _TM),
        a_spec=pl.BlockSpec((_TM, D), lambda j, i: (i, 0)),
        b_spec=pl.BlockSpec((None, D, _IN_TN), lambda j, i: (l, 0, j)),
        o_spec=pl.BlockSpec((_TM, _IN_TN), lambda j, i: (i, j)), out_shape=jax.ShapeDtypeStruct((T, P_END), F32))


def mm_dh1(dp2, wp_in, l, dep=None):
    return _matmul(
        "mm_dh1", dp2, wp_in, grid=(T // _TM, P_END // _IN_TN), tb=True, k_axis=1, dep=dep,
        a_spec=pl.BlockSpec((_TM, _IN_TN), lambda i, k: (i, k)),
        b_spec=pl.BlockSpec((None, D, _IN_TN), lambda i, k: (l, 0, k)),
        o_spec=pl.BlockSpec((_TM, D), lambda i, k: (i, 0)), out_shape=jax.ShapeDtypeStruct((T, D), F32))


def mm_dwin(h1, dp2):
    return _matmul(
        "mm_dwin", h1, dp2, grid=(P_END // _IN_TN, D // _TM), ta=True,
        a_spec=pl.BlockSpec((T, _TM), lambda j, i: (0, i)),
        b_spec=pl.BlockSpec((T, _IN_TN), lambda j, i: (0, j)),
        o_spec=pl.BlockSpec((_TM, _IN_TN), lambda j, i: (i, j)), out_shape=jax.ShapeDtypeStruct((D, P_END), BF16))


def _mm_square(name, a, w, l, res, tb, dep=None):
    tn = 1024
    b_spec = (pl.BlockSpec((None, tn, D), lambda j, i: (l, j, 0)) if tb else pl.BlockSpec((None, D, tn), lambda j, i: (l, 0, j)))
    return _matmul(
        name, a, w, grid=(D // tn, T // _TM), tb=tb, res=res, dep=dep,
        a_spec=pl.BlockSpec((_TM, D), lambda j, i: (i, 0)), b_spec=b_spec,
        o_spec=pl.BlockSpec((_TM, tn), lambda j, i: (i, j)), out_shape=jax.ShapeDtypeStruct((T, D), F32))


def mm_out(mix, wg_out, l, x):
    return _mm_square("mm_out", mix, wg_out, l, x, False)


def mm_dmix(dx1, wg_out, l, dep=None):
    return _mm_square("mm_dmix", dx1, wg_out, l, None, True, dep)


def mm_dwout(mix, dx1):
    tn = 1024
    return _matmul(
        "mm_dwout", mix, dx1, grid=(D // tn, D // _TM), ta=True,
        a_spec=pl.BlockSpec((T, _TM), lambda j, i: (0, i)), b_spec=pl.BlockSpec((T, tn), lambda j, i: (0, j)),
        o_spec=pl.BlockSpec((_TM, tn), lambda j, i: (i, j)), out_shape=jax.ShapeDtypeStruct((D, D), BF16))


_GU_TN = GU_SHARD // 2


_GU_NJ = FFN // _GU_TN


def mm_dh2(dgu, wg_gu, l, dep=None):
    return _matmul(
        "mm_dh2", dgu, wg_gu, grid=(T // _TM, 2 * N_CHIPS), tb=True, k_axis=1, dep=dep,
        a_spec=pl.BlockSpec((None, _TM, _GU_TN), lambda i, k: (k // _GU_NJ, i, k % _GU_NJ)),
        b_spec=pl.BlockSpec((None, None, D, _GU_TN), lambda i, k: (l, k // 2, 0, k % 2)),
        o_spec=pl.BlockSpec((_TM, D), lambda i, k: (i, 0)), out_shape=jax.ShapeDtypeStruct((T, D), F32))


def mm_dwgu(h2, dgu):
    return _matmul(
        "mm_dwgu", h2, dgu, grid=(N_CHIPS, 2, D // _TM), ta=True,
        a_spec=pl.BlockSpec((T, _TM), lambda s, j, i: (0, i)),
        b_spec=pl.BlockSpec((None, T, _GU_TN), lambda s, j, i: ((2 * s + j) // _GU_NJ, 0, (2 * s + j) % _GU_NJ)),
        o_spec=pl.BlockSpec((None, _TM, _GU_TN), lambda s, j, i: (s, i, j)),
        out_shape=jax.ShapeDtypeStruct((N_CHIPS, D, GU_SHARD), BF16))


def mm_down(act, wg_down, l, x1):
    tn = 512
    return _matmul(
        "mm_down", act, wg_down, grid=(D // tn, T // _TM), res=x1,
        a_spec=pl.BlockSpec((_TM, FFN), lambda j, i: (i, 0)),
        b_spec=pl.BlockSpec((None, FFN, tn), lambda j, i: (l, 0, j)),
        o_spec=pl.BlockSpec((_TM, tn), lambda j, i: (i, j)), out_shape=jax.ShapeDtypeStruct((T, D), F32))


def mm_dwdown(act, dx2):
    tm, tn = DOWN_SHARD, 512
    return _matmul(
        "mm_dwdown", act, dx2, grid=(D // tn, FFN // tm), ta=True,
        a_spec=pl.BlockSpec((T, tm), lambda j, i: (0, i)), b_spec=pl.BlockSpec((T, tn), lambda j, i: (0, j)),
        o_spec=pl.BlockSpec((tm, tn), lambda j, i: (i, j)), out_shape=jax.ShapeDtypeStruct((FFN, D), BF16))


_FF_TN = 1408


def _swiglu_fn(gt, up):
    return _silu(gt) * up


def _gate_up_specs():
    gate = pl.BlockSpec((None, None, D, _FF_TN), lambda j, i: (0, j // 2, 0, j % 2))
    up = pl.BlockSpec((None, None, D, _FF_TN), lambda j, i: (0, N_CHIPS // 2 + j // 2, 0, j % 2))
    both = pl.BlockSpec((2, _TM_FFN, _FF_TN), lambda j, i: (0, i, j))
    return gate, up, both


def mm_gu_swiglu(h2, wg_gu):
    gate, up, both = _gate_up_specs()

    def body(h_ref, wg_ref, wu_ref, gu_ref, act_ref):
        h = h_ref[...]
        gt = jnp.dot(h, wg_ref[...], preferred_element_type=F32)
        u = jnp.dot(h, wu_ref[...], preferred_element_type=F32)
        gu_ref[0] = gt.astype(BF16)
        gu_ref[1] = u.astype(BF16)
        act_ref[...] = _swiglu_fn(gt, u).astype(BF16)

    return pl.pallas_call(
        body, name="mm_gu_swiglu", grid=(FFN // _FF_TN, T // _TM_FFN),
        in_specs=[pl.BlockSpec((_TM_FFN, D), lambda j, i: (i, 0)), gate, up],
        out_specs=[both, pl.BlockSpec((_TM_FFN, _FF_TN), lambda j, i: (i, j))],
        out_shape=[jax.ShapeDtypeStruct((2, T, FFN), BF16), jax.ShapeDtypeStruct((T, FFN), BF16)],
        compiler_params=_cparams(("arbitrary", "arbitrary")),
    )(h2, wg_gu, wg_gu)


def mm_dact_swiglu(dx2, wg_down, gu, dep=None):
    _, _, both = _gate_up_specs()
    deps, dspecs = _dep_specs(dep, 2)

    def body(dx_ref, w_ref, gu_ref, *rest):
        dact = lax.dot_general(dx_ref[...].astype(BF16), w_ref[...], _dims(2, False, True), preferred_element_type=F32)
        _, vjp = jax.vjp(_swiglu_fn, gu_ref[0].astype(F32), gu_ref[1].astype(F32))
        dgt, dup = vjp(dact)
        rest[-1][0] = dgt.astype(BF16)
        rest[-1][1] = dup.astype(BF16)

    return pl.pallas_call(
        body, name="mm_dact_swiglu", grid=(FFN // _FF_TN, T // _TM_FFN),
        in_specs=[pl.BlockSpec((_TM_FFN, D), lambda j, i: (i, 0)), pl.BlockSpec((None, _FF_TN, D), lambda j, i: (0, j, 0)),
                  both]
        + dspecs,
        out_specs=both, out_shape=jax.ShapeDtypeStruct((2, T, FFN), BF16),
        compiler_params=_cparams(("arbitrary", "arbitrary")),
    )(dx2, wg_down, gu, *deps)


def loss_and_grad(y, target):
    blk = pl.BlockSpec((_ROWS, D), lambda i: (i, 0))
    acc = pl.BlockSpec((8, HD), lambda i: (0, 0))

    def body(y_ref, t_ref, dy_ref, l_ref):
        err = y_ref[...] - t_ref[...]
        dy_ref[...] = err * (1.0 / D)

        @pl.when(pl.program_id(0) == 0)
        def _():
            l_ref[...] = jnp.zeros_like(l_ref)

        l_ref[...] += (0.5 / D) * jnp.sum(err * err)

    return pl.pallas_call(
        body, name="loss_and_grad", grid=(T // _ROWS,), in_specs=[blk, blk], out_specs=[blk, acc],
        out_shape=[jax.ShapeDtypeStruct((T, D), F32), jax.ShapeDtypeStruct((8, HD), F32)],
        compiler_params=_cparams(("arbitrary",)),
    )(y, target)


def adamw(w, g, m, v, name):
    rows, cols = w.shape
    tr = _ROWS if rows % _ROWS == 0 else rows
    blk = pl.BlockSpec((tr, cols), lambda i: (i, 0))

    def body(w_ref, g_ref, m_ref, v_ref, d_ref, nm_ref, nv_ref):
        gg = g_ref[...]
        nm = ADAM_B1 * m_ref[...] + (1.0 - ADAM_B1) * gg
        nv = ADAM_B2 * v_ref[...] + (1.0 - ADAM_B2) * (gg * gg)
        m_hat = nm / (1.0 - ADAM_B1 ** ADAM_STEP)
        v_hat = nv / (1.0 - ADAM_B2 ** ADAM_STEP)
        d_ref[...] = -ADAM_LR * (m_hat / (jnp.sqrt(v_hat) + ADAM_EPS) + ADAM_WD * w_ref[...])
        nm_ref[...] = nm
        nv_ref[...] = nv

    return pl.pallas_call(
        body, name=name, grid=(rows // tr,), in_specs=[blk] * 4, out_specs=[blk] * 3,
        out_shape=[jax.ShapeDtypeStruct(w.shape, F32)] * 3, compiler_params=_cparams(("arbitrary",)),
    )(w, g, m, v)


_LANE = 128


def _segment_of_shard_column():
    flat = np.full(P_END, -1, np.int64)
    for o in range(P_END):
        if GATE_COLS <= o < P_CQ:
            continue
        c = o if o < GATE_COLS else o - (P_CQ - GATE_COLS)
        flat[o] = (c // IN_SHARD) * IN_SHARD_PAD + c % IN_SHARD
    return flat


def _block_pairs(src_of_dst):
    return [sorted({int(c) // _LANE for c in src_of_dst[db * _LANE:(db + 1) * _LANE] if c >= 0})
            for db in range(len(src_of_dst) // _LANE)]


_RELAYOUT_ROWS = 512
_SHARD_BLOCKS = IN_SHARD_PAD // _LANE


def _relayout(name, x, to_segments):
    seg_of = _segment_of_shard_column()
    if to_segments:
        src_of_dst = seg_of
    else:
        src_of_dst = np.full(N_CHIPS * IN_SHARD_PAD, -1, np.int64)
        src_of_dst[seg_of[seg_of >= 0]] = np.nonzero(seg_of >= 0)[0]
    sources = _block_pairs(src_of_dst)
    n_dst = len(sources)
    col_map = jnp.asarray(src_of_dst.reshape(n_dst, 1, _LANE), jnp.int32)
    shard_blk = pl.BlockSpec((N_CHIPS, _RELAYOUT_ROWS, IN_SHARD_PAD), lambda i: (0, i, 0))
    seg_blk = pl.BlockSpec((_RELAYOUT_ROWS, P_END), lambda i: (i, 0))

    def shard_cols(ref, b):
        return ref.at[b // _SHARD_BLOCKS, :, pl.ds((b % _SHARD_BLOCKS) * _LANE, _LANE)]

    def seg_cols(ref, b):
        return ref.at[:, pl.ds(b * _LANE, _LANE)]

    src_cols, dst_cols = (shard_cols, seg_cols) if to_segments else (seg_cols, shard_cols)

    def body(x_ref, map_ref, o_ref):
        src_row = _iota((_LANE, _LANE), 0)
        for d in range(n_dst):
            acc = jnp.zeros((_RELAYOUT_ROWS, _LANE), F32)
            for sb in sources[d]:
                sel = (src_row + sb * _LANE == map_ref[d]).astype(x_ref.dtype)
                acc = acc + jnp.dot(src_cols(x_ref, sb)[...], sel, preferred_element_type=F32)
            dst_cols(o_ref, d)[...] = acc.astype(o_ref.dtype)

    rows = x.shape[-2]
    out_shape = (rows, P_END) if to_segments else (N_CHIPS, rows, IN_SHARD_PAD)
    return pl.pallas_call(
        body, name=name, grid=(rows // _RELAYOUT_ROWS,),
        in_specs=[shard_blk if to_segments else seg_blk, pl.BlockSpec(col_map.shape, lambda i: (0, 0, 0))],
        out_specs=seg_blk if to_segments else shard_blk, out_shape=jax.ShapeDtypeStruct(out_shape, x.dtype),
        compiler_params=_cparams(("arbitrary",)),
    )(x, col_map)


def shards_to_segments(w):
    return _relayout("shards_to_segments", w, True)


def segments_to_shards(w):
    return _relayout("segments_to_shards", w, False)


def mixers_forward(x, w_in, sp, dep=None, h1=None):
    if h1 is None:
        h1 = rmsnorm_fwd(x, sp["norm1_g"], dep)
    p2 = mm_proj(h1, w_in, 0)
    y_a = sgu_fwd(p2, sp["sgu_norm_g"], sp["w_spatial"], sp["b_spatial"])
    y_b, saved_b = gdn_forward(p2, sp["conv_w"], sp["a_log"], sp["dt_bias"], sp["o_norm_g"])
    y_c, outs_c, lses_c = dattn_fwd(p2, sp["q_norm_g"], sp["k_norm_g"], _alibi_slopes())
    mix = jnp.concatenate([y_a, y_b, y_c], axis=1)
    return mix, (x, h1, p2, saved_b, (outs_c, lses_c), mix)


def ffn_up(x, mix, w_out, w_gu, sp):
    x1 = mm_out(mix, w_out, 0, x)
    h2 = rmsnorm_fwd(x1, sp["norm2_g"])
    gu, act = mm_gu_swiglu(h2, w_gu)
    return x1, h2, gu, act


def ffn_forward(x, mix, wg, sp):
    x1, h2, gu, act = ffn_up(x, mix, wg["out"], wg["gu"], sp)
    x2 = mm_down(act, wg["down"], 0, x1)
    return x2, (x1, h2, gu, act)


def ffn_backward(dx2, wg, sp, saved, dep=None, on_weight_grads=None):
    x1, h2, gu, act = saved
    dgu = mm_dact_swiglu(dx2, wg["down"], gu, dep)
    dw_down = mm_dwdown(act, dx2)
    dw_gu = mm_dwgu(h2, dgu)
    tok = None if on_weight_grads is None else on_weight_grads(dw_gu, dw_down)
    dh2 = mm_dh2(dgu, wg["gu"], 0, tok)
    dx1, dnorm2 = rmsnorm_bwd(x1, sp["norm2_g"], dh2, dx2)
    return dx1, dnorm2, dw_gu, dw_down


def mixers_backward(dx1, wg, sp, saved, dep=None, on_weight_grads=None):
    x, h1, p2, saved_b, saved_c, mix = saved
    dmix = mm_dmix(dx1, wg["out"], 0, dep)
    dw_out = mm_dwout(mix, dx1)
    du, dv, dsg, dws, dbs = sgu_bwd(p2, sp["sgu_norm_g"], sp["w_spatial"], sp["b_spatial"], dmix)
    dseg_b, dconv, dal, ddtb, dog = gdn_backward(p2, sp["conv_w"], sp["a_log"], sp["dt_bias"], sp["o_norm_g"], saved_b, dmix)
    dcq, dck, dcv, dqg, dkg = dattn_bwd(p2, sp["q_norm_g"], sp["k_norm_g"], _alibi_slopes(), *saved_c, dmix)
    dp2 = jnp.concatenate([du, dv] + dseg_b + [dcq, dck, dcv], axis=1)
    dw_in = segments_to_shards(mm_dwin(h1, dp2))
    tok = None if on_weight_grads is None else on_weight_grads(dw_in, dw_out)
    dh1 = mm_dh1(dp2, wg["in"], 0, tok)
    dx, dnorm1 = rmsnorm_bwd(x, sp["norm1_g"], dh1, dx1)
    small = {"norm1_g": dnorm1, "sgu_norm_g": dsg, "w_spatial": dws, "b_spatial": dbs, "conv_w": dconv, "a_log": dal,
             "dt_bias": ddtb, "o_norm_g": dog, "q_norm_g": dqg, "k_norm_g": dkg}
    return dx, dw_in, dw_out, small


_HBM = pl.BlockSpec(memory_space=pltpu.HBM)
_MESH = pl.DeviceIdType.MESH


def _place():
    x, y, c = lax.axis_index("x"), lax.axis_index("y"), lax.axis_index("c")
    chips = [(1 - x, y), (x, 1 - y), (1 - x, 1 - y)]
    return x, y, c, chips


def _rcopy(src, dst, ssem, rsem, dev):
    return pltpu.make_async_remote_copy(src_ref=src, dst_ref=dst, send_sem=ssem, recv_sem=rsem, device_id=dev,
                                        device_id_type=_MESH)


_SEM = pl.BlockSpec(memory_space=pltpu.SEMAPHORE)
_SIDE_EFFECT = pltpu.SideEffectType.DATAFLOW_SIDE_EFFECTING


def _in_hbm(a):
    return pltpu.with_memory_space_constraint(a, pltpu.HBM)


def _split_copy(name, srcs, land_shapes, n_sems, copies):
    n, m = len(srcs), len(land_shapes)
    thru = [pltpu.HBM(a.shape, a.dtype) for a in srcs] + [pltpu.HBM(s.shape, s.dtype) for s in land_shapes]
    sems = (pltpu.SemaphoreType.DMA((n_sems,)), pltpu.SemaphoreType.DMA((n_sems,)))

    def start(dep=None):
        deps = [] if dep is None else [dep]

        def body(*refs):
            ins, lands = refs[:n], refs[n:n + m]
            ssem, rsem, token = refs[n + m + len(deps)], refs[n + m + len(deps) + 1], refs[-1]
            for cp in copies(ins, lands, ssem, rsem)[0]:
                cp.start()
            token[...] = jnp.zeros_like(token)

        out = pl.pallas_call(
            body, name=name + "_start", out_shape=(*sems, *thru, jax.ShapeDtypeStruct((8, HD), F32)),
            in_specs=[_HBM] * (n + m) + [pl.BlockSpec(memory_space=pl.ANY)] * len(deps),
            out_specs=(_SEM, _SEM, *[_HBM] * (n + m), pl.BlockSpec(memory_space=pltpu.VMEM)),
            input_output_aliases={i: 2 + i for i in range(n + m)},
            compiler_params=pltpu.CompilerParams(has_side_effects=_SIDE_EFFECT),
        )(*[_in_hbm(a) for a in srcs], *[_in_hbm(lax.empty(s.shape, s.dtype)) for s in land_shapes], *deps)
        return out[:-1], out[-1]

    def wait(state, after):
        def body(*refs):
            ins, lands, ssem, rsem = refs[:n], refs[n:n + m], refs[n + m], refs[n + m + 1]
            sent, arrivals = copies(ins, lands, ssem, rsem)
            for cp in sent:
                cp.wait_send()
            for cp in arrivals:
                cp.wait_recv()

        out = pl.pallas_call(
            body, name=name + "_wait", out_shape=tuple(thru),
            in_specs=[_HBM] * (n + m) + [_SEM, _SEM, pl.BlockSpec(memory_space=pl.ANY)], out_specs=[_HBM] * (n + m),
            input_output_aliases={i: i for i in range(n + m)},
            compiler_params=pltpu.CompilerParams(has_side_effects=_SIDE_EFFECT),
        )(*state[2:], state[0], state[1], after)
        return list(out[:n]), list(out[n:])

    return start, wait


def gather_direct(shards, tag):
    n = len(shards)

    def copies(ins, lands, ssem, rsem):
        x, y, c, chips = _place()
        s = 2 * x + y
        sibling = (x, y, 1 - c)
        sent, arrivals = [], []
        for a in range(n):
            for u in range(2):
                cp = _rcopy(ins[a].at[u], lands[a].at[s, u], ssem.at[5 * a + u], rsem.at[5 * a + u], sibling)
                sent.append(cp)
                arrivals.append(cp)
            for j, (cx, cy) in enumerate(chips):
                k = 5 * a + 2 + j
                sent.append(_rcopy(ins[a].at[c], lands[a].at[s, c], ssem.at[k], rsem.at[k], (cx, cy, c)))
                arrivals.append(_rcopy(ins[a].at[c], lands[a].at[2 * cx + cy, c], ssem.at[k], rsem.at[k], (cx, cy, c)))
        return sent, arrivals

    lands = [jax.ShapeDtypeStruct((N_CHIPS,) + w.shape, w.dtype) for w in shards]
    return _split_copy("gather_direct_" + tag, shards, lands, 5 * n, copies)


def pass_to_sibling(lands):
    n = len(lands)

    def body(*refs):
        ins = refs[:n]
        ssem, rsem = refs[2 * n:]
        x, y, c, chips = _place()
        sibling = (x, y, 1 - c)
        cps, arrivals = [], []
        for a in range(n):
            for j, (cx, cy) in enumerate(chips):
                t = 2 * cx + cy
                cps.append(_rcopy(ins[a].at[t, c], ins[a].at[t, c], ssem.at[a, j], rsem.at[a, j], sibling))
                arrivals.append(_rcopy(ins[a].at[t, c], ins[a].at[t, 1 - c], ssem.at[a, j], rsem.at[a, j], sibling))
        for cp in cps:
            cp.start()
        for cp, ar in zip(cps, arrivals):
            cp.wait_send()
            ar.wait_recv()

    return pl.pallas_call(
        body, name="pass_to_sibling", in_specs=[_HBM] * n, out_specs=[_HBM] * n,
        out_shape=[jax.ShapeDtypeStruct(a.shape, a.dtype) for a in lands], input_output_aliases={a: a for a in range(n)},
        scratch_shapes=[pltpu.SemaphoreType.DMA((n, 3)), pltpu.SemaphoreType.DMA((n, 3))],
    )(*lands)


def exchange_halves(grads, tag):
    n = len(grads)

    def copies(ins, lands, ssem, rsem):
        x, y, c, _ = _place()
        cps = []
        for a in range(n):
            h = grads[a].shape[1] // 2
            cps.append(_rcopy(ins[a].at[:, pl.ds((1 - c) * h, h)], lands[a], ssem.at[a], rsem.at[a], (x, y, 1 - c)))
        return cps, cps

    lands = [jax.ShapeDtypeStruct((g.shape[0], g.shape[1] // 2, g.shape[2]), g.dtype) for g in grads]
    return _split_copy("exchange_halves_" + tag, grads, lands, n, copies)


def scatter_direct(parts, tag):
    n = len(parts)

    def copies(ins, lands, ssem, rsem):
        x, y, c, chips = _place()
        cps = [_rcopy(ins[a].at[2 * cx + cy], lands[a].at[j], ssem.at[3 * a + j], rsem.at[3 * a + j], (cx, cy, c))
               for a in range(n) for j, (cx, cy) in enumerate(chips)]
        return cps, cps

    lands = [jax.ShapeDtypeStruct((3,) + p.shape[1:], p.dtype) for p in parts]
    return _split_copy("scatter_direct_" + tag, parts, lands, 3 * n, copies)


def share_halves(halves):
    n = len(halves)

    def body(*refs):
        ins, outs = refs[:n], refs[n:2 * n]
        ssem, rsem = refs[2 * n:]
        x, y, c, _ = _place()
        cps = [_rcopy(ins[i], outs[i], ssem.at[i], rsem.at[i], (x, y, 1 - c)) for i in range(n)]
        for cp in cps:
            cp.start()
        for cp in cps:
            cp.wait()

    return pl.pallas_call(
        body, name="share_halves", in_specs=[_HBM] * n, out_specs=[_HBM] * n,
        out_shape=[jax.ShapeDtypeStruct(h.shape, h.dtype) for h in halves],
        scratch_shapes=[pltpu.SemaphoreType.DMA((n,)), pltpu.SemaphoreType.DMA((n,))],
    )(*halves)


_ADAMW_BLOCK_BYTES = 3 << 19


def adamw_shard(w, m, v, mine, theirs, c, name):
    _, r, cw = w.shape
    h, cg = mine[0].shape
    tr = next(t for t in (256, 176, 128) if h % t == 0 and t * cg * 4 <= _ADAMW_BLOCK_BYTES)
    nb = h // tr
    wblk = pl.BlockSpec((None, tr, cw), lambda l, i, c_ref: (l, i, 0))
    gblk = lambda layer, own: pl.BlockSpec((tr, cg), lambda l, i, c_ref: (_held_block(l, i, c_ref, layer, own, nb), 0))
    return _adamw_halves(w, m, v, mine, theirs, c, name, (DEPTH, r // tr), wblk, gblk, nb, cw)


def _held_block(l, i, c_ref, layer, own, nb):
    in_use = (l == layer) & (((i // nb) == c_ref[0]) == own)
    return jnp.where(in_use, i % nb, 0)


def _adamw_halves(w, m, v, mine, theirs, c, name, grid, wblk, gblk, nb, cw):
    def body(c_ref, w_ref, m_ref, v_ref, m0, m1, t0, t1, g_ref, d_ref, nm_ref, nv_ref):
        is_mine = (pl.program_id(1) // nb) == c_ref[0]
        first = pl.program_id(0) == 0
        gg = jnp.where(is_mine, jnp.where(first, m0[:, :cw], m1[:, :cw]), jnp.where(first, t0[:, :cw], t1[:, :cw]))
        nm = ADAM_B1 * m_ref[...] + (1.0 - ADAM_B1) * gg
        nv = ADAM_B2 * v_ref[...] + (1.0 - ADAM_B2) * (gg * gg)
        m_hat = nm / (1.0 - ADAM_B1 ** ADAM_STEP)
        v_hat = nv / (1.0 - ADAM_B2 ** ADAM_STEP)
        g_ref[...] = gg
        d_ref[...] = -ADAM_LR * (m_hat / (jnp.sqrt(v_hat) + ADAM_EPS) + ADAM_WD * w_ref[...])
        nm_ref[...] = nm
        nv_ref[...] = nv

    return pl.pallas_call(
        body, name=name,
        grid_spec=pltpu.PrefetchScalarGridSpec(
            num_scalar_prefetch=1, grid=grid,
            in_specs=[wblk] * 3 + [gblk(0, True), gblk(1, True), gblk(0, False), gblk(1, False)], out_specs=[wblk] * 4),
        out_shape=[jax.ShapeDtypeStruct(w.shape, F32)] * 4, compiler_params=_cparams(("arbitrary", "arbitrary")),
    )(c, w, m, v, mine[0], mine[1], theirs[0], theirs[1])


def adamw_shard_t(wt, mt, vt, mine_t, theirs_t, c, name):
    _, cw, r = wt.shape
    h = mine_t[0].shape[1]
    tc = 256
    nb = h // tc
    wblk = pl.BlockSpec((None, cw, tc), lambda l, j, c_ref: (l, 0, j))
    gblk = lambda layer, own: pl.BlockSpec((cw, tc), lambda l, j, c_ref: (0, _held_block(l, j, c_ref, layer, own, nb)))
    return _adamw_halves(wt, mt, vt, mine_t, theirs_t, c, name, (DEPTH, r // tc), wblk, gblk, nb, cw)


def _half_rows(h, cols):
    for tr in (512, 256, 352, 128, 64):
        if h % tr == 0 and tr * cols * 4 <= 6 * 1024 * 1024:
            return tr
    raise ValueError((h, cols))


def add_sibling(grad, recv, c):
    _, r, cols = grad.shape
    h = r // 2
    tr = _half_rows(h, cols)
    nb = h // tr

    def body(c_ref, g_ref, r_ref, o_ref):
        o_ref[...] = (g_ref[...].astype(F32) + r_ref[...].astype(F32)).astype(BF16)

    return pl.pallas_call(
        body, name="add_sibling",
        grid_spec=pltpu.PrefetchScalarGridSpec(
            num_scalar_prefetch=1, grid=(N_CHIPS, nb),
            in_specs=[pl.BlockSpec((None, tr, cols), lambda t, i, c_ref: (t, c_ref[0] * nb + i, 0)),
                      pl.BlockSpec((None, tr, cols), lambda t, i, c_ref: (t, i, 0))],
            out_specs=pl.BlockSpec((None, tr, cols), lambda t, i, c_ref: (t, i, 0))),
        out_shape=jax.ShapeDtypeStruct((N_CHIPS, h, cols), BF16), compiler_params=_cparams(("arbitrary", "arbitrary")),
    )(c, grad, recv)


def add_chips(part, recv, s):
    _, h, cols = part.shape
    tr = _half_rows(h, cols)

    def body(s_ref, p_ref, r_ref, o_ref):
        o_ref[...] = ((p_ref[...].astype(F32) + r_ref[0].astype(F32)) + r_ref[1].astype(F32)) + r_ref[2].astype(F32)

    return pl.pallas_call(
        body, name="add_chips",
        grid_spec=pltpu.PrefetchScalarGridSpec(
            num_scalar_prefetch=1, grid=(h // tr,),
            in_specs=[pl.BlockSpec((None, tr, cols), lambda i, s_ref: (s_ref[0], i, 0)),
                      pl.BlockSpec((3, tr, cols), lambda i, s_ref: (0, i, 0))],
            out_specs=pl.BlockSpec((tr, cols), lambda i, s_ref: (i, 0))),
        out_shape=jax.ShapeDtypeStruct((h, cols), F32), compiler_params=_cparams(("arbitrary",)),
    )(s, part, recv)


def allreduce_small(vec):
    rows = vec.shape[0]

    def body(v_ref, o_ref, buf, ssem, rsem, lsem):
        x, y, c, chips = _place()
        me, sibling = (x, y, c), (x, y, 1 - c)

        def blk(px, py, pc):
            return buf.at[4 * px + 2 * py + pc]

        def copy(k, block, to, src=None):
            return _rcopy(blk(*block) if src is None else src, blk(*block), ssem.at[k], rsem.at[k], to)

        mine = pltpu.make_async_copy(v_ref, blk(*me), lsem)
        mine.start()
        first = [copy(0, me, sibling, src=v_ref)] + [copy(1 + j, me, (*chip, c), src=v_ref) for j, chip in enumerate(chips)]
        for cp in first:
            cp.start()
        passed = [copy(4 + j, (*chip, c), sibling) for j, chip in enumerate(chips)]
        for j, chip in enumerate(chips):
            copy(1 + j, (*chip, c), me).wait_recv()
            passed[j].start()
        copy(0, sibling, me).wait_recv()
        for j, chip in enumerate(chips):
            copy(4 + j, (*chip, 1 - c), me).wait_recv()
        for cp in first + passed:
            cp.wait_send()
        mine.wait()
        acc = buf[0]
        for d in range(1, N_DEV):
            acc = acc + buf[d]
        o_ref[...] = acc

    vm = pl.BlockSpec(memory_space=pltpu.VMEM)
    return pl.pallas_call(
        body, name="allreduce_small", in_specs=[vm], out_specs=vm, out_shape=jax.ShapeDtypeStruct(vec.shape, F32),
        scratch_shapes=[pltpu.VMEM((N_DEV, rows, HD), F32), pltpu.SemaphoreType.DMA((7,)), pltpu.SemaphoreType.DMA((7,)),
                        pltpu.SemaphoreType.DMA],
        compiler_params=pltpu.CompilerParams(vmem_limit_bytes=VMEM_LIMIT),
    )(vec)


SMALL_NAMES = ("norm1_g", "sgu_norm_g", "w_spatial", "b_spatial", "conv_w", "a_log", "dt_bias", "o_norm_g", "q_norm_g",
               "k_norm_g", "norm2_g")


def small_params(l, p, conv_full):
    return {"norm1_g": p["norm1_g"][l][None], "sgu_norm_g": p["sgu_norm_g"][l][:, None, :], "w_spatial": p["w_spatial"][l],
            "b_spatial": p["b_spatial"][l][..., None], "conv_w": conv_full[l], "a_log": p["a_log"][l], "dt_bias": p["dt_bias"][l],
            "o_norm_g": p["o_norm_g"][l][None], "q_norm_g": p["q_norm_g"][l][None], "k_norm_g": p["k_norm_g"][l][None],
            "norm2_g": p["norm2_g"][l][None]}


_PACK_TILE = 8 * HD


def _pack(arrays):
    flat = jnp.concatenate([a.reshape(-1) for a in arrays])
    pad = -flat.shape[0] % _PACK_TILE
    return jnp.pad(flat, (0, pad)).reshape(-1, HD)


def _unpack(packed, shapes):
    flat, out, off = packed.reshape(-1), [], 0
    for shp in shapes:
        n = int(np.prod(shp))
        out.append(flat[off:off + n].reshape(shp))
        off += n
    return out


WEIGHT_ORDER = ("norm1_g", "w_in", "sgu_norm_g", "w_spatial", "b_spatial", "conv_w", "a_log", "dt_bias", "o_norm_g", "q_norm_g",
                "k_norm_g", "w_out", "norm2_g", "w_gate_up", "w_down")


def kernel(x, norm1_g, w_in, sgu_norm_g, w_spatial, b_spatial, conv_w, a_log, dt_bias, o_norm_g, q_norm_g, k_norm_g, w_out, norm2_g, w_gate_up, w_down, loss_target, m_norm1_g, m_w_in, m_sgu_norm_g, m_w_spatial, m_b_spatial, m_conv_w, m_a_log, m_dt_bias, m_o_norm_g, m_q_norm_g, m_k_norm_g, m_w_out, m_norm2_g, m_w_gate_up, m_w_down, v_norm1_g, v_w_in, v_sgu_norm_g, v_w_spatial, v_b_spatial, v_conv_w, v_a_log, v_dt_bias, v_o_norm_g, v_q_norm_g, v_k_norm_g, v_w_out, v_norm2_g, v_w_gate_up, v_w_down):
    w = dict(norm1_g=norm1_g, w_in=w_in, sgu_norm_g=sgu_norm_g, w_spatial=w_spatial, b_spatial=b_spatial, conv_w=conv_w,
             a_log=a_log, dt_bias=dt_bias, o_norm_g=o_norm_g, q_norm_g=q_norm_g, k_norm_g=k_norm_g, w_out=w_out,
             norm2_g=norm2_g, w_gate_up=w_gate_up, w_down=w_down)
    m = dict(norm1_g=m_norm1_g, w_in=m_w_in, sgu_norm_g=m_sgu_norm_g, w_spatial=m_w_spatial, b_spatial=m_b_spatial,
             conv_w=m_conv_w, a_log=m_a_log, dt_bias=m_dt_bias, o_norm_g=m_o_norm_g, q_norm_g=m_q_norm_g, k_norm_g=m_k_norm_g,
             w_out=m_w_out, norm2_g=m_norm2_g, w_gate_up=m_w_gate_up, w_down=m_w_down)
    v = dict(norm1_g=v_norm1_g, w_in=v_w_in, sgu_norm_g=v_sgu_norm_g, w_spatial=v_w_spatial, b_spatial=v_b_spatial,
             conv_w=v_conv_w, a_log=v_a_log, dt_bias=v_dt_bias, o_norm_g=v_o_norm_g, q_norm_g=v_q_norm_g, k_norm_g=v_k_norm_g,
             w_out=v_w_out, norm2_g=v_norm2_g, w_gate_up=v_w_gate_up, w_down=v_w_down)
    chip = (2 * lax.axis_index("x") + lax.axis_index("y")).astype(jnp.int32)
    core = lax.axis_index("c").astype(jnp.int32)

    in_pad = IN_SHARD_PAD - IN_SHARD
    w_in_pad = jnp.pad(w_in, ((0, 0), (0, 0), (0, in_pad)))

    halves_of = lambda a: a.reshape(2, a.shape[0] // 2, a.shape[1])
    start_0, wait_0 = gather_direct([halves_of(w_in_pad[0].astype(BF16)), halves_of(conv_w[0])], "mix0")
    state_0, token_0 = start_0()
    bf_halves = lambda a: halves_of((a + token_0[0, 0]).astype(BF16))

    def ffn_shards(l):
        return [bf_halves(w_gate_up[l]), bf_halves(w_down[l]), bf_halves(w_out[l])]

    def mixer_shards(l):
        return [bf_halves(w_in_pad[l]), halves_of(conv_w[l])]

    def mixer_weights(g):
        g_in, g_conv = g
        return (shards_to_segments(g_in.reshape(N_CHIPS, D, IN_SHARD_PAD))[None],
                g_conv.reshape(N_CHIPS, B_CONV, -1).transpose(1, 0, 2).reshape(B_CONV, 3 * B_WIDTH))

    def ffn_weights(g, w_in_seg):
        g_gu, g_down, g_out = g
        return {"in": w_in_seg, "out": g_out.reshape(1, D, D), "gu": g_gu.reshape(1, N_CHIPS, D, GU_SHARD),
                "down": None if g_down is None else g_down.reshape(1, FFN, D)}

    def layer_params(l, conv_full):
        return small_params(0, {n: w[n][l:l + 1] for n in SMALL_NAMES if n != "conv_w"}, conv_full[None])

    gu0, down0, out0 = ffn_shards(0)
    start_a, wait_a = gather_direct([gu0, out0], "ffn0")
    start_b, wait_b = gather_direct([down0] + mixer_shards(1), "mid")
    start_c, wait_c = gather_direct(ffn_shards(1), "ffn1")
    state_a, token_a = start_a(token_0)
    state_b, token_b = start_b(token_a)
    state_c, token_c = start_c(token_b)
    h1_0 = rmsnorm_fwd(x[0], norm1_g[0][None], token_c)
    w_in0, conv0 = mixer_weights(pass_to_sibling(wait_0(state_0, h1_0)[1]))
    sps = [layer_params(0, conv0), None]
    mix0, saved_m0 = mixers_forward(x[0], w_in0, sps[0], h1=h1_0)
    g_gu0, g_out0 = pass_to_sibling(wait_a(state_a, mix0)[1])
    wg0 = ffn_weights((g_gu0, None, g_out0), w_in0)
    x1_0, h2_0, gu_0, act_0 = ffn_up(x[0], mix0, wg0["out"], wg0["gu"], sps[0])
    g_down0, g_in1, g_conv1 = pass_to_sibling(wait_b(state_b, act_0)[1])
    wg0["down"] = g_down0.reshape(1, FFN, D)
    x1 = mm_down(act_0, wg0["down"], 0, x1_0)
    saved_f0 = (x1_0, h2_0, gu_0, act_0)
    w_in1, conv1 = mixer_weights((g_in1, g_conv1))
    sps[1] = layer_params(1, conv1)
    mix1, saved_m1 = mixers_forward(x1, w_in1, sps[1])
    wg1 = ffn_weights(pass_to_sibling(wait_c(state_c, mix1)[1]), w_in1)
    x2, saved_f1 = ffn_forward(x1, mix1, wg1, sps[1])
    saved1 = (saved_m1, saved_f1)
    dx, loss_tile = loss_and_grad(x2, loss_target[0])

    def sibling_exchange(tag, by_chip):
        cell = {}

        def on_weight_grads(*dws):
            start, wait = exchange_halves([f(t) for f, t in zip(by_chip, dws)], tag)
            state, token = start()
            cell["wait"] = functools.partial(wait, state)
            return token

        return on_weight_grads, cell

    def start_scatter(cell, after, tag, dep=None):
        grads, from_sibling = cell["wait"](after)
        parts = [add_sibling(g, r, core.reshape(1)) for g, r in zip(grads, from_sibling)]
        start, wait = scatter_direct(parts, tag)
        state, token = start(dep)
        return functools.partial(wait, state), token

    smalls = [None] * DEPTH
    same = lambda t: t
    ffn_by_chip = (same, lambda t: t.reshape(N_CHIPS, DOWN_SHARD, D))
    mix_by_chip = (same, lambda t: t.reshape(N_CHIPS, OUT_SHARD, D))
    hook, cell_f1 = sibling_exchange("ffn1", ffn_by_chip)
    dx1, dnorm2_1, _, _ = ffn_backward(dx, wg1, sps[1], saved1[1], on_weight_grads=hook)
    wait_f1, tok_f1 = start_scatter(cell_f1, dx1, "ffn1")
    hook, cell_m1 = sibling_exchange("mix1", mix_by_chip)
    dx, _, _, small1 = mixers_backward(dx1, wg1, sps[1], saved1[0], dep=tok_f1, on_weight_grads=hook)
    smalls[1] = {**small1, "norm2_g": dnorm2_1}
    wait_m1, tok_m1 = start_scatter(cell_m1, dx, "mix1")
    hook, cell_f0 = sibling_exchange("ffn0", ffn_by_chip)
    dx1, dnorm2_0, _, _ = ffn_backward(dx, wg0, sps[0], saved_f0, dep=tok_m1, on_weight_grads=hook)
    wait_f0, tok_f0 = start_scatter(cell_f0, dx1, "ffn0")
    hook, cell_m0 = sibling_exchange("mix0", mix_by_chip)
    dx, _, _, small0 = mixers_backward(dx1, wg0, sps[0], saved_m0, dep=tok_f0, on_weight_grads=hook)
    smalls[0] = {**small0, "norm2_g": dnorm2_0}

    grad, delta, new_m, new_v = {}, {}, {}, {}
    stacked = [jnp.stack([smalls[l][n] for l in range(DEPTH)]) for n in SMALL_NAMES]
    total = allreduce_small(_pack(stacked + [loss_tile[0, :1]]))
    shapes = [(DEPTH, B_CONV, 3 * B_WIDTH) if n == "conv_w" else w[n].shape for n in SMALL_NAMES]
    small_grads = dict(zip(SMALL_NAMES, _unpack(total, shapes + [(1,)])[:-1]))
    loss = _unpack(total, shapes + [(1,)])[-1][0]
    conv_cols = conv_w.shape[-1]
    small_grads["conv_w"] = lax.dynamic_slice_in_dim(small_grads["conv_w"], chip * conv_cols, conv_cols, axis=2)
    grad.update(small_grads)
    sshapes = [w[n].shape for n in SMALL_NAMES]
    packed = [_pack([d[n] for n in SMALL_NAMES]) for d in (w, grad, m, v)]
    for dst, t in zip((delta, new_m, new_v), adamw(*packed, "adamw_small")):
        dst.update(zip(SMALL_NAMES, _unpack(t, sshapes)))

    wait_m0, tok_m0 = start_scatter(cell_m0, dx, "mix0", dep=total)
    (pf0, rf0), (pm1, rm1), (pf1, rf1) = (wt(tok_m0) for wt in (wait_f0, wait_m1, wait_f1))

    def reduce_group(parts, from_chips):
        mine = [add_chips(p, r, chip.reshape(1)) for p, r in zip(parts, from_chips)]
        return mine, list(share_halves(mine))

    mine_f, theirs_f = reduce_group(pf0 + pf1 + pm1, rf0 + rf1 + rm1)
    for a, n in enumerate(("w_gate_up", "w_down")):
        grad[n], delta[n], new_m[n], new_v[n] = adamw_shard(w[n], m[n], v[n], [mine_f[a], mine_f[2 + a]],
                                                            [theirs_f[a], theirs_f[2 + a]], core.reshape(1), "adamw_" + n)

    mine_m0, theirs_m0 = reduce_group(*wait_m0(new_v["w_down"]))
    tr_ = lambda t: jnp.swapaxes(t, -1, -2)
    cut = lambda t: tr_(t[:, :IN_SHARD])
    res = adamw_shard_t(tr_(w_in), tr_(m_w_in), tr_(v_w_in), [cut(mine_m0[0]), cut(mine_f[4])],
                        [cut(theirs_m0[0]), cut(theirs_f[4])], core.reshape(1), "adamw_w_in")
    grad["w_in"], delta["w_in"], new_m["w_in"], new_v["w_in"] = (tr_(t) for t in res)
    grad["w_out"], delta["w_out"], new_m["w_out"], new_v["w_out"] = adamw_shard(
        w_out, m_w_out, v_w_out, [mine_m0[1], mine_f[5]], [theirs_m0[1], theirs_f[5]], core.reshape(1), "adamw_w_out")

    out = [loss, dx[None]]
    for d in (grad, delta, new_m, new_v):
        out += [d[n] for n in WEIGHT_ORDER]
    return tuple(out)
```

```python
import functools
import math

import numpy as np
import jax
import jax.numpy as jnp
from jax import lax
from jax.experimental import pallas as pl
from jax.experimental.pallas import tpu as pltpu

F32 = jnp.float32
BF16 = jnp.bfloat16
HI = lax.Precision.HIGH

T = 2048
D = 2048
DEPTH = 2
HD = 128
A_GROUPS, A_WIDTH, A_CHUNK = 4, 512, 128
B_HEADS, B_WIDTH, B_CONV, B_CHUNK = 6, 768, 4, 64
C_HEADS, C_WIDTH, C_BLOCK = 6, 768, 128
C_BRANCHES = ((128, 1), (512, 4), (2048, 16))
FFN = 5632
IN_TOTAL = 6412
EPS = 1e-6
N_CHIPS = 4
N_DEV = 8
IN_SHARD = IN_TOTAL // N_CHIPS
IN_SHARD_PAD = 1664
GU_SHARD = 2 * FFN // N_CHIPS
OUT_SHARD = D // N_CHIPS
DOWN_SHARD = FFN // N_CHIPS
P_AU, P_AV, P_BQ, P_BK, P_BV, P_BG, P_BB, P_CQ, P_CK, P_CV, P_END = (
    0, 512, 1024, 1792, 2560, 3328, 4096, 4224, 4992, 5760, 6528)
GATE_COLS = 4108
VMEM_LIMIT = 56 * 1024 * 1024

ADAM_LR, ADAM_B1, ADAM_B2, ADAM_EPS, ADAM_WD, ADAM_STEP = 0.001, 0.9, 0.999, 1e-08, 0.01, 10


def _cparams(sem, vmem=VMEM_LIMIT):
    return pltpu.CompilerParams(dimension_semantics=sem, vmem_limit_bytes=vmem)


def _dims(nd, ta, tb):
    off = nd - 2
    ca = off + (0 if ta else 1)
    cb = off + (1 if tb else 0)
    batch = ((0,), (0,)) if nd == 3 else ((), ())
    return (((ca,), (cb,)), batch)


def _raw_mm(a, b, ta, tb, hi):
    if hi:
        return lax.dot_general(a, b, _dims(a.ndim, ta, tb), precision=HI, preferred_element_type=F32)
    return lax.dot_general(a.astype(BF16), b.astype(BF16), _dims(a.ndim, ta, tb), preferred_element_type=F32)


@functools.partial(jax.custom_vjp, nondiff_argnums=(2, 3, 4))
def _mm(a, b, ta=False, tb=False, hi=False):
    return _raw_mm(a, b, ta, tb, hi)


def _mm_fwd(a, b, ta, tb, hi):
    return _raw_mm(a, b, ta, tb, hi), (a, b)


def _mm_bwd(ta, tb, hi, res, g):
    a, b = res
    da = _raw_mm(g, b, False, not tb, False) if not ta else _raw_mm(b, g, tb, True, False)
    db = _raw_mm(a, g, not ta, False, False) if not tb else _raw_mm(g, a, True, ta, False)
    return da.astype(a.dtype), db.astype(b.dtype)


_mm.defvjp(_mm_fwd, _mm_bwd)


def _rms(x, g):
    return x * lax.rsqrt(jnp.mean(x * x, axis=-1, keepdims=True) + EPS) * g


def _gelu(x):
    return 0.5 * x * (1.0 + jnp.tanh(math.sqrt(2.0 / math.pi) * (x + 0.044715 * (x * x * x))))


def _sigmoid(x):
    return 1.0 / (1.0 + jnp.exp(-x))


def _silu(x):
    return x * _sigmoid(x)


def _softplus(x):
    return jnp.maximum(x, 0.0) + jnp.log(1.0 + jnp.exp(-jnp.abs(x)))


def _iota(shape, dim):
    return lax.broadcasted_iota(jnp.int32, shape, dim)


def _sgu_fn(u, v, sg, w, b):
    nc = T // A_CHUNK
    ug = _gelu(u)
    vn = _rms(_gelu(v), sg)
    causal = _iota((A_CHUNK, A_CHUNK), 0) >= _iota((A_CHUNK, A_CHUNK), 1)
    wm = jnp.where(causal, w, 0.0)
    wb = jnp.broadcast_to(wm[None], (nc, A_CHUNK, A_CHUNK))
    z = _mm(wb, vn.reshape(nc, A_CHUNK, HD)) + b[None]
    return ug * z.reshape(T, HD)


def _sgu_specs():
    col = lambda off: pl.BlockSpec((T, HD), lambda g, off=off: (0, off + g))
    par = [pl.BlockSpec((None, 1, HD), lambda g: (g, 0, 0)),
           pl.BlockSpec((None, A_CHUNK, A_CHUNK), lambda g: (g, 0, 0)),
           pl.BlockSpec((None, A_CHUNK, 1), lambda g: (g, 0, 0))]
    return col, par


def sgu_fwd(p2, sg, w, b):
    col, par = _sgu_specs()

    def body(u_ref, v_ref, sg_ref, w_ref, b_ref, y_ref):
        y_ref[...] = _sgu_fn(u_ref[...], v_ref[...], sg_ref[...], w_ref[...], b_ref[...]).astype(BF16)

    return pl.pallas_call(
        body, name="sgu_fwd", grid=(A_GROUPS,),
        in_specs=[col(P_AU // HD), col(P_AV // HD)] + par,
        out_specs=pl.BlockSpec((T, HD), lambda g: (0, g)),
        out_shape=jax.ShapeDtypeStruct((T, A_WIDTH), BF16),
        compiler_params=_cparams(("arbitrary",)),
    )(p2, p2, sg, w, b)


def sgu_bwd(p2, sg, w, b, dmix):
    col, par = _sgu_specs()

    def body(u_ref, v_ref, sg_ref, w_ref, b_ref, dy_ref, du_ref, dv_ref, dsg_ref, dw_ref, db_ref):
        _, vjp = jax.vjp(_sgu_fn, u_ref[...], v_ref[...], sg_ref[...], w_ref[...], b_ref[...])
        du, dv, dsg, dw, db = vjp(dy_ref[...])
        du_ref[...] = du.astype(BF16)
        dv_ref[...] = dv.astype(BF16)
        dsg_ref[...] = dsg
        dw_ref[...] = dw
        db_ref[...] = db

    gcol = pl.BlockSpec((T, HD), lambda g: (0, g))
    return pl.pallas_call(
        body, name="sgu_bwd", grid=(A_GROUPS,),
        in_specs=[col(P_AU // HD), col(P_AV // HD)] + par + [gcol],
        out_specs=[gcol, gcol] + par,
        out_shape=[jax.ShapeDtypeStruct((T, A_WIDTH), BF16), jax.ShapeDtypeStruct((T, A_WIDTH), BF16),
                   jax.ShapeDtypeStruct((A_GROUPS, 1, HD), F32), jax.ShapeDtypeStruct((A_GROUPS, A_CHUNK, A_CHUNK), F32),
                   jax.ShapeDtypeStruct((A_GROUPS, A_CHUNK, 1), F32)],
        compiler_params=_cparams(("arbitrary",)),
    )(p2, p2, sg, w, b, dmix)


def _attn_fn(q, k, v, qg, kg, slope, *, dil, nb):
    n = T // C_BLOCK
    qb = _rms(q, qg).reshape(n, C_BLOCK, HD)
    kb = _rms(k, kg).reshape(n, C_BLOCK, HD)
    vb = v.reshape(n, C_BLOCK, HD)
    scale = HD ** -0.5
    qi = _iota((n, C_BLOCK, C_BLOCK), 1)
    kj = _iota((n, C_BLOCK, C_BLOCK), 2)
    sl = slope[None] * float(dil)
    d_cur = qi - kj
    sc = jnp.where(d_cur >= 0, _mm(qb, kb, tb=True) * scale - sl * d_cur.astype(F32), -jnp.inf)
    mx = jnp.max(sc, axis=-1, keepdims=True)
    if nb > 1:
        kp = jnp.concatenate([jnp.zeros((1, C_BLOCK, HD), F32), kb[:-1]], axis=0)
        vp = jnp.concatenate([jnp.zeros((1, C_BLOCK, HD), F32), vb[:-1]], axis=0)
        has_prev = (_iota((n, C_BLOCK, C_BLOCK), 0) % nb) > 0
        d_prev = C_BLOCK + qi - kj
        sp = jnp.where((kj >= qi) & has_prev, _mm(qb, kp, tb=True) * scale - sl * d_prev.astype(F32), -jnp.inf)
        mx = jnp.maximum(mx, jnp.max(sp, axis=-1, keepdims=True))
    p = jnp.exp(sc - mx)
    den = jnp.sum(p, axis=-1, keepdims=True)
    if nb > 1:
        pp = jnp.exp(sp - mx)
        den = den + jnp.sum(pp, axis=-1, keepdims=True)
    inv = 1.0 / den
    out = _mm(p * inv, vb)
    if nb > 1:
        out = out + _mm(pp * inv, vp)
    lse = mx + jnp.log(den)
    return out.reshape(T, HD), jnp.broadcast_to(lse, (n, C_BLOCK, HD)).reshape(T, HD)


def _combine_fn(o1, o2, o3, l1, l2, l3):
    mx = jnp.maximum(jnp.maximum(l1, l2), l3)
    e1, e2, e3 = jnp.exp(l1 - mx), jnp.exp(l2 - mx), jnp.exp(l3 - mx)
    r = 1.0 / (e1 + e2 + e3)
    return (e1 * r) * o1 + (e2 * r) * o2 + (e3 * r) * o3


def _branch_blocks(dil):
    return -(-(T // dil) // C_BLOCK)


def _load_branch_order(ref, dil):
    if dil == 1:
        return ref[...]
    seg = T // dil
    return jnp.concatenate([ref[pl.ds(r, seg, stride=dil), :] for r in range(dil)], axis=0)


def _store_position_order(ref, val, dil, add=False):
    seg = T // dil
    for r in range(dil):
        rows = slice(None) if dil == 1 else pl.ds(r, seg, stride=dil)
        piece = val if dil == 1 else val[r * seg:(r + 1) * seg]
        if add:
            ref[rows, :] += piece
        else:
            ref[rows, :] = piece


def _dattn_specs():
    col = lambda off: pl.BlockSpec((T, HD), lambda h, off=off: (0, off // HD + h))
    row = pl.BlockSpec((1, HD), lambda h: (0, 0))
    slope = pl.BlockSpec((None, 1, HD), lambda h: (h, 0, 0))
    return [col(P_CQ), col(P_CK), col(P_CV), row, row, slope]


def _dattn_branches(q_ref, k_ref, v_ref, qg, kg, slope, o_scr, l_scr):
    for b, (_, dil) in enumerate(C_BRANCHES):
        q, k, v = (_load_branch_order(r, dil) for r in (q_ref, k_ref, v_ref))
        o, l = _attn_fn(q, k, v, qg, kg, slope, dil=dil, nb=_branch_blocks(dil))
        _store_position_order(o_scr.at[b], o, dil)
        _store_position_order(l_scr.at[b], l, dil)


def dattn_fwd(p2, qg, kg, slopes):
    per_branch = pl.BlockSpec((3, T, HD), lambda h: (0, 0, h))

    def body(q_ref, k_ref, v_ref, qg_ref, kg_ref, s_ref, y_ref, o_ref, l_ref):
        _dattn_branches(q_ref, k_ref, v_ref, qg_ref[...], kg_ref[...], s_ref[...], o_ref, l_ref)
        y_ref[...] = _combine_fn(o_ref[0], o_ref[1], o_ref[2], l_ref[0], l_ref[1], l_ref[2]).astype(BF16)

    return pl.pallas_call(
        body, name="dattn_fwd", grid=(C_HEADS,), in_specs=_dattn_specs(),
        out_specs=[pl.BlockSpec((T, HD), lambda h: (0, h)), per_branch, per_branch],
        out_shape=[jax.ShapeDtypeStruct((T, C_WIDTH), BF16)] + [jax.ShapeDtypeStruct((3, T, C_WIDTH), F32)] * 2,
        compiler_params=_cparams(("arbitrary",)),
    )(p2, p2, p2, qg, kg, slopes)


def dattn_bwd(p2, qg, kg, slopes, outs, lses, dmix):
    hcol = pl.BlockSpec((T, HD), lambda h: (0, h))
    row = pl.BlockSpec((1, HD), lambda h: (0, 0))
    dy = pl.BlockSpec((T, HD), lambda h: (0, (A_WIDTH + B_WIDTH) // HD + h))
    per_branch = pl.BlockSpec((3, T, HD), lambda h: (0, 0, h))

    def body(q_ref, k_ref, v_ref, qg_ref, kg_ref, s_ref, o_scr, l_scr, dy_ref, dq_ref, dk_ref, dv_ref, dqg_ref, dkg_ref, g_scr,
             acc):
        qg, kg, slope = qg_ref[...], kg_ref[...], s_ref[...]
        _, vjp = jax.vjp(_combine_fn, o_scr[0], o_scr[1], o_scr[2], l_scr[0], l_scr[1], l_scr[2])
        for i, g in enumerate(vjp(dy_ref[...])):
            g_scr[i] = g

        @pl.when(pl.program_id(0) == 0)
        def _():
            dqg_ref[...] = jnp.zeros_like(dqg_ref)
            dkg_ref[...] = jnp.zeros_like(dkg_ref)

        for b, (_, dil) in enumerate(C_BRANCHES):
            q, k, v = (_load_branch_order(r, dil) for r in (q_ref, k_ref, v_ref))
            do, dl = _load_branch_order(g_scr.at[b], dil), _load_branch_order(g_scr.at[3 + b], dil)
            fn = functools.partial(_attn_fn, dil=dil, nb=_branch_blocks(dil))
            _, vjp_b = jax.vjp(lambda a, b_, c, d, e, fn=fn: fn(a, b_, c, d, e, slope), q, k, v, qg, kg)
            dq, dk, dv, dqg, dkg = vjp_b((do, dl))
            for i, val in enumerate((dq, dk, dv)):
                _store_position_order(acc.at[i], val, dil, add=b > 0)
            dqg_ref[...] += dqg
            dkg_ref[...] += dkg
        for i, ref in enumerate((dq_ref, dk_ref, dv_ref)):
            ref[...] = acc[i].astype(BF16)

    scr = lambda n: pltpu.VMEM((n, T, HD), F32)
    return pl.pallas_call(
        body, name="dattn_bwd", grid=(C_HEADS,), in_specs=_dattn_specs() + [per_branch, per_branch, dy],
        out_specs=[hcol, hcol, hcol, row, row],
        out_shape=[jax.ShapeDtypeStruct((T, C_WIDTH), BF16)] * 3 + [jax.ShapeDtypeStruct((1, HD), F32)] * 2,
        scratch_shapes=[scr(6), scr(3)], compiler_params=_cparams(("arbitrary",)),
    )(p2, p2, p2, qg, kg, slopes, outs, lses, dmix)


_NCH = T // B_CHUNK


def _conv_taps(x, w_ref):
    rows = _iota(x.shape, 0)
    taps = []
    for j in range(B_CONV):
        s = B_CONV - 1 - j
        taps.append(x if s == 0 else jnp.where(rows >= s, pltpu.roll(x, s, 0), 0.0))
    pre = sum(w_ref[j:j + 1, :] * taps[j] for j in range(B_CONV))
    return pre, taps


def _conv_post(pre, mode):
    y = _silu(pre)
    if mode == "v":
        return y
    y = y * lax.rsqrt(jnp.sum(y * y, axis=-1, keepdims=True) + EPS)
    return y * (HD ** -0.5) if mode == "q" else y


def conv_fwd(p2, conv_w, mode):
    idx = "qkv".index(mode)
    xcol = pl.BlockSpec((T, HD), lambda h: (0, P_BQ // HD + B_HEADS * idx + h))
    wcol = pl.BlockSpec((B_CONV, HD), lambda h: (0, B_HEADS * idx + h))
    hcol = pl.BlockSpec((T, HD), lambda h: (0, h))

    def body(x_ref, w_ref, y_ref):
        pre, _ = _conv_taps(x_ref[...], w_ref)
        y_ref[...] = _conv_post(pre, mode)

    return pl.pallas_call(
        body, name=f"conv_fwd_{mode}", grid=(B_HEADS,), in_specs=[xcol, wcol], out_specs=hcol,
        out_shape=jax.ShapeDtypeStruct((T, B_WIDTH), F32), compiler_params=_cparams(("arbitrary",)),
    )(p2, conv_w)


def conv_bwd(p2, conv_w, dys, mode):
    idx = "qkv".index(mode)
    xcol = pl.BlockSpec((T, HD), lambda h: (0, P_BQ // HD + B_HEADS * idx + h))
    wcol = pl.BlockSpec((B_CONV, HD), lambda h: (0, B_HEADS * idx + h))
    hcol = pl.BlockSpec((T, HD), lambda h: (0, h))
    wout = pl.BlockSpec((B_CONV, HD), lambda h: (0, h))

    def body(x_ref, w_ref, *rest):
        dy_refs, (dx_ref, dw_ref) = rest[:-2], rest[-2:]
        pre, taps = _conv_taps(x_ref[...], w_ref)
        _, vjp = jax.vjp(functools.partial(_conv_post, mode=mode), pre)
        (dpre,) = vjp(sum(r[...] for r in dy_refs))
        rows = _iota(dpre.shape, 0)
        dx = w_ref[B_CONV - 1:B_CONV, :] * dpre
        for j in range(B_CONV):
            s = B_CONV - 1 - j
            dw_ref[j:j + 1, :] = jnp.sum(dpre * taps[j], axis=0, keepdims=True)
            if s > 0:
                dx = dx + w_ref[j:j + 1, :] * jnp.where(rows < T - s, pltpu.roll(dpre, T - s, 0), 0.0)
        dx_ref[...] = dx.astype(BF16)

    return pl.pallas_call(
        body, name=f"conv_bwd_{mode}", grid=(B_HEADS,), in_specs=[xcol, wcol] + [hcol] * len(dys), out_specs=[hcol, wout],
        out_shape=[jax.ShapeDtypeStruct((T, B_WIDTH), BF16), jax.ShapeDtypeStruct((B_CONV, B_WIDTH), F32)],
        compiler_params=_cparams(("arbitrary",)),
    )(p2, conv_w, *dys)


def _gates_fn(bg, al, dtb, h):
    r = _iota((HD, HD), 0)
    logit = _mm(bg, (r == h).astype(F32), hi=True)
    a = _mm(bg, (r == h + B_HEADS).astype(F32), hi=True)
    beta = _sigmoid(logit)
    graw = -jnp.exp(al) * _softplus(a + dtb)
    tri = (_iota((_NCH, B_CHUNK, B_CHUNK), 1) >= _iota((_NCH, B_CHUNK, B_CHUNK), 2)).astype(F32)
    g = _mm(tri, graw.reshape(_NCH, B_CHUNK, HD), hi=True).reshape(T, HD)
    return beta, g


def _gates_specs():
    bg = pl.BlockSpec((T, HD), lambda h: (0, P_BB // HD))
    par = pl.BlockSpec((None, 1, HD), lambda h: (h, 0, 0))
    out = pl.BlockSpec((None, T, HD), lambda h: (h, 0, 0))
    return bg, par, out


def gates_fwd(p2, al, dtb):
    bg, par, out = _gates_specs()

    def body(bg_ref, al_ref, dtb_ref, beta_ref, g_ref):
        beta, g = _gates_fn(bg_ref[...], al_ref[...], dtb_ref[...], pl.program_id(0))
        beta_ref[...] = beta
        g_ref[...] = g

    return pl.pallas_call(
        body, name="gates_fwd", grid=(B_HEADS,), in_specs=[bg, par, par], out_specs=[out, out],
        out_shape=[jax.ShapeDtypeStruct((B_HEADS, T, HD), F32)] * 2, compiler_params=_cparams(("arbitrary",)),
    )(p2, al, dtb)


def gates_bwd(p2, al, dtb, dbeta, dg1, dg2):
    bg, par, out = _gates_specs()
    acc = pl.BlockSpec((T, HD), lambda h: (0, 0))

    def body(bg_ref, al_ref, dtb_ref, dbeta_ref, dg1_ref, dg2_ref, dbg_ref, dal_ref, ddtb_ref, acc_ref):
        h = pl.program_id(0)
        _, vjp = jax.vjp(lambda a, b, c: _gates_fn(a, b, c, h), bg_ref[...], al_ref[...], dtb_ref[...])
        dbg, dal, ddtb = vjp((dbeta_ref[...], dg1_ref[...] + dg2_ref[...]))

        @pl.when(h == 0)
        def _():
            acc_ref[...] = jnp.zeros_like(acc_ref)

        acc_ref[...] += dbg
        dbg_ref[...] = acc_ref[...].astype(BF16)
        dal_ref[...] = jnp.broadcast_to(jnp.sum(dal, axis=-1, keepdims=True), (1, HD))
        ddtb_ref[...] = jnp.broadcast_to(jnp.sum(ddtb, axis=-1, keepdims=True), (1, HD))

    return pl.pallas_call(
        body, name="gates_bwd", grid=(B_HEADS,), in_specs=[bg, par, par, out, out, out], out_specs=[acc, par, par],
        out_shape=[jax.ShapeDtypeStruct((T, HD), BF16)] + [jax.ShapeDtypeStruct((B_HEADS, 1, HD), F32)] * 2,
        scratch_shapes=[pltpu.VMEM((T, HD), F32)], compiler_params=_cparams(("arbitrary",)),
    )(p2, al, dtb, dbeta, dg1, dg2)


def _unit_lower_inverse(a):
    eye = (_iota(a.shape, 1) == _iota(a.shape, 2)).astype(F32)
    x = eye - a
    p = _mm(a, a, hi=True)
    for i in range(5):
        x = x + _mm(x, p, hi=True)
        if i < 4:
            p = _mm(p, p, hi=True)
    return x


_WY_CH = 16
_WY_ROWS = _WY_CH * B_CHUNK


def _wy_fn(q, k, v, beta, g):
    sh = (q.shape[0] // B_CHUNK, B_CHUNK, HD)
    q3, k3, v3, b3, g3 = (t.reshape(sh) for t in (q, k, v, beta, g))
    gd = g3[:, :, :B_CHUNK] - jnp.swapaxes(g3, 1, 2)[:, :B_CHUNK, :]
    ii, jj = _iota(gd.shape, 1), _iota(gd.shape, 2)
    decay = jnp.exp(jnp.where(ii >= jj, gd, -jnp.inf))
    kb = k3 * b3
    a = _mm(kb, k3, tb=True) * jnp.where(ii > jj, decay, 0.0)
    tinv = _unit_lower_inverse(a)
    u = _mm(tinv, v3 * b3, hi=True)
    w = _mm(tinv, kb * jnp.exp(g3), hi=True)
    attn = _mm(q3, k3, tb=True) * decay
    return u.reshape(q.shape), w.reshape(q.shape), attn


def _wy_specs():
    hcol = pl.BlockSpec((_WY_ROWS, HD), lambda h, i: (i, h))
    hb = pl.BlockSpec((None, _WY_ROWS, HD), lambda h, i: (h, i, 0))
    at = pl.BlockSpec((None, _WY_CH, B_CHUNK, B_CHUNK), lambda h, i: (h, i, 0, 0))
    return hcol, hb, at


_WY_GRID = (B_HEADS, _NCH // _WY_CH)


def wy_fwd(q, k, v, beta, g):
    hcol, hb, at = _wy_specs()

    def body(q_ref, k_ref, v_ref, b_ref, g_ref, u_ref, w_ref, a_ref):
        u, w, a = _wy_fn(q_ref[...], k_ref[...], v_ref[...], b_ref[...], g_ref[...])
        u_ref[...] = u
        w_ref[...] = w
        a_ref[...] = a

    return pl.pallas_call(
        body, name="wy_fwd", grid=_WY_GRID, in_specs=[hcol, hcol, hcol, hb, hb], out_specs=[hcol, hcol, at],
        out_shape=[jax.ShapeDtypeStruct((T, B_WIDTH), F32)] * 2 + [jax.ShapeDtypeStruct((B_HEADS, _NCH, B_CHUNK, B_CHUNK), F32)],
        compiler_params=_cparams(("arbitrary", "arbitrary")),
    )(q, k, v, beta, g)


def wy_bwd(q, k, v, beta, g, du, dw, dattn):
    hcol, hb, at = _wy_specs()

    def body(q_ref, k_ref, v_ref, b_ref, g_ref, du_ref, dw_ref, da_ref, dq_ref, dk_ref, dv_ref, db_ref, dg_ref):
        _, vjp = jax.vjp(_wy_fn, q_ref[...], k_ref[...], v_ref[...], b_ref[...], g_ref[...])
        for r, t in zip((dq_ref, dk_ref, dv_ref, db_ref, dg_ref), vjp((du_ref[...], dw_ref[...], da_ref[...]))):
            r[...] = t

    return pl.pallas_call(
        body, name="wy_bwd", grid=_WY_GRID, in_specs=[hcol, hcol, hcol, hb, hb, hcol, hcol, at],
        out_specs=[hcol, hcol, hcol, hb, hb],
        out_shape=[jax.ShapeDtypeStruct((T, B_WIDTH), F32)] * 3 + [jax.ShapeDtypeStruct((B_HEADS, T, HD), F32)] * 2,
        compiler_params=_cparams(("arbitrary", "arbitrary")),
    )(q, k, v, beta, g, du, dw, dattn)


def _scan_step_fn(q, k, u, w, g, attn, gate, og, s):
    v_new = u - _mm(w, s)
    o = _mm(q * jnp.exp(g), s) + _mm(attn, v_new)
    g_last = jnp.sum(jnp.where(_iota(g.shape, 0) == B_CHUNK - 1, g, 0.0), axis=0, keepdims=True)
    s_new = s * jnp.exp(g_last) + _mm(k * jnp.exp(g_last - g), v_new, ta=True)
    return _rms(o, og) * _silu(gate), s_new


def _scan_specs(rev):
    ch = (lambda n: _NCH - 1 - n) if rev else (lambda n: n)
    rows = pl.BlockSpec((B_CHUNK, B_WIDTH), lambda n: (ch(n), 0))
    gb = pl.BlockSpec((B_HEADS, B_CHUNK, HD), lambda n: (0, ch(n), 0))
    at = pl.BlockSpec((B_HEADS, None, B_CHUNK, B_CHUNK), lambda n: (0, ch(n), 0, 0))
    og = pl.BlockSpec((1, HD), lambda n: (0, 0))
    st = pl.BlockSpec((None, B_HEADS, HD, HD), lambda n: (ch(n), 0, 0, 0))
    return rows, gb, at, og, st


def scan_fwd(q, k, u, w, g, attn, gate, og):
    rows, gb, at, ogs, st = _scan_specs(False)

    def body(q_ref, k_ref, u_ref, w_ref, g_ref, a_ref, gate_ref, og_ref, y_ref, st_ref, s_ref):
        @pl.when(pl.program_id(0) == 0)
        def _():
            s_ref[...] = jnp.zeros_like(s_ref)

        for h in range(B_HEADS):
            c = slice(h * HD, (h + 1) * HD)
            s = s_ref[h]
            st_ref[h] = s
            y, s_new = _scan_step_fn(q_ref[:, c], k_ref[:, c], u_ref[:, c], w_ref[:, c], g_ref[h], a_ref[h],
                                     gate_ref[:, c], og_ref[...], s)
            y_ref[:, c] = y.astype(BF16)
            s_ref[h] = s_new

    return pl.pallas_call(
        body, name="scan_fwd", grid=(_NCH,), in_specs=[rows, rows, rows, rows, gb, at, rows, ogs], out_specs=[rows, st],
        out_shape=[jax.ShapeDtypeStruct((T, B_WIDTH), BF16), jax.ShapeDtypeStruct((_NCH, B_HEADS, HD, HD), F32)],
        scratch_shapes=[pltpu.VMEM((B_HEADS, HD, HD), F32)], compiler_params=_cparams(("arbitrary",)),
    )(q, k, u, w, g, attn, gate, og)


def scan_bwd(q, k, u, w, g, attn, gate, og, states, dmix):
    rows, gb, at, ogs, st = _scan_specs(True)
    dyb = pl.BlockSpec((B_CHUNK, HD), lambda n: (_NCH - 1 - n, 0))

    def body(q_ref, k_ref, u_ref, w_ref, g_ref, a_ref, gate_ref, og_ref, st_ref, *rest):
        dy_refs, (dq_ref, dk_ref, du_ref, dw_ref, dgate_ref, dg_ref, da_ref, dog_ref, ds_ref) = rest[:B_HEADS], rest[B_HEADS:]

        @pl.when(pl.program_id(0) == 0)
        def _():
            ds_ref[...] = jnp.zeros_like(ds_ref)
            dog_ref[...] = jnp.zeros_like(dog_ref)

        for h in range(B_HEADS):
            c = slice(h * HD, (h + 1) * HD)
            _, vjp = jax.vjp(_scan_step_fn, q_ref[:, c], k_ref[:, c], u_ref[:, c], w_ref[:, c], g_ref[h], a_ref[h],
                             gate_ref[:, c], og_ref[...], st_ref[h])
            dq, dk, du, dw, dg, da, dgate, dog, ds = vjp((dy_refs[h][...], ds_ref[h]))
            dq_ref[:, c] = dq
            dk_ref[:, c] = dk
            du_ref[:, c] = du
            dw_ref[:, c] = dw
            dgate_ref[:, c] = dgate.astype(BF16)
            dg_ref[h] = dg
            da_ref[h] = da
            dog_ref[...] += dog
            ds_ref[h] = ds

    dy_specs = [pl.BlockSpec((B_CHUNK, HD), lambda n, h=h: (_NCH - 1 - n, A_WIDTH // HD + h)) for h in range(B_HEADS)]
    return pl.pallas_call(
        body, name="scan_bwd", grid=(_NCH,),
        in_specs=[rows, rows, rows, rows, gb, at, rows, ogs, st] + dy_specs,
        out_specs=[rows] * 5 + [gb, at, ogs],
        out_shape=[jax.ShapeDtypeStruct((T, B_WIDTH), F32)] * 4 + [jax.ShapeDtypeStruct((T, B_WIDTH), BF16)]
        + [jax.ShapeDtypeStruct((B_HEADS, T, HD), F32), jax.ShapeDtypeStruct((B_HEADS, _NCH, B_CHUNK, B_CHUNK), F32),
           jax.ShapeDtypeStruct((1, HD), F32)],
        scratch_shapes=[pltpu.VMEM((B_HEADS, HD, HD), F32)], compiler_params=_cparams(("arbitrary",)),
    )(q, k, u, w, g, attn, gate, og, states, *([dmix] * B_HEADS))


def _lanes(vec):
    return jnp.broadcast_to(vec[:, None, None], (vec.shape[0], 1, HD))


def gdn_forward(p2, conv_w, a_log, dt_bias, og):
    qa, ka, va = (conv_fwd(p2, conv_w, m) for m in "qkv")
    beta, g = gates_fwd(p2, _lanes(a_log), _lanes(dt_bias))
    u, w, attn = wy_fwd(qa, ka, va, beta, g)
    gate = p2[:, P_BG:P_BB]
    y, states = scan_fwd(qa, ka, u, w, g, attn, gate, og)
    return y, (qa, ka, va, beta, g, u, w, attn, gate, states)


def gdn_backward(p2, conv_w, a_log, dt_bias, og, saved, dmix):
    qa, ka, va, beta, g, u, w, attn, gate, states = saved
    dq1, dk1, du, dw, dgate, dg1, dattn, dog = scan_bwd(qa, ka, u, w, g, attn, gate, og, states, dmix)
    dq2, dk2, dv, dbeta, dg2 = wy_bwd(qa, ka, va, beta, g, du, dw, dattn)
    dbg, dal, ddtb = gates_bwd(p2, _lanes(a_log), _lanes(dt_bias), dbeta, dg1, dg2)
    dxq, dwq = conv_bwd(p2, conv_w, [dq1, dq2], "q")
    dxk, dwk = conv_bwd(p2, conv_w, [dk1, dk2], "k")
    dxv, dwv = conv_bwd(p2, conv_w, [dv], "v")
    return [dxq, dxk, dxv, dgate, dbg], jnp.concatenate([dwq, dwk, dwv], axis=1), dal[:, 0, 0], ddtb[:, 0, 0], dog


_SLOPES = np.exp2(-8.0 * (np.arange(C_HEADS, dtype=np.float64) + 1.0) / C_HEADS).astype(np.float32)


def _alibi_slopes():
    return _lanes(jnp.asarray(_SLOPES))


_ROWS = 512
_TM = 1024
_TM_FFN = 512


def _dep_specs(dep, ngrid):
    if dep is None:
        return [], []
    return [dep], [pl.BlockSpec((8, HD), lambda *_: (0, 0))]


def rmsnorm_fwd(x, g, dep=None):
    blk = pl.BlockSpec((_ROWS, D), lambda i: (i, 0))
    deps, dspecs = _dep_specs(dep, 1)

    def body(x_ref, g_ref, *rest):
        rest[-1][...] = _rms(x_ref[...], g_ref[...]).astype(BF16)

    return pl.pallas_call(
        body, name="rmsnorm_fwd", grid=(T // _ROWS,), in_specs=[blk, pl.BlockSpec((1, D), lambda i: (0, 0))] + dspecs,
        out_specs=blk, out_shape=jax.ShapeDtypeStruct((T, D), BF16), compiler_params=_cparams(("arbitrary",)),
    )(x, g, *deps)


def rmsnorm_bwd(x, g, dh, dres):
    blk = pl.BlockSpec((_ROWS, D), lambda i: (i, 0))
    row = pl.BlockSpec((1, D), lambda i: (0, 0))

    def body(x_ref, g_ref, dh_ref, dres_ref, dx_ref, dg_ref):
        _, vjp = jax.vjp(_rms, x_ref[...], g_ref[...])
        dx, dg = vjp(dh_ref[...])
        dx_ref[...] = dres_ref[...] + dx

        @pl.when(pl.program_id(0) == 0)
        def _():
            dg_ref[...] = jnp.zeros_like(dg_ref)

        dg_ref[...] += dg

    return pl.pallas_call(
        body, name="rmsnorm_bwd", grid=(T // _ROWS,), in_specs=[blk, row, blk, blk], out_specs=[blk, row],
        out_shape=[jax.ShapeDtypeStruct((T, D), F32), jax.ShapeDtypeStruct((1, D), F32)],
        compiler_params=_cparams(("arbitrary",)),
    )(x, g, dh, dres)


def _matmul(name, a, b, *, grid, a_spec, b_spec, o_spec, out_shape, ta=False, tb=False, k_axis=None, res=None, dep=None):
    dims = _dims(2, ta, tb)
    deps, dspecs = _dep_specs(dep, len(grid))

    def body(a_ref, b_ref, *rest):
        o_ref = rest[-1]
        prod = lax.dot_general(a_ref[...].astype(BF16), b_ref[...].astype(BF16), dims, preferred_element_type=F32)
        if res is not None:
            prod = prod + rest[0][...]
        if k_axis is None:
            o_ref[...] = prod.astype(o_ref.dtype)
        else:
            @pl.when(pl.program_id(k_axis) == 0)
            def _():
                o_ref[...] = prod

            @pl.when(pl.program_id(k_axis) > 0)
            def _():
                o_ref[...] += prod

    sem = tuple("arbitrary" for _ in grid)
    ins = [a, b] + ([res] if res is not None else []) + deps
    specs = [a_spec, b_spec] + ([o_spec] if res is not None else []) + dspecs
    return pl.pallas_call(
        body, name=name, grid=grid, in_specs=specs, out_specs=o_spec, out_shape=out_shape, compiler_params=_cparams(sem),
    )(*ins)


_IN_TN = P_END // 3


def mm_proj(h1, wp_in, l):
    return _matmul(
        "mm_proj", h1, wp_in, grid=(P_END // _IN_TN, T // _TM),
        a_spec=pl.BlockSpec((_TM, D), lambda j, i: (i, 0)),
        b_spec=pl.BlockSpec((None, D, _IN_TN), lambda j, i: (l, 0, j)),
        o_spec=pl.BlockSpec((_TM, _IN_TN), lambda j, i: (i, j)), out_shape=jax.ShapeDtypeStruct((T, P_END), F32))


def mm_dh1(dp2, wp_in, l, dep=None):
    return _matmul(
        "mm_dh1", dp2, wp_in, grid=(T // _TM, P_END // _IN_TN), tb=True, k_axis=1, dep=dep,
        a_spec=pl.BlockSpec((_TM, _IN_TN), lambda i, k: (i, k)),
        b_spec=pl.BlockSpec((None, D, _IN_TN), lambda i, k: (l, 0, k)),
        o_spec=pl.BlockSpec((_TM, D), lambda i, k: (i, 0)), out_shape=jax.ShapeDtypeStruct((T, D), F32))


def mm_dwin(h1, dp2):
    return _matmul(
        "mm_dwin", h1, dp2, grid=(P_END // _IN_TN, D // _TM), ta=True,
        a_spec=pl.BlockSpec((T, _TM), lambda j, i: (0, i)),
        b_spec=pl.BlockSpec((T, _IN_TN), lambda j, i: (0, j)),
        o_spec=pl.BlockSpec((_TM, _IN_TN), lambda j, i: (i, j)), out_shape=jax.ShapeDtypeStruct((D, P_END), BF16))


def _mm_square(name, a, w, l, res, tb, dep=None):
    tn = 1024
    b_spec = (pl.BlockSpec((None, tn, D), lambda j, i: (l, j, 0)) if tb else pl.BlockSpec((None, D, tn), lambda j, i: (l, 0, j)))
    return _matmul(
        name, a, w, grid=(D // tn, T // _TM), tb=tb, res=res, dep=dep,
        a_spec=pl.BlockSpec((_TM, D), lambda j, i: (i, 0)), b_spec=b_spec,
        o_spec=pl.BlockSpec((_TM, tn), lambda j, i: (i, j)), out_shape=jax.ShapeDtypeStruct((T, D), F32))


def mm_out(mix, wg_out, l, x):
    return _mm_square("mm_out", mix, wg_out, l, x, False)


def mm_dmix(dx1, wg_out, l, dep=None):
    return _mm_square("mm_dmix", dx1, wg_out, l, None, True, dep)


def mm_dwout(mix, dx1):
    tn = 1024
    return _matmul(
        "mm_dwout", mix, dx1, grid=(D // tn, D // _TM), ta=True,
        a_spec=pl.BlockSpec((T, _TM), lambda j, i: (0, i)), b_spec=pl.BlockSpec((T, tn), lambda j, i: (0, j)),
        o_spec=pl.BlockSpec((_TM, tn), lambda j, i: (i, j)), out_shape=jax.ShapeDtypeStruct((D, D), BF16))


_GU_TN = GU_SHARD // 2


_GU_NJ = FFN // _GU_TN


def mm_dh2(dgu, wg_gu, l, dep=None):
    return _matmul(
        "mm_dh2", dgu, wg_gu, grid=(T // _TM, 2 * N_CHIPS), tb=True, k_axis=1, dep=dep,
        a_spec=pl.BlockSpec((None, _TM, _GU_TN), lambda i, k: (k // _GU_NJ, i, k % _GU_NJ)),
        b_spec=pl.BlockSpec((None, None, D, _GU_TN), lambda i, k: (l, k // 2, 0, k % 2)),
        o_spec=pl.BlockSpec((_TM, D), lambda i, k: (i, 0)), out_shape=jax.ShapeDtypeStruct((T, D), F32))


def mm_dwgu(h2, dgu):
    return _matmul(
        "mm_dwgu", h2, dgu, grid=(N_CHIPS, 2, D // _TM), ta=True,
        a_spec=pl.BlockSpec((T, _TM), lambda s, j, i: (0, i)),
        b_spec=pl.BlockSpec((None, T, _GU_TN), lambda s, j, i: ((2 * s + j) // _GU_NJ, 0, (2 * s + j) % _GU_NJ)),
        o_spec=pl.BlockSpec((None, _TM, _GU_TN), lambda s, j, i: (s, i, j)),
        out_shape=jax.ShapeDtypeStruct((N_CHIPS, D, GU_SHARD), BF16))


def mm_down(act, wg_down, l, x1):
    tn = 512
    return _matmul(
        "mm_down", act, wg_down, grid=(D // tn, T // _TM), res=x1,
        a_spec=pl.BlockSpec((_TM, FFN), lambda j, i: (i, 0)),
        b_spec=pl.BlockSpec((None, FFN, tn), lambda j, i: (l, 0, j)),
        o_spec=pl.BlockSpec((_TM, tn), lambda j, i: (i, j)), out_shape=jax.ShapeDtypeStruct((T, D), F32))


def mm_dwdown(act, dx2):
    tm, tn = DOWN_SHARD, 512
    return _matmul(
        "mm_dwdown", act, dx2, grid=(D // tn, FFN // tm), ta=True,
        a_spec=pl.BlockSpec((T, tm), lambda j, i: (0, i)), b_spec=pl.BlockSpec((T, tn), lambda j, i: (0, j)),
        o_spec=pl.BlockSpec((tm, tn), lambda j, i: (i, j)), out_shape=jax.ShapeDtypeStruct((FFN, D), BF16))


_FF_TN = 1408


def _swiglu_fn(gt, up):
    return _silu(gt) * up


def _gate_up_specs():
    gate = pl.BlockSpec((None, None, D, _FF_TN), lambda j, i: (0, j // 2, 0, j % 2))
    up = pl.BlockSpec((None, None, D, _FF_TN), lambda j, i: (0, N_CHIPS // 2 + j // 2, 0, j % 2))
    both = pl.BlockSpec((2, _TM_FFN, _FF_TN), lambda j, i: (0, i, j))
    return gate, up, both


def mm_gu_swiglu(h2, wg_gu):
    gate, up, both = _gate_up_specs()

    def body(h_ref, wg_ref, wu_ref, gu_ref, act_ref):
        h = h_ref[...]
        gt = jnp.dot(h, wg_ref[...], preferred_element_type=F32)
        u = jnp.dot(h, wu_ref[...], preferred_element_type=F32)
        gu_ref[0] = gt.astype(BF16)
        gu_ref[1] = u.astype(BF16)
        act_ref[...] = _swiglu_fn(gt, u).astype(BF16)

    return pl.pallas_call(
        body, name="mm_gu_swiglu", grid=(FFN // _FF_TN, T // _TM_FFN),
        in_specs=[pl.BlockSpec((_TM_FFN, D), lambda j, i: (i, 0)), gate, up],
        out_specs=[both, pl.BlockSpec((_TM_FFN, _FF_TN), lambda j, i: (i, j))],
        out_shape=[jax.ShapeDtypeStruct((2, T, FFN), BF16), jax.ShapeDtypeStruct((T, FFN), BF16)],
        compiler_params=_cparams(("arbitrary", "arbitrary")),
    )(h2, wg_gu, wg_gu)


def mm_dact_swiglu(dx2, wg_down, gu, dep=None):
    _, _, both = _gate_up_specs()
    deps, dspecs = _dep_specs(dep, 2)

    def body(dx_ref, w_ref, gu_ref, *rest):
        dact = lax.dot_general(dx_ref[...].astype(BF16), w_ref[...], _dims(2, False, True), preferred_element_type=F32)
        _, vjp = jax.vjp(_swiglu_fn, gu_ref[0].astype(F32), gu_ref[1].astype(F32))
        dgt, dup = vjp(dact)
        rest[-1][0] = dgt.astype(BF16)
        rest[-1][1] = dup.astype(BF16)

    return pl.pallas_call(
        body, name="mm_dact_swiglu", grid=(FFN // _FF_TN, T // _TM_FFN),
        in_specs=[pl.BlockSpec((_TM_FFN, D), lambda j, i: (i, 0)), pl.BlockSpec((None, _FF_TN, D), lambda j, i: (0, j, 0)),
                  both]
        + dspecs,
        out_specs=both, out_shape=jax.ShapeDtypeStruct((2, T, FFN), BF16),
        compiler_params=_cparams(("arbitrary", "arbitrary")),
    )(dx2, wg_down, gu, *deps)


def loss_and_grad(y, target):
    blk = pl.BlockSpec((_ROWS, D), lambda i: (i, 0))
    acc = pl.BlockSpec((8, HD), lambda i: (0, 0))

    def body(y_ref, t_ref, dy_ref, l_ref):
        err = y_ref[...] - t_ref[...]
        dy_ref[...] = err * (1.0 / D)

        @pl.when(pl.program_id(0) == 0)
        def _():
            l_ref[...] = jnp.zeros_like(l_ref)

        l_ref[...] += (0.5 / D) * jnp.sum(err * err)

    return pl.pallas_call(
        body, name="loss_and_grad", grid=(T // _ROWS,), in_specs=[blk, blk], out_specs=[blk, acc],
        out_shape=[jax.ShapeDtypeStruct((T, D), F32), jax.ShapeDtypeStruct((8, HD), F32)],
        compiler_params=_cparams(("arbitrary",)),
    )(y, target)


def adamw(w, g, m, v, name):
    rows, cols = w.shape
    tr = _ROWS if rows % _ROWS == 0 else rows
    blk = pl.BlockSpec((tr, cols), lambda i: (i, 0))

    def body(w_ref, g_ref, m_ref, v_ref, d_ref, nm_ref, nv_ref):
        gg = g_ref[...]
        nm = ADAM_B1 * m_ref[...] + (1.0 - ADAM_B1) * gg
        nv = ADAM_B2 * v_ref[...] + (1.0 - ADAM_B2) * (gg * gg)
        m_hat = nm / (1.0 - ADAM_B1 ** ADAM_STEP)
        v_hat = nv / (1.0 - ADAM_B2 ** ADAM_STEP)
        d_ref[...] = -ADAM_LR * (m_hat / (jnp.sqrt(v_hat) + ADAM_EPS) + ADAM_WD * w_ref[...])
        nm_ref[...] = nm
        nv_ref[...] = nv

    return pl.pallas_call(
        body, name=name, grid=(rows // tr,), in_specs=[blk] * 4, out_specs=[blk] * 3,
        out_shape=[jax.ShapeDtypeStruct(w.shape, F32)] * 3, compiler_params=_cparams(("arbitrary",)),
    )(w, g, m, v)


_LANE = 128


def _segment_of_shard_column():
    flat = np.full(P_END, -1, np.int64)
    for o in range(P_END):
        if GATE_COLS <= o < P_CQ:
            continue
        c = o if o < GATE_COLS else o - (P_CQ - GATE_COLS)
        flat[o] = (c // IN_SHARD) * IN_SHARD_PAD + c % IN_SHARD
    return flat


def _block_pairs(src_of_dst):
    return [sorted({int(c) // _LANE for c in src_of_dst[db * _LANE:(db + 1) * _LANE] if c >= 0})
            for db in range(len(src_of_dst) // _LANE)]


_RELAYOUT_ROWS = 512
_SHARD_BLOCKS = IN_SHARD_PAD // _LANE


def _relayout(name, x, to_segments):
    seg_of = _segment_of_shard_column()
    if to_segments:
        src_of_dst = seg_of
    else:
        src_of_dst = np.full(N_CHIPS * IN_SHARD_PAD, -1, np.int64)
        src_of_dst[seg_of[seg_of >= 0]] = np.nonzero(seg_of >= 0)[0]
    sources = _block_pairs(src_of_dst)
    n_dst = len(sources)
    col_map = jnp.asarray(src_of_dst.reshape(n_dst, 1, _LANE), jnp.int32)
    shard_blk = pl.BlockSpec((N_CHIPS, _RELAYOUT_ROWS, IN_SHARD_PAD), lambda i: (0, i, 0))
    seg_blk = pl.BlockSpec((_RELAYOUT_ROWS, P_END), lambda i: (i, 0))

    def shard_cols(ref, b):
        return ref.at[b // _SHARD_BLOCKS, :, pl.ds((b % _SHARD_BLOCKS) * _LANE, _LANE)]

    def seg_cols(ref, b):
        return ref.at[:, pl.ds(b * _LANE, _LANE)]

    src_cols, dst_cols = (shard_cols, seg_cols) if to_segments else (seg_cols, shard_cols)

    def body(x_ref, map_ref, o_ref):
        src_row = _iota((_LANE, _LANE), 0)
        for d in range(n_dst):
            acc = jnp.zeros((_RELAYOUT_ROWS, _LANE), F32)
            for sb in sources[d]:
                sel = (src_row + sb * _LANE == map_ref[d]).astype(x_ref.dtype)
                acc = acc + jnp.dot(src_cols(x_ref, sb)[...], sel, preferred_element_type=F32)
            dst_cols(o_ref, d)[...] = acc.astype(o_ref.dtype)

    rows = x.shape[-2]
    out_shape = (rows, P_END) if to_segments else (N_CHIPS, rows, IN_SHARD_PAD)
    return pl.pallas_call(
        body, name=name, grid=(rows // _RELAYOUT_ROWS,),
        in_specs=[shard_blk if to_segments else seg_blk, pl.BlockSpec(col_map.shape, lambda i: (0, 0, 0))],
        out_specs=seg_blk if to_segments else shard_blk, out_shape=jax.ShapeDtypeStruct(out_shape, x.dtype),
        compiler_params=_cparams(("arbitrary",)),
    )(x, col_map)


def shards_to_segments(w):
    return _relayout("shards_to_segments", w, True)


def segments_to_shards(w):
    return _relayout("segments_to_shards", w, False)


def mixers_forward(x, w_in, sp, dep=None, h1=None):
    if h1 is None:
        h1 = rmsnorm_fwd(x, sp["norm1_g"], dep)
    p2 = mm_proj(h1, w_in, 0)
    y_a = sgu_fwd(p2, sp["sgu_norm_g"], sp["w_spatial"], sp["b_spatial"])
    y_b, saved_b = gdn_forward(p2, sp["conv_w"], sp["a_log"], sp["dt_bias"], sp["o_norm_g"])
    y_c, outs_c, lses_c = dattn_fwd(p2, sp["q_norm_g"], sp["k_norm_g"], _alibi_slopes())
    mix = jnp.concatenate([y_a, y_b, y_c], axis=1)
    return mix, (x, h1, p2, saved_b, (outs_c, lses_c), mix)


def ffn_up(x, mix, w_out, w_gu, sp):
    x1 = mm_out(mix, w_out, 0, x)
    h2 = rmsnorm_fwd(x1, sp["norm2_g"])
    gu, act = mm_gu_swiglu(h2, w_gu)
    return x1, h2, gu, act


def ffn_forward(x, mix, wg, sp):
    x1, h2, gu, act = ffn_up(x, mix, wg["out"], wg["gu"], sp)
    x2 = mm_down(act, wg["down"], 0, x1)
    return x2, (x1, h2, gu, act)


def ffn_backward(dx2, wg, sp, saved, dep=None, on_weight_grads=None):
    x1, h2, gu, act = saved
    dgu = mm_dact_swiglu(dx2, wg["down"], gu, dep)
    dw_down = mm_dwdown(act, dx2)
    dw_gu = mm_dwgu(h2, dgu)
    tok = None if on_weight_grads is None else on_weight_grads(dw_gu, dw_down)
    dh2 = mm_dh2(dgu, wg["gu"], 0, tok)
    dx1, dnorm2 = rmsnorm_bwd(x1, sp["norm2_g"], dh2, dx2)
    return dx1, dnorm2, dw_gu, dw_down


def mixers_backward(dx1, wg, sp, saved, dep=None, on_weight_grads=None):
    x, h1, p2, saved_b, saved_c, mix = saved
    dmix = mm_dmix(dx1, wg["out"], 0, dep)
    dw_out = mm_dwout(mix, dx1)
    du, dv, dsg, dws, dbs = sgu_bwd(p2, sp["sgu_norm_g"], sp["w_spatial"], sp["b_spatial"], dmix)
    dseg_b, dconv, dal, ddtb, dog = gdn_backward(p2, sp["conv_w"], sp["a_log"], sp["dt_bias"], sp["o_norm_g"], saved_b, dmix)
    dcq, dck, dcv, dqg, dkg = dattn_bwd(p2, sp["q_norm_g"], sp["k_norm_g"], _alibi_slopes(), *saved_c, dmix)
    dp2 = jnp.concatenate([du, dv] + dseg_b + [dcq, dck, dcv], axis=1)
    dw_in = segments_to_shards(mm_dwin(h1, dp2))
    tok = None if on_weight_grads is None else on_weight_grads(dw_in, dw_out)
    dh1 = mm_dh1(dp2, wg["in"], 0, tok)
    dx, dnorm1 = rmsnorm_bwd(x, sp["norm1_g"], dh1, dx1)
    small = {"norm1_g": dnorm1, "sgu_norm_g": dsg, "w_spatial": dws, "b_spatial": dbs, "conv_w": dconv, "a_log": dal,
             "dt_bias": ddtb, "o_norm_g": dog, "q_norm_g": dqg, "k_norm_g": dkg}
    return dx, dw_in, dw_out, small


_HBM = pl.BlockSpec(memory_space=pltpu.HBM)
_MESH = pl.DeviceIdType.MESH


def _place():
    x, y, c = lax.axis_index("x"), lax.axis_index("y"), lax.axis_index("c")
    chips = [(1 - x, y), (x, 1 - y), (1 - x, 1 - y)]
    return x, y, c, chips


def _rcopy(src, dst, ssem, rsem, dev):
    return pltpu.make_async_remote_copy(src_ref=src, dst_ref=dst, send_sem=ssem, recv_sem=rsem, device_id=dev,
                                        device_id_type=_MESH)


_SEM = pl.BlockSpec(memory_space=pltpu.SEMAPHORE)
_SIDE_EFFECT = pltpu.SideEffectType.DATAFLOW_SIDE_EFFECTING


def _in_hbm(a):
    return pltpu.with_memory_space_constraint(a, pltpu.HBM)


def _split_copy(name, srcs, land_shapes, n_sems, copies):
    n, m = len(srcs), len(land_shapes)
    thru = [pltpu.HBM(a.shape, a.dtype) for a in srcs] + [pltpu.HBM(s.shape, s.dtype) for s in land_shapes]
    sems = (pltpu.SemaphoreType.DMA((n_sems,)), pltpu.SemaphoreType.DMA((n_sems,)))

    def start(dep=None):
        deps = [] if dep is None else [dep]

        def body(*refs):
            ins, lands = refs[:n], refs[n:n + m]
            ssem, rsem, token = refs[n + m + len(deps)], refs[n + m + len(deps) + 1], refs[-1]
            for cp in copies(ins, lands, ssem, rsem)[0]:
                cp.start()
            token[...] = jnp.zeros_like(token)

        out = pl.pallas_call(
            body, name=name + "_start", out_shape=(*sems, *thru, jax.ShapeDtypeStruct((8, HD), F32)),
            in_specs=[_HBM] * (n + m) + [pl.BlockSpec(memory_space=pl.ANY)] * len(deps),
            out_specs=(_SEM, _SEM, *[_HBM] * (n + m), pl.BlockSpec(memory_space=pltpu.VMEM)),
            input_output_aliases={i: 2 + i for i in range(n + m)},
            compiler_params=pltpu.CompilerParams(has_side_effects=_SIDE_EFFECT),
        )(*[_in_hbm(a) for a in srcs], *[_in_hbm(lax.empty(s.shape, s.dtype)) for s in land_shapes], *deps)
        return out[:-1], out[-1]

    def wait(state, after):
        def body(*refs):
            ins, lands, ssem, rsem = refs[:n], refs[n:n + m], refs[n + m], refs[n + m + 1]
            sent, arrivals = copies(ins, lands, ssem, rsem)
            for cp in sent:
                cp.wait_send()
            for cp in arrivals:
                cp.wait_recv()

        out = pl.pallas_call(
            body, name=name + "_wait", out_shape=tuple(thru),
            in_specs=[_HBM] * (n + m) + [_SEM, _SEM, pl.BlockSpec(memory_space=pl.ANY)], out_specs=[_HBM] * (n + m),
            input_output_aliases={i: i for i in range(n + m)},
            compiler_params=pltpu.CompilerParams(has_side_effects=_SIDE_EFFECT),
        )(*state[2:], state[0], state[1], after)
        return list(out[:n]), list(out[n:])

    return start, wait


def gather_direct(shards, tag):
    n = len(shards)

    def copies(ins, lands, ssem, rsem):
        x, y, c, chips = _place()
        s = 2 * x + y
        sibling = (x, y, 1 - c)
        sent, arrivals = [], []
        for a in range(n):
            for u in range(2):
                cp = _rcopy(ins[a].at[u], lands[a].at[s, u], ssem.at[5 * a + u], rsem.at[5 * a + u], sibling)
                sent.append(cp)
                arrivals.append(cp)
            for j, (cx, cy) in enumerate(chips):
                k = 5 * a + 2 + j
                sent.append(_rcopy(ins[a].at[c], lands[a].at[s, c], ssem.at[k], rsem.at[k], (cx, cy, c)))
                arrivals.append(_rcopy(ins[a].at[c], lands[a].at[2 * cx + cy, c], ssem.at[k], rsem.at[k], (cx, cy, c)))
        return sent, arrivals

    lands = [jax.ShapeDtypeStruct((N_CHIPS,) + w.shape, w.dtype) for w in shards]
    return _split_copy("gather_direct_" + tag, shards, lands, 5 * n, copies)


def pass_to_sibling(lands):
    n = len(lands)

    def body(*refs):
        ins = refs[:n]
        ssem, rsem = refs[2 * n:]
        x, y, c, chips = _place()
        sibling = (x, y, 1 - c)
        cps, arrivals = [], []
        for a in range(n):
            for j, (cx, cy) in enumerate(chips):
                t = 2 * cx + cy
                cps.append(_rcopy(ins[a].at[t, c], ins[a].at[t, c], ssem.at[a, j], rsem.at[a, j], sibling))
                arrivals.append(_rcopy(ins[a].at[t, c], ins[a].at[t, 1 - c], ssem.at[a, j], rsem.at[a, j], sibling))
        for cp in cps:
            cp.start()
        for cp, ar in zip(cps, arrivals):
            cp.wait_send()
            ar.wait_recv()

    return pl.pallas_call(
        body, name="pass_to_sibling", in_specs=[_HBM] * n, out_specs=[_HBM] * n,
        out_shape=[jax.ShapeDtypeStruct(a.shape, a.dtype) for a in lands], input_output_aliases={a: a for a in range(n)},
        scratch_shapes=[pltpu.SemaphoreType.DMA((n, 3)), pltpu.SemaphoreType.DMA((n, 3))],
    )(*lands)


def exchange_halves(grads, tag):
    n = len(grads)

    def copies(ins, lands, ssem, rsem):
        x, y, c, _ = _place()
        cps = []
        for a in range(n):
            h = grads[a].shape[1] // 2
            cps.append(_rcopy(ins[a].at[:, pl.ds((1 - c) * h, h)], lands[a], ssem.at[a], rsem.at[a], (x, y, 1 - c)))
        return cps, cps

    lands = [jax.ShapeDtypeStruct((g.shape[0], g.shape[1] // 2, g.shape[2]), g.dtype) for g in grads]
    return _split_copy("exchange_halves_" + tag, grads, lands, n, copies)


def scatter_direct(parts, tag):
    n = len(parts)

    def copies(ins, lands, ssem, rsem):
        x, y, c, chips = _place()
        cps = [_rcopy(ins[a].at[2 * cx + cy], lands[a].at[j], ssem.at[3 * a + j], rsem.at[3 * a + j], (cx, cy, c))
               for a in range(n) for j, (cx, cy) in enumerate(chips)]
        return cps, cps

    lands = [jax.ShapeDtypeStruct((3,) + p.shape[1:], p.dtype) for p in parts]
    return _split_copy("scatter_direct_" + tag, parts, lands, 3 * n, copies)


def share_halves(halves):
    n = len(halves)

    def body(*refs):
        ins, outs = refs[:n], refs[n:2 * n]
        ssem, rsem = refs[2 * n:]
        x, y, c, _ = _place()
        cps = [_rcopy(ins[i], outs[i], ssem.at[i], rsem.at[i], (x, y, 1 - c)) for i in range(n)]
        for cp in cps:
            cp.start()
        for cp in cps:
            cp.wait()

    return pl.pallas_call(
        body, name="share_halves", in_specs=[_HBM] * n, out_specs=[_HBM] * n,
        out_shape=[jax.ShapeDtypeStruct(h.shape, h.dtype) for h in halves],
        scratch_shapes=[pltpu.SemaphoreType.DMA((n,)), pltpu.SemaphoreType.DMA((n,))],
    )(*halves)


def share_direct(halves, tag):
    n = len(halves)

    def copies(ins, lands, ssem, rsem):
        x, y, c, _ = _place()
        cps = [_rcopy(ins[i], lands[i], ssem.at[i], rsem.at[i], (x, y, 1 - c)) for i in range(n)]
        return cps, cps

    return _split_copy("share_halves_" + tag, halves, [jax.ShapeDtypeStruct(h.shape, h.dtype) for h in halves], n, copies)


_ADAMW_BLOCK_BYTES = 3 << 19


def adamw_shard(w, m, v, mine, theirs, c, name, dep=None):
    _, r, cw = w.shape
    h, cg = mine[0].shape
    tr = next(t for t in (256, 176, 128) if h % t == 0 and t * cg * 4 <= _ADAMW_BLOCK_BYTES)
    nb = h // tr
    wblk = pl.BlockSpec((None, tr, cw), lambda l, i, c_ref: (l, i, 0))
    gblk = lambda layer, own: pl.BlockSpec((tr, cg), lambda l, i, c_ref: (_held_block(l, i, c_ref, layer, own, nb), 0))
    return _adamw_halves(w, m, v, mine, theirs, c, name, (DEPTH, r // tr), wblk, gblk, nb, cw, dep)


def _held_block(l, i, c_ref, layer, own, nb):
    in_use = (l == layer) & (((i // nb) == c_ref[0]) == own)
    return jnp.where(in_use, i % nb, 0)


def _adamw_halves(w, m, v, mine, theirs, c, name, grid, wblk, gblk, nb, cw, dep=None):
    deps, dspecs = _dep_specs(dep, 2)

    def body(c_ref, w_ref, m_ref, v_ref, m0, m1, t0, t1, *rest):
        g_ref, d_ref, nm_ref, nv_ref = rest[-4:]
        is_mine = (pl.program_id(1) // nb) == c_ref[0]
        first = pl.program_id(0) == 0
        gg = jnp.where(is_mine, jnp.where(first, m0[:, :cw], m1[:, :cw]), jnp.where(first, t0[:, :cw], t1[:, :cw]))
        nm = ADAM_B1 * m_ref[...] + (1.0 - ADAM_B1) * gg
        nv = ADAM_B2 * v_ref[...] + (1.0 - ADAM_B2) * (gg * gg)
        m_hat = nm / (1.0 - ADAM_B1 ** ADAM_STEP)
        v_hat = nv / (1.0 - ADAM_B2 ** ADAM_STEP)
        g_ref[...] = gg
        d_ref[...] = -ADAM_LR * (m_hat / (jnp.sqrt(v_hat) + ADAM_EPS) + ADAM_WD * w_ref[...])
        nm_ref[...] = nm
        nv_ref[...] = nv

    return pl.pallas_call(
        body, name=name,
        grid_spec=pltpu.PrefetchScalarGridSpec(
            num_scalar_prefetch=1, grid=grid,
            in_specs=[wblk] * 3 + [gblk(0, True), gblk(1, True), gblk(0, False), gblk(1, False)] + dspecs,
            out_specs=[wblk] * 4),
        out_shape=[jax.ShapeDtypeStruct(w.shape, F32)] * 4, compiler_params=_cparams(("arbitrary", "arbitrary")),
    )(c, w, m, v, mine[0], mine[1], theirs[0], theirs[1], *deps)


def adamw_shard_t(wt, mt, vt, mine_t, theirs_t, c, name):
    _, cw, r = wt.shape
    h = mine_t[0].shape[1]
    tc = 256
    nb = h // tc
    wblk = pl.BlockSpec((None, cw, tc), lambda l, j, c_ref: (l, 0, j))
    gblk = lambda layer, own: pl.BlockSpec((cw, tc), lambda l, j, c_ref: (0, _held_block(l, j, c_ref, layer, own, nb)))
    return _adamw_halves(wt, mt, vt, mine_t, theirs_t, c, name, (DEPTH, r // tc), wblk, gblk, nb, cw)


def _half_rows(h, cols):
    for tr in (512, 256, 352, 128, 64):
        if h % tr == 0 and tr * cols * 4 <= 6 * 1024 * 1024:
            return tr
    raise ValueError((h, cols))


def add_sibling(grad, recv, c):
    _, r, cols = grad.shape
    h = r // 2
    tr = _half_rows(h, cols)
    nb = h // tr

    def body(c_ref, g_ref, r_ref, o_ref):
        o_ref[...] = (g_ref[...].astype(F32) + r_ref[...].astype(F32)).astype(BF16)

    return pl.pallas_call(
        body, name="add_sibling",
        grid_spec=pltpu.PrefetchScalarGridSpec(
            num_scalar_prefetch=1, grid=(N_CHIPS, nb),
            in_specs=[pl.BlockSpec((None, tr, cols), lambda t, i, c_ref: (t, c_ref[0] * nb + i, 0)),
                      pl.BlockSpec((None, tr, cols), lambda t, i, c_ref: (t, i, 0))],
            out_specs=pl.BlockSpec((None, tr, cols), lambda t, i, c_ref: (t, i, 0))),
        out_shape=jax.ShapeDtypeStruct((N_CHIPS, h, cols), BF16), compiler_params=_cparams(("arbitrary", "arbitrary")),
    )(c, grad, recv)


def add_chips(part, recv, s):
    _, h, cols = part.shape
    tr = _half_rows(h, cols)

    def body(s_ref, p_ref, r_ref, o_ref):
        o_ref[...] = ((p_ref[...].astype(F32) + r_ref[0].astype(F32)) + r_ref[1].astype(F32)) + r_ref[2].astype(F32)

    return pl.pallas_call(
        body, name="add_chips",
        grid_spec=pltpu.PrefetchScalarGridSpec(
            num_scalar_prefetch=1, grid=(h // tr,),
            in_specs=[pl.BlockSpec((None, tr, cols), lambda i, s_ref: (s_ref[0], i, 0)),
                      pl.BlockSpec((3, tr, cols), lambda i, s_ref: (0, i, 0))],
            out_specs=pl.BlockSpec((tr, cols), lambda i, s_ref: (i, 0))),
        out_shape=jax.ShapeDtypeStruct((h, cols), F32), compiler_params=_cparams(("arbitrary",)),
    )(s, part, recv)


def allreduce_small(vec):
    rows = vec.shape[0]

    def body(v_ref, o_ref, buf, ssem, rsem, lsem):
        x, y, c, chips = _place()
        me, sibling = (x, y, c), (x, y, 1 - c)

        def blk(px, py, pc):
            return buf.at[4 * px + 2 * py + pc]

        def copy(k, block, to, src=None):
            return _rcopy(blk(*block) if src is None else src, blk(*block), ssem.at[k], rsem.at[k], to)

        mine = pltpu.make_async_copy(v_ref, blk(*me), lsem)
        mine.start()
        first = [copy(0, me, sibling, src=v_ref)] + [copy(1 + j, me, (*chip, c), src=v_ref) for j, chip in enumerate(chips)]
        for cp in first:
            cp.start()
        passed = [copy(4 + j, (*chip, c), sibling) for j, chip in enumerate(chips)]
        for j, chip in enumerate(chips):
            copy(1 + j, (*chip, c), me).wait_recv()
            passed[j].start()
        copy(0, sibling, me).wait_recv()
        for j, chip in enumerate(chips):
            copy(4 + j, (*chip, 1 - c), me).wait_recv()
        for cp in first + passed:
            cp.wait_send()
        mine.wait()
        acc = buf[0]
        for d in range(1, N_DEV):
            acc = acc + buf[d]
        o_ref[...] = acc

    vm = pl.BlockSpec(memory_space=pltpu.VMEM)
    return pl.pallas_call(
        body, name="allreduce_small", in_specs=[vm], out_specs=vm, out_shape=jax.ShapeDtypeStruct(vec.shape, F32),
        scratch_shapes=[pltpu.VMEM((N_DEV, rows, HD), F32), pltpu.SemaphoreType.DMA((7,)), pltpu.SemaphoreType.DMA((7,)),
                        pltpu.SemaphoreType.DMA],
        compiler_params=pltpu.CompilerParams(vmem_limit_bytes=VMEM_LIMIT),
    )(vec)


SMALL_NAMES = ("norm1_g", "sgu_norm_g", "w_spatial", "b_spatial", "conv_w", "a_log", "dt_bias", "o_norm_g", "q_norm_g",
               "k_norm_g", "norm2_g")


def small_params(l, p, conv_full):
    return {"norm1_g": p["norm1_g"][l][None], "sgu_norm_g": p["sgu_norm_g"][l][:, None, :], "w_spatial": p["w_spatial"][l],
            "b_spatial": p["b_spatial"][l][..., None], "conv_w": conv_full[l], "a_log": p["a_log"][l], "dt_bias": p["dt_bias"][l],
            "o_norm_g": p["o_norm_g"][l][None], "q_norm_g": p["q_norm_g"][l][None], "k_norm_g": p["k_norm_g"][l][None],
            "norm2_g": p["norm2_g"][l][None]}


_PACK_TILE = 8 * HD


def _pack(arrays):
    flat = jnp.concatenate([a.reshape(-1) for a in arrays])
    pad = -flat.shape[0] % _PACK_TILE
    return jnp.pad(flat, (0, pad)).reshape(-1, HD)


def _unpack(packed, shapes):
    flat, out, off = packed.reshape(-1), [], 0
    for shp in shapes:
        n = int(np.prod(shp))
        out.append(flat[off:off + n].reshape(shp))
        off += n
    return out


WEIGHT_ORDER = ("norm1_g", "w_in", "sgu_norm_g", "w_spatial", "b_spatial", "conv_w", "a_log", "dt_bias", "o_norm_g", "q_norm_g",
                "k_norm_g", "w_out", "norm2_g", "w_gate_up", "w_down")


def kernel(x, norm1_g, w_in, sgu_norm_g, w_spatial, b_spatial, conv_w, a_log, dt_bias, o_norm_g, q_norm_g, k_norm_g, w_out, norm2_g, w_gate_up, w_down, loss_target, m_norm1_g, m_w_in, m_sgu_norm_g, m_w_spatial, m_b_spatial, m_conv_w, m_a_log, m_dt_bias, m_o_norm_g, m_q_norm_g, m_k_norm_g, m_w_out, m_norm2_g, m_w_gate_up, m_w_down, v_norm1_g, v_w_in, v_sgu_norm_g, v_w_spatial, v_b_spatial, v_conv_w, v_a_log, v_dt_bias, v_o_norm_g, v_q_norm_g, v_k_norm_g, v_w_out, v_norm2_g, v_w_gate_up, v_w_down):
    w = dict(norm1_g=norm1_g, w_in=w_in, sgu_norm_g=sgu_norm_g, w_spatial=w_spatial, b_spatial=b_spatial, conv_w=conv_w,
             a_log=a_log, dt_bias=dt_bias, o_norm_g=o_norm_g, q_norm_g=q_norm_g, k_norm_g=k_norm_g, w_out=w_out,
             norm2_g=norm2_g, w_gate_up=w_gate_up, w_down=w_down)
    m = dict(norm1_g=m_norm1_g, w_in=m_w_in, sgu_norm_g=m_sgu_norm_g, w_spatial=m_w_spatial, b_spatial=m_b_spatial,
             conv_w=m_conv_w, a_log=m_a_log, dt_bias=m_dt_bias, o_norm_g=m_o_norm_g, q_norm_g=m_q_norm_g, k_norm_g=m_k_norm_g,
             w_out=m_w_out, norm2_g=m_norm2_g, w_gate_up=m_w_gate_up, w_down=m_w_down)
    v = dict(norm1_g=v_norm1_g, w_in=v_w_in, sgu_norm_g=v_sgu_norm_g, w_spatial=v_w_spatial, b_spatial=v_b_spatial,
             conv_w=v_conv_w, a_log=v_a_log, dt_bias=v_dt_bias, o_norm_g=v_o_norm_g, q_norm_g=v_q_norm_g, k_norm_g=v_k_norm_g,
             w_out=v_w_out, norm2_g=v_norm2_g, w_gate_up=v_w_gate_up, w_down=v_w_down)
    chip = (2 * lax.axis_index("x") + lax.axis_index("y")).astype(jnp.int32)
    core = lax.axis_index("c").astype(jnp.int32)

    in_pad = IN_SHARD_PAD - IN_SHARD
    w_in_pad = jnp.pad(w_in, ((0, 0), (0, 0), (0, in_pad)))

    halves_of = lambda a: a.reshape(2, a.shape[0] // 2, a.shape[1])
    start_0, wait_0 = gather_direct([halves_of(w_in_pad[0].astype(BF16)), halves_of(conv_w[0])], "mix0")
    state_0, token_0 = start_0()
    bf_halves = lambda a: halves_of((a + token_0[0, 0]).astype(BF16))

    def ffn_shards(l):
        return [bf_halves(w_gate_up[l]), bf_halves(w_down[l]), bf_halves(w_out[l])]

    def mixer_shards(l):
        return [bf_halves(w_in_pad[l]), halves_of(conv_w[l])]

    def mixer_weights(g):
        g_in, g_conv = g
        return (shards_to_segments(g_in.reshape(N_CHIPS, D, IN_SHARD_PAD))[None],
                g_conv.reshape(N_CHIPS, B_CONV, -1).transpose(1, 0, 2).reshape(B_CONV, 3 * B_WIDTH))

    def ffn_weights(g, w_in_seg):
        g_gu, g_down, g_out = g
        return {"in": w_in_seg, "out": g_out.reshape(1, D, D), "gu": g_gu.reshape(1, N_CHIPS, D, GU_SHARD),
                "down": None if g_down is None else g_down.reshape(1, FFN, D)}

    def layer_params(l, conv_full):
        return small_params(0, {n: w[n][l:l + 1] for n in SMALL_NAMES if n != "conv_w"}, conv_full[None])

    gu0, down0, out0 = ffn_shards(0)
    start_a, wait_a = gather_direct([gu0, out0], "ffn0")
    start_b, wait_b = gather_direct([down0] + mixer_shards(1), "mid")
    start_c, wait_c = gather_direct(ffn_shards(1), "ffn1")
    state_a, token_a = start_a(token_0)
    state_b, token_b = start_b(token_a)
    state_c, token_c = start_c(token_b)
    h1_0 = rmsnorm_fwd(x[0], norm1_g[0][None], token_c)
    w_in0, conv0 = mixer_weights(pass_to_sibling(wait_0(state_0, h1_0)[1]))
    sps = [layer_params(0, conv0), None]
    mix0, saved_m0 = mixers_forward(x[0], w_in0, sps[0], h1=h1_0)
    g_gu0, g_out0 = pass_to_sibling(wait_a(state_a, mix0)[1])
    wg0 = ffn_weights((g_gu0, None, g_out0), w_in0)
    x1_0, h2_0, gu_0, act_0 = ffn_up(x[0], mix0, wg0["out"], wg0["gu"], sps[0])
    g_down0, g_in1, g_conv1 = pass_to_sibling(wait_b(state_b, act_0)[1])
    wg0["down"] = g_down0.reshape(1, FFN, D)
    x1 = mm_down(act_0, wg0["down"], 0, x1_0)
    saved_f0 = (x1_0, h2_0, gu_0, act_0)
    w_in1, conv1 = mixer_weights((g_in1, g_conv1))
    sps[1] = layer_params(1, conv1)
    mix1, saved_m1 = mixers_forward(x1, w_in1, sps[1])
    wg1 = ffn_weights(pass_to_sibling(wait_c(state_c, mix1)[1]), w_in1)
    x2, saved_f1 = ffn_forward(x1, mix1, wg1, sps[1])
    saved1 = (saved_m1, saved_f1)
    dx, loss_tile = loss_and_grad(x2, loss_target[0])

    def sibling_exchange(tag, by_chip):
        cell = {}

        def on_weight_grads(*dws):
            start, wait = exchange_halves([f(t) for f, t in zip(by_chip, dws)], tag)
            state, token = start()
            cell["wait"] = functools.partial(wait, state)
            return token

        return on_weight_grads, cell

    def start_scatter(cell, after, tag, dep=None):
        grads, from_sibling = cell["wait"](after)
        parts = [add_sibling(g, r, core.reshape(1)) for g, r in zip(grads, from_sibling)]
        start, wait = scatter_direct(parts, tag)
        state, token = start(dep)
        return functools.partial(wait, state), token

    smalls = [None] * DEPTH
    same = lambda t: t
    ffn_by_chip = (same, lambda t: t.reshape(N_CHIPS, DOWN_SHARD, D))
    mix_by_chip = (same, lambda t: t.reshape(N_CHIPS, OUT_SHARD, D))
    hook, cell_f1 = sibling_exchange("ffn1", ffn_by_chip)
    dx1, dnorm2_1, _, _ = ffn_backward(dx, wg1, sps[1], saved1[1], on_weight_grads=hook)
    wait_f1, tok_f1 = start_scatter(cell_f1, dx1, "ffn1")
    hook, cell_m1 = sibling_exchange("mix1", mix_by_chip)
    dx, _, _, small1 = mixers_backward(dx1, wg1, sps[1], saved1[0], dep=tok_f1, on_weight_grads=hook)
    smalls[1] = {**small1, "norm2_g": dnorm2_1}
    wait_m1, tok_m1 = start_scatter(cell_m1, dx, "mix1")
    hook, cell_f0 = sibling_exchange("ffn0", ffn_by_chip)
    dx1, dnorm2_0, _, _ = ffn_backward(dx, wg0, sps[0], saved_f0, dep=tok_m1, on_weight_grads=hook)
    wait_f0, tok_f0 = start_scatter(cell_f0, dx1, "ffn0")
    hook, cell_m0 = sibling_exchange("mix0", mix_by_chip)
    dx, _, _, small0 = mixers_backward(dx1, wg0, sps[0], saved_m0, dep=tok_f0, on_weight_grads=hook)
    smalls[0] = {**small0, "norm2_g": dnorm2_0}

    grad, delta, new_m, new_v = {}, {}, {}, {}
    stacked = [jnp.stack([smalls[l][n] for l in range(DEPTH)]) for n in SMALL_NAMES]
    total = allreduce_small(_pack(stacked + [loss_tile[0, :1]]))
    shapes = [(DEPTH, B_CONV, 3 * B_WIDTH) if n == "conv_w" else w[n].shape for n in SMALL_NAMES]
    small_grads = dict(zip(SMALL_NAMES, _unpack(total, shapes + [(1,)])[:-1]))
    loss = _unpack(total, shapes + [(1,)])[-1][0]
    conv_cols = conv_w.shape[-1]
    small_grads["conv_w"] = lax.dynamic_slice_in_dim(small_grads["conv_w"], chip * conv_cols, conv_cols, axis=2)
    grad.update(small_grads)
    sshapes = [w[n].shape for n in SMALL_NAMES]
    packed = [_pack([d[n] for n in SMALL_NAMES]) for d in (w, grad, m, v)]
    for dst, t in zip((delta, new_m, new_v), adamw(*packed, "adamw_small")):
        dst.update(zip(SMALL_NAMES, _unpack(t, sshapes)))

    wait_m0, tok_m0 = start_scatter(cell_m0, dx, "mix0", dep=total)
    (pf0, rf0), (pm1, rm1), (pf1, rf1) = (wt(tok_m0) for wt in (wait_f0, wait_m1, wait_f1))

    def reduce_group(parts, from_chips):
        mine = [add_chips(p, r, chip.reshape(1)) for p, r in zip(parts, from_chips)]
        return mine, list(share_halves(mine))

    mine_gu, theirs_gu = reduce_group(pf0[:1] + pf1[:1], rf0[:1] + rf1[:1])
    mine_r = [add_chips(p, r, chip.reshape(1)) for p, r in zip(pf0[1:] + pf1[1:] + pm1, rf0[1:] + rf1[1:] + rm1)]
    start_r, wait_r = share_direct(mine_r, "rest")
    state_r, tok_r = start_r()
    grad["w_gate_up"], delta["w_gate_up"], new_m["w_gate_up"], new_v["w_gate_up"] = adamw_shard(
        w_gate_up, m_w_gate_up, v_w_gate_up, mine_gu, theirs_gu, core.reshape(1), "adamw_w_gate_up", dep=tok_r)
    mine_r, theirs_r = wait_r(state_r, new_v["w_gate_up"])
    grad["w_down"], delta["w_down"], new_m["w_down"], new_v["w_down"] = adamw_shard(
        w_down, m_w_down, v_w_down, mine_r[:2], theirs_r[:2], core.reshape(1), "adamw_w_down")
    mine_f = {4: mine_r[2], 5: mine_r[3]}
    theirs_f = {4: theirs_r[2], 5: theirs_r[3]}

    mine_m0, theirs_m0 = reduce_group(*wait_m0(new_v["w_down"]))
    tr_ = lambda t: jnp.swapaxes(t, -1, -2)
    cut = lambda t: tr_(t[:, :IN_SHARD])
    res = adamw_shard_t(tr_(w_in), tr_(m_w_in), tr_(v_w_in), [cut(mine_m0[0]), cut(mine_f[4])],
                        [cut(theirs_m0[0]), cut(theirs_f[4])], core.reshape(1), "adamw_w_in")
    grad["w_in"], delta["w_in"], new_m["w_in"], new_v["w_in"] = (tr_(t) for t in res)
    grad["w_out"], delta["w_out"], new_m["w_out"], new_v["w_out"] = adamw_shard(
        w_out, m_w_out, v_w_out, [mine_m0[1], mine_f[5]], [theirs_m0[1], theirs_f[5]], core.reshape(1), "adamw_w_out")

    out = [loss, dx[None]]
    for d in (grad, delta, new_m, new_v):
        out += [d[n] for n in WEIGHT_ORDER]
    return tuple(out)
```

```python
import functools
import math

import numpy as np
import jax
import jax.numpy as jnp
from jax import lax
from jax.experimental import pallas as pl
from jax.experimental.pallas import tpu as pltpu

F32 = jnp.float32
BF16 = jnp.bfloat16
HI = lax.Precision.HIGH

T = 2048
D = 2048
DEPTH = 2
HD = 128
A_GROUPS, A_WIDTH, A_CHUNK = 4, 512, 128
B_HEADS, B_WIDTH, B_CONV, B_CHUNK = 6, 768, 4, 64
C_HEADS, C_WIDTH, C_BLOCK = 6, 768, 128
C_BRANCHES = ((128, 1), (512, 4), (2048, 16))
FFN = 5632
IN_TOTAL = 6412
EPS = 1e-6
N_CHIPS = 4
N_DEV = 8
IN_SHARD = IN_TOTAL // N_CHIPS
IN_SHARD_PAD = 1664
GU_SHARD = 2 * FFN // N_CHIPS
OUT_SHARD = D // N_CHIPS
DOWN_SHARD = FFN // N_CHIPS
P_AU, P_AV, P_BQ, P_BK, P_BV, P_BG, P_BB, P_CQ, P_CK, P_CV, P_END = (
    0, 512, 1024, 1792, 2560, 3328, 4096, 4224, 4992, 5760, 6528)
GATE_COLS = 4108
VMEM_LIMIT = 56 * 1024 * 1024

ADAM_LR, ADAM_B1, ADAM_B2, ADAM_EPS, ADAM_WD, ADAM_STEP = 0.001, 0.9, 0.999, 1e-08, 0.01, 10


def _cparams(sem, vmem=VMEM_LIMIT):
    return pltpu.CompilerParams(dimension_semantics=sem, vmem_limit_bytes=vmem)


def _dims(nd, ta, tb):
    off = nd - 2
    ca = off + (0 if ta else 1)
    cb = off + (1 if tb else 0)
    batch = ((0,), (0,)) if nd == 3 else ((), ())
    return (((ca,), (cb,)), batch)


def _raw_mm(a, b, ta, tb, hi):
    if hi:
        return lax.dot_general(a, b, _dims(a.ndim, ta, tb), precision=HI, preferred_element_type=F32)
    return lax.dot_general(a.astype(BF16), b.astype(BF16), _dims(a.ndim, ta, tb), preferred_element_type=F32)


@functools.partial(jax.custom_vjp, nondiff_argnums=(2, 3, 4))
def _mm(a, b, ta=False, tb=False, hi=False):
    return _raw_mm(a, b, ta, tb, hi)


def _mm_fwd(a, b, ta, tb, hi):
    return _raw_mm(a, b, ta, tb, hi), (a, b)


def _mm_bwd(ta, tb, hi, res, g):
    a, b = res
    da = _raw_mm(g, b, False, not tb, False) if not ta else _raw_mm(b, g, tb, True, False)
    db = _raw_mm(a, g, not ta, False, False) if not tb else _raw_mm(g, a, True, ta, False)
    return da.astype(a.dtype), db.astype(b.dtype)


_mm.defvjp(_mm_fwd, _mm_bwd)


def _rms(x, g):
    return x * lax.rsqrt(jnp.mean(x * x, axis=-1, keepdims=True) + EPS) * g


def _gelu(x):
    return 0.5 * x * (1.0 + jnp.tanh(math.sqrt(2.0 / math.pi) * (x + 0.044715 * (x * x * x))))


def _sigmoid(x):
    return 1.0 / (1.0 + jnp.exp(-x))


def _silu(x):
    return x * _sigmoid(x)


def _softplus(x):
    return jnp.maximum(x, 0.0) + jnp.log(1.0 + jnp.exp(-jnp.abs(x)))


def _iota(shape, dim):
    return lax.broadcasted_iota(jnp.int32, shape, dim)


def _sgu_fn(u, v, sg, w, b):
    nc = T // A_CHUNK
    ug = _gelu(u)
    vn = _rms(_gelu(v), sg)
    causal = _iota((A_CHUNK, A_CHUNK), 0) >= _iota((A_CHUNK, A_CHUNK), 1)
    wm = jnp.where(causal, w, 0.0)
    wb = jnp.broadcast_to(wm[None], (nc, A_CHUNK, A_CHUNK))
    z = _mm(wb, vn.reshape(nc, A_CHUNK, HD)) + b[None]
    return ug * z.reshape(T, HD)


def _sgu_specs():
    col = lambda off: pl.BlockSpec((T, HD), lambda g, off=off: (0, off + g))
    par = [pl.BlockSpec((None, 1, HD), lambda g: (g, 0, 0)),
           pl.BlockSpec((None, A_CHUNK, A_CHUNK), lambda g: (g, 0, 0)),
           pl.BlockSpec((None, A_CHUNK, 1), lambda g: (g, 0, 0))]
    return col, par


def sgu_fwd(p2, sg, w, b):
    col, par = _sgu_specs()

    def body(u_ref, v_ref, sg_ref, w_ref, b_ref, y_ref):
        y_ref[...] = _sgu_fn(u_ref[...], v_ref[...], sg_ref[...], w_ref[...], b_ref[...]).astype(BF16)

    return pl.pallas_call(
        body, name="sgu_fwd", grid=(A_GROUPS,),
        in_specs=[col(P_AU // HD), col(P_AV // HD)] + par,
        out_specs=pl.BlockSpec((T, HD), lambda g: (0, g)),
        out_shape=jax.ShapeDtypeStruct((T, A_WIDTH), BF16),
        compiler_params=_cparams(("arbitrary",)),
    )(p2, p2, sg, w, b)


def sgu_bwd(p2, sg, w, b, dmix):
    col, par = _sgu_specs()

    def body(u_ref, v_ref, sg_ref, w_ref, b_ref, dy_ref, du_ref, dv_ref, dsg_ref, dw_ref, db_ref):
        _, vjp = jax.vjp(_sgu_fn, u_ref[...], v_ref[...], sg_ref[...], w_ref[...], b_ref[...])
        du, dv, dsg, dw, db = vjp(dy_ref[...])
        du_ref[...] = du.astype(BF16)
        dv_ref[...] = dv.astype(BF16)
        dsg_ref[...] = dsg
        dw_ref[...] = dw
        db_ref[...] = db

    gcol = pl.BlockSpec((T, HD), lambda g: (0, g))
    return pl.pallas_call(
        body, name="sgu_bwd", grid=(A_GROUPS,),
        in_specs=[col(P_AU // HD), col(P_AV // HD)] + par + [gcol],
        out_specs=[gcol, gcol] + par,
        out_shape=[jax.ShapeDtypeStruct((T, A_WIDTH), BF16), jax.ShapeDtypeStruct((T, A_WIDTH), BF16),
                   jax.ShapeDtypeStruct((A_GROUPS, 1, HD), F32), jax.ShapeDtypeStruct((A_GROUPS, A_CHUNK, A_CHUNK), F32),
                   jax.ShapeDtypeStruct((A_GROUPS, A_CHUNK, 1), F32)],
        compiler_params=_cparams(("arbitrary",)),
    )(p2, p2, sg, w, b, dmix)


def _attn_fn(q, k, v, qg, kg, slope, *, dil, nb):
    n = T // C_BLOCK
    qb = _rms(q, qg).reshape(n, C_BLOCK, HD)
    kb = _rms(k, kg).reshape(n, C_BLOCK, HD)
    vb = v.reshape(n, C_BLOCK, HD)
    scale = HD ** -0.5
    qi = _iota((n, C_BLOCK, C_BLOCK), 1)
    kj = _iota((n, C_BLOCK, C_BLOCK), 2)
    sl = slope[None] * float(dil)
    d_cur = qi - kj
    sc = jnp.where(d_cur >= 0, _mm(qb, kb, tb=True) * scale - sl * d_cur.astype(F32), -jnp.inf)
    mx = jnp.max(sc, axis=-1, keepdims=True)
    if nb > 1:
        kp = jnp.concatenate([jnp.zeros((1, C_BLOCK, HD), F32), kb[:-1]], axis=0)
        vp = jnp.concatenate([jnp.zeros((1, C_BLOCK, HD), F32), vb[:-1]], axis=0)
        has_prev = (_iota((n, C_BLOCK, C_BLOCK), 0) % nb) > 0
        d_prev = C_BLOCK + qi - kj
        sp = jnp.where((kj >= qi) & has_prev, _mm(qb, kp, tb=True) * scale - sl * d_prev.astype(F32), -jnp.inf)
        mx = jnp.maximum(mx, jnp.max(sp, axis=-1, keepdims=True))
    p = jnp.exp(sc - mx)
    den = jnp.sum(p, axis=-1, keepdims=True)
    if nb > 1:
        pp = jnp.exp(sp - mx)
        den = den + jnp.sum(pp, axis=-1, keepdims=True)
    inv = 1.0 / den
    out = _mm(p * inv, vb)
    if nb > 1:
        out = out + _mm(pp * inv, vp)
    lse = mx + jnp.log(den)
    return out.reshape(T, HD), jnp.broadcast_to(lse, (n, C_BLOCK, HD)).reshape(T, HD)


def _combine_fn(o1, o2, o3, l1, l2, l3):
    mx = jnp.maximum(jnp.maximum(l1, l2), l3)
    e1, e2, e3 = jnp.exp(l1 - mx), jnp.exp(l2 - mx), jnp.exp(l3 - mx)
    r = 1.0 / (e1 + e2 + e3)
    return (e1 * r) * o1 + (e2 * r) * o2 + (e3 * r) * o3


def _branch_blocks(dil):
    return -(-(T // dil) // C_BLOCK)


def _load_branch_order(ref, dil):
    if dil == 1:
        return ref[...]
    seg = T // dil
    return jnp.concatenate([ref[pl.ds(r, seg, stride=dil), :] for r in range(dil)], axis=0)


def _store_position_order(ref, val, dil, add=False):
    seg = T // dil
    for r in range(dil):
        rows = slice(None) if dil == 1 else pl.ds(r, seg, stride=dil)
        piece = val if dil == 1 else val[r * seg:(r + 1) * seg]
        if add:
            ref[rows, :] += piece
        else:
            ref[rows, :] = piece


def _dattn_specs():
    col = lambda off: pl.BlockSpec((T, HD), lambda h, off=off: (0, off // HD + h))
    row = pl.BlockSpec((1, HD), lambda h: (0, 0))
    slope = pl.BlockSpec((None, 1, HD), lambda h: (h, 0, 0))
    return [col(P_CQ), col(P_CK), col(P_CV), row, row, slope]


def _dattn_branches(q_ref, k_ref, v_ref, qg, kg, slope, o_scr, l_scr):
    for b, (_, dil) in enumerate(C_BRANCHES):
        q, k, v = (_load_branch_order(r, dil) for r in (q_ref, k_ref, v_ref))
        o, l = _attn_fn(q, k, v, qg, kg, slope, dil=dil, nb=_branch_blocks(dil))
        _store_position_order(o_scr.at[b], o, dil)
        _store_position_order(l_scr.at[b], l, dil)


def dattn_fwd(p2, qg, kg, slopes):
    per_branch = pl.BlockSpec((3, T, HD), lambda h: (0, 0, h))

    def body(q_ref, k_ref, v_ref, qg_ref, kg_ref, s_ref, y_ref, o_ref, l_ref):
        _dattn_branches(q_ref, k_ref, v_ref, qg_ref[...], kg_ref[...], s_ref[...], o_ref, l_ref)
        y_ref[...] = _combine_fn(o_ref[0], o_ref[1], o_ref[2], l_ref[0], l_ref[1], l_ref[2]).astype(BF16)

    return pl.pallas_call(
        body, name="dattn_fwd", grid=(C_HEADS,), in_specs=_dattn_specs(),
        out_specs=[pl.BlockSpec((T, HD), lambda h: (0, h)), per_branch, per_branch],
        out_shape=[jax.ShapeDtypeStruct((T, C_WIDTH), BF16)] + [jax.ShapeDtypeStruct((3, T, C_WIDTH), F32)] * 2,
        compiler_params=_cparams(("arbitrary",)),
    )(p2, p2, p2, qg, kg, slopes)


def dattn_bwd(p2, qg, kg, slopes, outs, lses, dmix):
    hcol = pl.BlockSpec((T, HD), lambda h: (0, h))
    row = pl.BlockSpec((1, HD), lambda h: (0, 0))
    dy = pl.BlockSpec((T, HD), lambda h: (0, (A_WIDTH + B_WIDTH) // HD + h))
    per_branch = pl.BlockSpec((3, T, HD), lambda h: (0, 0, h))

    def body(q_ref, k_ref, v_ref, qg_ref, kg_ref, s_ref, o_scr, l_scr, dy_ref, dq_ref, dk_ref, dv_ref, dqg_ref, dkg_ref, g_scr,
             acc):
        qg, kg, slope = qg_ref[...], kg_ref[...], s_ref[...]
        _, vjp = jax.vjp(_combine_fn, o_scr[0], o_scr[1], o_scr[2], l_scr[0], l_scr[1], l_scr[2])
        for i, g in enumerate(vjp(dy_ref[...])):
            g_scr[i] = g

        @pl.when(pl.program_id(0) == 0)
        def _():
            dqg_ref[...] = jnp.zeros_like(dqg_ref)
            dkg_ref[...] = jnp.zeros_like(dkg_ref)

        for b, (_, dil) in enumerate(C_BRANCHES):
            q, k, v = (_load_branch_order(r, dil) for r in (q_ref, k_ref, v_ref))
            do, dl = _load_branch_order(g_scr.at[b], dil), _load_branch_order(g_scr.at[3 + b], dil)
            fn = functools.partial(_attn_fn, dil=dil, nb=_branch_blocks(dil))
            _, vjp_b = jax.vjp(lambda a, b_, c, d, e, fn=fn: fn(a, b_, c, d, e, slope), q, k, v, qg, kg)
            dq, dk, dv, dqg, dkg = vjp_b((do, dl))
            for i, val in enumerate((dq, dk, dv)):
                _store_position_order(acc.at[i], val, dil, add=b > 0)
            dqg_ref[...] += dqg
            dkg_ref[...] += dkg
        for i, ref in enumerate((dq_ref, dk_ref, dv_ref)):
            ref[...] = acc[i].astype(BF16)

    scr = lambda n: pltpu.VMEM((n, T, HD), F32)
    return pl.pallas_call(
        body, name="dattn_bwd", grid=(C_HEADS,), in_specs=_dattn_specs() + [per_branch, per_branch, dy],
        out_specs=[hcol, hcol, hcol, row, row],
        out_shape=[jax.ShapeDtypeStruct((T, C_WIDTH), BF16)] * 3 + [jax.ShapeDtypeStruct((1, HD), F32)] * 2,
        scratch_shapes=[scr(6), scr(3)], compiler_params=_cparams(("arbitrary",)),
    )(p2, p2, p2, qg, kg, slopes, outs, lses, dmix)


_NCH = T // B_CHUNK


def _conv_taps(x, w_ref):
    rows = _iota(x.shape, 0)
    taps = []
    for j in range(B_CONV):
        s = B_CONV - 1 - j
        taps.append(x if s == 0 else jnp.where(rows >= s, pltpu.roll(x, s, 0), 0.0))
    pre = sum(w_ref[j:j + 1, :] * taps[j] for j in range(B_CONV))
    return pre, taps


def _conv_post(pre, mode):
    y = _silu(pre)
    if mode == "v":
        return y
    y = y * lax.rsqrt(jnp.sum(y * y, axis=-1, keepdims=True) + EPS)
    return y * (HD ** -0.5) if mode == "q" else y


def conv_fwd(p2, conv_w, mode):
    idx = "qkv".index(mode)
    xcol = pl.BlockSpec((T, HD), lambda h: (0, P_BQ // HD + B_HEADS * idx + h))
    wcol = pl.BlockSpec((B_CONV, HD), lambda h: (0, B_HEADS * idx + h))
    hcol = pl.BlockSpec((T, HD), lambda h: (0, h))

    def body(x_ref, w_ref, y_ref):
        pre, _ = _conv_taps(x_ref[...], w_ref)
        y_ref[...] = _conv_post(pre, mode)

    return pl.pallas_call(
        body, name=f"conv_fwd_{mode}", grid=(B_HEADS,), in_specs=[xcol, wcol], out_specs=hcol,
        out_shape=jax.ShapeDtypeStruct((T, B_WIDTH), F32), compiler_params=_cparams(("arbitrary",)),
    )(p2, conv_w)


def conv_bwd(p2, conv_w, dys, mode):
    idx = "qkv".index(mode)
    xcol = pl.BlockSpec((T, HD), lambda h: (0, P_BQ // HD + B_HEADS * idx + h))
    wcol = pl.BlockSpec((B_CONV, HD), lambda h: (0, B_HEADS * idx + h))
    hcol = pl.BlockSpec((T, HD), lambda h: (0, h))
    wout = pl.BlockSpec((B_CONV, HD), lambda h: (0, h))

    def body(x_ref, w_ref, *rest):
        dy_refs, (dx_ref, dw_ref) = rest[:-2], rest[-2:]
        pre, taps = _conv_taps(x_ref[...], w_ref)
        _, vjp = jax.vjp(functools.partial(_conv_post, mode=mode), pre)
        (dpre,) = vjp(sum(r[...] for r in dy_refs))
        rows = _iota(dpre.shape, 0)
        dx = w_ref[B_CONV - 1:B_CONV, :] * dpre
        for j in range(B_CONV):
            s = B_CONV - 1 - j
            dw_ref[j:j + 1, :] = jnp.sum(dpre * taps[j], axis=0, keepdims=True)
            if s > 0:
                dx = dx + w_ref[j:j + 1, :] * jnp.where(rows < T - s, pltpu.roll(dpre, T - s, 0), 0.0)
        dx_ref[...] = dx.astype(BF16)

    return pl.pallas_call(
        body, name=f"conv_bwd_{mode}", grid=(B_HEADS,), in_specs=[xcol, wcol] + [hcol] * len(dys), out_specs=[hcol, wout],
        out_shape=[jax.ShapeDtypeStruct((T, B_WIDTH), BF16), jax.ShapeDtypeStruct((B_CONV, B_WIDTH), F32)],
        compiler_params=_cparams(("arbitrary",)),
    )(p2, conv_w, *dys)


def _gates_fn(bg, al, dtb, h):
    r = _iota((HD, HD), 0)
    logit = _mm(bg, (r == h).astype(F32), hi=True)
    a = _mm(bg, (r == h + B_HEADS).astype(F32), hi=True)
    beta = _sigmoid(logit)
    graw = -jnp.exp(al) * _softplus(a + dtb)
    tri = (_iota((_NCH, B_CHUNK, B_CHUNK), 1) >= _iota((_NCH, B_CHUNK, B_CHUNK), 2)).astype(F32)
    g = _mm(tri, graw.reshape(_NCH, B_CHUNK, HD), hi=True).reshape(T, HD)
    return beta, g


def _gates_specs():
    bg = pl.BlockSpec((T, HD), lambda h: (0, P_BB // HD))
    par = pl.BlockSpec((None, 1, HD), lambda h: (h, 0, 0))
    out = pl.BlockSpec((None, T, HD), lambda h: (h, 0, 0))
    return bg, par, out


def gates_fwd(p2, al, dtb):
    bg, par, out = _gates_specs()

    def body(bg_ref, al_ref, dtb_ref, beta_ref, g_ref):
        beta, g = _gates_fn(bg_ref[...], al_ref[...], dtb_ref[...], pl.program_id(0))
        beta_ref[...] = beta
        g_ref[...] = g

    return pl.pallas_call(
        body, name="gates_fwd", grid=(B_HEADS,), in_specs=[bg, par, par], out_specs=[out, out],
        out_shape=[jax.ShapeDtypeStruct((B_HEADS, T, HD), F32)] * 2, compiler_params=_cparams(("arbitrary",)),
    )(p2, al, dtb)


def gates_bwd(p2, al, dtb, dbeta, dg1, dg2):
    bg, par, out = _gates_specs()
    acc = pl.BlockSpec((T, HD), lambda h: (0, 0))

    def body(bg_ref, al_ref, dtb_ref, dbeta_ref, dg1_ref, dg2_ref, dbg_ref, dal_ref, ddtb_ref, acc_ref):
        h = pl.program_id(0)
        _, vjp = jax.vjp(lambda a, b, c: _gates_fn(a, b, c, h), bg_ref[...], al_ref[...], dtb_ref[...])
        dbg, dal, ddtb = vjp((dbeta_ref[...], dg1_ref[...] + dg2_ref[...]))

        @pl.when(h == 0)
        def _():
            acc_ref[...] = jnp.zeros_like(acc_ref)

        acc_ref[...] += dbg
        dbg_ref[...] = acc_ref[...].astype(BF16)
        dal_ref[...] = jnp.broadcast_to(jnp.sum(dal, axis=-1, keepdims=True), (1, HD))
        ddtb_ref[...] = jnp.broadcast_to(jnp.sum(ddtb, axis=-1, keepdims=True), (1, HD))

    return pl.pallas_call(
        body, name="gates_bwd", grid=(B_HEADS,), in_specs=[bg, par, par, out, out, out], out_specs=[acc, par, par],
        out_shape=[jax.ShapeDtypeStruct((T, HD), BF16)] + [jax.ShapeDtypeStruct((B_HEADS, 1, HD), F32)] * 2,
        scratch_shapes=[pltpu.VMEM((T, HD), F32)], compiler_params=_cparams(("arbitrary",)),
    )(p2, al, dtb, dbeta, dg1, dg2)


def _unit_lower_inverse(a):
    eye = (_iota(a.shape, 1) == _iota(a.shape, 2)).astype(F32)
    x = eye - a
    p = _mm(a, a, hi=True)
    for i in range(5):
        x = x + _mm(x, p, hi=True)
        if i < 4:
            p = _mm(p, p, hi=True)
    return x


_WY_CH = 16
_WY_ROWS = _WY_CH * B_CHUNK


def _wy_fn(q, k, v, beta, g):
    sh = (q.shape[0] // B_CHUNK, B_CHUNK, HD)
    q3, k3, v3, b3, g3 = (t.reshape(sh) for t in (q, k, v, beta, g))
    gd = g3[:, :, :B_CHUNK] - jnp.swapaxes(g3, 1, 2)[:, :B_CHUNK, :]
    ii, jj = _iota(gd.shape, 1), _iota(gd.shape, 2)
    decay = jnp.exp(jnp.where(ii >= jj, gd, -jnp.inf))
    kb = k3 * b3
    a = _mm(kb, k3, tb=True) * jnp.where(ii > jj, decay, 0.0)
    tinv = _unit_lower_inverse(a)
    u = _mm(tinv, v3 * b3, hi=True)
    w = _mm(tinv, kb * jnp.exp(g3), hi=True)
    attn = _mm(q3, k3, tb=True) * decay
    return u.reshape(q.shape), w.reshape(q.shape), attn


def _wy_specs():
    hcol = pl.BlockSpec((_WY_ROWS, HD), lambda h, i: (i, h))
    hb = pl.BlockSpec((None, _WY_ROWS, HD), lambda h, i: (h, i, 0))
    at = pl.BlockSpec((None, _WY_CH, B_CHUNK, B_CHUNK), lambda h, i: (h, i, 0, 0))
    return hcol, hb, at


_WY_GRID = (B_HEADS, _NCH // _WY_CH)


def wy_fwd(q, k, v, beta, g):
    hcol, hb, at = _wy_specs()

    def body(q_ref, k_ref, v_ref, b_ref, g_ref, u_ref, w_ref, a_ref):
        u, w, a = _wy_fn(q_ref[...], k_ref[...], v_ref[...], b_ref[...], g_ref[...])
        u_ref[...] = u
        w_ref[...] = w
        a_ref[...] = a

    return pl.pallas_call(
        body, name="wy_fwd", grid=_WY_GRID, in_specs=[hcol, hcol, hcol, hb, hb], out_specs=[hcol, hcol, at],
        out_shape=[jax.ShapeDtypeStruct((T, B_WIDTH), F32)] * 2 + [jax.ShapeDtypeStruct((B_HEADS, _NCH, B_CHUNK, B_CHUNK), F32)],
        compiler_params=_cparams(("arbitrary", "arbitrary")),
    )(q, k, v, beta, g)


def wy_bwd(q, k, v, beta, g, du, dw, dattn):
    hcol, hb, at = _wy_specs()

    def body(q_ref, k_ref, v_ref, b_ref, g_ref, du_ref, dw_ref, da_ref, dq_ref, dk_ref, dv_ref, db_ref, dg_ref):
        _, vjp = jax.vjp(_wy_fn, q_ref[...], k_ref[...], v_ref[...], b_ref[...], g_ref[...])
        for r, t in zip((dq_ref, dk_ref, dv_ref, db_ref, dg_ref), vjp((du_ref[...], dw_ref[...], da_ref[...]))):
            r[...] = t

    return pl.pallas_call(
        body, name="wy_bwd", grid=_WY_GRID, in_specs=[hcol, hcol, hcol, hb, hb, hcol, hcol, at],
        out_specs=[hcol, hcol, hcol, hb, hb],
        out_shape=[jax.ShapeDtypeStruct((T, B_WIDTH), F32)] * 3 + [jax.ShapeDtypeStruct((B_HEADS, T, HD), F32)] * 2,
        compiler_params=_cparams(("arbitrary", "arbitrary")),
    )(q, k, v, beta, g, du, dw, dattn)


def _scan_step_fn(q, k, u, w, g, attn, gate, og, s):
    v_new = u - _mm(w, s)
    o = _mm(q * jnp.exp(g), s) + _mm(attn, v_new)
    g_last = jnp.sum(jnp.where(_iota(g.shape, 0) == B_CHUNK - 1, g, 0.0), axis=0, keepdims=True)
    s_new = s * jnp.exp(g_last) + _mm(k * jnp.exp(g_last - g), v_new, ta=True)
    return _rms(o, og) * _silu(gate), s_new


def _scan_specs(rev):
    ch = (lambda n: _NCH - 1 - n) if rev else (lambda n: n)
    rows = pl.BlockSpec((B_CHUNK, B_WIDTH), lambda n: (ch(n), 0))
    gb = pl.BlockSpec((B_HEADS, B_CHUNK, HD), lambda n: (0, ch(n), 0))
    at = pl.BlockSpec((B_HEADS, None, B_CHUNK, B_CHUNK), lambda n: (0, ch(n), 0, 0))
    og = pl.BlockSpec((1, HD), lambda n: (0, 0))
    st = pl.BlockSpec((None, B_HEADS, HD, HD), lambda n: (ch(n), 0, 0, 0))
    return rows, gb, at, og, st


def scan_fwd(q, k, u, w, g, attn, gate, og):
    rows, gb, at, ogs, st = _scan_specs(False)

    def body(q_ref, k_ref, u_ref, w_ref, g_ref, a_ref, gate_ref, og_ref, y_ref, st_ref, s_ref):
        @pl.when(pl.program_id(0) == 0)
        def _():
            s_ref[...] = jnp.zeros_like(s_ref)

        for h in range(B_HEADS):
            c = slice(h * HD, (h + 1) * HD)
            s = s_ref[h]
            st_ref[h] = s
            y, s_new = _scan_step_fn(q_ref[:, c], k_ref[:, c], u_ref[:, c], w_ref[:, c], g_ref[h], a_ref[h],
                                     gate_ref[:, c], og_ref[...], s)
            y_ref[:, c] = y.astype(BF16)
            s_ref[h] = s_new

    return pl.pallas_call(
        body, name="scan_fwd", grid=(_NCH,), in_specs=[rows, rows, rows, rows, gb, at, rows, ogs], out_specs=[rows, st],
        out_shape=[jax.ShapeDtypeStruct((T, B_WIDTH), BF16), jax.ShapeDtypeStruct((_NCH, B_HEADS, HD, HD), F32)],
        scratch_shapes=[pltpu.VMEM((B_HEADS, HD, HD), F32)], compiler_params=_cparams(("arbitrary",)),
    )(q, k, u, w, g, attn, gate, og)


def scan_bwd(q, k, u, w, g, attn, gate, og, states, dmix):
    rows, gb, at, ogs, st = _scan_specs(True)
    dyb = pl.BlockSpec((B_CHUNK, HD), lambda n: (_NCH - 1 - n, 0))

    def body(q_ref, k_ref, u_ref, w_ref, g_ref, a_ref, gate_ref, og_ref, st_ref, *rest):
        dy_refs, (dq_ref, dk_ref, du_ref, dw_ref, dgate_ref, dg_ref, da_ref, dog_ref, ds_ref) = rest[:B_HEADS], rest[B_HEADS:]

        @pl.when(pl.program_id(0) == 0)
        def _():
            ds_ref[...] = jnp.zeros_like(ds_ref)
            dog_ref[...] = jnp.zeros_like(dog_ref)

        for h in range(B_HEADS):
            c = slice(h * HD, (h + 1) * HD)
            _, vjp = jax.vjp(_scan_step_fn, q_ref[:, c], k_ref[:, c], u_ref[:, c], w_ref[:, c], g_ref[h], a_ref[h],
                             gate_ref[:, c], og_ref[...], st_ref[h])
            dq, dk, du, dw, dg, da, dgate, dog, ds = vjp((dy_refs[h][...], ds_ref[h]))
            dq_ref[:, c] = dq
            dk_ref[:, c] = dk
            du_ref[:, c] = du
            dw_ref[:, c] = dw
            dgate_ref[:, c] = dgate.astype(BF16)
            dg_ref[h] = dg
            da_ref[h] = da
            dog_ref[...] += dog
            ds_ref[h] = ds

    dy_specs = [pl.BlockSpec((B_CHUNK, HD), lambda n, h=h: (_NCH - 1 - n, A_WIDTH // HD + h)) for h in range(B_HEADS)]
    return pl.pallas_call(
        body, name="scan_bwd", grid=(_NCH,),
        in_specs=[rows, rows, rows, rows, gb, at, rows, ogs, st] + dy_specs,
        out_specs=[rows] * 5 + [gb, at, ogs],
        out_shape=[jax.ShapeDtypeStruct((T, B_WIDTH), F32)] * 4 + [jax.ShapeDtypeStruct((T, B_WIDTH), BF16)]
        + [jax.ShapeDtypeStruct((B_HEADS, T, HD), F32), jax.ShapeDtypeStruct((B_HEADS, _NCH, B_CHUNK, B_CHUNK), F32),
           jax.ShapeDtypeStruct((1, HD), F32)],
        scratch_shapes=[pltpu.VMEM((B_HEADS, HD, HD), F32)], compiler_params=_cparams(("arbitrary",)),
    )(q, k, u, w, g, attn, gate, og, states, *([dmix] * B_HEADS))


def _lanes(vec):
    return jnp.broadcast_to(vec[:, None, None], (vec.shape[0], 1, HD))


def gdn_forward(p2, conv_w, a_log, dt_bias, og):
    qa, ka, va = (conv_fwd(p2, conv_w, m) for m in "qkv")
    beta, g = gates_fwd(p2, _lanes(a_log), _lanes(dt_bias))
    u, w, attn = wy_fwd(qa, ka, va, beta, g)
    gate = p2[:, P_BG:P_BB]
    y, states = scan_fwd(qa, ka, u, w, g, attn, gate, og)
    return y, (qa, ka, va, beta, g, u, w, attn, gate, states)


def gdn_backward(p2, conv_w, a_log, dt_bias, og, saved, dmix):
    qa, ka, va, beta, g, u, w, attn, gate, states = saved
    dq1, dk1, du, dw, dgate, dg1, dattn, dog = scan_bwd(qa, ka, u, w, g, attn, gate, og, states, dmix)
    dq2, dk2, dv, dbeta, dg2 = wy_bwd(qa, ka, va, beta, g, du, dw, dattn)
    dbg, dal, ddtb = gates_bwd(p2, _lanes(a_log), _lanes(dt_bias), dbeta, dg1, dg2)
    dxq, dwq = conv_bwd(p2, conv_w, [dq1, dq2], "q")
    dxk, dwk = conv_bwd(p2, conv_w, [dk1, dk2], "k")
    dxv, dwv = conv_bwd(p2, conv_w, [dv], "v")
    return [dxq, dxk, dxv, dgate, dbg], jnp.concatenate([dwq, dwk, dwv], axis=1), dal[:, 0, 0], ddtb[:, 0, 0], dog


_SLOPES = np.exp2(-8.0 * (np.arange(C_HEADS, dtype=np.float64) + 1.0) / C_HEADS).astype(np.float32)


def _alibi_slopes():
    return _lanes(jnp.asarray(_SLOPES))


_ROWS = 512
_TM = 1024
_TM_FFN = 512


def _dep_specs(dep, ngrid):
    if dep is None:
        return [], []
    return [dep], [pl.BlockSpec((8, HD), lambda *_: (0, 0))]


def rmsnorm_fwd(x, g, dep=None):
    blk = pl.BlockSpec((_ROWS, D), lambda i: (i, 0))
    deps, dspecs = _dep_specs(dep, 1)

    def body(x_ref, g_ref, *rest):
        rest[-1][...] = _rms(x_ref[...], g_ref[...]).astype(BF16)

    return pl.pallas_call(
        body, name="rmsnorm_fwd", grid=(T // _ROWS,), in_specs=[blk, pl.BlockSpec((1, D), lambda i: (0, 0))] + dspecs,
        out_specs=blk, out_shape=jax.ShapeDtypeStruct((T, D), BF16), compiler_params=_cparams(("arbitrary",)),
    )(x, g, *deps)


def rmsnorm_bwd(x, g, dh, dres):
    blk = pl.BlockSpec((_ROWS, D), lambda i: (i, 0))
    row = pl.BlockSpec((1, D), lambda i: (0, 0))

    def body(x_ref, g_ref, dh_ref, dres_ref, dx_ref, dxb_ref, dg_ref):
        _, vjp = jax.vjp(_rms, x_ref[...], g_ref[...])
        dx, dg = vjp(dh_ref[...])
        total = dres_ref[...] + dx
        dx_ref[...] = total
        dxb_ref[...] = total.astype(BF16)

        @pl.when(pl.program_id(0) == 0)
        def _():
            dg_ref[...] = jnp.zeros_like(dg_ref)

        dg_ref[...] += dg

    dx, dxb, dg = pl.pallas_call(
        body, name="rmsnorm_bwd", grid=(T // _ROWS,), in_specs=[blk, row, blk, blk], out_specs=[blk, blk, row],
        out_shape=[jax.ShapeDtypeStruct((T, D), F32), jax.ShapeDtypeStruct((T, D), BF16), jax.ShapeDtypeStruct((1, D), F32)],
        compiler_params=_cparams(("arbitrary",)),
    )(x, g, dh, dres)
    return (dx, dxb), dg


def _matmul(name, a, b, *, grid, a_spec, b_spec, o_spec, out_shape, ta=False, tb=False, k_axis=None, res=None, dep=None):
    dims = _dims(2, ta, tb)
    deps, dspecs = _dep_specs(dep, len(grid))

    def body(a_ref, b_ref, *rest):
        o_ref = rest[-1]
        prod = lax.dot_general(a_ref[...].astype(BF16), b_ref[...].astype(BF16), dims, preferred_element_type=F32)
        if res is not None:
            prod = prod + rest[0][...]
        if k_axis is None:
            o_ref[...] = prod.astype(o_ref.dtype)
        else:
            @pl.when(pl.program_id(k_axis) == 0)
            def _():
                o_ref[...] = prod

            @pl.when(pl.program_id(k_axis) > 0)
            def _():
                o_ref[...] += prod

    sem = tuple("arbitrary" for _ in grid)
    ins = [a, b] + ([res] if res is not None else []) + deps
    specs = [a_spec, b_spec] + ([o_spec] if res is not None else []) + dspecs
    return pl.pallas_call(
        body, name=name, grid=grid, in_specs=specs, out_specs=o_spec, out_shape=out_shape, compiler_params=_cparams(sem),
    )(*ins)


_IN_TN = P_END // 3


def mm_proj(h1, wp_in, l):
    return _matmul(
        "mm_proj", h1, wp_in, grid=(P_END // _IN_TN, T // _TM),
        a_spec=pl.BlockSpec((_TM, D), lambda j, i: (i, 0)),
        b_spec=pl.BlockSpec((None, D, _IN_TN), lambda j, i: (l, 0, j)),
        o_spec=pl.BlockSpec((_TM, _IN_TN), lambda j, i: (i, j)), out_shape=jax.ShapeDtypeStruct((T, P_END), F32))


def mm_dh1(dp2, wp_in, l, dep=None):
    return _matmul(
        "mm_dh1", dp2, wp_in, grid=(T // _TM, P_END // _IN_TN), tb=True, k_axis=1, dep=dep,
        a_spec=pl.BlockSpec((_TM, _IN_TN), lambda i, k: (i, k)),
        b_spec=pl.BlockSpec((None, D, _IN_TN), lambda i, k: (l, 0, k)),
        o_spec=pl.BlockSpec((_TM, D), lambda i, k: (i, 0)), out_shape=jax.ShapeDtypeStruct((T, D), F32))


def mm_dwin(h1, dp2):
    return _matmul(
        "mm_dwin", h1, dp2, grid=(P_END // _IN_TN, D // _TM), ta=True,
        a_spec=pl.BlockSpec((T, _TM), lambda j, i: (0, i)),
        b_spec=pl.BlockSpec((T, _IN_TN), lambda j, i: (0, j)),
        o_spec=pl.BlockSpec((_TM, _IN_TN), lambda j, i: (i, j)), out_shape=jax.ShapeDtypeStruct((D, P_END), BF16))


def _mm_square(name, a, w, l, res, tb, dep=None):
    tn = 1024
    b_spec = (pl.BlockSpec((None, tn, D), lambda j, i: (l, j, 0)) if tb else pl.BlockSpec((None, D, tn), lambda j, i: (l, 0, j)))
    return _matmul(
        name, a, w, grid=(D // tn, T // _TM), tb=tb, res=res, dep=dep,
        a_spec=pl.BlockSpec((_TM, D), lambda j, i: (i, 0)), b_spec=b_spec,
        o_spec=pl.BlockSpec((_TM, tn), lambda j, i: (i, j)), out_shape=jax.ShapeDtypeStruct((T, D), F32))


def mm_out(mix, wg_out, l, x):
    return _mm_square("mm_out", mix, wg_out, l, x, False)


def mm_dmix(dx1, wg_out, l, dep=None):
    return _mm_square("mm_dmix", dx1, wg_out, l, None, True, dep)


def mm_dwout(mix, dx1):
    tn = 1024
    return _matmul(
        "mm_dwout", mix, dx1, grid=(D // tn, D // _TM), ta=True,
        a_spec=pl.BlockSpec((T, _TM), lambda j, i: (0, i)), b_spec=pl.BlockSpec((T, tn), lambda j, i: (0, j)),
        o_spec=pl.BlockSpec((_TM, tn), lambda j, i: (i, j)), out_shape=jax.ShapeDtypeStruct((D, D), BF16))


_GU_TN = GU_SHARD // 2


_GU_NJ = FFN // _GU_TN


def mm_dh2(dgu, wg_gu, l, dep=None):
    return _matmul(
        "mm_dh2", dgu, wg_gu, grid=(T // _TM, 2 * N_CHIPS), tb=True, k_axis=1, dep=dep,
        a_spec=pl.BlockSpec((None, _TM, _GU_TN), lambda i, k: (k // _GU_NJ, i, k % _GU_NJ)),
        b_spec=pl.BlockSpec((None, None, D, _GU_TN), lambda i, k: (l, k // 2, 0, k % 2)),
        o_spec=pl.BlockSpec((_TM, D), lambda i, k: (i, 0)), out_shape=jax.ShapeDtypeStruct((T, D), F32))


def mm_dwgu(h2, dgu):
    return _matmul(
        "mm_dwgu", h2, dgu, grid=(N_CHIPS, 2, D // _TM), ta=True,
        a_spec=pl.BlockSpec((T, _TM), lambda s, j, i: (0, i)),
        b_spec=pl.BlockSpec((None, T, _GU_TN), lambda s, j, i: ((2 * s + j) // _GU_NJ, 0, (2 * s + j) % _GU_NJ)),
        o_spec=pl.BlockSpec((None, _TM, _GU_TN), lambda s, j, i: (s, i, j)),
        out_shape=jax.ShapeDtypeStruct((N_CHIPS, D, GU_SHARD), BF16))


def mm_down(act, wg_down, l, x1):
    tn = 512
    return _matmul(
        "mm_down", act, wg_down, grid=(D // tn, T // _TM), res=x1,
        a_spec=pl.BlockSpec((_TM, FFN), lambda j, i: (i, 0)),
        b_spec=pl.BlockSpec((None, FFN, tn), lambda j, i: (l, 0, j)),
        o_spec=pl.BlockSpec((_TM, tn), lambda j, i: (i, j)), out_shape=jax.ShapeDtypeStruct((T, D), F32))


def mm_dwdown(act, dx2):
    tm, tn = DOWN_SHARD, 512
    return _matmul(
        "mm_dwdown", act, dx2, grid=(D // tn, FFN // tm), ta=True,
        a_spec=pl.BlockSpec((T, tm), lambda j, i: (0, i)), b_spec=pl.BlockSpec((T, tn), lambda j, i: (0, j)),
        o_spec=pl.BlockSpec((tm, tn), lambda j, i: (i, j)), out_shape=jax.ShapeDtypeStruct((FFN, D), BF16))


_FF_TN = 1408


def _swiglu_fn(gt, up):
    return _silu(gt) * up


def _gate_up_specs():
    gate = pl.BlockSpec((None, None, D, _FF_TN), lambda j, i: (0, j // 2, 0, j % 2))
    up = pl.BlockSpec((None, None, D, _FF_TN), lambda j, i: (0, N_CHIPS // 2 + j // 2, 0, j % 2))
    both = pl.BlockSpec((2, _TM_FFN, _FF_TN), lambda j, i: (0, i, j))
    return gate, up, both


def mm_gu_swiglu(h2, wg_gu):
    gate, up, both = _gate_up_specs()

    def body(h_ref, wg_ref, wu_ref, gu_ref, act_ref):
        h = h_ref[...]
        gt = jnp.dot(h, wg_ref[...], preferred_element_type=F32)
        u = jnp.dot(h, wu_ref[...], preferred_element_type=F32)
        gu_ref[0] = gt.astype(BF16)
        gu_ref[1] = u.astype(BF16)
        act_ref[...] = _swiglu_fn(gt, u).astype(BF16)

    return pl.pallas_call(
        body, name="mm_gu_swiglu", grid=(FFN // _FF_TN, T // _TM_FFN),
        in_specs=[pl.BlockSpec((_TM_FFN, D), lambda j, i: (i, 0)), gate, up],
        out_specs=[both, pl.BlockSpec((_TM_FFN, _FF_TN), lambda j, i: (i, j))],
        out_shape=[jax.ShapeDtypeStruct((2, T, FFN), BF16), jax.ShapeDtypeStruct((T, FFN), BF16)],
        compiler_params=_cparams(("arbitrary", "arbitrary")),
    )(h2, wg_gu, wg_gu)


def mm_dact_swiglu(dx2, wg_down, gu, dep=None):
    _, _, both = _gate_up_specs()
    deps, dspecs = _dep_specs(dep, 2)

    def body(dx_ref, w_ref, gu_ref, *rest):
        dact = lax.dot_general(dx_ref[...].astype(BF16), w_ref[...], _dims(2, False, True), preferred_element_type=F32)
        _, vjp = jax.vjp(_swiglu_fn, gu_ref[0].astype(F32), gu_ref[1].astype(F32))
        dgt, dup = vjp(dact)
        rest[-1][0] = dgt.astype(BF16)
        rest[-1][1] = dup.astype(BF16)

    return pl.pallas_call(
        body, name="mm_dact_swiglu", grid=(FFN // _FF_TN, T // _TM_FFN),
        in_specs=[pl.BlockSpec((_TM_FFN, D), lambda j, i: (i, 0)), pl.BlockSpec((None, _FF_TN, D), lambda j, i: (0, j, 0)),
                  both]
        + dspecs,
        out_specs=both, out_shape=jax.ShapeDtypeStruct((2, T, FFN), BF16),
        compiler_params=_cparams(("arbitrary", "arbitrary")),
    )(dx2, wg_down, gu, *deps)


def loss_and_grad(y, target):
    blk = pl.BlockSpec((_ROWS, D), lambda i: (i, 0))
    acc = pl.BlockSpec((8, HD), lambda i: (0, 0))

    def body(y_ref, t_ref, dy_ref, dyb_ref, l_ref):
        err = y_ref[...] - t_ref[...]
        dy_ref[...] = err * (1.0 / D)
        dyb_ref[...] = (err * (1.0 / D)).astype(BF16)

        @pl.when(pl.program_id(0) == 0)
        def _():
            l_ref[...] = jnp.zeros_like(l_ref)

        l_ref[...] += (0.5 / D) * jnp.sum(err * err)

    dy, dyb, loss = pl.pallas_call(
        body, name="loss_and_grad", grid=(T // _ROWS,), in_specs=[blk, blk], out_specs=[blk, blk, acc],
        out_shape=[jax.ShapeDtypeStruct((T, D), F32), jax.ShapeDtypeStruct((T, D), BF16), jax.ShapeDtypeStruct((8, HD), F32)],
        compiler_params=_cparams(("arbitrary",)),
    )(y, target)
    return (dy, dyb), loss


def adamw(w, g, m, v, name):
    rows, cols = w.shape
    tr = _ROWS if rows % _ROWS == 0 else rows
    blk = pl.BlockSpec((tr, cols), lambda i: (i, 0))

    def body(w_ref, g_ref, m_ref, v_ref, d_ref, nm_ref, nv_ref):
        gg = g_ref[...]
        nm = ADAM_B1 * m_ref[...] + (1.0 - ADAM_B1) * gg
        nv = ADAM_B2 * v_ref[...] + (1.0 - ADAM_B2) * (gg * gg)
        m_hat = nm / (1.0 - ADAM_B1 ** ADAM_STEP)
        v_hat = nv / (1.0 - ADAM_B2 ** ADAM_STEP)
        d_ref[...] = -ADAM_LR * (m_hat / (jnp.sqrt(v_hat) + ADAM_EPS) + ADAM_WD * w_ref[...])
        nm_ref[...] = nm
        nv_ref[...] = nv

    return pl.pallas_call(
        body, name=name, grid=(rows // tr,), in_specs=[blk] * 4, out_specs=[blk] * 3,
        out_shape=[jax.ShapeDtypeStruct(w.shape, F32)] * 3, compiler_params=_cparams(("arbitrary",)),
    )(w, g, m, v)


_LANE = 128


def _segment_of_shard_column():
    flat = np.full(P_END, -1, np.int64)
    for o in range(P_END):
        if GATE_COLS <= o < P_CQ:
            continue
        c = o if o < GATE_COLS else o - (P_CQ - GATE_COLS)
        flat[o] = (c // IN_SHARD) * IN_SHARD_PAD + c % IN_SHARD
    return flat


def _block_pairs(src_of_dst):
    return [sorted({int(c) // _LANE for c in src_of_dst[db * _LANE:(db + 1) * _LANE] if c >= 0})
            for db in range(len(src_of_dst) // _LANE)]


_RELAYOUT_ROWS = 512
_SHARD_BLOCKS = IN_SHARD_PAD // _LANE


def _relayout(name, x, to_segments):
    seg_of = _segment_of_shard_column()
    if to_segments:
        src_of_dst = seg_of
    else:
        src_of_dst = np.full(N_CHIPS * IN_SHARD_PAD, -1, np.int64)
        src_of_dst[seg_of[seg_of >= 0]] = np.nonzero(seg_of >= 0)[0]
    sources = _block_pairs(src_of_dst)
    n_dst = len(sources)
    col_map = jnp.asarray(src_of_dst.reshape(n_dst, 1, _LANE), jnp.int32)
    shard_blk = pl.BlockSpec((N_CHIPS, _RELAYOUT_ROWS, IN_SHARD_PAD), lambda i: (0, i, 0))
    seg_blk = pl.BlockSpec((_RELAYOUT_ROWS, P_END), lambda i: (i, 0))

    def shard_cols(ref, b):
        return ref.at[b // _SHARD_BLOCKS, :, pl.ds((b % _SHARD_BLOCKS) * _LANE, _LANE)]

    def seg_cols(ref, b):
        return ref.at[:, pl.ds(b * _LANE, _LANE)]

    src_cols, dst_cols = (shard_cols, seg_cols) if to_segments else (seg_cols, shard_cols)

    def body(x_ref, map_ref, o_ref):
        src_row = _iota((_LANE, _LANE), 0)
        for d in range(n_dst):
            acc = jnp.zeros((_RELAYOUT_ROWS, _LANE), F32)
            for sb in sources[d]:
                sel = (src_row + sb * _LANE == map_ref[d]).astype(x_ref.dtype)
                acc = acc + jnp.dot(src_cols(x_ref, sb)[...], sel, preferred_element_type=F32)
            dst_cols(o_ref, d)[...] = acc.astype(o_ref.dtype)

    rows = x.shape[-2]
    out_shape = (rows, P_END) if to_segments else (N_CHIPS, rows, IN_SHARD_PAD)
    return pl.pallas_call(
        body, name=name, grid=(rows // _RELAYOUT_ROWS,),
        in_specs=[shard_blk if to_segments else seg_blk, pl.BlockSpec(col_map.shape, lambda i: (0, 0, 0))],
        out_specs=seg_blk if to_segments else shard_blk, out_shape=jax.ShapeDtypeStruct(out_shape, x.dtype),
        compiler_params=_cparams(("arbitrary",)),
    )(x, col_map)


def shards_to_segments(w):
    return _relayout("shards_to_segments", w, True)


def segments_to_shards(w):
    return _relayout("segments_to_shards", w, False)


def mixers_forward(x, w_in, sp, dep=None, h1=None):
    if h1 is None:
        h1 = rmsnorm_fwd(x, sp["norm1_g"], dep)
    p2 = mm_proj(h1, w_in, 0)
    y_a = sgu_fwd(p2, sp["sgu_norm_g"], sp["w_spatial"], sp["b_spatial"])
    y_b, saved_b = gdn_forward(p2, sp["conv_w"], sp["a_log"], sp["dt_bias"], sp["o_norm_g"])
    y_c, outs_c, lses_c = dattn_fwd(p2, sp["q_norm_g"], sp["k_norm_g"], _alibi_slopes())
    mix = jnp.concatenate([y_a, y_b, y_c], axis=1)
    return mix, (x, h1, p2, saved_b, (outs_c, lses_c), mix)


def ffn_up(x, mix, w_out, w_gu, sp):
    x1 = mm_out(mix, w_out, 0, x)
    h2 = rmsnorm_fwd(x1, sp["norm2_g"])
    gu, act = mm_gu_swiglu(h2, w_gu)
    return x1, h2, gu, act


def ffn_forward(x, mix, wg, sp):
    x1, h2, gu, act = ffn_up(x, mix, wg["out"], wg["gu"], sp)
    x2 = mm_down(act, wg["down"], 0, x1)
    return x2, (x1, h2, gu, act)


def ffn_backward(dx2, wg, sp, saved, dep=None, on_weight_grads=None):
    x1, h2, gu, act = saved
    dx2, dx2_bf = dx2
    dgu = mm_dact_swiglu(dx2_bf, wg["down"], gu, dep)
    dw_down = mm_dwdown(act, dx2_bf)
    dw_gu = mm_dwgu(h2, dgu)
    tok = None if on_weight_grads is None else on_weight_grads(dw_gu, dw_down)
    dh2 = mm_dh2(dgu, wg["gu"], 0, tok)
    dx1, dnorm2 = rmsnorm_bwd(x1, sp["norm2_g"], dh2, dx2)
    return dx1, dnorm2, dw_gu, dw_down


def mixers_backward(dx1, wg, sp, saved, dep=None, on_weight_grads=None):
    x, h1, p2, saved_b, saved_c, mix = saved
    dx1, dx1_bf = dx1
    dmix = mm_dmix(dx1_bf, wg["out"], 0, dep)
    dw_out = mm_dwout(mix, dx1_bf)
    du, dv, dsg, dws, dbs = sgu_bwd(p2, sp["sgu_norm_g"], sp["w_spatial"], sp["b_spatial"], dmix)
    dseg_b, dconv, dal, ddtb, dog = gdn_backward(p2, sp["conv_w"], sp["a_log"], sp["dt_bias"], sp["o_norm_g"], saved_b, dmix)
    dcq, dck, dcv, dqg, dkg = dattn_bwd(p2, sp["q_norm_g"], sp["k_norm_g"], _alibi_slopes(), *saved_c, dmix)
    dp2 = jnp.concatenate([du, dv] + dseg_b + [dcq, dck, dcv], axis=1)
    dw_in = segments_to_shards(mm_dwin(h1, dp2))
    tok = None if on_weight_grads is None else on_weight_grads(dw_in, dw_out)
    dh1 = mm_dh1(dp2, wg["in"], 0, tok)
    dx, dnorm1 = rmsnorm_bwd(x, sp["norm1_g"], dh1, dx1)
    small = {"norm1_g": dnorm1, "sgu_norm_g": dsg, "w_spatial": dws, "b_spatial": dbs, "conv_w": dconv, "a_log": dal,
             "dt_bias": ddtb, "o_norm_g": dog, "q_norm_g": dqg, "k_norm_g": dkg}
    return dx, dw_in, dw_out, small


_HBM = pl.BlockSpec(memory_space=pltpu.HBM)
_MESH = pl.DeviceIdType.MESH


def _place():
    x, y, c = lax.axis_index("x"), lax.axis_index("y"), lax.axis_index("c")
    chips = [(1 - x, y), (x, 1 - y), (1 - x, 1 - y)]
    return x, y, c, chips


def _rcopy(src, dst, ssem, rsem, dev):
    return pltpu.make_async_remote_copy(src_ref=src, dst_ref=dst, send_sem=ssem, recv_sem=rsem, device_id=dev,
                                        device_id_type=_MESH)


_SEM = pl.BlockSpec(memory_space=pltpu.SEMAPHORE)
_SIDE_EFFECT = pltpu.SideEffectType.DATAFLOW_SIDE_EFFECTING


def _in_hbm(a):
    return pltpu.with_memory_space_constraint(a, pltpu.HBM)


def _split_copy(name, srcs, land_shapes, n_sems, copies):
    n, m = len(srcs), len(land_shapes)
    thru = [pltpu.HBM(a.shape, a.dtype) for a in srcs] + [pltpu.HBM(s.shape, s.dtype) for s in land_shapes]
    sems = (pltpu.SemaphoreType.DMA((n_sems,)), pltpu.SemaphoreType.DMA((n_sems,)))

    def start(dep=None):
        deps = [] if dep is None else [dep]

        def body(*refs):
            ins, lands = refs[:n], refs[n:n + m]
            ssem, rsem, token = refs[n + m + len(deps)], refs[n + m + len(deps) + 1], refs[-1]
            for cp in copies(ins, lands, ssem, rsem)[0]:
                cp.start()
            token[...] = jnp.zeros_like(token)

        out = pl.pallas_call(
            body, name=name + "_start", out_shape=(*sems, *thru, jax.ShapeDtypeStruct((8, HD), F32)),
            in_specs=[_HBM] * (n + m) + [pl.BlockSpec(memory_space=pl.ANY)] * len(deps),
            out_specs=(_SEM, _SEM, *[_HBM] * (n + m), pl.BlockSpec(memory_space=pltpu.VMEM)),
            input_output_aliases={i: 2 + i for i in range(n + m)},
            compiler_params=pltpu.CompilerParams(has_side_effects=_SIDE_EFFECT),
        )(*[_in_hbm(a) for a in srcs], *[_in_hbm(lax.empty(s.shape, s.dtype)) for s in land_shapes], *deps)
        return out[:-1], out[-1]

    def wait(state, after):
        def body(*refs):
            ins, lands, ssem, rsem = refs[:n], refs[n:n + m], refs[n + m], refs[n + m + 1]
            sent, arrivals = copies(ins, lands, ssem, rsem)
            for cp in sent:
                cp.wait_send()
            for cp in arrivals:
                cp.wait_recv()

        out = pl.pallas_call(
            body, name=name + "_wait", out_shape=tuple(thru),
            in_specs=[_HBM] * (n + m) + [_SEM, _SEM, pl.BlockSpec(memory_space=pl.ANY)], out_specs=[_HBM] * (n + m),
            input_output_aliases={i: i for i in range(n + m)},
            compiler_params=pltpu.CompilerParams(has_side_effects=_SIDE_EFFECT),
        )(*state[2:], state[0], state[1], after)
        return list(out[:n]), list(out[n:])

    return start, wait


def gather_direct(shards, tag):
    n = len(shards)

    def copies(ins, lands, ssem, rsem):
        x, y, c, chips = _place()
        s = 2 * x + y
        sibling = (x, y, 1 - c)
        sent, arrivals = [], []
        for a in range(n):
            for u in range(2):
                cp = _rcopy(ins[a].at[u], lands[a].at[s, u], ssem.at[5 * a + u], rsem.at[5 * a + u], sibling)
                sent.append(cp)
                arrivals.append(cp)
            for j, (cx, cy) in enumerate(chips):
                k = 5 * a + 2 + j
                sent.append(_rcopy(ins[a].at[c], lands[a].at[s, c], ssem.at[k], rsem.at[k], (cx, cy, c)))
                arrivals.append(_rcopy(ins[a].at[c], lands[a].at[2 * cx + cy, c], ssem.at[k], rsem.at[k], (cx, cy, c)))
        return sent, arrivals

    lands = [jax.ShapeDtypeStruct((N_CHIPS,) + w.shape, w.dtype) for w in shards]
    return _split_copy("gather_direct_" + tag, shards, lands, 5 * n, copies)


def pass_to_sibling(lands):
    n = len(lands)

    def body(*refs):
        ins = refs[:n]
        ssem, rsem = refs[2 * n:]
        x, y, c, chips = _place()
        sibling = (x, y, 1 - c)
        cps, arrivals = [], []
        for a in range(n):
            for j, (cx, cy) in enumerate(chips):
                t = 2 * cx + cy
                cps.append(_rcopy(ins[a].at[t, c], ins[a].at[t, c], ssem.at[a, j], rsem.at[a, j], sibling))
                arrivals.append(_rcopy(ins[a].at[t, c], ins[a].at[t, 1 - c], ssem.at[a, j], rsem.at[a, j], sibling))
        for cp in cps:
            cp.start()
        for cp, ar in zip(cps, arrivals):
            cp.wait_send()
            ar.wait_recv()

    return pl.pallas_call(
        body, name="pass_to_sibling", in_specs=[_HBM] * n, out_specs=[_HBM] * n,
        out_shape=[jax.ShapeDtypeStruct(a.shape, a.dtype) for a in lands], input_output_aliases={a: a for a in range(n)},
        scratch_shapes=[pltpu.SemaphoreType.DMA((n, 3)), pltpu.SemaphoreType.DMA((n, 3))],
    )(*lands)


def exchange_halves(grads, tag):
    n = len(grads)

    def copies(ins, lands, ssem, rsem):
        x, y, c, _ = _place()
        cps = []
        for a in range(n):
            h = grads[a].shape[1] // 2
            cps.append(_rcopy(ins[a].at[:, pl.ds((1 - c) * h, h)], lands[a], ssem.at[a], rsem.at[a], (x, y, 1 - c)))
        return cps, cps

    lands = [jax.ShapeDtypeStruct((g.shape[0], g.shape[1] // 2, g.shape[2]), g.dtype) for g in grads]
    return _split_copy("exchange_halves_" + tag, grads, lands, n, copies)


def scatter_direct(parts, tag):
    n = len(parts)

    def copies(ins, lands, ssem, rsem):
        x, y, c, chips = _place()
        cps = [_rcopy(ins[a].at[2 * cx + cy], lands[a].at[j], ssem.at[3 * a + j], rsem.at[3 * a + j], (cx, cy, c))
               for a in range(n) for j, (cx, cy) in enumerate(chips)]
        return cps, cps

    lands = [jax.ShapeDtypeStruct((3,) + p.shape[1:], p.dtype) for p in parts]
    return _split_copy("scatter_direct_" + tag, parts, lands, 3 * n, copies)


def share_halves(halves):
    n = len(halves)

    def body(*refs):
        ins, outs = refs[:n], refs[n:2 * n]
        ssem, rsem = refs[2 * n:]
        x, y, c, _ = _place()
        cps = [_rcopy(ins[i], outs[i], ssem.at[i], rsem.at[i], (x, y, 1 - c)) for i in range(n)]
        for cp in cps:
            cp.start()
        for cp in cps:
            cp.wait()

    return pl.pallas_call(
        body, name="share_halves", in_specs=[_HBM] * n, out_specs=[_HBM] * n,
        out_shape=[jax.ShapeDtypeStruct(h.shape, h.dtype) for h in halves],
        scratch_shapes=[pltpu.SemaphoreType.DMA((n,)), pltpu.SemaphoreType.DMA((n,))],
    )(*halves)


_ADAMW_BLOCK_BYTES = 3 << 19


def adamw_shard(w, m, v, mine, theirs, c, name):
    _, r, cw = w.shape
    h, cg = mine[0].shape
    tr = next(t for t in (256, 176, 128) if h % t == 0 and t * cg * 4 <= _ADAMW_BLOCK_BYTES)
    nb = h // tr
    wblk = pl.BlockSpec((None, tr, cw), lambda l, i, c_ref: (l, i, 0))
    gblk = lambda layer, own: pl.BlockSpec((tr, cg), lambda l, i, c_ref: (_held_block(l, i, c_ref, layer, own, nb), 0))
    return _adamw_halves(w, m, v, mine, theirs, c, name, (DEPTH, r // tr), wblk, gblk, nb, cw)


def _held_block(l, i, c_ref, layer, own, nb):
    in_use = (l == layer) & (((i // nb) == c_ref[0]) == own)
    return jnp.where(in_use, i % nb, 0)


def _adamw_halves(w, m, v, mine, theirs, c, name, grid, wblk, gblk, nb, cw):
    def body(c_ref, w_ref, m_ref, v_ref, m0, m1, t0, t1, g_ref, d_ref, nm_ref, nv_ref):
        is_mine = (pl.program_id(1) // nb) == c_ref[0]
        first = pl.program_id(0) == 0
        gg = jnp.where(is_mine, jnp.where(first, m0[:, :cw], m1[:, :cw]), jnp.where(first, t0[:, :cw], t1[:, :cw]))
        nm = ADAM_B1 * m_ref[...] + (1.0 - ADAM_B1) * gg
        nv = ADAM_B2 * v_ref[...] + (1.0 - ADAM_B2) * (gg * gg)
        m_hat = nm / (1.0 - ADAM_B1 ** ADAM_STEP)
        v_hat = nv / (1.0 - ADAM_B2 ** ADAM_STEP)
        g_ref[...] = gg
        d_ref[...] = -ADAM_LR * (m_hat / (jnp.sqrt(v_hat) + ADAM_EPS) + ADAM_WD * w_ref[...])
        nm_ref[...] = nm
        nv_ref[...] = nv

    return pl.pallas_call(
        body, name=name,
        grid_spec=pltpu.PrefetchScalarGridSpec(
            num_scalar_prefetch=1, grid=grid,
            in_specs=[wblk] * 3 + [gblk(0, True), gblk(1, True), gblk(0, False), gblk(1, False)], out_specs=[wblk] * 4),
        out_shape=[jax.ShapeDtypeStruct(w.shape, F32)] * 4, compiler_params=_cparams(("arbitrary", "arbitrary")),
    )(c, w, m, v, mine[0], mine[1], theirs[0], theirs[1])


def adamw_shard_t(wt, mt, vt, mine_t, theirs_t, c, name):
    _, cw, r = wt.shape
    h = mine_t[0].shape[1]
    tc = 256
    nb = h // tc
    wblk = pl.BlockSpec((None, cw, tc), lambda l, j, c_ref: (l, 0, j))
    gblk = lambda layer, own: pl.BlockSpec((cw, tc), lambda l, j, c_ref: (0, _held_block(l, j, c_ref, layer, own, nb)))
    return _adamw_halves(wt, mt, vt, mine_t, theirs_t, c, name, (DEPTH, r // tc), wblk, gblk, nb, cw)


def _half_rows(h, cols):
    for tr in (512, 256, 352, 128, 64):
        if h % tr == 0 and tr * cols * 4 <= 6 * 1024 * 1024:
            return tr
    raise ValueError((h, cols))


def add_sibling(grad, recv, c):
    _, r, cols = grad.shape
    h = r // 2
    tr = _half_rows(h, cols)
    nb = h // tr

    def body(c_ref, g_ref, r_ref, o_ref):
        o_ref[...] = (g_ref[...].astype(F32) + r_ref[...].astype(F32)).astype(BF16)

    return pl.pallas_call(
        body, name="add_sibling",
        grid_spec=pltpu.PrefetchScalarGridSpec(
            num_scalar_prefetch=1, grid=(N_CHIPS, nb),
            in_specs=[pl.BlockSpec((None, tr, cols), lambda t, i, c_ref: (t, c_ref[0] * nb + i, 0)),
                      pl.BlockSpec((None, tr, cols), lambda t, i, c_ref: (t, i, 0))],
            out_specs=pl.BlockSpec((None, tr, cols), lambda t, i, c_ref: (t, i, 0))),
        out_shape=jax.ShapeDtypeStruct((N_CHIPS, h, cols), BF16), compiler_params=_cparams(("arbitrary", "arbitrary")),
    )(c, grad, recv)


def add_chips(part, recv, s):
    _, h, cols = part.shape
    tr = _half_rows(h, cols)

    def body(s_ref, p_ref, r_ref, o_ref):
        o_ref[...] = ((p_ref[...].astype(F32) + r_ref[0].astype(F32)) + r_ref[1].astype(F32)) + r_ref[2].astype(F32)

    return pl.pallas_call(
        body, name="add_chips",
        grid_spec=pltpu.PrefetchScalarGridSpec(
            num_scalar_prefetch=1, grid=(h // tr,),
            in_specs=[pl.BlockSpec((None, tr, cols), lambda i, s_ref: (s_ref[0], i, 0)),
                      pl.BlockSpec((3, tr, cols), lambda i, s_ref: (0, i, 0))],
            out_specs=pl.BlockSpec((tr, cols), lambda i, s_ref: (i, 0))),
        out_shape=jax.ShapeDtypeStruct((h, cols), F32), compiler_params=_cparams(("arbitrary",)),
    )(s, part, recv)


def allreduce_small(vec):
    rows = vec.shape[0]

    def body(v_ref, o_ref, buf, ssem, rsem, lsem):
        x, y, c, chips = _place()
        me, sibling = (x, y, c), (x, y, 1 - c)

        def blk(px, py, pc):
            return buf.at[4 * px + 2 * py + pc]

        def copy(k, block, to, src=None):
            return _rcopy(blk(*block) if src is None else src, blk(*block), ssem.at[k], rsem.at[k], to)

        mine = pltpu.make_async_copy(v_ref, blk(*me), lsem)
        mine.start()
        first = [copy(0, me, sibling, src=v_ref)] + [copy(1 + j, me, (*chip, c), src=v_ref) for j, chip in enumerate(chips)]
        for cp in first:
            cp.start()
        passed = [copy(4 + j, (*chip, c), sibling) for j, chip in enumerate(chips)]
        for j, chip in enumerate(chips):
            copy(1 + j, (*chip, c), me).wait_recv()
            passed[j].start()
        copy(0, sibling, me).wait_recv()
        for j, chip in enumerate(chips):
            copy(4 + j, (*chip, 1 - c), me).wait_recv()
        for cp in first + passed:
            cp.wait_send()
        mine.wait()
        acc = buf[0]
        for d in range(1, N_DEV):
            acc = acc + buf[d]
        o_ref[...] = acc

    vm = pl.BlockSpec(memory_space=pltpu.VMEM)
    return pl.pallas_call(
        body, name="allreduce_small", in_specs=[vm], out_specs=vm, out_shape=jax.ShapeDtypeStruct(vec.shape, F32),
        scratch_shapes=[pltpu.VMEM((N_DEV, rows, HD), F32), pltpu.SemaphoreType.DMA((7,)), pltpu.SemaphoreType.DMA((7,)),
                        pltpu.SemaphoreType.DMA],
        compiler_params=pltpu.CompilerParams(vmem_limit_bytes=VMEM_LIMIT),
    )(vec)


SMALL_NAMES = ("norm1_g", "sgu_norm_g", "w_spatial", "b_spatial", "conv_w", "a_log", "dt_bias", "o_norm_g", "q_norm_g",
               "k_norm_g", "norm2_g")


def small_params(l, p, conv_full):
    return {"norm1_g": p["norm1_g"][l][None], "sgu_norm_g": p["sgu_norm_g"][l][:, None, :], "w_spatial": p["w_spatial"][l],
            "b_spatial": p["b_spatial"][l][..., None], "conv_w": conv_full[l], "a_log": p["a_log"][l], "dt_bias": p["dt_bias"][l],
            "o_norm_g": p["o_norm_g"][l][None], "q_norm_g": p["q_norm_g"][l][None], "k_norm_g": p["k_norm_g"][l][None],
            "norm2_g": p["norm2_g"][l][None]}


_PACK_TILE = 8 * HD


def _pack(arrays):
    flat = jnp.concatenate([a.reshape(-1) for a in arrays])
    pad = -flat.shape[0] % _PACK_TILE
    return jnp.pad(flat, (0, pad)).reshape(-1, HD)


def _unpack(packed, shapes):
    flat, out, off = packed.reshape(-1), [], 0
    for shp in shapes:
        n = int(np.prod(shp))
        out.append(flat[off:off + n].reshape(shp))
        off += n
    return out


WEIGHT_ORDER = ("norm1_g", "w_in", "sgu_norm_g", "w_spatial", "b_spatial", "conv_w", "a_log", "dt_bias", "o_norm_g", "q_norm_g",
                "k_norm_g", "w_out", "norm2_g", "w_gate_up", "w_down")


def kernel(x, norm1_g, w_in, sgu_norm_g, w_spatial, b_spatial, conv_w, a_log, dt_bias, o_norm_g, q_norm_g, k_norm_g, w_out, norm2_g, w_gate_up, w_down, loss_target, m_norm1_g, m_w_in, m_sgu_norm_g, m_w_spatial, m_b_spatial, m_conv_w, m_a_log, m_dt_bias, m_o_norm_g, m_q_norm_g, m_k_norm_g, m_w_out, m_norm2_g, m_w_gate_up, m_w_down, v_norm1_g, v_w_in, v_sgu_norm_g, v_w_spatial, v_b_spatial, v_conv_w, v_a_log, v_dt_bias, v_o_norm_g, v_q_norm_g, v_k_norm_g, v_w_out, v_norm2_g, v_w_gate_up, v_w_down):
    w = dict(norm1_g=norm1_g, w_in=w_in, sgu_norm_g=sgu_norm_g, w_spatial=w_spatial, b_spatial=b_spatial, conv_w=conv_w,
             a_log=a_log, dt_bias=dt_bias, o_norm_g=o_norm_g, q_norm_g=q_norm_g, k_norm_g=k_norm_g, w_out=w_out,
             norm2_g=norm2_g, w_gate_up=w_gate_up, w_down=w_down)
    m = dict(norm1_g=m_norm1_g, w_in=m_w_in, sgu_norm_g=m_sgu_norm_g, w_spatial=m_w_spatial, b_spatial=m_b_spatial,
             conv_w=m_conv_w, a_log=m_a_log, dt_bias=m_dt_bias, o_norm_g=m_o_norm_g, q_norm_g=m_q_norm_g, k_norm_g=m_k_norm_g,
             w_out=m_w_out, norm2_g=m_norm2_g, w_gate_up=m_w_gate_up, w_down=m_w_down)
    v = dict(norm1_g=v_norm1_g, w_in=v_w_in, sgu_norm_g=v_sgu_norm_g, w_spatial=v_w_spatial, b_spatial=v_b_spatial,
             conv_w=v_conv_w, a_log=v_a_log, dt_bias=v_dt_bias, o_norm_g=v_o_norm_g, q_norm_g=v_q_norm_g, k_norm_g=v_k_norm_g,
             w_out=v_w_out, norm2_g=v_norm2_g, w_gate_up=v_w_gate_up, w_down=v_w_down)
    chip = (2 * lax.axis_index("x") + lax.axis_index("y")).astype(jnp.int32)
    core = lax.axis_index("c").astype(jnp.int32)

    in_pad = IN_SHARD_PAD - IN_SHARD
    w_in_pad = jnp.pad(w_in, ((0, 0), (0, 0), (0, in_pad)))

    halves_of = lambda a: a.reshape(2, a.shape[0] // 2, a.shape[1])
    start_0, wait_0 = gather_direct([halves_of(w_in_pad[0].astype(BF16)), halves_of(conv_w[0])], "mix0")
    state_0, token_0 = start_0()
    bf_halves = lambda a: halves_of((a + token_0[0, 0]).astype(BF16))

    def ffn_shards(l):
        return [bf_halves(w_gate_up[l]), bf_halves(w_down[l]), bf_halves(w_out[l])]

    def mixer_shards(l):
        return [bf_halves(w_in_pad[l]), halves_of(conv_w[l])]

    def mixer_weights(g):
        g_in, g_conv = g
        return (shards_to_segments(g_in.reshape(N_CHIPS, D, IN_SHARD_PAD))[None],
                g_conv.reshape(N_CHIPS, B_CONV, -1).transpose(1, 0, 2).reshape(B_CONV, 3 * B_WIDTH))

    def ffn_weights(g, w_in_seg):
        g_gu, g_down, g_out = g
        return {"in": w_in_seg, "out": g_out.reshape(1, D, D), "gu": g_gu.reshape(1, N_CHIPS, D, GU_SHARD),
                "down": None if g_down is None else g_down.reshape(1, FFN, D)}

    def layer_params(l, conv_full):
        return small_params(0, {n: w[n][l:l + 1] for n in SMALL_NAMES if n != "conv_w"}, conv_full[None])

    gu0, down0, out0 = ffn_shards(0)
    start_a, wait_a = gather_direct([gu0, out0], "ffn0")
    start_b, wait_b = gather_direct([down0] + mixer_shards(1), "mid")
    start_c, wait_c = gather_direct(ffn_shards(1), "ffn1")
    state_a, token_a = start_a(token_0)
    state_b, token_b = start_b(token_a)
    state_c, token_c = start_c(token_b)
    h1_0 = rmsnorm_fwd(x[0], norm1_g[0][None], token_c)
    w_in0, conv0 = mixer_weights(pass_to_sibling(wait_0(state_0, h1_0)[1]))
    sps = [layer_params(0, conv0), None]
    mix0, saved_m0 = mixers_forward(x[0], w_in0, sps[0], h1=h1_0)
    g_gu0, g_out0 = pass_to_sibling(wait_a(state_a, mix0)[1])
    wg0 = ffn_weights((g_gu0, None, g_out0), w_in0)
    x1_0, h2_0, gu_0, act_0 = ffn_up(x[0], mix0, wg0["out"], wg0["gu"], sps[0])
    g_down0, g_in1, g_conv1 = pass_to_sibling(wait_b(state_b, act_0)[1])
    wg0["down"] = g_down0.reshape(1, FFN, D)
    x1 = mm_down(act_0, wg0["down"], 0, x1_0)
    saved_f0 = (x1_0, h2_0, gu_0, act_0)
    w_in1, conv1 = mixer_weights((g_in1, g_conv1))
    sps[1] = layer_params(1, conv1)
    mix1, saved_m1 = mixers_forward(x1, w_in1, sps[1])
    wg1 = ffn_weights(pass_to_sibling(wait_c(state_c, mix1)[1]), w_in1)
    x2, saved_f1 = ffn_forward(x1, mix1, wg1, sps[1])
    saved1 = (saved_m1, saved_f1)
    dx, loss_tile = loss_and_grad(x2, loss_target[0])

    def sibling_exchange(tag, by_chip):
        cell = {}

        def on_weight_grads(*dws):
            start, wait = exchange_halves([f(t) for f, t in zip(by_chip, dws)], tag)
            state, token = start()
            cell["wait"] = functools.partial(wait, state)
            return token

        return on_weight_grads, cell

    def start_scatter(cell, after, tag, dep=None):
        grads, from_sibling = cell["wait"](after[0])
        parts = [add_sibling(g, r, core.reshape(1)) for g, r in zip(grads, from_sibling)]
        start, wait = scatter_direct(parts, tag)
        state, token = start(dep)
        return functools.partial(wait, state), token

    smalls = [None] * DEPTH
    same = lambda t: t
    ffn_by_chip = (same, lambda t: t.reshape(N_CHIPS, DOWN_SHARD, D))
    mix_by_chip = (same, lambda t: t.reshape(N_CHIPS, OUT_SHARD, D))
    hook, cell_f1 = sibling_exchange("ffn1", ffn_by_chip)
    dx1, dnorm2_1, _, _ = ffn_backward(dx, wg1, sps[1], saved1[1], on_weight_grads=hook)
    wait_f1, tok_f1 = start_scatter(cell_f1, dx1, "ffn1")
    hook, cell_m1 = sibling_exchange("mix1", mix_by_chip)
    dx, _, _, small1 = mixers_backward(dx1, wg1, sps[1], saved1[0], dep=tok_f1, on_weight_grads=hook)
    smalls[1] = {**small1, "norm2_g": dnorm2_1}
    wait_m1, tok_m1 = start_scatter(cell_m1, dx, "mix1")
    hook, cell_f0 = sibling_exchange("ffn0", ffn_by_chip)
    dx1, dnorm2_0, _, _ = ffn_backward(dx, wg0, sps[0], saved_f0, dep=tok_m1, on_weight_grads=hook)
    wait_f0, tok_f0 = start_scatter(cell_f0, dx1, "ffn0")
    hook, cell_m0 = sibling_exchange("mix0", mix_by_chip)
    dx, _, _, small0 = mixers_backward(dx1, wg0, sps[0], saved_m0, dep=tok_f0, on_weight_grads=hook)
    smalls[0] = {**small0, "norm2_g": dnorm2_0}

    grad, delta, new_m, new_v = {}, {}, {}, {}
    stacked = [jnp.stack([smalls[l][n] for l in range(DEPTH)]) for n in SMALL_NAMES]
    total = allreduce_small(_pack(stacked + [loss_tile[0, :1]]))
    shapes = [(DEPTH, B_CONV, 3 * B_WIDTH) if n == "conv_w" else w[n].shape for n in SMALL_NAMES]
    small_grads = dict(zip(SMALL_NAMES, _unpack(total, shapes + [(1,)])[:-1]))
    loss = _unpack(total, shapes + [(1,)])[-1][0]
    conv_cols = conv_w.shape[-1]
    small_grads["conv_w"] = lax.dynamic_slice_in_dim(small_grads["conv_w"], chip * conv_cols, conv_cols, axis=2)
    grad.update(small_grads)
    sshapes = [w[n].shape for n in SMALL_NAMES]
    packed = [_pack([d[n] for n in SMALL_NAMES]) for d in (w, grad, m, v)]
    for dst, t in zip((delta, new_m, new_v), adamw(*packed, "adamw_small")):
        dst.update(zip(SMALL_NAMES, _unpack(t, sshapes)))

    wait_m0, tok_m0 = start_scatter(cell_m0, dx, "mix0", dep=total)
    (pf0, rf0), (pm1, rm1), (pf1, rf1) = (wt(tok_m0) for wt in (wait_f0, wait_m1, wait_f1))

    def reduce_group(parts, from_chips):
        mine = [add_chips(p, r, chip.reshape(1)) for p, r in zip(parts, from_chips)]
        return mine, list(share_halves(mine))

    mine_f, theirs_f = reduce_group(pf0 + pf1 + pm1, rf0 + rf1 + rm1)
    for a, n in enumerate(("w_gate_up", "w_down")):
        grad[n], delta[n], new_m[n], new_v[n] = adamw_shard(w[n], m[n], v[n], [mine_f[a], mine_f[2 + a]],
                                                            [theirs_f[a], theirs_f[2 + a]], core.reshape(1), "adamw_" + n)

    mine_m0, theirs_m0 = reduce_group(*wait_m0(new_v["w_down"]))
    tr_ = lambda t: jnp.swapaxes(t, -1, -2)
    cut = lambda t: tr_(t[:, :IN_SHARD])
    res = adamw_shard_t(tr_(w_in), tr_(m_w_in), tr_(v_w_in), [cut(mine_m0[0]), cut(mine_f[4])],
                        [cut(theirs_m0[0]), cut(theirs_f[4])], core.reshape(1), "adamw_w_in")
    grad["w_in"], delta["w_in"], new_m["w_in"], new_v["w_in"] = (tr_(t) for t in res)
    grad["w_out"], delta["w_out"], new_m["w_out"], new_v["w_out"] = adamw_shard(
        w_out, m_w_out, v_w_out, [mine_m0[1], mine_f[5]], [theirs_m0[1], theirs_f[5]], core.reshape(1), "adamw_w_out")

    out = [loss, dx[0][None]]
    for d in (grad, delta, new_m, new_v):
        out += [d[n] for n in WEIGHT_ORDER]
    return tuple(out)
```

```python
import functools
import math

import numpy as np
import jax
import jax.numpy as jnp
from jax import lax
from jax.experimental import pallas as pl
from jax.experimental.pallas import tpu as pltpu

F32 = jnp.float32
BF16 = jnp.bfloat16
HI = lax.Precision.HIGH

T = 2048
D = 2048
DEPTH = 2
HD = 128
A_GROUPS, A_WIDTH, A_CHUNK = 4, 512, 128
B_HEADS, B_WIDTH, B_CONV, B_CHUNK = 6, 768, 4, 64
C_HEADS, C_WIDTH, C_BLOCK = 6, 768, 128
C_BRANCHES = ((128, 1), (512, 4), (2048, 16))
FFN = 5632
IN_TOTAL = 6412
EPS = 1e-6
N_CHIPS = 4
N_DEV = 8
IN_SHARD = IN_TOTAL // N_CHIPS
IN_SHARD_PAD = 1664
GU_SHARD = 2 * FFN // N_CHIPS
OUT_SHARD = D // N_CHIPS
DOWN_SHARD = FFN // N_CHIPS
P_AU, P_AV, P_BQ, P_BK, P_BV, P_BG, P_BB, P_CQ, P_CK, P_CV, P_END = (
    0, 512, 1024, 1792, 2560, 3328, 4096, 4224, 4992, 5760, 6528)
GATE_COLS = 4108
VMEM_LIMIT = 56 * 1024 * 1024

ADAM_LR, ADAM_B1, ADAM_B2, ADAM_EPS, ADAM_WD, ADAM_STEP = 0.001, 0.9, 0.999, 1e-08, 0.01, 10


def _cparams(sem, vmem=VMEM_LIMIT):
    return pltpu.CompilerParams(dimension_semantics=sem, vmem_limit_bytes=vmem)


def _dims(nd, ta, tb):
    off = nd - 2
    ca = off + (0 if ta else 1)
    cb = off + (1 if tb else 0)
    batch = ((0,), (0,)) if nd == 3 else ((), ())
    return (((ca,), (cb,)), batch)


def _raw_mm(a, b, ta, tb, hi):
    if hi:
        return lax.dot_general(a, b, _dims(a.ndim, ta, tb), precision=HI, preferred_element_type=F32)
    return lax.dot_general(a.astype(BF16), b.astype(BF16), _dims(a.ndim, ta, tb), preferred_element_type=F32)


@functools.partial(jax.custom_vjp, nondiff_argnums=(2, 3, 4))
def _mm(a, b, ta=False, tb=False, hi=False):
    return _raw_mm(a, b, ta, tb, hi)


def _mm_fwd(a, b, ta, tb, hi):
    return _raw_mm(a, b, ta, tb, hi), (a, b)


def _mm_bwd(ta, tb, hi, res, g):
    a, b = res
    da = _raw_mm(g, b, False, not tb, False) if not ta else _raw_mm(b, g, tb, True, False)
    db = _raw_mm(a, g, not ta, False, False) if not tb else _raw_mm(g, a, True, ta, False)
    return da.astype(a.dtype), db.astype(b.dtype)


_mm.defvjp(_mm_fwd, _mm_bwd)


def _rms(x, g):
    return x * lax.rsqrt(jnp.mean(x * x, axis=-1, keepdims=True) + EPS) * g


def _gelu(x):
    return 0.5 * x * (1.0 + jnp.tanh(math.sqrt(2.0 / math.pi) * (x + 0.044715 * (x * x * x))))


def _sigmoid(x):
    return 1.0 / (1.0 + jnp.exp(-x))


def _silu(x):
    return x * _sigmoid(x)


def _softplus(x):
    return jnp.maximum(x, 0.0) + jnp.log(1.0 + jnp.exp(-jnp.abs(x)))


def _iota(shape, dim):
    return lax.broadcasted_iota(jnp.int32, shape, dim)


def _sgu_fn(u, v, sg, w, b):
    nc = T // A_CHUNK
    ug = _gelu(u)
    vn = _rms(_gelu(v), sg)
    causal = _iota((A_CHUNK, A_CHUNK), 0) >= _iota((A_CHUNK, A_CHUNK), 1)
    wm = jnp.where(causal, w, 0.0)
    wb = jnp.broadcast_to(wm[None], (nc, A_CHUNK, A_CHUNK))
    z = _mm(wb, vn.reshape(nc, A_CHUNK, HD)) + b[None]
    return ug * z.reshape(T, HD)


def _sgu_specs():
    col = lambda off: pl.BlockSpec((T, HD), lambda g, off=off: (0, off + g))
    par = [pl.BlockSpec((None, 1, HD), lambda g: (g, 0, 0)),
           pl.BlockSpec((None, A_CHUNK, A_CHUNK), lambda g: (g, 0, 0)),
           pl.BlockSpec((None, A_CHUNK, 1), lambda g: (g, 0, 0))]
    return col, par


def sgu_fwd(p2, sg, w, b):
    col, par = _sgu_specs()

    def body(u_ref, v_ref, sg_ref, w_ref, b_ref, y_ref):
        y_ref[...] = _sgu_fn(u_ref[...], v_ref[...], sg_ref[...], w_ref[...], b_ref[...]).astype(BF16)

    return pl.pallas_call(
        body, name="sgu_fwd", grid=(A_GROUPS,),
        in_specs=[col(P_AU // HD), col(P_AV // HD)] + par,
        out_specs=pl.BlockSpec((T, HD), lambda g: (0, g)),
        out_shape=jax.ShapeDtypeStruct((T, A_WIDTH), BF16),
        compiler_params=_cparams(("arbitrary",)),
    )(p2, p2, sg, w, b)


def sgu_bwd(p2, sg, w, b, dmix):
    col, par = _sgu_specs()

    def body(u_ref, v_ref, sg_ref, w_ref, b_ref, dy_ref, du_ref, dv_ref, dsg_ref, dw_ref, db_ref):
        _, vjp = jax.vjp(_sgu_fn, u_ref[...], v_ref[...], sg_ref[...], w_ref[...], b_ref[...])
        du, dv, dsg, dw, db = vjp(dy_ref[...])
        du_ref[...] = du.astype(BF16)
        dv_ref[...] = dv.astype(BF16)
        dsg_ref[...] = dsg
        dw_ref[...] = dw
        db_ref[...] = db

    gcol = pl.BlockSpec((T, HD), lambda g: (0, g))
    return pl.pallas_call(
        body, name="sgu_bwd", grid=(A_GROUPS,),
        in_specs=[col(P_AU // HD), col(P_AV // HD)] + par + [gcol],
        out_specs=[gcol, gcol] + par,
        out_shape=[jax.ShapeDtypeStruct((T, A_WIDTH), BF16), jax.ShapeDtypeStruct((T, A_WIDTH), BF16),
                   jax.ShapeDtypeStruct((A_GROUPS, 1, HD), F32), jax.ShapeDtypeStruct((A_GROUPS, A_CHUNK, A_CHUNK), F32),
                   jax.ShapeDtypeStruct((A_GROUPS, A_CHUNK, 1), F32)],
        compiler_params=_cparams(("arbitrary",)),
    )(p2, p2, sg, w, b, dmix)


def _attn_fn(q, k, v, qg, kg, slope, *, dil, nb):
    n = T // C_BLOCK
    qb = _rms(q, qg).reshape(n, C_BLOCK, HD)
    kb = _rms(k, kg).reshape(n, C_BLOCK, HD)
    vb = v.reshape(n, C_BLOCK, HD)
    scale = HD ** -0.5
    qi = _iota((n, C_BLOCK, C_BLOCK), 1)
    kj = _iota((n, C_BLOCK, C_BLOCK), 2)
    sl = slope[None] * float(dil)
    d_cur = qi - kj
    sc = jnp.where(d_cur >= 0, _mm(qb, kb, tb=True) * scale - sl * d_cur.astype(F32), -jnp.inf)
    mx = jnp.max(sc, axis=-1, keepdims=True)
    if nb > 1:
        kp = jnp.concatenate([jnp.zeros((1, C_BLOCK, HD), F32), kb[:-1]], axis=0)
        vp = jnp.concatenate([jnp.zeros((1, C_BLOCK, HD), F32), vb[:-1]], axis=0)
        has_prev = (_iota((n, C_BLOCK, C_BLOCK), 0) % nb) > 0
        d_prev = C_BLOCK + qi - kj
        sp = jnp.where((kj >= qi) & has_prev, _mm(qb, kp, tb=True) * scale - sl * d_prev.astype(F32), -jnp.inf)
        mx = jnp.maximum(mx, jnp.max(sp, axis=-1, keepdims=True))
    p = jnp.exp(sc - mx)
    den = jnp.sum(p, axis=-1, keepdims=True)
    if nb > 1:
        pp = jnp.exp(sp - mx)
        den = den + jnp.sum(pp, axis=-1, keepdims=True)
    inv = 1.0 / den
    out = _mm(p * inv, vb)
    if nb > 1:
        out = out + _mm(pp * inv, vp)
    lse = mx + jnp.log(den)
    return out.reshape(T, HD), jnp.broadcast_to(lse, (n, C_BLOCK, HD)).reshape(T, HD)


def _combine_fn(o1, o2, o3, l1, l2, l3):
    mx = jnp.maximum(jnp.maximum(l1, l2), l3)
    e1, e2, e3 = jnp.exp(l1 - mx), jnp.exp(l2 - mx), jnp.exp(l3 - mx)
    r = 1.0 / (e1 + e2 + e3)
    return (e1 * r) * o1 + (e2 * r) * o2 + (e3 * r) * o3


def _branch_blocks(dil):
    return -(-(T // dil) // C_BLOCK)


def _load_branch_order(ref, dil):
    if dil == 1:
        return ref[...]
    seg = T // dil
    return jnp.concatenate([ref[pl.ds(r, seg, stride=dil), :] for r in range(dil)], axis=0)


def _store_position_order(ref, val, dil, add=False):
    seg = T // dil
    for r in range(dil):
        rows = slice(None) if dil == 1 else pl.ds(r, seg, stride=dil)
        piece = val if dil == 1 else val[r * seg:(r + 1) * seg]
        if add:
            ref[rows, :] += piece
        else:
            ref[rows, :] = piece


def _dattn_specs():
    col = lambda off: pl.BlockSpec((T, HD), lambda h, off=off: (0, off // HD + h))
    row = pl.BlockSpec((1, HD), lambda h: (0, 0))
    slope = pl.BlockSpec((None, 1, HD), lambda h: (h, 0, 0))
    return [col(P_CQ), col(P_CK), col(P_CV), row, row, slope]


def _dattn_branches(q_ref, k_ref, v_ref, qg, kg, slope, o_scr, l_scr):
    for b, (_, dil) in enumerate(C_BRANCHES):
        q, k, v = (_load_branch_order(r, dil) for r in (q_ref, k_ref, v_ref))
        o, l = _attn_fn(q, k, v, qg, kg, slope, dil=dil, nb=_branch_blocks(dil))
        _store_position_order(o_scr.at[b], o, dil)
        _store_position_order(l_scr.at[b], l, dil)


def dattn_fwd(p2, qg, kg, slopes):
    per_branch = pl.BlockSpec((3, T, HD), lambda h: (0, 0, h))

    def body(q_ref, k_ref, v_ref, qg_ref, kg_ref, s_ref, y_ref, o_ref, l_ref):
        _dattn_branches(q_ref, k_ref, v_ref, qg_ref[...], kg_ref[...], s_ref[...], o_ref, l_ref)
        y_ref[...] = _combine_fn(o_ref[0], o_ref[1], o_ref[2], l_ref[0], l_ref[1], l_ref[2]).astype(BF16)

    return pl.pallas_call(
        body, name="dattn_fwd", grid=(C_HEADS,), in_specs=_dattn_specs(),
        out_specs=[pl.BlockSpec((T, HD), lambda h: (0, h)), per_branch, per_branch],
        out_shape=[jax.ShapeDtypeStruct((T, C_WIDTH), BF16)] + [jax.ShapeDtypeStruct((3, T, C_WIDTH), F32)] * 2,
        compiler_params=_cparams(("arbitrary",)),
    )(p2, p2, p2, qg, kg, slopes)


def dattn_bwd(p2, qg, kg, slopes, outs, lses, dmix):
    hcol = pl.BlockSpec((T, HD), lambda h: (0, h))
    row = pl.BlockSpec((1, HD), lambda h: (0, 0))
    dy = pl.BlockSpec((T, HD), lambda h: (0, (A_WIDTH + B_WIDTH) // HD + h))
    per_branch = pl.BlockSpec((3, T, HD), lambda h: (0, 0, h))

    def body(q_ref, k_ref, v_ref, qg_ref, kg_ref, s_ref, o_scr, l_scr, dy_ref, dq_ref, dk_ref, dv_ref, dqg_ref, dkg_ref, g_scr,
             acc):
        qg, kg, slope = qg_ref[...], kg_ref[...], s_ref[...]
        _, vjp = jax.vjp(_combine_fn, o_scr[0], o_scr[1], o_scr[2], l_scr[0], l_scr[1], l_scr[2])
        for i, g in enumerate(vjp(dy_ref[...])):
            g_scr[i] = g

        @pl.when(pl.program_id(0) == 0)
        def _():
            dqg_ref[...] = jnp.zeros_like(dqg_ref)
            dkg_ref[...] = jnp.zeros_like(dkg_ref)

        for b, (_, dil) in enumerate(C_BRANCHES):
            q, k, v = (_load_branch_order(r, dil) for r in (q_ref, k_ref, v_ref))
            do, dl = _load_branch_order(g_scr.at[b], dil), _load_branch_order(g_scr.at[3 + b], dil)
            fn = functools.partial(_attn_fn, dil=dil, nb=_branch_blocks(dil))
            _, vjp_b = jax.vjp(lambda a, b_, c, d, e, fn=fn: fn(a, b_, c, d, e, slope), q, k, v, qg, kg)
            dq, dk, dv, dqg, dkg = vjp_b((do, dl))
            for i, val in enumerate((dq, dk, dv)):
                _store_position_order(acc.at[i], val, dil, add=b > 0)
            dqg_ref[...] += dqg
            dkg_ref[...] += dkg
        for i, ref in enumerate((dq_ref, dk_ref, dv_ref)):
            ref[...] = acc[i].astype(BF16)

    scr = lambda n: pltpu.VMEM((n, T, HD), F32)
    return pl.pallas_call(
        body, name="dattn_bwd", grid=(C_HEADS,), in_specs=_dattn_specs() + [per_branch, per_branch, dy],
        out_specs=[hcol, hcol, hcol, row, row],
        out_shape=[jax.ShapeDtypeStruct((T, C_WIDTH), BF16)] * 3 + [jax.ShapeDtypeStruct((1, HD), F32)] * 2,
        scratch_shapes=[scr(6), scr(3)], compiler_params=_cparams(("arbitrary",)),
    )(p2, p2, p2, qg, kg, slopes, outs, lses, dmix)


_NCH = T // B_CHUNK


def _conv_taps(x, w_ref):
    rows = _iota(x.shape, 0)
    taps = []
    for j in range(B_CONV):
        s = B_CONV - 1 - j
        taps.append(x if s == 0 else jnp.where(rows >= s, pltpu.roll(x, s, 0), 0.0))
    pre = sum(w_ref[j:j + 1, :] * taps[j] for j in range(B_CONV))
    return pre, taps


def _conv_post(pre, mode):
    y = _silu(pre)
    if mode == "v":
        return y
    y = y * lax.rsqrt(jnp.sum(y * y, axis=-1, keepdims=True) + EPS)
    return y * (HD ** -0.5) if mode == "q" else y


def conv_fwd(p2, conv_w, mode):
    idx = "qkv".index(mode)
    xcol = pl.BlockSpec((T, HD), lambda h: (0, P_BQ // HD + B_HEADS * idx + h))
    wcol = pl.BlockSpec((B_CONV, HD), lambda h: (0, B_HEADS * idx + h))
    hcol = pl.BlockSpec((T, HD), lambda h: (0, h))

    def body(x_ref, w_ref, y_ref):
        pre, _ = _conv_taps(x_ref[...], w_ref)
        y_ref[...] = _conv_post(pre, mode)

    return pl.pallas_call(
        body, name=f"conv_fwd_{mode}", grid=(B_HEADS,), in_specs=[xcol, wcol], out_specs=hcol,
        out_shape=jax.ShapeDtypeStruct((T, B_WIDTH), F32), compiler_params=_cparams(("arbitrary",)),
    )(p2, conv_w)


def conv_bwd(p2, conv_w, dys, mode):
    idx = "qkv".index(mode)
    xcol = pl.BlockSpec((T, HD), lambda h: (0, P_BQ // HD + B_HEADS * idx + h))
    wcol = pl.BlockSpec((B_CONV, HD), lambda h: (0, B_HEADS * idx + h))
    hcol = pl.BlockSpec((T, HD), lambda h: (0, h))
    wout = pl.BlockSpec((B_CONV, HD), lambda h: (0, h))

    def body(x_ref, w_ref, *rest):
        dy_refs, (dx_ref, dw_ref) = rest[:-2], rest[-2:]
        pre, taps = _conv_taps(x_ref[...], w_ref)
        _, vjp = jax.vjp(functools.partial(_conv_post, mode=mode), pre)
        (dpre,) = vjp(sum(r[...] for r in dy_refs))
        rows = _iota(dpre.shape, 0)
        dx = w_ref[B_CONV - 1:B_CONV, :] * dpre
        for j in range(B_CONV):
            s = B_CONV - 1 - j
            dw_ref[j:j + 1, :] = jnp.sum(dpre * taps[j], axis=0, keepdims=True)
            if s > 0:
                dx = dx + w_ref[j:j + 1, :] * jnp.where(rows < T - s, pltpu.roll(dpre, T - s, 0), 0.0)
        dx_ref[...] = dx.astype(BF16)

    return pl.pallas_call(
        body, name=f"conv_bwd_{mode}", grid=(B_HEADS,), in_specs=[xcol, wcol] + [hcol] * len(dys), out_specs=[hcol, wout],
        out_shape=[jax.ShapeDtypeStruct((T, B_WIDTH), BF16), jax.ShapeDtypeStruct((B_CONV, B_WIDTH), F32)],
        compiler_params=_cparams(("arbitrary",)),
    )(p2, conv_w, *dys)


def _gates_fn(bg, al, dtb, h):
    r = _iota((HD, HD), 0)
    logit = _mm(bg, (r == h).astype(F32), hi=True)
    a = _mm(bg, (r == h + B_HEADS).astype(F32), hi=True)
    beta = _sigmoid(logit)
    graw = -jnp.exp(al) * _softplus(a + dtb)
    tri = (_iota((_NCH, B_CHUNK, B_CHUNK), 1) >= _iota((_NCH, B_CHUNK, B_CHUNK), 2)).astype(F32)
    g = _mm(tri, graw.reshape(_NCH, B_CHUNK, HD), hi=True).reshape(T, HD)
    return beta, g


def _gates_specs():
    bg = pl.BlockSpec((T, HD), lambda h: (0, P_BB // HD))
    par = pl.BlockSpec((None, 1, HD), lambda h: (h, 0, 0))
    out = pl.BlockSpec((None, T, HD), lambda h: (h, 0, 0))
    return bg, par, out


def gates_fwd(p2, al, dtb):
    bg, par, out = _gates_specs()

    def body(bg_ref, al_ref, dtb_ref, beta_ref, g_ref):
        beta, g = _gates_fn(bg_ref[...], al_ref[...], dtb_ref[...], pl.program_id(0))
        beta_ref[...] = beta
        g_ref[...] = g

    return pl.pallas_call(
        body, name="gates_fwd", grid=(B_HEADS,), in_specs=[bg, par, par], out_specs=[out, out],
        out_shape=[jax.ShapeDtypeStruct((B_HEADS, T, HD), F32)] * 2, compiler_params=_cparams(("arbitrary",)),
    )(p2, al, dtb)


def gates_bwd(p2, al, dtb, dbeta, dg1, dg2):
    bg, par, out = _gates_specs()
    acc = pl.BlockSpec((T, HD), lambda h: (0, 0))

    def body(bg_ref, al_ref, dtb_ref, dbeta_ref, dg1_ref, dg2_ref, dbg_ref, dal_ref, ddtb_ref, acc_ref):
        h = pl.program_id(0)
        _, vjp = jax.vjp(lambda a, b, c: _gates_fn(a, b, c, h), bg_ref[...], al_ref[...], dtb_ref[...])
        dbg, dal, ddtb = vjp((dbeta_ref[...], dg1_ref[...] + dg2_ref[...]))

        @pl.when(h == 0)
        def _():
            acc_ref[...] = jnp.zeros_like(acc_ref)

        acc_ref[...] += dbg
        dbg_ref[...] = acc_ref[...].astype(BF16)
        dal_ref[...] = jnp.broadcast_to(jnp.sum(dal, axis=-1, keepdims=True), (1, HD))
        ddtb_ref[...] = jnp.broadcast_to(jnp.sum(ddtb, axis=-1, keepdims=True), (1, HD))

    return pl.pallas_call(
        body, name="gates_bwd", grid=(B_HEADS,), in_specs=[bg, par, par, out, out, out], out_specs=[acc, par, par],
        out_shape=[jax.ShapeDtypeStruct((T, HD), BF16)] + [jax.ShapeDtypeStruct((B_HEADS, 1, HD), F32)] * 2,
        scratch_shapes=[pltpu.VMEM((T, HD), F32)], compiler_params=_cparams(("arbitrary",)),
    )(p2, al, dtb, dbeta, dg1, dg2)


def _unit_lower_inverse(a):
    eye = (_iota(a.shape, 1) == _iota(a.shape, 2)).astype(F32)
    x = eye - a
    p = _mm(a, a, hi=True)
    for i in range(5):
        x = x + _mm(x, p, hi=True)
        if i < 4:
            p = _mm(p, p, hi=True)
    return x


_WY_CH = 16
_WY_ROWS = _WY_CH * B_CHUNK


def _wy_fn(q, k, v, beta, g):
    sh = (q.shape[0] // B_CHUNK, B_CHUNK, HD)
    q3, k3, v3, b3, g3 = (t.reshape(sh) for t in (q, k, v, beta, g))
    gd = g3[:, :, :B_CHUNK] - jnp.swapaxes(g3, 1, 2)[:, :B_CHUNK, :]
    ii, jj = _iota(gd.shape, 1), _iota(gd.shape, 2)
    decay = jnp.exp(jnp.where(ii >= jj, gd, -jnp.inf))
    kb = k3 * b3
    a = _mm(kb, k3, tb=True) * jnp.where(ii > jj, decay, 0.0)
    tinv = _unit_lower_inverse(a)
    u = _mm(tinv, v3 * b3, hi=True)
    w = _mm(tinv, kb * jnp.exp(g3), hi=True)
    attn = _mm(q3, k3, tb=True) * decay
    return u.reshape(q.shape), w.reshape(q.shape), attn


def _wy_specs():
    hcol = pl.BlockSpec((_WY_ROWS, HD), lambda h, i: (i, h))
    hb = pl.BlockSpec((None, _WY_ROWS, HD), lambda h, i: (h, i, 0))
    at = pl.BlockSpec((None, _WY_CH, B_CHUNK, B_CHUNK), lambda h, i: (h, i, 0, 0))
    return hcol, hb, at


_WY_GRID = (B_HEADS, _NCH // _WY_CH)


def wy_fwd(q, k, v, beta, g):
    hcol, hb, at = _wy_specs()

    def body(q_ref, k_ref, v_ref, b_ref, g_ref, u_ref, w_ref, a_ref):
        u, w, a = _wy_fn(q_ref[...], k_ref[...], v_ref[...], b_ref[...], g_ref[...])
        u_ref[...] = u
        w_ref[...] = w
        a_ref[...] = a

    return pl.pallas_call(
        body, name="wy_fwd", grid=_WY_GRID, in_specs=[hcol, hcol, hcol, hb, hb], out_specs=[hcol, hcol, at],
        out_shape=[jax.ShapeDtypeStruct((T, B_WIDTH), F32)] * 2 + [jax.ShapeDtypeStruct((B_HEADS, _NCH, B_CHUNK, B_CHUNK), F32)],
        compiler_params=_cparams(("arbitrary", "arbitrary")),
    )(q, k, v, beta, g)


def wy_bwd(q, k, v, beta, g, du, dw, dattn):
    hcol, hb, at = _wy_specs()

    def body(q_ref, k_ref, v_ref, b_ref, g_ref, du_ref, dw_ref, da_ref, dq_ref, dk_ref, dv_ref, db_ref, dg_ref):
        _, vjp = jax.vjp(_wy_fn, q_ref[...], k_ref[...], v_ref[...], b_ref[...], g_ref[...])
        for r, t in zip((dq_ref, dk_ref, dv_ref, db_ref, dg_ref), vjp((du_ref[...], dw_ref[...], da_ref[...]))):
            r[...] = t

    return pl.pallas_call(
        body, name="wy_bwd", grid=_WY_GRID, in_specs=[hcol, hcol, hcol, hb, hb, hcol, hcol, at],
        out_specs=[hcol, hcol, hcol, hb, hb],
        out_shape=[jax.ShapeDtypeStruct((T, B_WIDTH), F32)] * 3 + [jax.ShapeDtypeStruct((B_HEADS, T, HD), F32)] * 2,
        compiler_params=_cparams(("arbitrary", "arbitrary")),
    )(q, k, v, beta, g, du, dw, dattn)


def _scan_step_fn(q, k, u, w, g, attn, gate, og, s):
    v_new = u - _mm(w, s)
    o = _mm(q * jnp.exp(g), s) + _mm(attn, v_new)
    g_last = jnp.sum(jnp.where(_iota(g.shape, 0) == B_CHUNK - 1, g, 0.0), axis=0, keepdims=True)
    s_new = s * jnp.exp(g_last) + _mm(k * jnp.exp(g_last - g), v_new, ta=True)
    return _rms(o, og) * _silu(gate), s_new


def _scan_specs(rev):
    ch = (lambda n: _NCH - 1 - n) if rev else (lambda n: n)
    rows = pl.BlockSpec((B_CHUNK, B_WIDTH), lambda n: (ch(n), 0))
    gb = pl.BlockSpec((B_HEADS, B_CHUNK, HD), lambda n: (0, ch(n), 0))
    at = pl.BlockSpec((B_HEADS, None, B_CHUNK, B_CHUNK), lambda n: (0, ch(n), 0, 0))
    og = pl.BlockSpec((1, HD), lambda n: (0, 0))
    st = pl.BlockSpec((None, B_HEADS, HD, HD), lambda n: (ch(n), 0, 0, 0))
    return rows, gb, at, og, st


def scan_fwd(q, k, u, w, g, attn, gate, og):
    rows, gb, at, ogs, st = _scan_specs(False)

    def body(q_ref, k_ref, u_ref, w_ref, g_ref, a_ref, gate_ref, og_ref, y_ref, st_ref, s_ref):
        @pl.when(pl.program_id(0) == 0)
        def _():
            s_ref[...] = jnp.zeros_like(s_ref)

        for h in range(B_HEADS):
            c = slice(h * HD, (h + 1) * HD)
            s = s_ref[h]
            st_ref[h] = s
            y, s_new = _scan_step_fn(q_ref[:, c], k_ref[:, c], u_ref[:, c], w_ref[:, c], g_ref[h], a_ref[h],
                                     gate_ref[:, c], og_ref[...], s)
            y_ref[:, c] = y.astype(BF16)
            s_ref[h] = s_new

    return pl.pallas_call(
        body, name="scan_fwd", grid=(_NCH,), in_specs=[rows, rows, rows, rows, gb, at, rows, ogs], out_specs=[rows, st],
        out_shape=[jax.ShapeDtypeStruct((T, B_WIDTH), BF16), jax.ShapeDtypeStruct((_NCH, B_HEADS, HD, HD), F32)],
        scratch_shapes=[pltpu.VMEM((B_HEADS, HD, HD), F32)], compiler_params=_cparams(("arbitrary",)),
    )(q, k, u, w, g, attn, gate, og)


def scan_bwd(q, k, u, w, g, attn, gate, og, states, dmix):
    rows, gb, at, ogs, st = _scan_specs(True)
    dyb = pl.BlockSpec((B_CHUNK, HD), lambda n: (_NCH - 1 - n, 0))

    def body(q_ref, k_ref, u_ref, w_ref, g_ref, a_ref, gate_ref, og_ref, st_ref, *rest):
        dy_refs, (dq_ref, dk_ref, du_ref, dw_ref, dgate_ref, dg_ref, da_ref, dog_ref, ds_ref) = rest[:B_HEADS], rest[B_HEADS:]

        @pl.when(pl.program_id(0) == 0)
        def _():
            ds_ref[...] = jnp.zeros_like(ds_ref)
            dog_ref[...] = jnp.zeros_like(dog_ref)

        for h in range(B_HEADS):
            c = slice(h * HD, (h + 1) * HD)
            _, vjp = jax.vjp(_scan_step_fn, q_ref[:, c], k_ref[:, c], u_ref[:, c], w_ref[:, c], g_ref[h], a_ref[h],
                             gate_ref[:, c], og_ref[...], st_ref[h])
            dq, dk, du, dw, dg, da, dgate, dog, ds = vjp((dy_refs[h][...], ds_ref[h]))
            dq_ref[:, c] = dq
            dk_ref[:, c] = dk
            du_ref[:, c] = du
            dw_ref[:, c] = dw
            dgate_ref[:, c] = dgate.astype(BF16)
            dg_ref[h] = dg
            da_ref[h] = da
            dog_ref[...] += dog
            ds_ref[h] = ds

    dy_specs = [pl.BlockSpec((B_CHUNK, HD), lambda n, h=h: (_NCH - 1 - n, A_WIDTH // HD + h)) for h in range(B_HEADS)]
    return pl.pallas_call(
        body, name="scan_bwd", grid=(_NCH,),
        in_specs=[rows, rows, rows, rows, gb, at, rows, ogs, st] + dy_specs,
        out_specs=[rows] * 5 + [gb, at, ogs],
        out_shape=[jax.ShapeDtypeStruct((T, B_WIDTH), F32)] * 4 + [jax.ShapeDtypeStruct((T, B_WIDTH), BF16)]
        + [jax.ShapeDtypeStruct((B_HEADS, T, HD), F32), jax.ShapeDtypeStruct((B_HEADS, _NCH, B_CHUNK, B_CHUNK), F32),
           jax.ShapeDtypeStruct((1, HD), F32)],
        scratch_shapes=[pltpu.VMEM((B_HEADS, HD, HD), F32)], compiler_params=_cparams(("arbitrary",)),
    )(q, k, u, w, g, attn, gate, og, states, *([dmix] * B_HEADS))


def _lanes(vec):
    return jnp.broadcast_to(vec[:, None, None], (vec.shape[0], 1, HD))


def gdn_forward(p2, conv_w, a_log, dt_bias, og):
    qa, ka, va = (conv_fwd(p2, conv_w, m) for m in "qkv")
    beta, g = gates_fwd(p2, _lanes(a_log), _lanes(dt_bias))
    u, w, attn = wy_fwd(qa, ka, va, beta, g)
    gate = p2[:, P_BG:P_BB]
    y, states = scan_fwd(qa, ka, u, w, g, attn, gate, og)
    return y, (qa, ka, va, beta, g, u, w, attn, gate, states)


def gdn_backward(p2, conv_w, a_log, dt_bias, og, saved, dmix):
    qa, ka, va, beta, g, u, w, attn, gate, states = saved
    dq1, dk1, du, dw, dgate, dg1, dattn, dog = scan_bwd(qa, ka, u, w, g, attn, gate, og, states, dmix)
    dq2, dk2, dv, dbeta, dg2 = wy_bwd(qa, ka, va, beta, g, du, dw, dattn)
    dbg, dal, ddtb = gates_bwd(p2, _lanes(a_log), _lanes(dt_bias), dbeta, dg1, dg2)
    dxq, dwq = conv_bwd(p2, conv_w, [dq1, dq2], "q")
    dxk, dwk = conv_bwd(p2, conv_w, [dk1, dk2], "k")
    dxv, dwv = conv_bwd(p2, conv_w, [dv], "v")
    return [dxq, dxk, dxv, dgate, dbg], jnp.concatenate([dwq, dwk, dwv], axis=1), dal[:, 0, 0], ddtb[:, 0, 0], dog


_SLOPES = np.exp2(-8.0 * (np.arange(C_HEADS, dtype=np.float64) + 1.0) / C_HEADS).astype(np.float32)


def _alibi_slopes():
    return _lanes(jnp.asarray(_SLOPES))


_ROWS = 512
_NORM_BWD_ROWS = 256
_TM = 1024
_TM_FFN = 512


def _dep_specs(dep, ngrid):
    if dep is None:
        return [], []
    return [dep], [pl.BlockSpec((8, HD), lambda *_: (0, 0))]


def rmsnorm_fwd(x, g, dep=None):
    blk = pl.BlockSpec((_ROWS, D), lambda i: (i, 0))
    deps, dspecs = _dep_specs(dep, 1)

    def body(x_ref, g_ref, *rest):
        rest[-1][...] = _rms(x_ref[...], g_ref[...]).astype(BF16)

    return pl.pallas_call(
        body, name="rmsnorm_fwd", grid=(T // _ROWS,), in_specs=[blk, pl.BlockSpec((1, D), lambda i: (0, 0))] + dspecs,
        out_specs=blk, out_shape=jax.ShapeDtypeStruct((T, D), BF16), compiler_params=_cparams(("arbitrary",)),
    )(x, g, *deps)


def rmsnorm_bwd(x, g, dh, dres):
    blk = pl.BlockSpec((_NORM_BWD_ROWS, D), lambda i: (i, 0))
    row = pl.BlockSpec((1, D), lambda i: (0, 0))

    def body(x_ref, g_ref, dh_ref, dres_ref, dx_ref, dxb_ref, dg_ref):
        _, vjp = jax.vjp(_rms, x_ref[...], g_ref[...])
        dx, dg = vjp(dh_ref[...])
        total = dres_ref[...] + dx
        dx_ref[...] = total
        dxb_ref[...] = total.astype(BF16)

        @pl.when(pl.program_id(0) == 0)
        def _():
            dg_ref[...] = jnp.zeros_like(dg_ref)

        dg_ref[...] += dg

    dx, dxb, dg = pl.pallas_call(
        body, name="rmsnorm_bwd", grid=(T // _NORM_BWD_ROWS,), in_specs=[blk, row, blk, blk], out_specs=[blk, blk, row],
        out_shape=[jax.ShapeDtypeStruct((T, D), F32), jax.ShapeDtypeStruct((T, D), BF16), jax.ShapeDtypeStruct((1, D), F32)],
        compiler_params=_cparams(("arbitrary",)),
    )(x, g, dh, dres)
    return (dx, dxb), dg


def _matmul(name, a, b, *, grid, a_spec, b_spec, o_spec, out_shape, ta=False, tb=False, k_axis=None, res=None, dep=None):
    dims = _dims(2, ta, tb)
    deps, dspecs = _dep_specs(dep, len(grid))

    def body(a_ref, b_ref, *rest):
        o_ref = rest[-1]
        prod = lax.dot_general(a_ref[...].astype(BF16), b_ref[...].astype(BF16), dims, preferred_element_type=F32)
        if res is not None:
            prod = prod + rest[0][...]
        if k_axis is None:
            o_ref[...] = prod.astype(o_ref.dtype)
        else:
            @pl.when(pl.program_id(k_axis) == 0)
            def _():
                o_ref[...] = prod

            @pl.when(pl.program_id(k_axis) > 0)
            def _():
                o_ref[...] += prod

    sem = tuple("arbitrary" for _ in grid)
    ins = [a, b] + ([res] if res is not None else []) + deps
    specs = [a_spec, b_spec] + ([o_spec] if res is not None else []) + dspecs
    return pl.pallas_call(
        body, name=name, grid=grid, in_specs=specs, out_specs=o_spec, out_shape=out_shape, compiler_params=_cparams(sem),
    )(*ins)


_IN_TN = P_END // 3


def mm_proj(h1, wp_in, l):
    return _matmul(
        "mm_proj", h1, wp_in, grid=(P_END // _IN_TN, T // _TM),
        a_spec=pl.BlockSpec((_TM, D), lambda j, i: (i, 0)),
        b_spec=pl.BlockSpec((None, D, _IN_TN), lambda j, i: (l, 0, j)),
        o_spec=pl.BlockSpec((_TM, _IN_TN), lambda j, i: (i, j)), out_shape=jax.ShapeDtypeStruct((T, P_END), F32))


def mm_dh1(dp2, wp_in, l, dep=None):
    return _matmul(
        "mm_dh1", dp2, wp_in, grid=(T // _TM, P_END // _IN_TN), tb=True, k_axis=1, dep=dep,
        a_spec=pl.BlockSpec((_TM, _IN_TN), lambda i, k: (i, k)),
        b_spec=pl.BlockSpec((None, D, _IN_TN), lambda i, k: (l, 0, k)),
        o_spec=pl.BlockSpec((_TM, D), lambda i, k: (i, 0)), out_shape=jax.ShapeDtypeStruct((T, D), F32))


def mm_dwin(h1, dp2):
    return _matmul(
        "mm_dwin", h1, dp2, grid=(P_END // _IN_TN, D // _TM), ta=True,
        a_spec=pl.BlockSpec((T, _TM), lambda j, i: (0, i)),
        b_spec=pl.BlockSpec((T, _IN_TN), lambda j, i: (0, j)),
        o_spec=pl.BlockSpec((_TM, _IN_TN), lambda j, i: (i, j)), out_shape=jax.ShapeDtypeStruct((D, P_END), BF16))


def _mm_square(name, a, w, l, res, tb, dep=None):
    tn = 1024
    b_spec = (pl.BlockSpec((None, tn, D), lambda j, i: (l, j, 0)) if tb else pl.BlockSpec((None, D, tn), lambda j, i: (l, 0, j)))
    return _matmul(
        name, a, w, grid=(D // tn, T // _TM), tb=tb, res=res, dep=dep,
        a_spec=pl.BlockSpec((_TM, D), lambda j, i: (i, 0)), b_spec=b_spec,
        o_spec=pl.BlockSpec((_TM, tn), lambda j, i: (i, j)), out_shape=jax.ShapeDtypeStruct((T, D), F32))


def mm_out(mix, wg_out, l, x):
    return _mm_square("mm_out", mix, wg_out, l, x, False)


def mm_dmix(dx1, wg_out, l, dep=None):
    return _mm_square("mm_dmix", dx1, wg_out, l, None, True, dep)


def mm_dwout(mix, dx1):
    tn = 1024
    return _matmul(
        "mm_dwout", mix, dx1, grid=(D // tn, D // _TM), ta=True,
        a_spec=pl.BlockSpec((T, _TM), lambda j, i: (0, i)), b_spec=pl.BlockSpec((T, tn), lambda j, i: (0, j)),
        o_spec=pl.BlockSpec((_TM, tn), lambda j, i: (i, j)), out_shape=jax.ShapeDtypeStruct((D, D), BF16))


_GU_TN = GU_SHARD // 2


_GU_NJ = FFN // _GU_TN


def mm_dh2(dgu, wg_gu, l, dep=None):
    return _matmul(
        "mm_dh2", dgu, wg_gu, grid=(T // _TM, 2 * N_CHIPS), tb=True, k_axis=1, dep=dep,
        a_spec=pl.BlockSpec((None, _TM, _GU_TN), lambda i, k: (k // _GU_NJ, i, k % _GU_NJ)),
        b_spec=pl.BlockSpec((None, None, D, _GU_TN), lambda i, k: (l, k // 2, 0, k % 2)),
        o_spec=pl.BlockSpec((_TM, D), lambda i, k: (i, 0)), out_shape=jax.ShapeDtypeStruct((T, D), F32))


def mm_dwgu(h2, dgu):
    return _matmul(
        "mm_dwgu", h2, dgu, grid=(N_CHIPS, 2, D // _TM), ta=True,
        a_spec=pl.BlockSpec((T, _TM), lambda s, j, i: (0, i)),
        b_spec=pl.BlockSpec((None, T, _GU_TN), lambda s, j, i: ((2 * s + j) // _GU_NJ, 0, (2 * s + j) % _GU_NJ)),
        o_spec=pl.BlockSpec((None, _TM, _GU_TN), lambda s, j, i: (s, i, j)),
        out_shape=jax.ShapeDtypeStruct((N_CHIPS, D, GU_SHARD), BF16))


def mm_down(act, wg_down, l, x1):
    tn = 512
    return _matmul(
        "mm_down", act, wg_down, grid=(D // tn, T // _TM), res=x1,
        a_spec=pl.BlockSpec((_TM, FFN), lambda j, i: (i, 0)),
        b_spec=pl.BlockSpec((None, FFN, tn), lambda j, i: (l, 0, j)),
        o_spec=pl.BlockSpec((_TM, tn), lambda j, i: (i, j)), out_shape=jax.ShapeDtypeStruct((T, D), F32))


def mm_dwdown(act, dx2):
    tm, tn = DOWN_SHARD, 512
    return _matmul(
        "mm_dwdown", act, dx2, grid=(D // tn, FFN // tm), ta=True,
        a_spec=pl.BlockSpec((T, tm), lambda j, i: (0, i)), b_spec=pl.BlockSpec((T, tn), lambda j, i: (0, j)),
        o_spec=pl.BlockSpec((tm, tn), lambda j, i: (i, j)), out_shape=jax.ShapeDtypeStruct((FFN, D), BF16))


_FF_TN = 1408


def _swiglu_fn(gt, up):
    return _silu(gt) * up


def _gate_up_specs():
    gate = pl.BlockSpec((None, None, D, _FF_TN), lambda j, i: (0, j // 2, 0, j % 2))
    up = pl.BlockSpec((None, None, D, _FF_TN), lambda j, i: (0, N_CHIPS // 2 + j // 2, 0, j % 2))
    both = pl.BlockSpec((2, _TM_FFN, _FF_TN), lambda j, i: (0, i, j))
    return gate, up, both


def mm_gu_swiglu(h2, wg_gu):
    gate, up, both = _gate_up_specs()

    def body(h_ref, wg_ref, wu_ref, gu_ref, act_ref):
        h = h_ref[...]
        gt = jnp.dot(h, wg_ref[...], preferred_element_type=F32)
        u = jnp.dot(h, wu_ref[...], preferred_element_type=F32)
        gu_ref[0] = gt.astype(BF16)
        gu_ref[1] = u.astype(BF16)
        act_ref[...] = _swiglu_fn(gt, u).astype(BF16)

    return pl.pallas_call(
        body, name="mm_gu_swiglu", grid=(FFN // _FF_TN, T // _TM_FFN),
        in_specs=[pl.BlockSpec((_TM_FFN, D), lambda j, i: (i, 0)), gate, up],
        out_specs=[both, pl.BlockSpec((_TM_FFN, _FF_TN), lambda j, i: (i, j))],
        out_shape=[jax.ShapeDtypeStruct((2, T, FFN), BF16), jax.ShapeDtypeStruct((T, FFN), BF16)],
        compiler_params=_cparams(("arbitrary", "arbitrary")),
    )(h2, wg_gu, wg_gu)


def mm_dact_swiglu(dx2, wg_down, gu, dep=None):
    _, _, both = _gate_up_specs()
    deps, dspecs = _dep_specs(dep, 2)

    def body(dx_ref, w_ref, gu_ref, *rest):
        dact = lax.dot_general(dx_ref[...].astype(BF16), w_ref[...], _dims(2, False, True), preferred_element_type=F32)
        _, vjp = jax.vjp(_swiglu_fn, gu_ref[0].astype(F32), gu_ref[1].astype(F32))
        dgt, dup = vjp(dact)
        rest[-1][0] = dgt.astype(BF16)
        rest[-1][1] = dup.astype(BF16)

    return pl.pallas_call(
        body, name="mm_dact_swiglu", grid=(FFN // _FF_TN, T // _TM_FFN),
        in_specs=[pl.BlockSpec((_TM_FFN, D), lambda j, i: (i, 0)), pl.BlockSpec((None, _FF_TN, D), lambda j, i: (0, j, 0)),
                  both]
        + dspecs,
        out_specs=both, out_shape=jax.ShapeDtypeStruct((2, T, FFN), BF16),
        compiler_params=_cparams(("arbitrary", "arbitrary")),
    )(dx2, wg_down, gu, *deps)


def loss_and_grad(y, target):
    blk = pl.BlockSpec((_ROWS, D), lambda i: (i, 0))
    acc = pl.BlockSpec((8, HD), lambda i: (0, 0))

    def body(y_ref, t_ref, dy_ref, dyb_ref, l_ref):
        err = y_ref[...] - t_ref[...]
        dy_ref[...] = err * (1.0 / D)
        dyb_ref[...] = (err * (1.0 / D)).astype(BF16)

        @pl.when(pl.program_id(0) == 0)
        def _():
            l_ref[...] = jnp.zeros_like(l_ref)

        l_ref[...] += (0.5 / D) * jnp.sum(err * err)

    dy, dyb, loss = pl.pallas_call(
        body, name="loss_and_grad", grid=(T // _ROWS,), in_specs=[blk, blk], out_specs=[blk, blk, acc],
        out_shape=[jax.ShapeDtypeStruct((T, D), F32), jax.ShapeDtypeStruct((T, D), BF16), jax.ShapeDtypeStruct((8, HD), F32)],
        compiler_params=_cparams(("arbitrary",)),
    )(y, target)
    return (dy, dyb), loss


def adamw(w, g, m, v, name):
    rows, cols = w.shape
    tr = _ROWS if rows % _ROWS == 0 else rows
    blk = pl.BlockSpec((tr, cols), lambda i: (i, 0))

    def body(w_ref, g_ref, m_ref, v_ref, d_ref, nm_ref, nv_ref):
        gg = g_ref[...]
        nm = ADAM_B1 * m_ref[...] + (1.0 - ADAM_B1) * gg
        nv = ADAM_B2 * v_ref[...] + (1.0 - ADAM_B2) * (gg * gg)
        m_hat = nm / (1.0 - ADAM_B1 ** ADAM_STEP)
        v_hat = nv / (1.0 - ADAM_B2 ** ADAM_STEP)
        d_ref[...] = -ADAM_LR * (m_hat / (jnp.sqrt(v_hat) + ADAM_EPS) + ADAM_WD * w_ref[...])
        nm_ref[...] = nm
        nv_ref[...] = nv

    return pl.pallas_call(
        body, name=name, grid=(rows // tr,), in_specs=[blk] * 4, out_specs=[blk] * 3,
        out_shape=[jax.ShapeDtypeStruct(w.shape, F32)] * 3, compiler_params=_cparams(("arbitrary",)),
    )(w, g, m, v)


_LANE = 128


def _segment_of_shard_column():
    flat = np.full(P_END, -1, np.int64)
    for o in range(P_END):
        if GATE_COLS <= o < P_CQ:
            continue
        c = o if o < GATE_COLS else o - (P_CQ - GATE_COLS)
        flat[o] = (c // IN_SHARD) * IN_SHARD_PAD + c % IN_SHARD
    return flat


def _block_pairs(src_of_dst):
    return [sorted({int(c) // _LANE for c in src_of_dst[db * _LANE:(db + 1) * _LANE] if c >= 0})
            for db in range(len(src_of_dst) // _LANE)]


_RELAYOUT_ROWS = 512
_SHARD_BLOCKS = IN_SHARD_PAD // _LANE


def _relayout(name, x, to_segments):
    seg_of = _segment_of_shard_column()
    if to_segments:
        src_of_dst = seg_of
    else:
        src_of_dst = np.full(N_CHIPS * IN_SHARD_PAD, -1, np.int64)
        src_of_dst[seg_of[seg_of >= 0]] = np.nonzero(seg_of >= 0)[0]
    sources = _block_pairs(src_of_dst)
    n_dst = len(sources)
    col_map = jnp.asarray(src_of_dst.reshape(n_dst, 1, _LANE), jnp.int32)
    shard_blk = pl.BlockSpec((N_CHIPS, _RELAYOUT_ROWS, IN_SHARD_PAD), lambda i: (0, i, 0))
    seg_blk = pl.BlockSpec((_RELAYOUT_ROWS, P_END), lambda i: (i, 0))

    def shard_cols(ref, b):
        return ref.at[b // _SHARD_BLOCKS, :, pl.ds((b % _SHARD_BLOCKS) * _LANE, _LANE)]

    def seg_cols(ref, b):
        return ref.at[:, pl.ds(b * _LANE, _LANE)]

    src_cols, dst_cols = (shard_cols, seg_cols) if to_segments else (seg_cols, shard_cols)

    def body(x_ref, map_ref, o_ref):
        src_row = _iota((_LANE, _LANE), 0)
        for d in range(n_dst):
            acc = jnp.zeros((_RELAYOUT_ROWS, _LANE), F32)
            for sb in sources[d]:
                sel = (src_row + sb * _LANE == map_ref[d]).astype(x_ref.dtype)
                acc = acc + jnp.dot(src_cols(x_ref, sb)[...], sel, preferred_element_type=F32)
            dst_cols(o_ref, d)[...] = acc.astype(o_ref.dtype)

    rows = x.shape[-2]
    out_shape = (rows, P_END) if to_segments else (N_CHIPS, rows, IN_SHARD_PAD)
    return pl.pallas_call(
        body, name=name, grid=(rows // _RELAYOUT_ROWS,),
        in_specs=[shard_blk if to_segments else seg_blk, pl.BlockSpec(col_map.shape, lambda i: (0, 0, 0))],
        out_specs=seg_blk if to_segments else shard_blk, out_shape=jax.ShapeDtypeStruct(out_shape, x.dtype),
        compiler_params=_cparams(("arbitrary",)),
    )(x, col_map)


def shards_to_segments(w):
    return _relayout("shards_to_segments", w, True)


def segments_to_shards(w):
    return _relayout("segments_to_shards", w, False)


def mixers_forward(x, w_in, sp, dep=None, h1=None):
    if h1 is None:
        h1 = rmsnorm_fwd(x, sp["norm1_g"], dep)
    p2 = mm_proj(h1, w_in, 0)
    y_a = sgu_fwd(p2, sp["sgu_norm_g"], sp["w_spatial"], sp["b_spatial"])
    y_b, saved_b = gdn_forward(p2, sp["conv_w"], sp["a_log"], sp["dt_bias"], sp["o_norm_g"])
    y_c, outs_c, lses_c = dattn_fwd(p2, sp["q_norm_g"], sp["k_norm_g"], _alibi_slopes())
    mix = jnp.concatenate([y_a, y_b, y_c], axis=1)
    return mix, (x, h1, p2, saved_b, (outs_c, lses_c), mix)


def ffn_up(x, mix, w_out, w_gu, sp):
    x1 = mm_out(mix, w_out, 0, x)
    h2 = rmsnorm_fwd(x1, sp["norm2_g"])
    gu, act = mm_gu_swiglu(h2, w_gu)
    return x1, h2, gu, act


def ffn_forward(x, mix, wg, sp):
    x1, h2, gu, act = ffn_up(x, mix, wg["out"], wg["gu"], sp)
    x2 = mm_down(act, wg["down"], 0, x1)
    return x2, (x1, h2, gu, act)


def ffn_backward(dx2, wg, sp, saved, dep=None, on_weight_grads=None):
    x1, h2, gu, act = saved
    dx2, dx2_bf = dx2
    dgu = mm_dact_swiglu(dx2_bf, wg["down"], gu, dep)
    dw_down = mm_dwdown(act, dx2_bf)
    dw_gu = mm_dwgu(h2, dgu)
    tok = None if on_weight_grads is None else on_weight_grads(dw_gu, dw_down)
    dh2 = mm_dh2(dgu, wg["gu"], 0, tok)
    dx1, dnorm2 = rmsnorm_bwd(x1, sp["norm2_g"], dh2, dx2)
    return dx1, dnorm2, dw_gu, dw_down


def mixers_backward(dx1, wg, sp, saved, dep=None, on_weight_grads=None):
    x, h1, p2, saved_b, saved_c, mix = saved
    dx1, dx1_bf = dx1
    dmix = mm_dmix(dx1_bf, wg["out"], 0, dep)
    dw_out = mm_dwout(mix, dx1_bf)
    du, dv, dsg, dws, dbs = sgu_bwd(p2, sp["sgu_norm_g"], sp["w_spatial"], sp["b_spatial"], dmix)
    dseg_b, dconv, dal, ddtb, dog = gdn_backward(p2, sp["conv_w"], sp["a_log"], sp["dt_bias"], sp["o_norm_g"], saved_b, dmix)
    dcq, dck, dcv, dqg, dkg = dattn_bwd(p2, sp["q_norm_g"], sp["k_norm_g"], _alibi_slopes(), *saved_c, dmix)
    dp2 = jnp.concatenate([du, dv] + dseg_b + [dcq, dck, dcv], axis=1)
    dw_in = segments_to_shards(mm_dwin(h1, dp2))
    tok = None if on_weight_grads is None else on_weight_grads(dw_in, dw_out)
    dh1 = mm_dh1(dp2, wg["in"], 0, tok)
    dx, dnorm1 = rmsnorm_bwd(x, sp["norm1_g"], dh1, dx1)
    small = {"norm1_g": dnorm1, "sgu_norm_g": dsg, "w_spatial": dws, "b_spatial": dbs, "conv_w": dconv, "a_log": dal,
             "dt_bias": ddtb, "o_norm_g": dog, "q_norm_g": dqg, "k_norm_g": dkg}
    return dx, dw_in, dw_out, small


_HBM = pl.BlockSpec(memory_space=pltpu.HBM)
_MESH = pl.DeviceIdType.MESH


def _place():
    x, y, c = lax.axis_index("x"), lax.axis_index("y"), lax.axis_index("c")
    chips = [(1 - x, y), (x, 1 - y), (1 - x, 1 - y)]
    return x, y, c, chips


def _rcopy(src, dst, ssem, rsem, dev):
    return pltpu.make_async_remote_copy(src_ref=src, dst_ref=dst, send_sem=ssem, recv_sem=rsem, device_id=dev,
                                        device_id_type=_MESH)


_SEM = pl.BlockSpec(memory_space=pltpu.SEMAPHORE)
_SIDE_EFFECT = pltpu.SideEffectType.DATAFLOW_SIDE_EFFECTING


def _in_hbm(a):
    return pltpu.with_memory_space_constraint(a, pltpu.HBM)


def _split_copy(name, srcs, land_shapes, n_sems, copies):
    n, m = len(srcs), len(land_shapes)
    thru = [pltpu.HBM(a.shape, a.dtype) for a in srcs] + [pltpu.HBM(s.shape, s.dtype) for s in land_shapes]
    sems = (pltpu.SemaphoreType.DMA((n_sems,)), pltpu.SemaphoreType.DMA((n_sems,)))

    def start(dep=None):
        deps = [] if dep is None else [dep]

        def body(*refs):
            ins, lands = refs[:n], refs[n:n + m]
            ssem, rsem, token = refs[n + m + len(deps)], refs[n + m + len(deps) + 1], refs[-1]
            for cp in copies(ins, lands, ssem, rsem)[0]:
                cp.start()
            token[...] = jnp.zeros_like(token)

        out = pl.pallas_call(
            body, name=name + "_start", out_shape=(*sems, *thru, jax.ShapeDtypeStruct((8, HD), F32)),
            in_specs=[_HBM] * (n + m) + [pl.BlockSpec(memory_space=pl.ANY)] * len(deps),
            out_specs=(_SEM, _SEM, *[_HBM] * (n + m), pl.BlockSpec(memory_space=pltpu.VMEM)),
            input_output_aliases={i: 2 + i for i in range(n + m)},
            compiler_params=pltpu.CompilerParams(has_side_effects=_SIDE_EFFECT),
        )(*[_in_hbm(a) for a in srcs], *[_in_hbm(lax.empty(s.shape, s.dtype)) for s in land_shapes], *deps)
        return out[:-1], out[-1]

    def wait(state, after):
        def body(*refs):
            ins, lands, ssem, rsem = refs[:n], refs[n:n + m], refs[n + m], refs[n + m + 1]
            sent, arrivals = copies(ins, lands, ssem, rsem)
            for cp in sent:
                cp.wait_send()
            for cp in arrivals:
                cp.wait_recv()

        out = pl.pallas_call(
            body, name=name + "_wait", out_shape=tuple(thru),
            in_specs=[_HBM] * (n + m) + [_SEM, _SEM, pl.BlockSpec(memory_space=pl.ANY)], out_specs=[_HBM] * (n + m),
            input_output_aliases={i: i for i in range(n + m)},
            compiler_params=pltpu.CompilerParams(has_side_effects=_SIDE_EFFECT),
        )(*state[2:], state[0], state[1], after)
        return list(out[:n]), list(out[n:])

    return start, wait


def gather_direct(shards, tag):
    n = len(shards)

    def copies(ins, lands, ssem, rsem):
        x, y, c, chips = _place()
        s = 2 * x + y
        sibling = (x, y, 1 - c)
        sent, arrivals = [], []
        for a in range(n):
            for u in range(2):
                cp = _rcopy(ins[a].at[u], lands[a].at[s, u], ssem.at[5 * a + u], rsem.at[5 * a + u], sibling)
                sent.append(cp)
                arrivals.append(cp)
            for j, (cx, cy) in enumerate(chips):
                k = 5 * a + 2 + j
                sent.append(_rcopy(ins[a].at[c], lands[a].at[s, c], ssem.at[k], rsem.at[k], (cx, cy, c)))
                arrivals.append(_rcopy(ins[a].at[c], lands[a].at[2 * cx + cy, c], ssem.at[k], rsem.at[k], (cx, cy, c)))
        return sent, arrivals

    lands = [jax.ShapeDtypeStruct((N_CHIPS,) + w.shape, w.dtype) for w in shards]
    return _split_copy("gather_direct_" + tag, shards, lands, 5 * n, copies)


def pass_to_sibling(lands):
    n = len(lands)

    def body(*refs):
        ins = refs[:n]
        ssem, rsem = refs[2 * n:]
        x, y, c, chips = _place()
        sibling = (x, y, 1 - c)
        cps, arrivals = [], []
        for a in range(n):
            for j, (cx, cy) in enumerate(chips):
                t = 2 * cx + cy
                cps.append(_rcopy(ins[a].at[t, c], ins[a].at[t, c], ssem.at[a, j], rsem.at[a, j], sibling))
                arrivals.append(_rcopy(ins[a].at[t, c], ins[a].at[t, 1 - c], ssem.at[a, j], rsem.at[a, j], sibling))
        for cp in cps:
            cp.start()
        for cp, ar in zip(cps, arrivals):
            cp.wait_send()
            ar.wait_recv()

    return pl.pallas_call(
        body, name="pass_to_sibling", in_specs=[_HBM] * n, out_specs=[_HBM] * n,
        out_shape=[jax.ShapeDtypeStruct(a.shape, a.dtype) for a in lands], input_output_aliases={a: a for a in range(n)},
        scratch_shapes=[pltpu.SemaphoreType.DMA((n, 3)), pltpu.SemaphoreType.DMA((n, 3))],
    )(*lands)


def exchange_halves(grads, tag):
    n = len(grads)

    def copies(ins, lands, ssem, rsem):
        x, y, c, _ = _place()
        cps = []
        for a in range(n):
            h = grads[a].shape[1] // 2
            cps.append(_rcopy(ins[a].at[:, pl.ds((1 - c) * h, h)], lands[a], ssem.at[a], rsem.at[a], (x, y, 1 - c)))
        return cps, cps

    lands = [jax.ShapeDtypeStruct((g.shape[0], g.shape[1] // 2, g.shape[2]), g.dtype) for g in grads]
    return _split_copy("exchange_halves_" + tag, grads, lands, n, copies)


def scatter_direct(parts, tag):
    n = len(parts)

    def copies(ins, lands, ssem, rsem):
        x, y, c, chips = _place()
        cps = [_rcopy(ins[a].at[2 * cx + cy], lands[a].at[j], ssem.at[3 * a + j], rsem.at[3 * a + j], (cx, cy, c))
               for a in range(n) for j, (cx, cy) in enumerate(chips)]
        return cps, cps

    lands = [jax.ShapeDtypeStruct((3,) + p.shape[1:], p.dtype) for p in parts]
    return _split_copy("scatter_direct_" + tag, parts, lands, 3 * n, copies)


def share_halves(halves):
    n = len(halves)

    def body(*refs):
        ins, outs = refs[:n], refs[n:2 * n]
        ssem, rsem = refs[2 * n:]
        x, y, c, _ = _place()
        cps = [_rcopy(ins[i], outs[i], ssem.at[i], rsem.at[i], (x, y, 1 - c)) for i in range(n)]
        for cp in cps:
            cp.start()
        for cp in cps:
            cp.wait()

    return pl.pallas_call(
        body, name="share_halves", in_specs=[_HBM] * n, out_specs=[_HBM] * n,
        out_shape=[jax.ShapeDtypeStruct(h.shape, h.dtype) for h in halves],
        scratch_shapes=[pltpu.SemaphoreType.DMA((n,)), pltpu.SemaphoreType.DMA((n,))],
    )(*halves)


_ADAMW_BLOCK_BYTES = 3 << 19


def adamw_shard(w, m, v, mine, theirs, c, name):
    _, r, cw = w.shape
    h, cg = mine[0].shape
    tr = next(t for t in (256, 176, 128) if h % t == 0 and t * cg * 4 <= _ADAMW_BLOCK_BYTES)
    nb = h // tr
    wblk = pl.BlockSpec((None, tr, cw), lambda l, i, c_ref: (l, i, 0))
    gblk = lambda layer, own: pl.BlockSpec((tr, cg), lambda l, i, c_ref: (_held_block(l, i, c_ref, layer, own, nb), 0))
    return _adamw_halves(w, m, v, mine, theirs, c, name, (DEPTH, r // tr), wblk, gblk, nb, cw)


def _held_block(l, i, c_ref, layer, own, nb):
    in_use = (l == layer) & (((i // nb) == c_ref[0]) == own)
    return jnp.where(in_use, i % nb, 0)


def _adamw_halves(w, m, v, mine, theirs, c, name, grid, wblk, gblk, nb, cw):
    def body(c_ref, w_ref, m_ref, v_ref, m0, m1, t0, t1, g_ref, d_ref, nm_ref, nv_ref):
        is_mine = (pl.program_id(1) // nb) == c_ref[0]
        first = pl.program_id(0) == 0
        gg = jnp.where(is_mine, jnp.where(first, m0[:, :cw], m1[:, :cw]), jnp.where(first, t0[:, :cw], t1[:, :cw]))
        nm = ADAM_B1 * m_ref[...] + (1.0 - ADAM_B1) * gg
        nv = ADAM_B2 * v_ref[...] + (1.0 - ADAM_B2) * (gg * gg)
        m_hat = nm / (1.0 - ADAM_B1 ** ADAM_STEP)
        v_hat = nv / (1.0 - ADAM_B2 ** ADAM_STEP)
        g_ref[...] = gg
        d_ref[...] = -ADAM_LR * (m_hat / (jnp.sqrt(v_hat) + ADAM_EPS) + ADAM_WD * w_ref[...])
        nm_ref[...] = nm
        nv_ref[...] = nv

    return pl.pallas_call(
        body, name=name,
        grid_spec=pltpu.PrefetchScalarGridSpec(
            num_scalar_prefetch=1, grid=grid,
            in_specs=[wblk] * 3 + [gblk(0, True), gblk(1, True), gblk(0, False), gblk(1, False)], out_specs=[wblk] * 4),
        out_shape=[jax.ShapeDtypeStruct(w.shape, F32)] * 4, compiler_params=_cparams(("arbitrary", "arbitrary")),
    )(c, w, m, v, mine[0], mine[1], theirs[0], theirs[1])


def adamw_shard_t(wt, mt, vt, mine_t, theirs_t, c, name):
    _, cw, r = wt.shape
    h = mine_t[0].shape[1]
    tc = 256
    nb = h // tc
    wblk = pl.BlockSpec((None, cw, tc), lambda l, j, c_ref: (l, 0, j))
    gblk = lambda layer, own: pl.BlockSpec((cw, tc), lambda l, j, c_ref: (0, _held_block(l, j, c_ref, layer, own, nb)))
    return _adamw_halves(wt, mt, vt, mine_t, theirs_t, c, name, (DEPTH, r // tc), wblk, gblk, nb, cw)


def _half_rows(h, cols):
    for tr in (512, 256, 352, 128, 64):
        if h % tr == 0 and tr * cols * 4 <= 6 * 1024 * 1024:
            return tr
    raise ValueError((h, cols))


def add_sibling(grad, recv, c):
    _, r, cols = grad.shape
    h = r // 2
    tr = _half_rows(h, cols)
    nb = h // tr

    def body(c_ref, g_ref, r_ref, o_ref):
        o_ref[...] = (g_ref[...].astype(F32) + r_ref[...].astype(F32)).astype(BF16)

    return pl.pallas_call(
        body, name="add_sibling",
        grid_spec=pltpu.PrefetchScalarGridSpec(
            num_scalar_prefetch=1, grid=(N_CHIPS, nb),
            in_specs=[pl.BlockSpec((None, tr, cols), lambda t, i, c_ref: (t, c_ref[0] * nb + i, 0)),
                      pl.BlockSpec((None, tr, cols), lambda t, i, c_ref: (t, i, 0))],
            out_specs=pl.BlockSpec((None, tr, cols), lambda t, i, c_ref: (t, i, 0))),
        out_shape=jax.ShapeDtypeStruct((N_CHIPS, h, cols), BF16), compiler_params=_cparams(("arbitrary", "arbitrary")),
    )(c, grad, recv)


def add_chips(part, recv, s):
    _, h, cols = part.shape
    tr = _half_rows(h, cols)

    def body(s_ref, p_ref, r_ref, o_ref):
        o_ref[...] = ((p_ref[...].astype(F32) + r_ref[0].astype(F32)) + r_ref[1].astype(F32)) + r_ref[2].astype(F32)

    return pl.pallas_call(
        body, name="add_chips",
        grid_spec=pltpu.PrefetchScalarGridSpec(
            num_scalar_prefetch=1, grid=(h // tr,),
            in_specs=[pl.BlockSpec((None, tr, cols), lambda i, s_ref: (s_ref[0], i, 0)),
                      pl.BlockSpec((3, tr, cols), lambda i, s_ref: (0, i, 0))],
            out_specs=pl.BlockSpec((tr, cols), lambda i, s_ref: (i, 0))),
        out_shape=jax.ShapeDtypeStruct((h, cols), F32), compiler_params=_cparams(("arbitrary",)),
    )(s, part, recv)


def allreduce_small(vec):
    rows = vec.shape[0]

    def body(v_ref, o_ref, buf, ssem, rsem, lsem):
        x, y, c, chips = _place()
        me, sibling = (x, y, c), (x, y, 1 - c)

        def blk(px, py, pc):
            return buf.at[4 * px + 2 * py + pc]

        def copy(k, block, to, src=None):
            return _rcopy(blk(*block) if src is None else src, blk(*block), ssem.at[k], rsem.at[k], to)

        mine = pltpu.make_async_copy(v_ref, blk(*me), lsem)
        mine.start()
        first = [copy(0, me, sibling, src=v_ref)] + [copy(1 + j, me, (*chip, c), src=v_ref) for j, chip in enumerate(chips)]
        for cp in first:
            cp.start()
        passed = [copy(4 + j, (*chip, c), sibling) for j, chip in enumerate(chips)]
        for j, chip in enumerate(chips):
            copy(1 + j, (*chip, c), me).wait_recv()
            passed[j].start()
        copy(0, sibling, me).wait_recv()
        for j, chip in enumerate(chips):
            copy(4 + j, (*chip, 1 - c), me).wait_recv()
        for cp in first + passed:
            cp.wait_send()
        mine.wait()
        acc = buf[0]
        for d in range(1, N_DEV):
            acc = acc + buf[d]
        o_ref[...] = acc

    vm = pl.BlockSpec(memory_space=pltpu.VMEM)
    return pl.pallas_call(
        body, name="allreduce_small", in_specs=[vm], out_specs=vm, out_shape=jax.ShapeDtypeStruct(vec.shape, F32),
        scratch_shapes=[pltpu.VMEM((N_DEV, rows, HD), F32), pltpu.SemaphoreType.DMA((7,)), pltpu.SemaphoreType.DMA((7,)),
                        pltpu.SemaphoreType.DMA],
        compiler_params=pltpu.CompilerParams(vmem_limit_bytes=VMEM_LIMIT),
    )(vec)


SMALL_NAMES = ("norm1_g", "sgu_norm_g", "w_spatial", "b_spatial", "conv_w", "a_log", "dt_bias", "o_norm_g", "q_norm_g",
               "k_norm_g", "norm2_g")


def small_params(l, p, conv_full):
    return {"norm1_g": p["norm1_g"][l][None], "sgu_norm_g": p["sgu_norm_g"][l][:, None, :], "w_spatial": p["w_spatial"][l],
            "b_spatial": p["b_spatial"][l][..., None], "conv_w": conv_full[l], "a_log": p["a_log"][l], "dt_bias": p["dt_bias"][l],
            "o_norm_g": p["o_norm_g"][l][None], "q_norm_g": p["q_norm_g"][l][None], "k_norm_g": p["k_norm_g"][l][None],
            "norm2_g": p["norm2_g"][l][None]}


_PACK_TILE = 8 * HD


def _pack(arrays):
    flat = jnp.concatenate([a.reshape(-1) for a in arrays])
    pad = -flat.shape[0] % _PACK_TILE
    return jnp.pad(flat, (0, pad)).reshape(-1, HD)


def _unpack(packed, shapes):
    flat, out, off = packed.reshape(-1), [], 0
    for shp in shapes:
        n = int(np.prod(shp))
        out.append(flat[off:off + n].reshape(shp))
        off += n
    return out


WEIGHT_ORDER = ("norm1_g", "w_in", "sgu_norm_g", "w_spatial", "b_spatial", "conv_w", "a_log", "dt_bias", "o_norm_g", "q_norm_g",
                "k_norm_g", "w_out", "norm2_g", "w_gate_up", "w_down")


def kernel(x, norm1_g, w_in, sgu_norm_g, w_spatial, b_spatial, conv_w, a_log, dt_bias, o_norm_g, q_norm_g, k_norm_g, w_out, norm2_g, w_gate_up, w_down, loss_target, m_norm1_g, m_w_in, m_sgu_norm_g, m_w_spatial, m_b_spatial, m_conv_w, m_a_log, m_dt_bias, m_o_norm_g, m_q_norm_g, m_k_norm_g, m_w_out, m_norm2_g, m_w_gate_up, m_w_down, v_norm1_g, v_w_in, v_sgu_norm_g, v_w_spatial, v_b_spatial, v_conv_w, v_a_log, v_dt_bias, v_o_norm_g, v_q_norm_g, v_k_norm_g, v_w_out, v_norm2_g, v_w_gate_up, v_w_down):
    w = dict(norm1_g=norm1_g, w_in=w_in, sgu_norm_g=sgu_norm_g, w_spatial=w_spatial, b_spatial=b_spatial, conv_w=conv_w,
             a_log=a_log, dt_bias=dt_bias, o_norm_g=o_norm_g, q_norm_g=q_norm_g, k_norm_g=k_norm_g, w_out=w_out,
             norm2_g=norm2_g, w_gate_up=w_gate_up, w_down=w_down)
    m = dict(norm1_g=m_norm1_g, w_in=m_w_in, sgu_norm_g=m_sgu_norm_g, w_spatial=m_w_spatial, b_spatial=m_b_spatial,
             conv_w=m_conv_w, a_log=m_a_log, dt_bias=m_dt_bias, o_norm_g=m_o_norm_g, q_norm_g=m_q_norm_g, k_norm_g=m_k_norm_g,
             w_out=m_w_out, norm2_g=m_norm2_g, w_gate_up=m_w_gate_up, w_down=m_w_down)
    v = dict(norm1_g=v_norm1_g, w_in=v_w_in, sgu_norm_g=v_sgu_norm_g, w_spatial=v_w_spatial, b_spatial=v_b_spatial,
             conv_w=v_conv_w, a_log=v_a_log, dt_bias=v_dt_bias, o_norm_g=v_o_norm_g, q_norm_g=v_q_norm_g, k_norm_g=v_k_norm_g,
             w_out=v_w_out, norm2_g=v_norm2_g, w_gate_up=v_w_gate_up, w_down=v_w_down)
    chip = (2 * lax.axis_index("x") + lax.axis_index("y")).astype(jnp.int32)
    core = lax.axis_index("c").astype(jnp.int32)

    in_pad = IN_SHARD_PAD - IN_SHARD
    w_in_pad = jnp.pad(w_in, ((0, 0), (0, 0), (0, in_pad)))

    halves_of = lambda a: a.reshape(2, a.shape[0] // 2, a.shape[1])
    start_0, wait_0 = gather_direct([halves_of(w_in_pad[0].astype(BF16)), halves_of(conv_w[0])], "mix0")
    state_0, token_0 = start_0()
    bf_halves = lambda a: halves_of((a + token_0[0, 0]).astype(BF16))

    def ffn_shards(l):
        return [bf_halves(w_gate_up[l]), bf_halves(w_down[l]), bf_halves(w_out[l])]

    def mixer_shards(l):
        return [bf_halves(w_in_pad[l]), halves_of(conv_w[l])]

    def mixer_weights(g):
        g_in, g_conv = g
        return (shards_to_segments(g_in.reshape(N_CHIPS, D, IN_SHARD_PAD))[None],
                g_conv.reshape(N_CHIPS, B_CONV, -1).transpose(1, 0, 2).reshape(B_CONV, 3 * B_WIDTH))

    def ffn_weights(g, w_in_seg):
        g_gu, g_down, g_out = g
        return {"in": w_in_seg, "out": g_out.reshape(1, D, D), "gu": g_gu.reshape(1, N_CHIPS, D, GU_SHARD),
                "down": None if g_down is None else g_down.reshape(1, FFN, D)}

    def layer_params(l, conv_full):
        return small_params(0, {n: w[n][l:l + 1] for n in SMALL_NAMES if n != "conv_w"}, conv_full[None])

    gu0, down0, out0 = ffn_shards(0)
    start_a, wait_a = gather_direct([gu0, out0], "ffn0")
    start_b, wait_b = gather_direct([down0] + mixer_shards(1), "mid")
    start_c, wait_c = gather_direct(ffn_shards(1), "ffn1")
    state_a, token_a = start_a(token_0)
    state_b, token_b = start_b(token_a)
    state_c, token_c = start_c(token_b)
    h1_0 = rmsnorm_fwd(x[0], norm1_g[0][None], token_c)
    w_in0, conv0 = mixer_weights(pass_to_sibling(wait_0(state_0, h1_0)[1]))
    sps = [layer_params(0, conv0), None]
    mix0, saved_m0 = mixers_forward(x[0], w_in0, sps[0], h1=h1_0)
    g_gu0, g_out0 = pass_to_sibling(wait_a(state_a, mix0)[1])
    wg0 = ffn_weights((g_gu0, None, g_out0), w_in0)
    x1_0, h2_0, gu_0, act_0 = ffn_up(x[0], mix0, wg0["out"], wg0["gu"], sps[0])
    g_down0, g_in1, g_conv1 = pass_to_sibling(wait_b(state_b, act_0)[1])
    wg0["down"] = g_down0.reshape(1, FFN, D)
    x1 = mm_down(act_0, wg0["down"], 0, x1_0)
    saved_f0 = (x1_0, h2_0, gu_0, act_0)
    w_in1, conv1 = mixer_weights((g_in1, g_conv1))
    sps[1] = layer_params(1, conv1)
    mix1, saved_m1 = mixers_forward(x1, w_in1, sps[1])
    wg1 = ffn_weights(pass_to_sibling(wait_c(state_c, mix1)[1]), w_in1)
    x2, saved_f1 = ffn_forward(x1, mix1, wg1, sps[1])
    saved1 = (saved_m1, saved_f1)
    dx, loss_tile = loss_and_grad(x2, loss_target[0])

    def sibling_exchange(tag, by_chip):
        cell = {}

        def on_weight_grads(*dws):
            start, wait = exchange_halves([f(t) for f, t in zip(by_chip, dws)], tag)
            state, token = start()
            cell["wait"] = functools.partial(wait, state)
            return token

        return on_weight_grads, cell

    def start_scatter(cell, after, tag, dep=None):
        grads, from_sibling = cell["wait"](after[0])
        parts = [add_sibling(g, r, core.reshape(1)) for g, r in zip(grads, from_sibling)]
        start, wait = scatter_direct(parts, tag)
        state, token = start(dep)
        return functools.partial(wait, state), token

    smalls = [None] * DEPTH
    same = lambda t: t
    ffn_by_chip = (same, lambda t: t.reshape(N_CHIPS, DOWN_SHARD, D))
    mix_by_chip = (same, lambda t: t.reshape(N_CHIPS, OUT_SHARD, D))
    hook, cell_f1 = sibling_exchange("ffn1", ffn_by_chip)
    dx1, dnorm2_1, _, _ = ffn_backward(dx, wg1, sps[1], saved1[1], on_weight_grads=hook)
    wait_f1, tok_f1 = start_scatter(cell_f1, dx1, "ffn1")
    hook, cell_m1 = sibling_exchange("mix1", mix_by_chip)
    dx, _, _, small1 = mixers_backward(dx1, wg1, sps[1], saved1[0], dep=tok_f1, on_weight_grads=hook)
    smalls[1] = {**small1, "norm2_g": dnorm2_1}
    wait_m1, tok_m1 = start_scatter(cell_m1, dx, "mix1")
    hook, cell_f0 = sibling_exchange("ffn0", ffn_by_chip)
    dx1, dnorm2_0, _, _ = ffn_backward(dx, wg0, sps[0], saved_f0, dep=tok_m1, on_weight_grads=hook)
    wait_f0, tok_f0 = start_scatter(cell_f0, dx1, "ffn0")
    hook, cell_m0 = sibling_exchange("mix0", mix_by_chip)
    dx, _, _, small0 = mixers_backward(dx1, wg0, sps[0], saved_m0, dep=tok_f0, on_weight_grads=hook)
    smalls[0] = {**small0, "norm2_g": dnorm2_0}

    grad, delta, new_m, new_v = {}, {}, {}, {}
    stacked = [jnp.stack([smalls[l][n] for l in range(DEPTH)]) for n in SMALL_NAMES]
    total = allreduce_small(_pack(stacked + [loss_tile[0, :1]]))
    shapes = [(DEPTH, B_CONV, 3 * B_WIDTH) if n == "conv_w" else w[n].shape for n in SMALL_NAMES]
    small_grads = dict(zip(SMALL_NAMES, _unpack(total, shapes + [(1,)])[:-1]))
    loss = _unpack(total, shapes + [(1,)])[-1][0]
    conv_cols = conv_w.shape[-1]
    small_grads["conv_w"] = lax.dynamic_slice_in_dim(small_grads["conv_w"], chip * conv_cols, conv_cols, axis=2)
    grad.update(small_grads)
    sshapes = [w[n].shape for n in SMALL_NAMES]
    packed = [_pack([d[n] for n in SMALL_NAMES]) for d in (w, grad, m, v)]
    for dst, t in zip((delta, new_m, new_v), adamw(*packed, "adamw_small")):
        dst.update(zip(SMALL_NAMES, _unpack(t, sshapes)))

    wait_m0, tok_m0 = start_scatter(cell_m0, dx, "mix0", dep=total)
    (pf0, rf0), (pm1, rm1), (pf1, rf1) = (wt(tok_m0) for wt in (wait_f0, wait_m1, wait_f1))

    def reduce_group(parts, from_chips):
        mine = [add_chips(p, r, chip.reshape(1)) for p, r in zip(parts, from_chips)]
        return mine, list(share_halves(mine))

    mine_f, theirs_f = reduce_group(pf0 + pf1 + pm1, rf0 + rf1 + rm1)
    for a, n in enumerate(("w_gate_up", "w_down")):
        grad[n], delta[n], new_m[n], new_v[n] = adamw_shard(w[n], m[n], v[n], [mine_f[a], mine_f[2 + a]],
                                                            [theirs_f[a], theirs_f[2 + a]], core.reshape(1), "adamw_" + n)

    mine_m0, theirs_m0 = reduce_group(*wait_m0(new_v["w_down"]))
    tr_ = lambda t: jnp.swapaxes(t, -1, -2)
    cut = lambda t: tr_(t[:, :IN_SHARD])
    res = adamw_shard_t(tr_(w_in), tr_(m_w_in), tr_(v_w_in), [cut(mine_m0[0]), cut(mine_f[4])],
                        [cut(theirs_m0[0]), cut(theirs_f[4])], core.reshape(1), "adamw_w_in")
    grad["w_in"], delta["w_in"], new_m["w_in"], new_v["w_in"] = (tr_(t) for t in res)
    grad["w_out"], delta["w_out"], new_m["w_out"], new_v["w_out"] = adamw_shard(
        w_out, m_w_out, v_w_out, [mine_m0[1], mine_f[5]], [theirs_m0[1], theirs_f[5]], core.reshape(1), "adamw_w_out")

    out = [loss, dx[0][None]]
    for d in (grad, delta, new_m, new_v):
        out += [d[n] for n in WEIGHT_ORDER]
    return tuple(out)
```
